```python
import math
import jax, jax.numpy as jnp
from jax import lax
import numpy as np

D_MODEL = 1024
BATCH = 16
SEQ = 4096
DEPTH = 2

HEAD_DIM = 64
BRANCH_WIDTH = D_MODEL
N_BRANCH = 3
SSM_INNER = BRANCH_WIDTH
SSM_HEAD_DIM = 64
SSM_HEADS = SSM_INNER // SSM_HEAD_DIM
SSM_GROUPS = 4
SSM_HEADS_PER_GROUP = SSM_HEADS // SSM_GROUPS
SSM_STATE = 128
SSM_CONV_DIM = SSM_INNER + 2 * SSM_GROUPS * SSM_STATE
CONV_WIDTH = 4
SSD_CHUNK = 128
SWA_HEADS = BRANCH_WIDTH // HEAD_DIM
SWA_KV_HEADS = 4
SWA_GROUP = SWA_HEADS // SWA_KV_HEADS
SWA_WINDOW = 128
SWA_BLOCK = 128
FOX_HEADS = BRANCH_WIDTH // HEAD_DIM
FOX_BLOCK = 128
ROPE_THETA = 10000.0
NORM_EPS = 1e-6
IN_SIZES = (SSM_CONV_DIM, SSM_INNER, SSM_HEADS,
            SWA_HEADS * HEAD_DIM, SWA_KV_HEADS * HEAD_DIM, SWA_KV_HEADS * HEAD_DIM, BRANCH_WIDTH,
            FOX_HEADS * HEAD_DIM, FOX_HEADS * HEAD_DIM, FOX_HEADS * HEAD_DIM, FOX_HEADS, BRANCH_WIDTH,
            N_BRANCH * D_MODEL)
N_IN = sum(IN_SIZES)

kernel_name = "hybrid_ssd_swa_fox_gated_block"


def rms_norm(x, w):
    xf = x.astype(jnp.float32)
    y = xf * lax.rsqrt(jnp.mean(xf * xf, axis=-1, keepdims=True) + NORM_EPS)
    return (y * w.astype(jnp.float32)).astype(x.dtype)


def grouped_rms_norm(y, w, groups):
    b, s, d = y.shape
    yf = y.astype(jnp.float32).reshape(b, s, groups, d // groups)
    yf = yf * lax.rsqrt(jnp.mean(yf * yf, axis=-1, keepdims=True) + NORM_EPS)
    return (yf.reshape(b, s, d) * w.astype(jnp.float32)).astype(y.dtype)


def rope(x, cos, sin):
    half = x.shape[-1] // 2
    x1, x2 = x[..., :half], x[..., half:]
    c, s = cos[None, :, None, :], sin[None, :, None, :]
    return jnp.concatenate([x1 * c - x2 * s, x2 * c + x1 * s], axis=-1)


def causal_depthwise_conv(u, w, bias):
    c = u.shape[-1]
    out = lax.conv_general_dilated(u, w[:, None, :], window_strides=(1,),
                                   padding=[(CONV_WIDTH - 1, 0)],
                                   dimension_numbers=('NWC', 'WIO', 'NWC'),
                                   feature_group_count=c)
    return out + bias


def ssd_chunked_scan(xh, dt, a, bm, cm):
    b, s, g, r, p = xh.shape
    n = bm.shape[-1]
    nc, l = s // SSD_CHUNK, SSD_CHUNK
    dtype = xh.dtype
    x = xh.reshape(b, nc, l, g, r, p)
    dtc = dt.reshape(b, nc, l, g, r)
    x_dt = x * dtc[..., None]
    bc = bm.reshape(b, nc, l, g, n)
    cc = cm.reshape(b, nc, l, g, n)
    a_dt = jnp.transpose(dtc.astype(jnp.float32) * a.astype(jnp.float32), (0, 3, 4, 1, 2))
    a_cum = jnp.cumsum(a_dt, axis=-1)
    idx = jnp.arange(l)
    causal = idx[:, None] >= idx[None, :]
    seg = a_cum[..., :, None] - a_cum[..., None, :]
    decay = jnp.where(causal, jnp.exp(jnp.where(causal, seg, 0.0)), 0.0).astype(dtype)
    cb = jnp.einsum('bclgn,bcsgn->bgcls', cc, bc)
    y_diag = jnp.einsum('bgrcls,bcsgrp->bclgrp', cb[:, :, None] * decay, x_dt)
    decay_states = jnp.exp(a_cum[..., -1:] - a_cum).astype(dtype)
    states = jnp.einsum('bcsgn,bgrcs,bcsgrp->bcgrpn', bc, decay_states, x_dt)
    chunk_decay = jnp.exp(a_cum[..., -1]).astype(dtype)

    def step(h, inp):
        st, dec = inp
        return h * dec[..., None, None] + st, h

    h0 = jnp.zeros((b, g, r, p, n), dtype)
    _, prev = lax.scan(step, h0, (jnp.moveaxis(states, 1, 0), jnp.moveaxis(chunk_decay, -1, 0)))
    prev = jnp.moveaxis(prev, 0, 1)
    y_off = jnp.einsum('bclgn,bcgrpn,bgrcl->bclgrp', cc, prev, jnp.exp(a_cum).astype(dtype))
    return (y_diag + y_off).reshape(b, s, g, r, p)


def mamba2_branch(xbc, z, dt_raw, conv_w, conv_b, dt_bias, a_log, d_skip, norm_w):
    b, s, _ = xbc.shape
    xbc = jax.nn.silu(causal_depthwise_conv(xbc, conv_w, conv_b))
    gn = SSM_GROUPS * SSM_STATE
    xs, bm, cm = jnp.split(xbc, [SSM_INNER, SSM_INNER + gn], axis=-1)
    xh = xs.reshape(b, s, SSM_GROUPS, SSM_HEADS_PER_GROUP, SSM_HEAD_DIM)
    dt = jax.nn.softplus(dt_raw + dt_bias).reshape(b, s, SSM_GROUPS, SSM_HEADS_PER_GROUP)
    a = -jnp.exp(a_log).reshape(SSM_GROUPS, SSM_HEADS_PER_GROUP)
    y = ssd_chunked_scan(xh, dt, a,
                         bm.reshape(b, s, SSM_GROUPS, SSM_STATE),
                         cm.reshape(b, s, SSM_GROUPS, SSM_STATE))
    y = y + d_skip.reshape(SSM_GROUPS, SSM_HEADS_PER_GROUP)[:, :, None] * xh
    y = y.reshape(b, s, SSM_INNER) * jax.nn.silu(z)
    return grouped_rms_norm(y, norm_w, SSM_GROUPS)


def sliding_window_branch(q, k, v, z, sinks, cos, sin):
    b, s, _ = q.shape
    nb, blk = s // SWA_BLOCK, SWA_BLOCK
    q = rope(q.reshape(b, s, SWA_HEADS, HEAD_DIM), cos, sin)
    k = rope(k.reshape(b, s, SWA_KV_HEADS, HEAD_DIM), cos, sin)
    v = v.reshape(b, s, SWA_KV_HEADS, HEAD_DIM)
    qb = q.reshape(b, nb, blk, SWA_KV_HEADS, SWA_GROUP, HEAD_DIM)
    pad = ((0, 0), (blk, 0), (0, 0), (0, 0))
    kp, vp = jnp.pad(k, pad)[:, :s], jnp.pad(v, pad)[:, :s]
    kb = jnp.concatenate([kp.reshape(b, nb, blk, SWA_KV_HEADS, HEAD_DIM),
                          k.reshape(b, nb, blk, SWA_KV_HEADS, HEAD_DIM)], axis=2)
    vb = jnp.concatenate([vp.reshape(b, nb, blk, SWA_KV_HEADS, HEAD_DIM),
                          v.reshape(b, nb, blk, SWA_KV_HEADS, HEAD_DIM)], axis=2)
    scores = jnp.einsum('bnqkgd,bnskd->bnkgqs', qb, kb).astype(jnp.float32) * (HEAD_DIM ** -0.5)
    qi = jnp.arange(blk)[:, None]
    sj = jnp.arange(2 * blk)[None, :]
    diff = qi + blk - sj
    band = (diff >= 0) & (diff < SWA_WINDOW)
    key_pos = jnp.arange(nb)[:, None, None] * blk - blk + sj[None]
    mask = band[None] & (key_pos >= 0)
    scores = jnp.where(mask[None, :, None, None], scores, -jnp.inf)
    sink = jnp.broadcast_to(sinks.astype(jnp.float32).reshape(1, 1, SWA_KV_HEADS, SWA_GROUP, 1, 1),
                            scores.shape[:-1] + (1,))
    probs = jax.nn.softmax(jnp.concatenate([scores, sink], axis=-1), axis=-1)[..., :-1]
    out = jnp.einsum('bnkgqs,bnskd->bnqkgd', probs.astype(v.dtype), vb)
    return out.reshape(b, s, SWA_HEADS * HEAD_DIM) * jax.nn.silu(z)


def forgetting_attention_branch(q, k, v, f_logit, z, f_bias):
    b, s, _ = q.shape
    q = q.reshape(b, s, FOX_HEADS, HEAD_DIM)
    k = k.reshape(b, s, FOX_HEADS, HEAD_DIM)
    v = v.reshape(b, s, FOX_HEADS, HEAD_DIM)
    log_f = jax.nn.log_sigmoid((f_logit + f_bias).astype(jnp.float32))
    cum = jnp.transpose(jnp.cumsum(log_f, axis=1), (0, 2, 1))
    scale = HEAD_DIM ** -0.5
    outs = []
    for i in range(s // FOX_BLOCK):
        start, end = i * FOX_BLOCK, (i + 1) * FOX_BLOCK
        sc = jnp.einsum('bqhd,bkhd->bhqk', q[:, start:end], k[:, :end]).astype(jnp.float32) * scale
        sc = sc + cum[:, :, start:end, None] - cum[:, :, None, :end]
        causal = (start + jnp.arange(FOX_BLOCK))[:, None] >= jnp.arange(end)[None, :]
        sc = jnp.where(causal[None, None], sc, -jnp.inf)
        probs = jax.nn.softmax(sc, axis=-1).astype(v.dtype)
        outs.append(jnp.einsum('bhqk,bkhd->bqhd', probs, v[:, :end]))
    out = jnp.concatenate(outs, axis=1)
    return out.reshape(b, s, FOX_HEADS * HEAD_DIM) * jax.nn.silu(z)


def _fwd_setup_inputs(seed: int = 0) -> dict:
    key = jax.random.key(seed)
    ks = jax.random.split(key, 16)
    L, D, W = DEPTH, D_MODEL, BRANCH_WIDTH
    x = jax.random.normal(ks[0], (BATCH, SEQ, D), jnp.float32)
    norm_w = 1.0 + 0.1 * jax.random.normal(ks[1], (L, D), jnp.float32)
    w_in = jax.random.normal(ks[2], (L, D, N_IN), jnp.float32) * D ** -0.5
    conv_w = jax.random.normal(ks[3], (L, CONV_WIDTH, SSM_CONV_DIM), jnp.float32) * CONV_WIDTH ** -0.5
    conv_b = 0.01 * jax.random.normal(ks[4], (L, SSM_CONV_DIM), jnp.float32)
    u = jax.random.uniform(ks[5], (L, SSM_HEADS), jnp.float32)
    dt0 = jnp.exp(u * (math.log(0.1) - math.log(0.001)) + math.log(0.001))
    dt_bias = dt0 + jnp.log(-jnp.expm1(-dt0))
    a_log = jnp.log(jax.random.uniform(ks[6], (L, SSM_HEADS), jnp.float32, minval=1.0, maxval=16.0))
    d_skip = 1.0 + 0.1 * jax.random.normal(ks[7], (L, SSM_HEADS), jnp.float32)
    ssm_norm_w = 1.0 + 0.1 * jax.random.normal(ks[8], (L, SSM_INNER), jnp.float32)
    sinks = 0.5 * jax.random.normal(ks[9], (L, SWA_HEADS), jnp.float32)
    f_bias = 3.0 + 0.5 * jax.random.normal(ks[10], (L, FOX_HEADS), jnp.float32)
    gate_bias = 0.1 * jax.random.normal(ks[11], (L, N_BRANCH, D), jnp.float32)
    w_proj = jax.random.normal(ks[12], (L, N_BRANCH, W, D), jnp.float32) * W ** -0.5
    w_out = jax.random.normal(ks[13], (L, D, D), jnp.float32) * D ** -0.5
    final_norm_w = 1.0 + 0.1 * jax.random.normal(ks[14], (D,), jnp.float32)
    return {"x": x, "norm_w": norm_w, "w_in": w_in, "conv_w": conv_w, "conv_b": conv_b,
            "dt_bias": dt_bias, "a_log": a_log, "d_skip": d_skip, "ssm_norm_w": ssm_norm_w,
            "sinks": sinks, "f_bias": f_bias, "gate_bias": gate_bias, "w_proj": w_proj,
            "w_out": w_out, "final_norm_w": final_norm_w}


def _fwd_reference(x, norm_w, w_in, conv_w, conv_b, dt_bias, a_log, d_skip, ssm_norm_w,
              sinks, f_bias, gate_bias, w_proj, w_out, final_norm_w):
    b, s, d = x.shape
    pos = jnp.arange(s, dtype=jnp.float32)
    inv_freq = ROPE_THETA ** (-jnp.arange(0, HEAD_DIM, 2, dtype=jnp.float32) / HEAD_DIM)
    ang = pos[:, None] * inv_freq[None, :]
    cos, sin = jnp.cos(ang).astype(x.dtype), jnp.sin(ang).astype(x.dtype)
    split_at = [int(v) for v in np.cumsum(IN_SIZES)[:-1]]
    for layer in range(DEPTH):
        h = rms_norm(x, norm_w[layer])
        proj = jnp.einsum('bsd,de->bse', h, w_in[layer])
        (a_xbc, a_z, a_dt, b_q, b_k, b_v, b_z,
         c_q, c_k, c_v, c_f, c_z, gates) = jnp.split(proj, split_at, axis=-1)
        y_a = mamba2_branch(a_xbc, a_z, a_dt, conv_w[layer], conv_b[layer], dt_bias[layer],
                            a_log[layer], d_skip[layer], ssm_norm_w[layer])
        y_b = sliding_window_branch(b_q, b_k, b_v, b_z, sinks[layer], cos, sin)
        y_c = forgetting_attention_branch(c_q, c_k, c_v, c_f, c_z, f_bias[layer])
        ys = jnp.stack([y_a, y_b, y_c], axis=2)
        branch = jnp.einsum('bsiw,iwd->bsid', ys, w_proj[layer])
        g = jax.nn.sigmoid(gates.reshape(b, s, N_BRANCH, d) + gate_bias[layer])
        merged = jnp.sum(g * branch, axis=2)
        x = x + jnp.einsum('bsd,de->bse', merged, w_out[layer])
    return rms_norm(x, final_norm_w)


import jax as _jax
import jax.numpy as _jnp

TWIN_FORMAT = 'train_step'
FWD_PARAMS = ['x', 'norm_w', 'w_in', 'conv_w', 'conv_b', 'dt_bias', 'a_log', 'd_skip', 'ssm_norm_w', 'sinks', 'f_bias', 'gate_bias', 'w_proj', 'w_out', 'final_norm_w']
TWIN_WEIGHTS = ['norm_w', 'w_in', 'conv_w', 'conv_b', 'dt_bias', 'a_log', 'd_skip', 'ssm_norm_w', 'sinks', 'f_bias', 'gate_bias', 'w_proj', 'w_out', 'final_norm_w']
TWIN_DIFF_INPUT = 'x'
TWIN_INPUTS = ['x', 'norm_w', 'w_in', 'conv_w', 'conv_b', 'dt_bias', 'a_log', 'd_skip', 'ssm_norm_w', 'sinks', 'f_bias', 'gate_bias', 'w_proj', 'w_out', 'final_norm_w', 'loss_target', 'm_norm_w', 'm_w_in', 'm_conv_w', 'm_conv_b', 'm_dt_bias', 'm_a_log', 'm_d_skip', 'm_ssm_norm_w', 'm_sinks', 'm_f_bias', 'm_gate_bias', 'm_w_proj', 'm_w_out', 'm_final_norm_w', 'v_norm_w', 'v_w_in', 'v_conv_w', 'v_conv_b', 'v_dt_bias', 'v_a_log', 'v_d_skip', 'v_ssm_norm_w', 'v_sinks', 'v_f_bias', 'v_gate_bias', 'v_w_proj', 'v_w_out', 'v_final_norm_w']
TWIN_OUTPUTS = ['loss', 'grad_x', 'grad_norm_w', 'grad_w_in', 'grad_conv_w', 'grad_conv_b', 'grad_dt_bias', 'grad_a_log', 'grad_d_skip', 'grad_ssm_norm_w', 'grad_sinks', 'grad_f_bias', 'grad_gate_bias', 'grad_w_proj', 'grad_w_out', 'grad_final_norm_w', 'delta_norm_w', 'delta_w_in', 'delta_conv_w', 'delta_conv_b', 'delta_dt_bias', 'delta_a_log', 'delta_d_skip', 'delta_ssm_norm_w', 'delta_sinks', 'delta_f_bias', 'delta_gate_bias', 'delta_w_proj', 'delta_w_out', 'delta_final_norm_w', 'new_m_norm_w', 'new_m_w_in', 'new_m_conv_w', 'new_m_conv_b', 'new_m_dt_bias', 'new_m_a_log', 'new_m_d_skip', 'new_m_ssm_norm_w', 'new_m_sinks', 'new_m_f_bias', 'new_m_gate_bias', 'new_m_w_proj', 'new_m_w_out', 'new_m_final_norm_w', 'new_v_norm_w', 'new_v_w_in', 'new_v_conv_w', 'new_v_conv_b', 'new_v_dt_bias', 'new_v_a_log', 'new_v_d_skip', 'new_v_ssm_norm_w', 'new_v_sinks', 'new_v_f_bias', 'new_v_gate_bias', 'new_v_w_proj', 'new_v_w_out', 'new_v_final_norm_w']
TWIN_LEAF_KINDS = {'loss': 'loss', 'grad_x': 'grad_x', 'grad_norm_w': 'grad_w', 'grad_w_in': 'grad_w', 'grad_conv_w': 'grad_w', 'grad_conv_b': 'grad_w', 'grad_dt_bias': 'grad_w', 'grad_a_log': 'grad_w', 'grad_d_skip': 'grad_w', 'grad_ssm_norm_w': 'grad_w', 'grad_sinks': 'grad_w', 'grad_f_bias': 'grad_w', 'grad_gate_bias': 'grad_w', 'grad_w_proj': 'grad_w', 'grad_w_out': 'grad_w', 'grad_final_norm_w': 'grad_w', 'delta_norm_w': 'delta_w', 'delta_w_in': 'delta_w', 'delta_conv_w': 'delta_w', 'delta_conv_b': 'delta_w', 'delta_dt_bias': 'delta_w', 'delta_a_log': 'delta_w', 'delta_d_skip': 'delta_w', 'delta_ssm_norm_w': 'delta_w', 'delta_sinks': 'delta_w', 'delta_f_bias': 'delta_w', 'delta_gate_bias': 'delta_w', 'delta_w_proj': 'delta_w', 'delta_w_out': 'delta_w', 'delta_final_norm_w': 'delta_w', 'new_m_norm_w': 'new_m', 'new_m_w_in': 'new_m', 'new_m_conv_w': 'new_m', 'new_m_conv_b': 'new_m', 'new_m_dt_bias': 'new_m', 'new_m_a_log': 'new_m', 'new_m_d_skip': 'new_m', 'new_m_ssm_norm_w': 'new_m', 'new_m_sinks': 'new_m', 'new_m_f_bias': 'new_m', 'new_m_gate_bias': 'new_m', 'new_m_w_proj': 'new_m', 'new_m_w_out': 'new_m', 'new_m_final_norm_w': 'new_m', 'new_v_norm_w': 'new_v', 'new_v_w_in': 'new_v', 'new_v_conv_w': 'new_v', 'new_v_conv_b': 'new_v', 'new_v_dt_bias': 'new_v', 'new_v_a_log': 'new_v', 'new_v_d_skip': 'new_v', 'new_v_ssm_norm_w': 'new_v', 'new_v_sinks': 'new_v', 'new_v_f_bias': 'new_v', 'new_v_gate_bias': 'new_v', 'new_v_w_proj': 'new_v', 'new_v_w_out': 'new_v', 'new_v_final_norm_w': 'new_v'}


def _forward(args):
    return _fwd_reference(*[args[k] for k in FWD_PARAMS])


def _output_shape():
    out = _jax.eval_shape(lambda: _forward(_fwd_setup_inputs(0)))
    return out.shape, out.dtype

N_MICROBATCH = 1
ADAM_LR = 0.001
ADAM_B1 = 0.9
ADAM_B2 = 0.999
ADAM_EPS = 1e-08
ADAM_WD = 0.01
ADAM_STEP = 10
PER_EXAMPLE_BATCH_AXIS = {'x': 0, 'loss_target': 0}
SHARED_INPUTS = []
_WEIGHT_DTYPES = {'norm_w': _jnp.float32, 'w_in': _jnp.float32, 'conv_w': _jnp.float32, 'conv_b': _jnp.float32, 'dt_bias': _jnp.float32, 'a_log': _jnp.float32, 'd_skip': _jnp.float32, 'ssm_norm_w': _jnp.float32, 'sinks': _jnp.float32, 'f_bias': _jnp.float32, 'gate_bias': _jnp.float32, 'w_proj': _jnp.float32, 'w_out': _jnp.float32, 'final_norm_w': _jnp.float32}
MOMENT_SCALE = {'norm_w': 2.015450e-01, 'w_in': 5.764424e-02, 'conv_w': 9.673484e-02, 'conv_b': 1.365550e-01, 'dt_bias': 2.912411e-01, 'a_log': 5.409992e-01, 'd_skip': 8.556686e-01, 'ssm_norm_w': 1.308874e-01, 'sinks': 1.331509e-02, 'f_bias': 1.056287e-01, 'gate_bias': 3.395263e-02, 'w_proj': 8.104556e-02, 'w_out': 1.444398e-01, 'final_norm_w': 6.442581e+01}


def _to_microbatches(a, axis):
    t = _jnp.moveaxis(a, axis, 0)
    t = t.reshape((N_MICROBATCH, t.shape[0] // N_MICROBATCH) + t.shape[1:])
    return _jnp.moveaxis(t, 1, axis + 1)


def setup_inputs(seed: int = 0) -> dict:
    inp = _fwd_setup_inputs(seed)
    key = _jax.random.fold_in(_jax.random.key(seed), 7919)
    shape, _ = _output_shape()
    out = dict(inp)
    out["loss_target"] = _jax.random.normal(_jax.random.fold_in(key, 0), shape, _jnp.float32)
    for i, name in enumerate(TWIN_WEIGHTS):
        w = inp[name].astype(_jnp.float32)
        if MOMENT_SCALE is None:
            s = _jnp.sqrt(_jnp.mean(_jnp.square(w)) + 1e-30)
        else:
            s = MOMENT_SCALE[name]
        km, kv = _jax.random.split(_jax.random.fold_in(key, i + 1))
        out[name] = w
        out["m_" + name] = s * _jax.random.normal(km, w.shape, _jnp.float32)
        out["v_" + name] = (s * s) * _jax.random.uniform(kv, w.shape, _jnp.float32, 0.5, 1.5)
    if N_MICROBATCH > 1:
        for name, axis in PER_EXAMPLE_BATCH_AXIS.items():
            out[name] = _to_microbatches(out[name], axis)
    return {'x': out['x'], 'norm_w': out['norm_w'], 'w_in': out['w_in'], 'conv_w': out['conv_w'], 'conv_b': out['conv_b'], 'dt_bias': out['dt_bias'], 'a_log': out['a_log'], 'd_skip': out['d_skip'], 'ssm_norm_w': out['ssm_norm_w'], 'sinks': out['sinks'], 'f_bias': out['f_bias'], 'gate_bias': out['gate_bias'], 'w_proj': out['w_proj'], 'w_out': out['w_out'], 'final_norm_w': out['final_norm_w'], 'loss_target': out['loss_target'], 'm_norm_w': out['m_norm_w'], 'm_w_in': out['m_w_in'], 'm_conv_w': out['m_conv_w'], 'm_conv_b': out['m_conv_b'], 'm_dt_bias': out['m_dt_bias'], 'm_a_log': out['m_a_log'], 'm_d_skip': out['m_d_skip'], 'm_ssm_norm_w': out['m_ssm_norm_w'], 'm_sinks': out['m_sinks'], 'm_f_bias': out['m_f_bias'], 'm_gate_bias': out['m_gate_bias'], 'm_w_proj': out['m_w_proj'], 'm_w_out': out['m_w_out'], 'm_final_norm_w': out['m_final_norm_w'], 'v_norm_w': out['v_norm_w'], 'v_w_in': out['v_w_in'], 'v_conv_w': out['v_conv_w'], 'v_conv_b': out['v_conv_b'], 'v_dt_bias': out['v_dt_bias'], 'v_a_log': out['v_a_log'], 'v_d_skip': out['v_d_skip'], 'v_ssm_norm_w': out['v_ssm_norm_w'], 'v_sinks': out['v_sinks'], 'v_f_bias': out['v_f_bias'], 'v_gate_bias': out['v_gate_bias'], 'v_w_proj': out['v_w_proj'], 'v_w_out': out['v_w_out'], 'v_final_norm_w': out['v_final_norm_w']}


def _loss(weights, diff, rest, loss_target):
    with _jax.named_scope("forward"):
        args = {**rest, TWIN_DIFF_INPUT: diff, **{k: w.astype(_WEIGHT_DTYPES[k]) for k, w in weights.items()}}
        y = _forward(args)
    with _jax.named_scope("loss_head"):
        err = _jnp.square(y.astype(_jnp.float32) - loss_target)
        return 0.5 * _jnp.sum(_jnp.mean(err, axis=-1)) if err.ndim else 0.5 * err


def _adamw(w, g, m, v):
    m = ADAM_B1 * m + (1.0 - ADAM_B1) * g
    v = ADAM_B2 * v + (1.0 - ADAM_B2) * _jnp.square(g)
    m_hat = m / (1.0 - ADAM_B1 ** ADAM_STEP)
    v_hat = v / (1.0 - ADAM_B2 ** ADAM_STEP)
    delta = -ADAM_LR * (m_hat / (_jnp.sqrt(v_hat) + ADAM_EPS) + ADAM_WD * w)
    return delta, m, v


def reference(x, norm_w, w_in, conv_w, conv_b, dt_bias, a_log, d_skip, ssm_norm_w, sinks, f_bias, gate_bias, w_proj, w_out, final_norm_w, loss_target, m_norm_w, m_w_in, m_conv_w, m_conv_b, m_dt_bias, m_a_log, m_d_skip, m_ssm_norm_w, m_sinks, m_f_bias, m_gate_bias, m_w_proj, m_w_out, m_final_norm_w, v_norm_w, v_w_in, v_conv_w, v_conv_b, v_dt_bias, v_a_log, v_d_skip, v_ssm_norm_w, v_sinks, v_f_bias, v_gate_bias, v_w_proj, v_w_out, v_final_norm_w):
    given = dict(x=x, norm_w=norm_w, w_in=w_in, conv_w=conv_w, conv_b=conv_b, dt_bias=dt_bias, a_log=a_log, d_skip=d_skip, ssm_norm_w=ssm_norm_w, sinks=sinks, f_bias=f_bias, gate_bias=gate_bias, w_proj=w_proj, w_out=w_out, final_norm_w=final_norm_w, loss_target=loss_target, m_norm_w=m_norm_w, m_w_in=m_w_in, m_conv_w=m_conv_w, m_conv_b=m_conv_b, m_dt_bias=m_dt_bias, m_a_log=m_a_log, m_d_skip=m_d_skip, m_ssm_norm_w=m_ssm_norm_w, m_sinks=m_sinks, m_f_bias=m_f_bias, m_gate_bias=m_gate_bias, m_w_proj=m_w_proj, m_w_out=m_w_out, m_final_norm_w=m_final_norm_w, v_norm_w=v_norm_w, v_w_in=v_w_in, v_conv_w=v_conv_w, v_conv_b=v_conv_b, v_dt_bias=v_dt_bias, v_a_log=v_a_log, v_d_skip=v_d_skip, v_ssm_norm_w=v_ssm_norm_w, v_sinks=v_sinks, v_f_bias=v_f_bias, v_gate_bias=v_gate_bias, v_w_proj=v_w_proj, v_w_out=v_w_out, v_final_norm_w=v_final_norm_w)
    weights = {n: given[n] for n in TWIN_WEIGHTS}
    shared = {n: given[n] for n in SHARED_INPUTS}
    per_example = {n: given[n] for n in ['x']}
    grad_fn = _jax.value_and_grad(_loss, argnums=(0, 1))

    def one_microbatch(ex, loss_target):
        ex = dict(ex)
        diff = ex.pop(TWIN_DIFF_INPUT)
        return grad_fn(weights, diff, {**shared, **ex}, loss_target)

    if N_MICROBATCH == 1:
        loss, (grad_w, grad_x) = one_microbatch(per_example, given["loss_target"])
    else:
        def body(carry, xs):
            loss_sum, grad_sum = carry
            l_k, (gw_k, gx_k) = one_microbatch(xs[0], xs[1])
            with _jax.named_scope("update"):
                return (loss_sum + l_k, _jax.tree.map(_jnp.add, grad_sum, gw_k)), gx_k

        init = (_jnp.zeros((), _jnp.float32), _jax.tree.map(_jnp.zeros_like, weights))
        (loss, grad_w), grad_x = _jax.lax.scan(body, init, (per_example, given["loss_target"]))
    with _jax.named_scope("update"):
        delta_w, new_m, new_v = {}, {}, {}
        for n in TWIN_WEIGHTS:
            delta_w[n], new_m[n], new_v[n] = _adamw(weights[n], grad_w[n], given["m_" + n], given["v_" + n])
    return (loss, grad_x, *[grad_w[n] for n in TWIN_WEIGHTS], *[delta_w[n] for n in TWIN_WEIGHTS],
            *[new_m[n] for n in TWIN_WEIGHTS], *[new_v[n] for n in TWIN_WEIGHTS])
```

```python
import functools
import math

import jax
import jax.numpy as jnp
from jax import lax
from jax.experimental import pallas as pl
from jax.experimental.pallas import tpu as pltpu

F32 = jnp.float32
BF16 = jnp.bfloat16
MESH = pl.DeviceIdType.MESH
NDEV = 8

D = 1024
NH = 16
HD = 64
NST = 128
NGRP = 4
LCH = 128
EPS = 1e-6
ROPE_THETA = 10000.0
SCALE = HD ** -0.5
NEG = -1e30

LANES = 128
VMEM_LIMIT = 56 * 1024 * 1024

OFF_XBC, OFF_AZ, OFF_BQ, OFF_BZ, OFF_CQ, OFF_CK, OFF_CV, OFF_CZ, OFF_G, OFF_BK, OFF_BV = (
    0, 2048, 3072, 4096, 5120, 6144, 7168, 8192, 9216, 12288, 12544)
NMAIN = 12800
NIN = 12832
NSH = NIN // NDEV

ADAM_LR, ADAM_B1, ADAM_B2, ADAM_EPS, ADAM_WD, ADAM_STEP = 0.001, 0.9, 0.999, 1e-08, 0.01, 10


def _cparams(dims=None, vmem=None):
    return pltpu.CompilerParams(dimension_semantics=dims, vmem_limit_bytes=vmem)


def _dot(a, b):
    return jnp.dot(a, b, preferred_element_type=F32)


def _dot_nt(a, b):
    return lax.dot_general(a, b, (((1,), (1,)), ((), ())), preferred_element_type=F32)


def _dot_tn(a, b):
    return lax.dot_general(a, b, (((0,), (0,)), ((), ())), preferred_element_type=F32)


def _dot_hi(a, b):
    return jnp.dot(a, b, precision=lax.Precision.HIGHEST, preferred_element_type=F32)


def _sigmoid(x):
    return 1.0 / (1.0 + jnp.exp(-x))


def _softplus(x):
    return jnp.maximum(x, 0.0) + jnp.log(1.0 + jnp.exp(-jnp.abs(x)))


def _lane_iota(n=LANES):
    return lax.broadcasted_iota(jnp.int32, (1, n), 1)


def _rot_half(x):
    first = (_lane_iota() % HD) < (HD // 2)
    return jnp.where(first, pltpu.roll(x, LANES - HD // 2, 1), pltpu.roll(x, HD // 2, 1))


def _head_sum(x, head):
    m = (_lane_iota() < HD) if head == 0 else (_lane_iota() >= HD)
    return jnp.sum(jnp.where(m, x, 0.0), axis=1, keepdims=True)


def _me_and_peers():
    x, y, c = lax.axis_index("x"), lax.axis_index("y"), lax.axis_index("c")
    me = 4 * x + 2 * y + c
    peers = []
    for k in range(1, NDEV):
        kx, ky, kc = (k >> 2) & 1, (k >> 1) & 1, k & 1
        px, py, pc = x ^ kx, y ^ ky, c ^ kc
        peers.append(((px, py, pc), 4 * px + 2 * py + pc))
    return me, peers


def _all_gather(arrays, name):
    n = len(arrays)

    def body(*refs):
        ins, outs = refs[:n], refs[n:2 * n]
        send_sems, recv_sems, local_sems = refs[2 * n:]
        me, peers = _me_and_peers()
        copies = []
        for a in range(n):
            loc = pltpu.make_async_copy(ins[a], outs[a].at[me], local_sems.at[a])
            loc.start()
            copies.append(loc)
            for k, (peer, _) in enumerate(peers):
                cp = pltpu.make_async_remote_copy(
                    src_ref=ins[a], dst_ref=outs[a].at[me],
                    send_sem=send_sems.at[a, k], recv_sem=recv_sems.at[a, k],
                    device_id=peer, device_id_type=MESH)
                cp.start()
                copies.append(cp)
        for cp in copies:
            cp.wait()

    any_spec = pl.BlockSpec(memory_space=pl.ANY)
    return pl.pallas_call(
        body, name=name,
        out_shape=[jax.ShapeDtypeStruct((NDEV,) + a.shape, a.dtype) for a in arrays],
        in_specs=[any_spec] * n, out_specs=[any_spec] * n,
        scratch_shapes=[pltpu.SemaphoreType.DMA((n, NDEV - 1)), pltpu.SemaphoreType.DMA((n, NDEV - 1)),
                        pltpu.SemaphoreType.DMA((n,))],
    )(*arrays)


def _scatter_blocks(arrays, name):
    n = len(arrays)

    def body(*refs):
        ins, outs = refs[:n], refs[n:2 * n]
        send_sems, recv_sems, local_sems = refs[2 * n:]
        me, peers = _me_and_peers()
        copies = []
        for a in range(n):
            loc = pltpu.make_async_copy(ins[a].at[me], outs[a].at[me], local_sems.at[a])
            loc.start()
            copies.append(loc)
            for k, (peer, pidx) in enumerate(peers):
                cp = pltpu.make_async_remote_copy(
                    src_ref=ins[a].at[pidx], dst_ref=outs[a].at[me],
                    send_sem=send_sems.at[a, k], recv_sem=recv_sems.at[a, k],
                    device_id=peer, device_id_type=MESH)
                cp.start()
                copies.append(cp)
        for cp in copies:
            cp.wait()

    any_spec = pl.BlockSpec(memory_space=pl.ANY)
    return pl.pallas_call(
        body, name=name,
        out_shape=[jax.ShapeDtypeStruct(a.shape, a.dtype) for a in arrays],
        in_specs=[any_spec] * n, out_specs=[any_spec] * n,
        scratch_shapes=[pltpu.SemaphoreType.DMA((n, NDEV - 1)), pltpu.SemaphoreType.DMA((n, NDEV - 1)),
                        pltpu.SemaphoreType.DMA((n,))],
    )(*arrays)


def _all_reduce_small(v):
    rows = v.shape[0]

    def body(v_ref, sum_ref, all_ref, send_sems, recv_sems):
        me, peers = _me_and_peers()
        all_ref[me] = v_ref[...]
        copies = []
        for k, (peer, _) in enumerate(peers):
            cp = pltpu.make_async_remote_copy(
                src_ref=v_ref, dst_ref=all_ref.at[me],
                send_sem=send_sems.at[k], recv_sem=recv_sems.at[k],
                device_id=peer, device_id_type=MESH)
            cp.start()
            copies.append(cp)
        for cp in copies:
            cp.wait()
        acc = all_ref[0]
        for d in range(1, NDEV):
            acc = acc + all_ref[d]
        sum_ref[...] = acc

    vm = pl.BlockSpec(memory_space=pltpu.VMEM)
    return pl.pallas_call(
        body, name="all_reduce_small",
        out_shape=jax.ShapeDtypeStruct((rows, LANES), F32),
        in_specs=[vm], out_specs=vm,
        scratch_shapes=[pltpu.VMEM((NDEV, rows, LANES), F32),
                        pltpu.SemaphoreType.DMA((NDEV - 1,)), pltpu.SemaphoreType.DMA((NDEV - 1,))],
    )(v)


def _adamw_math(w, g, m, v):
    m = ADAM_B1 * m + (1.0 - ADAM_B1) * g
    v = ADAM_B2 * v + (1.0 - ADAM_B2) * jnp.square(g)
    m_hat = m / (1.0 - ADAM_B1 ** ADAM_STEP)
    v_hat = v / (1.0 - ADAM_B2 ** ADAM_STEP)
    delta = -ADAM_LR * (m_hat / (jnp.sqrt(v_hat) + ADAM_EPS) + ADAM_WD * w)
    return delta, m, v


def _sum_adamw(parts, w, m, v, name):
    rows, cols = w.shape
    tr = rows
    for cand in (256, 128, 64, 32, 16, 8):
        if rows % cand == 0:
            tr = cand
            break

    def body(p_ref, w_ref, m_ref, v_ref, g_ref, d_ref, nm_ref, nv_ref):
        g = p_ref[0]
        for d in range(1, NDEV):
            g = g + p_ref[d]
        delta, nm, nv = _adamw_math(w_ref[...], g, m_ref[...], v_ref[...])
        g_ref[...] = g
        d_ref[...] = delta
        nm_ref[...] = nm
        nv_ref[...] = nv

    blk = pl.BlockSpec((tr, cols), lambda i: (i, 0))
    sds = jax.ShapeDtypeStruct((rows, cols), F32)
    return pl.pallas_call(
        body, name=name, grid=(rows // tr,),
        in_specs=[pl.BlockSpec((NDEV, tr, cols), lambda i: (0, i, 0)), blk, blk, blk],
        out_specs=[blk, blk, blk, blk], out_shape=[sds, sds, sds, sds],
        compiler_params=_cparams(("parallel",), VMEM_LIMIT),
    )(parts, w, m, v)


def _adamw_small(g, w, m, v):
    def body(g_ref, w_ref, m_ref, v_ref, d_ref, nm_ref, nv_ref):
        delta, nm, nv = _adamw_math(w_ref[...], g_ref[...], m_ref[...], v_ref[...])
        d_ref[...] = delta
        nm_ref[...] = nm
        nv_ref[...] = nv

    sds = jax.ShapeDtypeStruct(g.shape, F32)
    return pl.pallas_call(body, name="adamw_small", out_shape=[sds, sds, sds])(g, w, m, v)


def _matmul(a, b, out_dtype, name, tm=1024, tn=1024, tk=512):
    M, K = a.shape
    N = b.shape[1]
    tm, tn, tk = min(tm, M), min(tn, N), min(tk, K)
    nk = K // tk

    def body(a_ref, b_ref, o_ref, acc):
        k = pl.program_id(2)

        @pl.when(k == 0)
        def _():
            acc[...] = jnp.zeros_like(acc)

        acc[...] += _dot(a_ref[...], b_ref[...])

        @pl.when(k == nk - 1)
        def _():
            o_ref[...] = acc[...].astype(out_dtype)

    return pl.pallas_call(
        body, name=name, grid=(M // tm, N // tn, nk),
        in_specs=[pl.BlockSpec((tm, tk), lambda i, j, k: (i, k)), pl.BlockSpec((tk, tn), lambda i, j, k: (k, j))],
        out_specs=pl.BlockSpec((tm, tn), lambda i, j, k: (i, j)),
        out_shape=jax.ShapeDtypeStruct((M, N), out_dtype),
        scratch_shapes=[pltpu.VMEM((tm, tn), F32)],
        compiler_params=_cparams(("parallel", "parallel", "arbitrary"), VMEM_LIMIT),
    )(a, b)


def _inproj_fwd(x2, nw, wmain, wsmall, cos128, sin128, S, li):
    T = x2.shape[0]
    tm, tn = min(1024, S), 512
    nj, npos = NMAIN // tn, S // tm
    jq0, jk = OFF_BQ // tn, OFF_BK // tn

    def body(x_ref, nw_ref, w_ref, ws_ref, cos_ref, sin_ref, proj_ref, ps_ref, h_ref, h_scr):
        j = pl.program_id(1)

        @pl.when(j == 0)
        def _():
            x = x_ref[...]
            r = lax.rsqrt(jnp.mean(x * x, axis=-1, keepdims=True) + EPS)
            h = (x * r * nw_ref[...]).astype(BF16)
            h_scr[...] = h
            h_ref[...] = h
            ps_ref[...] = _dot(h, ws_ref[...])

        acc = _dot(h_scr[...], w_ref[...])

        def roped(c):
            xc = acc[:, LANES * c:LANES * (c + 1)]
            return (xc * cos_ref[...] + _rot_half(xc) * sin_ref[...]).astype(BF16)

        def plain(c):
            return acc[:, LANES * c:LANES * (c + 1)].astype(BF16)

        is_q = jnp.logical_or(j == jq0, j == jq0 + 1)
        is_k = j == jk

        @pl.when(is_q)
        def _():
            for c in range(4):
                proj_ref[:, LANES * c:LANES * (c + 1)] = roped(c)

        @pl.when(is_k)
        def _():
            for c in range(4):
                proj_ref[:, LANES * c:LANES * (c + 1)] = roped(c) if c < 2 else plain(c)

        @pl.when(jnp.logical_not(jnp.logical_or(is_q, is_k)))
        def _():
            proj_ref[...] = acc.astype(BF16)

    return pl.pallas_call(
        body, name=f"inproj_fwd_{li}", grid=(T // tm, nj),
        in_specs=[pl.BlockSpec((tm, D), lambda i, j: (i, 0)),
                  pl.BlockSpec((1, D), lambda i, j: (0, 0)),
                  pl.BlockSpec((D, tn), lambda i, j: (0, j)),
                  pl.BlockSpec((D, LANES), lambda i, j: (0, 0)),
                  pl.BlockSpec((tm, LANES), lambda i, j: (i % npos, 0)),
                  pl.BlockSpec((tm, LANES), lambda i, j: (i % npos, 0))],
        out_specs=[pl.BlockSpec((tm, tn), lambda i, j: (i, j)),
                   pl.BlockSpec((tm, LANES), lambda i, j: (i, 0)),
                   pl.BlockSpec((tm, D), lambda i, j: (i, 0))],
        out_shape=[jax.ShapeDtypeStruct((T, NMAIN), BF16), jax.ShapeDtypeStruct((T, LANES), F32),
                   jax.ShapeDtypeStruct((T, D), BF16)],
        scratch_shapes=[pltpu.VMEM((tm, D), BF16)],
        compiler_params=_cparams(("parallel", "arbitrary"), VMEM_LIMIT),
    )(x2, nw, wmain, wsmall, cos128, sin128)


def _inproj_bwd_dx(dproj, wmain_t, dps16, wsmall_t, x2, nw, dxo, li):
    T = x2.shape[0]
    tm, tk = min(1024, T), 512
    nk = NMAIN // tk
    ni = T // tm

    def body(dp_ref, wt_ref, ds_ref, wst_ref, x_ref, nw_ref, dxo_ref, dx_ref, dx16_ref, dnw_ref, acc):
        i, k = pl.program_id(0), pl.program_id(1)

        @pl.when(k == 0)
        def _():
            acc[...] = _dot(ds_ref[...], wst_ref[...])

        acc[...] += _dot(dp_ref[...], wt_ref[...])

        @pl.when(jnp.logical_and(i == 0, k == 0))
        def _():
            dnw_ref[...] = jnp.zeros_like(dnw_ref)

        @pl.when(k == nk - 1)
        def _():
            x = x_ref[...]
            r = lax.rsqrt(jnp.mean(x * x, axis=-1, keepdims=True) + EPS)
            dh = acc[...]
            g = dh * nw_ref[...]
            dx = dxo_ref[...] + r * g - x * (r * r * r) * jnp.mean(g * x, axis=-1, keepdims=True)
            dx_ref[...] = dx
            dx16_ref[...] = dx.astype(BF16)
            dnw_ref[0:1, :] += jnp.sum(dh * x * r, axis=0, keepdims=True)

    return pl.pallas_call(
        body, name=f"inproj_bwd_dx_{li}", grid=(ni, nk),
        in_specs=[pl.BlockSpec((tm, tk), lambda i, k: (i, k)),
                  pl.BlockSpec((tk, D), lambda i, k: (k, 0)),
                  pl.BlockSpec((tm, LANES), lambda i, k: (i, 0)),
                  pl.BlockSpec((LANES, D), lambda i, k: (0, 0)),
                  pl.BlockSpec((tm, D), lambda i, k: (i, 0)),
                  pl.BlockSpec((1, D), lambda i, k: (0, 0)),
                  pl.BlockSpec((tm, D), lambda i, k: (i, 0))],
        out_specs=[pl.BlockSpec((tm, D), lambda i, k: (i, 0)),
                   pl.BlockSpec((tm, D), lambda i, k: (i, 0)),
                   pl.BlockSpec((8, D), lambda i, k: (0, 0))],
        out_shape=[jax.ShapeDtypeStruct((T, D), F32), jax.ShapeDtypeStruct((T, D), BF16),
                   jax.ShapeDtypeStruct((8, D), F32)],
        scratch_shapes=[pltpu.VMEM((tm, D), F32)],
        compiler_params=_cparams(("arbitrary", "arbitrary"), VMEM_LIMIT),
    )(dproj, wmain_t, dps16, wsmall_t, x2, nw, dxo)


def _merge_fwd(ya, yb, yc, proj, gbias, wp, wout, x2, li):
    T = x2.shape[0]
    tm = min(512, T)
    gcol = OFF_G // D

    def body(ya_ref, yb_ref, yc_ref, g0_ref, g1_ref, g2_ref, gb_ref, wp_ref, wo_ref, x_ref, xn_ref, br_ref):
        merged = jnp.zeros((tm, D), F32)
        for i, (y_ref, g_ref) in enumerate(((ya_ref, g0_ref), (yb_ref, g1_ref), (yc_ref, g2_ref))):
            br = _dot(y_ref[...], wp_ref[i])
            br_ref[i] = br.astype(BF16)
            gate = _sigmoid(g_ref[...].astype(F32) + gb_ref[i:i + 1, :])
            merged = merged + gate * br
        xn_ref[...] = x_ref[...] + _dot(merged.astype(BF16), wo_ref[...])

    row = lambda c: pl.BlockSpec((tm, D), lambda i, c=c: (i, c))
    return pl.pallas_call(
        body, name=f"merge_fwd_{li}", grid=(T // tm,),
        in_specs=[row(0), row(0), row(0), row(gcol), row(gcol + 1), row(gcol + 2),
                  pl.BlockSpec((3, D), lambda i: (0, 0)),
                  pl.BlockSpec((3, D, D), lambda i: (0, 0, 0)),
                  pl.BlockSpec((D, D), lambda i: (0, 0)),
                  row(0)],
        out_specs=[row(0), pl.BlockSpec((3, tm, D), lambda i: (0, i, 0))],
        out_shape=[jax.ShapeDtypeStruct((T, D), F32), jax.ShapeDtypeStruct((3, T, D), BF16)],
        compiler_params=_cparams(("parallel",), VMEM_LIMIT),
    )(ya, yb, yc, proj, proj, proj, gbias, wp, wout, x2)


def _merge_bwd_gates(dxo16, wout_t, br, proj, gbias, li):
    T = dxo16.shape[0]
    tm = min(512, T)
    gcol = OFF_G // D

    def body(dx_ref, wot_ref, br_ref, g0_ref, g1_ref, g2_ref, gb_ref, dbr_ref, dg_ref, mg_ref, dgb_ref):
        @pl.when(pl.program_id(0) == 0)
        def _():
            dgb_ref[...] = jnp.zeros_like(dgb_ref)

        dm = _dot(dx_ref[...], wot_ref[...])
        merged = jnp.zeros((tm, D), F32)
        for i, g_ref in enumerate((g0_ref, g1_ref, g2_ref)):
            b = br_ref[i].astype(F32)
            gate = _sigmoid(g_ref[...].astype(F32) + gb_ref[i:i + 1, :])
            merged = merged + gate * b
            dbr_ref[i] = (dm * gate).astype(BF16)
            dgate = dm * b * gate * (1.0 - gate)
            dg_ref[:, D * i:D * (i + 1)] = dgate.astype(BF16)
            dgb_ref[i:i + 1, :] += jnp.sum(dgate, axis=0, keepdims=True)
        mg_ref[...] = merged.astype(BF16)

    row = lambda c: pl.BlockSpec((tm, D), lambda i, c=c: (i, c))
    return pl.pallas_call(
        body, name=f"merge_bwd_gates_{li}", grid=(T // tm,),
        in_specs=[row(0), pl.BlockSpec((D, D), lambda i: (0, 0)),
                  pl.BlockSpec((3, tm, D), lambda i: (0, i, 0)),
                  row(gcol), row(gcol + 1), row(gcol + 2),
                  pl.BlockSpec((3, D), lambda i: (0, 0))],
        out_specs=[pl.BlockSpec((3, tm, D), lambda i: (0, i, 0)),
                   pl.BlockSpec((tm, 3 * D), lambda i: (i, 0)),
                   row(0),
                   pl.BlockSpec((8, D), lambda i: (0, 0))],
        out_shape=[jax.ShapeDtypeStruct((3, T, D), BF16), jax.ShapeDtypeStruct((T, 3 * D), BF16),
                   jax.ShapeDtypeStruct((T, D), BF16), jax.ShapeDtypeStruct((8, D), F32)],
        compiler_params=_cparams(("arbitrary",), VMEM_LIMIT),
    )(dxo16, wout_t, br, proj, proj, proj, gbias)


def _final_loss(x2, tgt, fw):
    T = x2.shape[0]
    tm = min(512, T)
    ni = T // tm

    def body(x_ref, t_ref, w_ref, dx_ref, dx16_ref, st_ref):
        i = pl.program_id(0)

        @pl.when(i == 0)
        def _():
            st_ref[...] = jnp.zeros_like(st_ref)

        x = x_ref[...]
        r = lax.rsqrt(jnp.mean(x * x, axis=-1, keepdims=True) + EPS)
        xh = x * r
        err = xh * w_ref[...] - t_ref[...]
        dy = err * (1.0 / D)
        g = dy * w_ref[...]
        dx = r * g - x * (r * r * r) * jnp.mean(g * x, axis=-1, keepdims=True)
        dx_ref[...] = dx
        dx16_ref[...] = dx.astype(BF16)
        st_ref[0:1, :] += jnp.sum(dy * xh, axis=0, keepdims=True)
        st_ref[1:2, :] += jnp.sum(err * err, axis=0, keepdims=True)

        @pl.when(i == ni - 1)
        def _():
            tot = jnp.sum(st_ref[1:2, :], axis=1, keepdims=True) * (0.5 / D)
            st_ref[2:3, :] = jnp.broadcast_to(tot, (1, D))

    row = pl.BlockSpec((tm, D), lambda i: (i, 0))
    return pl.pallas_call(
        body, name="final_loss", grid=(ni,),
        in_specs=[row, row, pl.BlockSpec((1, D), lambda i: (0, 0))],
        out_specs=[row, row, pl.BlockSpec((8, D), lambda i: (0, 0))],
        out_shape=[jax.ShapeDtypeStruct((T, D), F32), jax.ShapeDtypeStruct((T, D), BF16),
                   jax.ShapeDtypeStruct((8, D), F32)],
        compiler_params=_cparams(("arbitrary",), VMEM_LIMIT),
    )(x2, tgt, fw)


def _zgate_bwd(dy, o, proj, zcol, name):
    T = dy.shape[0]
    tm = min(512, T)

    def body(dy_ref, o_ref, z_ref, do_ref, dz_ref):
        z = z_ref[...].astype(F32)
        dyv = dy_ref[...].astype(F32)
        sg = _sigmoid(z)
        do_ref[...] = (dyv * z * sg).astype(BF16)
        dz_ref[...] = (dyv * o_ref[...].astype(F32) * sg * (1.0 + z * (1.0 - sg))).astype(BF16)

    row = lambda c: pl.BlockSpec((tm, D), lambda i, c=c: (i, c))
    sds = jax.ShapeDtypeStruct((T, D), BF16)
    return pl.pallas_call(
        body, name=name, grid=(T // tm,),
        in_specs=[row(0), row(0), row(zcol)], out_specs=[row(0), row(0)], out_shape=[sds, sds],
        compiler_params=_cparams(("parallel",), VMEM_LIMIT),
    )(dy, o, proj)


def _fox_cum(ps, fb_row, S, li):
    T = ps.shape[0]
    nb = S // LCH

    def body(ps_ref, fb_ref, cum_ref, carry):
        @pl.when(pl.program_id(1) == 0)
        def _():
            carry[...] = jnp.zeros_like(carry)

        logf = -_softplus(-(ps_ref[...] + fb_ref[...]))
        r = lax.broadcasted_iota(jnp.int32, (LCH, LCH), 0)
        c = lax.broadcasted_iota(jnp.int32, (LCH, LCH), 1)
        tri = (r >= c).astype(F32)
        cum = _dot_hi(tri, logf) + carry[0:1, :]
        cum_ref[...] = cum
        carry[0:1, :] = cum[LCH - 1:LCH, :]

    return pl.pallas_call(
        body, name=f"fox_cum_{li}", grid=(T // S, nb),
        in_specs=[pl.BlockSpec((LCH, LANES), lambda b, i: (b * nb + i, 0)),
                  pl.BlockSpec((1, LANES), lambda b, i: (0, 0))],
        out_specs=pl.BlockSpec((LCH, LANES), lambda b, i: (b * nb + i, 0)),
        out_shape=jax.ShapeDtypeStruct((T, LANES), F32),
        scratch_shapes=[pltpu.VMEM((8, LANES), F32)],
        compiler_params=_cparams(("arbitrary", "arbitrary")),
    )(ps, fb_row)


def _fox_cum_bwd(dcum, ps, fb_row, S, li):
    T = ps.shape[0]
    nb = S // LCH

    def body(dc_ref, ps_ref, fb_ref, df_ref, dfb_ref, carry):
        b, i = pl.program_id(0), pl.program_id(1)

        @pl.when(i == 0)
        def _():
            carry[...] = jnp.zeros_like(carry)

        @pl.when(jnp.logical_and(b == 0, i == 0))
        def _():
            dfb_ref[...] = jnp.zeros_like(dfb_ref)

        dc = dc_ref[...]
        r = lax.broadcasted_iota(jnp.int32, (LCH, LCH), 0)
        c = lax.broadcasted_iota(jnp.int32, (LCH, LCH), 1)
        tri = (c >= r).astype(F32)
        dlogf = _dot_hi(tri, dc) + carry[0:1, :]
        carry[0:1, :] += jnp.sum(dc, axis=0, keepdims=True)
        df = dlogf * _sigmoid(-(ps_ref[...] + fb_ref[...]))
        lane = _lane_iota()
        df = jnp.where(jnp.logical_and(lane >= NH, lane < 2 * NH), df, 0.0)
        df_ref[...] = df
        dfb_ref[0:1, :] += jnp.sum(df, axis=0, keepdims=True)

    blk = pl.BlockSpec((LCH, LANES), lambda b, i: (b * nb + nb - 1 - i, 0))
    return pl.pallas_call(
        body, name=f"fox_cum_bwd_{li}", grid=(T // S, nb),
        in_specs=[blk, blk, pl.BlockSpec((1, LANES), lambda b, i: (0, 0))],
        out_specs=[blk, pl.BlockSpec((8, LANES), lambda b, i: (0, 0))],
        out_shape=[jax.ShapeDtypeStruct((T, LANES), F32), jax.ShapeDtypeStruct((8, LANES), F32)],
        scratch_shapes=[pltpu.VMEM((8, LANES), F32)],
        compiler_params=_cparams(("arbitrary", "arbitrary")),
    )(dcum, ps, fb_row)


def _fox_blocks(S):
    bq = min(512, S)
    return bq, S // bq


def _fox_fwd(proj, cum_col, cum_row, S, li):
    T = proj.shape[0]
    B = T // S
    bq, nq = _fox_blocks(S)
    qc, kc, vc, zc = OFF_CQ // LANES, OFF_CK // LANES, OFF_CV // LANES, OFF_CZ // LANES

    def body(q_ref, k_ref, v_ref, z_ref, cq_ref, ck_ref, y_ref, o_ref, lse_ref):
        i = pl.program_id(2)
        q2 = q_ref[...]
        m0 = _lane_iota() < HD
        qh = (jnp.where(m0, q2, jnp.zeros_like(q2)), jnp.where(m0, jnp.zeros_like(q2), q2))
        row = lax.broadcasted_iota(jnp.int32, (bq, bq), 0)
        col = lax.broadcasted_iota(jnp.int32, (bq, bq), 1)
        outs = []
        for hh in range(2):
            cq = cq_ref[0, hh]

            def step(j, carry, masked, hh=hh, cq=cq):
                m, l, acc = carry
                start = pl.multiple_of(j * bq, bq)
                k2 = k_ref[pl.ds(start, bq), :]
                v2 = v_ref[pl.ds(start, bq), :]
                s = _dot_nt(qh[hh], k2) * SCALE + cq - ck_ref[0, hh, j]
                if masked:
                    s = jnp.where(row >= col, s, NEG)
                mn = jnp.maximum(m, jnp.max(s, axis=1, keepdims=True))
                alpha = jnp.exp(m - mn)
                p = jnp.exp(s - mn)
                l = alpha * l + jnp.sum(p, axis=1, keepdims=True)
                acc = alpha * acc + _dot(p.astype(BF16), v2)
                return mn, l, acc

            init = (jnp.full((bq, 1), NEG, F32), jnp.zeros((bq, 1), F32), jnp.zeros((bq, LANES), F32))
            carry = lax.fori_loop(0, i, functools.partial(step, masked=False), init)
            m, l, acc = step(i, carry, True)
            outs.append(acc / l)
            lse_ref[0, hh] = m + jnp.log(l)
        o2 = jnp.where(m0, outs[0], outs[1])
        z = z_ref[...].astype(F32)
        o_ref[...] = o2.astype(BF16)
        y_ref[...] = (o2 * z * _sigmoid(z)).astype(BF16)

    qblk = lambda c: pl.BlockSpec((bq, LANES), lambda b, p, i, c=c: (b * nq + i, c + p))
    sblk = lambda c: pl.BlockSpec((S, LANES), lambda b, p, i, c=c: (b, c + p))
    return pl.pallas_call(
        body, name=f"fox_fwd_{li}", grid=(B, NH // 2, nq),
        in_specs=[qblk(qc), sblk(kc), sblk(vc), qblk(zc),
                  pl.BlockSpec((1, 2, bq, 1), lambda b, p, i: (b, p, i, 0)),
                  pl.BlockSpec((1, 2, nq, 1, bq), lambda b, p, i: (b, p, 0, 0, 0))],
        out_specs=[qblk(0), qblk(0), pl.BlockSpec((1, 2, bq, 1), lambda b, p, i: (b, p, i, 0))],
        out_shape=[jax.ShapeDtypeStruct((T, D), BF16), jax.ShapeDtypeStruct((T, D), BF16),
                   jax.ShapeDtypeStruct((B, NH, S, 1), F32)],
        compiler_params=_cparams(("parallel", "parallel", "arbitrary"), VMEM_LIMIT),
    )(proj, proj, proj, proj, cum_col, cum_row)


def _fox_bwd(proj, do, o, cum_col, cum_row, lse, S, li):
    T = proj.shape[0]
    B = T // S
    bq, nq = _fox_blocks(S)
    qc, kc, vc = OFF_CQ // LANES, OFF_CK // LANES, OFF_CV // LANES

    def body(q_ref, k_ref, v_ref, do_ref, o_ref, cq_ref, ck_ref, lse_ref, dq_ref, dk_ref, dv_ref, dc_ref, dr_ref,
             dq_scr, dr_scr):
        j = pl.program_id(2)

        @pl.when(j == 0)
        def _():
            dq_scr[...] = jnp.zeros_like(dq_scr)
            dr_scr[...] = jnp.zeros_like(dr_scr)

        m0 = _lane_iota() < HD
        k2 = k_ref[...]
        v2 = v_ref[...]
        zk = jnp.zeros_like(k2)
        kh = (jnp.where(m0, k2, zk), jnp.where(m0, zk, k2))
        row = lax.broadcasted_iota(jnp.int32, (bq, bq), 0)
        col = lax.broadcasted_iota(jnp.int32, (bq, bq), 1)
        ck = (ck_ref[0, 0, 0], ck_ref[0, 1, 0])

        def step(i, carry, masked):
            dk, dv, dc0, dc1 = carry
            dcs = [dc0, dc1]
            start = pl.multiple_of(i * bq, bq)
            q2 = q_ref[pl.ds(start, bq), :]
            do2 = do_ref[pl.ds(start, bq), :]
            prod = do2.astype(F32) * o_ref[pl.ds(start, bq), :].astype(F32)
            zq = jnp.zeros_like(q2)
            dq = jnp.zeros((bq, LANES), F32)
            for hh in range(2):
                sel = m0 if hh == 0 else jnp.logical_not(m0)
                qh = jnp.where(sel, q2, zq)
                doh = jnp.where(sel, do2, zq)
                delta = _head_sum(prod, hh)
                s = _dot_nt(qh, k2) * SCALE + cq_ref[0, hh, pl.ds(start, bq), :] - ck[hh]
                if masked:
                    s = jnp.where(row >= col, s, NEG)
                p = jnp.exp(s - lse_ref[0, hh, pl.ds(start, bq), :])
                dp = _dot_nt(doh, v2)
                ds = p * (dp - delta)
                dcs[hh] = dcs[hh] - jnp.sum(ds, axis=0, keepdims=True)
                dr_scr[hh, pl.ds(start, bq), :] += jnp.sum(ds, axis=1, keepdims=True)
                dsb = (ds * SCALE).astype(BF16)
                dv = dv + _dot_tn(p.astype(BF16), doh)
                dk = dk + _dot_tn(dsb, qh)
                dq = dq + _dot(dsb, kh[hh])
            dq_scr[pl.ds(start, bq), :] += dq
            return dk, dv, dcs[0], dcs[1]

        zero = jnp.zeros((bq, LANES), F32)
        zrow = jnp.zeros((1, bq), F32)
        carry = step(j, (zero, zero, zrow, zrow), True)
        dk, dv, dc0, dc1 = lax.fori_loop(j + 1, nq, functools.partial(step, masked=False), carry)
        dk_ref[...] = dk.astype(BF16)
        dv_ref[...] = dv.astype(BF16)
        dc_ref[0, 0, 0] = dc0
        dc_ref[0, 1, 0] = dc1

        @pl.when(j == nq - 1)
        def _():
            dq_ref[...] = dq_scr[...].astype(BF16)
            dr_ref[0] = dr_scr[...]

    sblk = lambda c: pl.BlockSpec((S, LANES), lambda b, p, j, c=c: (b, c + p))
    kblk = lambda c: pl.BlockSpec((bq, LANES), lambda b, p, j, c=c: (b * nq + j, c + p))
    col_spec = pl.BlockSpec((1, 2, S, 1), lambda b, p, j: (b, p, 0, 0))
    return pl.pallas_call(
        body, name=f"fox_bwd_{li}", grid=(B, NH // 2, nq),
        in_specs=[sblk(qc), kblk(kc), kblk(vc), sblk(0), sblk(0), col_spec,
                  pl.BlockSpec((1, 2, 1, 1, bq), lambda b, p, j: (b, p, j, 0, 0)), col_spec],
        out_specs=[sblk(0), kblk(0), kblk(0), pl.BlockSpec((1, 2, 1, 1, bq), lambda b, p, j: (b, p, j, 0, 0)),
                   col_spec],
        out_shape=[jax.ShapeDtypeStruct((T, D), BF16), jax.ShapeDtypeStruct((T, D), BF16),
                   jax.ShapeDtypeStruct((T, D), BF16), jax.ShapeDtypeStruct((B, NH, nq, 1, bq), F32),
                   jax.ShapeDtypeStruct((B, NH, S, 1), F32)],
        scratch_shapes=[pltpu.VMEM((S, LANES), F32), pltpu.VMEM((2, S, 1), F32)],
        compiler_params=_cparams(("parallel", "parallel", "arbitrary"), VMEM_LIMIT),
    )(proj, proj, proj, do, o, cum_col, cum_row, lse)


def _swa_blocks(S):
    bq = min(512, S)
    return bq, S // bq, bq // LCH


def _dup_head(xw, kvl):
    m0 = _lane_iota() < HD
    a = jnp.where(m0 if kvl == 0 else jnp.logical_not(m0), xw, 0.0)
    return (a + pltpu.roll(a, HD, 1)).astype(BF16)


def _band(same_block):
    r = lax.broadcasted_iota(jnp.int32, (LCH, LCH), 0)
    c = lax.broadcasted_iota(jnp.int32, (LCH, LCH), 1)
    return (c <= r) if same_block else (c > r)


def _swa_fwd(proj, sinks, S, li):
    T = proj.shape[0]
    B = T // S
    bq, nq, nsub = _swa_blocks(S)
    nrow = S // LCH
    qc, zc, kc, vc = OFF_BQ // 512, OFF_BZ // 512, OFF_BK // LANES, OFF_BV // LANES

    def body(sk_ref, q_ref, z_ref, kp_ref, kc_ref, vp_ref, vc_ref, y_ref, o_ref, lse_ref):
        c, i = pl.program_id(0), pl.program_id(2)
        m0 = _lane_iota() < HD
        kw = jnp.concatenate([kp_ref[...].astype(F32), kc_ref[...].astype(F32)], axis=0)
        vw = jnp.concatenate([vp_ref[...].astype(F32), vc_ref[...].astype(F32)], axis=0)
        kd = (_dup_head(kw, 0), _dup_head(kw, 1))
        vd = (_dup_head(vw, 0), _dup_head(vw, 1))
        valid = jnp.concatenate([_band(False), _band(True)], axis=1)
        col = lax.broadcasted_iota(jnp.int32, (LCH, 2 * LCH), 1)
        valid_first = jnp.logical_and(valid, jnp.logical_or(col >= LCH, i > 0))
        for r in range(nsub):
            rows = slice(LCH * r, LCH * (r + 1))
            msk = valid_first if r == 0 else valid
            for ch in range(4):
                kvl = ch // 2
                kwin = kd[kvl][LCH * r:LCH * (r + 2)]
                vwin = vd[kvl][LCH * r:LCH * (r + 2)]
                q2 = q_ref[rows, LANES * ch:LANES * (ch + 1)]
                outs = []
                for hh in range(2):
                    hl = 2 * ch + hh
                    sel = m0 if hh == 0 else jnp.logical_not(m0)
                    qh = jnp.where(sel, q2, jnp.zeros_like(q2))
                    s = jnp.where(msk, _dot_nt(qh, kwin) * SCALE, NEG)
                    sink = sk_ref[8 * c + hl]
                    m = jnp.maximum(jnp.max(s, axis=1, keepdims=True), sink)
                    p = jnp.exp(s - m)
                    l = jnp.sum(p, axis=1, keepdims=True) + jnp.exp(sink - m)
                    outs.append(_dot(p.astype(BF16), vwin) / l)
                    lse_ref[0, hl, rows, :] = m + jnp.log(l)
                o2 = jnp.where(m0, outs[0], outs[1])
                z = z_ref[rows, LANES * ch:LANES * (ch + 1)].astype(F32)
                o_ref[rows, LANES * ch:LANES * (ch + 1)] = o2.astype(BF16)
                y_ref[rows, LANES * ch:LANES * (ch + 1)] = (o2 * z * _sigmoid(z)).astype(BF16)

    wide = lambda cc: pl.BlockSpec((bq, 512), lambda c, b, i, cc=cc: (b * nq + i, cc + c))
    cur = lambda cc: pl.BlockSpec((bq, LANES), lambda c, b, i, cc=cc: (b * nq + i, cc + c))
    prev = lambda cc: pl.BlockSpec((LCH, LANES), lambda c, b, i, cc=cc: (b * nrow + jnp.maximum(i * nsub - 1, 0), cc + c))
    return pl.pallas_call(
        body, name=f"swa_fwd_{li}", grid=(2, B, nq),
        in_specs=[pl.BlockSpec(memory_space=pltpu.SMEM), wide(qc), wide(zc), prev(kc), cur(kc), prev(vc), cur(vc)],
        out_specs=[wide(0), wide(0), pl.BlockSpec((1, 8, bq, 1), lambda c, b, i: (b, c, i, 0))],
        out_shape=[jax.ShapeDtypeStruct((T, D), BF16), jax.ShapeDtypeStruct((T, D), BF16),
                   jax.ShapeDtypeStruct((B, NH, S, 1), F32)],
        compiler_params=_cparams(("parallel", "parallel", "parallel"), VMEM_LIMIT),
    )(sinks, proj, proj, proj, proj, proj, proj)


def _swa_bwd_dq(proj, do, o, lse, sinks, cos128, sin128, S, li):
    T = proj.shape[0]
    B = T // S
    bq, nq, nsub = _swa_blocks(S)
    nrow = S // LCH
    qc, kc, vc = OFF_BQ // 512, OFF_BK // LANES, OFF_BV // LANES

    def body(sk_ref, q_ref, do_ref, o_ref, lse_ref, kp_ref, kc_ref, vp_ref, vc_ref, cos_ref, sin_ref, dq_ref, dsk_ref):
        c, b, i = pl.program_id(0), pl.program_id(1), pl.program_id(2)

        @pl.when(jnp.logical_and(b == 0, i == 0))
        def _():
            dsk_ref[...] = jnp.zeros_like(dsk_ref)

        m0 = _lane_iota() < HD
        kw = jnp.concatenate([kp_ref[...].astype(F32), kc_ref[...].astype(F32)], axis=0)
        vw = jnp.concatenate([vp_ref[...].astype(F32), vc_ref[...].astype(F32)], axis=0)
        kd = (_dup_head(kw, 0), _dup_head(kw, 1))
        vd = (_dup_head(vw, 0), _dup_head(vw, 1))
        valid = jnp.concatenate([_band(False), _band(True)], axis=1)
        col = lax.broadcasted_iota(jnp.int32, (LCH, 2 * LCH), 1)
        valid_first = jnp.logical_and(valid, jnp.logical_or(col >= LCH, i > 0))
        dsk = [jnp.zeros((1, 1), F32) for _ in range(8)]
        for r in range(nsub):
            rows = slice(LCH * r, LCH * (r + 1))
            msk = valid_first if r == 0 else valid
            for ch in range(4):
                kvl = ch // 2
                kwin = kd[kvl][LCH * r:LCH * (r + 2)]
                vwin = vd[kvl][LCH * r:LCH * (r + 2)]
                lanes = slice(LANES * ch, LANES * (ch + 1))
                q2 = q_ref[rows, lanes]
                do2 = do_ref[rows, lanes]
                prod = do2.astype(F32) * o_ref[rows, lanes].astype(F32)
                dqs = []
                for hh in range(2):
                    hl = 2 * ch + hh
                    sel = m0 if hh == 0 else jnp.logical_not(m0)
                    qh = jnp.where(sel, q2, jnp.zeros_like(q2))
                    doh = jnp.where(sel, do2, jnp.zeros_like(do2))
                    lse = lse_ref[0, hl, rows, :]
                    s = jnp.where(msk, _dot_nt(qh, kwin) * SCALE, NEG)
                    p = jnp.exp(s - lse)
                    delta = _head_sum(prod, hh)
                    ds = p * (_dot_nt(doh, vwin) - delta)
                    dqs.append(_dot((ds * SCALE).astype(BF16), kwin))
                    psink = jnp.exp(sk_ref[8 * c + hl] - lse)
                    dsk[hl] = dsk[hl] - jnp.sum(psink * delta, axis=0, keepdims=True)
                dq2 = jnp.where(m0, dqs[0], dqs[1])
                dq2 = dq2 * cos_ref[rows, :] - _rot_half(dq2) * sin_ref[rows, :]
                dq_ref[rows, lanes] = dq2.astype(BF16)
        for hl in range(8):
            dsk_ref[0, hl:hl + 1, :] += jnp.broadcast_to(dsk[hl], (1, LANES))

    wide = lambda cc: pl.BlockSpec((bq, 512), lambda c, b, i, cc=cc: (b * nq + i, cc + c))
    cur = lambda cc: pl.BlockSpec((bq, LANES), lambda c, b, i, cc=cc: (b * nq + i, cc + c))
    prev = lambda cc: pl.BlockSpec((LCH, LANES), lambda c, b, i, cc=cc: (b * nrow + jnp.maximum(i * nsub - 1, 0), cc + c))
    pos = pl.BlockSpec((bq, LANES), lambda c, b, i: (i, 0))
    return pl.pallas_call(
        body, name=f"swa_bwd_dq_{li}", grid=(2, B, nq),
        in_specs=[pl.BlockSpec(memory_space=pltpu.SMEM), wide(qc), wide(0), wide(0),
                  pl.BlockSpec((1, 8, bq, 1), lambda c, b, i: (b, c, i, 0)),
                  prev(kc), cur(kc), prev(vc), cur(vc), pos, pos],
        out_specs=[wide(0), pl.BlockSpec((1, 8, LANES), lambda c, b, i: (c, 0, 0))],
        out_shape=[jax.ShapeDtypeStruct((T, D), BF16), jax.ShapeDtypeStruct((2, 8, LANES), F32)],
        compiler_params=_cparams(("arbitrary", "arbitrary", "arbitrary"), VMEM_LIMIT),
    )(sinks, proj, do, o, lse, proj, proj, proj, proj, cos128, sin128)


def _swa_bwd_dkv(proj, do, o, lse, cos128, sin128, S, li):
    T = proj.shape[0]
    B = T // S
    bk, nk, nsub = _swa_blocks(S)
    nrow = S // LCH
    qc, kc, vc = OFF_BQ // 512, OFF_BK // LANES, OFF_BV // LANES

    def body(q_ref, qn_ref, do_ref, don_ref, o_ref, on_ref, lse_ref, lsen_ref, k_ref, v_ref, cos_ref, sin_ref,
             dk_ref, dv_ref):
        j = pl.program_id(2)
        m0 = _lane_iota() < HD
        has_next = (j < nk - 1).astype(F32)
        kf = k_ref[...].astype(F32)
        vf = v_ref[...].astype(F32)
        kd = (_dup_head(kf, 0), _dup_head(kf, 1))
        vd = (_dup_head(vf, 0), _dup_head(vf, 1))
        masks = (_band(True), _band(False))
        for kr in range(nsub):
            krows = slice(LCH * kr, LCH * (kr + 1))
            dk = jnp.zeros((LCH, LANES), F32)
            dv = jnp.zeros((LCH, LANES), F32)
            for dq_blk in range(2):
                rq = kr + dq_blk
                nxt = rq == nsub
                qrows = slice(0, LCH) if nxt else slice(LCH * rq, LCH * (rq + 1))
                qr, dor, orr, lr = (qn_ref, don_ref, on_ref, lsen_ref) if nxt else (q_ref, do_ref, o_ref, lse_ref)
                for ch in range(4):
                    kvl = ch // 2
                    lanes = slice(LANES * ch, LANES * (ch + 1))
                    q2 = qr[qrows, lanes]
                    do2 = dor[qrows, lanes]
                    if nxt:
                        do2 = (do2.astype(F32) * has_next).astype(BF16)
                    prod = do2.astype(F32) * orr[qrows, lanes].astype(F32)
                    for hh in range(2):
                        hl = 2 * ch + hh
                        sel = m0 if hh == 0 else jnp.logical_not(m0)
                        qh = jnp.where(sel, q2, jnp.zeros_like(q2))
                        doh = jnp.where(sel, do2, jnp.zeros_like(do2))
                        s = jnp.where(masks[dq_blk], _dot_nt(qh, kd[kvl][krows]) * SCALE, NEG)
                        p = jnp.exp(s - lr[0, hl, qrows, :])
                        ds = p * (_dot_nt(doh, vd[kvl][krows]) - _head_sum(prod, hh))
                        dvc = _dot_tn(p.astype(BF16), doh)
                        dkc = _dot_tn((ds * SCALE).astype(BF16), qh)
                        if hh != kvl:
                            dvc = pltpu.roll(dvc, HD, 1)
                            dkc = pltpu.roll(dkc, HD, 1)
                        dv = dv + dvc
                        dk = dk + dkc
            dk = dk * cos_ref[krows, :] - _rot_half(dk) * sin_ref[krows, :]
            dk_ref[krows, :] = dk.astype(BF16)
            dv_ref[krows, :] = dv.astype(BF16)

    wide = lambda cc: pl.BlockSpec((bk, 512), lambda c, b, j, cc=cc: (b * nk + j, cc + c))
    nxt = lambda cc: pl.BlockSpec((LCH, 512), lambda c, b, j, cc=cc: (b * nrow + jnp.minimum((j + 1) * nsub, nrow - 1), cc + c))
    cur = lambda cc: pl.BlockSpec((bk, LANES), lambda c, b, j, cc=cc: (b * nk + j, cc + c))
    pos = pl.BlockSpec((bk, LANES), lambda c, b, j: (j, 0))
    return pl.pallas_call(
        body, name=f"swa_bwd_dkv_{li}", grid=(2, B, nk),
        in_specs=[wide(qc), nxt(qc), wide(0), nxt(0), wide(0), nxt(0),
                  pl.BlockSpec((1, 8, bk, 1), lambda c, b, j: (b, c, j, 0)),
                  pl.BlockSpec((1, 8, LCH, 1), lambda c, b, j: (b, c, jnp.minimum((j + 1) * nsub, nrow - 1), 0)),
                  cur(kc), cur(vc), pos, pos],
        out_specs=[cur(0), cur(0)],
        out_shape=[jax.ShapeDtypeStruct((T, 2 * LANES), BF16), jax.ShapeDtypeStruct((T, 2 * LANES), BF16)],
        compiler_params=_cparams(("parallel", "parallel", "parallel"), VMEM_LIMIT),
    )(proj, proj, do, do, o, o, lse, lse, proj, proj, cos128, sin128)


HALO = 16


def _ssm_chunk_pre(ext, cw_ref, cb_ref, ps, dtb, alog):
    pre = cb_ref[...]
    for k in range(4):
        pre = pre + cw_ref[k:k + 1, :] * ext[pl.ds(HALO - 3 + k, LCH), :]
    sg = _sigmoid(pre)
    dt = _softplus(ps + dtb)
    a = -jnp.exp(alog)
    r = lax.broadcasted_iota(jnp.int32, (LCH, LCH), 0)
    c = lax.broadcasted_iota(jnp.int32, (LCH, LCH), 1)
    acum = _dot_hi((r >= c).astype(F32), dt * a)
    return pre, sg, dt, a, acum


def _pairsel(v, p):
    return jnp.where(_lane_iota() < HD, v[:, 2 * p:2 * p + 1], v[:, 2 * p + 1:2 * p + 2])


def _decay(acum, acum_t, h):
    r = lax.broadcasted_iota(jnp.int32, (LCH, LCH), 0)
    c = lax.broadcasted_iota(jnp.int32, (LCH, LCH), 1)
    causal = r >= c
    seg = acum[:, h:h + 1] - acum_t[h:h + 1, :]
    return jnp.where(causal, jnp.exp(jnp.where(causal, seg, 0.0)), 0.0)


def _ssm_pair_fwd(p, x, dt, acum, acum_t, e_all, w_all, cd, cb_g, b_g, c_g, hprev, dsk_ref):
    m0 = _lane_iota() < HD
    lanes = slice(LANES * p, LANES * (p + 1))
    x2 = x[:, lanes]
    dt2 = _pairsel(dt, p)
    xdt2 = x2 * dt2
    xdtb = xdt2.astype(BF16)
    lms, ms, yds = [], [], []
    for hh in range(2):
        lm = _decay(acum, acum_t, 2 * p + hh)
        mm = cb_g * lm
        lms.append(lm)
        ms.append(mm)
        yds.append(_dot(mm.astype(BF16), xdtb))
    yd2 = jnp.where(m0, yds[0], yds[1])
    w2 = _pairsel(w_all, p)
    xw = (xdt2 * w2).astype(BF16)
    s2 = _dot_tn(xw, b_g)
    z2 = _dot_nt(c_g, hprev.astype(BF16))
    e2 = _pairsel(e_all, p)
    rowsel = lax.broadcasted_iota(jnp.int32, (LANES, 1), 0) < HD
    cdcol = jnp.where(rowsel, cd[:, 2 * p:2 * p + 1], cd[:, 2 * p + 1:2 * p + 2])
    y2 = yd2 + z2 * e2 + dsk_ref[:, lanes] * x2
    return dict(x2=x2, dt2=dt2, xdt2=xdt2, xdtb=xdtb, lms=lms, ms=ms, yd2=yd2, w2=w2, xw=xw, s2=s2, z2=z2, e2=e2,
                cdcol=cdcol, y2=y2)


def _ssm_specs(S, rev):
    nc = S // LCH
    ch = (lambda c: nc - 1 - c) if rev else (lambda c: c)
    prev = pl.BlockSpec((HALO, 2 * D), lambda b, c: (jnp.maximum(b * (S // HALO) + ch(c) * (LCH // HALO) - 1, 0), 0))
    cur = pl.BlockSpec((LCH, 2 * D), lambda b, c: (b * nc + ch(c), 0))
    zed = pl.BlockSpec((LCH, D), lambda b, c: (b * nc + ch(c), OFF_AZ // D))
    row = pl.BlockSpec((LCH, D), lambda b, c: (b * nc + ch(c), 0))
    psb = pl.BlockSpec((LCH, LANES), lambda b, c: (b * nc + ch(c), 0))
    hpb = pl.BlockSpec((1, 1, NH // 2, LANES, NST), lambda b, c: (b, ch(c), 0, 0, 0))
    const = lambda r, w: pl.BlockSpec((r, w), lambda b, c: (0, 0))
    return nc, prev, cur, zed, row, psb, hpb, const


def _ssm_fwd(proj, ps, cw, cb, dtb, alog, dsk, nw, S, li):
    T = proj.shape[0]
    B = T // S
    nc, prev, cur, zed, row, psb, hpb, const = _ssm_specs(S, False)

    def body(prev_ref, cur_ref, z_ref, ps_ref, cw_ref, cb_ref, dtb_ref, alog_ref, dsk_ref, nw_ref,
             ya_ref, hp_ref, h_scr, ext):
        c = pl.program_id(1)

        @pl.when(c == 0)
        def _():
            h_scr[...] = jnp.zeros_like(h_scr)

        ext[0:HALO, :] = prev_ref[...].astype(F32) * (c > 0).astype(F32)
        ext[HALO:HALO + LCH, :] = cur_ref[...].astype(F32)
        pre, sg, dt, a, acum = _ssm_chunk_pre(ext, cw_ref, cb_ref, ps_ref[...], dtb_ref[...], alog_ref[...])
        act = pre * sg
        acum_t = acum.T
        e_all = jnp.exp(acum)
        last = acum[LCH - 1:LCH, :]
        w_all = jnp.exp(last - acum)
        cd = jnp.exp(last)
        x = act[:, :D]
        for g in range(NGRP):
            b_g = act[:, D + NST * g:D + NST * (g + 1)].astype(BF16)
            c_g = act[:, D + NGRP * NST + NST * g:D + NGRP * NST + NST * (g + 1)].astype(BF16)
            cb_g = _dot_nt(c_g, b_g)
            ygs = []
            for p in (2 * g, 2 * g + 1):
                hprev = h_scr[p]
                hp_ref[0, 0, p] = hprev
                f = _ssm_pair_fwd(p, x, dt, acum, acum_t, e_all, w_all, cd, cb_g, b_g, c_g, hprev, dsk_ref)
                h_scr[p] = hprev * f["cdcol"] + f["s2"]
                z2 = z_ref[:, LANES * p:LANES * (p + 1)].astype(F32)
                ygs.append(f["y2"] * z2 * _sigmoid(z2))
            yg = jnp.concatenate(ygs, axis=1)
            r = lax.rsqrt(jnp.mean(yg * yg, axis=1, keepdims=True) + EPS)
            ya_ref[:, 2 * LANES * g:2 * LANES * (g + 1)] = (yg * r * nw_ref[:, 2 * LANES * g:2 * LANES * (g + 1)]).astype(BF16)

    return pl.pallas_call(
        body, name=f"ssm_fwd_{li}", grid=(B, nc),
        in_specs=[prev, cur, zed, psb, const(4, 2 * D), const(1, 2 * D), const(1, LANES), const(1, LANES),
                  const(1, D), const(1, D)],
        out_specs=[row, hpb],
        out_shape=[jax.ShapeDtypeStruct((T, D), BF16), jax.ShapeDtypeStruct((B, nc, NH // 2, LANES, NST), F32)],
        scratch_shapes=[pltpu.VMEM((NH // 2, LANES, NST), F32), pltpu.VMEM((HALO + LCH, 2 * D), F32)],
        compiler_params=_cparams(("arbitrary", "arbitrary"), VMEM_LIMIT),
    )(proj, proj, proj, ps, cw, cb, dtb, alog, dsk, nw)


def _ssm_bwd(proj, ps, hp, dya, cw, cb, dtb, alog, dsk, nw, S, li):
    T = proj.shape[0]
    B = T // S
    nc, prev, cur, zed, row, psb, hpb, const = _ssm_specs(S, True)

    def body(prev_ref, cur_ref, z_ref, ps_ref, hp_ref, dy_ref, cw_ref, cb_ref, dtb_ref, alog_ref, dsk_ref, nw_ref,
             dxbc_ref, dz_ref, dps_ref, pgw_ref, pg1_ref, pgh_ref, dh_scr, ext, extd, dact):
        b, cc = pl.program_id(0), pl.program_id(1)
        c = nc - 1 - cc

        @pl.when(jnp.logical_and(b == 0, cc == 0))
        def _():
            pgw_ref[...] = jnp.zeros_like(pgw_ref)
            pg1_ref[...] = jnp.zeros_like(pg1_ref)
            pgh_ref[...] = jnp.zeros_like(pgh_ref)

        @pl.when(cc == 0)
        def _():
            dh_scr[...] = jnp.zeros_like(dh_scr)
            extd[LCH:LCH + HALO, :] = jnp.zeros((HALO, 2 * D), F32)

        ext[0:HALO, :] = prev_ref[...].astype(F32) * (c > 0).astype(F32)
        ext[HALO:HALO + LCH, :] = cur_ref[...].astype(F32)
        psv = ps_ref[...]
        pre, sg, dt, a, acum = _ssm_chunk_pre(ext, cw_ref, cb_ref, psv, dtb_ref[...], alog_ref[...])
        act = pre * sg
        acum_t = acum.T
        e_all = jnp.exp(acum)
        last = acum[LCH - 1:LCH, :]
        w_all = jnp.exp(last - acum)
        cd = jnp.exp(last)
        x = act[:, :D]
        lane = _lane_iota()
        m0 = lane < HD
        rowsel = lax.broadcasted_iota(jnp.int32, (LANES, 1), 0) < HD
        is_last_row = lax.broadcasted_iota(jnp.int32, (LCH, 1), 0) == LCH - 1
        dacum_all = jnp.zeros((LCH, LANES), F32)
        ddt_all = jnp.zeros((LCH, LANES), F32)
        dd_row = jnp.zeros((1, LANES), F32)
        for g in range(NGRP):
            b_g = act[:, D + NST * g:D + NST * (g + 1)].astype(BF16)
            c_g = act[:, D + NGRP * NST + NST * g:D + NGRP * NST + NST * (g + 1)].astype(BF16)
            cb_g = _dot_nt(c_g, b_g)
            pairs = (2 * g, 2 * g + 1)
            fs, hps, zs, ygs = [], [], [], []
            for p in pairs:
                hprev = hp_ref[0, 0, p]
                f = _ssm_pair_fwd(p, x, dt, acum, acum_t, e_all, w_all, cd, cb_g, b_g, c_g, hprev, dsk_ref)
                z2 = z_ref[:, LANES * p:LANES * (p + 1)].astype(F32)
                fs.append(f)
                hps.append(hprev)
                zs.append(z2)
                ygs.append(f["y2"] * z2 * _sigmoid(z2))
            gl = slice(2 * LANES * g, 2 * LANES * (g + 1))
            yg = jnp.concatenate(ygs, axis=1)
            r = lax.rsqrt(jnp.mean(yg * yg, axis=1, keepdims=True) + EPS)
            dyn = dy_ref[:, gl].astype(F32)
            gg = dyn * nw_ref[:, gl]
            dyg = r * gg - yg * (r * r * r) * jnp.mean(gg * yg, axis=1, keepdims=True)
            pg1_ref[0:1, gl] += jnp.sum(dyn * yg * r, axis=0, keepdims=True)
            dg_g = jnp.zeros((LCH, LCH), F32)
            db_g = jnp.zeros((LCH, NST), F32)
            dc_g = jnp.zeros((LCH, NST), F32)
            for idx, p in enumerate(pairs):
                f, hprev, z2 = fs[idx], hps[idx], zs[idx]
                lanes = slice(LANES * p, LANES * (p + 1))
                dyg2 = dyg[:, LANES * idx:LANES * (idx + 1)]
                sgz = _sigmoid(z2)
                dy2 = dyg2 * z2 * sgz
                dz_ref[:, lanes] = (dyg2 * f["y2"] * sgz * (1.0 + z2 * (1.0 - sgz))).astype(BF16)
                x2, dt2, xdt2, xdtb, w2, e2, z2m = f["x2"], f["dt2"], f["xdt2"], f["xdtb"], f["w2"], f["e2"], f["z2"]
                dx2 = dsk_ref[:, lanes] * dy2
                dyx = dy2 * x2
                dxdt2 = jnp.zeros((LCH, LANES), F32)
                diag_cols = []
                for hh in range(2):
                    sel = m0 if hh == 0 else jnp.logical_not(m0)
                    dyb = jnp.where(sel, dy2, 0.0).astype(BF16)
                    dm = _dot_nt(dyb, xdtb)
                    dg_g = dg_g + dm * f["lms"][hh]
                    dxdt2 = dxdt2 + _dot_tn(f["ms"][hh].astype(BF16), dyb)
                    em = dm * f["ms"][hh]
                    diag_cols.append(jnp.sum(em, axis=1, keepdims=True) - jnp.sum(em.T, axis=1, keepdims=True))
                dz2m = dy2 * e2
                t_off = dz2m * z2m
                dc_g = dc_g + _dot(dz2m.astype(BF16), hprev.astype(BF16))
                dhprev = _dot_tn(dz2m.astype(BF16), c_g)
                dhn = dh_scr[p]
                dhnb = dhn.astype(BF16)
                dhprev = dhprev + dhn * f["cdcol"]
                t_h = dhn * hprev
                dxw2 = _dot_nt(b_g, dhnb)
                db_g = db_g + _dot(f["xw"], dhnb)
                dxdt2 = dxdt2 + dxw2 * w2
                t_w = dxw2 * xdt2
                dx2 = dx2 + dxdt2 * dt2
                t_dt = dxdt2 * x2
                for hh in range(2):
                    h = 2 * p + hh
                    onehot = (lane == h).astype(F32)
                    w_col = w_all[:, h:h + 1]
                    dw_col = _head_sum(t_w, hh) * w_col
                    rs = rowsel if hh == 0 else jnp.logical_not(rowsel)
                    dlast = (jnp.sum(jnp.where(rs, t_h, 0.0), keepdims=True) * cd[:, h:h + 1]
                             + jnp.sum(dw_col, keepdims=True))
                    dacum_col = diag_cols[hh] + _head_sum(t_off, hh) - dw_col + jnp.where(is_last_row, dlast, 0.0)
                    dacum_all = dacum_all + dacum_col * onehot
                    ddt_all = ddt_all + _head_sum(t_dt, hh) * onehot
                    sel = m0 if hh == 0 else jnp.logical_not(m0)
                    dd_row = dd_row + jnp.sum(jnp.where(sel, dyx, 0.0), keepdims=True) * onehot
                dh_scr[p] = dhprev
                dact[:, lanes] = dx2
            dgb = dg_g.astype(BF16)
            dc_g = dc_g + _dot(dgb, b_g)
            db_g = db_g + _dot_tn(dgb, c_g)
            dact[:, D + NST * g:D + NST * (g + 1)] = db_g
            dact[:, D + NGRP * NST + NST * g:D + NGRP * NST + NST * (g + 1)] = dc_g
        rr = lax.broadcasted_iota(jnp.int32, (LCH, LCH), 0)
        cc2 = lax.broadcasted_iota(jnp.int32, (LCH, LCH), 1)
        dadt = _dot_hi((cc2 >= rr).astype(F32), dacum_all)
        ddt_all = ddt_all + dadt * a
        heads = lane < NH
        da = jnp.sum(dadt * dt, axis=0, keepdims=True)
        dr = jnp.where(heads, ddt_all * _sigmoid(psv + dtb_ref[...]), 0.0)
        dps_ref[...] = dr
        pgh_ref[0:1, :] += jnp.sum(dr, axis=0, keepdims=True)
        pgh_ref[1:2, :] += jnp.where(heads, da * a, 0.0)
        pgh_ref[2:3, :] += dd_row
        dpre = dact[...] * sg * (1.0 + pre * (1.0 - sg))
        extd[0:LCH, :] = dpre
        du = jnp.zeros((LCH, 2 * D), F32)
        for k in range(4):
            du = du + cw_ref[k:k + 1, :] * extd[pl.ds(3 - k, LCH), :]
            pgw_ref[k:k + 1, :] += jnp.sum(dpre * ext[pl.ds(HALO - 3 + k, LCH), :], axis=0, keepdims=True)
        pgw_ref[4:5, :] += jnp.sum(dpre, axis=0, keepdims=True)
        dxbc_ref[...] = du.astype(BF16)
        extd[LCH:LCH + HALO, :] = dpre[0:HALO, :]

    xbc_out = pl.BlockSpec((LCH, 2 * D), lambda b, c: (b * nc + nc - 1 - c, 0))
    acc = lambda w: pl.BlockSpec((8, w), lambda b, c: (0, 0))
    return pl.pallas_call(
        body, name=f"ssm_bwd_{li}", grid=(B, nc),
        in_specs=[prev, cur, zed, psb, hpb, row, const(4, 2 * D), const(1, 2 * D), const(1, LANES), const(1, LANES),
                  const(1, D), const(1, D)],
        out_specs=[xbc_out, row, psb, acc(2 * D), acc(D), acc(LANES)],
        out_shape=[jax.ShapeDtypeStruct((T, 2 * D), BF16), jax.ShapeDtypeStruct((T, D), BF16),
                   jax.ShapeDtypeStruct((T, LANES), F32), jax.ShapeDtypeStruct((8, 2 * D), F32),
                   jax.ShapeDtypeStruct((8, D), F32), jax.ShapeDtypeStruct((8, LANES), F32)],
        scratch_shapes=[pltpu.VMEM((NH // 2, LANES, NST), F32), pltpu.VMEM((HALO + LCH, 2 * D), F32),
                        pltpu.VMEM((LCH + HALO, 2 * D), F32), pltpu.VMEM((LCH, 2 * D), F32)],
        compiler_params=_cparams(("arbitrary", "arbitrary"), VMEM_LIMIT),
    )(proj, proj, proj, ps, hp, dya, cw, cb, dtb, alog, dsk, nw)


def _lane_row(v, offset):
    return jnp.pad(v.astype(F32), (offset, LANES - offset - v.shape[0]))[None]


def _pack_rows(arrays):
    parts = []
    for a in arrays:
        flat = a.reshape(-1).astype(F32)
        pad = (-flat.shape[0]) % LANES
        parts.append(jnp.pad(flat, (0, pad)))
    flat = jnp.concatenate(parts)
    pad = (-flat.shape[0]) % (8 * LANES)
    return jnp.pad(flat, (0, pad)).reshape(-1, LANES)


def _unpack_rows(pack, shapes):
    flat = pack.reshape(-1)
    out, pos = [], 0
    for shp in shapes:
        n = math.prod(shp)
        out.append(flat[pos:pos + n].reshape(shp))
        pos += n + (-n) % LANES
    return out


def _split_w_in(w):
    main = jnp.concatenate([w[:, 0:3072], w[:, 3088:4112], w[:, 4624:5648], w[:, 5648:8720], w[:, 8736:12832],
                            w[:, 4112:4624]], axis=1)
    small = jnp.concatenate([w[:, 3072:3088], w[:, 8720:8736], jnp.zeros((D, LANES - 2 * NH), w.dtype)], axis=1)
    return main, small


def _join_w_in(dm, ds):
    return jnp.concatenate([dm[:, 0:3072], ds[:, 0:NH], dm[:, 3072:4096], dm[:, 12288:12800], dm[:, 4096:5120],
                            dm[:, 5120:8192], ds[:, NH:2 * NH], dm[:, 8192:12288]], axis=1)


def kernel(x, norm_w, w_in, conv_w, conv_b, dt_bias, a_log, d_skip, ssm_norm_w, sinks, f_bias, gate_bias, w_proj, w_out, final_norm_w, loss_target, m_norm_w, m_w_in, m_conv_w, m_conv_b, m_dt_bias, m_a_log, m_d_skip, m_ssm_norm_w, m_sinks, m_f_bias, m_gate_bias, m_w_proj, m_w_out, m_final_norm_w, v_norm_w, v_w_in, v_conv_w, v_conv_b, v_dt_bias, v_a_log, v_d_skip, v_ssm_norm_w, v_sinks, v_f_bias, v_gate_bias, v_w_proj, v_w_out, v_final_norm_w):
    Bl, S, _ = x.shape
    T = Bl * S
    depth = norm_w.shape[0]
    me = 4 * lax.axis_index("x") + 2 * lax.axis_index("y") + lax.axis_index("c")
    csh, gsh = conv_w.shape[2], gate_bias.shape[2]

    small_pack = jnp.concatenate([conv_w.reshape(depth, -1), gate_bias.reshape(depth, -1)], axis=1)
    g_win, g_wp, g_wo, g_small = _all_gather(
        [w_in.astype(BF16), w_proj.astype(BF16), w_out.astype(BF16), small_pack], "gather_weights")
    win_full = g_win.transpose(1, 2, 0, 3).reshape(depth, D, NIN)
    wp_full = g_wp.transpose(1, 2, 0, 3, 4).reshape(depth, 3, D, D)
    wo_full = g_wo.transpose(1, 0, 2, 3).reshape(depth, D, D)
    cw_full = g_small[:, :, :4 * csh].reshape(NDEV, depth, 4, csh).transpose(1, 2, 0, 3).reshape(depth, 4, 2 * D)
    gb_full = g_small[:, :, 4 * csh:].reshape(NDEV, depth, 3, gsh).transpose(1, 2, 0, 3).reshape(depth, 3, D)

    pos = jnp.arange(S, dtype=F32)
    inv_freq = ROPE_THETA ** (-jnp.arange(0, HD, 2, dtype=F32) / HD)
    ang = pos[:, None] * inv_freq[None, :]
    cos128 = jnp.tile(jnp.cos(ang), (1, 4))
    sign = jnp.where((jnp.arange(LANES) % HD) < HD // 2, -1.0, 1.0).astype(F32)
    sin128 = jnp.tile(jnp.sin(ang), (1, 4)) * sign[None, :]

    bq, nq = _fox_blocks(S)
    x2 = x.reshape(T, D)
    tgt2 = loss_target.reshape(T, D)

    saved = []
    xcur = x2
    for l in range(depth):
        wmain, wsmall = _split_w_in(win_full[l])
        proj, ps, h16 = _inproj_fwd(xcur, norm_w[l][None], wmain, wsmall, cos128, sin128, S, l)
        dtb = _lane_row(dt_bias[l], 0)
        alog = _lane_row(a_log[l], 0)
        fb = _lane_row(f_bias[l], NH)
        dsk = jnp.repeat(d_skip[l], HD)[None]
        ya, hp = _ssm_fwd(proj, ps, cw_full[l], conv_b[l][None], dtb, alog, dsk, ssm_norm_w[l][None], S, l)
        yb, ob, lse_b = _swa_fwd(proj, sinks[l], S, l)
        cum = _fox_cum(ps, fb, S, l)
        cumh = cum[:, NH:2 * NH].reshape(Bl, S, NH).transpose(0, 2, 1)
        cum_col = cumh[..., None]
        cum_row = cumh.reshape(Bl, NH, nq, 1, bq)
        yc, oc, lse_c = _fox_fwd(proj, cum_col, cum_row, S, l)
        xnext, br = _merge_fwd(ya, yb, yc, proj, gb_full[l], wp_full[l], wo_full[l], xcur, l)
        saved.append(dict(x=xcur, wmain=wmain, wsmall=wsmall, proj=proj, ps=ps, h16=h16, dtb=dtb, alog=alog, fb=fb,
                          dsk=dsk, ya=ya, hp=hp, yb=yb, ob=ob, lse_b=lse_b, cum_col=cum_col, cum_row=cum_row,
                          yc=yc, oc=oc, lse_c=lse_c, br=br))
        xcur = xnext

    dx, dx16, st = _final_loss(xcur, tgt2, final_norm_w[None])
    loss_part = st[2, 0]
    g_final = st[0]

    gsm = {k: [None] * depth for k in ("norm_w", "conv_w", "conv_b", "dt_bias", "a_log", "d_skip", "ssm_norm_w",
                                      "sinks", "f_bias", "gate_bias")}
    g_win_l, g_wp_l, g_wo_l = [None] * depth, [None] * depth, [None] * depth
    for l in reversed(range(depth)):
        sv = saved[l]
        proj, ps = sv["proj"], sv["ps"]
        dbr, dgates, merged16, dgb = _merge_bwd_gates(dx16, wo_full[l].T, sv["br"], proj, gb_full[l], l)
        g_wo_l[l] = _matmul(merged16.T, dx16, F32, f"dwout_{l}")
        dys, dwps = [], []
        for i, y in enumerate((sv["ya"], sv["yb"], sv["yc"])):
            dys.append(_matmul(dbr[i], wp_full[l, i].T, BF16, f"dy_{l}_{i}"))
            dwps.append(_matmul(y.T, dbr[i], F32, f"dwproj_{l}_{i}"))
        g_wp_l[l] = jnp.stack(dwps)
        gsm["gate_bias"][l] = dgb[0:3]
        dxbc, daz, dps_a, pgw, pg1, pgh = _ssm_bwd(proj, ps, sv["hp"], dys[0], cw_full[l], conv_b[l][None], sv["dtb"],
                                                   sv["alog"], sv["dsk"], ssm_norm_w[l][None], S, l)
        gsm["conv_w"][l], gsm["conv_b"][l] = pgw[0:4], pgw[4]
        gsm["ssm_norm_w"][l] = pg1[0]
        gsm["dt_bias"][l], gsm["a_log"][l], gsm["d_skip"][l] = pgh[0, :NH], pgh[1, :NH], pgh[2, :NH]
        do_b, dbz = _zgate_bwd(dys[1], sv["ob"], proj, OFF_BZ // D, f"zgate_bwd_swa_{l}")
        dq_b, dsk_b = _swa_bwd_dq(proj, do_b, sv["ob"], sv["lse_b"], sinks[l], cos128, sin128, S, l)
        dk_b, dv_b = _swa_bwd_dkv(proj, do_b, sv["ob"], sv["lse_b"], cos128, sin128, S, l)
        gsm["sinks"][l] = dsk_b[:, :, 0].reshape(NH)
        do_c, dcz = _zgate_bwd(dys[2], sv["oc"], proj, OFF_CZ // D, f"zgate_bwd_fox_{l}")
        dq_c, dk_c, dv_c, dcum_k, dcum_q = _fox_bwd(proj, do_c, sv["oc"], sv["cum_col"], sv["cum_row"], sv["lse_c"], S, l)
        dcum_tm = (dcum_k.reshape(Bl, NH, S) + dcum_q.reshape(Bl, NH, S)).transpose(0, 2, 1).reshape(T, NH)
        dcum_pad = jnp.pad(dcum_tm, ((0, 0), (NH, LANES - 2 * NH)))
        df, dfb = _fox_cum_bwd(dcum_pad, ps, sv["fb"], S, l)
        gsm["f_bias"][l] = dfb[0, NH:2 * NH]
        dps16 = (dps_a + df).astype(BF16)
        dproj = jnp.concatenate([dxbc, daz, dq_b, dbz, dq_c, dk_c, dv_c, dcz, dgates, dk_b, dv_b], axis=1)
        h_t = sv["h16"].T
        dwm = _matmul(h_t, dproj, F32, f"dwin_main_{l}", tm=1024, tn=1280, tk=512)
        dws = _matmul(h_t, dps16, F32, f"dwin_small_{l}")
        g_win_l[l] = _join_w_in(dwm, dws)
        dx, dx16, dnw = _inproj_bwd_dx(dproj, sv["wmain"].T, dps16, sv["wsmall"].T, sv["x"], norm_w[l][None], dx, l)
        gsm["norm_w"][l] = dnw[0]

    p_win = jnp.stack(g_win_l).reshape(depth, D, NDEV, NSH).transpose(2, 0, 1, 3)
    p_wp = jnp.stack(g_wp_l).reshape(depth, 3, NDEV, D // NDEV, D).transpose(2, 0, 1, 3, 4)
    p_wo = jnp.stack(g_wo_l).reshape(depth, NDEV, D // NDEV, D).transpose(1, 0, 2, 3)
    r_win, r_wp, r_wo = _scatter_blocks([p_win, p_wp, p_wo], "scatter_grads")
    big = {}
    for name, parts, w, m, v in (("w_in", r_win, w_in, m_w_in, v_w_in), ("w_proj", r_wp, w_proj, m_w_proj, v_w_proj),
                                 ("w_out", r_wo, w_out, m_w_out, v_w_out)):
        cols = w.shape[-1]
        res = _sum_adamw(parts.reshape(NDEV, -1, cols), w.reshape(-1, cols), m.reshape(-1, cols), v.reshape(-1, cols),
                         f"adamw_{name}")
        big[name] = [r.reshape(w.shape) for r in res]

    small_names = ("norm_w", "conv_b", "dt_bias", "a_log", "d_skip", "ssm_norm_w", "sinks", "f_bias")
    small_parts = [jnp.stack(gsm[k]) for k in small_names] + [g_final, jnp.stack(gsm["conv_w"]),
                                                              jnp.stack(gsm["gate_bias"]), loss_part.reshape(1)]
    shapes = [a.shape for a in small_parts]
    summed = _unpack_rows(_all_reduce_small(_pack_rows(small_parts)), shapes)
    g_small = dict(zip(small_names, summed[:len(small_names)]))
    g_small["final_norm_w"] = summed[len(small_names)]
    g_small["conv_w"] = lax.dynamic_slice_in_dim(summed[len(small_names) + 1], me * csh, csh, axis=2)
    g_small["gate_bias"] = lax.dynamic_slice_in_dim(summed[len(small_names) + 2], me * gsh, gsh, axis=2)
    loss = summed[len(small_names) + 3][0]

    ws = dict(norm_w=norm_w, conv_w=conv_w, conv_b=conv_b, dt_bias=dt_bias, a_log=a_log, d_skip=d_skip,
              ssm_norm_w=ssm_norm_w, sinks=sinks, f_bias=f_bias, gate_bias=gate_bias, final_norm_w=final_norm_w)
    ms = dict(norm_w=m_norm_w, conv_w=m_conv_w, conv_b=m_conv_b, dt_bias=m_dt_bias, a_log=m_a_log, d_skip=m_d_skip,
              ssm_norm_w=m_ssm_norm_w, sinks=m_sinks, f_bias=m_f_bias, gate_bias=m_gate_bias,
              final_norm_w=m_final_norm_w)
    vs = dict(norm_w=v_norm_w, conv_w=v_conv_w, conv_b=v_conv_b, dt_bias=v_dt_bias, a_log=v_a_log, d_skip=v_d_skip,
              ssm_norm_w=v_ssm_norm_w, sinks=v_sinks, f_bias=v_f_bias, gate_bias=v_gate_bias,
              final_norm_w=v_final_norm_w)
    order = list(ws)
    oshapes = [ws[k].shape for k in order]
    res = _adamw_small(_pack_rows([g_small[k] for k in order]), _pack_rows([ws[k] for k in order]),
                       _pack_rows([ms[k] for k in order]), _pack_rows([vs[k] for k in order]))
    d_s, m_s, v_s = (dict(zip(order, _unpack_rows(r, oshapes))) for r in res)

    names = ("norm_w", "w_in", "conv_w", "conv_b", "dt_bias", "a_log", "d_skip", "ssm_norm_w", "sinks", "f_bias",
             "gate_bias", "w_proj", "w_out", "final_norm_w")
    grads, deltas, new_m, new_v = [], [], [], []
    for k in names:
        if k in big:
            g, d_, m_, v_ = big[k]
        else:
            g, d_, m_, v_ = g_small[k], d_s[k], m_s[k], v_s[k]
        grads.append(g)
        deltas.append(d_)
        new_m.append(m_)
        new_v.append(v_)
    return (loss, dx.reshape(Bl, S, D), *grads, *deltas, *new_m, *new_v)
```

```python
import functools
import math

import jax
import jax.numpy as jnp
from jax import lax
from jax.experimental import pallas as pl
from jax.experimental.pallas import tpu as pltpu

F32 = jnp.float32
BF16 = jnp.bfloat16
MESH = pl.DeviceIdType.MESH
NDEV = 8

D = 1024
NH = 16
HD = 64
NST = 128
NGRP = 4
LCH = 128
EPS = 1e-6
ROPE_THETA = 10000.0
SCALE = HD ** -0.5
NEG = -1e30

LANES = 128
VMEM_LIMIT = 56 * 1024 * 1024

OFF_XBC, OFF_AZ, OFF_BQ, OFF_BZ, OFF_CQ, OFF_CK, OFF_CV, OFF_CZ, OFF_G, OFF_BK, OFF_BV = (
    0, 2048, 3072, 4096, 5120, 6144, 7168, 8192, 9216, 12288, 12544)
NMAIN = 12800
NIN = 12832
NSH = NIN // NDEV

ADAM_LR, ADAM_B1, ADAM_B2, ADAM_EPS, ADAM_WD, ADAM_STEP = 0.001, 0.9, 0.999, 1e-08, 0.01, 10


def _cparams(dims=None, vmem=None):
    return pltpu.CompilerParams(dimension_semantics=dims, vmem_limit_bytes=vmem)


def _dot(a, b):
    return jnp.dot(a, b, preferred_element_type=F32)


def _dot_nt(a, b):
    return lax.dot_general(a, b, (((1,), (1,)), ((), ())), preferred_element_type=F32)


def _dot_tn(a, b):
    return lax.dot_general(a, b, (((0,), (0,)), ((), ())), preferred_element_type=F32)


def _dot_hi(a, b):
    return jnp.dot(a, b, precision=lax.Precision.HIGHEST, preferred_element_type=F32)


def _sigmoid(x):
    return 1.0 / (1.0 + jnp.exp(-x))


def _softplus(x):
    return jnp.maximum(x, 0.0) + jnp.log(1.0 + jnp.exp(-jnp.abs(x)))


def _lane_iota(n=LANES):
    return lax.broadcasted_iota(jnp.int32, (1, n), 1)


def _rot_half(x):
    first = (_lane_iota() % HD) < (HD // 2)
    return jnp.where(first, pltpu.roll(x, LANES - HD // 2, 1), pltpu.roll(x, HD // 2, 1))


def _head_sum(x, head):
    m = (_lane_iota() < HD) if head == 0 else (_lane_iota() >= HD)
    return jnp.sum(jnp.where(m, x, 0.0), axis=1, keepdims=True)


def _me_and_peers():
    x, y, c = lax.axis_index("x"), lax.axis_index("y"), lax.axis_index("c")
    me = 4 * x + 2 * y + c
    peers = []
    for k in range(1, NDEV):
        kx, ky, kc = (k >> 2) & 1, (k >> 1) & 1, k & 1
        px, py, pc = x ^ kx, y ^ ky, c ^ kc
        peers.append(((px, py, pc), 4 * px + 2 * py + pc))
    return me, peers


class _Comm:
    def __init__(self, kind, arrays):
        self.kind, self.arrays, self.n = kind, list(arrays), len(arrays)
        any_spec = pl.BlockSpec(memory_space=pl.ANY)
        self.in_specs = [any_spec] * self.n
        self.out_specs = [any_spec] * self.n
        self.out_shape = [jax.ShapeDtypeStruct(((NDEV,) + a.shape) if kind == "gather" else a.shape, a.dtype)
                          for a in self.arrays]
        self.scratch = [pltpu.SemaphoreType.DMA((self.n, NDEV - 1)), pltpu.SemaphoreType.DMA((self.n, NDEV - 1)),
                        pltpu.SemaphoreType.DMA((self.n,))]

    def copies(self, ins, outs, sems):
        send_sems, recv_sems, local_sems = sems
        me, peers = _me_and_peers()
        out = []
        for a in range(self.n):
            mine = ins[a] if self.kind == "gather" else ins[a].at[me]
            out.append(pltpu.make_async_copy(mine, outs[a].at[me], local_sems.at[a]))
            for k, (peer, pidx) in enumerate(peers):
                src = ins[a] if self.kind == "gather" else ins[a].at[pidx]
                out.append(pltpu.make_async_remote_copy(
                    src_ref=src, dst_ref=outs[a].at[me], send_sem=send_sems.at[a, k], recv_sem=recv_sems.at[a, k],
                    device_id=peer, device_id_type=MESH))
        return out

    def call(self, name):
        def body(*refs):
            cps = self.copies(refs[:self.n], refs[self.n:2 * self.n], refs[2 * self.n:])
            for cp in cps:
                cp.start()
            for cp in cps:
                cp.wait()

        return pl.pallas_call(body, name=name, out_shape=self.out_shape, in_specs=self.in_specs,
                              out_specs=self.out_specs, scratch_shapes=self.scratch)(*self.arrays)


def _hosted_call(body, comm, name, grid, in_specs, out_specs, out_shape, scratch, dims, operands):
    if comm is None:
        return pl.pallas_call(body, name=name, grid=grid, in_specs=in_specs, out_specs=out_specs, out_shape=out_shape,
                              scratch_shapes=scratch, compiler_params=_cparams(dims, VMEM_LIMIT))(*operands)
    n_in, n_out, n_scr, n = len(in_specs), len(out_specs), len(scratch), comm.n

    def hosted(*refs):
        hin, cin = refs[:n_in], refs[n_in:n_in + n]
        hout = refs[n_in + n:n_in + n + n_out]
        cout = refs[n_in + n + n_out:n_in + 2 * n + n_out]
        hscr = refs[n_in + 2 * n + n_out:n_in + 2 * n + n_out + n_scr]
        sems = refs[n_in + 2 * n + n_out + n_scr:]
        ids = [pl.program_id(a) for a in range(len(grid))]
        first = functools.reduce(jnp.logical_and, [i == 0 for i in ids])
        last = functools.reduce(jnp.logical_and, [i == g - 1 for i, g in zip(ids, grid)])

        @pl.when(first)
        def _():
            for cp in comm.copies(cin, cout, sems):
                cp.start()

        body(*hin, *hout, *hscr)

        @pl.when(last)
        def _():
            for cp in comm.copies(cin, cout, sems):
                cp.wait()

    return pl.pallas_call(
        hosted, name=name, grid=grid, in_specs=list(in_specs) + comm.in_specs,
        out_specs=list(out_specs) + comm.out_specs, out_shape=list(out_shape) + comm.out_shape,
        scratch_shapes=list(scratch) + comm.scratch,
        compiler_params=_cparams(("arbitrary",) * len(grid), VMEM_LIMIT))(*operands, *comm.arrays)


def _all_reduce_small(v):
    rows = v.shape[0]

    def body(v_ref, sum_ref, all_ref, send_sems, recv_sems):
        me, peers = _me_and_peers()
        all_ref[me] = v_ref[...]
        copies = []
        for k, (peer, _) in enumerate(peers):
            cp = pltpu.make_async_remote_copy(
                src_ref=v_ref, dst_ref=all_ref.at[me],
                send_sem=send_sems.at[k], recv_sem=recv_sems.at[k],
                device_id=peer, device_id_type=MESH)
            cp.start()
            copies.append(cp)
        for cp in copies:
            cp.wait()
        acc = all_ref[0]
        for d in range(1, NDEV):
            acc = acc + all_ref[d]
        sum_ref[...] = acc

    vm = pl.BlockSpec(memory_space=pltpu.VMEM)
    return pl.pallas_call(
        body, name="all_reduce_small",
        out_shape=jax.ShapeDtypeStruct((rows, LANES), F32),
        in_specs=[vm], out_specs=vm,
        scratch_shapes=[pltpu.VMEM((NDEV, rows, LANES), F32),
                        pltpu.SemaphoreType.DMA((NDEV - 1,)), pltpu.SemaphoreType.DMA((NDEV - 1,))],
    )(v)


def _adamw_math(w, g, m, v):
    m = ADAM_B1 * m + (1.0 - ADAM_B1) * g
    v = ADAM_B2 * v + (1.0 - ADAM_B2) * jnp.square(g)
    m_hat = m / (1.0 - ADAM_B1 ** ADAM_STEP)
    v_hat = v / (1.0 - ADAM_B2 ** ADAM_STEP)
    delta = -ADAM_LR * (m_hat / (jnp.sqrt(v_hat) + ADAM_EPS) + ADAM_WD * w)
    return delta, m, v


def _sum_adamw(parts, w, m, v, name):
    depth, rows, cols = w.shape
    tr = next(c for c in (256, 128, 64, 32, 16) if rows % c == 0)
    nb = rows // tr

    def body(*refs):
        p_refs, (w_ref, m_ref, v_ref, g_ref, d_ref, nm_ref, nv_ref) = refs[:depth], refs[depth:]
        l = pl.program_id(0)
        for ll in range(depth):
            @pl.when(l == ll)
            def _(ll=ll):
                g = p_refs[ll][0].astype(F32)
                for d in range(1, NDEV):
                    g = g + p_refs[ll][d].astype(F32)
                delta, nm, nv = _adamw_math(w_ref[0], g, m_ref[0], v_ref[0])
                g_ref[0] = g
                d_ref[0] = delta
                nm_ref[0] = nm
                nv_ref[0] = nv

    part = lambda ll: pl.BlockSpec((NDEV, tr, cols), lambda l, i, ll=ll: (0, jnp.where(l == ll, i, jnp.where(l < ll, 0, nb - 1)), 0))
    blk = pl.BlockSpec((1, tr, cols), lambda l, i: (l, i, 0))
    sds = jax.ShapeDtypeStruct((depth, rows, cols), F32)
    return pl.pallas_call(
        body, name=name, grid=(depth, nb),
        in_specs=[part(ll) for ll in range(depth)] + [blk, blk, blk],
        out_specs=[blk, blk, blk, blk], out_shape=[sds, sds, sds, sds],
        compiler_params=_cparams(("arbitrary", "arbitrary"), VMEM_LIMIT),
    )(*parts, w, m, v)


def _adamw_small(g, w, m, v):
    def body(g_ref, w_ref, m_ref, v_ref, d_ref, nm_ref, nv_ref):
        delta, nm, nv = _adamw_math(w_ref[...], g_ref[...], m_ref[...], v_ref[...])
        d_ref[...] = delta
        nm_ref[...] = nm
        nv_ref[...] = nv

    sds = jax.ShapeDtypeStruct(g.shape, F32)
    return pl.pallas_call(body, name="adamw_small", out_shape=[sds, sds, sds])(g, w, m, v)


def _matmul(a, b, out_dtype, name, tm=1024, tn=1024, tk=512):
    M, K = a.shape
    N = b.shape[1]
    tm, tn, tk = min(tm, M), min(tn, N), min(tk, K)
    nk = K // tk

    def body(a_ref, b_ref, o_ref, acc):
        k = pl.program_id(2)

        @pl.when(k == 0)
        def _():
            acc[...] = jnp.zeros_like(acc)

        acc[...] += _dot(a_ref[...], b_ref[...])

        @pl.when(k == nk - 1)
        def _():
            o_ref[...] = acc[...].astype(out_dtype)

    return pl.pallas_call(
        body, name=name, grid=(M // tm, N // tn, nk),
        in_specs=[pl.BlockSpec((tm, tk), lambda i, j, k: (i, k)), pl.BlockSpec((tk, tn), lambda i, j, k: (k, j))],
        out_specs=pl.BlockSpec((tm, tn), lambda i, j, k: (i, j)),
        out_shape=jax.ShapeDtypeStruct((M, N), out_dtype),
        scratch_shapes=[pltpu.VMEM((tm, tn), F32)],
        compiler_params=_cparams(("parallel", "parallel", "arbitrary"), VMEM_LIMIT),
    )(a, b)


def _inproj_fwd(x2, nw, wmain, wsmall, cos128, sin128, S, li, comm=None):
    T = x2.shape[0]
    tm, tn = min(1024, S), 512
    nj, npos = NMAIN // tn, S // tm
    jq0, jk = OFF_BQ // tn, OFF_BK // tn

    def body(x_ref, nw_ref, w_ref, ws_ref, cos_ref, sin_ref, proj_ref, ps_ref, h_ref, h_scr):
        j = pl.program_id(1)

        @pl.when(j == 0)
        def _():
            x = x_ref[...]
            r = lax.rsqrt(jnp.mean(x * x, axis=-1, keepdims=True) + EPS)
            h = (x * r * nw_ref[...]).astype(BF16)
            h_scr[...] = h
            h_ref[...] = h
            ps_ref[...] = _dot(h, ws_ref[...])

        acc = _dot(h_scr[...], w_ref[...])

        def roped(c):
            xc = acc[:, LANES * c:LANES * (c + 1)]
            return (xc * cos_ref[...] + _rot_half(xc) * sin_ref[...]).astype(BF16)

        def plain(c):
            return acc[:, LANES * c:LANES * (c + 1)].astype(BF16)

        is_q = jnp.logical_or(j == jq0, j == jq0 + 1)
        is_k = j == jk

        @pl.when(is_q)
        def _():
            for c in range(4):
                proj_ref[:, LANES * c:LANES * (c + 1)] = roped(c)

        @pl.when(is_k)
        def _():
            for c in range(4):
                proj_ref[:, LANES * c:LANES * (c + 1)] = roped(c) if c < 2 else plain(c)

        @pl.when(jnp.logical_not(jnp.logical_or(is_q, is_k)))
        def _():
            proj_ref[...] = acc.astype(BF16)

    return _hosted_call(
        body, comm, f"inproj_fwd_{li}", (T // tm, nj),
        in_specs=[pl.BlockSpec((tm, D), lambda i, j: (i, 0)),
                  pl.BlockSpec((1, D), lambda i, j: (0, 0)),
                  pl.BlockSpec((D, tn), lambda i, j: (0, j)),
                  pl.BlockSpec((D, LANES), lambda i, j: (0, 0)),
                  pl.BlockSpec((tm, LANES), lambda i, j: (i % npos, 0)),
                  pl.BlockSpec((tm, LANES), lambda i, j: (i % npos, 0))],
        out_specs=[pl.BlockSpec((tm, tn), lambda i, j: (i, j)),
                   pl.BlockSpec((tm, LANES), lambda i, j: (i, 0)),
                   pl.BlockSpec((tm, D), lambda i, j: (i, 0))],
        out_shape=[jax.ShapeDtypeStruct((T, NMAIN), BF16), jax.ShapeDtypeStruct((T, LANES), F32),
                   jax.ShapeDtypeStruct((T, D), BF16)],
        scratch=[pltpu.VMEM((tm, D), BF16)], dims=("parallel", "arbitrary"),
        operands=(x2, nw, wmain, wsmall, cos128, sin128))


def _inproj_bwd_dx(dproj, wmain_t, dps16, wsmall_t, x2, nw, dxo, li, comm=None):
    T = x2.shape[0]
    tm, tk = min(1024, T), 512
    nk = NMAIN // tk
    ni = T // tm

    def body(dp_ref, wt_ref, ds_ref, wst_ref, x_ref, nw_ref, dxo_ref, dx_ref, dx16_ref, dnw_ref, acc):
        i, k = pl.program_id(0), pl.program_id(1)

        @pl.when(k == 0)
        def _():
            acc[...] = _dot(ds_ref[...], wst_ref[...])

        acc[...] += _dot(dp_ref[...], wt_ref[...])

        @pl.when(jnp.logical_and(i == 0, k == 0))
        def _():
            dnw_ref[...] = jnp.zeros_like(dnw_ref)

        @pl.when(k == nk - 1)
        def _():
            x = x_ref[...]
            r = lax.rsqrt(jnp.mean(x * x, axis=-1, keepdims=True) + EPS)
            dh = acc[...]
            g = dh * nw_ref[...]
            dx = dxo_ref[...] + r * g - x * (r * r * r) * jnp.mean(g * x, axis=-1, keepdims=True)
            dx_ref[...] = dx
            dx16_ref[...] = dx.astype(BF16)
            dnw_ref[0:1, :] += jnp.sum(dh * x * r, axis=0, keepdims=True)

    return _hosted_call(
        body, comm, f"inproj_bwd_dx_{li}", (ni, nk),
        in_specs=[pl.BlockSpec((tm, tk), lambda i, k: (i, k)),
                  pl.BlockSpec((tk, D), lambda i, k: (k, 0)),
                  pl.BlockSpec((tm, LANES), lambda i, k: (i, 0)),
                  pl.BlockSpec((LANES, D), lambda i, k: (0, 0)),
                  pl.BlockSpec((tm, D), lambda i, k: (i, 0)),
                  pl.BlockSpec((1, D), lambda i, k: (0, 0)),
                  pl.BlockSpec((tm, D), lambda i, k: (i, 0))],
        out_specs=[pl.BlockSpec((tm, D), lambda i, k: (i, 0)),
                   pl.BlockSpec((tm, D), lambda i, k: (i, 0)),
                   pl.BlockSpec((8, D), lambda i, k: (0, 0))],
        out_shape=[jax.ShapeDtypeStruct((T, D), F32), jax.ShapeDtypeStruct((T, D), BF16),
                   jax.ShapeDtypeStruct((8, D), F32)],
        scratch=[pltpu.VMEM((tm, D), F32)], dims=("arbitrary", "arbitrary"),
        operands=(dproj, wmain_t, dps16, wsmall_t, x2, nw, dxo))


def _merge_fwd(ya, yb, yc, proj, gbias, wp, wout, x2, li):
    T = x2.shape[0]
    tm = min(512, T)
    gcol = OFF_G // D

    def body(ya_ref, yb_ref, yc_ref, g0_ref, g1_ref, g2_ref, gb_ref, wp_ref, wo_ref, x_ref, xn_ref, br_ref):
        merged = jnp.zeros((tm, D), F32)
        for i, (y_ref, g_ref) in enumerate(((ya_ref, g0_ref), (yb_ref, g1_ref), (yc_ref, g2_ref))):
            br = _dot(y_ref[...], wp_ref[i])
            br_ref[i] = br.astype(BF16)
            gate = _sigmoid(g_ref[...].astype(F32) + gb_ref[i:i + 1, :])
            merged = merged + gate * br
        xn_ref[...] = x_ref[...] + _dot(merged.astype(BF16), wo_ref[...])

    row = lambda c: pl.BlockSpec((tm, D), lambda i, c=c: (i, c))
    return pl.pallas_call(
        body, name=f"merge_fwd_{li}", grid=(T // tm,),
        in_specs=[row(0), row(0), row(0), row(gcol), row(gcol + 1), row(gcol + 2),
                  pl.BlockSpec((3, D), lambda i: (0, 0)),
                  pl.BlockSpec((3, D, D), lambda i: (0, 0, 0)),
                  pl.BlockSpec((D, D), lambda i: (0, 0)),
                  row(0)],
        out_specs=[row(0), pl.BlockSpec((3, tm, D), lambda i: (0, i, 0))],
        out_shape=[jax.ShapeDtypeStruct((T, D), F32), jax.ShapeDtypeStruct((3, T, D), BF16)],
        compiler_params=_cparams(("parallel",), VMEM_LIMIT),
    )(ya, yb, yc, proj, proj, proj, gbias, wp, wout, x2)


def _merge_bwd_gates(dxo16, wout_t, br, proj, gbias, li):
    T = dxo16.shape[0]
    tm = min(512, T)
    gcol = OFF_G // D

    def body(dx_ref, wot_ref, br_ref, g0_ref, g1_ref, g2_ref, gb_ref, dbr_ref, dg_ref, mg_ref, dgb_ref):
        @pl.when(pl.program_id(0) == 0)
        def _():
            dgb_ref[...] = jnp.zeros_like(dgb_ref)

        dm = _dot(dx_ref[...], wot_ref[...])
        merged = jnp.zeros((tm, D), F32)
        for i, g_ref in enumerate((g0_ref, g1_ref, g2_ref)):
            b = br_ref[i].astype(F32)
            gate = _sigmoid(g_ref[...].astype(F32) + gb_ref[i:i + 1, :])
            merged = merged + gate * b
            dbr_ref[i] = (dm * gate).astype(BF16)
            dgate = dm * b * gate * (1.0 - gate)
            dg_ref[:, D * i:D * (i + 1)] = dgate.astype(BF16)
            dgb_ref[i:i + 1, :] += jnp.sum(dgate, axis=0, keepdims=True)
        mg_ref[...] = merged.astype(BF16)

    row = lambda c: pl.BlockSpec((tm, D), lambda i, c=c: (i, c))
    return pl.pallas_call(
        body, name=f"merge_bwd_gates_{li}", grid=(T // tm,),
        in_specs=[row(0), pl.BlockSpec((D, D), lambda i: (0, 0)),
                  pl.BlockSpec((3, tm, D), lambda i: (0, i, 0)),
                  row(gcol), row(gcol + 1), row(gcol + 2),
                  pl.BlockSpec((3, D), lambda i: (0, 0))],
        out_specs=[pl.BlockSpec((3, tm, D), lambda i: (0, i, 0)),
                   pl.BlockSpec((tm, 3 * D), lambda i: (i, 0)),
                   row(0),
                   pl.BlockSpec((8, D), lambda i: (0, 0))],
        out_shape=[jax.ShapeDtypeStruct((3, T, D), BF16), jax.ShapeDtypeStruct((T, 3 * D), BF16),
                   jax.ShapeDtypeStruct((T, D), BF16), jax.ShapeDtypeStruct((8, D), F32)],
        compiler_params=_cparams(("arbitrary",), VMEM_LIMIT),
    )(dxo16, wout_t, br, proj, proj, proj, gbias)


def _final_loss(x2, tgt, fw):
    T = x2.shape[0]
    tm = min(512, T)
    ni = T // tm

    def body(x_ref, t_ref, w_ref, dx_ref, dx16_ref, st_ref):
        i = pl.program_id(0)

        @pl.when(i == 0)
        def _():
            st_ref[...] = jnp.zeros_like(st_ref)

        x = x_ref[...]
        r = lax.rsqrt(jnp.mean(x * x, axis=-1, keepdims=True) + EPS)
        xh = x * r
        err = xh * w_ref[...] - t_ref[...]
        dy = err * (1.0 / D)
        g = dy * w_ref[...]
        dx = r * g - x * (r * r * r) * jnp.mean(g * x, axis=-1, keepdims=True)
        dx_ref[...] = dx
        dx16_ref[...] = dx.astype(BF16)
        st_ref[0:1, :] += jnp.sum(dy * xh, axis=0, keepdims=True)
        st_ref[1:2, :] += jnp.sum(err * err, axis=0, keepdims=True)

        @pl.when(i == ni - 1)
        def _():
            tot = jnp.sum(st_ref[1:2, :], axis=1, keepdims=True) * (0.5 / D)
            st_ref[2:3, :] = jnp.broadcast_to(tot, (1, D))

    row = pl.BlockSpec((tm, D), lambda i: (i, 0))
    return pl.pallas_call(
        body, name="final_loss", grid=(ni,),
        in_specs=[row, row, pl.BlockSpec((1, D), lambda i: (0, 0))],
        out_specs=[row, row, pl.BlockSpec((8, D), lambda i: (0, 0))],
        out_shape=[jax.ShapeDtypeStruct((T, D), F32), jax.ShapeDtypeStruct((T, D), BF16),
                   jax.ShapeDtypeStruct((8, D), F32)],
        compiler_params=_cparams(("arbitrary",), VMEM_LIMIT),
    )(x2, tgt, fw)


def _zgate_bwd(dy, o, proj, zcol, name):
    T = dy.shape[0]
    tm = min(512, T)

    def body(dy_ref, o_ref, z_ref, do_ref, dz_ref):
        z = z_ref[...].astype(F32)
        dyv = dy_ref[...].astype(F32)
        sg = _sigmoid(z)
        do_ref[...] = (dyv * z * sg).astype(BF16)
        dz_ref[...] = (dyv * o_ref[...].astype(F32) * sg * (1.0 + z * (1.0 - sg))).astype(BF16)

    row = lambda c: pl.BlockSpec((tm, D), lambda i, c=c: (i, c))
    sds = jax.ShapeDtypeStruct((T, D), BF16)
    return pl.pallas_call(
        body, name=name, grid=(T // tm,),
        in_specs=[row(0), row(0), row(zcol)], out_specs=[row(0), row(0)], out_shape=[sds, sds],
        compiler_params=_cparams(("parallel",), VMEM_LIMIT),
    )(dy, o, proj)


def _fox_cum(ps, fb_row, S, li):
    T = ps.shape[0]
    nb = S // LCH

    def body(ps_ref, fb_ref, cum_ref, carry):
        @pl.when(pl.program_id(1) == 0)
        def _():
            carry[...] = jnp.zeros_like(carry)

        logf = -_softplus(-(ps_ref[...] + fb_ref[...]))
        r = lax.broadcasted_iota(jnp.int32, (LCH, LCH), 0)
        c = lax.broadcasted_iota(jnp.int32, (LCH, LCH), 1)
        tri = (r >= c).astype(F32)
        cum = _dot_hi(tri, logf) + carry[0:1, :]
        cum_ref[...] = cum
        carry[0:1, :] = cum[LCH - 1:LCH, :]

    return pl.pallas_call(
        body, name=f"fox_cum_{li}", grid=(T // S, nb),
        in_specs=[pl.BlockSpec((LCH, LANES), lambda b, i: (b * nb + i, 0)),
                  pl.BlockSpec((1, LANES), lambda b, i: (0, 0))],
        out_specs=pl.BlockSpec((LCH, LANES), lambda b, i: (b * nb + i, 0)),
        out_shape=jax.ShapeDtypeStruct((T, LANES), F32),
        scratch_shapes=[pltpu.VMEM((8, LANES), F32)],
        compiler_params=_cparams(("arbitrary", "arbitrary")),
    )(ps, fb_row)


def _fox_cum_bwd(dcum, ps, fb_row, S, li):
    T = ps.shape[0]
    nb = S // LCH

    def body(dc_ref, ps_ref, fb_ref, df_ref, dfb_ref, carry):
        b, i = pl.program_id(0), pl.program_id(1)

        @pl.when(i == 0)
        def _():
            carry[...] = jnp.zeros_like(carry)

        @pl.when(jnp.logical_and(b == 0, i == 0))
        def _():
            dfb_ref[...] = jnp.zeros_like(dfb_ref)

        dc = dc_ref[...]
        r = lax.broadcasted_iota(jnp.int32, (LCH, LCH), 0)
        c = lax.broadcasted_iota(jnp.int32, (LCH, LCH), 1)
        tri = (c >= r).astype(F32)
        dlogf = _dot_hi(tri, dc) + carry[0:1, :]
        carry[0:1, :] += jnp.sum(dc, axis=0, keepdims=True)
        df = dlogf * _sigmoid(-(ps_ref[...] + fb_ref[...]))
        lane = _lane_iota()
        df = jnp.where(jnp.logical_and(lane >= NH, lane < 2 * NH), df, 0.0)
        df_ref[...] = df
        dfb_ref[0:1, :] += jnp.sum(df, axis=0, keepdims=True)

    blk = pl.BlockSpec((LCH, LANES), lambda b, i: (b * nb + nb - 1 - i, 0))
    return pl.pallas_call(
        body, name=f"fox_cum_bwd_{li}", grid=(T // S, nb),
        in_specs=[blk, blk, pl.BlockSpec((1, LANES), lambda b, i: (0, 0))],
        out_specs=[blk, pl.BlockSpec((8, LANES), lambda b, i: (0, 0))],
        out_shape=[jax.ShapeDtypeStruct((T, LANES), F32), jax.ShapeDtypeStruct((8, LANES), F32)],
        scratch_shapes=[pltpu.VMEM((8, LANES), F32)],
        compiler_params=_cparams(("arbitrary", "arbitrary")),
    )(dcum, ps, fb_row)


def _fox_blocks(S):
    bq = min(512, S)
    return bq, S // bq


def _fox_fwd(proj, cum_col, cum_row, S, li):
    T = proj.shape[0]
    B = T // S
    bq, nq = _fox_blocks(S)
    qc, kc, vc, zc = OFF_CQ // LANES, OFF_CK // LANES, OFF_CV // LANES, OFF_CZ // LANES

    def body(q_ref, k_ref, v_ref, z_ref, cq_ref, ck_ref, y_ref, o_ref, lse_ref):
        i = pl.program_id(2)
        q2 = q_ref[...]
        m0 = _lane_iota() < HD
        qh = (jnp.where(m0, q2, jnp.zeros_like(q2)), jnp.where(m0, jnp.zeros_like(q2), q2))
        row = lax.broadcasted_iota(jnp.int32, (bq, bq), 0)
        col = lax.broadcasted_iota(jnp.int32, (bq, bq), 1)
        outs = []
        for hh in range(2):
            cq = cq_ref[0, hh]

            def step(j, carry, masked, hh=hh, cq=cq):
                m, l, acc = carry
                start = pl.multiple_of(j * bq, bq)
                k2 = k_ref[pl.ds(start, bq), :]
                v2 = v_ref[pl.ds(start, bq), :]
                s = _dot_nt(qh[hh], k2) * SCALE + cq - ck_ref[0, hh, j]
                if masked:
                    s = jnp.where(row >= col, s, NEG)
                mn = jnp.maximum(m, jnp.max(s, axis=1, keepdims=True))
                alpha = jnp.exp(m - mn)
                p = jnp.exp(s - mn)
                l = alpha * l + jnp.sum(p, axis=1, keepdims=True)
                acc = alpha * acc + _dot(p.astype(BF16), v2)
                return mn, l, acc

            init = (jnp.full((bq, 1), NEG, F32), jnp.zeros((bq, 1), F32), jnp.zeros((bq, LANES), F32))
            carry = lax.fori_loop(0, i, functools.partial(step, masked=False), init)
            m, l, acc = step(i, carry, True)
            outs.append(acc / l)
            lse_ref[0, hh] = m + jnp.log(l)
        o2 = jnp.where(m0, outs[0], outs[1])
        z = z_ref[...].astype(F32)
        o_ref[...] = o2.astype(BF16)
        y_ref[...] = (o2 * z * _sigmoid(z)).astype(BF16)

    qblk = lambda c: pl.BlockSpec((bq, LANES), lambda b, p, i, c=c: (b * nq + i, c + p))
    sblk = lambda c: pl.BlockSpec((S, LANES), lambda b, p, i, c=c: (b, c + p))
    return pl.pallas_call(
        body, name=f"fox_fwd_{li}", grid=(B, NH // 2, nq),
        in_specs=[qblk(qc), sblk(kc), sblk(vc), qblk(zc),
                  pl.BlockSpec((1, 2, bq, 1), lambda b, p, i: (b, p, i, 0)),
                  pl.BlockSpec((1, 2, nq, 1, bq), lambda b, p, i: (b, p, 0, 0, 0))],
        out_specs=[qblk(0), qblk(0), pl.BlockSpec((1, 2, bq, 1), lambda b, p, i: (b, p, i, 0))],
        out_shape=[jax.ShapeDtypeStruct((T, D), BF16), jax.ShapeDtypeStruct((T, D), BF16),
                   jax.ShapeDtypeStruct((B, NH, S, 1), F32)],
        compiler_params=_cparams(("parallel", "parallel", "arbitrary"), VMEM_LIMIT),
    )(proj, proj, proj, proj, cum_col, cum_row)


def _fox_bwd(proj, do, o, cum_col, cum_row, lse, S, li):
    T = proj.shape[0]
    B = T // S
    bq, nq = _fox_blocks(S)
    qc, kc, vc = OFF_CQ // LANES, OFF_CK // LANES, OFF_CV // LANES

    def body(q_ref, k_ref, v_ref, do_ref, o_ref, cq_ref, ck_ref, lse_ref, dq_ref, dk_ref, dv_ref, dc_ref, dr_ref,
             dq_scr, dr_scr):
        j = pl.program_id(2)

        @pl.when(j == 0)
        def _():
            dq_scr[...] = jnp.zeros_like(dq_scr)
            dr_scr[...] = jnp.zeros_like(dr_scr)

        m0 = _lane_iota() < HD
        k2 = k_ref[...]
        v2 = v_ref[...]
        zk = jnp.zeros_like(k2)
        kh = (jnp.where(m0, k2, zk), jnp.where(m0, zk, k2))
        row = lax.broadcasted_iota(jnp.int32, (bq, bq), 0)
        col = lax.broadcasted_iota(jnp.int32, (bq, bq), 1)
        ck = (ck_ref[0, 0, 0], ck_ref[0, 1, 0])

        def step(i, carry, masked):
            dk, dv, dc0, dc1 = carry
            dcs = [dc0, dc1]
            start = pl.multiple_of(i * bq, bq)
            q2 = q_ref[pl.ds(start, bq), :]
            do2 = do_ref[pl.ds(start, bq), :]
            prod = do2.astype(F32) * o_ref[pl.ds(start, bq), :].astype(F32)
            zq = jnp.zeros_like(q2)
            dq = jnp.zeros((bq, LANES), F32)
            for hh in range(2):
                sel = m0 if hh == 0 else jnp.logical_not(m0)
                qh = jnp.where(sel, q2, zq)
                doh = jnp.where(sel, do2, zq)
                delta = _head_sum(prod, hh)
                s = _dot_nt(qh, k2) * SCALE + cq_ref[0, hh, pl.ds(start, bq), :] - ck[hh]
                if masked:
                    s = jnp.where(row >= col, s, NEG)
                p = jnp.exp(s - lse_ref[0, hh, pl.ds(start, bq), :])
                dp = _dot_nt(doh, v2)
                ds = p * (dp - delta)
                dcs[hh] = dcs[hh] - jnp.sum(ds, axis=0, keepdims=True)
                dr_scr[hh, pl.ds(start, bq), :] += jnp.sum(ds, axis=1, keepdims=True)
                dsb = (ds * SCALE).astype(BF16)
                dv = dv + _dot_tn(p.astype(BF16), doh)
                dk = dk + _dot_tn(dsb, qh)
                dq = dq + _dot(dsb, kh[hh])
            dq_scr[pl.ds(start, bq), :] += dq
            return dk, dv, dcs[0], dcs[1]

        zero = jnp.zeros((bq, LANES), F32)
        zrow = jnp.zeros((1, bq), F32)
        carry = step(j, (zero, zero, zrow, zrow), True)
        dk, dv, dc0, dc1 = lax.fori_loop(j + 1, nq, functools.partial(step, masked=False), carry)
        dk_ref[...] = dk.astype(BF16)
        dv_ref[...] = dv.astype(BF16)
        dc_ref[0, 0, 0] = dc0
        dc_ref[0, 1, 0] = dc1

        @pl.when(j == nq - 1)
        def _():
            dq_ref[...] = dq_scr[...].astype(BF16)
            dr_ref[0] = dr_scr[...]

    sblk = lambda c: pl.BlockSpec((S, LANES), lambda b, p, j, c=c: (b, c + p))
    kblk = lambda c: pl.BlockSpec((bq, LANES), lambda b, p, j, c=c: (b * nq + j, c + p))
    col_spec = pl.BlockSpec((1, 2, S, 1), lambda b, p, j: (b, p, 0, 0))
    return pl.pallas_call(
        body, name=f"fox_bwd_{li}", grid=(B, NH // 2, nq),
        in_specs=[sblk(qc), kblk(kc), kblk(vc), sblk(0), sblk(0), col_spec,
                  pl.BlockSpec((1, 2, 1, 1, bq), lambda b, p, j: (b, p, j, 0, 0)), col_spec],
        out_specs=[sblk(0), kblk(0), kblk(0), pl.BlockSpec((1, 2, 1, 1, bq), lambda b, p, j: (b, p, j, 0, 0)),
                   col_spec],
        out_shape=[jax.ShapeDtypeStruct((T, D), BF16), jax.ShapeDtypeStruct((T, D), BF16),
                   jax.ShapeDtypeStruct((T, D), BF16), jax.ShapeDtypeStruct((B, NH, nq, 1, bq), F32),
                   jax.ShapeDtypeStruct((B, NH, S, 1), F32)],
        scratch_shapes=[pltpu.VMEM((S, LANES), F32), pltpu.VMEM((2, S, 1), F32)],
        compiler_params=_cparams(("parallel", "parallel", "arbitrary"), VMEM_LIMIT),
    )(proj, proj, proj, do, o, cum_col, cum_row, lse)


def _swa_blocks(S):
    bq = min(512, S)
    return bq, S // bq, bq // LCH


def _dup_head(xw, kvl):
    m0 = _lane_iota() < HD
    a = jnp.where(m0 if kvl == 0 else jnp.logical_not(m0), xw, 0.0)
    return (a + pltpu.roll(a, HD, 1)).astype(BF16)


def _band(same_block):
    r = lax.broadcasted_iota(jnp.int32, (LCH, LCH), 0)
    c = lax.broadcasted_iota(jnp.int32, (LCH, LCH), 1)
    return (c <= r) if same_block else (c > r)


def _swa_fwd(proj, sinks, S, li):
    T = proj.shape[0]
    B = T // S
    bq, nq, nsub = _swa_blocks(S)
    nrow = S // LCH
    qc, zc, kc, vc = OFF_BQ // 512, OFF_BZ // 512, OFF_BK // LANES, OFF_BV // LANES

    def body(sk_ref, q_ref, z_ref, kp_ref, kc_ref, vp_ref, vc_ref, y_ref, o_ref, lse_ref):
        c, i = pl.program_id(0), pl.program_id(2)
        m0 = _lane_iota() < HD
        kw = jnp.concatenate([kp_ref[...].astype(F32), kc_ref[...].astype(F32)], axis=0)
        vw = jnp.concatenate([vp_ref[...].astype(F32), vc_ref[...].astype(F32)], axis=0)
        kd = (_dup_head(kw, 0), _dup_head(kw, 1))
        vd = (_dup_head(vw, 0), _dup_head(vw, 1))
        valid = jnp.concatenate([_band(False), _band(True)], axis=1)
        col = lax.broadcasted_iota(jnp.int32, (LCH, 2 * LCH), 1)
        valid_first = jnp.logical_and(valid, jnp.logical_or(col >= LCH, i > 0))
        for r in range(nsub):
            rows = slice(LCH * r, LCH * (r + 1))
            msk = valid_first if r == 0 else valid
            for ch in range(4):
                kvl = ch // 2
                kwin = kd[kvl][LCH * r:LCH * (r + 2)]
                vwin = vd[kvl][LCH * r:LCH * (r + 2)]
                q2 = q_ref[rows, LANES * ch:LANES * (ch + 1)]
                outs = []
                for hh in range(2):
                    hl = 2 * ch + hh
                    sel = m0 if hh == 0 else jnp.logical_not(m0)
                    qh = jnp.where(sel, q2, jnp.zeros_like(q2))
                    s = jnp.where(msk, _dot_nt(qh, kwin) * SCALE, NEG)
                    sink = sk_ref[8 * c + hl]
                    m = jnp.maximum(jnp.max(s, axis=1, keepdims=True), sink)
                    p = jnp.exp(s - m)
                    l = jnp.sum(p, axis=1, keepdims=True) + jnp.exp(sink - m)
                    outs.append(_dot(p.astype(BF16), vwin) / l)
                    lse_ref[0, hl, rows, :] = m + jnp.log(l)
                o2 = jnp.where(m0, outs[0], outs[1])
                z = z_ref[rows, LANES * ch:LANES * (ch + 1)].astype(F32)
                o_ref[rows, LANES * ch:LANES * (ch + 1)] = o2.astype(BF16)
                y_ref[rows, LANES * ch:LANES * (ch + 1)] = (o2 * z * _sigmoid(z)).astype(BF16)

    wide = lambda cc: pl.BlockSpec((bq, 512), lambda c, b, i, cc=cc: (b * nq + i, cc + c))
    cur = lambda cc: pl.BlockSpec((bq, LANES), lambda c, b, i, cc=cc: (b * nq + i, cc + c))
    prev = lambda cc: pl.BlockSpec((LCH, LANES), lambda c, b, i, cc=cc: (b * nrow + jnp.maximum(i * nsub - 1, 0), cc + c))
    return pl.pallas_call(
        body, name=f"swa_fwd_{li}", grid=(2, B, nq),
        in_specs=[pl.BlockSpec(memory_space=pltpu.SMEM), wide(qc), wide(zc), prev(kc), cur(kc), prev(vc), cur(vc)],
        out_specs=[wide(0), wide(0), pl.BlockSpec((1, 8, bq, 1), lambda c, b, i: (b, c, i, 0))],
        out_shape=[jax.ShapeDtypeStruct((T, D), BF16), jax.ShapeDtypeStruct((T, D), BF16),
                   jax.ShapeDtypeStruct((B, NH, S, 1), F32)],
        compiler_params=_cparams(("parallel", "parallel", "parallel"), VMEM_LIMIT),
    )(sinks, proj, proj, proj, proj, proj, proj)


def _swa_bwd_dq(proj, do, o, lse, sinks, cos128, sin128, S, li):
    T = proj.shape[0]
    B = T // S
    bq, nq, nsub = _swa_blocks(S)
    nrow = S // LCH
    qc, kc, vc = OFF_BQ // 512, OFF_BK // LANES, OFF_BV // LANES

    def body(sk_ref, q_ref, do_ref, o_ref, lse_ref, kp_ref, kc_ref, vp_ref, vc_ref, cos_ref, sin_ref, dq_ref, dsk_ref):
        c, b, i = pl.program_id(0), pl.program_id(1), pl.program_id(2)

        @pl.when(jnp.logical_and(b == 0, i == 0))
        def _():
            dsk_ref[...] = jnp.zeros_like(dsk_ref)

        m0 = _lane_iota() < HD
        kw = jnp.concatenate([kp_ref[...].astype(F32), kc_ref[...].astype(F32)], axis=0)
        vw = jnp.concatenate([vp_ref[...].astype(F32), vc_ref[...].astype(F32)], axis=0)
        kd = (_dup_head(kw, 0), _dup_head(kw, 1))
        vd = (_dup_head(vw, 0), _dup_head(vw, 1))
        valid = jnp.concatenate([_band(False), _band(True)], axis=1)
        col = lax.broadcasted_iota(jnp.int32, (LCH, 2 * LCH), 1)
        valid_first = jnp.logical_and(valid, jnp.logical_or(col >= LCH, i > 0))
        dsk = [jnp.zeros((1, 1), F32) for _ in range(8)]
        for r in range(nsub):
            rows = slice(LCH * r, LCH * (r + 1))
            msk = valid_first if r == 0 else valid
            for ch in range(4):
                kvl = ch // 2
                kwin = kd[kvl][LCH * r:LCH * (r + 2)]
                vwin = vd[kvl][LCH * r:LCH * (r + 2)]
                lanes = slice(LANES * ch, LANES * (ch + 1))
                q2 = q_ref[rows, lanes]
                do2 = do_ref[rows, lanes]
                prod = do2.astype(F32) * o_ref[rows, lanes].astype(F32)
                dqs = []
                for hh in range(2):
                    hl = 2 * ch + hh
                    sel = m0 if hh == 0 else jnp.logical_not(m0)
                    qh = jnp.where(sel, q2, jnp.zeros_like(q2))
                    doh = jnp.where(sel, do2, jnp.zeros_like(do2))
                    lse = lse_ref[0, hl, rows, :]
                    s = jnp.where(msk, _dot_nt(qh, kwin) * SCALE, NEG)
                    p = jnp.exp(s - lse)
                    delta = _head_sum(prod, hh)
                    ds = p * (_dot_nt(doh, vwin) - delta)
                    dqs.append(_dot((ds * SCALE).astype(BF16), kwin))
                    psink = jnp.exp(sk_ref[8 * c + hl] - lse)
                    dsk[hl] = dsk[hl] - jnp.sum(psink * delta, axis=0, keepdims=True)
                dq2 = jnp.where(m0, dqs[0], dqs[1])
                dq2 = dq2 * cos_ref[rows, :] - _rot_half(dq2) * sin_ref[rows, :]
                dq_ref[rows, lanes] = dq2.astype(BF16)
        for hl in range(8):
            dsk_ref[0, hl:hl + 1, :] += jnp.broadcast_to(dsk[hl], (1, LANES))

    wide = lambda cc: pl.BlockSpec((bq, 512), lambda c, b, i, cc=cc: (b * nq + i, cc + c))
    cur = lambda cc: pl.BlockSpec((bq, LANES), lambda c, b, i, cc=cc: (b * nq + i, cc + c))
    prev = lambda cc: pl.BlockSpec((LCH, LANES), lambda c, b, i, cc=cc: (b * nrow + jnp.maximum(i * nsub - 1, 0), cc + c))
    pos = pl.BlockSpec((bq, LANES), lambda c, b, i: (i, 0))
    return pl.pallas_call(
        body, name=f"swa_bwd_dq_{li}", grid=(2, B, nq),
        in_specs=[pl.BlockSpec(memory_space=pltpu.SMEM), wide(qc), wide(0), wide(0),
                  pl.BlockSpec((1, 8, bq, 1), lambda c, b, i: (b, c, i, 0)),
                  prev(kc), cur(kc), prev(vc), cur(vc), pos, pos],
        out_specs=[wide(0), pl.BlockSpec((1, 8, LANES), lambda c, b, i: (c, 0, 0))],
        out_shape=[jax.ShapeDtypeStruct((T, D), BF16), jax.ShapeDtypeStruct((2, 8, LANES), F32)],
        compiler_params=_cparams(("arbitrary", "arbitrary", "arbitrary"), VMEM_LIMIT),
    )(sinks, proj, do, o, lse, proj, proj, proj, proj, cos128, sin128)


def _swa_bwd_dkv(proj, do, o, lse, cos128, sin128, S, li):
    T = proj.shape[0]
    B = T // S
    bk, nk, nsub = _swa_blocks(S)
    nrow = S // LCH
    qc, kc, vc = OFF_BQ // 512, OFF_BK // LANES, OFF_BV // LANES

    def body(q_ref, qn_ref, do_ref, don_ref, o_ref, on_ref, lse_ref, lsen_ref, k_ref, v_ref, cos_ref, sin_ref,
             dk_ref, dv_ref):
        j = pl.program_id(2)
        m0 = _lane_iota() < HD
        has_next = (j < nk - 1).astype(F32)
        kf = k_ref[...].astype(F32)
        vf = v_ref[...].astype(F32)
        kd = (_dup_head(kf, 0), _dup_head(kf, 1))
        vd = (_dup_head(vf, 0), _dup_head(vf, 1))
        masks = (_band(True), _band(False))
        for kr in range(nsub):
            krows = slice(LCH * kr, LCH * (kr + 1))
            dk = jnp.zeros((LCH, LANES), F32)
            dv = jnp.zeros((LCH, LANES), F32)
            for dq_blk in range(2):
                rq = kr + dq_blk
                nxt = rq == nsub
                qrows = slice(0, LCH) if nxt else slice(LCH * rq, LCH * (rq + 1))
                qr, dor, orr, lr = (qn_ref, don_ref, on_ref, lsen_ref) if nxt else (q_ref, do_ref, o_ref, lse_ref)
                for ch in range(4):
                    kvl = ch // 2
                    lanes = slice(LANES * ch, LANES * (ch + 1))
                    q2 = qr[qrows, lanes]
                    do2 = dor[qrows, lanes]
                    if nxt:
                        do2 = (do2.astype(F32) * has_next).astype(BF16)
                    prod = do2.astype(F32) * orr[qrows, lanes].astype(F32)
                    for hh in range(2):
                        hl = 2 * ch + hh
                        sel = m0 if hh == 0 else jnp.logical_not(m0)
                        qh = jnp.where(sel, q2, jnp.zeros_like(q2))
                        doh = jnp.where(sel, do2, jnp.zeros_like(do2))
                        s = jnp.where(masks[dq_blk], _dot_nt(qh, kd[kvl][krows]) * SCALE, NEG)
                        p = jnp.exp(s - lr[0, hl, qrows, :])
                        ds = p * (_dot_nt(doh, vd[kvl][krows]) - _head_sum(prod, hh))
                        dvc = _dot_tn(p.astype(BF16), doh)
                        dkc = _dot_tn((ds * SCALE).astype(BF16), qh)
                        if hh != kvl:
                            dvc = pltpu.roll(dvc, HD, 1)
                            dkc = pltpu.roll(dkc, HD, 1)
                        dv = dv + dvc
                        dk = dk + dkc
            dk = dk * cos_ref[krows, :] - _rot_half(dk) * sin_ref[krows, :]
            dk_ref[krows, :] = dk.astype(BF16)
            dv_ref[krows, :] = dv.astype(BF16)

    wide = lambda cc: pl.BlockSpec((bk, 512), lambda c, b, j, cc=cc: (b * nk + j, cc + c))
    nxt = lambda cc: pl.BlockSpec((LCH, 512), lambda c, b, j, cc=cc: (b * nrow + jnp.minimum((j + 1) * nsub, nrow - 1), cc + c))
    cur = lambda cc: pl.BlockSpec((bk, LANES), lambda c, b, j, cc=cc: (b * nk + j, cc + c))
    pos = pl.BlockSpec((bk, LANES), lambda c, b, j: (j, 0))
    return pl.pallas_call(
        body, name=f"swa_bwd_dkv_{li}", grid=(2, B, nk),
        in_specs=[wide(qc), nxt(qc), wide(0), nxt(0), wide(0), nxt(0),
                  pl.BlockSpec((1, 8, bk, 1), lambda c, b, j: (b, c, j, 0)),
                  pl.BlockSpec((1, 8, LCH, 1), lambda c, b, j: (b, c, jnp.minimum((j + 1) * nsub, nrow - 1), 0)),
                  cur(kc), cur(vc), pos, pos],
        out_specs=[cur(0), cur(0)],
        out_shape=[jax.ShapeDtypeStruct((T, 2 * LANES), BF16), jax.ShapeDtypeStruct((T, 2 * LANES), BF16)],
        compiler_params=_cparams(("parallel", "parallel", "parallel"), VMEM_LIMIT),
    )(proj, proj, do, do, o, o, lse, lse, proj, proj, cos128, sin128)


HALO = 16


def _ssm_chunk_pre(ext, cw_ref, cb_ref, ps, dtb, alog):
    pre = cb_ref[...]
    for k in range(4):
        pre = pre + cw_ref[k:k + 1, :] * ext[pl.ds(HALO - 3 + k, LCH), :]
    sg = _sigmoid(pre)
    dt = _softplus(ps + dtb)
    a = -jnp.exp(alog)
    r = lax.broadcasted_iota(jnp.int32, (LCH, LCH), 0)
    c = lax.broadcasted_iota(jnp.int32, (LCH, LCH), 1)
    acum = _dot_hi((r >= c).astype(F32), dt * a)
    return pre, sg, dt, a, acum


def _pairsel(v, p):
    return jnp.where(_lane_iota() < HD, v[:, 2 * p:2 * p + 1], v[:, 2 * p + 1:2 * p + 2])


def _decay(acum, acum_t, h):
    r = lax.broadcasted_iota(jnp.int32, (LCH, LCH), 0)
    c = lax.broadcasted_iota(jnp.int32, (LCH, LCH), 1)
    causal = r >= c
    seg = acum[:, h:h + 1] - acum_t[h:h + 1, :]
    return jnp.where(causal, jnp.exp(jnp.where(causal, seg, 0.0)), 0.0)


def _ssm_pair_fwd(p, x, dt, acum, acum_t, e_all, w_all, cd, cb_g, b_g, c_g, hprev, dsk_ref):
    m0 = _lane_iota() < HD
    lanes = slice(LANES * p, LANES * (p + 1))
    x2 = x[:, lanes]
    dt2 = _pairsel(dt, p)
    xdt2 = x2 * dt2
    xdtb = xdt2.astype(BF16)
    lms, ms, yds = [], [], []
    for hh in range(2):
        lm = _decay(acum, acum_t, 2 * p + hh)
        mm = cb_g * lm
        lms.append(lm)
        ms.append(mm)
        yds.append(_dot(mm.astype(BF16), xdtb))
    yd2 = jnp.where(m0, yds[0], yds[1])
    w2 = _pairsel(w_all, p)
    xw = (xdt2 * w2).astype(BF16)
    s2 = _dot_tn(xw, b_g)
    z2 = _dot_nt(c_g, hprev.astype(BF16))
    e2 = _pairsel(e_all, p)
    rowsel = lax.broadcasted_iota(jnp.int32, (LANES, 1), 0) < HD
    cdcol = jnp.where(rowsel, cd[:, 2 * p:2 * p + 1], cd[:, 2 * p + 1:2 * p + 2])
    y2 = yd2 + z2 * e2 + dsk_ref[:, lanes] * x2
    return dict(x2=x2, dt2=dt2, xdt2=xdt2, xdtb=xdtb, lms=lms, ms=ms, yd2=yd2, w2=w2, xw=xw, s2=s2, z2=z2, e2=e2,
                cdcol=cdcol, y2=y2)


def _ssm_specs(S, rev):
    nc = S // LCH
    ch = (lambda c: nc - 1 - c) if rev else (lambda c: c)
    prev = pl.BlockSpec((HALO, 2 * D), lambda b, c: (jnp.maximum(b * (S // HALO) + ch(c) * (LCH // HALO) - 1, 0), 0))
    cur = pl.BlockSpec((LCH, 2 * D), lambda b, c: (b * nc + ch(c), 0))
    zed = pl.BlockSpec((LCH, D), lambda b, c: (b * nc + ch(c), OFF_AZ // D))
    row = pl.BlockSpec((LCH, D), lambda b, c: (b * nc + ch(c), 0))
    psb = pl.BlockSpec((LCH, LANES), lambda b, c: (b * nc + ch(c), 0))
    hpb = pl.BlockSpec((1, 1, NH // 2, LANES, NST), lambda b, c: (b, ch(c), 0, 0, 0))
    const = lambda r, w: pl.BlockSpec((r, w), lambda b, c: (0, 0))
    return nc, prev, cur, zed, row, psb, hpb, const


def _ssm_fwd(proj, ps, cw, cb, dtb, alog, dsk, nw, S, li):
    T = proj.shape[0]
    B = T // S
    nc, prev, cur, zed, row, psb, hpb, const = _ssm_specs(S, False)

    def body(prev_ref, cur_ref, z_ref, ps_ref, cw_ref, cb_ref, dtb_ref, alog_ref, dsk_ref, nw_ref,
             ya_ref, hp_ref, h_scr, ext):
        c = pl.program_id(1)

        @pl.when(c == 0)
        def _():
            h_scr[...] = jnp.zeros_like(h_scr)

        ext[0:HALO, :] = prev_ref[...].astype(F32) * (c > 0).astype(F32)
        ext[HALO:HALO + LCH, :] = cur_ref[...].astype(F32)
        pre, sg, dt, a, acum = _ssm_chunk_pre(ext, cw_ref, cb_ref, ps_ref[...], dtb_ref[...], alog_ref[...])
        act = pre * sg
        acum_t = acum.T
        e_all = jnp.exp(acum)
        last = acum[LCH - 1:LCH, :]
        w_all = jnp.exp(last - acum)
        cd = jnp.exp(last)
        x = act[:, :D]
        for g in range(NGRP):
            b_g = act[:, D + NST * g:D + NST * (g + 1)].astype(BF16)
            c_g = act[:, D + NGRP * NST + NST * g:D + NGRP * NST + NST * (g + 1)].astype(BF16)
            cb_g = _dot_nt(c_g, b_g)
            ygs = []
            for p in (2 * g, 2 * g + 1):
                hprev = h_scr[p]
                hp_ref[0, 0, p] = hprev
                f = _ssm_pair_fwd(p, x, dt, acum, acum_t, e_all, w_all, cd, cb_g, b_g, c_g, hprev, dsk_ref)
                h_scr[p] = hprev * f["cdcol"] + f["s2"]
                z2 = z_ref[:, LANES * p:LANES * (p + 1)].astype(F32)
                ygs.append(f["y2"] * z2 * _sigmoid(z2))
            yg = jnp.concatenate(ygs, axis=1)
            r = lax.rsqrt(jnp.mean(yg * yg, axis=1, keepdims=True) + EPS)
            ya_ref[:, 2 * LANES * g:2 * LANES * (g + 1)] = (yg * r * nw_ref[:, 2 * LANES * g:2 * LANES * (g + 1)]).astype(BF16)

    return pl.pallas_call(
        body, name=f"ssm_fwd_{li}", grid=(B, nc),
        in_specs=[prev, cur, zed, psb, const(4, 2 * D), const(1, 2 * D), const(1, LANES), const(1, LANES),
                  const(1, D), const(1, D)],
        out_specs=[row, hpb],
        out_shape=[jax.ShapeDtypeStruct((T, D), BF16), jax.ShapeDtypeStruct((B, nc, NH // 2, LANES, NST), F32)],
        scratch_shapes=[pltpu.VMEM((NH // 2, LANES, NST), F32), pltpu.VMEM((HALO + LCH, 2 * D), F32)],
        compiler_params=_cparams(("arbitrary", "arbitrary"), VMEM_LIMIT),
    )(proj, proj, proj, ps, cw, cb, dtb, alog, dsk, nw)


def _ssm_bwd(proj, ps, hp, dya, cw, cb, dtb, alog, dsk, nw, S, li, comm=None):
    T = proj.shape[0]
    B = T // S
    nc, prev, cur, zed, row, psb, hpb, const = _ssm_specs(S, True)

    def body(prev_ref, cur_ref, z_ref, ps_ref, hp_ref, dy_ref, cw_ref, cb_ref, dtb_ref, alog_ref, dsk_ref, nw_ref,
             dxbc_ref, dz_ref, dps_ref, pgw_ref, pg1_ref, pgh_ref, dh_scr, ext, extd, dact):
        b, cc = pl.program_id(0), pl.program_id(1)
        c = nc - 1 - cc

        @pl.when(jnp.logical_and(b == 0, cc == 0))
        def _():
            pgw_ref[...] = jnp.zeros_like(pgw_ref)
            pg1_ref[...] = jnp.zeros_like(pg1_ref)
            pgh_ref[...] = jnp.zeros_like(pgh_ref)

        @pl.when(cc == 0)
        def _():
            dh_scr[...] = jnp.zeros_like(dh_scr)
            extd[LCH:LCH + HALO, :] = jnp.zeros((HALO, 2 * D), F32)

        ext[0:HALO, :] = prev_ref[...].astype(F32) * (c > 0).astype(F32)
        ext[HALO:HALO + LCH, :] = cur_ref[...].astype(F32)
        psv = ps_ref[...]
        pre, sg, dt, a, acum = _ssm_chunk_pre(ext, cw_ref, cb_ref, psv, dtb_ref[...], alog_ref[...])
        act = pre * sg
        acum_t = acum.T
        e_all = jnp.exp(acum)
        last = acum[LCH - 1:LCH, :]
        w_all = jnp.exp(last - acum)
        cd = jnp.exp(last)
        x = act[:, :D]
        lane = _lane_iota()
        m0 = lane < HD
        rowsel = lax.broadcasted_iota(jnp.int32, (LANES, 1), 0) < HD
        is_last_row = lax.broadcasted_iota(jnp.int32, (LCH, 1), 0) == LCH - 1
        dacum_all = jnp.zeros((LCH, LANES), F32)
        ddt_all = jnp.zeros((LCH, LANES), F32)
        dd_row = jnp.zeros((1, LANES), F32)
        for g in range(NGRP):
            b_g = act[:, D + NST * g:D + NST * (g + 1)].astype(BF16)
            c_g = act[:, D + NGRP * NST + NST * g:D + NGRP * NST + NST * (g + 1)].astype(BF16)
            cb_g = _dot_nt(c_g, b_g)
            pairs = (2 * g, 2 * g + 1)
            fs, hps, zs, ygs = [], [], [], []
            for p in pairs:
                hprev = hp_ref[0, 0, p]
                f = _ssm_pair_fwd(p, x, dt, acum, acum_t, e_all, w_all, cd, cb_g, b_g, c_g, hprev, dsk_ref)
                z2 = z_ref[:, LANES * p:LANES * (p + 1)].astype(F32)
                fs.append(f)
                hps.append(hprev)
                zs.append(z2)
                ygs.append(f["y2"] * z2 * _sigmoid(z2))
            gl = slice(2 * LANES * g, 2 * LANES * (g + 1))
            yg = jnp.concatenate(ygs, axis=1)
            r = lax.rsqrt(jnp.mean(yg * yg, axis=1, keepdims=True) + EPS)
            dyn = dy_ref[:, gl].astype(F32)
            gg = dyn * nw_ref[:, gl]
            dyg = r * gg - yg * (r * r * r) * jnp.mean(gg * yg, axis=1, keepdims=True)
            pg1_ref[0:1, gl] += jnp.sum(dyn * yg * r, axis=0, keepdims=True)
            dg_g = jnp.zeros((LCH, LCH), F32)
            db_g = jnp.zeros((LCH, NST), F32)
            dc_g = jnp.zeros((LCH, NST), F32)
            for idx, p in enumerate(pairs):
                f, hprev, z2 = fs[idx], hps[idx], zs[idx]
                lanes = slice(LANES * p, LANES * (p + 1))
                dyg2 = dyg[:, LANES * idx:LANES * (idx + 1)]
                sgz = _sigmoid(z2)
                dy2 = dyg2 * z2 * sgz
                dz_ref[:, lanes] = (dyg2 * f["y2"] * sgz * (1.0 + z2 * (1.0 - sgz))).astype(BF16)
                x2, dt2, xdt2, xdtb, w2, e2, z2m = f["x2"], f["dt2"], f["xdt2"], f["xdtb"], f["w2"], f["e2"], f["z2"]
                dx2 = dsk_ref[:, lanes] * dy2
                dyx = dy2 * x2
                dxdt2 = jnp.zeros((LCH, LANES), F32)
                diag_cols = []
                for hh in range(2):
                    sel = m0 if hh == 0 else jnp.logical_not(m0)
                    dyb = jnp.where(sel, dy2, 0.0).astype(BF16)
                    dm = _dot_nt(dyb, xdtb)
                    dg_g = dg_g + dm * f["lms"][hh]
                    dxdt2 = dxdt2 + _dot_tn(f["ms"][hh].astype(BF16), dyb)
                    em = dm * f["ms"][hh]
                    diag_cols.append(jnp.sum(em, axis=1, keepdims=True) - jnp.sum(em.T, axis=1, keepdims=True))
                dz2m = dy2 * e2
                t_off = dz2m * z2m
                dc_g = dc_g + _dot(dz2m.astype(BF16), hprev.astype(BF16))
                dhprev = _dot_tn(dz2m.astype(BF16), c_g)
                dhn = dh_scr[p]
                dhnb = dhn.astype(BF16)
                dhprev = dhprev + dhn * f["cdcol"]
                t_h = dhn * hprev
                dxw2 = _dot_nt(b_g, dhnb)
                db_g = db_g + _dot(f["xw"], dhnb)
                dxdt2 = dxdt2 + dxw2 * w2
                t_w = dxw2 * xdt2
                dx2 = dx2 + dxdt2 * dt2
                t_dt = dxdt2 * x2
                for hh in range(2):
                    h = 2 * p + hh
                    onehot = (lane == h).astype(F32)
                    w_col = w_all[:, h:h + 1]
                    dw_col = _head_sum(t_w, hh) * w_col
                    rs = rowsel if hh == 0 else jnp.logical_not(rowsel)
                    dlast = (jnp.sum(jnp.where(rs, t_h, 0.0), keepdims=True) * cd[:, h:h + 1]
                             + jnp.sum(dw_col, keepdims=True))
                    dacum_col = diag_cols[hh] + _head_sum(t_off, hh) - dw_col + jnp.where(is_last_row, dlast, 0.0)
                    dacum_all = dacum_all + dacum_col * onehot
                    ddt_all = ddt_all + _head_sum(t_dt, hh) * onehot
                    sel = m0 if hh == 0 else jnp.logical_not(m0)
                    dd_row = dd_row + jnp.sum(jnp.where(sel, dyx, 0.0), keepdims=True) * onehot
                dh_scr[p] = dhprev
                dact[:, lanes] = dx2
            dgb = dg_g.astype(BF16)
            dc_g = dc_g + _dot(dgb, b_g)
            db_g = db_g + _dot_tn(dgb, c_g)
            dact[:, D + NST * g:D + NST * (g + 1)] = db_g
            dact[:, D + NGRP * NST + NST * g:D + NGRP * NST + NST * (g + 1)] = dc_g
        rr = lax.broadcasted_iota(jnp.int32, (LCH, LCH), 0)
        cc2 = lax.broadcasted_iota(jnp.int32, (LCH, LCH), 1)
        dadt = _dot_hi((cc2 >= rr).astype(F32), dacum_all)
        ddt_all = ddt_all + dadt * a
        heads = lane < NH
        da = jnp.sum(dadt * dt, axis=0, keepdims=True)
        dr = jnp.where(heads, ddt_all * _sigmoid(psv + dtb_ref[...]), 0.0)
        dps_ref[...] = dr
        pgh_ref[0:1, :] += jnp.sum(dr, axis=0, keepdims=True)
        pgh_ref[1:2, :] += jnp.where(heads, da * a, 0.0)
        pgh_ref[2:3, :] += dd_row
        dpre = dact[...] * sg * (1.0 + pre * (1.0 - sg))
        extd[0:LCH, :] = dpre
        du = jnp.zeros((LCH, 2 * D), F32)
        for k in range(4):
            du = du + cw_ref[k:k + 1, :] * extd[pl.ds(3 - k, LCH), :]
            pgw_ref[k:k + 1, :] += jnp.sum(dpre * ext[pl.ds(HALO - 3 + k, LCH), :], axis=0, keepdims=True)
        pgw_ref[4:5, :] += jnp.sum(dpre, axis=0, keepdims=True)
        dxbc_ref[...] = du.astype(BF16)
        extd[LCH:LCH + HALO, :] = dpre[0:HALO, :]

    xbc_out = pl.BlockSpec((LCH, 2 * D), lambda b, c: (b * nc + nc - 1 - c, 0))
    acc = lambda w: pl.BlockSpec((8, w), lambda b, c: (0, 0))
    return _hosted_call(
        body, comm, f"ssm_bwd_{li}", (B, nc),
        in_specs=[prev, cur, zed, psb, hpb, row, const(4, 2 * D), const(1, 2 * D), const(1, LANES), const(1, LANES),
                  const(1, D), const(1, D)],
        out_specs=[xbc_out, row, psb, acc(2 * D), acc(D), acc(LANES)],
        out_shape=[jax.ShapeDtypeStruct((T, 2 * D), BF16), jax.ShapeDtypeStruct((T, D), BF16),
                   jax.ShapeDtypeStruct((T, LANES), F32), jax.ShapeDtypeStruct((8, 2 * D), F32),
                   jax.ShapeDtypeStruct((8, D), F32), jax.ShapeDtypeStruct((8, LANES), F32)],
        scratch=[pltpu.VMEM((NH // 2, LANES, NST), F32), pltpu.VMEM((HALO + LCH, 2 * D), F32),
                 pltpu.VMEM((LCH + HALO, 2 * D), F32), pltpu.VMEM((LCH, 2 * D), F32)],
        dims=("arbitrary", "arbitrary"),
        operands=(proj, proj, proj, ps, hp, dya, cw, cb, dtb, alog, dsk, nw))


def _lane_row(v, offset):
    return jnp.pad(v.astype(F32), (offset, LANES - offset - v.shape[0]))[None]


def _pack_rows(arrays):
    parts = []
    for a in arrays:
        flat = a.reshape(-1).astype(F32)
        pad = (-flat.shape[0]) % LANES
        parts.append(jnp.pad(flat, (0, pad)))
    flat = jnp.concatenate(parts)
    pad = (-flat.shape[0]) % (8 * LANES)
    return jnp.pad(flat, (0, pad)).reshape(-1, LANES)


def _unpack_rows(pack, shapes):
    flat = pack.reshape(-1)
    out, pos = [], 0
    for shp in shapes:
        n = math.prod(shp)
        out.append(flat[pos:pos + n].reshape(shp))
        pos += n + (-n) % LANES
    return out


def _split_w_in(w):
    main = jnp.concatenate([w[:, 0:3072], w[:, 3088:4112], w[:, 4624:5648], w[:, 5648:8720], w[:, 8736:12832],
                            w[:, 4112:4624]], axis=1)
    small = jnp.concatenate([w[:, 3072:3088], w[:, 8720:8736], jnp.zeros((D, LANES - 2 * NH), w.dtype)], axis=1)
    return main, small


def _join_w_in(dm, ds):
    return jnp.concatenate([dm[:, 0:3072], ds[:, 0:NH], dm[:, 3072:4096], dm[:, 12288:12800], dm[:, 4096:5120],
                            dm[:, 5120:8192], ds[:, NH:2 * NH], dm[:, 8192:12288]], axis=1)


def kernel(x, norm_w, w_in, conv_w, conv_b, dt_bias, a_log, d_skip, ssm_norm_w, sinks, f_bias, gate_bias, w_proj, w_out, final_norm_w, loss_target, m_norm_w, m_w_in, m_conv_w, m_conv_b, m_dt_bias, m_a_log, m_d_skip, m_ssm_norm_w, m_sinks, m_f_bias, m_gate_bias, m_w_proj, m_w_out, m_final_norm_w, v_norm_w, v_w_in, v_conv_w, v_conv_b, v_dt_bias, v_a_log, v_d_skip, v_ssm_norm_w, v_sinks, v_f_bias, v_gate_bias, v_w_proj, v_w_out, v_final_norm_w):
    Bl, S, _ = x.shape
    T = Bl * S
    depth = norm_w.shape[0]
    me = 4 * lax.axis_index("x") + 2 * lax.axis_index("y") + lax.axis_index("c")
    csh, gsh = conv_w.shape[2], gate_bias.shape[2]

    def gather_plan(l):
        small = jnp.concatenate([conv_w[l].reshape(-1), gate_bias[l].reshape(-1)]).reshape(-1, LANES)
        return _Comm("gather", [w_in[l].astype(BF16), w_proj[l].astype(BF16), w_out[l].astype(BF16), small])

    def unpack_weights(res):
        g_win, g_wp, g_wo, g_small = res
        flat = g_small.reshape(NDEV, -1)
        return (g_win.transpose(1, 0, 2).reshape(D, NIN),
                g_wp.transpose(1, 0, 2, 3).reshape(3, D, D),
                g_wo.reshape(D, D),
                flat[:, :4 * csh].reshape(NDEV, 4, csh).transpose(1, 0, 2).reshape(4, 2 * D),
                flat[:, 4 * csh:].reshape(NDEV, 3, gsh).transpose(1, 0, 2).reshape(3, D))

    def scatter_plan(gw_in, gw_p, gw_o):
        return _Comm("scatter", [gw_in.astype(BF16).reshape(D, NDEV, NSH).transpose(1, 0, 2),
                                 gw_p.astype(BF16).reshape(3, NDEV, D // NDEV, D).transpose(1, 0, 2, 3),
                                 gw_o.astype(BF16).reshape(NDEV, D // NDEV, D)])

    pos = jnp.arange(S, dtype=F32)
    inv_freq = ROPE_THETA ** (-jnp.arange(0, HD, 2, dtype=F32) / HD)
    ang = pos[:, None] * inv_freq[None, :]
    cos128 = jnp.tile(jnp.cos(ang), (1, 4))
    sign = jnp.where((jnp.arange(LANES) % HD) < HD // 2, -1.0, 1.0).astype(F32)
    sin128 = jnp.tile(jnp.sin(ang), (1, 4)) * sign[None, :]

    bq, nq = _fox_blocks(S)
    x2 = x.reshape(T, D)
    tgt2 = loss_target.reshape(T, D)

    saved = []
    xcur = x2
    weights = [None] * depth
    weights[0] = unpack_weights(gather_plan(0).call("gather_weights_0"))
    for l in range(depth):
        win_l, wp_l, wo_l, cw_l, gb_l = weights[l]
        wmain, wsmall = _split_w_in(win_l)
        comm = gather_plan(l + 1) if l + 1 < depth else None
        res = _inproj_fwd(xcur, norm_w[l][None], wmain, wsmall, cos128, sin128, S, l, comm)
        proj, ps, h16 = res[:3]
        if comm is not None:
            weights[l + 1] = unpack_weights(res[3:])
        dtb = _lane_row(dt_bias[l], 0)
        alog = _lane_row(a_log[l], 0)
        fb = _lane_row(f_bias[l], NH)
        dsk = jnp.repeat(d_skip[l], HD)[None]
        ya, hp = _ssm_fwd(proj, ps, cw_l, conv_b[l][None], dtb, alog, dsk, ssm_norm_w[l][None], S, l)
        yb, ob, lse_b = _swa_fwd(proj, sinks[l], S, l)
        cum = _fox_cum(ps, fb, S, l)
        cumh = cum[:, NH:2 * NH].reshape(Bl, S, NH).transpose(0, 2, 1)
        cum_col = cumh[..., None]
        cum_row = cumh.reshape(Bl, NH, nq, 1, bq)
        yc, oc, lse_c = _fox_fwd(proj, cum_col, cum_row, S, l)
        xnext, br = _merge_fwd(ya, yb, yc, proj, gb_l, wp_l, wo_l, xcur, l)
        saved.append(dict(x=xcur, wmain=wmain, wsmall=wsmall, proj=proj, ps=ps, h16=h16, dtb=dtb, alog=alog, fb=fb,
                          dsk=dsk, ya=ya, hp=hp, yb=yb, ob=ob, lse_b=lse_b, cum_col=cum_col, cum_row=cum_row,
                          yc=yc, oc=oc, lse_c=lse_c, br=br))
        xcur = xnext

    dx, dx16, st = _final_loss(xcur, tgt2, final_norm_w[None])
    loss_part = st[2, 0]
    g_final = st[0]

    gsm = {k: [None] * depth for k in ("norm_w", "conv_w", "conv_b", "dt_bias", "a_log", "d_skip", "ssm_norm_w",
                                      "sinks", "f_bias", "gate_bias")}
    parts = [None] * depth
    pending = None
    for l in reversed(range(depth)):
        sv = saved[l]
        proj, ps = sv["proj"], sv["ps"]
        _, wp_l, wo_l, cw_l, gb_l = weights[l]
        dbr, dgates, merged16, dgb = _merge_bwd_gates(dx16, wo_l.T, sv["br"], proj, gb_l, l)
        g_wo = _matmul(merged16.T, dx16, F32, f"dwout_{l}")
        dys, dwps = [], []
        for i, y in enumerate((sv["ya"], sv["yb"], sv["yc"])):
            dys.append(_matmul(dbr[i], wp_l[i].T, BF16, f"dy_{l}_{i}"))
            dwps.append(_matmul(y.T, dbr[i], F32, f"dwproj_{l}_{i}"))
        g_wp = jnp.stack(dwps)
        gsm["gate_bias"][l] = dgb[0:3]
        res = _ssm_bwd(proj, ps, sv["hp"], dys[0], cw_l, conv_b[l][None], sv["dtb"], sv["alog"], sv["dsk"],
                       ssm_norm_w[l][None], S, l, pending)
        dxbc, daz, dps_a, pgw, pg1, pgh = res[:6]
        if pending is not None:
            parts[l + 1] = res[6:]
        gsm["conv_w"][l], gsm["conv_b"][l] = pgw[0:4], pgw[4]
        gsm["ssm_norm_w"][l] = pg1[0]
        gsm["dt_bias"][l], gsm["a_log"][l], gsm["d_skip"][l] = pgh[0, :NH], pgh[1, :NH], pgh[2, :NH]
        do_b, dbz = _zgate_bwd(dys[1], sv["ob"], proj, OFF_BZ // D, f"zgate_bwd_swa_{l}")
        dq_b, dsk_b = _swa_bwd_dq(proj, do_b, sv["ob"], sv["lse_b"], sinks[l], cos128, sin128, S, l)
        dk_b, dv_b = _swa_bwd_dkv(proj, do_b, sv["ob"], sv["lse_b"], cos128, sin128, S, l)
        gsm["sinks"][l] = dsk_b[:, :, 0].reshape(NH)
        do_c, dcz = _zgate_bwd(dys[2], sv["oc"], proj, OFF_CZ // D, f"zgate_bwd_fox_{l}")
        dq_c, dk_c, dv_c, dcum_k, dcum_q = _fox_bwd(proj, do_c, sv["oc"], sv["cum_col"], sv["cum_row"], sv["lse_c"], S, l)
        dcum_tm = (dcum_k.reshape(Bl, NH, S) + dcum_q.reshape(Bl, NH, S)).transpose(0, 2, 1).reshape(T, NH)
        dcum_pad = jnp.pad(dcum_tm, ((0, 0), (NH, LANES - 2 * NH)))
        df, dfb = _fox_cum_bwd(dcum_pad, ps, sv["fb"], S, l)
        gsm["f_bias"][l] = dfb[0, NH:2 * NH]
        dps16 = (dps_a + df).astype(BF16)
        dproj = jnp.concatenate([dxbc, daz, dq_b, dbz, dq_c, dk_c, dv_c, dcz, dgates, dk_b, dv_b], axis=1)
        h_t = sv["h16"].T
        dwm = _matmul(h_t, dproj, F32, f"dwin_main_{l}", tm=1024, tn=1280, tk=512)
        dws = _matmul(h_t, dps16, F32, f"dwin_small_{l}")
        plan = scatter_plan(_join_w_in(dwm, dws), g_wp, g_wo)
        res = _inproj_bwd_dx(dproj, sv["wmain"].T, dps16, sv["wsmall"].T, sv["x"], norm_w[l][None], dx, l,
                             plan if l == 0 else None)
        dx, dx16, dnw = res[:3]
        if l == 0:
            parts[0] = res[3:]
        else:
            pending = plan
        gsm["norm_w"][l] = dnw[0]

    big = {}
    for idx, (name, w, m, v) in enumerate((("w_in", w_in, m_w_in, v_w_in), ("w_proj", w_proj, m_w_proj, v_w_proj),
                                          ("w_out", w_out, m_w_out, v_w_out))):
        cols = w.shape[-1]
        res = _sum_adamw([parts[l][idx].reshape(NDEV, -1, cols) for l in range(depth)], w.reshape(depth, -1, cols),
                         m.reshape(depth, -1, cols), v.reshape(depth, -1, cols), f"adamw_{name}")
        big[name] = [r.reshape(w.shape) for r in res]

    small_names = ("norm_w", "conv_b", "dt_bias", "a_log", "d_skip", "ssm_norm_w", "sinks", "f_bias")
    small_parts = [jnp.stack(gsm[k]) for k in small_names] + [g_final, jnp.stack(gsm["conv_w"]),
                                                              jnp.stack(gsm["gate_bias"]), loss_part.reshape(1)]
    shapes = [a.shape for a in small_parts]
    summed = _unpack_rows(_all_reduce_small(_pack_rows(small_parts)), shapes)
    g_small = dict(zip(small_names, summed[:len(small_names)]))
    g_small["final_norm_w"] = summed[len(small_names)]
    g_small["conv_w"] = lax.dynamic_slice_in_dim(summed[len(small_names) + 1], me * csh, csh, axis=2)
    g_small["gate_bias"] = lax.dynamic_slice_in_dim(summed[len(small_names) + 2], me * gsh, gsh, axis=2)
    loss = summed[len(small_names) + 3][0]

    ws = dict(norm_w=norm_w, conv_w=conv_w, conv_b=conv_b, dt_bias=dt_bias, a_log=a_log, d_skip=d_skip,
              ssm_norm_w=ssm_norm_w, sinks=sinks, f_bias=f_bias, gate_bias=gate_bias, final_norm_w=final_norm_w)
    ms = dict(norm_w=m_norm_w, conv_w=m_conv_w, conv_b=m_conv_b, dt_bias=m_dt_bias, a_log=m_a_log, d_skip=m_d_skip,
              ssm_norm_w=m_ssm_norm_w, sinks=m_sinks, f_bias=m_f_bias, gate_bias=m_gate_bias,
              final_norm_w=m_final_norm_w)
    vs = dict(norm_w=v_norm_w, conv_w=v_conv_w, conv_b=v_conv_b, dt_bias=v_dt_bias, a_log=v_a_log, d_skip=v_d_skip,
              ssm_norm_w=v_ssm_norm_w, sinks=v_sinks, f_bias=v_f_bias, gate_bias=v_gate_bias,
              final_norm_w=v_final_norm_w)
    order = list(ws)
    oshapes = [ws[k].shape for k in order]
    res = _adamw_small(_pack_rows([g_small[k] for k in order]), _pack_rows([ws[k] for k in order]),
                       _pack_rows([ms[k] for k in order]), _pack_rows([vs[k] for k in order]))
    d_s, m_s, v_s = (dict(zip(order, _unpack_rows(r, oshapes))) for r in res)

    names = ("norm_w", "w_in", "conv_w", "conv_b", "dt_bias", "a_log", "d_skip", "ssm_norm_w", "sinks", "f_bias",
             "gate_bias", "w_proj", "w_out", "final_norm_w")
    grads, deltas, new_m, new_v = [], [], [], []
    for k in names:
        if k in big:
            g, d_, m_, v_ = big[k]
        else:
            g, d_, m_, v_ = g_small[k], d_s[k], m_s[k], v_s[k]
        grads.append(g)
        deltas.append(d_)
        new_m.append(m_)
        new_v.append(v_)
    return (loss, dx.reshape(Bl, S, D), *grads, *deltas, *new_m, *new_v)
```

```python
import functools
import math

import jax
import jax.numpy as jnp
from jax import lax
from jax.experimental import pallas as pl
from jax.experimental.pallas import tpu as pltpu

F32 = jnp.float32
BF16 = jnp.bfloat16
MESH = pl.DeviceIdType.MESH
NDEV = 8

D = 1024
NH = 16
HD = 64
NST = 128
NGRP = 4
LCH = 128
EPS = 1e-6
ROPE_THETA = 10000.0
SCALE = HD ** -0.5
NEG = -1e30

LANES = 128
VMEM_LIMIT = 56 * 1024 * 1024

OFF_XBC, OFF_AZ, OFF_BQ, OFF_BZ, OFF_CQ, OFF_CK, OFF_CV, OFF_CZ, OFF_G, OFF_BK, OFF_BV = (
    0, 2048, 3072, 4096, 5120, 6144, 7168, 8192, 9216, 12288, 12544)
NMAIN = 12800
NIN = 12832
NSH = NIN // NDEV

ADAM_LR, ADAM_B1, ADAM_B2, ADAM_EPS, ADAM_WD, ADAM_STEP = 0.001, 0.9, 0.999, 1e-08, 0.01, 10


def _cparams(dims=None, vmem=None):
    return pltpu.CompilerParams(dimension_semantics=dims, vmem_limit_bytes=vmem)


def _dot(a, b):
    return jnp.dot(a, b, preferred_element_type=F32)


def _dot_nt(a, b):
    return lax.dot_general(a, b, (((1,), (1,)), ((), ())), preferred_element_type=F32)


def _dot_tn(a, b):
    return lax.dot_general(a, b, (((0,), (0,)), ((), ())), preferred_element_type=F32)


def _dot_hi(a, b):
    return jnp.dot(a, b, precision=lax.Precision.HIGHEST, preferred_element_type=F32)


def _sigmoid(x):
    return 1.0 / (1.0 + jnp.exp(-x))


def _softplus(x):
    return jnp.maximum(x, 0.0) + jnp.log(1.0 + jnp.exp(-jnp.abs(x)))


def _lane_iota(n=LANES):
    return lax.broadcasted_iota(jnp.int32, (1, n), 1)


def _rot_half(x):
    first = (_lane_iota() % HD) < (HD // 2)
    return jnp.where(first, pltpu.roll(x, LANES - HD // 2, 1), pltpu.roll(x, HD // 2, 1))


def _head_sum(x, head):
    m = (_lane_iota() < HD) if head == 0 else (_lane_iota() >= HD)
    return jnp.sum(jnp.where(m, x, 0.0), axis=1, keepdims=True)


def _me_and_peers():
    x, y, c = lax.axis_index("x"), lax.axis_index("y"), lax.axis_index("c")
    me = 4 * x + 2 * y + c
    peers = []
    for k in range(1, NDEV):
        kx, ky, kc = (k >> 2) & 1, (k >> 1) & 1, k & 1
        px, py, pc = x ^ kx, y ^ ky, c ^ kc
        peers.append(((px, py, pc), 4 * px + 2 * py + pc))
    return me, peers


class _Comm:
    def __init__(self, kind, arrays):
        self.kind, self.arrays, self.n = kind, list(arrays), len(arrays)
        any_spec = pl.BlockSpec(memory_space=pl.ANY)
        self.in_specs = [any_spec] * self.n
        self.out_specs = [any_spec] * self.n
        self.out_shape = [jax.ShapeDtypeStruct(((NDEV,) + a.shape) if kind == "gather" else a.shape, a.dtype)
                          for a in self.arrays]
        self.scratch = [pltpu.SemaphoreType.DMA((self.n, NDEV - 1)), pltpu.SemaphoreType.DMA((self.n, NDEV - 1)),
                        pltpu.SemaphoreType.DMA((self.n,))]

    def copies(self, ins, outs, sems):
        send_sems, recv_sems, local_sems = sems
        me, peers = _me_and_peers()
        out = []
        for a in range(self.n):
            mine = ins[a] if self.kind == "gather" else ins[a].at[me]
            out.append(pltpu.make_async_copy(mine, outs[a].at[me], local_sems.at[a]))
            for k, (peer, pidx) in enumerate(peers):
                src = ins[a] if self.kind == "gather" else ins[a].at[pidx]
                out.append(pltpu.make_async_remote_copy(
                    src_ref=src, dst_ref=outs[a].at[me], send_sem=send_sems.at[a, k], recv_sem=recv_sems.at[a, k],
                    device_id=peer, device_id_type=MESH))
        return out

    def call(self, name):
        def body(*refs):
            cps = self.copies(refs[:self.n], refs[self.n:2 * self.n], refs[2 * self.n:])
            for cp in cps:
                cp.start()
            for cp in cps:
                cp.wait()

        return pl.pallas_call(body, name=name, out_shape=self.out_shape, in_specs=self.in_specs,
                              out_specs=self.out_specs, scratch_shapes=self.scratch)(*self.arrays)


def _hosted_call(body, comm, name, grid, in_specs, out_specs, out_shape, scratch, dims, operands):
    if comm is None:
        return pl.pallas_call(body, name=name, grid=grid, in_specs=in_specs, out_specs=out_specs, out_shape=out_shape,
                              scratch_shapes=scratch, compiler_params=_cparams(dims, VMEM_LIMIT))(*operands)
    n_in, n_out, n_scr, n = len(in_specs), len(out_specs), len(scratch), comm.n

    def hosted(*refs):
        hin, cin = refs[:n_in], refs[n_in:n_in + n]
        hout = refs[n_in + n:n_in + n + n_out]
        cout = refs[n_in + n + n_out:n_in + 2 * n + n_out]
        hscr = refs[n_in + 2 * n + n_out:n_in + 2 * n + n_out + n_scr]
        sems = refs[n_in + 2 * n + n_out + n_scr:]
        ids = [pl.program_id(a) for a in range(len(grid))]
        first = functools.reduce(jnp.logical_and, [i == 0 for i in ids])
        last = functools.reduce(jnp.logical_and, [i == g - 1 for i, g in zip(ids, grid)])

        @pl.when(first)
        def _():
            for cp in comm.copies(cin, cout, sems):
                cp.start()

        body(*hin, *hout, *hscr)

        @pl.when(last)
        def _():
            for cp in comm.copies(cin, cout, sems):
                cp.wait()

    return pl.pallas_call(
        hosted, name=name, grid=grid, in_specs=list(in_specs) + comm.in_specs,
        out_specs=list(out_specs) + comm.out_specs, out_shape=list(out_shape) + comm.out_shape,
        scratch_shapes=list(scratch) + comm.scratch,
        compiler_params=_cparams(("arbitrary",) * len(grid), VMEM_LIMIT))(*operands, *comm.arrays)


def _all_reduce_small(v):
    rows = v.shape[0]

    def body(v_ref, sum_ref, all_ref, send_sems, recv_sems):
        me, peers = _me_and_peers()
        all_ref[me] = v_ref[...]
        copies = []
        for k, (peer, _) in enumerate(peers):
            cp = pltpu.make_async_remote_copy(
                src_ref=v_ref, dst_ref=all_ref.at[me],
                send_sem=send_sems.at[k], recv_sem=recv_sems.at[k],
                device_id=peer, device_id_type=MESH)
            cp.start()
            copies.append(cp)
        for cp in copies:
            cp.wait()
        acc = all_ref[0]
        for d in range(1, NDEV):
            acc = acc + all_ref[d]
        sum_ref[...] = acc

    vm = pl.BlockSpec(memory_space=pltpu.VMEM)
    return pl.pallas_call(
        body, name="all_reduce_small",
        out_shape=jax.ShapeDtypeStruct((rows, LANES), F32),
        in_specs=[vm], out_specs=vm,
        scratch_shapes=[pltpu.VMEM((NDEV, rows, LANES), F32),
                        pltpu.SemaphoreType.DMA((NDEV - 1,)), pltpu.SemaphoreType.DMA((NDEV - 1,))],
    )(v)


def _adamw_math(w, g, m, v):
    m = ADAM_B1 * m + (1.0 - ADAM_B1) * g
    v = ADAM_B2 * v + (1.0 - ADAM_B2) * jnp.square(g)
    m_hat = m / (1.0 - ADAM_B1 ** ADAM_STEP)
    v_hat = v / (1.0 - ADAM_B2 ** ADAM_STEP)
    delta = -ADAM_LR * (m_hat / (jnp.sqrt(v_hat) + ADAM_EPS) + ADAM_WD * w)
    return delta, m, v


def _sum_adamw(parts, w, m, v, name):
    depth, rows, cols = w.shape
    tr = next(c for c in (256, 128, 64, 32, 16) if rows % c == 0)
    nb = rows // tr

    def body(*refs):
        p_refs, (w_ref, m_ref, v_ref, g_ref, d_ref, nm_ref, nv_ref) = refs[:depth], refs[depth:]
        l = pl.program_id(0)
        for ll in range(depth):
            @pl.when(l == ll)
            def _(ll=ll):
                g = p_refs[ll][0].astype(F32)
                for d in range(1, NDEV):
                    g = g + p_refs[ll][d].astype(F32)
                delta, nm, nv = _adamw_math(w_ref[0], g, m_ref[0], v_ref[0])
                g_ref[0] = g
                d_ref[0] = delta
                nm_ref[0] = nm
                nv_ref[0] = nv

    part = lambda ll: pl.BlockSpec((NDEV, tr, cols), lambda l, i, ll=ll: (0, jnp.where(l == ll, i, jnp.where(l < ll, 0, nb - 1)), 0))
    blk = pl.BlockSpec((1, tr, cols), lambda l, i: (l, i, 0))
    sds = jax.ShapeDtypeStruct((depth, rows, cols), F32)
    return pl.pallas_call(
        body, name=name, grid=(depth, nb),
        in_specs=[part(ll) for ll in range(depth)] + [blk, blk, blk],
        out_specs=[blk, blk, blk, blk], out_shape=[sds, sds, sds, sds],
        compiler_params=_cparams(("arbitrary", "arbitrary"), VMEM_LIMIT),
    )(*parts, w, m, v)


def _adamw_small(g, w, m, v):
    def body(g_ref, w_ref, m_ref, v_ref, d_ref, nm_ref, nv_ref):
        delta, nm, nv = _adamw_math(w_ref[...], g_ref[...], m_ref[...], v_ref[...])
        d_ref[...] = delta
        nm_ref[...] = nm
        nv_ref[...] = nv

    sds = jax.ShapeDtypeStruct(g.shape, F32)
    return pl.pallas_call(body, name="adamw_small", out_shape=[sds, sds, sds])(g, w, m, v)


def _matmul(a, b, out_dtype, name, tm=1024, tn=1024, tk=512):
    M, K = a.shape
    N = b.shape[1]
    tm, tn, tk = min(tm, M), min(tn, N), min(tk, K)
    nk = K // tk

    def body(a_ref, b_ref, o_ref, acc):
        k = pl.program_id(2)

        @pl.when(k == 0)
        def _():
            acc[...] = jnp.zeros_like(acc)

        acc[...] += _dot(a_ref[...], b_ref[...])

        @pl.when(k == nk - 1)
        def _():
            o_ref[...] = acc[...].astype(out_dtype)

    return pl.pallas_call(
        body, name=name, grid=(M // tm, N // tn, nk),
        in_specs=[pl.BlockSpec((tm, tk), lambda i, j, k: (i, k)), pl.BlockSpec((tk, tn), lambda i, j, k: (k, j))],
        out_specs=pl.BlockSpec((tm, tn), lambda i, j, k: (i, j)),
        out_shape=jax.ShapeDtypeStruct((M, N), out_dtype),
        scratch_shapes=[pltpu.VMEM((tm, tn), F32)],
        compiler_params=_cparams(("parallel", "parallel", "arbitrary"), VMEM_LIMIT),
    )(a, b)


def _inproj_fwd(x2, nw, wmain, wsmall, cos128, sin128, S, li, comm=None):
    T = x2.shape[0]
    tm, tn = min(1024, S), 512
    nj, npos = NMAIN // tn, S // tm
    jq0, jk = OFF_BQ // tn, OFF_BK // tn

    def body(x_ref, nw_ref, w_ref, ws_ref, cos_ref, sin_ref, proj_ref, ps_ref, h_ref, h_scr):
        j = pl.program_id(1)

        @pl.when(j == 0)
        def _():
            x = x_ref[...]
            r = lax.rsqrt(jnp.mean(x * x, axis=-1, keepdims=True) + EPS)
            h = (x * r * nw_ref[...]).astype(BF16)
            h_scr[...] = h
            h_ref[...] = h
            ps_ref[...] = _dot(h, ws_ref[...])

        acc = _dot(h_scr[...], w_ref[...])

        def roped(c):
            xc = acc[:, LANES * c:LANES * (c + 1)]
            return (xc * cos_ref[...] + _rot_half(xc) * sin_ref[...]).astype(BF16)

        def plain(c):
            return acc[:, LANES * c:LANES * (c + 1)].astype(BF16)

        is_q = jnp.logical_or(j == jq0, j == jq0 + 1)
        is_k = j == jk

        @pl.when(is_q)
        def _():
            for c in range(4):
                proj_ref[:, LANES * c:LANES * (c + 1)] = roped(c)

        @pl.when(is_k)
        def _():
            for c in range(4):
                proj_ref[:, LANES * c:LANES * (c + 1)] = roped(c) if c < 2 else plain(c)

        @pl.when(jnp.logical_not(jnp.logical_or(is_q, is_k)))
        def _():
            proj_ref[...] = acc.astype(BF16)

    return _hosted_call(
        body, comm, f"inproj_fwd_{li}", (T // tm, nj),
        in_specs=[pl.BlockSpec((tm, D), lambda i, j: (i, 0)),
                  pl.BlockSpec((1, D), lambda i, j: (0, 0)),
                  pl.BlockSpec((D, tn), lambda i, j: (0, j)),
                  pl.BlockSpec((D, LANES), lambda i, j: (0, 0)),
                  pl.BlockSpec((tm, LANES), lambda i, j: (i % npos, 0)),
                  pl.BlockSpec((tm, LANES), lambda i, j: (i % npos, 0))],
        out_specs=[pl.BlockSpec((tm, tn), lambda i, j: (i, j)),
                   pl.BlockSpec((tm, LANES), lambda i, j: (i, 0)),
                   pl.BlockSpec((tm, D), lambda i, j: (i, 0))],
        out_shape=[jax.ShapeDtypeStruct((T, NMAIN), BF16), jax.ShapeDtypeStruct((T, LANES), F32),
                   jax.ShapeDtypeStruct((T, D), BF16)],
        scratch=[pltpu.VMEM((tm, D), BF16)], dims=("parallel", "arbitrary"),
        operands=(x2, nw, wmain, wsmall, cos128, sin128))


def _inproj_bwd_dx(dproj, wmain_t, dps16, wsmall_t, x2, nw, dxo, li, comm=None):
    T = x2.shape[0]
    tm, tk = min(1024, T), 512
    nk = NMAIN // tk
    ni = T // tm

    def body(dp_ref, wt_ref, ds_ref, wst_ref, x_ref, nw_ref, dxo_ref, dx_ref, dx16_ref, dnw_ref, acc):
        i, k = pl.program_id(0), pl.program_id(1)

        @pl.when(k == 0)
        def _():
            acc[...] = _dot(ds_ref[...], wst_ref[...])

        acc[...] += _dot(dp_ref[...], wt_ref[...])

        @pl.when(jnp.logical_and(i == 0, k == 0))
        def _():
            dnw_ref[...] = jnp.zeros_like(dnw_ref)

        @pl.when(k == nk - 1)
        def _():
            x = x_ref[...]
            r = lax.rsqrt(jnp.mean(x * x, axis=-1, keepdims=True) + EPS)
            dh = acc[...]
            g = dh * nw_ref[...]
            dx = dxo_ref[...] + r * g - x * (r * r * r) * jnp.mean(g * x, axis=-1, keepdims=True)
            dx_ref[...] = dx
            dx16_ref[...] = dx.astype(BF16)
            dnw_ref[0:1, :] += jnp.sum(dh * x * r, axis=0, keepdims=True)

    return _hosted_call(
        body, comm, f"inproj_bwd_dx_{li}", (ni, nk),
        in_specs=[pl.BlockSpec((tm, tk), lambda i, k: (i, k)),
                  pl.BlockSpec((tk, D), lambda i, k: (k, 0)),
                  pl.BlockSpec((tm, LANES), lambda i, k: (i, 0)),
                  pl.BlockSpec((LANES, D), lambda i, k: (0, 0)),
                  pl.BlockSpec((tm, D), lambda i, k: (i, 0)),
                  pl.BlockSpec((1, D), lambda i, k: (0, 0)),
                  pl.BlockSpec((tm, D), lambda i, k: (i, 0))],
        out_specs=[pl.BlockSpec((tm, D), lambda i, k: (i, 0)),
                   pl.BlockSpec((tm, D), lambda i, k: (i, 0)),
                   pl.BlockSpec((8, D), lambda i, k: (0, 0))],
        out_shape=[jax.ShapeDtypeStruct((T, D), F32), jax.ShapeDtypeStruct((T, D), BF16),
                   jax.ShapeDtypeStruct((8, D), F32)],
        scratch=[pltpu.VMEM((tm, D), F32)], dims=("arbitrary", "arbitrary"),
        operands=(dproj, wmain_t, dps16, wsmall_t, x2, nw, dxo))


def _merge_fwd(ya, yb, yc, proj, gbias, wp, wout, x2, li):
    T = x2.shape[0]
    tm = min(512, T)
    gcol = OFF_G // D

    def body(ya_ref, yb_ref, yc_ref, g0_ref, g1_ref, g2_ref, gb_ref, wp_ref, wo_ref, x_ref, xn_ref, br_ref):
        merged = jnp.zeros((tm, D), F32)
        for i, (y_ref, g_ref) in enumerate(((ya_ref, g0_ref), (yb_ref, g1_ref), (yc_ref, g2_ref))):
            br = _dot(y_ref[...], wp_ref[i])
            br_ref[i] = br.astype(BF16)
            gate = _sigmoid(g_ref[...].astype(F32) + gb_ref[i:i + 1, :])
            merged = merged + gate * br
        xn_ref[...] = x_ref[...] + _dot(merged.astype(BF16), wo_ref[...])

    row = lambda c: pl.BlockSpec((tm, D), lambda i, c=c: (i, c))
    return pl.pallas_call(
        body, name=f"merge_fwd_{li}", grid=(T // tm,),
        in_specs=[row(0), row(0), row(0), row(gcol), row(gcol + 1), row(gcol + 2),
                  pl.BlockSpec((3, D), lambda i: (0, 0)),
                  pl.BlockSpec((3, D, D), lambda i: (0, 0, 0)),
                  pl.BlockSpec((D, D), lambda i: (0, 0)),
                  row(0)],
        out_specs=[row(0), pl.BlockSpec((3, tm, D), lambda i: (0, i, 0))],
        out_shape=[jax.ShapeDtypeStruct((T, D), F32), jax.ShapeDtypeStruct((3, T, D), BF16)],
        compiler_params=_cparams(("parallel",), VMEM_LIMIT),
    )(ya, yb, yc, proj, proj, proj, gbias, wp, wout, x2)


def _merge_bwd_gates(dxo16, wout_t, br, proj, gbias, li):
    T = dxo16.shape[0]
    tm = min(512, T)
    gcol = OFF_G // D

    def body(dx_ref, wot_ref, br_ref, g0_ref, g1_ref, g2_ref, gb_ref, dbr_ref, dg_ref, mg_ref, dgb_ref):
        @pl.when(pl.program_id(0) == 0)
        def _():
            dgb_ref[...] = jnp.zeros_like(dgb_ref)

        dm = _dot(dx_ref[...], wot_ref[...])
        merged = jnp.zeros((tm, D), F32)
        for i, g_ref in enumerate((g0_ref, g1_ref, g2_ref)):
            b = br_ref[i].astype(F32)
            gate = _sigmoid(g_ref[...].astype(F32) + gb_ref[i:i + 1, :])
            merged = merged + gate * b
            dbr_ref[i] = (dm * gate).astype(BF16)
            dgate = dm * b * gate * (1.0 - gate)
            dg_ref[:, D * i:D * (i + 1)] = dgate.astype(BF16)
            dgb_ref[i:i + 1, :] += jnp.sum(dgate, axis=0, keepdims=True)
        mg_ref[...] = merged.astype(BF16)

    row = lambda c: pl.BlockSpec((tm, D), lambda i, c=c: (i, c))
    return pl.pallas_call(
        body, name=f"merge_bwd_gates_{li}", grid=(T // tm,),
        in_specs=[row(0), pl.BlockSpec((D, D), lambda i: (0, 0)),
                  pl.BlockSpec((3, tm, D), lambda i: (0, i, 0)),
                  row(gcol), row(gcol + 1), row(gcol + 2),
                  pl.BlockSpec((3, D), lambda i: (0, 0))],
        out_specs=[pl.BlockSpec((3, tm, D), lambda i: (0, i, 0)),
                   pl.BlockSpec((tm, 3 * D), lambda i: (i, 0)),
                   row(0),
                   pl.BlockSpec((8, D), lambda i: (0, 0))],
        out_shape=[jax.ShapeDtypeStruct((3, T, D), BF16), jax.ShapeDtypeStruct((T, 3 * D), BF16),
                   jax.ShapeDtypeStruct((T, D), BF16), jax.ShapeDtypeStruct((8, D), F32)],
        compiler_params=_cparams(("arbitrary",), VMEM_LIMIT),
    )(dxo16, wout_t, br, proj, proj, proj, gbias)


def _final_loss(x2, tgt, fw):
    T = x2.shape[0]
    tm = min(512, T)
    ni = T // tm

    def body(x_ref, t_ref, w_ref, dx_ref, dx16_ref, st_ref):
        i = pl.program_id(0)

        @pl.when(i == 0)
        def _():
            st_ref[...] = jnp.zeros_like(st_ref)

        x = x_ref[...]
        r = lax.rsqrt(jnp.mean(x * x, axis=-1, keepdims=True) + EPS)
        xh = x * r
        err = xh * w_ref[...] - t_ref[...]
        dy = err * (1.0 / D)
        g = dy * w_ref[...]
        dx = r * g - x * (r * r * r) * jnp.mean(g * x, axis=-1, keepdims=True)
        dx_ref[...] = dx
        dx16_ref[...] = dx.astype(BF16)
        st_ref[0:1, :] += jnp.sum(dy * xh, axis=0, keepdims=True)
        st_ref[1:2, :] += jnp.sum(err * err, axis=0, keepdims=True)

        @pl.when(i == ni - 1)
        def _():
            tot = jnp.sum(st_ref[1:2, :], axis=1, keepdims=True) * (0.5 / D)
            st_ref[2:3, :] = jnp.broadcast_to(tot, (1, D))

    row = pl.BlockSpec((tm, D), lambda i: (i, 0))
    return pl.pallas_call(
        body, name="final_loss", grid=(ni,),
        in_specs=[row, row, pl.BlockSpec((1, D), lambda i: (0, 0))],
        out_specs=[row, row, pl.BlockSpec((8, D), lambda i: (0, 0))],
        out_shape=[jax.ShapeDtypeStruct((T, D), F32), jax.ShapeDtypeStruct((T, D), BF16),
                   jax.ShapeDtypeStruct((8, D), F32)],
        compiler_params=_cparams(("arbitrary",), VMEM_LIMIT),
    )(x2, tgt, fw)


def _zgate_bwd(dy, o, proj, zcol, name):
    T = dy.shape[0]
    tm = min(512, T)

    def body(dy_ref, o_ref, z_ref, do_ref, dz_ref):
        z = z_ref[...].astype(F32)
        dyv = dy_ref[...].astype(F32)
        sg = _sigmoid(z)
        do_ref[...] = (dyv * z * sg).astype(BF16)
        dz_ref[...] = (dyv * o_ref[...].astype(F32) * sg * (1.0 + z * (1.0 - sg))).astype(BF16)

    row = lambda c: pl.BlockSpec((tm, D), lambda i, c=c: (i, c))
    sds = jax.ShapeDtypeStruct((T, D), BF16)
    return pl.pallas_call(
        body, name=name, grid=(T // tm,),
        in_specs=[row(0), row(0), row(zcol)], out_specs=[row(0), row(0)], out_shape=[sds, sds],
        compiler_params=_cparams(("parallel",), VMEM_LIMIT),
    )(dy, o, proj)


def _fox_cum(ps, fb_row, S, li):
    T = ps.shape[0]
    nb = S // LCH

    def body(ps_ref, fb_ref, cum_ref, carry):
        @pl.when(pl.program_id(1) == 0)
        def _():
            carry[...] = jnp.zeros_like(carry)

        logf = -_softplus(-(ps_ref[...] + fb_ref[...]))
        r = lax.broadcasted_iota(jnp.int32, (LCH, LCH), 0)
        c = lax.broadcasted_iota(jnp.int32, (LCH, LCH), 1)
        tri = (r >= c).astype(F32)
        cum = _dot_hi(tri, logf) + carry[0:1, :]
        cum_ref[...] = cum
        carry[0:1, :] = cum[LCH - 1:LCH, :]

    return pl.pallas_call(
        body, name=f"fox_cum_{li}", grid=(T // S, nb),
        in_specs=[pl.BlockSpec((LCH, LANES), lambda b, i: (b * nb + i, 0)),
                  pl.BlockSpec((1, LANES), lambda b, i: (0, 0))],
        out_specs=pl.BlockSpec((LCH, LANES), lambda b, i: (b * nb + i, 0)),
        out_shape=jax.ShapeDtypeStruct((T, LANES), F32),
        scratch_shapes=[pltpu.VMEM((8, LANES), F32)],
        compiler_params=_cparams(("arbitrary", "arbitrary")),
    )(ps, fb_row)


def _fox_cum_bwd(dcum, ps, fb_row, S, li):
    T = ps.shape[0]
    nb = S // LCH

    def body(dc_ref, ps_ref, fb_ref, df_ref, dfb_ref, carry):
        b, i = pl.program_id(0), pl.program_id(1)

        @pl.when(i == 0)
        def _():
            carry[...] = jnp.zeros_like(carry)

        @pl.when(jnp.logical_and(b == 0, i == 0))
        def _():
            dfb_ref[...] = jnp.zeros_like(dfb_ref)

        dc = dc_ref[...]
        r = lax.broadcasted_iota(jnp.int32, (LCH, LCH), 0)
        c = lax.broadcasted_iota(jnp.int32, (LCH, LCH), 1)
        tri = (c >= r).astype(F32)
        dlogf = _dot_hi(tri, dc) + carry[0:1, :]
        carry[0:1, :] += jnp.sum(dc, axis=0, keepdims=True)
        df = dlogf * _sigmoid(-(ps_ref[...] + fb_ref[...]))
        lane = _lane_iota()
        df = jnp.where(jnp.logical_and(lane >= NH, lane < 2 * NH), df, 0.0)
        df_ref[...] = df
        dfb_ref[0:1, :] += jnp.sum(df, axis=0, keepdims=True)

    blk = pl.BlockSpec((LCH, LANES), lambda b, i: (b * nb + nb - 1 - i, 0))
    return pl.pallas_call(
        body, name=f"fox_cum_bwd_{li}", grid=(T // S, nb),
        in_specs=[blk, blk, pl.BlockSpec((1, LANES), lambda b, i: (0, 0))],
        out_specs=[blk, pl.BlockSpec((8, LANES), lambda b, i: (0, 0))],
        out_shape=[jax.ShapeDtypeStruct((T, LANES), F32), jax.ShapeDtypeStruct((8, LANES), F32)],
        scratch_shapes=[pltpu.VMEM((8, LANES), F32)],
        compiler_params=_cparams(("arbitrary", "arbitrary")),
    )(dcum, ps, fb_row)


def _fox_blocks(S):
    bq = min(512, S)
    return bq, S // bq


def _split3(c):
    hi = c.astype(BF16).astype(F32)
    r = c - hi
    mid = r.astype(BF16).astype(F32)
    return hi, mid, (r - mid).astype(BF16).astype(F32)


def _augment(x, parts, key_side, hh):
    lane = _lane_iota()
    b0 = HD if hh == 0 else 0
    p0, o0 = (b0 + 3, b0) if key_side else (b0, b0 + 3)
    out = jnp.where(jnp.logical_and(lane >= o0, lane < o0 + 3), 1.0, x)
    for t in range(3):
        out = jnp.where(lane == p0 + t, parts[t], out)
    return out.astype(BF16)


def _fox_fwd(proj, cum_col, S, li):
    T = proj.shape[0]
    B = T // S
    bq, nq = _fox_blocks(S)
    qc, kc, vc, zc = OFF_CQ // LANES, OFF_CK // LANES, OFF_CV // LANES, OFF_CZ // LANES

    def body(q_ref, k_ref, v_ref, z_ref, cc_ref, y_ref, o_ref, lse_ref, kaug):
        i = pl.program_id(2)
        m0 = _lane_iota() < HD

        @pl.when(i == 0)
        def _():
            kf = k_ref[...].astype(F32)
            for hh in range(2):
                kaug[hh] = _augment(kf, _split3(-cc_ref[0, hh]), True, hh)

        q2 = q_ref[...].astype(F32) * SCALE
        rows_q = pl.ds(pl.multiple_of(i * bq, bq), bq)
        row = lax.broadcasted_iota(jnp.int32, (bq, bq), 0)
        col = lax.broadcasted_iota(jnp.int32, (bq, bq), 1)
        outs = []
        for hh in range(2):
            sel = m0 if hh == 0 else jnp.logical_not(m0)
            qa = _augment(jnp.where(sel, q2, 0.0), _split3(cc_ref[0, hh, rows_q, :]), False, hh)

            def step(j, carry, masked, hh=hh, qa=qa):
                m, l, acc = carry
                start = pl.multiple_of(j * bq, bq)
                v2 = v_ref[pl.ds(start, bq), :]
                s = _dot_nt(qa, kaug[hh, pl.ds(start, bq), :])
                if masked:
                    s = jnp.where(row >= col, s, NEG)
                mn = jnp.maximum(m, jnp.max(s, axis=1, keepdims=True))
                alpha = jnp.exp(m - mn)
                p = jnp.exp(s - mn)
                l = alpha * l + jnp.sum(p, axis=1, keepdims=True)
                acc = alpha * acc + _dot(p.astype(BF16), v2)
                return mn, l, acc

            init = (jnp.full((bq, 1), NEG, F32), jnp.zeros((bq, 1), F32), jnp.zeros((bq, LANES), F32))
            carry = lax.fori_loop(0, i, functools.partial(step, masked=False), init)
            m, l, acc = step(i, carry, True)
            outs.append(acc / l)
            lse_ref[0, hh] = m + jnp.log(l)
        o2 = jnp.where(m0, outs[0], outs[1])
        z = z_ref[...].astype(F32)
        o_ref[...] = o2.astype(BF16)
        y_ref[...] = (o2 * z * _sigmoid(z)).astype(BF16)

    qblk = lambda c: pl.BlockSpec((bq, LANES), lambda b, p, i, c=c: (b * nq + i, c + p))
    sblk = lambda c: pl.BlockSpec((S, LANES), lambda b, p, i, c=c: (b, c + p))
    return pl.pallas_call(
        body, name=f"fox_fwd_{li}", grid=(B, NH // 2, nq),
        in_specs=[qblk(qc), sblk(kc), sblk(vc), qblk(zc),
                  pl.BlockSpec((1, 2, S, 1), lambda b, p, i: (b, p, 0, 0))],
        out_specs=[qblk(0), qblk(0), pl.BlockSpec((1, 2, bq, 1), lambda b, p, i: (b, p, i, 0))],
        out_shape=[jax.ShapeDtypeStruct((T, D), BF16), jax.ShapeDtypeStruct((T, D), BF16),
                   jax.ShapeDtypeStruct((B, NH, S, 1), F32)],
        scratch_shapes=[pltpu.VMEM((2, S, LANES), BF16)],
        compiler_params=_cparams(("parallel", "parallel", "arbitrary"), VMEM_LIMIT),
    )(proj, proj, proj, proj, cum_col)


def _fox_bwd(proj, do, o, cum_col, lse, S, li):
    T = proj.shape[0]
    B = T // S
    bq, nq = _fox_blocks(S)
    qc, kc, vc = OFF_CQ // LANES, OFF_CK // LANES, OFF_CV // LANES

    def body(q_ref, k_ref, v_ref, do_ref, o_ref, cc_ref, lse_ref, dq_ref, dk_ref, dv_ref, dc_ref, dr_ref,
             dq_scr, dr_scr, qaug):
        j = pl.program_id(2)
        m0 = _lane_iota() < HD

        @pl.when(j == 0)
        def _():
            dq_scr[...] = jnp.zeros_like(dq_scr)
            dr_scr[...] = jnp.zeros_like(dr_scr)
            qf = q_ref[...].astype(F32) * SCALE
            for hh in range(2):
                sel = m0 if hh == 0 else jnp.logical_not(m0)
                qaug[hh] = _augment(jnp.where(sel, qf, 0.0), _split3(cc_ref[0, hh] - lse_ref[0, hh]), False, hh)

        k2 = k_ref[...]
        v2 = v_ref[...]
        zk = jnp.zeros_like(k2)
        kh = (jnp.where(m0, k2, zk), jnp.where(m0, zk, k2))
        kf = k2.astype(F32)
        rows_k = pl.ds(pl.multiple_of(j * bq, bq), bq)
        ka = [_augment(kf, _split3(-cc_ref[0, hh, rows_k, :]), True, hh) for hh in range(2)]
        row = lax.broadcasted_iota(jnp.int32, (bq, bq), 0)
        col = lax.broadcasted_iota(jnp.int32, (bq, bq), 1)

        def step(i, carry, masked):
            dk, dv, dc0, dc1 = carry
            dcs = [dc0, dc1]
            start = pl.multiple_of(i * bq, bq)
            q2 = q_ref[pl.ds(start, bq), :]
            do2 = do_ref[pl.ds(start, bq), :]
            prod = do2.astype(F32) * o_ref[pl.ds(start, bq), :].astype(F32)
            zq = jnp.zeros_like(q2)
            dq = jnp.zeros((bq, LANES), F32)
            for hh in range(2):
                sel = m0 if hh == 0 else jnp.logical_not(m0)
                qh = jnp.where(sel, q2, zq)
                doh = jnp.where(sel, do2, zq)
                delta = _head_sum(prod, hh)
                s = _dot_nt(qaug[hh, pl.ds(start, bq), :], ka[hh])
                if masked:
                    s = jnp.where(row >= col, s, NEG)
                p = jnp.exp(s)
                dp = _dot_nt(doh, v2)
                ds = p * (dp - delta)
                dcs[hh] = dcs[hh] - jnp.sum(ds, axis=0, keepdims=True)
                dr_scr[hh, pl.ds(start, bq), :] += jnp.sum(ds, axis=1, keepdims=True)
                dsb = ds.astype(BF16)
                dv = dv + _dot_tn(p.astype(BF16), doh)
                dk = dk + _dot_tn(dsb, qh)
                dq = dq + _dot(dsb, kh[hh])
            dq_scr[pl.ds(start, bq), :] += dq
            return dk, dv, dcs[0], dcs[1]

        zero = jnp.zeros((bq, LANES), F32)
        zrow = jnp.zeros((1, bq), F32)
        carry = step(j, (zero, zero, zrow, zrow), True)
        dk, dv, dc0, dc1 = lax.fori_loop(j + 1, nq, functools.partial(step, masked=False), carry)
        dk_ref[...] = (dk * SCALE).astype(BF16)
        dv_ref[...] = dv.astype(BF16)
        dc_ref[0, 0, 0] = dc0
        dc_ref[0, 1, 0] = dc1

        @pl.when(j == nq - 1)
        def _():
            dq_ref[...] = (dq_scr[...] * SCALE).astype(BF16)
            dr_ref[0] = dr_scr[...]

    sblk = lambda c: pl.BlockSpec((S, LANES), lambda b, p, j, c=c: (b, c + p))
    kblk = lambda c: pl.BlockSpec((bq, LANES), lambda b, p, j, c=c: (b * nq + j, c + p))
    col_spec = pl.BlockSpec((1, 2, S, 1), lambda b, p, j: (b, p, 0, 0))
    return pl.pallas_call(
        body, name=f"fox_bwd_{li}", grid=(B, NH // 2, nq),
        in_specs=[sblk(qc), kblk(kc), kblk(vc), sblk(0), sblk(0), col_spec, col_spec],
        out_specs=[sblk(0), kblk(0), kblk(0), pl.BlockSpec((1, 2, 1, 1, bq), lambda b, p, j: (b, p, j, 0, 0)),
                   col_spec],
        out_shape=[jax.ShapeDtypeStruct((T, D), BF16), jax.ShapeDtypeStruct((T, D), BF16),
                   jax.ShapeDtypeStruct((T, D), BF16), jax.ShapeDtypeStruct((B, NH, nq, 1, bq), F32),
                   jax.ShapeDtypeStruct((B, NH, S, 1), F32)],
        scratch_shapes=[pltpu.VMEM((S, LANES), F32), pltpu.VMEM((2, S, 1), F32), pltpu.VMEM((2, S, LANES), BF16)],
        compiler_params=_cparams(("parallel", "parallel", "arbitrary"), VMEM_LIMIT),
    )(proj, proj, proj, do, o, cum_col, lse)


def _swa_blocks(S):
    bq = min(512, S)
    return bq, S // bq, bq // LCH


def _dup_head(xw, kvl):
    m0 = _lane_iota() < HD
    a = jnp.where(m0 if kvl == 0 else jnp.logical_not(m0), xw, 0.0)
    return (a + pltpu.roll(a, HD, 1)).astype(BF16)


def _band(same_block):
    r = lax.broadcasted_iota(jnp.int32, (LCH, LCH), 0)
    c = lax.broadcasted_iota(jnp.int32, (LCH, LCH), 1)
    return (c <= r) if same_block else (c > r)


def _stack_heads(ref, rows, kvl):
    m0 = _lane_iota() < HD
    parts = []
    for ch in (2 * kvl, 2 * kvl + 1):
        x = ref[rows, LANES * ch:LANES * (ch + 1)]
        parts += [jnp.where(m0, x, jnp.zeros_like(x)), jnp.where(m0, jnp.zeros_like(x), x)]
    return jnp.concatenate(parts, axis=0)


def _stack_delta(do_ref, o_ref, rows, kvl, scale=None):
    parts = []
    for ch in (2 * kvl, 2 * kvl + 1):
        lanes = slice(LANES * ch, LANES * (ch + 1))
        prod = do_ref[rows, lanes].astype(F32) * o_ref[rows, lanes].astype(F32)
        parts += [_head_sum(prod, 0), _head_sum(prod, 1)]
    out = jnp.concatenate(parts, axis=0)
    return out if scale is None else out * scale


def _stack_cols(ref, rows, kvl):
    return jnp.concatenate([ref[0, 4 * kvl + t, rows, :] for t in range(4)], axis=0)


def _swa_fwd(proj, sinks, S, li):
    T = proj.shape[0]
    B = T // S
    bq, nq, nsub = _swa_blocks(S)
    nrow = S // LCH
    qc, zc, kc, vc = OFF_BQ // 512, OFF_BZ // 512, OFF_BK // LANES, OFF_BV // LANES

    def body(sk_ref, q_ref, z_ref, kp_ref, kc_ref, vp_ref, vc_ref, y_ref, o_ref, lse_ref):
        c, i = pl.program_id(0), pl.program_id(2)
        m0 = _lane_iota() < HD
        kw = jnp.concatenate([kp_ref[...].astype(F32), kc_ref[...].astype(F32)], axis=0)
        vw = jnp.concatenate([vp_ref[...].astype(F32), vc_ref[...].astype(F32)], axis=0)
        kd = (_dup_head(kw, 0), _dup_head(kw, 1))
        vd = (_dup_head(vw, 0), _dup_head(vw, 1))
        valid = jnp.concatenate([_band(False), _band(True)], axis=1)
        col = lax.broadcasted_iota(jnp.int32, (LCH, 2 * LCH), 1)
        valid_first = jnp.logical_and(valid, jnp.logical_or(col >= LCH, i > 0))
        valid4 = jnp.concatenate([valid] * 4, axis=0)
        valid4_first = jnp.concatenate([valid_first] * 4, axis=0)
        for r in range(nsub):
            rows = slice(LCH * r, LCH * (r + 1))
            msk = valid4_first if r == 0 else valid4
            for kvl in range(2):
                kwin = kd[kvl][LCH * r:LCH * (r + 2)]
                vwin = vd[kvl][LCH * r:LCH * (r + 2)]
                qs = _stack_heads(q_ref, rows, kvl)
                sink = jnp.concatenate([jnp.full((LCH, 1), sk_ref[8 * c + 4 * kvl + t], F32) for t in range(4)], axis=0)
                s = jnp.where(msk, _dot_nt(qs, kwin) * SCALE, NEG)
                m = jnp.maximum(jnp.max(s, axis=1, keepdims=True), sink)
                p = jnp.exp(s - m)
                l = jnp.sum(p, axis=1, keepdims=True) + jnp.exp(sink - m)
                os_ = _dot(p.astype(BF16), vwin) / l
                lse = m + jnp.log(l)
                for t in range(4):
                    lse_ref[0, 4 * kvl + t, rows, :] = lse[LCH * t:LCH * (t + 1)]
                for u in range(2):
                    lanes = slice(LANES * (2 * kvl + u), LANES * (2 * kvl + u + 1))
                    o2 = jnp.where(m0, os_[LCH * 2 * u:LCH * (2 * u + 1)], os_[LCH * (2 * u + 1):LCH * (2 * u + 2)])
                    z = z_ref[rows, lanes].astype(F32)
                    o_ref[rows, lanes] = o2.astype(BF16)
                    y_ref[rows, lanes] = (o2 * z * _sigmoid(z)).astype(BF16)

    wide = lambda cc: pl.BlockSpec((bq, 512), lambda c, b, i, cc=cc: (b * nq + i, cc + c))
    cur = lambda cc: pl.BlockSpec((bq, LANES), lambda c, b, i, cc=cc: (b * nq + i, cc + c))
    prev = lambda cc: pl.BlockSpec((LCH, LANES), lambda c, b, i, cc=cc: (b * nrow + jnp.maximum(i * nsub - 1, 0), cc + c))
    return pl.pallas_call(
        body, name=f"swa_fwd_{li}", grid=(2, B, nq),
        in_specs=[pl.BlockSpec(memory_space=pltpu.SMEM), wide(qc), wide(zc), prev(kc), cur(kc), prev(vc), cur(vc)],
        out_specs=[wide(0), wide(0), pl.BlockSpec((1, 8, bq, 1), lambda c, b, i: (b, c, i, 0))],
        out_shape=[jax.ShapeDtypeStruct((T, D), BF16), jax.ShapeDtypeStruct((T, D), BF16),
                   jax.ShapeDtypeStruct((B, NH, S, 1), F32)],
        compiler_params=_cparams(("parallel", "parallel", "parallel"), VMEM_LIMIT),
    )(sinks, proj, proj, proj, proj, proj, proj)


def _swa_bwd_dq(proj, do, o, lse, sinks, cos128, sin128, S, li):
    T = proj.shape[0]
    B = T // S
    bq, nq, nsub = _swa_blocks(S)
    nrow = S // LCH
    qc, kc, vc = OFF_BQ // 512, OFF_BK // LANES, OFF_BV // LANES

    def body(sk_ref, q_ref, do_ref, o_ref, lse_ref, kp_ref, kc_ref, vp_ref, vc_ref, cos_ref, sin_ref, dq_ref, dsk_ref):
        c, b, i = pl.program_id(0), pl.program_id(1), pl.program_id(2)

        @pl.when(jnp.logical_and(b == 0, i == 0))
        def _():
            dsk_ref[...] = jnp.zeros_like(dsk_ref)

        m0 = _lane_iota() < HD
        kw = jnp.concatenate([kp_ref[...].astype(F32), kc_ref[...].astype(F32)], axis=0)
        vw = jnp.concatenate([vp_ref[...].astype(F32), vc_ref[...].astype(F32)], axis=0)
        kd = (_dup_head(kw, 0), _dup_head(kw, 1))
        vd = (_dup_head(vw, 0), _dup_head(vw, 1))
        valid = jnp.concatenate([_band(False), _band(True)], axis=1)
        col = lax.broadcasted_iota(jnp.int32, (LCH, 2 * LCH), 1)
        valid_first = jnp.logical_and(valid, jnp.logical_or(col >= LCH, i > 0))
        dsk = [jnp.zeros((1, 1), F32) for _ in range(8)]
        valid4 = jnp.concatenate([valid] * 4, axis=0)
        valid4_first = jnp.concatenate([valid_first] * 4, axis=0)
        for r in range(nsub):
            rows = slice(LCH * r, LCH * (r + 1))
            msk = valid4_first if r == 0 else valid4
            for kvl in range(2):
                kwin = kd[kvl][LCH * r:LCH * (r + 2)]
                vwin = vd[kvl][LCH * r:LCH * (r + 2)]
                qs = _stack_heads(q_ref, rows, kvl)
                dos = _stack_heads(do_ref, rows, kvl)
                delta = _stack_delta(do_ref, o_ref, rows, kvl)
                lse = _stack_cols(lse_ref, rows, kvl)
                sink = jnp.concatenate([jnp.full((LCH, 1), sk_ref[8 * c + 4 * kvl + t], F32) for t in range(4)], axis=0)
                s = jnp.where(msk, _dot_nt(qs, kwin) * SCALE, NEG)
                p = jnp.exp(s - lse)
                ds = p * (_dot_nt(dos, vwin) - delta)
                dqs = _dot(ds.astype(BF16), kwin) * SCALE
                dsink = jnp.exp(sink - lse) * delta
                for t in range(4):
                    hl = 4 * kvl + t
                    dsk[hl] = dsk[hl] - jnp.sum(dsink[LCH * t:LCH * (t + 1)], axis=0, keepdims=True)
                for u in range(2):
                    lanes = slice(LANES * (2 * kvl + u), LANES * (2 * kvl + u + 1))
                    dq2 = jnp.where(m0, dqs[LCH * 2 * u:LCH * (2 * u + 1)], dqs[LCH * (2 * u + 1):LCH * (2 * u + 2)])
                    dq2 = dq2 * cos_ref[rows, :] - _rot_half(dq2) * sin_ref[rows, :]
                    dq_ref[rows, lanes] = dq2.astype(BF16)
        for hl in range(8):
            dsk_ref[0, hl:hl + 1, :] += jnp.broadcast_to(dsk[hl], (1, LANES))

    wide = lambda cc: pl.BlockSpec((bq, 512), lambda c, b, i, cc=cc: (b * nq + i, cc + c))
    cur = lambda cc: pl.BlockSpec((bq, LANES), lambda c, b, i, cc=cc: (b * nq + i, cc + c))
    prev = lambda cc: pl.BlockSpec((LCH, LANES), lambda c, b, i, cc=cc: (b * nrow + jnp.maximum(i * nsub - 1, 0), cc + c))
    pos = pl.BlockSpec((bq, LANES), lambda c, b, i: (i, 0))
    return pl.pallas_call(
        body, name=f"swa_bwd_dq_{li}", grid=(2, B, nq),
        in_specs=[pl.BlockSpec(memory_space=pltpu.SMEM), wide(qc), wide(0), wide(0),
                  pl.BlockSpec((1, 8, bq, 1), lambda c, b, i: (b, c, i, 0)),
                  prev(kc), cur(kc), prev(vc), cur(vc), pos, pos],
        out_specs=[wide(0), pl.BlockSpec((1, 8, LANES), lambda c, b, i: (c, 0, 0))],
        out_shape=[jax.ShapeDtypeStruct((T, D), BF16), jax.ShapeDtypeStruct((2, 8, LANES), F32)],
        compiler_params=_cparams(("arbitrary", "arbitrary", "arbitrary"), VMEM_LIMIT),
    )(sinks, proj, do, o, lse, proj, proj, proj, proj, cos128, sin128)


def _swa_bwd_dkv(proj, do, o, lse, cos128, sin128, S, li):
    T = proj.shape[0]
    B = T // S
    bk, nk, nsub = _swa_blocks(S)
    nrow = S // LCH
    qc, kc, vc = OFF_BQ // 512, OFF_BK // LANES, OFF_BV // LANES

    def body(q_ref, qn_ref, do_ref, don_ref, o_ref, on_ref, lse_ref, lsen_ref, k_ref, v_ref, cos_ref, sin_ref,
             dk_ref, dv_ref):
        j = pl.program_id(2)
        m0 = _lane_iota() < HD
        has_next = (j < nk - 1).astype(F32)
        kf = k_ref[...].astype(F32)
        vf = v_ref[...].astype(F32)
        kd = (_dup_head(kf, 0), _dup_head(kf, 1))
        vd = (_dup_head(vf, 0), _dup_head(vf, 1))
        masks4 = (jnp.concatenate([_band(True)] * 4, axis=0), jnp.concatenate([_band(False)] * 4, axis=0))
        for kr in range(nsub):
            krows = slice(LCH * kr, LCH * (kr + 1))
            dk = jnp.zeros((LCH, LANES), F32)
            dv = jnp.zeros((LCH, LANES), F32)
            for dq_blk in range(2):
                rq = kr + dq_blk
                nxt = rq == nsub
                qrows = slice(0, LCH) if nxt else slice(LCH * rq, LCH * (rq + 1))
                qr, dor, orr, lr = (qn_ref, don_ref, on_ref, lsen_ref) if nxt else (q_ref, do_ref, o_ref, lse_ref)
                for kvl in range(2):
                    qs = _stack_heads(qr, qrows, kvl)
                    dos = _stack_heads(dor, qrows, kvl)
                    delta = _stack_delta(dor, orr, qrows, kvl, has_next if nxt else None)
                    if nxt:
                        dos = (dos.astype(F32) * has_next).astype(BF16)
                    s = jnp.where(masks4[dq_blk], _dot_nt(qs, kd[kvl][krows]) * SCALE, NEG)
                    p = jnp.exp(s - _stack_cols(lr, qrows, kvl))
                    ds = p * (_dot_nt(dos, vd[kvl][krows]) - delta)
                    dvc = _dot_tn(p.astype(BF16), dos)
                    dkc = _dot_tn(ds.astype(BF16), qs) * SCALE
                    own = m0 if kvl == 0 else jnp.logical_not(m0)
                    dv = dv + jnp.where(own, dvc + pltpu.roll(dvc, HD, 1), 0.0)
                    dk = dk + jnp.where(own, dkc + pltpu.roll(dkc, HD, 1), 0.0)
            dk = dk * cos_ref[krows, :] - _rot_half(dk) * sin_ref[krows, :]
            dk_ref[krows, :] = dk.astype(BF16)
            dv_ref[krows, :] = dv.astype(BF16)

    wide = lambda cc: pl.BlockSpec((bk, 512), lambda c, b, j, cc=cc: (b * nk + j, cc + c))
    nxt = lambda cc: pl.BlockSpec((LCH, 512), lambda c, b, j, cc=cc: (b * nrow + jnp.minimum((j + 1) * nsub, nrow - 1), cc + c))
    cur = lambda cc: pl.BlockSpec((bk, LANES), lambda c, b, j, cc=cc: (b * nk + j, cc + c))
    pos = pl.BlockSpec((bk, LANES), lambda c, b, j: (j, 0))
    return pl.pallas_call(
        body, name=f"swa_bwd_dkv_{li}", grid=(2, B, nk),
        in_specs=[wide(qc), nxt(qc), wide(0), nxt(0), wide(0), nxt(0),
                  pl.BlockSpec((1, 8, bk, 1), lambda c, b, j: (b, c, j, 0)),
                  pl.BlockSpec((1, 8, LCH, 1), lambda c, b, j: (b, c, jnp.minimum((j + 1) * nsub, nrow - 1), 0)),
                  cur(kc), cur(vc), pos, pos],
        out_specs=[cur(0), cur(0)],
        out_shape=[jax.ShapeDtypeStruct((T, 2 * LANES), BF16), jax.ShapeDtypeStruct((T, 2 * LANES), BF16)],
        compiler_params=_cparams(("parallel", "parallel", "parallel"), VMEM_LIMIT),
    )(proj, proj, do, do, o, o, lse, lse, proj, proj, cos128, sin128)


HALO = 16


def _shift_matrices():
    r = lax.broadcasted_iota(jnp.int32, (3 * LCH, LCH + HALO), 0)
    c = lax.broadcasted_iota(jnp.int32, (3 * LCH, LCH + HALO), 1)
    t, d = r % LCH, r // LCH + 1
    return (c == HALO + t - d).astype(BF16), (c == t + d).astype(BF16)


def _ssm_chunk_pre(prev16, cur16, first, sdn_ref, cw_ref, cb_ref, ps, dtb, alog):
    ext16 = jnp.concatenate([jnp.where(first, jnp.zeros_like(prev16), prev16), cur16], axis=0)
    sh = _dot(sdn_ref[...], ext16)
    pre = cb_ref[...] + cw_ref[3:4, :] * cur16.astype(F32)
    for d in range(1, 4):
        pre = pre + cw_ref[3 - d:4 - d, :] * sh[LCH * (d - 1):LCH * d]
    sg = _sigmoid(pre)
    dt = _softplus(ps + dtb)
    a = -jnp.exp(alog)
    r = lax.broadcasted_iota(jnp.int32, (LCH, LCH), 0)
    c = lax.broadcasted_iota(jnp.int32, (LCH, LCH), 1)
    acum = _dot_hi((r >= c).astype(F32), dt * a)
    return pre, sg, dt, a, acum, sh


def _pairsel(v, p):
    return jnp.where(_lane_iota() < HD, v[:, 2 * p:2 * p + 1], v[:, 2 * p + 1:2 * p + 2])


def _decay(acum, acum_t, h):
    r = lax.broadcasted_iota(jnp.int32, (LCH, LCH), 0)
    c = lax.broadcasted_iota(jnp.int32, (LCH, LCH), 1)
    causal = r >= c
    seg = acum[:, h:h + 1] - acum_t[h:h + 1, :]
    return jnp.where(causal, jnp.exp(jnp.where(causal, seg, 0.0)), 0.0)


def _ssm_pair_fwd(p, x, dt, acum, acum_t, e_all, w_all, cd, cb_g, b_g, c_g, hprev, dsk_ref):
    m0 = _lane_iota() < HD
    lanes = slice(LANES * p, LANES * (p + 1))
    x2 = x[:, lanes]
    dt2 = _pairsel(dt, p)
    xdt2 = x2 * dt2
    xdtb = xdt2.astype(BF16)
    lms, ms, yds = [], [], []
    for hh in range(2):
        lm = _decay(acum, acum_t, 2 * p + hh)
        mm = cb_g * lm
        lms.append(lm)
        ms.append(mm)
        yds.append(_dot(mm.astype(BF16), xdtb))
    yd2 = jnp.where(m0, yds[0], yds[1])
    w2 = _pairsel(w_all, p)
    xw = (xdt2 * w2).astype(BF16)
    s2 = _dot_tn(xw, b_g)
    z2 = _dot_nt(c_g, hprev.astype(BF16))
    e2 = _pairsel(e_all, p)
    rowsel = lax.broadcasted_iota(jnp.int32, (LANES, 1), 0) < HD
    cdcol = jnp.where(rowsel, cd[:, 2 * p:2 * p + 1], cd[:, 2 * p + 1:2 * p + 2])
    y2 = yd2 + z2 * e2 + dsk_ref[:, lanes] * x2
    return dict(x2=x2, dt2=dt2, xdt2=xdt2, xdtb=xdtb, lms=lms, ms=ms, yd2=yd2, w2=w2, xw=xw, s2=s2, z2=z2, e2=e2,
                cdcol=cdcol, y2=y2)


def _ssm_specs(S, rev):
    nc = S // LCH
    ch = (lambda c: nc - 1 - c) if rev else (lambda c: c)
    prev = pl.BlockSpec((HALO, 2 * D), lambda b, c: (jnp.maximum(b * (S // HALO) + ch(c) * (LCH // HALO) - 1, 0), 0))
    cur = pl.BlockSpec((LCH, 2 * D), lambda b, c: (b * nc + ch(c), 0))
    zed = pl.BlockSpec((LCH, D), lambda b, c: (b * nc + ch(c), OFF_AZ // D))
    row = pl.BlockSpec((LCH, D), lambda b, c: (b * nc + ch(c), 0))
    psb = pl.BlockSpec((LCH, LANES), lambda b, c: (b * nc + ch(c), 0))
    hpb = pl.BlockSpec((1, 1, NH // 2, LANES, NST), lambda b, c: (b, ch(c), 0, 0, 0))
    const = lambda r, w: pl.BlockSpec((r, w), lambda b, c: (0, 0))
    return nc, prev, cur, zed, row, psb, hpb, const


def _ssm_fwd(proj, ps, cw, cb, dtb, alog, dsk, nw, S, li):
    T = proj.shape[0]
    B = T // S
    nc, prev, cur, zed, row, psb, hpb, const = _ssm_specs(S, False)

    def body(prev_ref, cur_ref, z_ref, ps_ref, sdn_ref, cw_ref, cb_ref, dtb_ref, alog_ref, dsk_ref, nw_ref,
             ya_ref, hp_ref, h_scr):
        c = pl.program_id(1)

        @pl.when(c == 0)
        def _():
            h_scr[...] = jnp.zeros_like(h_scr)

        pre, sg, dt, a, acum, _ = _ssm_chunk_pre(prev_ref[...], cur_ref[...], c == 0, sdn_ref, cw_ref, cb_ref,
                                                 ps_ref[...], dtb_ref[...], alog_ref[...])
        act = pre * sg
        acum_t = acum.T
        e_all = jnp.exp(acum)
        last = acum[LCH - 1:LCH, :]
        w_all = jnp.exp(last - acum)
        cd = jnp.exp(last)
        x = act[:, :D]
        for g in range(NGRP):
            b_g = act[:, D + NST * g:D + NST * (g + 1)].astype(BF16)
            c_g = act[:, D + NGRP * NST + NST * g:D + NGRP * NST + NST * (g + 1)].astype(BF16)
            cb_g = _dot_nt(c_g, b_g)
            ygs = []
            for p in (2 * g, 2 * g + 1):
                hprev = h_scr[p]
                hp_ref[0, 0, p] = hprev
                f = _ssm_pair_fwd(p, x, dt, acum, acum_t, e_all, w_all, cd, cb_g, b_g, c_g, hprev, dsk_ref)
                h_scr[p] = hprev * f["cdcol"] + f["s2"]
                z2 = z_ref[:, LANES * p:LANES * (p + 1)].astype(F32)
                ygs.append(f["y2"] * z2 * _sigmoid(z2))
            yg = jnp.concatenate(ygs, axis=1)
            r = lax.rsqrt(jnp.mean(yg * yg, axis=1, keepdims=True) + EPS)
            ya_ref[:, 2 * LANES * g:2 * LANES * (g + 1)] = (yg * r * nw_ref[:, 2 * LANES * g:2 * LANES * (g + 1)]).astype(BF16)

    return pl.pallas_call(
        body, name=f"ssm_fwd_{li}", grid=(B, nc),
        in_specs=[prev, cur, zed, psb, const(3 * LCH, LCH + HALO), const(4, 2 * D), const(1, 2 * D), const(1, LANES),
                  const(1, LANES), const(1, D), const(1, D)],
        out_specs=[row, hpb],
        out_shape=[jax.ShapeDtypeStruct((T, D), BF16), jax.ShapeDtypeStruct((B, nc, NH // 2, LANES, NST), F32)],
        scratch_shapes=[pltpu.VMEM((NH // 2, LANES, NST), F32)],
        compiler_params=_cparams(("arbitrary", "arbitrary"), VMEM_LIMIT),
    )(proj, proj, proj, ps, _shift_matrices()[0], cw, cb, dtb, alog, dsk, nw)


def _ssm_bwd(proj, ps, hp, dya, cw, cb, dtb, alog, dsk, nw, S, li, comm=None):
    T = proj.shape[0]
    B = T // S
    nc, prev, cur, zed, row, psb, hpb, const = _ssm_specs(S, True)

    def body(prev_ref, cur_ref, z_ref, ps_ref, hp_ref, dy_ref, sdn_ref, sup_ref, cw_ref, cb_ref, dtb_ref, alog_ref,
             dsk_ref, nw_ref, dxbc_ref, dz_ref, dps_ref, pgw_ref, pg1_ref, pgh_ref, dh_scr, dhead, dact):
        b, cc = pl.program_id(0), pl.program_id(1)
        c = nc - 1 - cc

        @pl.when(jnp.logical_and(b == 0, cc == 0))
        def _():
            pgw_ref[...] = jnp.zeros_like(pgw_ref)
            pg1_ref[...] = jnp.zeros_like(pg1_ref)
            pgh_ref[...] = jnp.zeros_like(pgh_ref)

        @pl.when(cc == 0)
        def _():
            dh_scr[...] = jnp.zeros_like(dh_scr)
            dhead[...] = jnp.zeros_like(dhead)

        psv = ps_ref[...]
        cur16 = cur_ref[...]
        pre, sg, dt, a, acum, sh = _ssm_chunk_pre(prev_ref[...], cur16, c == 0, sdn_ref, cw_ref, cb_ref, psv,
                                                  dtb_ref[...], alog_ref[...])
        act = pre * sg
        acum_t = acum.T
        e_all = jnp.exp(acum)
        last = acum[LCH - 1:LCH, :]
        w_all = jnp.exp(last - acum)
        cd = jnp.exp(last)
        x = act[:, :D]
        lane = _lane_iota()
        m0 = lane < HD
        rowsel = lax.broadcasted_iota(jnp.int32, (LANES, 1), 0) < HD
        is_last_row = lax.broadcasted_iota(jnp.int32, (LCH, 1), 0) == LCH - 1
        dacum_all = jnp.zeros((LCH, LANES), F32)
        ddt_all = jnp.zeros((LCH, LANES), F32)
        dd_row = jnp.zeros((1, LANES), F32)
        for g in range(NGRP):
            b_g = act[:, D + NST * g:D + NST * (g + 1)].astype(BF16)
            c_g = act[:, D + NGRP * NST + NST * g:D + NGRP * NST + NST * (g + 1)].astype(BF16)
            cb_g = _dot_nt(c_g, b_g)
            pairs = (2 * g, 2 * g + 1)
            fs, hps, zs, ygs = [], [], [], []
            for p in pairs:
                hprev = hp_ref[0, 0, p]
                f = _ssm_pair_fwd(p, x, dt, acum, acum_t, e_all, w_all, cd, cb_g, b_g, c_g, hprev, dsk_ref)
                z2 = z_ref[:, LANES * p:LANES * (p + 1)].astype(F32)
                fs.append(f)
                hps.append(hprev)
                zs.append(z2)
                ygs.append(f["y2"] * z2 * _sigmoid(z2))
            gl = slice(2 * LANES * g, 2 * LANES * (g + 1))
            yg = jnp.concatenate(ygs, axis=1)
            r = lax.rsqrt(jnp.mean(yg * yg, axis=1, keepdims=True) + EPS)
            dyn = dy_ref[:, gl].astype(F32)
            gg = dyn * nw_ref[:, gl]
            dyg = r * gg - yg * (r * r * r) * jnp.mean(gg * yg, axis=1, keepdims=True)
            pg1_ref[0:1, gl] += jnp.sum(dyn * yg * r, axis=0, keepdims=True)
            dg_g = jnp.zeros((LCH, LCH), F32)
            db_g = jnp.zeros((LCH, NST), F32)
            dc_g = jnp.zeros((LCH, NST), F32)
            for idx, p in enumerate(pairs):
                f, hprev, z2 = fs[idx], hps[idx], zs[idx]
                lanes = slice(LANES * p, LANES * (p + 1))
                dyg2 = dyg[:, LANES * idx:LANES * (idx + 1)]
                sgz = _sigmoid(z2)
                dy2 = dyg2 * z2 * sgz
                dz_ref[:, lanes] = (dyg2 * f["y2"] * sgz * (1.0 + z2 * (1.0 - sgz))).astype(BF16)
                x2, dt2, xdt2, xdtb, w2, e2, z2m = f["x2"], f["dt2"], f["xdt2"], f["xdtb"], f["w2"], f["e2"], f["z2"]
                dx2 = dsk_ref[:, lanes] * dy2
                dyx = dy2 * x2
                dxdt2 = jnp.zeros((LCH, LANES), F32)
                diag_cols = []
                for hh in range(2):
                    sel = m0 if hh == 0 else jnp.logical_not(m0)
                    dyb = jnp.where(sel, dy2, 0.0).astype(BF16)
                    dm = _dot_nt(dyb, xdtb)
                    dg_g = dg_g + dm * f["lms"][hh]
                    dxdt2 = dxdt2 + _dot_tn(f["ms"][hh].astype(BF16), dyb)
                    em = dm * f["ms"][hh]
                    diag_cols.append(jnp.sum(em, axis=1, keepdims=True) - jnp.sum(em.T, axis=1, keepdims=True))
                dz2m = dy2 * e2
                t_off = dz2m * z2m
                dc_g = dc_g + _dot(dz2m.astype(BF16), hprev.astype(BF16))
                dhprev = _dot_tn(dz2m.astype(BF16), c_g)
                dhn = dh_scr[p]
                dhnb = dhn.astype(BF16)
                dhprev = dhprev + dhn * f["cdcol"]
                t_h = dhn * hprev
                dxw2 = _dot_nt(b_g, dhnb)
                db_g = db_g + _dot(f["xw"], dhnb)
                dxdt2 = dxdt2 + dxw2 * w2
                t_w = dxw2 * xdt2
                dx2 = dx2 + dxdt2 * dt2
                t_dt = dxdt2 * x2
                for hh in range(2):
                    h = 2 * p + hh
                    onehot = (lane == h).astype(F32)
                    w_col = w_all[:, h:h + 1]
                    dw_col = _head_sum(t_w, hh) * w_col
                    rs = rowsel if hh == 0 else jnp.logical_not(rowsel)
                    dlast = (jnp.sum(jnp.where(rs, t_h, 0.0), keepdims=True) * cd[:, h:h + 1]
                             + jnp.sum(dw_col, keepdims=True))
                    dacum_col = diag_cols[hh] + _head_sum(t_off, hh) - dw_col + jnp.where(is_last_row, dlast, 0.0)
                    dacum_all = dacum_all + dacum_col * onehot
                    ddt_all = ddt_all + _head_sum(t_dt, hh) * onehot
                    sel = m0 if hh == 0 else jnp.logical_not(m0)
                    dd_row = dd_row + jnp.sum(jnp.where(sel, dyx, 0.0), keepdims=True) * onehot
                dh_scr[p] = dhprev
                dact[:, lanes] = dx2
            dgb = dg_g.astype(BF16)
            dc_g = dc_g + _dot(dgb, b_g)
            db_g = db_g + _dot_tn(dgb, c_g)
            dact[:, D + NST * g:D + NST * (g + 1)] = db_g
            dact[:, D + NGRP * NST + NST * g:D + NGRP * NST + NST * (g + 1)] = dc_g
        rr = lax.broadcasted_iota(jnp.int32, (LCH, LCH), 0)
        cc2 = lax.broadcasted_iota(jnp.int32, (LCH, LCH), 1)
        dadt = _dot_hi((cc2 >= rr).astype(F32), dacum_all)
        ddt_all = ddt_all + dadt * a
        heads = lane < NH
        da = jnp.sum(dadt * dt, axis=0, keepdims=True)
        dr = jnp.where(heads, ddt_all * _sigmoid(psv + dtb_ref[...]), 0.0)
        dps_ref[...] = dr
        pgh_ref[0:1, :] += jnp.sum(dr, axis=0, keepdims=True)
        pgh_ref[1:2, :] += jnp.where(heads, da * a, 0.0)
        pgh_ref[2:3, :] += dd_row
        dpre = dact[...] * sg * (1.0 + pre * (1.0 - sg))
        extd = jnp.concatenate([dpre, dhead[...]], axis=0)
        hi = extd.astype(BF16)
        lo = (extd - hi.astype(F32)).astype(BF16)
        up = _dot(sup_ref[...], hi) + _dot(sup_ref[...], lo)
        du = cw_ref[3:4, :] * dpre
        pgw_ref[3:4, :] += jnp.sum(dpre * cur16.astype(F32), axis=0, keepdims=True)
        for d in range(1, 4):
            du = du + cw_ref[3 - d:4 - d, :] * up[LCH * (d - 1):LCH * d]
            pgw_ref[3 - d:4 - d, :] += jnp.sum(dpre * sh[LCH * (d - 1):LCH * d], axis=0, keepdims=True)
        pgw_ref[4:5, :] += jnp.sum(dpre, axis=0, keepdims=True)
        dxbc_ref[...] = du.astype(BF16)
        dhead[...] = dpre[0:HALO, :]

    xbc_out = pl.BlockSpec((LCH, 2 * D), lambda b, c: (b * nc + nc - 1 - c, 0))
    acc = lambda w: pl.BlockSpec((8, w), lambda b, c: (0, 0))
    sdn, sup = _shift_matrices()
    return _hosted_call(
        body, comm, f"ssm_bwd_{li}", (B, nc),
        in_specs=[prev, cur, zed, psb, hpb, row, const(3 * LCH, LCH + HALO), const(3 * LCH, LCH + HALO),
                  const(4, 2 * D), const(1, 2 * D), const(1, LANES), const(1, LANES), const(1, D), const(1, D)],
        out_specs=[xbc_out, row, psb, acc(2 * D), acc(D), acc(LANES)],
        out_shape=[jax.ShapeDtypeStruct((T, 2 * D), BF16), jax.ShapeDtypeStruct((T, D), BF16),
                   jax.ShapeDtypeStruct((T, LANES), F32), jax.ShapeDtypeStruct((8, 2 * D), F32),
                   jax.ShapeDtypeStruct((8, D), F32), jax.ShapeDtypeStruct((8, LANES), F32)],
        scratch=[pltpu.VMEM((NH // 2, LANES, NST), F32), pltpu.VMEM((HALO, 2 * D), F32),
                 pltpu.VMEM((LCH, 2 * D), F32)],
        dims=("arbitrary", "arbitrary"),
        operands=(proj, proj, proj, ps, hp, dya, sdn, sup, cw, cb, dtb, alog, dsk, nw))


def _lane_row(v, offset):
    return jnp.pad(v.astype(F32), (offset, LANES - offset - v.shape[0]))[None]


def _pack_rows(arrays):
    parts = []
    for a in arrays:
        flat = a.reshape(-1).astype(F32)
        pad = (-flat.shape[0]) % LANES
        parts.append(jnp.pad(flat, (0, pad)))
    flat = jnp.concatenate(parts)
    pad = (-flat.shape[0]) % (8 * LANES)
    return jnp.pad(flat, (0, pad)).reshape(-1, LANES)


def _unpack_rows(pack, shapes):
    flat = pack.reshape(-1)
    out, pos = [], 0
    for shp in shapes:
        n = math.prod(shp)
        out.append(flat[pos:pos + n].reshape(shp))
        pos += n + (-n) % LANES
    return out


def _split_w_in(w):
    main = jnp.concatenate([w[:, 0:3072], w[:, 3088:4112], w[:, 4624:5648], w[:, 5648:8720], w[:, 8736:12832],
                            w[:, 4112:4624]], axis=1)
    small = jnp.concatenate([w[:, 3072:3088], w[:, 8720:8736], jnp.zeros((D, LANES - 2 * NH), w.dtype)], axis=1)
    return main, small


def _join_w_in(dm, ds):
    return jnp.concatenate([dm[:, 0:3072], ds[:, 0:NH], dm[:, 3072:4096], dm[:, 12288:12800], dm[:, 4096:5120],
                            dm[:, 5120:8192], ds[:, NH:2 * NH], dm[:, 8192:12288]], axis=1)


def kernel(x, norm_w, w_in, conv_w, conv_b, dt_bias, a_log, d_skip, ssm_norm_w, sinks, f_bias, gate_bias, w_proj, w_out, final_norm_w, loss_target, m_norm_w, m_w_in, m_conv_w, m_conv_b, m_dt_bias, m_a_log, m_d_skip, m_ssm_norm_w, m_sinks, m_f_bias, m_gate_bias, m_w_proj, m_w_out, m_final_norm_w, v_norm_w, v_w_in, v_conv_w, v_conv_b, v_dt_bias, v_a_log, v_d_skip, v_ssm_norm_w, v_sinks, v_f_bias, v_gate_bias, v_w_proj, v_w_out, v_final_norm_w):
    Bl, S, _ = x.shape
    T = Bl * S
    depth = norm_w.shape[0]
    me = 4 * lax.axis_index("x") + 2 * lax.axis_index("y") + lax.axis_index("c")
    csh, gsh = conv_w.shape[2], gate_bias.shape[2]

    def gather_plan(l):
        small = jnp.concatenate([conv_w[l].reshape(-1), gate_bias[l].reshape(-1)]).reshape(-1, LANES)
        return _Comm("gather", [w_in[l].astype(BF16), w_proj[l].astype(BF16), w_out[l].astype(BF16), small])

    def unpack_weights(res):
        g_win, g_wp, g_wo, g_small = res
        flat = g_small.reshape(NDEV, -1)
        return (g_win.transpose(1, 0, 2).reshape(D, NIN),
                g_wp.transpose(1, 0, 2, 3).reshape(3, D, D),
                g_wo.reshape(D, D),
                flat[:, :4 * csh].reshape(NDEV, 4, csh).transpose(1, 0, 2).reshape(4, 2 * D),
                flat[:, 4 * csh:].reshape(NDEV, 3, gsh).transpose(1, 0, 2).reshape(3, D))

    def scatter_plan(gw_in, gw_p, gw_o):
        return _Comm("scatter", [gw_in.astype(BF16).reshape(D, NDEV, NSH).transpose(1, 0, 2),
                                 gw_p.astype(BF16).reshape(3, NDEV, D // NDEV, D).transpose(1, 0, 2, 3),
                                 gw_o.astype(BF16).reshape(NDEV, D // NDEV, D)])

    pos = jnp.arange(S, dtype=F32)
    inv_freq = ROPE_THETA ** (-jnp.arange(0, HD, 2, dtype=F32) / HD)
    ang = pos[:, None] * inv_freq[None, :]
    cos128 = jnp.tile(jnp.cos(ang), (1, 4))
    sign = jnp.where((jnp.arange(LANES) % HD) < HD // 2, -1.0, 1.0).astype(F32)
    sin128 = jnp.tile(jnp.sin(ang), (1, 4)) * sign[None, :]

    bq, nq = _fox_blocks(S)
    x2 = x.reshape(T, D)
    tgt2 = loss_target.reshape(T, D)

    saved = []
    xcur = x2
    weights = [None] * depth
    weights[0] = unpack_weights(gather_plan(0).call("gather_weights_0"))
    for l in range(depth):
        win_l, wp_l, wo_l, cw_l, gb_l = weights[l]
        wmain, wsmall = _split_w_in(win_l)
        comm = gather_plan(l + 1) if l + 1 < depth else None
        res = _inproj_fwd(xcur, norm_w[l][None], wmain, wsmall, cos128, sin128, S, l, comm)
        proj, ps, h16 = res[:3]
        if comm is not None:
            weights[l + 1] = unpack_weights(res[3:])
        dtb = _lane_row(dt_bias[l], 0)
        alog = _lane_row(a_log[l], 0)
        fb = _lane_row(f_bias[l], NH)
        dsk = jnp.repeat(d_skip[l], HD)[None]
        ya, hp = _ssm_fwd(proj, ps, cw_l, conv_b[l][None], dtb, alog, dsk, ssm_norm_w[l][None], S, l)
        yb, ob, lse_b = _swa_fwd(proj, sinks[l], S, l)
        cum = _fox_cum(ps, fb, S, l)
        cumh = cum[:, NH:2 * NH].reshape(Bl, S, NH).transpose(0, 2, 1)
        cum_col = cumh[..., None]
        yc, oc, lse_c = _fox_fwd(proj, cum_col, S, l)
        xnext, br = _merge_fwd(ya, yb, yc, proj, gb_l, wp_l, wo_l, xcur, l)
        saved.append(dict(x=xcur, wmain=wmain, wsmall=wsmall, proj=proj, ps=ps, h16=h16, dtb=dtb, alog=alog, fb=fb,
                          dsk=dsk, ya=ya, hp=hp, yb=yb, ob=ob, lse_b=lse_b, cum_col=cum_col,
                          yc=yc, oc=oc, lse_c=lse_c, br=br))
        xcur = xnext

    dx, dx16, st = _final_loss(xcur, tgt2, final_norm_w[None])
    loss_part = st[2, 0]
    g_final = st[0]

    gsm = {k: [None] * depth for k in ("norm_w", "conv_w", "conv_b", "dt_bias", "a_log", "d_skip", "ssm_norm_w",
                                      "sinks", "f_bias", "gate_bias")}
    parts = [None] * depth
    pending = None
    for l in reversed(range(depth)):
        sv = saved[l]
        proj, ps = sv["proj"], sv["ps"]
        _, wp_l, wo_l, cw_l, gb_l = weights[l]
        dbr, dgates, merged16, dgb = _merge_bwd_gates(dx16, wo_l.T, sv["br"], proj, gb_l, l)
        g_wo = _matmul(merged16.T, dx16, F32, f"dwout_{l}")
        dys, dwps = [], []
        for i, y in enumerate((sv["ya"], sv["yb"], sv["yc"])):
            dys.append(_matmul(dbr[i], wp_l[i].T, BF16, f"dy_{l}_{i}"))
            dwps.append(_matmul(y.T, dbr[i], F32, f"dwproj_{l}_{i}"))
        g_wp = jnp.stack(dwps)
        gsm["gate_bias"][l] = dgb[0:3]
        res = _ssm_bwd(proj, ps, sv["hp"], dys[0], cw_l, conv_b[l][None], sv["dtb"], sv["alog"], sv["dsk"],
                       ssm_norm_w[l][None], S, l, pending)
        dxbc, daz, dps_a, pgw, pg1, pgh = res[:6]
        if pending is not None:
            parts[l + 1] = res[6:]
        gsm["conv_w"][l], gsm["conv_b"][l] = pgw[0:4], pgw[4]
        gsm["ssm_norm_w"][l] = pg1[0]
        gsm["dt_bias"][l], gsm["a_log"][l], gsm["d_skip"][l] = pgh[0, :NH], pgh[1, :NH], pgh[2, :NH]
        do_b, dbz = _zgate_bwd(dys[1], sv["ob"], proj, OFF_BZ // D, f"zgate_bwd_swa_{l}")
        dq_b, dsk_b = _swa_bwd_dq(proj, do_b, sv["ob"], sv["lse_b"], sinks[l], cos128, sin128, S, l)
        dk_b, dv_b = _swa_bwd_dkv(proj, do_b, sv["ob"], sv["lse_b"], cos128, sin128, S, l)
        gsm["sinks"][l] = dsk_b[:, :, 0].reshape(NH)
        do_c, dcz = _zgate_bwd(dys[2], sv["oc"], proj, OFF_CZ // D, f"zgate_bwd_fox_{l}")
        dq_c, dk_c, dv_c, dcum_k, dcum_q = _fox_bwd(proj, do_c, sv["oc"], sv["cum_col"], sv["lse_c"], S, l)
        dcum_tm = (dcum_k.reshape(Bl, NH, S) + dcum_q.reshape(Bl, NH, S)).transpose(0, 2, 1).reshape(T, NH)
        dcum_pad = jnp.pad(dcum_tm, ((0, 0), (NH, LANES - 2 * NH)))
        df, dfb = _fox_cum_bwd(dcum_pad, ps, sv["fb"], S, l)
        gsm["f_bias"][l] = dfb[0, NH:2 * NH]
        dps16 = (dps_a + df).astype(BF16)
        dproj = jnp.concatenate([dxbc, daz, dq_b, dbz, dq_c, dk_c, dv_c, dcz, dgates, dk_b, dv_b], axis=1)
        h_t = sv["h16"].T
        dwm = _matmul(h_t, dproj, F32, f"dwin_main_{l}", tm=1024, tn=1280, tk=512)
        dws = _matmul(h_t, dps16, F32, f"dwin_small_{l}")
        plan = scatter_plan(_join_w_in(dwm, dws), g_wp, g_wo)
        res = _inproj_bwd_dx(dproj, sv["wmain"].T, dps16, sv["wsmall"].T, sv["x"], norm_w[l][None], dx, l,
                             plan if l == 0 else None)
        dx, dx16, dnw = res[:3]
        if l == 0:
            parts[0] = res[3:]
        else:
            pending = plan
        gsm["norm_w"][l] = dnw[0]

    big = {}
    for idx, (name, w, m, v) in enumerate((("w_in", w_in, m_w_in, v_w_in), ("w_proj", w_proj, m_w_proj, v_w_proj),
                                          ("w_out", w_out, m_w_out, v_w_out))):
        cols = w.shape[-1]
        res = _sum_adamw([parts[l][idx].reshape(NDEV, -1, cols) for l in range(depth)], w.reshape(depth, -1, cols),
                         m.reshape(depth, -1, cols), v.reshape(depth, -1, cols), f"adamw_{name}")
        big[name] = [r.reshape(w.shape) for r in res]

    small_names = ("norm_w", "conv_b", "dt_bias", "a_log", "d_skip", "ssm_norm_w", "sinks", "f_bias")
    small_parts = [jnp.stack(gsm[k]) for k in small_names] + [g_final, jnp.stack(gsm["conv_w"]),
                                                              jnp.stack(gsm["gate_bias"]), loss_part.reshape(1)]
    shapes = [a.shape for a in small_parts]
    summed = _unpack_rows(_all_reduce_small(_pack_rows(small_parts)), shapes)
    g_small = dict(zip(small_names, summed[:len(small_names)]))
    g_small["final_norm_w"] = summed[len(small_names)]
    g_small["conv_w"] = lax.dynamic_slice_in_dim(summed[len(small_names) + 1], me * csh, csh, axis=2)
    g_small["gate_bias"] = lax.dynamic_slice_in_dim(summed[len(small_names) + 2], me * gsh, gsh, axis=2)
    loss = summed[len(small_names) + 3][0]

    ws = dict(norm_w=norm_w, conv_w=conv_w, conv_b=conv_b, dt_bias=dt_bias, a_log=a_log, d_skip=d_skip,
              ssm_norm_w=ssm_norm_w, sinks=sinks, f_bias=f_bias, gate_bias=gate_bias, final_norm_w=final_norm_w)
    ms = dict(norm_w=m_norm_w, conv_w=m_conv_w, conv_b=m_conv_b, dt_bias=m_dt_bias, a_log=m_a_log, d_skip=m_d_skip,
              ssm_norm_w=m_ssm_norm_w, sinks=m_sinks, f_bias=m_f_bias, gate_bias=m_gate_bias,
              final_norm_w=m_final_norm_w)
    vs = dict(norm_w=v_norm_w, conv_w=v_conv_w, conv_b=v_conv_b, dt_bias=v_dt_bias, a_log=v_a_log, d_skip=v_d_skip,
              ssm_norm_w=v_ssm_norm_w, sinks=v_sinks, f_bias=v_f_bias, gate_bias=v_gate_bias,
              final_norm_w=v_final_norm_w)
    order = list(ws)
    oshapes = [ws[k].shape for k in order]
    res = _adamw_small(_pack_rows([g_small[k] for k in order]), _pack_rows([ws[k] for k in order]),
                       _pack_rows([ms[k] for k in order]), _pack_rows([vs[k] for k in order]))
    d_s, m_s, v_s = (dict(zip(order, _unpack_rows(r, oshapes))) for r in res)

    names = ("norm_w", "w_in", "conv_w", "conv_b", "dt_bias", "a_log", "d_skip", "ssm_norm_w", "sinks", "f_bias",
             "gate_bias", "w_proj", "w_out", "final_norm_w")
    grads, deltas, new_m, new_v = [], [], [], []
    for k in names:
        if k in big:
            g, d_, m_, v_ = big[k]
        else:
            g, d_, m_, v_ = g_small[k], d_s[k], m_s[k], v_s[k]
        grads.append(g)
        deltas.append(d_)
        new_m.append(m_)
        new_v.append(v_)
    return (loss, dx.reshape(Bl, S, D), *grads, *deltas, *new_m, *new_v)
```

```python
import functools
import math

import jax
import jax.numpy as jnp
from jax import lax
from jax.experimental import pallas as pl
from jax.experimental.pallas import tpu as pltpu

F32 = jnp.float32
BF16 = jnp.bfloat16
MESH = pl.DeviceIdType.MESH
NDEV = 8

D = 1024
NH = 16
HD = 64
NST = 128
NGRP = 4
LCH = 128
EPS = 1e-6
ROPE_THETA = 10000.0
SCALE = HD ** -0.5
NEG = -1e30

LANES = 128
VMEM_LIMIT = 56 * 1024 * 1024

OFF_XBC, OFF_AZ, OFF_BQ, OFF_BZ, OFF_CQ, OFF_CK, OFF_CV, OFF_CZ, OFF_G, OFF_BK, OFF_BV = (
    0, 2048, 3072, 4096, 5120, 6144, 7168, 8192, 9216, 12288, 12544)
NMAIN = 12800
NIN = 12832
NSH = NIN // NDEV

ADAM_LR, ADAM_B1, ADAM_B2, ADAM_EPS, ADAM_WD, ADAM_STEP = 0.001, 0.9, 0.999, 1e-08, 0.01, 10


def _cparams(dims=None, vmem=None):
    return pltpu.CompilerParams(dimension_semantics=dims, vmem_limit_bytes=vmem)


def _dot(a, b):
    return jnp.dot(a, b, preferred_element_type=F32)


def _dot_nt(a, b):
    return lax.dot_general(a, b, (((1,), (1,)), ((), ())), preferred_element_type=F32)


def _dot_tn(a, b):
    return lax.dot_general(a, b, (((0,), (0,)), ((), ())), preferred_element_type=F32)


def _dot_hi(a, b):
    return jnp.dot(a, b, precision=lax.Precision.HIGHEST, preferred_element_type=F32)


def _sigmoid(x):
    return 1.0 / (1.0 + jnp.exp(-x))


def _softplus(x):
    return jnp.maximum(x, 0.0) + jnp.log(1.0 + jnp.exp(-jnp.abs(x)))


def _lane_iota(n=LANES):
    return lax.broadcasted_iota(jnp.int32, (1, n), 1)


def _rot_half(x):
    first = (_lane_iota() % HD) < (HD // 2)
    return jnp.where(first, pltpu.roll(x, LANES - HD // 2, 1), pltpu.roll(x, HD // 2, 1))


def _head_sum(x, head):
    m = (_lane_iota() < HD) if head == 0 else (_lane_iota() >= HD)
    return jnp.sum(jnp.where(m, x, 0.0), axis=1, keepdims=True)


def _me_and_peers():
    x, y, c = lax.axis_index("x"), lax.axis_index("y"), lax.axis_index("c")
    me = 4 * x + 2 * y + c
    peers = []
    for k in range(1, NDEV):
        kx, ky, kc = (k >> 2) & 1, (k >> 1) & 1, k & 1
        px, py, pc = x ^ kx, y ^ ky, c ^ kc
        peers.append(((px, py, pc), 4 * px + 2 * py + pc))
    return me, peers


class _Comm:
    def __init__(self, kind, arrays):
        self.kind, self.arrays, self.n = kind, list(arrays), len(arrays)
        any_spec = pl.BlockSpec(memory_space=pl.ANY)
        self.in_specs = [any_spec] * self.n
        self.out_specs = [any_spec] * self.n
        self.out_shape = [jax.ShapeDtypeStruct(((NDEV,) + a.shape) if kind == "gather" else a.shape, a.dtype)
                          for a in self.arrays]
        self.scratch = [pltpu.SemaphoreType.DMA((self.n, NDEV - 1)), pltpu.SemaphoreType.DMA((self.n, NDEV - 1)),
                        pltpu.SemaphoreType.DMA((self.n,))]

    def copies(self, ins, outs, sems):
        send_sems, recv_sems, local_sems = sems
        me, peers = _me_and_peers()
        out = []
        for a in range(self.n):
            mine = ins[a] if self.kind == "gather" else ins[a].at[me]
            out.append(pltpu.make_async_copy(mine, outs[a].at[me], local_sems.at[a]))
            for k, (peer, pidx) in enumerate(peers):
                src = ins[a] if self.kind == "gather" else ins[a].at[pidx]
                out.append(pltpu.make_async_remote_copy(
                    src_ref=src, dst_ref=outs[a].at[me], send_sem=send_sems.at[a, k], recv_sem=recv_sems.at[a, k],
                    device_id=peer, device_id_type=MESH))
        return out

    def call(self, name):
        def body(*refs):
            cps = self.copies(refs[:self.n], refs[self.n:2 * self.n], refs[2 * self.n:])
            for cp in cps:
                cp.start()
            for cp in cps:
                cp.wait()

        return pl.pallas_call(body, name=name, out_shape=self.out_shape, in_specs=self.in_specs,
                              out_specs=self.out_specs, scratch_shapes=self.scratch)(*self.arrays)


def _gather_two_level(arrays, name):
    n = len(arrays)

    def body(*refs):
        ins, outs = refs[:n], refs[n:2 * n]
        send_sems, recv_sems, local_sems = refs[2 * n:]
        x, y, c = lax.axis_index("x"), lax.axis_index("y"), lax.axis_index("c")
        me, sibling = (x, y, c), (x, y, 1 - c)
        chips = [(1 - x, y), (x, 1 - y), (1 - x, 1 - y)]

        def slot(a, dev):
            return outs[a].at[4 * dev[0] + 2 * dev[1] + dev[2]]

        def copy(a, k, block, to, src=None):
            return pltpu.make_async_remote_copy(
                src_ref=slot(a, block) if src is None else src, dst_ref=slot(a, block),
                send_sem=send_sems.at[a, k], recv_sem=recv_sems.at[a, k], device_id=to, device_id_type=MESH)

        mine = [pltpu.make_async_copy(ins[a], slot(a, me), local_sems.at[a]) for a in range(n)]
        for cp in mine:
            cp.start()
        first = []
        for a in range(n):
            first.append(copy(a, 0, me, sibling, src=ins[a]))
            first += [copy(a, 1 + j, me, (*chip, c), src=ins[a]) for j, chip in enumerate(chips)]
        for cp in first:
            cp.start()
        passed = []
        for j, chip in enumerate(chips):
            for a in range(n):
                copy(a, 1 + j, (*chip, c), me).wait_recv()
                fwd = copy(a, 4 + j, (*chip, c), sibling)
                fwd.start()
                passed.append(fwd)
        for a in range(n):
            copy(a, 0, sibling, me).wait_recv()
            for j, chip in enumerate(chips):
                copy(a, 4 + j, (*chip, 1 - c), me).wait_recv()
        for cp in first + passed:
            cp.wait_send()
        for cp in mine:
            cp.wait()

    any_spec = pl.BlockSpec(memory_space=pl.ANY)
    return pl.pallas_call(
        body, name=name, out_shape=[jax.ShapeDtypeStruct((NDEV,) + a.shape, a.dtype) for a in arrays],
        in_specs=[any_spec] * n, out_specs=[any_spec] * n,
        scratch_shapes=[pltpu.SemaphoreType.DMA((n, NDEV - 1)), pltpu.SemaphoreType.DMA((n, NDEV - 1)),
                        pltpu.SemaphoreType.DMA((n,))])(*arrays)


def _hosted_call(body, comm, name, grid, in_specs, out_specs, out_shape, scratch, dims, operands):
    if comm is None:
        return pl.pallas_call(body, name=name, grid=grid, in_specs=in_specs, out_specs=out_specs, out_shape=out_shape,
                              scratch_shapes=scratch, compiler_params=_cparams(dims, VMEM_LIMIT))(*operands)
    n_in, n_out, n_scr, n = len(in_specs), len(out_specs), len(scratch), comm.n

    def hosted(*refs):
        hin, cin = refs[:n_in], refs[n_in:n_in + n]
        hout = refs[n_in + n:n_in + n + n_out]
        cout = refs[n_in + n + n_out:n_in + 2 * n + n_out]
        hscr = refs[n_in + 2 * n + n_out:n_in + 2 * n + n_out + n_scr]
        sems = refs[n_in + 2 * n + n_out + n_scr:]
        ids = [pl.program_id(a) for a in range(len(grid))]
        first = functools.reduce(jnp.logical_and, [i == 0 for i in ids])
        last = functools.reduce(jnp.logical_and, [i == g - 1 for i, g in zip(ids, grid)])

        @pl.when(first)
        def _():
            for cp in comm.copies(cin, cout, sems):
                cp.start()

        body(*hin, *hout, *hscr)

        @pl.when(last)
        def _():
            for cp in comm.copies(cin, cout, sems):
                cp.wait()

    return pl.pallas_call(
        hosted, name=name, grid=grid, in_specs=list(in_specs) + comm.in_specs,
        out_specs=list(out_specs) + comm.out_specs, out_shape=list(out_shape) + comm.out_shape,
        scratch_shapes=list(scratch) + comm.scratch,
        compiler_params=_cparams(("arbitrary",) * len(grid), VMEM_LIMIT))(*operands, *comm.arrays)


def _all_reduce_small(v):
    rows = v.shape[0]

    def body(v_ref, sum_ref, all_ref, send_sems, recv_sems):
        me, peers = _me_and_peers()
        all_ref[me] = v_ref[...]
        copies = []
        for k, (peer, _) in enumerate(peers):
            cp = pltpu.make_async_remote_copy(
                src_ref=v_ref, dst_ref=all_ref.at[me],
                send_sem=send_sems.at[k], recv_sem=recv_sems.at[k],
                device_id=peer, device_id_type=MESH)
            cp.start()
            copies.append(cp)
        for cp in copies:
            cp.wait()
        acc = all_ref[0]
        for d in range(1, NDEV):
            acc = acc + all_ref[d]
        sum_ref[...] = acc

    vm = pl.BlockSpec(memory_space=pltpu.VMEM)
    return pl.pallas_call(
        body, name="all_reduce_small",
        out_shape=jax.ShapeDtypeStruct((rows, LANES), F32),
        in_specs=[vm], out_specs=vm,
        scratch_shapes=[pltpu.VMEM((NDEV, rows, LANES), F32),
                        pltpu.SemaphoreType.DMA((NDEV - 1,)), pltpu.SemaphoreType.DMA((NDEV - 1,))],
    )(v)


def _adamw_math(w, g, m, v):
    m = ADAM_B1 * m + (1.0 - ADAM_B1) * g
    v = ADAM_B2 * v + (1.0 - ADAM_B2) * jnp.square(g)
    m_hat = m / (1.0 - ADAM_B1 ** ADAM_STEP)
    v_hat = v / (1.0 - ADAM_B2 ** ADAM_STEP)
    delta = -ADAM_LR * (m_hat / (jnp.sqrt(v_hat) + ADAM_EPS) + ADAM_WD * w)
    return delta, m, v


def _sum_adamw(parts, w, m, v, name):
    depth, rows, cols = w.shape
    tr = next(c for c in (256, 128, 64, 32, 16) if rows % c == 0)
    nb = rows // tr

    def body(*refs):
        p_refs, (w_ref, m_ref, v_ref, g_ref, d_ref, nm_ref, nv_ref) = refs[:depth], refs[depth:]
        l = pl.program_id(0)
        for ll in range(depth):
            @pl.when(l == ll)
            def _(ll=ll):
                g = p_refs[ll][0].astype(F32)
                for d in range(1, NDEV):
                    g = g + p_refs[ll][d].astype(F32)
                delta, nm, nv = _adamw_math(w_ref[0], g, m_ref[0], v_ref[0])
                g_ref[0] = g
                d_ref[0] = delta
                nm_ref[0] = nm
                nv_ref[0] = nv

    part = lambda ll: pl.BlockSpec((NDEV, tr, cols), lambda l, i, ll=ll: (0, jnp.where(l == ll, i, jnp.where(l < ll, 0, nb - 1)), 0))
    blk = pl.BlockSpec((1, tr, cols), lambda l, i: (l, i, 0))
    sds = jax.ShapeDtypeStruct((depth, rows, cols), F32)
    return pl.pallas_call(
        body, name=name, grid=(depth, nb),
        in_specs=[part(ll) for ll in range(depth)] + [blk, blk, blk],
        out_specs=[blk, blk, blk, blk], out_shape=[sds, sds, sds, sds],
        compiler_params=_cparams(("arbitrary", "arbitrary"), VMEM_LIMIT),
    )(*parts, w, m, v)


def _adamw_small(g, w, m, v):
    def body(g_ref, w_ref, m_ref, v_ref, d_ref, nm_ref, nv_ref):
        delta, nm, nv = _adamw_math(w_ref[...], g_ref[...], m_ref[...], v_ref[...])
        d_ref[...] = delta
        nm_ref[...] = nm
        nv_ref[...] = nv

    sds = jax.ShapeDtypeStruct(g.shape, F32)
    return pl.pallas_call(body, name="adamw_small", out_shape=[sds, sds, sds])(g, w, m, v)


def _matmul(a, b, out_dtype, name, tm=1024, tn=1024, tk=512):
    M, K = a.shape
    N = b.shape[1]
    tm, tn, tk = min(tm, M), min(tn, N), min(tk, K)
    nk = K // tk

    def body(a_ref, b_ref, o_ref, acc):
        k = pl.program_id(2)

        @pl.when(k == 0)
        def _():
            acc[...] = jnp.zeros_like(acc)

        acc[...] += _dot(a_ref[...], b_ref[...])

        @pl.when(k == nk - 1)
        def _():
            o_ref[...] = acc[...].astype(out_dtype)

    return pl.pallas_call(
        body, name=name, grid=(M // tm, N // tn, nk),
        in_specs=[pl.BlockSpec((tm, tk), lambda i, j, k: (i, k)), pl.BlockSpec((tk, tn), lambda i, j, k: (k, j))],
        out_specs=pl.BlockSpec((tm, tn), lambda i, j, k: (i, j)),
        out_shape=jax.ShapeDtypeStruct((M, N), out_dtype),
        scratch_shapes=[pltpu.VMEM((tm, tn), F32)],
        compiler_params=_cparams(("parallel", "parallel", "arbitrary"), VMEM_LIMIT),
    )(a, b)


def _inproj_fwd(x2, nw, wmain, wsmall, cos128, sin128, S, li, comm=None):
    T = x2.shape[0]
    tm, tn = min(1024, S), 512
    nj, npos = NMAIN // tn, S // tm
    jq0, jk = OFF_BQ // tn, OFF_BK // tn

    def body(x_ref, nw_ref, w_ref, ws_ref, cos_ref, sin_ref, proj_ref, ps_ref, ht_ref, h_scr):
        j = pl.program_id(1)

        @pl.when(j == 0)
        def _():
            x = x_ref[...]
            r = lax.rsqrt(jnp.mean(x * x, axis=-1, keepdims=True) + EPS)
            h = (x * r * nw_ref[...]).astype(BF16)
            h_scr[...] = h
            ht_ref[...] = h.T
            ps_ref[...] = _dot(h, ws_ref[...])

        acc = _dot(h_scr[...], w_ref[...])

        def roped(c):
            xc = acc[:, LANES * c:LANES * (c + 1)]
            return (xc * cos_ref[...] + _rot_half(xc) * sin_ref[...]).astype(BF16)

        def plain(c):
            return acc[:, LANES * c:LANES * (c + 1)].astype(BF16)

        is_q = jnp.logical_or(j == jq0, j == jq0 + 1)
        is_k = j == jk

        @pl.when(is_q)
        def _():
            for c in range(4):
                proj_ref[:, LANES * c:LANES * (c + 1)] = roped(c)

        @pl.when(is_k)
        def _():
            for c in range(4):
                proj_ref[:, LANES * c:LANES * (c + 1)] = roped(c) if c < 2 else plain(c)

        @pl.when(jnp.logical_not(jnp.logical_or(is_q, is_k)))
        def _():
            proj_ref[...] = acc.astype(BF16)

    return _hosted_call(
        body, comm, f"inproj_fwd_{li}", (T // tm, nj),
        in_specs=[pl.BlockSpec((tm, D), lambda i, j: (i, 0)),
                  pl.BlockSpec((1, D), lambda i, j: (0, 0)),
                  pl.BlockSpec((D, tn), lambda i, j: (0, j)),
                  pl.BlockSpec((D, LANES), lambda i, j: (0, 0)),
                  pl.BlockSpec((tm, LANES), lambda i, j: (i % npos, 0)),
                  pl.BlockSpec((tm, LANES), lambda i, j: (i % npos, 0))],
        out_specs=[pl.BlockSpec((tm, tn), lambda i, j: (i, j)),
                   pl.BlockSpec((tm, LANES), lambda i, j: (i, 0)),
                   pl.BlockSpec((D, tm), lambda i, j: (0, i))],
        out_shape=[jax.ShapeDtypeStruct((T, NMAIN), BF16), jax.ShapeDtypeStruct((T, LANES), F32),
                   jax.ShapeDtypeStruct((D, T), BF16)],
        scratch=[pltpu.VMEM((tm, D), BF16)], dims=("parallel", "arbitrary"),
        operands=(x2, nw, wmain, wsmall, cos128, sin128))


def _inproj_bwd_dx(dproj, wmain, dps16, wsmall, x2, nw, dxo, li, comm=None):
    T = x2.shape[0]
    tm, tk = min(1024, T), 512
    nk = NMAIN // tk
    ni = T // tm

    def body(dp_ref, w_ref, ds_ref, ws_ref, x_ref, nw_ref, dxo_ref, dx_ref, dx16_ref, dnw_ref, acc):
        i, k = pl.program_id(0), pl.program_id(1)

        @pl.when(k == 0)
        def _():
            acc[...] = _dot_nt(ds_ref[...], ws_ref[...])

        acc[...] += _dot_nt(dp_ref[...], w_ref[...])

        @pl.when(jnp.logical_and(i == 0, k == 0))
        def _():
            dnw_ref[...] = jnp.zeros_like(dnw_ref)

        @pl.when(k == nk - 1)
        def _():
            x = x_ref[...]
            r = lax.rsqrt(jnp.mean(x * x, axis=-1, keepdims=True) + EPS)
            dh = acc[...]
            g = dh * nw_ref[...]
            dx = dxo_ref[...] + r * g - x * (r * r * r) * jnp.mean(g * x, axis=-1, keepdims=True)
            dx_ref[...] = dx
            dx16_ref[...] = dx.astype(BF16)
            dnw_ref[0:1, :] += jnp.sum(dh * x * r, axis=0, keepdims=True)

    return _hosted_call(
        body, comm, f"inproj_bwd_dx_{li}", (ni, nk),
        in_specs=[pl.BlockSpec((tm, tk), lambda i, k: (i, k)),
                  pl.BlockSpec((D, tk), lambda i, k: (0, k)),
                  pl.BlockSpec((tm, LANES), lambda i, k: (i, 0)),
                  pl.BlockSpec((D, LANES), lambda i, k: (0, 0)),
                  pl.BlockSpec((tm, D), lambda i, k: (i, 0)),
                  pl.BlockSpec((1, D), lambda i, k: (0, 0)),
                  pl.BlockSpec((tm, D), lambda i, k: (i, 0))],
        out_specs=[pl.BlockSpec((tm, D), lambda i, k: (i, 0)),
                   pl.BlockSpec((tm, D), lambda i, k: (i, 0)),
                   pl.BlockSpec((8, D), lambda i, k: (0, 0))],
        out_shape=[jax.ShapeDtypeStruct((T, D), F32), jax.ShapeDtypeStruct((T, D), BF16),
                   jax.ShapeDtypeStruct((8, D), F32)],
        scratch=[pltpu.VMEM((tm, D), F32)], dims=("arbitrary", "arbitrary"),
        operands=(dproj, wmain, dps16, wsmall, x2, nw, dxo))


def _merge_fwd(ya, yb, yc, proj, gbias, wp, wout, x2, li):
    T = x2.shape[0]
    tm = min(512, T)
    gcol = OFF_G // D

    def body(ya_ref, yb_ref, yc_ref, g0_ref, g1_ref, g2_ref, gb_ref, wp_ref, wo_ref, x_ref, xn_ref, br_ref, yt_ref):
        merged = jnp.zeros((tm, D), F32)
        for i, (y_ref, g_ref) in enumerate(((ya_ref, g0_ref), (yb_ref, g1_ref), (yc_ref, g2_ref))):
            y = y_ref[...]
            yt_ref[i] = y.T
            br = _dot(y, wp_ref[i])
            br_ref[i] = br.astype(BF16)
            gate = _sigmoid(g_ref[...].astype(F32) + gb_ref[i:i + 1, :])
            merged = merged + gate * br
        xn_ref[...] = x_ref[...] + _dot(merged.astype(BF16), wo_ref[...])

    row = lambda c: pl.BlockSpec((tm, D), lambda i, c=c: (i, c))
    return pl.pallas_call(
        body, name=f"merge_fwd_{li}", grid=(T // tm,),
        in_specs=[row(0), row(0), row(0), row(gcol), row(gcol + 1), row(gcol + 2),
                  pl.BlockSpec((3, D), lambda i: (0, 0)),
                  pl.BlockSpec((3, D, D), lambda i: (0, 0, 0)),
                  pl.BlockSpec((D, D), lambda i: (0, 0)),
                  row(0)],
        out_specs=[row(0), pl.BlockSpec((3, tm, D), lambda i: (0, i, 0)), pl.BlockSpec((3, D, tm), lambda i: (0, 0, i))],
        out_shape=[jax.ShapeDtypeStruct((T, D), F32), jax.ShapeDtypeStruct((3, T, D), BF16),
                   jax.ShapeDtypeStruct((3, D, T), BF16)],
        compiler_params=_cparams(("parallel",), VMEM_LIMIT),
    )(ya, yb, yc, proj, proj, proj, gbias, wp, wout, x2)


def _merge_bwd(dxo16, wout, wp, br, proj, gbias, ob, oc, li):
    T = dxo16.shape[0]
    tm = min(256, T)
    gcol = OFF_G // D

    def body(dx_ref, wo_ref, wp_ref, br_ref, g0_ref, g1_ref, g2_ref, gb_ref, ob_ref, oc_ref, zb_ref, zc_ref,
             dbr_ref, dg_ref, mt_ref, dgb_ref, dya_ref, dob_ref, dzb_ref, doc_ref, dzc_ref):
        @pl.when(pl.program_id(0) == 0)
        def _():
            dgb_ref[...] = jnp.zeros_like(dgb_ref)

        dm = _dot_nt(dx_ref[...], wo_ref[...])
        merged = jnp.zeros((tm, D), F32)
        dys = []
        for i, g_ref in enumerate((g0_ref, g1_ref, g2_ref)):
            b = br_ref[i].astype(F32)
            gate = _sigmoid(g_ref[...].astype(F32) + gb_ref[i:i + 1, :])
            merged = merged + gate * b
            dbr = (dm * gate).astype(BF16)
            dbr_ref[i] = dbr
            dgate = dm * b * gate * (1.0 - gate)
            dg_ref[:, D * i:D * (i + 1)] = dgate.astype(BF16)
            dgb_ref[i:i + 1, :] += jnp.sum(dgate, axis=0, keepdims=True)
            dys.append(_dot_nt(dbr, wp_ref[i]))
        mt_ref[...] = merged.astype(BF16).T
        dya_ref[...] = dys[0].astype(BF16)
        for dy, o_ref, z_ref, do_ref, dz_ref in ((dys[1], ob_ref, zb_ref, dob_ref, dzb_ref),
                                                 (dys[2], oc_ref, zc_ref, doc_ref, dzc_ref)):
            z = z_ref[...].astype(F32)
            sg = _sigmoid(z)
            do_ref[...] = (dy * z * sg).astype(BF16)
            dz_ref[...] = (dy * o_ref[...].astype(F32) * sg * (1.0 + z * (1.0 - sg))).astype(BF16)

    row = lambda c: pl.BlockSpec((tm, D), lambda i, c=c: (i, c))
    sds = jax.ShapeDtypeStruct((T, D), BF16)
    return pl.pallas_call(
        body, name=f"merge_bwd_{li}", grid=(T // tm,),
        in_specs=[row(0), pl.BlockSpec((D, D), lambda i: (0, 0)), pl.BlockSpec((3, D, D), lambda i: (0, 0, 0)),
                  pl.BlockSpec((3, tm, D), lambda i: (0, i, 0)),
                  row(gcol), row(gcol + 1), row(gcol + 2),
                  pl.BlockSpec((3, D), lambda i: (0, 0)),
                  row(0), row(0), row(OFF_BZ // D), row(OFF_CZ // D)],
        out_specs=[pl.BlockSpec((3, tm, D), lambda i: (0, i, 0)),
                   pl.BlockSpec((tm, 3 * D), lambda i: (i, 0)),
                   pl.BlockSpec((D, tm), lambda i: (0, i)),
                   pl.BlockSpec((8, D), lambda i: (0, 0)),
                   row(0), row(0), row(0), row(0), row(0)],
        out_shape=[jax.ShapeDtypeStruct((3, T, D), BF16), jax.ShapeDtypeStruct((T, 3 * D), BF16),
                   jax.ShapeDtypeStruct((D, T), BF16), jax.ShapeDtypeStruct((8, D), F32), sds, sds, sds, sds, sds],
        compiler_params=_cparams(("arbitrary",), VMEM_LIMIT),
    )(dxo16, wout, wp, br, proj, proj, proj, gbias, ob, oc, proj, proj)


def _final_loss(x2, tgt, fw):
    T = x2.shape[0]
    tm = min(512, T)
    ni = T // tm

    def body(x_ref, t_ref, w_ref, dx_ref, dx16_ref, st_ref):
        i = pl.program_id(0)

        @pl.when(i == 0)
        def _():
            st_ref[...] = jnp.zeros_like(st_ref)

        x = x_ref[...]
        r = lax.rsqrt(jnp.mean(x * x, axis=-1, keepdims=True) + EPS)
        xh = x * r
        err = xh * w_ref[...] - t_ref[...]
        dy = err * (1.0 / D)
        g = dy * w_ref[...]
        dx = r * g - x * (r * r * r) * jnp.mean(g * x, axis=-1, keepdims=True)
        dx_ref[...] = dx
        dx16_ref[...] = dx.astype(BF16)
        st_ref[0:1, :] += jnp.sum(dy * xh, axis=0, keepdims=True)
        st_ref[1:2, :] += jnp.sum(err * err, axis=0, keepdims=True)

        @pl.when(i == ni - 1)
        def _():
            tot = jnp.sum(st_ref[1:2, :], axis=1, keepdims=True) * (0.5 / D)
            st_ref[2:3, :] = jnp.broadcast_to(tot, (1, D))

    row = pl.BlockSpec((tm, D), lambda i: (i, 0))
    return pl.pallas_call(
        body, name="final_loss", grid=(ni,),
        in_specs=[row, row, pl.BlockSpec((1, D), lambda i: (0, 0))],
        out_specs=[row, row, pl.BlockSpec((8, D), lambda i: (0, 0))],
        out_shape=[jax.ShapeDtypeStruct((T, D), F32), jax.ShapeDtypeStruct((T, D), BF16),
                   jax.ShapeDtypeStruct((8, D), F32)],
        compiler_params=_cparams(("arbitrary",), VMEM_LIMIT),
    )(x2, tgt, fw)


def _fox_cum(ps, fb_row, S, li):
    T = ps.shape[0]
    nb = S // LCH

    def body(ps_ref, fb_ref, cum_ref, carry):
        @pl.when(pl.program_id(1) == 0)
        def _():
            carry[...] = jnp.zeros_like(carry)

        logf = -_softplus(-(ps_ref[...] + fb_ref[...]))
        r = lax.broadcasted_iota(jnp.int32, (LCH, LCH), 0)
        c = lax.broadcasted_iota(jnp.int32, (LCH, LCH), 1)
        tri = (r >= c).astype(F32)
        cum = _dot_hi(tri, logf) + carry[0:1, :]
        cum_ref[...] = cum
        carry[0:1, :] = cum[LCH - 1:LCH, :]

    return pl.pallas_call(
        body, name=f"fox_cum_{li}", grid=(T // S, nb),
        in_specs=[pl.BlockSpec((LCH, LANES), lambda b, i: (b * nb + i, 0)),
                  pl.BlockSpec((1, LANES), lambda b, i: (0, 0))],
        out_specs=pl.BlockSpec((LCH, LANES), lambda b, i: (b * nb + i, 0)),
        out_shape=jax.ShapeDtypeStruct((T, LANES), F32),
        scratch_shapes=[pltpu.VMEM((8, LANES), F32)],
        compiler_params=_cparams(("arbitrary", "arbitrary")),
    )(ps, fb_row)


def _fox_cum_bwd(dcum, ps, fb_row, S, li):
    T = ps.shape[0]
    nb = S // LCH

    def body(dc_ref, ps_ref, fb_ref, df_ref, dfb_ref, carry):
        b, i = pl.program_id(0), pl.program_id(1)

        @pl.when(i == 0)
        def _():
            carry[...] = jnp.zeros_like(carry)

        @pl.when(jnp.logical_and(b == 0, i == 0))
        def _():
            dfb_ref[...] = jnp.zeros_like(dfb_ref)

        dc = dc_ref[...]
        r = lax.broadcasted_iota(jnp.int32, (LCH, LCH), 0)
        c = lax.broadcasted_iota(jnp.int32, (LCH, LCH), 1)
        tri = (c >= r).astype(F32)
        dlogf = _dot_hi(tri, dc) + carry[0:1, :]
        carry[0:1, :] += jnp.sum(dc, axis=0, keepdims=True)
        df = dlogf * _sigmoid(-(ps_ref[...] + fb_ref[...]))
        lane = _lane_iota()
        df = jnp.where(jnp.logical_and(lane >= NH, lane < 2 * NH), df, 0.0)
        df_ref[...] = df
        dfb_ref[0:1, :] += jnp.sum(df, axis=0, keepdims=True)

    blk = pl.BlockSpec((LCH, LANES), lambda b, i: (b * nb + nb - 1 - i, 0))
    return pl.pallas_call(
        body, name=f"fox_cum_bwd_{li}", grid=(T // S, nb),
        in_specs=[blk, blk, pl.BlockSpec((1, LANES), lambda b, i: (0, 0))],
        out_specs=[blk, pl.BlockSpec((8, LANES), lambda b, i: (0, 0))],
        out_shape=[jax.ShapeDtypeStruct((T, LANES), F32), jax.ShapeDtypeStruct((8, LANES), F32)],
        scratch_shapes=[pltpu.VMEM((8, LANES), F32)],
        compiler_params=_cparams(("arbitrary", "arbitrary")),
    )(dcum, ps, fb_row)


def _fox_blocks(S):
    bq = min(512, S)
    return bq, S // bq


def _split3(c):
    hi = c.astype(BF16).astype(F32)
    r = c - hi
    mid = r.astype(BF16).astype(F32)
    return hi, mid, (r - mid).astype(BF16).astype(F32)


def _augment(x, parts, key_side, hh):
    lane = _lane_iota()
    b0 = HD if hh == 0 else 0
    p0, o0 = (b0 + 3, b0) if key_side else (b0, b0 + 3)
    out = jnp.where(jnp.logical_and(lane >= o0, lane < o0 + 3), 1.0, x)
    for t in range(3):
        out = jnp.where(lane == p0 + t, parts[t], out)
    return out.astype(BF16)


def _fox_fwd(proj, cum_col, S, li):
    T = proj.shape[0]
    B = T // S
    bq, nq = _fox_blocks(S)
    qc, kc, vc, zc = OFF_CQ // LANES, OFF_CK // LANES, OFF_CV // LANES, OFF_CZ // LANES

    def body(q_ref, k_ref, v_ref, z_ref, cc_ref, y_ref, o_ref, lse_ref, kaug):
        i = pl.program_id(2)
        m0 = _lane_iota() < HD

        @pl.when(i == 0)
        def _():
            kf = k_ref[...].astype(F32)
            for hh in range(2):
                kaug[hh] = _augment(kf, _split3(-cc_ref[0, hh]), True, hh)

        q2 = q_ref[...].astype(F32) * SCALE
        rows_q = pl.ds(pl.multiple_of(i * bq, bq), bq)
        row = lax.broadcasted_iota(jnp.int32, (bq, bq), 0)
        col = lax.broadcasted_iota(jnp.int32, (bq, bq), 1)
        outs = []
        for hh in range(2):
            sel = m0 if hh == 0 else jnp.logical_not(m0)
            qa = _augment(jnp.where(sel, q2, 0.0), _split3(cc_ref[0, hh, rows_q, :]), False, hh)

            def step(j, carry, masked, hh=hh, qa=qa):
                m, l, acc = carry
                start = pl.multiple_of(j * bq, bq)
                v2 = v_ref[pl.ds(start, bq), :]
                s = _dot_nt(qa, kaug[hh, pl.ds(start, bq), :])
                if masked:
                    s = jnp.where(row >= col, s, NEG)
                mn = jnp.maximum(m, jnp.max(s, axis=1, keepdims=True))
                alpha = jnp.exp(m - mn)
                p = jnp.exp(s - mn)
                l = alpha * l + jnp.sum(p, axis=1, keepdims=True)
                acc = alpha * acc + _dot(p.astype(BF16), v2)
                return mn, l, acc

            init = (jnp.full((bq, 1), NEG, F32), jnp.zeros((bq, 1), F32), jnp.zeros((bq, LANES), F32))
            carry = lax.fori_loop(0, i, functools.partial(step, masked=False), init)
            m, l, acc = step(i, carry, True)
            outs.append(acc / l)
            lse_ref[0, hh] = m + jnp.log(l)
        o2 = jnp.where(m0, outs[0], outs[1])
        z = z_ref[...].astype(F32)
        o_ref[...] = o2.astype(BF16)
        y_ref[...] = (o2 * z * _sigmoid(z)).astype(BF16)

    qblk = lambda c: pl.BlockSpec((bq, LANES), lambda b, p, i, c=c: (b * nq + i, c + p))
    sblk = lambda c: pl.BlockSpec((S, LANES), lambda b, p, i, c=c: (b, c + p))
    return pl.pallas_call(
        body, name=f"fox_fwd_{li}", grid=(B, NH // 2, nq),
        in_specs=[qblk(qc), sblk(kc), sblk(vc), qblk(zc),
                  pl.BlockSpec((1, 2, S, 1), lambda b, p, i: (b, p, 0, 0))],
        out_specs=[qblk(0), qblk(0), pl.BlockSpec((1, 2, bq, 1), lambda b, p, i: (b, p, i, 0))],
        out_shape=[jax.ShapeDtypeStruct((T, D), BF16), jax.ShapeDtypeStruct((T, D), BF16),
                   jax.ShapeDtypeStruct((B, NH, S, 1), F32)],
        scratch_shapes=[pltpu.VMEM((2, S, LANES), BF16)],
        compiler_params=_cparams(("parallel", "parallel", "arbitrary"), VMEM_LIMIT),
    )(proj, proj, proj, proj, cum_col)


def _fox_bwd(proj, do, o, cum_col, lse, S, li):
    T = proj.shape[0]
    B = T // S
    bq, nq = _fox_blocks(S)
    qc, kc, vc = OFF_CQ // LANES, OFF_CK // LANES, OFF_CV // LANES

    def body(q_ref, k_ref, v_ref, do_ref, o_ref, cc_ref, lse_ref, dq_ref, dk_ref, dv_ref, dc_ref, dr_ref,
             dq_scr, dr_scr, qaug):
        j = pl.program_id(2)
        m0 = _lane_iota() < HD

        @pl.when(j == 0)
        def _():
            dq_scr[...] = jnp.zeros_like(dq_scr)
            dr_scr[...] = jnp.zeros_like(dr_scr)
            qf = q_ref[...].astype(F32) * SCALE
            for hh in range(2):
                sel = m0 if hh == 0 else jnp.logical_not(m0)
                qaug[hh] = _augment(jnp.where(sel, qf, 0.0), _split3(cc_ref[0, hh] - lse_ref[0, hh]), False, hh)

        k2 = k_ref[...]
        v2 = v_ref[...]
        zk = jnp.zeros_like(k2)
        kh = (jnp.where(m0, k2, zk), jnp.where(m0, zk, k2))
        kf = k2.astype(F32)
        rows_k = pl.ds(pl.multiple_of(j * bq, bq), bq)
        ka = [_augment(kf, _split3(-cc_ref[0, hh, rows_k, :]), True, hh) for hh in range(2)]
        row = lax.broadcasted_iota(jnp.int32, (bq, bq), 0)
        col = lax.broadcasted_iota(jnp.int32, (bq, bq), 1)

        def step(i, carry, masked):
            dk, dv, dc0, dc1 = carry
            dcs = [dc0, dc1]
            start = pl.multiple_of(i * bq, bq)
            q2 = q_ref[pl.ds(start, bq), :]
            do2 = do_ref[pl.ds(start, bq), :]
            prod = do2.astype(F32) * o_ref[pl.ds(start, bq), :].astype(F32)
            zq = jnp.zeros_like(q2)
            dq = jnp.zeros((bq, LANES), F32)
            for hh in range(2):
                sel = m0 if hh == 0 else jnp.logical_not(m0)
                qh = jnp.where(sel, q2, zq)
                doh = jnp.where(sel, do2, zq)
                delta = _head_sum(prod, hh)
                s = _dot_nt(qaug[hh, pl.ds(start, bq), :], ka[hh])
                if masked:
                    s = jnp.where(row >= col, s, NEG)
                p = jnp.exp(s)
                dp = _dot_nt(doh, v2)
                ds = p * (dp - delta)
                dcs[hh] = dcs[hh] - jnp.sum(ds, axis=0, keepdims=True)
                dr_scr[hh, pl.ds(start, bq), :] += jnp.sum(ds, axis=1, keepdims=True)
                dsb = ds.astype(BF16)
                dv = dv + _dot_tn(p.astype(BF16), doh)
                dk = dk + _dot_tn(dsb, qh)
                dq = dq + _dot(dsb, kh[hh])
            dq_scr[pl.ds(start, bq), :] += dq
            return dk, dv, dcs[0], dcs[1]

        zero = jnp.zeros((bq, LANES), F32)
        zrow = jnp.zeros((1, bq), F32)
        carry = step(j, (zero, zero, zrow, zrow), True)
        dk, dv, dc0, dc1 = lax.fori_loop(j + 1, nq, functools.partial(step, masked=False), carry)
        dk_ref[...] = (dk * SCALE).astype(BF16)
        dv_ref[...] = dv.astype(BF16)
        dc_ref[0, 0, 0] = dc0
        dc_ref[0, 1, 0] = dc1

        @pl.when(j == nq - 1)
        def _():
            dq_ref[...] = (dq_scr[...] * SCALE).astype(BF16)
            dr_ref[0] = dr_scr[...]

    sblk = lambda c: pl.BlockSpec((S, LANES), lambda b, p, j, c=c: (b, c + p))
    kblk = lambda c: pl.BlockSpec((bq, LANES), lambda b, p, j, c=c: (b * nq + j, c + p))
    col_spec = pl.BlockSpec((1, 2, S, 1), lambda b, p, j: (b, p, 0, 0))
    return pl.pallas_call(
        body, name=f"fox_bwd_{li}", grid=(B, NH // 2, nq),
        in_specs=[sblk(qc), kblk(kc), kblk(vc), sblk(0), sblk(0), col_spec, col_spec],
        out_specs=[sblk(0), kblk(0), kblk(0), pl.BlockSpec((1, 2, 1, 1, bq), lambda b, p, j: (b, p, j, 0, 0)),
                   col_spec],
        out_shape=[jax.ShapeDtypeStruct((T, D), BF16), jax.ShapeDtypeStruct((T, D), BF16),
                   jax.ShapeDtypeStruct((T, D), BF16), jax.ShapeDtypeStruct((B, NH, nq, 1, bq), F32),
                   jax.ShapeDtypeStruct((B, NH, S, 1), F32)],
        scratch_shapes=[pltpu.VMEM((S, LANES), F32), pltpu.VMEM((2, S, 1), F32), pltpu.VMEM((2, S, LANES), BF16)],
        compiler_params=_cparams(("parallel", "parallel", "arbitrary"), VMEM_LIMIT),
    )(proj, proj, proj, do, o, cum_col, lse)


def _swa_blocks(S):
    bq = min(512, S)
    return bq, S // bq, bq // LCH


def _dup_head(xw, kvl):
    m0 = _lane_iota() < HD
    a = jnp.where(m0 if kvl == 0 else jnp.logical_not(m0), xw, 0.0)
    return (a + pltpu.roll(a, HD, 1)).astype(BF16)


def _band(same_block):
    r = lax.broadcasted_iota(jnp.int32, (LCH, LCH), 0)
    c = lax.broadcasted_iota(jnp.int32, (LCH, LCH), 1)
    return (c <= r) if same_block else (c > r)


def _stack_heads(ref, rows, kvl):
    m0 = _lane_iota() < HD
    parts = []
    for ch in (2 * kvl, 2 * kvl + 1):
        x = ref[rows, LANES * ch:LANES * (ch + 1)]
        parts += [jnp.where(m0, x, jnp.zeros_like(x)), jnp.where(m0, jnp.zeros_like(x), x)]
    return jnp.concatenate(parts, axis=0)


def _stack_delta(do_ref, o_ref, rows, kvl, scale=None):
    parts = []
    for ch in (2 * kvl, 2 * kvl + 1):
        lanes = slice(LANES * ch, LANES * (ch + 1))
        prod = do_ref[rows, lanes].astype(F32) * o_ref[rows, lanes].astype(F32)
        parts += [_head_sum(prod, 0), _head_sum(prod, 1)]
    out = jnp.concatenate(parts, axis=0)
    return out if scale is None else out * scale


def _stack_cols(ref, rows, kvl):
    return jnp.concatenate([ref[0, 4 * kvl + t, rows, :] for t in range(4)], axis=0)


def _swa_fwd(proj, sinks, S, li):
    T = proj.shape[0]
    B = T // S
    bq, nq, nsub = _swa_blocks(S)
    nrow = S // LCH
    qc, zc, kc, vc = OFF_BQ // 512, OFF_BZ // 512, OFF_BK // LANES, OFF_BV // LANES

    def body(sk_ref, q_ref, z_ref, kp_ref, kc_ref, vp_ref, vc_ref, y_ref, o_ref, lse_ref):
        c, i = pl.program_id(0), pl.program_id(2)
        m0 = _lane_iota() < HD
        kw = jnp.concatenate([kp_ref[...].astype(F32), kc_ref[...].astype(F32)], axis=0)
        vw = jnp.concatenate([vp_ref[...].astype(F32), vc_ref[...].astype(F32)], axis=0)
        kd = (_dup_head(kw, 0), _dup_head(kw, 1))
        vd = (_dup_head(vw, 0), _dup_head(vw, 1))
        valid = jnp.concatenate([_band(False), _band(True)], axis=1)
        col = lax.broadcasted_iota(jnp.int32, (LCH, 2 * LCH), 1)
        valid_first = jnp.logical_and(valid, jnp.logical_or(col >= LCH, i > 0))
        valid4 = jnp.concatenate([valid] * 4, axis=0)
        valid4_first = jnp.concatenate([valid_first] * 4, axis=0)
        for r in range(nsub):
            rows = slice(LCH * r, LCH * (r + 1))
            msk = valid4_first if r == 0 else valid4
            for kvl in range(2):
                kwin = kd[kvl][LCH * r:LCH * (r + 2)]
                vwin = vd[kvl][LCH * r:LCH * (r + 2)]
                qs = _stack_heads(q_ref, rows, kvl)
                sink = jnp.concatenate([jnp.full((LCH, 1), sk_ref[8 * c + 4 * kvl + t], F32) for t in range(4)], axis=0)
                s = jnp.where(msk, _dot_nt(qs, kwin) * SCALE, NEG)
                m = jnp.maximum(jnp.max(s, axis=1, keepdims=True), sink)
                p = jnp.exp(s - m)
                l = jnp.sum(p, axis=1, keepdims=True) + jnp.exp(sink - m)
                os_ = _dot(p.astype(BF16), vwin) / l
                lse = m + jnp.log(l)
                for t in range(4):
                    lse_ref[0, 4 * kvl + t, rows, :] = lse[LCH * t:LCH * (t + 1)]
                for u in range(2):
                    lanes = slice(LANES * (2 * kvl + u), LANES * (2 * kvl + u + 1))
                    o2 = jnp.where(m0, os_[LCH * 2 * u:LCH * (2 * u + 1)], os_[LCH * (2 * u + 1):LCH * (2 * u + 2)])
                    z = z_ref[rows, lanes].astype(F32)
                    o_ref[rows, lanes] = o2.astype(BF16)
                    y_ref[rows, lanes] = (o2 * z * _sigmoid(z)).astype(BF16)

    wide = lambda cc: pl.BlockSpec((bq, 512), lambda c, b, i, cc=cc: (b * nq + i, cc + c))
    cur = lambda cc: pl.BlockSpec((bq, LANES), lambda c, b, i, cc=cc: (b * nq + i, cc + c))
    prev = lambda cc: pl.BlockSpec((LCH, LANES), lambda c, b, i, cc=cc: (b * nrow + jnp.maximum(i * nsub - 1, 0), cc + c))
    return pl.pallas_call(
        body, name=f"swa_fwd_{li}", grid=(2, B, nq),
        in_specs=[pl.BlockSpec(memory_space=pltpu.SMEM), wide(qc), wide(zc), prev(kc), cur(kc), prev(vc), cur(vc)],
        out_specs=[wide(0), wide(0), pl.BlockSpec((1, 8, bq, 1), lambda c, b, i: (b, c, i, 0))],
        out_shape=[jax.ShapeDtypeStruct((T, D), BF16), jax.ShapeDtypeStruct((T, D), BF16),
                   jax.ShapeDtypeStruct((B, NH, S, 1), F32)],
        compiler_params=_cparams(("parallel", "parallel", "parallel"), VMEM_LIMIT),
    )(sinks, proj, proj, proj, proj, proj, proj)


def _swa_bwd_dq(proj, do, o, lse, sinks, cos128, sin128, S, li):
    T = proj.shape[0]
    B = T // S
    bq, nq, nsub = _swa_blocks(S)
    nrow = S // LCH
    qc, kc, vc = OFF_BQ // 512, OFF_BK // LANES, OFF_BV // LANES

    def body(sk_ref, q_ref, do_ref, o_ref, lse_ref, kp_ref, kc_ref, vp_ref, vc_ref, cos_ref, sin_ref, dq_ref, dsk_ref):
        c, b, i = pl.program_id(0), pl.program_id(1), pl.program_id(2)

        @pl.when(jnp.logical_and(b == 0, i == 0))
        def _():
            dsk_ref[...] = jnp.zeros_like(dsk_ref)

        m0 = _lane_iota() < HD
        kw = jnp.concatenate([kp_ref[...].astype(F32), kc_ref[...].astype(F32)], axis=0)
        vw = jnp.concatenate([vp_ref[...].astype(F32), vc_ref[...].astype(F32)], axis=0)
        kd = (_dup_head(kw, 0), _dup_head(kw, 1))
        vd = (_dup_head(vw, 0), _dup_head(vw, 1))
        valid = jnp.concatenate([_band(False), _band(True)], axis=1)
        col = lax.broadcasted_iota(jnp.int32, (LCH, 2 * LCH), 1)
        valid_first = jnp.logical_and(valid, jnp.logical_or(col >= LCH, i > 0))
        dsk = [jnp.zeros((1, 1), F32) for _ in range(8)]
        valid4 = jnp.concatenate([valid] * 4, axis=0)
        valid4_first = jnp.concatenate([valid_first] * 4, axis=0)
        for r in range(nsub):
            rows = slice(LCH * r, LCH * (r + 1))
            msk = valid4_first if r == 0 else valid4
            for kvl in range(2):
                kwin = kd[kvl][LCH * r:LCH * (r + 2)]
                vwin = vd[kvl][LCH * r:LCH * (r + 2)]
                qs = _stack_heads(q_ref, rows, kvl)
                dos = _stack_heads(do_ref, rows, kvl)
                delta = _stack_delta(do_ref, o_ref, rows, kvl)
                lse = _stack_cols(lse_ref, rows, kvl)
                sink = jnp.concatenate([jnp.full((LCH, 1), sk_ref[8 * c + 4 * kvl + t], F32) for t in range(4)], axis=0)
                s = jnp.where(msk, _dot_nt(qs, kwin) * SCALE, NEG)
                p = jnp.exp(s - lse)
                ds = p * (_dot_nt(dos, vwin) - delta)
                dqs = _dot(ds.astype(BF16), kwin) * SCALE
                dsink = jnp.exp(sink - lse) * delta
                for t in range(4):
                    hl = 4 * kvl + t
                    dsk[hl] = dsk[hl] - jnp.sum(dsink[LCH * t:LCH * (t + 1)], axis=0, keepdims=True)
                for u in range(2):
                    lanes = slice(LANES * (2 * kvl + u), LANES * (2 * kvl + u + 1))
                    dq2 = jnp.where(m0, dqs[LCH * 2 * u:LCH * (2 * u + 1)], dqs[LCH * (2 * u + 1):LCH * (2 * u + 2)])
                    dq2 = dq2 * cos_ref[rows, :] - _rot_half(dq2) * sin_ref[rows, :]
                    dq_ref[rows, lanes] = dq2.astype(BF16)
        for hl in range(8):
            dsk_ref[0, hl:hl + 1, :] += jnp.broadcast_to(dsk[hl], (1, LANES))

    wide = lambda cc: pl.BlockSpec((bq, 512), lambda c, b, i, cc=cc: (b * nq + i, cc + c))
    cur = lambda cc: pl.BlockSpec((bq, LANES), lambda c, b, i, cc=cc: (b * nq + i, cc + c))
    prev = lambda cc: pl.BlockSpec((LCH, LANES), lambda c, b, i, cc=cc: (b * nrow + jnp.maximum(i * nsub - 1, 0), cc + c))
    pos = pl.BlockSpec((bq, LANES), lambda c, b, i: (i, 0))
    return pl.pallas_call(
        body, name=f"swa_bwd_dq_{li}", grid=(2, B, nq),
        in_specs=[pl.BlockSpec(memory_space=pltpu.SMEM), wide(qc), wide(0), wide(0),
                  pl.BlockSpec((1, 8, bq, 1), lambda c, b, i: (b, c, i, 0)),
                  prev(kc), cur(kc), prev(vc), cur(vc), pos, pos],
        out_specs=[wide(0), pl.BlockSpec((1, 8, LANES), lambda c, b, i: (c, 0, 0))],
        out_shape=[jax.ShapeDtypeStruct((T, D), BF16), jax.ShapeDtypeStruct((2, 8, LANES), F32)],
        compiler_params=_cparams(("arbitrary", "arbitrary", "arbitrary"), VMEM_LIMIT),
    )(sinks, proj, do, o, lse, proj, proj, proj, proj, cos128, sin128)


def _swa_bwd_dkv(proj, do, o, lse, cos128, sin128, S, li):
    T = proj.shape[0]
    B = T // S
    bk, nk, nsub = _swa_blocks(S)
    nrow = S // LCH
    qc, kc, vc = OFF_BQ // 512, OFF_BK // LANES, OFF_BV // LANES

    def body(q_ref, qn_ref, do_ref, don_ref, o_ref, on_ref, lse_ref, lsen_ref, k_ref, v_ref, cos_ref, sin_ref,
             dk_ref, dv_ref):
        j = pl.program_id(2)
        m0 = _lane_iota() < HD
        has_next = (j < nk - 1).astype(F32)
        kf = k_ref[...].astype(F32)
        vf = v_ref[...].astype(F32)
        kd = (_dup_head(kf, 0), _dup_head(kf, 1))
        vd = (_dup_head(vf, 0), _dup_head(vf, 1))
        masks4 = (jnp.concatenate([_band(True)] * 4, axis=0), jnp.concatenate([_band(False)] * 4, axis=0))
        for kr in range(nsub):
            krows = slice(LCH * kr, LCH * (kr + 1))
            dk = jnp.zeros((LCH, LANES), F32)
            dv = jnp.zeros((LCH, LANES), F32)
            for dq_blk in range(2):
                rq = kr + dq_blk
                nxt = rq == nsub
                qrows = slice(0, LCH) if nxt else slice(LCH * rq, LCH * (rq + 1))
                qr, dor, orr, lr = (qn_ref, don_ref, on_ref, lsen_ref) if nxt else (q_ref, do_ref, o_ref, lse_ref)
                for kvl in range(2):
                    qs = _stack_heads(qr, qrows, kvl)
                    dos = _stack_heads(dor, qrows, kvl)
                    delta = _stack_delta(dor, orr, qrows, kvl, has_next if nxt else None)
                    if nxt:
                        dos = (dos.astype(F32) * has_next).astype(BF16)
                    s = jnp.where(masks4[dq_blk], _dot_nt(qs, kd[kvl][krows]) * SCALE, NEG)
                    p = jnp.exp(s - _stack_cols(lr, qrows, kvl))
                    ds = p * (_dot_nt(dos, vd[kvl][krows]) - delta)
                    dvc = _dot_tn(p.astype(BF16), dos)
                    dkc = _dot_tn(ds.astype(BF16), qs) * SCALE
                    own = m0 if kvl == 0 else jnp.logical_not(m0)
                    dv = dv + jnp.where(own, dvc + pltpu.roll(dvc, HD, 1), 0.0)
                    dk = dk + jnp.where(own, dkc + pltpu.roll(dkc, HD, 1), 0.0)
            dk = dk * cos_ref[krows, :] - _rot_half(dk) * sin_ref[krows, :]
            dk_ref[krows, :] = dk.astype(BF16)
            dv_ref[krows, :] = dv.astype(BF16)

    wide = lambda cc: pl.BlockSpec((bk, 512), lambda c, b, j, cc=cc: (b * nk + j, cc + c))
    nxt = lambda cc: pl.BlockSpec((LCH, 512), lambda c, b, j, cc=cc: (b * nrow + jnp.minimum((j + 1) * nsub, nrow - 1), cc + c))
    cur = lambda cc: pl.BlockSpec((bk, LANES), lambda c, b, j, cc=cc: (b * nk + j, cc + c))
    pos = pl.BlockSpec((bk, LANES), lambda c, b, j: (j, 0))
    return pl.pallas_call(
        body, name=f"swa_bwd_dkv_{li}", grid=(2, B, nk),
        in_specs=[wide(qc), nxt(qc), wide(0), nxt(0), wide(0), nxt(0),
                  pl.BlockSpec((1, 8, bk, 1), lambda c, b, j: (b, c, j, 0)),
                  pl.BlockSpec((1, 8, LCH, 1), lambda c, b, j: (b, c, jnp.minimum((j + 1) * nsub, nrow - 1), 0)),
                  cur(kc), cur(vc), pos, pos],
        out_specs=[cur(0), cur(0)],
        out_shape=[jax.ShapeDtypeStruct((T, 2 * LANES), BF16), jax.ShapeDtypeStruct((T, 2 * LANES), BF16)],
        compiler_params=_cparams(("parallel", "parallel", "parallel"), VMEM_LIMIT),
    )(proj, proj, do, do, o, o, lse, lse, proj, proj, cos128, sin128)


HALO = 16


def _shift_matrices():
    r = lax.broadcasted_iota(jnp.int32, (3 * LCH, LCH + HALO), 0)
    c = lax.broadcasted_iota(jnp.int32, (3 * LCH, LCH + HALO), 1)
    t, d = r % LCH, r // LCH + 1
    return (c == HALO + t - d).astype(BF16), (c == t + d).astype(BF16)


def _ssm_chunk_pre(prev16, cur16, first, sdn_ref, cw_ref, cb_ref, ps, dtb, alog):
    ext16 = jnp.concatenate([jnp.where(first, jnp.zeros_like(prev16), prev16), cur16], axis=0)
    sh = _dot(sdn_ref[...], ext16)
    pre = cb_ref[...] + cw_ref[3:4, :] * cur16.astype(F32)
    for d in range(1, 4):
        pre = pre + cw_ref[3 - d:4 - d, :] * sh[LCH * (d - 1):LCH * d]
    sg = _sigmoid(pre)
    dt = _softplus(ps + dtb)
    a = -jnp.exp(alog)
    r = lax.broadcasted_iota(jnp.int32, (LCH, LCH), 0)
    c = lax.broadcasted_iota(jnp.int32, (LCH, LCH), 1)
    acum = _dot_hi((r >= c).astype(F32), dt * a)
    return pre, sg, dt, a, acum, sh


def _pairsel(v, p):
    return jnp.where(_lane_iota() < HD, v[:, 2 * p:2 * p + 1], v[:, 2 * p + 1:2 * p + 2])


def _decay(acum, acum_t, h):
    r = lax.broadcasted_iota(jnp.int32, (LCH, LCH), 0)
    c = lax.broadcasted_iota(jnp.int32, (LCH, LCH), 1)
    causal = r >= c
    seg = acum[:, h:h + 1] - acum_t[h:h + 1, :]
    return jnp.where(causal, jnp.exp(jnp.where(causal, seg, 0.0)), 0.0)


def _ssm_pair_fwd(p, x, dt, acum, acum_t, e_all, w_all, cd, cb_g, b_g, c_g, hprev, dsk_ref):
    m0 = _lane_iota() < HD
    lanes = slice(LANES * p, LANES * (p + 1))
    x2 = x[:, lanes]
    dt2 = _pairsel(dt, p)
    xdt2 = x2 * dt2
    xdtb = xdt2.astype(BF16)
    lms, ms, yds = [], [], []
    for hh in range(2):
        lm = _decay(acum, acum_t, 2 * p + hh)
        mm = cb_g * lm
        lms.append(lm)
        ms.append(mm)
        yds.append(_dot(mm.astype(BF16), xdtb))
    yd2 = jnp.where(m0, yds[0], yds[1])
    w2 = _pairsel(w_all, p)
    xw = (xdt2 * w2).astype(BF16)
    s2 = _dot_tn(xw, b_g)
    z2 = _dot_nt(c_g, hprev.astype(BF16))
    e2 = _pairsel(e_all, p)
    rowsel = lax.broadcasted_iota(jnp.int32, (LANES, 1), 0) < HD
    cdcol = jnp.where(rowsel, cd[:, 2 * p:2 * p + 1], cd[:, 2 * p + 1:2 * p + 2])
    y2 = yd2 + z2 * e2 + dsk_ref[:, lanes] * x2
    return dict(x2=x2, dt2=dt2, xdt2=xdt2, xdtb=xdtb, lms=lms, ms=ms, yd2=yd2, w2=w2, xw=xw, s2=s2, z2=z2, e2=e2,
                cdcol=cdcol, y2=y2)


def _ssm_specs(S, rev):
    nc = S // LCH
    ch = (lambda c: nc - 1 - c) if rev else (lambda c: c)
    prev = pl.BlockSpec((HALO, 2 * D), lambda b, c: (jnp.maximum(b * (S // HALO) + ch(c) * (LCH // HALO) - 1, 0), 0))
    cur = pl.BlockSpec((LCH, 2 * D), lambda b, c: (b * nc + ch(c), 0))
    zed = pl.BlockSpec((LCH, D), lambda b, c: (b * nc + ch(c), OFF_AZ // D))
    row = pl.BlockSpec((LCH, D), lambda b, c: (b * nc + ch(c), 0))
    psb = pl.BlockSpec((LCH, LANES), lambda b, c: (b * nc + ch(c), 0))
    hpb = pl.BlockSpec((1, 1, NH // 2, LANES, NST), lambda b, c: (b, ch(c), 0, 0, 0))
    const = lambda r, w: pl.BlockSpec((r, w), lambda b, c: (0, 0))
    return nc, prev, cur, zed, row, psb, hpb, const


def _ssm_fwd(proj, ps, cw, cb, dtb, alog, dsk, nw, S, li):
    T = proj.shape[0]
    B = T // S
    nc, prev, cur, zed, row, psb, hpb, const = _ssm_specs(S, False)

    def body(prev_ref, cur_ref, z_ref, ps_ref, sdn_ref, cw_ref, cb_ref, dtb_ref, alog_ref, dsk_ref, nw_ref,
             ya_ref, hp_ref, h_scr):
        c = pl.program_id(1)

        @pl.when(c == 0)
        def _():
            h_scr[...] = jnp.zeros_like(h_scr)

        pre, sg, dt, a, acum, _ = _ssm_chunk_pre(prev_ref[...], cur_ref[...], c == 0, sdn_ref, cw_ref, cb_ref,
                                                 ps_ref[...], dtb_ref[...], alog_ref[...])
        act = pre * sg
        acum_t = acum.T
        e_all = jnp.exp(acum)
        last = acum[LCH - 1:LCH, :]
        w_all = jnp.exp(last - acum)
        cd = jnp.exp(last)
        x = act[:, :D]
        for g in range(NGRP):
            b_g = act[:, D + NST * g:D + NST * (g + 1)].astype(BF16)
            c_g = act[:, D + NGRP * NST + NST * g:D + NGRP * NST + NST * (g + 1)].astype(BF16)
            cb_g = _dot_nt(c_g, b_g)
            ygs = []
            for p in (2 * g, 2 * g + 1):
                hprev = h_scr[p]
                hp_ref[0, 0, p] = hprev
                f = _ssm_pair_fwd(p, x, dt, acum, acum_t, e_all, w_all, cd, cb_g, b_g, c_g, hprev, dsk_ref)
                h_scr[p] = hprev * f["cdcol"] + f["s2"]
                z2 = z_ref[:, LANES * p:LANES * (p + 1)].astype(F32)
                ygs.append(f["y2"] * z2 * _sigmoid(z2))
            yg = jnp.concatenate(ygs, axis=1)
            r = lax.rsqrt(jnp.mean(yg * yg, axis=1, keepdims=True) + EPS)
            ya_ref[:, 2 * LANES * g:2 * LANES * (g + 1)] = (yg * r * nw_ref[:, 2 * LANES * g:2 * LANES * (g + 1)]).astype(BF16)

    return pl.pallas_call(
        body, name=f"ssm_fwd_{li}", grid=(B, nc),
        in_specs=[prev, cur, zed, psb, const(3 * LCH, LCH + HALO), const(4, 2 * D), const(1, 2 * D), const(1, LANES),
                  const(1, LANES), const(1, D), const(1, D)],
        out_specs=[row, hpb],
        out_shape=[jax.ShapeDtypeStruct((T, D), BF16), jax.ShapeDtypeStruct((B, nc, NH // 2, LANES, NST), F32)],
        scratch_shapes=[pltpu.VMEM((NH // 2, LANES, NST), F32)],
        compiler_params=_cparams(("arbitrary", "arbitrary"), VMEM_LIMIT),
    )(proj, proj, proj, ps, _shift_matrices()[0], cw, cb, dtb, alog, dsk, nw)


def _ssm_bwd(proj, ps, hp, dya, cw, cb, dtb, alog, dsk, nw, S, li, comm=None):
    T = proj.shape[0]
    B = T // S
    nc, prev, cur, zed, row, psb, hpb, const = _ssm_specs(S, True)

    def body(prev_ref, cur_ref, z_ref, ps_ref, hp_ref, dy_ref, sdn_ref, sup_ref, cw_ref, cb_ref, dtb_ref, alog_ref,
             dsk_ref, nw_ref, dxbc_ref, dz_ref, dps_ref, pgw_ref, pg1_ref, pgh_ref, dh_scr, dhead, dact):
        b, cc = pl.program_id(0), pl.program_id(1)
        c = nc - 1 - cc

        @pl.when(jnp.logical_and(b == 0, cc == 0))
        def _():
            pgw_ref[...] = jnp.zeros_like(pgw_ref)
            pg1_ref[...] = jnp.zeros_like(pg1_ref)
            pgh_ref[...] = jnp.zeros_like(pgh_ref)

        @pl.when(cc == 0)
        def _():
            dh_scr[...] = jnp.zeros_like(dh_scr)
            dhead[...] = jnp.zeros_like(dhead)

        psv = ps_ref[...]
        cur16 = cur_ref[...]
        pre, sg, dt, a, acum, sh = _ssm_chunk_pre(prev_ref[...], cur16, c == 0, sdn_ref, cw_ref, cb_ref, psv,
                                                  dtb_ref[...], alog_ref[...])
        act = pre * sg
        acum_t = acum.T
        e_all = jnp.exp(acum)
        last = acum[LCH - 1:LCH, :]
        w_all = jnp.exp(last - acum)
        cd = jnp.exp(last)
        x = act[:, :D]
        lane = _lane_iota()
        m0 = lane < HD
        rowsel = lax.broadcasted_iota(jnp.int32, (LANES, 1), 0) < HD
        is_last_row = lax.broadcasted_iota(jnp.int32, (LCH, 1), 0) == LCH - 1
        dacum_all = jnp.zeros((LCH, LANES), F32)
        ddt_all = jnp.zeros((LCH, LANES), F32)
        dd_row = jnp.zeros((1, LANES), F32)
        for g in range(NGRP):
            b_g = act[:, D + NST * g:D + NST * (g + 1)].astype(BF16)
            c_g = act[:, D + NGRP * NST + NST * g:D + NGRP * NST + NST * (g + 1)].astype(BF16)
            cb_g = _dot_nt(c_g, b_g)
            pairs = (2 * g, 2 * g + 1)
            fs, hps, zs, ygs = [], [], [], []
            for p in pairs:
                hprev = hp_ref[0, 0, p]
                f = _ssm_pair_fwd(p, x, dt, acum, acum_t, e_all, w_all, cd, cb_g, b_g, c_g, hprev, dsk_ref)
                z2 = z_ref[:, LANES * p:LANES * (p + 1)].astype(F32)
                fs.append(f)
                hps.append(hprev)
                zs.append(z2)
                ygs.append(f["y2"] * z2 * _sigmoid(z2))
            gl = slice(2 * LANES * g, 2 * LANES * (g + 1))
            yg = jnp.concatenate(ygs, axis=1)
            r = lax.rsqrt(jnp.mean(yg * yg, axis=1, keepdims=True) + EPS)
            dyn = dy_ref[:, gl].astype(F32)
            gg = dyn * nw_ref[:, gl]
            dyg = r * gg - yg * (r * r * r) * jnp.mean(gg * yg, axis=1, keepdims=True)
            pg1_ref[0:1, gl] += jnp.sum(dyn * yg * r, axis=0, keepdims=True)
            dg_g = jnp.zeros((LCH, LCH), F32)
            db_g = jnp.zeros((LCH, NST), F32)
            dc_g = jnp.zeros((LCH, NST), F32)
            for idx, p in enumerate(pairs):
                f, hprev, z2 = fs[idx], hps[idx], zs[idx]
                lanes = slice(LANES * p, LANES * (p + 1))
                dyg2 = dyg[:, LANES * idx:LANES * (idx + 1)]
                sgz = _sigmoid(z2)
                dy2 = dyg2 * z2 * sgz
                dz_ref[:, lanes] = (dyg2 * f["y2"] * sgz * (1.0 + z2 * (1.0 - sgz))).astype(BF16)
                x2, dt2, xdt2, xdtb, w2, e2, z2m = f["x2"], f["dt2"], f["xdt2"], f["xdtb"], f["w2"], f["e2"], f["z2"]
                dx2 = dsk_ref[:, lanes] * dy2
                dyx = dy2 * x2
                dxdt2 = jnp.zeros((LCH, LANES), F32)
                diag_cols = []
                for hh in range(2):
                    sel = m0 if hh == 0 else jnp.logical_not(m0)
                    dyb = jnp.where(sel, dy2, 0.0).astype(BF16)
                    dm = _dot_nt(dyb, xdtb)
                    dg_g = dg_g + dm * f["lms"][hh]
                    dxdt2 = dxdt2 + _dot_tn(f["ms"][hh].astype(BF16), dyb)
                    em = dm * f["ms"][hh]
                    diag_cols.append(jnp.sum(em, axis=1, keepdims=True) - jnp.sum(em.T, axis=1, keepdims=True))
                dz2m = dy2 * e2
                t_off = dz2m * z2m
                dc_g = dc_g + _dot(dz2m.astype(BF16), hprev.astype(BF16))
                dhprev = _dot_tn(dz2m.astype(BF16), c_g)
                dhn = dh_scr[p]
                dhnb = dhn.astype(BF16)
                dhprev = dhprev + dhn * f["cdcol"]
                t_h = dhn * hprev
                dxw2 = _dot_nt(b_g, dhnb)
                db_g = db_g + _dot(f["xw"], dhnb)
                dxdt2 = dxdt2 + dxw2 * w2
                t_w = dxw2 * xdt2
                dx2 = dx2 + dxdt2 * dt2
                t_dt = dxdt2 * x2
                for hh in range(2):
                    h = 2 * p + hh
                    onehot = (lane == h).astype(F32)
                    w_col = w_all[:, h:h + 1]
                    dw_col = _head_sum(t_w, hh) * w_col
                    rs = rowsel if hh == 0 else jnp.logical_not(rowsel)
                    dlast = (jnp.sum(jnp.where(rs, t_h, 0.0), keepdims=True) * cd[:, h:h + 1]
                             + jnp.sum(dw_col, keepdims=True))
                    dacum_col = diag_cols[hh] + _head_sum(t_off, hh) - dw_col + jnp.where(is_last_row, dlast, 0.0)
                    dacum_all = dacum_all + dacum_col * onehot
                    ddt_all = ddt_all + _head_sum(t_dt, hh) * onehot
                    sel = m0 if hh == 0 else jnp.logical_not(m0)
                    dd_row = dd_row + jnp.sum(jnp.where(sel, dyx, 0.0), keepdims=True) * onehot
                dh_scr[p] = dhprev
                dact[:, lanes] = dx2
            dgb = dg_g.astype(BF16)
            dc_g = dc_g + _dot(dgb, b_g)
            db_g = db_g + _dot_tn(dgb, c_g)
            dact[:, D + NST * g:D + NST * (g + 1)] = db_g
            dact[:, D + NGRP * NST + NST * g:D + NGRP * NST + NST * (g + 1)] = dc_g
        rr = lax.broadcasted_iota(jnp.int32, (LCH, LCH), 0)
        cc2 = lax.broadcasted_iota(jnp.int32, (LCH, LCH), 1)
        dadt = _dot_hi((cc2 >= rr).astype(F32), dacum_all)
        ddt_all = ddt_all + dadt * a
        heads = lane < NH
        da = jnp.sum(dadt * dt, axis=0, keepdims=True)
        dr = jnp.where(heads, ddt_all * _sigmoid(psv + dtb_ref[...]), 0.0)
        dps_ref[...] = dr
        pgh_ref[0:1, :] += jnp.sum(dr, axis=0, keepdims=True)
        pgh_ref[1:2, :] += jnp.where(heads, da * a, 0.0)
        pgh_ref[2:3, :] += dd_row
        dpre = dact[...] * sg * (1.0 + pre * (1.0 - sg))
        extd = jnp.concatenate([dpre, dhead[...]], axis=0)
        hi = extd.astype(BF16)
        lo = (extd - hi.astype(F32)).astype(BF16)
        up = _dot(sup_ref[...], hi) + _dot(sup_ref[...], lo)
        du = cw_ref[3:4, :] * dpre
        pgw_ref[3:4, :] += jnp.sum(dpre * cur16.astype(F32), axis=0, keepdims=True)
        for d in range(1, 4):
            du = du + cw_ref[3 - d:4 - d, :] * up[LCH * (d - 1):LCH * d]
            pgw_ref[3 - d:4 - d, :] += jnp.sum(dpre * sh[LCH * (d - 1):LCH * d], axis=0, keepdims=True)
        pgw_ref[4:5, :] += jnp.sum(dpre, axis=0, keepdims=True)
        dxbc_ref[...] = du.astype(BF16)
        dhead[...] = dpre[0:HALO, :]

    xbc_out = pl.BlockSpec((LCH, 2 * D), lambda b, c: (b * nc + nc - 1 - c, 0))
    acc = lambda w: pl.BlockSpec((8, w), lambda b, c: (0, 0))
    sdn, sup = _shift_matrices()
    return _hosted_call(
        body, comm, f"ssm_bwd_{li}", (B, nc),
        in_specs=[prev, cur, zed, psb, hpb, row, const(3 * LCH, LCH + HALO), const(3 * LCH, LCH + HALO),
                  const(4, 2 * D), const(1, 2 * D), const(1, LANES), const(1, LANES), const(1, D), const(1, D)],
        out_specs=[xbc_out, row, psb, acc(2 * D), acc(D), acc(LANES)],
        out_shape=[jax.ShapeDtypeStruct((T, 2 * D), BF16), jax.ShapeDtypeStruct((T, D), BF16),
                   jax.ShapeDtypeStruct((T, LANES), F32), jax.ShapeDtypeStruct((8, 2 * D), F32),
                   jax.ShapeDtypeStruct((8, D), F32), jax.ShapeDtypeStruct((8, LANES), F32)],
        scratch=[pltpu.VMEM((NH // 2, LANES, NST), F32), pltpu.VMEM((HALO, 2 * D), F32),
                 pltpu.VMEM((LCH, 2 * D), F32)],
        dims=("arbitrary", "arbitrary"),
        operands=(proj, proj, proj, ps, hp, dya, sdn, sup, cw, cb, dtb, alog, dsk, nw))


def _lane_row(v, offset):
    return jnp.pad(v.astype(F32), (offset, LANES - offset - v.shape[0]))[None]


def _pack_rows(arrays):
    parts = []
    for a in arrays:
        flat = a.reshape(-1).astype(F32)
        pad = (-flat.shape[0]) % LANES
        parts.append(jnp.pad(flat, (0, pad)))
    flat = jnp.concatenate(parts)
    pad = (-flat.shape[0]) % (8 * LANES)
    return jnp.pad(flat, (0, pad)).reshape(-1, LANES)


def _unpack_rows(pack, shapes):
    flat = pack.reshape(-1)
    out, pos = [], 0
    for shp in shapes:
        n = math.prod(shp)
        out.append(flat[pos:pos + n].reshape(shp))
        pos += n + (-n) % LANES
    return out


def _split_w_in(w):
    main = jnp.concatenate([w[:, 0:3072], w[:, 3088:4112], w[:, 4624:5648], w[:, 5648:8720], w[:, 8736:12832],
                            w[:, 4112:4624]], axis=1)
    small = jnp.concatenate([w[:, 3072:3088], w[:, 8720:8736], jnp.zeros((D, LANES - 2 * NH), w.dtype)], axis=1)
    return main, small


def _join_w_in(dm, ds):
    return jnp.concatenate([dm[:, 0:3072], ds[:, 0:NH], dm[:, 3072:4096], dm[:, 12288:12800], dm[:, 4096:5120],
                            dm[:, 5120:8192], ds[:, NH:2 * NH], dm[:, 8192:12288]], axis=1)


def kernel(x, norm_w, w_in, conv_w, conv_b, dt_bias, a_log, d_skip, ssm_norm_w, sinks, f_bias, gate_bias, w_proj, w_out, final_norm_w, loss_target, m_norm_w, m_w_in, m_conv_w, m_conv_b, m_dt_bias, m_a_log, m_d_skip, m_ssm_norm_w, m_sinks, m_f_bias, m_gate_bias, m_w_proj, m_w_out, m_final_norm_w, v_norm_w, v_w_in, v_conv_w, v_conv_b, v_dt_bias, v_a_log, v_d_skip, v_ssm_norm_w, v_sinks, v_f_bias, v_gate_bias, v_w_proj, v_w_out, v_final_norm_w):
    Bl, S, _ = x.shape
    T = Bl * S
    depth = norm_w.shape[0]
    me = 4 * lax.axis_index("x") + 2 * lax.axis_index("y") + lax.axis_index("c")
    csh, gsh = conv_w.shape[2], gate_bias.shape[2]

    def gather_plan(l):
        small = jnp.concatenate([conv_w[l].reshape(-1), gate_bias[l].reshape(-1)]).reshape(-1, LANES)
        return _Comm("gather", [w_in[l].astype(BF16), w_proj[l].astype(BF16), w_out[l].astype(BF16), small])

    def unpack_weights(res):
        g_win, g_wp, g_wo, g_small = res
        flat = g_small.reshape(NDEV, -1)
        return (g_win.transpose(1, 0, 2).reshape(D, NIN),
                g_wp.transpose(1, 0, 2, 3).reshape(3, D, D),
                g_wo.reshape(D, D),
                flat[:, :4 * csh].reshape(NDEV, 4, csh).transpose(1, 0, 2).reshape(4, 2 * D),
                flat[:, 4 * csh:].reshape(NDEV, 3, gsh).transpose(1, 0, 2).reshape(3, D))

    def scatter_plan(gw_in, gw_p, gw_o):
        return _Comm("scatter", [gw_in.astype(BF16).reshape(D, NDEV, NSH).transpose(1, 0, 2),
                                 gw_p.astype(BF16).reshape(3, NDEV, D // NDEV, D).transpose(1, 0, 2, 3),
                                 gw_o.astype(BF16).reshape(NDEV, D // NDEV, D)])

    pos = jnp.arange(S, dtype=F32)
    inv_freq = ROPE_THETA ** (-jnp.arange(0, HD, 2, dtype=F32) / HD)
    ang = pos[:, None] * inv_freq[None, :]
    cos128 = jnp.tile(jnp.cos(ang), (1, 4))
    sign = jnp.where((jnp.arange(LANES) % HD) < HD // 2, -1.0, 1.0).astype(F32)
    sin128 = jnp.tile(jnp.sin(ang), (1, 4)) * sign[None, :]

    bq, nq = _fox_blocks(S)
    x2 = x.reshape(T, D)
    tgt2 = loss_target.reshape(T, D)

    saved = []
    xcur = x2
    weights = [None] * depth
    weights[0] = unpack_weights(_gather_two_level(gather_plan(0).arrays, "gather_weights_0"))
    for l in range(depth):
        win_l, wp_l, wo_l, cw_l, gb_l = weights[l]
        wmain, wsmall = _split_w_in(win_l)
        comm = gather_plan(l + 1) if l + 1 < depth else None
        res = _inproj_fwd(xcur, norm_w[l][None], wmain, wsmall, cos128, sin128, S, l, comm)
        proj, ps, h_t = res[:3]
        if comm is not None:
            weights[l + 1] = unpack_weights(res[3:])
        dtb = _lane_row(dt_bias[l], 0)
        alog = _lane_row(a_log[l], 0)
        fb = _lane_row(f_bias[l], NH)
        dsk = jnp.repeat(d_skip[l], HD)[None]
        ya, hp = _ssm_fwd(proj, ps, cw_l, conv_b[l][None], dtb, alog, dsk, ssm_norm_w[l][None], S, l)
        yb, ob, lse_b = _swa_fwd(proj, sinks[l], S, l)
        cum = _fox_cum(ps, fb, S, l)
        cumh = cum[:, NH:2 * NH].reshape(Bl, S, NH).transpose(0, 2, 1)
        cum_col = cumh[..., None]
        yc, oc, lse_c = _fox_fwd(proj, cum_col, S, l)
        xnext, br, y_t = _merge_fwd(ya, yb, yc, proj, gb_l, wp_l, wo_l, xcur, l)
        saved.append(dict(x=xcur, wmain=wmain, wsmall=wsmall, proj=proj, ps=ps, h_t=h_t, dtb=dtb, alog=alog, fb=fb,
                          dsk=dsk, hp=hp, ob=ob, lse_b=lse_b, cum_col=cum_col, oc=oc, lse_c=lse_c, br=br, y_t=y_t))
        xcur = xnext

    dx, dx16, st = _final_loss(xcur, tgt2, final_norm_w[None])
    loss_part = st[2, 0]
    g_final = st[0]

    gsm = {k: [None] * depth for k in ("norm_w", "conv_w", "conv_b", "dt_bias", "a_log", "d_skip", "ssm_norm_w",
                                      "sinks", "f_bias", "gate_bias")}
    parts = [None] * depth
    pending = None
    for l in reversed(range(depth)):
        sv = saved[l]
        proj, ps = sv["proj"], sv["ps"]
        _, wp_l, wo_l, cw_l, gb_l = weights[l]
        dbr, dgates, merged_t, dgb, dy_a, do_b, dbz, do_c, dcz = _merge_bwd(dx16, wo_l, wp_l, sv["br"], proj, gb_l,
                                                                            sv["ob"], sv["oc"], l)
        g_wo = _matmul(merged_t, dx16, F32, f"dwout_{l}")
        g_wp = jnp.stack([_matmul(sv["y_t"][i], dbr[i], F32, f"dwproj_{l}_{i}") for i in range(3)])
        gsm["gate_bias"][l] = dgb[0:3]
        res = _ssm_bwd(proj, ps, sv["hp"], dy_a, cw_l, conv_b[l][None], sv["dtb"], sv["alog"], sv["dsk"],
                       ssm_norm_w[l][None], S, l, pending)
        dxbc, daz, dps_a, pgw, pg1, pgh = res[:6]
        if pending is not None:
            parts[l + 1] = res[6:]
        gsm["conv_w"][l], gsm["conv_b"][l] = pgw[0:4], pgw[4]
        gsm["ssm_norm_w"][l] = pg1[0]
        gsm["dt_bias"][l], gsm["a_log"][l], gsm["d_skip"][l] = pgh[0, :NH], pgh[1, :NH], pgh[2, :NH]
        dq_b, dsk_b = _swa_bwd_dq(proj, do_b, sv["ob"], sv["lse_b"], sinks[l], cos128, sin128, S, l)
        dk_b, dv_b = _swa_bwd_dkv(proj, do_b, sv["ob"], sv["lse_b"], cos128, sin128, S, l)
        gsm["sinks"][l] = dsk_b[:, :, 0].reshape(NH)
        dq_c, dk_c, dv_c, dcum_k, dcum_q = _fox_bwd(proj, do_c, sv["oc"], sv["cum_col"], sv["lse_c"], S, l)
        dcum_tm = (dcum_k.reshape(Bl, NH, S) + dcum_q.reshape(Bl, NH, S)).transpose(0, 2, 1).reshape(T, NH)
        dcum_pad = jnp.pad(dcum_tm, ((0, 0), (NH, LANES - 2 * NH)))
        df, dfb = _fox_cum_bwd(dcum_pad, ps, sv["fb"], S, l)
        gsm["f_bias"][l] = dfb[0, NH:2 * NH]
        dps16 = (dps_a + df).astype(BF16)
        dproj = jnp.concatenate([dxbc, daz, dq_b, dbz, dq_c, dk_c, dv_c, dcz, dgates, dk_b, dv_b], axis=1)
        dwm = _matmul(sv["h_t"], dproj, F32, f"dwin_main_{l}", tm=1024, tn=1280, tk=512)
        dws = _matmul(sv["h_t"], dps16, F32, f"dwin_small_{l}")
        plan = scatter_plan(_join_w_in(dwm, dws), g_wp, g_wo)
        res = _inproj_bwd_dx(dproj, sv["wmain"], dps16, sv["wsmall"], sv["x"], norm_w[l][None], dx, l,
                             plan if l == 0 else None)
        dx, dx16, dnw = res[:3]
        if l == 0:
            parts[0] = res[3:]
        else:
            pending = plan
        gsm["norm_w"][l] = dnw[0]

    big = {}
    for idx, (name, w, m, v) in enumerate((("w_in", w_in, m_w_in, v_w_in), ("w_proj", w_proj, m_w_proj, v_w_proj),
                                          ("w_out", w_out, m_w_out, v_w_out))):
        cols = w.shape[-1]
        res = _sum_adamw([parts[l][idx].reshape(NDEV, -1, cols) for l in range(depth)], w.reshape(depth, -1, cols),
                         m.reshape(depth, -1, cols), v.reshape(depth, -1, cols), f"adamw_{name}")
        big[name] = [r.reshape(w.shape) for r in res]

    small_names = ("norm_w", "conv_b", "dt_bias", "a_log", "d_skip", "ssm_norm_w", "sinks", "f_bias")
    small_parts = [jnp.stack(gsm[k]) for k in small_names] + [g_final, jnp.stack(gsm["conv_w"]),
                                                              jnp.stack(gsm["gate_bias"]), loss_part.reshape(1)]
    shapes = [a.shape for a in small_parts]
    summed = _unpack_rows(_all_reduce_small(_pack_rows(small_parts)), shapes)
    g_small = dict(zip(small_names, summed[:len(small_names)]))
    g_small["final_norm_w"] = summed[len(small_names)]
    g_small["conv_w"] = lax.dynamic_slice_in_dim(summed[len(small_names) + 1], me * csh, csh, axis=2)
    g_small["gate_bias"] = lax.dynamic_slice_in_dim(summed[len(small_names) + 2], me * gsh, gsh, axis=2)
    loss = summed[len(small_names) + 3][0]

    ws = dict(norm_w=norm_w, conv_w=conv_w, conv_b=conv_b, dt_bias=dt_bias, a_log=a_log, d_skip=d_skip,
              ssm_norm_w=ssm_norm_w, sinks=sinks, f_bias=f_bias, gate_bias=gate_bias, final_norm_w=final_norm_w)
    ms = dict(norm_w=m_norm_w, conv_w=m_conv_w, conv_b=m_conv_b, dt_bias=m_dt_bias, a_log=m_a_log, d_skip=m_d_skip,
              ssm_norm_w=m_ssm_norm_w, sinks=m_sinks, f_bias=m_f_bias, gate_bias=m_gate_bias,
              final_norm_w=m_final_norm_w)
    vs = dict(norm_w=v_norm_w, conv_w=v_conv_w, conv_b=v_conv_b, dt_bias=v_dt_bias, a_log=v_a_log, d_skip=v_d_skip,
              ssm_norm_w=v_ssm_norm_w, sinks=v_sinks, f_bias=v_f_bias, gate_bias=v_gate_bias,
              final_norm_w=v_final_norm_w)
    order = list(ws)
    oshapes = [ws[k].shape for k in order]
    res = _adamw_small(_pack_rows([g_small[k] for k in order]), _pack_rows([ws[k] for k in order]),
                       _pack_rows([ms[k] for k in order]), _pack_rows([vs[k] for k in order]))
    d_s, m_s, v_s = (dict(zip(order, _unpack_rows(r, oshapes))) for r in res)

    names = ("norm_w", "w_in", "conv_w", "conv_b", "dt_bias", "a_log", "d_skip", "ssm_norm_w", "sinks", "f_bias",
             "gate_bias", "w_proj", "w_out", "final_norm_w")
    grads, deltas, new_m, new_v = [], [], [], []
    for k in names:
        if k in big:
            g, d_, m_, v_ = big[k]
        else:
            g, d_, m_, v_ = g_small[k], d_s[k], m_s[k], v_s[k]
        grads.append(g)
        deltas.append(d_)
        new_m.append(m_)
        new_v.append(v_)
    return (loss, dx.reshape(Bl, S, D), *grads, *deltas, *new_m, *new_v)
```

```python
import functools
import math

import jax
import jax.numpy as jnp
from jax import lax
from jax.experimental import pallas as pl
from jax.experimental.pallas import tpu as pltpu

F32 = jnp.float32
BF16 = jnp.bfloat16
MESH = pl.DeviceIdType.MESH
NDEV = 8

D = 1024
NH = 16
HD = 64
NST = 128
NGRP = 4
LCH = 128
EPS = 1e-6
ROPE_THETA = 10000.0
SCALE = HD ** -0.5
NEG = -1e30

LANES = 128
VMEM_LIMIT = 56 * 1024 * 1024

OFF_XBC, OFF_AZ, OFF_BQ, OFF_BZ, OFF_CQ, OFF_CK, OFF_CV, OFF_CZ, OFF_G, OFF_BK, OFF_BV = (
    0, 2048, 3072, 4096, 5120, 6144, 7168, 8192, 9216, 12288, 12544)
NMAIN = 12800
NIN = 12832
NSH = NIN // NDEV

ADAM_LR, ADAM_B1, ADAM_B2, ADAM_EPS, ADAM_WD, ADAM_STEP = 0.001, 0.9, 0.999, 1e-08, 0.01, 10


def _cparams(dims=None, vmem=None):
    return pltpu.CompilerParams(dimension_semantics=dims, vmem_limit_bytes=vmem)


def _dot(a, b):
    return jnp.dot(a, b, preferred_element_type=F32)


def _dot_nt(a, b):
    return lax.dot_general(a, b, (((1,), (1,)), ((), ())), preferred_element_type=F32)


def _dot_tn(a, b):
    return lax.dot_general(a, b, (((0,), (0,)), ((), ())), preferred_element_type=F32)


def _dot_hi(a, b):
    return jnp.dot(a, b, precision=lax.Precision.HIGHEST, preferred_element_type=F32)


def _sigmoid(x):
    return 1.0 / (1.0 + jnp.exp(-x))


def _softplus(x):
    return jnp.maximum(x, 0.0) + jnp.log(1.0 + jnp.exp(-jnp.abs(x)))


def _lane_iota(n=LANES):
    return lax.broadcasted_iota(jnp.int32, (1, n), 1)


def _rot_half(x):
    first = (_lane_iota() % HD) < (HD // 2)
    return jnp.where(first, pltpu.roll(x, LANES - HD // 2, 1), pltpu.roll(x, HD // 2, 1))


def _head_sum(x, head):
    m = (_lane_iota() < HD) if head == 0 else (_lane_iota() >= HD)
    return jnp.sum(jnp.where(m, x, 0.0), axis=1, keepdims=True)


def _me_and_peers():
    x, y, c = lax.axis_index("x"), lax.axis_index("y"), lax.axis_index("c")
    me = 4 * x + 2 * y + c
    peers = []
    for k in range(1, NDEV):
        kx, ky, kc = (k >> 2) & 1, (k >> 1) & 1, k & 1
        px, py, pc = x ^ kx, y ^ ky, c ^ kc
        peers.append(((px, py, pc), 4 * px + 2 * py + pc))
    return me, peers


class _Comm:
    def __init__(self, kind, arrays):
        self.kind, self.arrays, self.n = kind, list(arrays), len(arrays)
        any_spec = pl.BlockSpec(memory_space=pl.ANY)
        self.in_specs = [any_spec] * self.n
        self.out_specs = [any_spec] * self.n
        self.out_shape = [jax.ShapeDtypeStruct(((NDEV,) + a.shape) if kind == "gather" else a.shape, a.dtype)
                          for a in self.arrays]
        self.scratch = [pltpu.SemaphoreType.DMA((self.n, NDEV - 1)), pltpu.SemaphoreType.DMA((self.n, NDEV - 1)),
                        pltpu.SemaphoreType.DMA((self.n,))]

    def copies(self, ins, outs, sems):
        send_sems, recv_sems, local_sems = sems
        me, peers = _me_and_peers()
        out = []
        for a in range(self.n):
            mine = ins[a] if self.kind == "gather" else ins[a].at[me]
            out.append(pltpu.make_async_copy(mine, outs[a].at[me], local_sems.at[a]))
            for k, (peer, pidx) in enumerate(peers):
                src = ins[a] if self.kind == "gather" else ins[a].at[pidx]
                out.append(pltpu.make_async_remote_copy(
                    src_ref=src, dst_ref=outs[a].at[me], send_sem=send_sems.at[a, k], recv_sem=recv_sems.at[a, k],
                    device_id=peer, device_id_type=MESH))
        return out

    def call(self, name):
        def body(*refs):
            cps = self.copies(refs[:self.n], refs[self.n:2 * self.n], refs[2 * self.n:])
            for cp in cps:
                cp.start()
            for cp in cps:
                cp.wait()

        return pl.pallas_call(body, name=name, out_shape=self.out_shape, in_specs=self.in_specs,
                              out_specs=self.out_specs, scratch_shapes=self.scratch)(*self.arrays)


def _gather_two_level(arrays, name):
    n = len(arrays)

    def body(*refs):
        ins, outs = refs[:n], refs[n:2 * n]
        send_sems, recv_sems, local_sems = refs[2 * n:]
        x, y, c = lax.axis_index("x"), lax.axis_index("y"), lax.axis_index("c")
        me, sibling = (x, y, c), (x, y, 1 - c)
        chips = [(1 - x, y), (x, 1 - y), (1 - x, 1 - y)]

        def slot(a, dev):
            return outs[a].at[4 * dev[0] + 2 * dev[1] + dev[2]]

        def copy(a, k, block, to, src=None):
            return pltpu.make_async_remote_copy(
                src_ref=slot(a, block) if src is None else src, dst_ref=slot(a, block),
                send_sem=send_sems.at[a, k], recv_sem=recv_sems.at[a, k], device_id=to, device_id_type=MESH)

        mine = [pltpu.make_async_copy(ins[a], slot(a, me), local_sems.at[a]) for a in range(n)]
        for cp in mine:
            cp.start()
        first = []
        for a in range(n):
            first.append(copy(a, 0, me, sibling, src=ins[a]))
            first += [copy(a, 1 + j, me, (*chip, c), src=ins[a]) for j, chip in enumerate(chips)]
        for cp in first:
            cp.start()
        passed = []
        for j, chip in enumerate(chips):
            for a in range(n):
                copy(a, 1 + j, (*chip, c), me).wait_recv()
                fwd = copy(a, 4 + j, (*chip, c), sibling)
                fwd.start()
                passed.append(fwd)
        for a in range(n):
            copy(a, 0, sibling, me).wait_recv()
            for j, chip in enumerate(chips):
                copy(a, 4 + j, (*chip, 1 - c), me).wait_recv()
        for cp in first + passed:
            cp.wait_send()
        for cp in mine:
            cp.wait()

    any_spec = pl.BlockSpec(memory_space=pl.ANY)
    return pl.pallas_call(
        body, name=name, out_shape=[jax.ShapeDtypeStruct((NDEV,) + a.shape, a.dtype) for a in arrays],
        in_specs=[any_spec] * n, out_specs=[any_spec] * n,
        scratch_shapes=[pltpu.SemaphoreType.DMA((n, NDEV - 1)), pltpu.SemaphoreType.DMA((n, NDEV - 1)),
                        pltpu.SemaphoreType.DMA((n,))])(*arrays)


def _hosted_call(body, comm, name, grid, in_specs, out_specs, out_shape, scratch, dims, operands):
    if comm is None:
        return pl.pallas_call(body, name=name, grid=grid, in_specs=in_specs, out_specs=out_specs, out_shape=out_shape,
                              scratch_shapes=scratch, compiler_params=_cparams(dims, VMEM_LIMIT))(*operands)
    n_in, n_out, n_scr, n = len(in_specs), len(out_specs), len(scratch), comm.n

    def hosted(*refs):
        hin, cin = refs[:n_in], refs[n_in:n_in + n]
        hout = refs[n_in + n:n_in + n + n_out]
        cout = refs[n_in + n + n_out:n_in + 2 * n + n_out]
        hscr = refs[n_in + 2 * n + n_out:n_in + 2 * n + n_out + n_scr]
        sems = refs[n_in + 2 * n + n_out + n_scr:]
        ids = [pl.program_id(a) for a in range(len(grid))]
        first = functools.reduce(jnp.logical_and, [i == 0 for i in ids])
        last = functools.reduce(jnp.logical_and, [i == g - 1 for i, g in zip(ids, grid)])

        @pl.when(first)
        def _():
            for cp in comm.copies(cin, cout, sems):
                cp.start()

        body(*hin, *hout, *hscr)

        @pl.when(last)
        def _():
            for cp in comm.copies(cin, cout, sems):
                cp.wait()

    return pl.pallas_call(
        hosted, name=name, grid=grid, in_specs=list(in_specs) + comm.in_specs,
        out_specs=list(out_specs) + comm.out_specs, out_shape=list(out_shape) + comm.out_shape,
        scratch_shapes=list(scratch) + comm.scratch,
        compiler_params=_cparams(("arbitrary",) * len(grid), VMEM_LIMIT))(*operands, *comm.arrays)


def _all_reduce_small(v):
    rows = v.shape[0]

    def body(v_ref, sum_ref, all_ref, send_sems, recv_sems):
        me, peers = _me_and_peers()
        all_ref[me] = v_ref[...]
        copies = []
        for k, (peer, _) in enumerate(peers):
            cp = pltpu.make_async_remote_copy(
                src_ref=v_ref, dst_ref=all_ref.at[me],
                send_sem=send_sems.at[k], recv_sem=recv_sems.at[k],
                device_id=peer, device_id_type=MESH)
            cp.start()
            copies.append(cp)
        for cp in copies:
            cp.wait()
        acc = all_ref[0]
        for d in range(1, NDEV):
            acc = acc + all_ref[d]
        sum_ref[...] = acc

    vm = pl.BlockSpec(memory_space=pltpu.VMEM)
    return pl.pallas_call(
        body, name="all_reduce_small",
        out_shape=jax.ShapeDtypeStruct((rows, LANES), F32),
        in_specs=[vm], out_specs=vm,
        scratch_shapes=[pltpu.VMEM((NDEV, rows, LANES), F32),
                        pltpu.SemaphoreType.DMA((NDEV - 1,)), pltpu.SemaphoreType.DMA((NDEV - 1,))],
    )(v)


def _adamw_math(w, g, m, v):
    m = ADAM_B1 * m + (1.0 - ADAM_B1) * g
    v = ADAM_B2 * v + (1.0 - ADAM_B2) * jnp.square(g)
    m_hat = m / (1.0 - ADAM_B1 ** ADAM_STEP)
    v_hat = v / (1.0 - ADAM_B2 ** ADAM_STEP)
    delta = -ADAM_LR * (m_hat / (jnp.sqrt(v_hat) + ADAM_EPS) + ADAM_WD * w)
    return delta, m, v


def _sum_adamw(parts, w, m, v, name):
    depth, rows, cols = w.shape
    tr = next(c for c in (256, 128, 64, 32, 16) if rows % c == 0)
    nb = rows // tr

    def body(*refs):
        p_refs, (w_ref, m_ref, v_ref, g_ref, d_ref, nm_ref, nv_ref) = refs[:depth], refs[depth:]
        l = pl.program_id(0)
        for ll in range(depth):
            @pl.when(l == ll)
            def _(ll=ll):
                g = p_refs[ll][0].astype(F32)
                for d in range(1, NDEV):
                    g = g + p_refs[ll][d].astype(F32)
                delta, nm, nv = _adamw_math(w_ref[0], g, m_ref[0], v_ref[0])
                g_ref[0] = g
                d_ref[0] = delta
                nm_ref[0] = nm
                nv_ref[0] = nv

    part = lambda ll: pl.BlockSpec((NDEV, tr, cols), lambda l, i, ll=ll: (0, jnp.where(l == ll, i, jnp.where(l < ll, 0, nb - 1)), 0))
    blk = pl.BlockSpec((1, tr, cols), lambda l, i: (l, i, 0))
    sds = jax.ShapeDtypeStruct((depth, rows, cols), F32)
    return pl.pallas_call(
        body, name=name, grid=(depth, nb),
        in_specs=[part(ll) for ll in range(depth)] + [blk, blk, blk],
        out_specs=[blk, blk, blk, blk], out_shape=[sds, sds, sds, sds],
        compiler_params=_cparams(("arbitrary", "arbitrary"), VMEM_LIMIT),
    )(*parts, w, m, v)


def _adamw_small(g, w, m, v):
    def body(g_ref, w_ref, m_ref, v_ref, d_ref, nm_ref, nv_ref):
        delta, nm, nv = _adamw_math(w_ref[...], g_ref[...], m_ref[...], v_ref[...])
        d_ref[...] = delta
        nm_ref[...] = nm
        nv_ref[...] = nv

    sds = jax.ShapeDtypeStruct(g.shape, F32)
    return pl.pallas_call(body, name="adamw_small", out_shape=[sds, sds, sds])(g, w, m, v)


def _matmul(a, b, out_dtype, name, tm=1024, tn=1024, tk=512):
    M, K = a.shape
    N = b.shape[1]
    tm, tn, tk = min(tm, M), min(tn, N), min(tk, K)
    nk = K // tk

    def body(a_ref, b_ref, o_ref, acc):
        k = pl.program_id(2)

        @pl.when(k == 0)
        def _():
            acc[...] = jnp.zeros_like(acc)

        acc[...] += _dot(a_ref[...], b_ref[...])

        @pl.when(k == nk - 1)
        def _():
            o_ref[...] = acc[...].astype(out_dtype)

    return pl.pallas_call(
        body, name=name, grid=(M // tm, N // tn, nk),
        in_specs=[pl.BlockSpec((tm, tk), lambda i, j, k: (i, k)), pl.BlockSpec((tk, tn), lambda i, j, k: (k, j))],
        out_specs=pl.BlockSpec((tm, tn), lambda i, j, k: (i, j)),
        out_shape=jax.ShapeDtypeStruct((M, N), out_dtype),
        scratch_shapes=[pltpu.VMEM((tm, tn), F32)],
        compiler_params=_cparams(("parallel", "parallel", "arbitrary"), VMEM_LIMIT),
    )(a, b)


def _matmul_batched(a, b, out_dtype, name, tm=1024, tn=1024, tk=512):
    G, M, K = a.shape
    N = b.shape[2]
    tm, tn, tk = min(tm, M), min(tn, N), min(tk, K)
    nk = K // tk

    def body(a_ref, b_ref, o_ref, acc):
        k = pl.program_id(3)

        @pl.when(k == 0)
        def _():
            acc[...] = jnp.zeros_like(acc)

        acc[...] += _dot(a_ref[0], b_ref[0])

        @pl.when(k == nk - 1)
        def _():
            o_ref[0] = acc[...].astype(out_dtype)

    return pl.pallas_call(
        body, name=name, grid=(G, M // tm, N // tn, nk),
        in_specs=[pl.BlockSpec((1, tm, tk), lambda g, i, j, k: (g, i, k)),
                  pl.BlockSpec((1, tk, tn), lambda g, i, j, k: (g, k, j))],
        out_specs=pl.BlockSpec((1, tm, tn), lambda g, i, j, k: (g, i, j)),
        out_shape=jax.ShapeDtypeStruct((G, M, N), out_dtype),
        scratch_shapes=[pltpu.VMEM((tm, tn), F32)],
        compiler_params=_cparams(("parallel", "parallel", "parallel", "arbitrary"), VMEM_LIMIT),
    )(a, b)


def _inproj_fwd(x2, nw, wmain, wsmall, cos128, sin128, S, li, comm=None):
    T = x2.shape[0]
    tm, tn = min(1024, S), 512
    nj, npos = NMAIN // tn, S // tm
    jq0, jk = OFF_BQ // tn, OFF_BK // tn

    def body(x_ref, nw_ref, w_ref, ws_ref, cos_ref, sin_ref, proj_ref, ps_ref, ht_ref, h_scr):
        j = pl.program_id(1)

        @pl.when(j == 0)
        def _():
            x = x_ref[...]
            r = lax.rsqrt(jnp.mean(x * x, axis=-1, keepdims=True) + EPS)
            h = (x * r * nw_ref[...]).astype(BF16)
            h_scr[...] = h
            ht_ref[...] = h.T
            ps_ref[...] = _dot(h, ws_ref[...])

        acc = _dot(h_scr[...], w_ref[...])

        def roped(c):
            xc = acc[:, LANES * c:LANES * (c + 1)]
            return (xc * cos_ref[...] + _rot_half(xc) * sin_ref[...]).astype(BF16)

        def plain(c):
            return acc[:, LANES * c:LANES * (c + 1)].astype(BF16)

        is_q = jnp.logical_or(j == jq0, j == jq0 + 1)
        is_k = j == jk

        @pl.when(is_q)
        def _():
            for c in range(4):
                proj_ref[:, LANES * c:LANES * (c + 1)] = roped(c)

        @pl.when(is_k)
        def _():
            for c in range(4):
                proj_ref[:, LANES * c:LANES * (c + 1)] = roped(c) if c < 2 else plain(c)

        @pl.when(jnp.logical_not(jnp.logical_or(is_q, is_k)))
        def _():
            proj_ref[...] = acc.astype(BF16)

    return _hosted_call(
        body, comm, f"inproj_fwd_{li}", (T // tm, nj),
        in_specs=[pl.BlockSpec((tm, D), lambda i, j: (i, 0)),
                  pl.BlockSpec((1, D), lambda i, j: (0, 0)),
                  pl.BlockSpec((D, tn), lambda i, j: (0, j)),
                  pl.BlockSpec((D, LANES), lambda i, j: (0, 0)),
                  pl.BlockSpec((tm, LANES), lambda i, j: (i % npos, 0)),
                  pl.BlockSpec((tm, LANES), lambda i, j: (i % npos, 0))],
        out_specs=[pl.BlockSpec((tm, tn), lambda i, j: (i, j)),
                   pl.BlockSpec((tm, LANES), lambda i, j: (i, 0)),
                   pl.BlockSpec((D, tm), lambda i, j: (0, i))],
        out_shape=[jax.ShapeDtypeStruct((T, NMAIN), BF16), jax.ShapeDtypeStruct((T, LANES), F32),
                   jax.ShapeDtypeStruct((D, T), BF16)],
        scratch=[pltpu.VMEM((tm, D), BF16)], dims=("parallel", "arbitrary"),
        operands=(x2, nw, wmain, wsmall, cos128, sin128))


def _inproj_bwd_dx(segs, wmain, init, final, name, comm=None):
    T = segs[0][0].shape[0]
    tm, tk = min(1024, T), 512
    ni = T // tm
    k0s, nks, c0s = [], [], []
    for arr, col0 in segs:
        k0s.append(sum(nks))
        nks.append(arr.shape[1] // tk)
        c0s.append(col0 // tk)
    nk = sum(nks)
    ns = len(segs)

    def in_range(k, s):
        return jnp.logical_and(k >= k0s[s], k < k0s[s] + nks[s])

    def wcol(i, k):
        g = 0
        for s in range(ns):
            g = g + jnp.where(in_range(k, s), c0s[s] + k - k0s[s], 0)
        return (0, g)

    n_init = 2 if init[0] == "narrow" else 1

    def body(*refs):
        seg_refs, w_ref = refs[:ns], refs[ns]
        init_refs = refs[ns + 1:ns + 1 + n_init]
        rest = refs[ns + 1 + n_init:]
        i, k = pl.program_id(0), pl.program_id(1)
        acc = rest[-1]

        @pl.when(k == 0)
        def _():
            if init[0] == "narrow":
                acc[...] = _dot_nt(init_refs[0][...], init_refs[1][...])
            else:
                acc[...] = init_refs[0][...]

        for s in range(ns):
            @pl.when(in_range(k, s))
            def _(s=s):
                acc[...] += _dot_nt(seg_refs[s][...], w_ref[...])

        if final is None:
            @pl.when(k == nk - 1)
            def _():
                rest[0][...] = acc[...]
        else:
            x_ref, nw_ref, dxo_ref, dx_ref, dx16_ref, dnw_ref = rest[:6]

            @pl.when(jnp.logical_and(i == 0, k == 0))
            def _():
                dnw_ref[...] = jnp.zeros_like(dnw_ref)

            @pl.when(k == nk - 1)
            def _():
                x = x_ref[...]
                r = lax.rsqrt(jnp.mean(x * x, axis=-1, keepdims=True) + EPS)
                dh = acc[...]
                g = dh * nw_ref[...]
                dx = dxo_ref[...] + r * g - x * (r * r * r) * jnp.mean(g * x, axis=-1, keepdims=True)
                dx_ref[...] = dx
                dx16_ref[...] = dx.astype(BF16)
                dnw_ref[0:1, :] += jnp.sum(dh * x * r, axis=0, keepdims=True)

    row = pl.BlockSpec((tm, D), lambda i, k: (i, 0))
    in_specs = [pl.BlockSpec((tm, tk), lambda i, k, s=s: (i, jnp.clip(k - k0s[s], 0, nks[s] - 1))) for s in range(ns)]
    in_specs.append(pl.BlockSpec((D, tk), wcol))
    operands = [a for a, _ in segs] + [wmain]
    if init[0] == "narrow":
        in_specs += [pl.BlockSpec((tm, LANES), lambda i, k: (i, 0)), pl.BlockSpec((D, LANES), lambda i, k: (0, 0))]
    else:
        in_specs.append(row)
    operands += list(init[1:])
    if final is None:
        out_specs, out_shape = [row], [jax.ShapeDtypeStruct((T, D), F32)]
    else:
        in_specs += [row, pl.BlockSpec((1, D), lambda i, k: (0, 0)), row]
        operands += list(final)
        out_specs = [row, row, pl.BlockSpec((8, D), lambda i, k: (0, 0))]
        out_shape = [jax.ShapeDtypeStruct((T, D), F32), jax.ShapeDtypeStruct((T, D), BF16),
                     jax.ShapeDtypeStruct((8, D), F32)]
    return _hosted_call(body, comm, name, (ni, nk), in_specs=in_specs, out_specs=out_specs, out_shape=out_shape,
                        scratch=[pltpu.VMEM((tm, D), F32)], dims=("arbitrary", "arbitrary"), operands=tuple(operands))


def _merge_fwd(ya, yb, yc, proj, gbias, wp, wout, x2, li):
    T = x2.shape[0]
    tm = min(512, T)
    gcol = OFF_G // D

    def body(ya_ref, yb_ref, yc_ref, g0_ref, g1_ref, g2_ref, gb_ref, wp_ref, wo_ref, x_ref, xn_ref, br_ref, yt_ref):
        merged = jnp.zeros((tm, D), F32)
        for i, (y_ref, g_ref) in enumerate(((ya_ref, g0_ref), (yb_ref, g1_ref), (yc_ref, g2_ref))):
            y = y_ref[...]
            yt_ref[i] = y.T
            br = _dot(y, wp_ref[i])
            br_ref[i] = br.astype(BF16)
            gate = _sigmoid(g_ref[...].astype(F32) + gb_ref[i:i + 1, :])
            merged = merged + gate * br
        xn_ref[...] = x_ref[...] + _dot(merged.astype(BF16), wo_ref[...])

    row = lambda c: pl.BlockSpec((tm, D), lambda i, c=c: (i, c))
    return pl.pallas_call(
        body, name=f"merge_fwd_{li}", grid=(T // tm,),
        in_specs=[row(0), row(0), row(0), row(gcol), row(gcol + 1), row(gcol + 2),
                  pl.BlockSpec((3, D), lambda i: (0, 0)),
                  pl.BlockSpec((3, D, D), lambda i: (0, 0, 0)),
                  pl.BlockSpec((D, D), lambda i: (0, 0)),
                  row(0)],
        out_specs=[row(0), pl.BlockSpec((3, tm, D), lambda i: (0, i, 0)), pl.BlockSpec((3, D, tm), lambda i: (0, 0, i))],
        out_shape=[jax.ShapeDtypeStruct((T, D), F32), jax.ShapeDtypeStruct((3, T, D), BF16),
                   jax.ShapeDtypeStruct((3, D, T), BF16)],
        compiler_params=_cparams(("parallel",), VMEM_LIMIT),
    )(ya, yb, yc, proj, proj, proj, gbias, wp, wout, x2)


def _merge_bwd(dxo16, wout, wp, br, proj, gbias, ob, oc, li):
    T = dxo16.shape[0]
    tm = min(256, T)
    gcol = OFF_G // D

    def body(dx_ref, wo_ref, wp_ref, br_ref, g0_ref, g1_ref, g2_ref, gb_ref, ob_ref, oc_ref, zb_ref, zc_ref,
             dbr_ref, dg_ref, mt_ref, dgb_ref, dya_ref, dob_ref, dzb_ref, doc_ref, dzc_ref):
        @pl.when(pl.program_id(0) == 0)
        def _():
            dgb_ref[...] = jnp.zeros_like(dgb_ref)

        dm = _dot_nt(dx_ref[...], wo_ref[...])
        merged = jnp.zeros((tm, D), F32)
        dys = []
        for i, g_ref in enumerate((g0_ref, g1_ref, g2_ref)):
            b = br_ref[i].astype(F32)
            gate = _sigmoid(g_ref[...].astype(F32) + gb_ref[i:i + 1, :])
            merged = merged + gate * b
            dbr = (dm * gate).astype(BF16)
            dbr_ref[i] = dbr
            dgate = dm * b * gate * (1.0 - gate)
            dg_ref[:, D * i:D * (i + 1)] = dgate.astype(BF16)
            dgb_ref[i:i + 1, :] += jnp.sum(dgate, axis=0, keepdims=True)
            dys.append(_dot_nt(dbr, wp_ref[i]))
        mt_ref[...] = merged.astype(BF16).T
        dya_ref[...] = dys[0].astype(BF16)
        for dy, o_ref, z_ref, do_ref, dz_ref in ((dys[1], ob_ref, zb_ref, dob_ref, dzb_ref),
                                                 (dys[2], oc_ref, zc_ref, doc_ref, dzc_ref)):
            z = z_ref[...].astype(F32)
            sg = _sigmoid(z)
            do_ref[...] = (dy * z * sg).astype(BF16)
            dz_ref[...] = (dy * o_ref[...].astype(F32) * sg * (1.0 + z * (1.0 - sg))).astype(BF16)

    row = lambda c: pl.BlockSpec((tm, D), lambda i, c=c: (i, c))
    sds = jax.ShapeDtypeStruct((T, D), BF16)
    return pl.pallas_call(
        body, name=f"merge_bwd_{li}", grid=(T // tm,),
        in_specs=[row(0), pl.BlockSpec((D, D), lambda i: (0, 0)), pl.BlockSpec((3, D, D), lambda i: (0, 0, 0)),
                  pl.BlockSpec((3, tm, D), lambda i: (0, i, 0)),
                  row(gcol), row(gcol + 1), row(gcol + 2),
                  pl.BlockSpec((3, D), lambda i: (0, 0)),
                  row(0), row(0), row(OFF_BZ // D), row(OFF_CZ // D)],
        out_specs=[pl.BlockSpec((3, tm, D), lambda i: (0, i, 0)),
                   pl.BlockSpec((tm, 3 * D), lambda i: (i, 0)),
                   pl.BlockSpec((D, tm), lambda i: (0, i)),
                   pl.BlockSpec((8, D), lambda i: (0, 0)),
                   row(0), row(0), row(0), row(0), row(0)],
        out_shape=[jax.ShapeDtypeStruct((3, T, D), BF16), jax.ShapeDtypeStruct((T, 3 * D), BF16),
                   jax.ShapeDtypeStruct((D, T), BF16), jax.ShapeDtypeStruct((8, D), F32), sds, sds, sds, sds, sds],
        compiler_params=_cparams(("arbitrary",), VMEM_LIMIT),
    )(dxo16, wout, wp, br, proj, proj, proj, gbias, ob, oc, proj, proj)


def _final_loss(x2, tgt, fw):
    T = x2.shape[0]
    tm = min(512, T)
    ni = T // tm

    def body(x_ref, t_ref, w_ref, dx_ref, dx16_ref, st_ref):
        i = pl.program_id(0)

        @pl.when(i == 0)
        def _():
            st_ref[...] = jnp.zeros_like(st_ref)

        x = x_ref[...]
        r = lax.rsqrt(jnp.mean(x * x, axis=-1, keepdims=True) + EPS)
        xh = x * r
        err = xh * w_ref[...] - t_ref[...]
        dy = err * (1.0 / D)
        g = dy * w_ref[...]
        dx = r * g - x * (r * r * r) * jnp.mean(g * x, axis=-1, keepdims=True)
        dx_ref[...] = dx
        dx16_ref[...] = dx.astype(BF16)
        st_ref[0:1, :] += jnp.sum(dy * xh, axis=0, keepdims=True)
        st_ref[1:2, :] += jnp.sum(err * err, axis=0, keepdims=True)

        @pl.when(i == ni - 1)
        def _():
            tot = jnp.sum(st_ref[1:2, :], axis=1, keepdims=True) * (0.5 / D)
            st_ref[2:3, :] = jnp.broadcast_to(tot, (1, D))

    row = pl.BlockSpec((tm, D), lambda i: (i, 0))
    return pl.pallas_call(
        body, name="final_loss", grid=(ni,),
        in_specs=[row, row, pl.BlockSpec((1, D), lambda i: (0, 0))],
        out_specs=[row, row, pl.BlockSpec((8, D), lambda i: (0, 0))],
        out_shape=[jax.ShapeDtypeStruct((T, D), F32), jax.ShapeDtypeStruct((T, D), BF16),
                   jax.ShapeDtypeStruct((8, D), F32)],
        compiler_params=_cparams(("arbitrary",), VMEM_LIMIT),
    )(x2, tgt, fw)


def _fox_cum(ps, fb_row, S, li):
    T = ps.shape[0]
    nb = S // LCH

    def body(ps_ref, fb_ref, cum_ref, carry):
        @pl.when(pl.program_id(1) == 0)
        def _():
            carry[...] = jnp.zeros_like(carry)

        logf = -_softplus(-(ps_ref[...] + fb_ref[...]))
        r = lax.broadcasted_iota(jnp.int32, (LCH, LCH), 0)
        c = lax.broadcasted_iota(jnp.int32, (LCH, LCH), 1)
        tri = (r >= c).astype(F32)
        cum = _dot_hi(tri, logf) + carry[0:1, :]
        cum_ref[...] = cum
        carry[0:1, :] = cum[LCH - 1:LCH, :]

    return pl.pallas_call(
        body, name=f"fox_cum_{li}", grid=(T // S, nb),
        in_specs=[pl.BlockSpec((LCH, LANES), lambda b, i: (b * nb + i, 0)),
                  pl.BlockSpec((1, LANES), lambda b, i: (0, 0))],
        out_specs=pl.BlockSpec((LCH, LANES), lambda b, i: (b * nb + i, 0)),
        out_shape=jax.ShapeDtypeStruct((T, LANES), F32),
        scratch_shapes=[pltpu.VMEM((8, LANES), F32)],
        compiler_params=_cparams(("arbitrary", "arbitrary")),
    )(ps, fb_row)


def _fox_cum_bwd(dcum, ps, fb_row, S, li):
    T = ps.shape[0]
    nb = S // LCH

    def body(dc_ref, ps_ref, fb_ref, df_ref, dfb_ref, carry):
        b, i = pl.program_id(0), pl.program_id(1)

        @pl.when(i == 0)
        def _():
            carry[...] = jnp.zeros_like(carry)

        @pl.when(jnp.logical_and(b == 0, i == 0))
        def _():
            dfb_ref[...] = jnp.zeros_like(dfb_ref)

        dc = dc_ref[...]
        r = lax.broadcasted_iota(jnp.int32, (LCH, LCH), 0)
        c = lax.broadcasted_iota(jnp.int32, (LCH, LCH), 1)
        tri = (c >= r).astype(F32)
        dlogf = _dot_hi(tri, dc) + carry[0:1, :]
        carry[0:1, :] += jnp.sum(dc, axis=0, keepdims=True)
        df = dlogf * _sigmoid(-(ps_ref[...] + fb_ref[...]))
        lane = _lane_iota()
        df = jnp.where(jnp.logical_and(lane >= NH, lane < 2 * NH), df, 0.0)
        df_ref[...] = df
        dfb_ref[0:1, :] += jnp.sum(df, axis=0, keepdims=True)

    blk = pl.BlockSpec((LCH, LANES), lambda b, i: (b * nb + nb - 1 - i, 0))
    return pl.pallas_call(
        body, name=f"fox_cum_bwd_{li}", grid=(T // S, nb),
        in_specs=[blk, blk, pl.BlockSpec((1, LANES), lambda b, i: (0, 0))],
        out_specs=[blk, pl.BlockSpec((8, LANES), lambda b, i: (0, 0))],
        out_shape=[jax.ShapeDtypeStruct((T, LANES), F32), jax.ShapeDtypeStruct((8, LANES), F32)],
        scratch_shapes=[pltpu.VMEM((8, LANES), F32)],
        compiler_params=_cparams(("arbitrary", "arbitrary")),
    )(dcum, ps, fb_row)


def _fox_blocks(S):
    bq = min(512, S)
    return bq, S // bq


def _split3(c):
    hi = c.astype(BF16).astype(F32)
    r = c - hi
    mid = r.astype(BF16).astype(F32)
    return hi, mid, (r - mid).astype(BF16).astype(F32)


def _augment(x, parts, key_side, hh):
    lane = _lane_iota()
    b0 = HD if hh == 0 else 0
    p0, o0 = (b0 + 3, b0) if key_side else (b0, b0 + 3)
    out = jnp.where(jnp.logical_and(lane >= o0, lane < o0 + 3), 1.0, x)
    for t in range(3):
        out = jnp.where(lane == p0 + t, parts[t], out)
    return out.astype(BF16)


def _fox_fwd(proj, cum_col, S, li):
    T = proj.shape[0]
    B = T // S
    bq, nq = _fox_blocks(S)
    qc, kc, vc, zc = OFF_CQ // LANES, OFF_CK // LANES, OFF_CV // LANES, OFF_CZ // LANES

    def body(q_ref, k_ref, v_ref, z_ref, cc_ref, y_ref, o_ref, lse_ref, kaug):
        i = pl.program_id(2)
        m0 = _lane_iota() < HD

        @pl.when(i == 0)
        def _():
            kf = k_ref[...].astype(F32)
            for hh in range(2):
                kaug[hh] = _augment(kf, _split3(-cc_ref[0, hh]), True, hh)

        q2 = q_ref[...].astype(F32) * SCALE
        rows_q = pl.ds(pl.multiple_of(i * bq, bq), bq)
        row = lax.broadcasted_iota(jnp.int32, (bq, bq), 0)
        col = lax.broadcasted_iota(jnp.int32, (bq, bq), 1)
        outs = []
        for hh in range(2):
            sel = m0 if hh == 0 else jnp.logical_not(m0)
            qa = _augment(jnp.where(sel, q2, 0.0), _split3(cc_ref[0, hh, rows_q, :]), False, hh)

            def step(j, carry, masked, hh=hh, qa=qa):
                m, l, acc = carry
                start = pl.multiple_of(j * bq, bq)
                v2 = v_ref[pl.ds(start, bq), :]
                s = _dot_nt(qa, kaug[hh, pl.ds(start, bq), :])
                if masked:
                    s = jnp.where(row >= col, s, NEG)
                mn = jnp.maximum(m, jnp.max(s, axis=1, keepdims=True))
                alpha = jnp.exp(m - mn)
                p = jnp.exp(s - mn)
                l = alpha * l + jnp.sum(p, axis=1, keepdims=True)
                acc = alpha * acc + _dot(p.astype(BF16), v2)
                return mn, l, acc

            init = (jnp.full((bq, 1), NEG, F32), jnp.zeros((bq, 1), F32), jnp.zeros((bq, LANES), F32))
            carry = lax.fori_loop(0, i, functools.partial(step, masked=False), init)
            m, l, acc = step(i, carry, True)
            outs.append(acc / l)
            lse_ref[0, hh] = m + jnp.log(l)
        o2 = jnp.where(m0, outs[0], outs[1])
        z = z_ref[...].astype(F32)
        o_ref[...] = o2.astype(BF16)
        y_ref[...] = (o2 * z * _sigmoid(z)).astype(BF16)

    qblk = lambda c: pl.BlockSpec((bq, LANES), lambda b, p, i, c=c: (b * nq + i, c + p))
    sblk = lambda c: pl.BlockSpec((S, LANES), lambda b, p, i, c=c: (b, c + p))
    return pl.pallas_call(
        body, name=f"fox_fwd_{li}", grid=(B, NH // 2, nq),
        in_specs=[qblk(qc), sblk(kc), sblk(vc), qblk(zc),
                  pl.BlockSpec((1, 2, S, 1), lambda b, p, i: (b, p, 0, 0))],
        out_specs=[qblk(0), qblk(0), pl.BlockSpec((1, 2, bq, 1), lambda b, p, i: (b, p, i, 0))],
        out_shape=[jax.ShapeDtypeStruct((T, D), BF16), jax.ShapeDtypeStruct((T, D), BF16),
                   jax.ShapeDtypeStruct((B, NH, S, 1), F32)],
        scratch_shapes=[pltpu.VMEM((2, S, LANES), BF16)],
        compiler_params=_cparams(("parallel", "parallel", "arbitrary"), VMEM_LIMIT),
    )(proj, proj, proj, proj, cum_col)


def _fox_bwd(proj, do, o, cum_col, lse, S, li):
    T = proj.shape[0]
    B = T // S
    bq, nq = _fox_blocks(S)
    qc, kc, vc = OFF_CQ // LANES, OFF_CK // LANES, OFF_CV // LANES

    def body(q_ref, k_ref, v_ref, do_ref, o_ref, cc_ref, lse_ref, dq_ref, dk_ref, dv_ref, dc_ref, dr_ref,
             dq_scr, dr_scr, qaug):
        j = pl.program_id(2)
        m0 = _lane_iota() < HD

        @pl.when(j == 0)
        def _():
            dq_scr[...] = jnp.zeros_like(dq_scr)
            dr_scr[...] = jnp.zeros_like(dr_scr)
            qf = q_ref[...].astype(F32) * SCALE
            for hh in range(2):
                sel = m0 if hh == 0 else jnp.logical_not(m0)
                qaug[hh] = _augment(jnp.where(sel, qf, 0.0), _split3(cc_ref[0, hh] - lse_ref[0, hh]), False, hh)

        k2 = k_ref[...]
        v2 = v_ref[...]
        zk = jnp.zeros_like(k2)
        kh = (jnp.where(m0, k2, zk), jnp.where(m0, zk, k2))
        kf = k2.astype(F32)
        rows_k = pl.ds(pl.multiple_of(j * bq, bq), bq)
        ka = [_augment(kf, _split3(-cc_ref[0, hh, rows_k, :]), True, hh) for hh in range(2)]
        row = lax.broadcasted_iota(jnp.int32, (bq, bq), 0)
        col = lax.broadcasted_iota(jnp.int32, (bq, bq), 1)

        def step(i, carry, masked):
            dk, dv, dc0, dc1 = carry
            dcs = [dc0, dc1]
            start = pl.multiple_of(i * bq, bq)
            q2 = q_ref[pl.ds(start, bq), :]
            do2 = do_ref[pl.ds(start, bq), :]
            prod = do2.astype(F32) * o_ref[pl.ds(start, bq), :].astype(F32)
            zq = jnp.zeros_like(q2)
            dq = jnp.zeros((bq, LANES), F32)
            for hh in range(2):
                sel = m0 if hh == 0 else jnp.logical_not(m0)
                qh = jnp.where(sel, q2, zq)
                doh = jnp.where(sel, do2, zq)
                delta = _head_sum(prod, hh)
                s = _dot_nt(qaug[hh, pl.ds(start, bq), :], ka[hh])
                if masked:
                    s = jnp.where(row >= col, s, NEG)
                p = jnp.exp(s)
                dp = _dot_nt(doh, v2)
                ds = p * (dp - delta)
                dcs[hh] = dcs[hh] - jnp.sum(ds, axis=0, keepdims=True)
                dr_scr[hh, pl.ds(start, bq), :] += jnp.sum(ds, axis=1, keepdims=True)
                dsb = ds.astype(BF16)
                dv = dv + _dot_tn(p.astype(BF16), doh)
                dk = dk + _dot_tn(dsb, qh)
                dq = dq + _dot(dsb, kh[hh])
            dq_scr[pl.ds(start, bq), :] += dq
            return dk, dv, dcs[0], dcs[1]

        zero = jnp.zeros((bq, LANES), F32)
        zrow = jnp.zeros((1, bq), F32)
        carry = step(j, (zero, zero, zrow, zrow), True)
        dk, dv, dc0, dc1 = lax.fori_loop(j + 1, nq, functools.partial(step, masked=False), carry)
        dk_ref[...] = (dk * SCALE).astype(BF16)
        dv_ref[...] = dv.astype(BF16)
        dc_ref[0, 0, 0] = dc0
        dc_ref[0, 1, 0] = dc1

        @pl.when(j == nq - 1)
        def _():
            dq_ref[...] = (dq_scr[...] * SCALE).astype(BF16)
            dr_ref[0] = dr_scr[...]

    sblk = lambda c: pl.BlockSpec((S, LANES), lambda b, p, j, c=c: (b, c + p))
    kblk = lambda c: pl.BlockSpec((bq, LANES), lambda b, p, j, c=c: (b * nq + j, c + p))
    col_spec = pl.BlockSpec((1, 2, S, 1), lambda b, p, j: (b, p, 0, 0))
    return pl.pallas_call(
        body, name=f"fox_bwd_{li}", grid=(B, NH // 2, nq),
        in_specs=[sblk(qc), kblk(kc), kblk(vc), sblk(0), sblk(0), col_spec, col_spec],
        out_specs=[sblk(0), kblk(0), kblk(0), pl.BlockSpec((1, 2, 1, 1, bq), lambda b, p, j: (b, p, j, 0, 0)),
                   col_spec],
        out_shape=[jax.ShapeDtypeStruct((T, D), BF16), jax.ShapeDtypeStruct((T, D), BF16),
                   jax.ShapeDtypeStruct((T, D), BF16), jax.ShapeDtypeStruct((B, NH, nq, 1, bq), F32),
                   jax.ShapeDtypeStruct((B, NH, S, 1), F32)],
        scratch_shapes=[pltpu.VMEM((S, LANES), F32), pltpu.VMEM((2, S, 1), F32), pltpu.VMEM((2, S, LANES), BF16)],
        compiler_params=_cparams(("parallel", "parallel", "arbitrary"), VMEM_LIMIT),
    )(proj, proj, proj, do, o, cum_col, lse)


def _swa_blocks(S):
    bq = min(512, S)
    return bq, S // bq, bq // LCH


def _dup_head(xw, kvl):
    m0 = _lane_iota() < HD
    a = jnp.where(m0 if kvl == 0 else jnp.logical_not(m0), xw, 0.0)
    return (a + pltpu.roll(a, HD, 1)).astype(BF16)


def _band(same_block):
    r = lax.broadcasted_iota(jnp.int32, (LCH, LCH), 0)
    c = lax.broadcasted_iota(jnp.int32, (LCH, LCH), 1)
    return (c <= r) if same_block else (c > r)


def _stack_heads(ref, rows, kvl):
    m0 = _lane_iota() < HD
    parts = []
    for ch in (2 * kvl, 2 * kvl + 1):
        x = ref[rows, LANES * ch:LANES * (ch + 1)]
        parts += [jnp.where(m0, x, jnp.zeros_like(x)), jnp.where(m0, jnp.zeros_like(x), x)]
    return jnp.concatenate(parts, axis=0)


def _stack_delta(do_ref, o_ref, rows, kvl, scale=None):
    parts = []
    for ch in (2 * kvl, 2 * kvl + 1):
        lanes = slice(LANES * ch, LANES * (ch + 1))
        prod = do_ref[rows, lanes].astype(F32) * o_ref[rows, lanes].astype(F32)
        parts += [_head_sum(prod, 0), _head_sum(prod, 1)]
    out = jnp.concatenate(parts, axis=0)
    return out if scale is None else out * scale


def _stack_cols(ref, rows, kvl):
    return jnp.concatenate([ref[0, 4 * kvl + t, rows, :] for t in range(4)], axis=0)


def _swa_fwd(proj, sinks, S, li):
    T = proj.shape[0]
    B = T // S
    bq, nq, nsub = _swa_blocks(S)
    nrow = S // LCH
    qc, zc, kc, vc = OFF_BQ // 512, OFF_BZ // 512, OFF_BK // LANES, OFF_BV // LANES

    def body(sk_ref, q_ref, z_ref, kp_ref, kc_ref, vp_ref, vc_ref, y_ref, o_ref, lse_ref):
        c, i = pl.program_id(0), pl.program_id(2)
        m0 = _lane_iota() < HD
        kw = jnp.concatenate([kp_ref[...].astype(F32), kc_ref[...].astype(F32)], axis=0)
        vw = jnp.concatenate([vp_ref[...].astype(F32), vc_ref[...].astype(F32)], axis=0)
        kd = (_dup_head(kw, 0), _dup_head(kw, 1))
        vd = (_dup_head(vw, 0), _dup_head(vw, 1))
        valid = jnp.concatenate([_band(False), _band(True)], axis=1)
        col = lax.broadcasted_iota(jnp.int32, (LCH, 2 * LCH), 1)
        valid_first = jnp.logical_and(valid, jnp.logical_or(col >= LCH, i > 0))
        valid4 = jnp.concatenate([valid] * 4, axis=0)
        valid4_first = jnp.concatenate([valid_first] * 4, axis=0)
        for r in range(nsub):
            rows = slice(LCH * r, LCH * (r + 1))
            msk = valid4_first if r == 0 else valid4
            for kvl in range(2):
                kwin = kd[kvl][LCH * r:LCH * (r + 2)]
                vwin = vd[kvl][LCH * r:LCH * (r + 2)]
                qs = _stack_heads(q_ref, rows, kvl)
                sink = jnp.concatenate([jnp.full((LCH, 1), sk_ref[8 * c + 4 * kvl + t], F32) for t in range(4)], axis=0)
                s = jnp.where(msk, _dot_nt(qs, kwin) * SCALE, NEG)
                m = jnp.maximum(jnp.max(s, axis=1, keepdims=True), sink)
                p = jnp.exp(s - m)
                l = jnp.sum(p, axis=1, keepdims=True) + jnp.exp(sink - m)
                os_ = _dot(p.astype(BF16), vwin) / l
                lse = m + jnp.log(l)
                for t in range(4):
                    lse_ref[0, 4 * kvl + t, rows, :] = lse[LCH * t:LCH * (t + 1)]
                for u in range(2):
                    lanes = slice(LANES * (2 * kvl + u), LANES * (2 * kvl + u + 1))
                    o2 = jnp.where(m0, os_[LCH * 2 * u:LCH * (2 * u + 1)], os_[LCH * (2 * u + 1):LCH * (2 * u + 2)])
                    z = z_ref[rows, lanes].astype(F32)
                    o_ref[rows, lanes] = o2.astype(BF16)
                    y_ref[rows, lanes] = (o2 * z * _sigmoid(z)).astype(BF16)

    wide = lambda cc: pl.BlockSpec((bq, 512), lambda c, b, i, cc=cc: (b * nq + i, cc + c))
    cur = lambda cc: pl.BlockSpec((bq, LANES), lambda c, b, i, cc=cc: (b * nq + i, cc + c))
    prev = lambda cc: pl.BlockSpec((LCH, LANES), lambda c, b, i, cc=cc: (b * nrow + jnp.maximum(i * nsub - 1, 0), cc + c))
    return pl.pallas_call(
        body, name=f"swa_fwd_{li}", grid=(2, B, nq),
        in_specs=[pl.BlockSpec(memory_space=pltpu.SMEM), wide(qc), wide(zc), prev(kc), cur(kc), prev(vc), cur(vc)],
        out_specs=[wide(0), wide(0), pl.BlockSpec((1, 8, bq, 1), lambda c, b, i: (b, c, i, 0))],
        out_shape=[jax.ShapeDtypeStruct((T, D), BF16), jax.ShapeDtypeStruct((T, D), BF16),
                   jax.ShapeDtypeStruct((B, NH, S, 1), F32)],
        compiler_params=_cparams(("parallel", "parallel", "parallel"), VMEM_LIMIT),
    )(sinks, proj, proj, proj, proj, proj, proj)


def _swa_bwd_dq(proj, do, o, lse, sinks, cos128, sin128, S, li):
    T = proj.shape[0]
    B = T // S
    bq, nq, nsub = _swa_blocks(S)
    nrow = S // LCH
    qc, kc, vc = OFF_BQ // 512, OFF_BK // LANES, OFF_BV // LANES

    def body(sk_ref, q_ref, do_ref, o_ref, lse_ref, kp_ref, kc_ref, vp_ref, vc_ref, cos_ref, sin_ref, dq_ref, dsk_ref):
        c, b, i = pl.program_id(0), pl.program_id(1), pl.program_id(2)

        @pl.when(jnp.logical_and(b == 0, i == 0))
        def _():
            dsk_ref[...] = jnp.zeros_like(dsk_ref)

        m0 = _lane_iota() < HD
        kw = jnp.concatenate([kp_ref[...].astype(F32), kc_ref[...].astype(F32)], axis=0)
        vw = jnp.concatenate([vp_ref[...].astype(F32), vc_ref[...].astype(F32)], axis=0)
        kd = (_dup_head(kw, 0), _dup_head(kw, 1))
        vd = (_dup_head(vw, 0), _dup_head(vw, 1))
        valid = jnp.concatenate([_band(False), _band(True)], axis=1)
        col = lax.broadcasted_iota(jnp.int32, (LCH, 2 * LCH), 1)
        valid_first = jnp.logical_and(valid, jnp.logical_or(col >= LCH, i > 0))
        dsk = [jnp.zeros((1, 1), F32) for _ in range(8)]
        valid4 = jnp.concatenate([valid] * 4, axis=0)
        valid4_first = jnp.concatenate([valid_first] * 4, axis=0)
        for r in range(nsub):
            rows = slice(LCH * r, LCH * (r + 1))
            msk = valid4_first if r == 0 else valid4
            for kvl in range(2):
                kwin = kd[kvl][LCH * r:LCH * (r + 2)]
                vwin = vd[kvl][LCH * r:LCH * (r + 2)]
                qs = _stack_heads(q_ref, rows, kvl)
                dos = _stack_heads(do_ref, rows, kvl)
                delta = _stack_delta(do_ref, o_ref, rows, kvl)
                lse = _stack_cols(lse_ref, rows, kvl)
                sink = jnp.concatenate([jnp.full((LCH, 1), sk_ref[8 * c + 4 * kvl + t], F32) for t in range(4)], axis=0)
                s = jnp.where(msk, _dot_nt(qs, kwin) * SCALE, NEG)
                p = jnp.exp(s - lse)
                ds = p * (_dot_nt(dos, vwin) - delta)
                dqs = _dot(ds.astype(BF16), kwin) * SCALE
                dsink = jnp.exp(sink - lse) * delta
                for t in range(4):
                    hl = 4 * kvl + t
                    dsk[hl] = dsk[hl] - jnp.sum(dsink[LCH * t:LCH * (t + 1)], axis=0, keepdims=True)
                for u in range(2):
                    lanes = slice(LANES * (2 * kvl + u), LANES * (2 * kvl + u + 1))
                    dq2 = jnp.where(m0, dqs[LCH * 2 * u:LCH * (2 * u + 1)], dqs[LCH * (2 * u + 1):LCH * (2 * u + 2)])
                    dq2 = dq2 * cos_ref[rows, :] - _rot_half(dq2) * sin_ref[rows, :]
                    dq_ref[rows, lanes] = dq2.astype(BF16)
        for hl in range(8):
            dsk_ref[0, hl:hl + 1, :] += jnp.broadcast_to(dsk[hl], (1, LANES))

    wide = lambda cc: pl.BlockSpec((bq, 512), lambda c, b, i, cc=cc: (b * nq + i, cc + c))
    cur = lambda cc: pl.BlockSpec((bq, LANES), lambda c, b, i, cc=cc: (b * nq + i, cc + c))
    prev = lambda cc: pl.BlockSpec((LCH, LANES), lambda c, b, i, cc=cc: (b * nrow + jnp.maximum(i * nsub - 1, 0), cc + c))
    pos = pl.BlockSpec((bq, LANES), lambda c, b, i: (i, 0))
    return pl.pallas_call(
        body, name=f"swa_bwd_dq_{li}", grid=(2, B, nq),
        in_specs=[pl.BlockSpec(memory_space=pltpu.SMEM), wide(qc), wide(0), wide(0),
                  pl.BlockSpec((1, 8, bq, 1), lambda c, b, i: (b, c, i, 0)),
                  prev(kc), cur(kc), prev(vc), cur(vc), pos, pos],
        out_specs=[wide(0), pl.BlockSpec((1, 8, LANES), lambda c, b, i: (c, 0, 0))],
        out_shape=[jax.ShapeDtypeStruct((T, D), BF16), jax.ShapeDtypeStruct((2, 8, LANES), F32)],
        compiler_params=_cparams(("arbitrary", "arbitrary", "arbitrary"), VMEM_LIMIT),
    )(sinks, proj, do, o, lse, proj, proj, proj, proj, cos128, sin128)


def _swa_bwd_dkv(proj, do, o, lse, cos128, sin128, S, li):
    T = proj.shape[0]
    B = T // S
    bk, nk, nsub = _swa_blocks(S)
    nrow = S // LCH
    qc, kc, vc = OFF_BQ // 512, OFF_BK // LANES, OFF_BV // LANES

    def body(q_ref, qn_ref, do_ref, don_ref, o_ref, on_ref, lse_ref, lsen_ref, k_ref, v_ref, cos_ref, sin_ref,
             dk_ref, dv_ref):
        j = pl.program_id(2)
        m0 = _lane_iota() < HD
        has_next = (j < nk - 1).astype(F32)
        kf = k_ref[...].astype(F32)
        vf = v_ref[...].astype(F32)
        kd = (_dup_head(kf, 0), _dup_head(kf, 1))
        vd = (_dup_head(vf, 0), _dup_head(vf, 1))
        masks4 = (jnp.concatenate([_band(True)] * 4, axis=0), jnp.concatenate([_band(False)] * 4, axis=0))
        for kr in range(nsub):
            krows = slice(LCH * kr, LCH * (kr + 1))
            dk = jnp.zeros((LCH, LANES), F32)
            dv = jnp.zeros((LCH, LANES), F32)
            for dq_blk in range(2):
                rq = kr + dq_blk
                nxt = rq == nsub
                qrows = slice(0, LCH) if nxt else slice(LCH * rq, LCH * (rq + 1))
                qr, dor, orr, lr = (qn_ref, don_ref, on_ref, lsen_ref) if nxt else (q_ref, do_ref, o_ref, lse_ref)
                for kvl in range(2):
                    qs = _stack_heads(qr, qrows, kvl)
                    dos = _stack_heads(dor, qrows, kvl)
                    delta = _stack_delta(dor, orr, qrows, kvl, has_next if nxt else None)
                    if nxt:
                        dos = (dos.astype(F32) * has_next).astype(BF16)
                    s = jnp.where(masks4[dq_blk], _dot_nt(qs, kd[kvl][krows]) * SCALE, NEG)
                    p = jnp.exp(s - _stack_cols(lr, qrows, kvl))
                    ds = p * (_dot_nt(dos, vd[kvl][krows]) - delta)
                    dvc = _dot_tn(p.astype(BF16), dos)
                    dkc = _dot_tn(ds.astype(BF16), qs) * SCALE
                    own = m0 if kvl == 0 else jnp.logical_not(m0)
                    dv = dv + jnp.where(own, dvc + pltpu.roll(dvc, HD, 1), 0.0)
                    dk = dk + jnp.where(own, dkc + pltpu.roll(dkc, HD, 1), 0.0)
            dk = dk * cos_ref[krows, :] - _rot_half(dk) * sin_ref[krows, :]
            dk_ref[krows, :] = dk.astype(BF16)
            dv_ref[krows, :] = dv.astype(BF16)

    wide = lambda cc: pl.BlockSpec((bk, 512), lambda c, b, j, cc=cc: (b * nk + j, cc + c))
    nxt = lambda cc: pl.BlockSpec((LCH, 512), lambda c, b, j, cc=cc: (b * nrow + jnp.minimum((j + 1) * nsub, nrow - 1), cc + c))
    cur = lambda cc: pl.BlockSpec((bk, LANES), lambda c, b, j, cc=cc: (b * nk + j, cc + c))
    pos = pl.BlockSpec((bk, LANES), lambda c, b, j: (j, 0))
    return pl.pallas_call(
        body, name=f"swa_bwd_dkv_{li}", grid=(2, B, nk),
        in_specs=[wide(qc), nxt(qc), wide(0), nxt(0), wide(0), nxt(0),
                  pl.BlockSpec((1, 8, bk, 1), lambda c, b, j: (b, c, j, 0)),
                  pl.BlockSpec((1, 8, LCH, 1), lambda c, b, j: (b, c, jnp.minimum((j + 1) * nsub, nrow - 1), 0)),
                  cur(kc), cur(vc), pos, pos],
        out_specs=[cur(0), cur(0)],
        out_shape=[jax.ShapeDtypeStruct((T, 2 * LANES), BF16), jax.ShapeDtypeStruct((T, 2 * LANES), BF16)],
        compiler_params=_cparams(("parallel", "parallel", "parallel"), VMEM_LIMIT),
    )(proj, proj, do, do, o, o, lse, lse, proj, proj, cos128, sin128)


HALO = 16


def _shift_matrices():
    r = lax.broadcasted_iota(jnp.int32, (3 * LCH, LCH + HALO), 0)
    c = lax.broadcasted_iota(jnp.int32, (3 * LCH, LCH + HALO), 1)
    t, d = r % LCH, r // LCH + 1
    return (c == HALO + t - d).astype(BF16), (c == t + d).astype(BF16)


def _ssm_chunk_pre(prev16, cur16, first, sdn_ref, cw_ref, cb_ref, ps, dtb, alog):
    ext16 = jnp.concatenate([jnp.where(first, jnp.zeros_like(prev16), prev16), cur16], axis=0)
    sh = _dot(sdn_ref[...], ext16)
    pre = cb_ref[...] + cw_ref[3:4, :] * cur16.astype(F32)
    for d in range(1, 4):
        pre = pre + cw_ref[3 - d:4 - d, :] * sh[LCH * (d - 1):LCH * d]
    sg = _sigmoid(pre)
    dt = _softplus(ps + dtb)
    a = -jnp.exp(alog)
    r = lax.broadcasted_iota(jnp.int32, (LCH, LCH), 0)
    c = lax.broadcasted_iota(jnp.int32, (LCH, LCH), 1)
    acum = _dot_hi((r >= c).astype(F32), dt * a)
    return pre, sg, dt, a, acum, sh


def _pairsel(v, p):
    return jnp.where(_lane_iota() < HD, v[:, 2 * p:2 * p + 1], v[:, 2 * p + 1:2 * p + 2])


def _decay(acum, acum_t, h):
    r = lax.broadcasted_iota(jnp.int32, (LCH, LCH), 0)
    c = lax.broadcasted_iota(jnp.int32, (LCH, LCH), 1)
    causal = r >= c
    seg = acum[:, h:h + 1] - acum_t[h:h + 1, :]
    return jnp.where(causal, jnp.exp(jnp.where(causal, seg, 0.0)), 0.0)


def _ssm_pair_fwd(p, x, dt, acum, acum_t, e_all, w_all, cd, cb_g, b_g, c_g, hprev, dsk_ref):
    m0 = _lane_iota() < HD
    lanes = slice(LANES * p, LANES * (p + 1))
    x2 = x[:, lanes]
    dt2 = _pairsel(dt, p)
    xdt2 = x2 * dt2
    xdtb = xdt2.astype(BF16)
    lms, ms, yds = [], [], []
    for hh in range(2):
        lm = _decay(acum, acum_t, 2 * p + hh)
        mm = cb_g * lm
        lms.append(lm)
        ms.append(mm)
        yds.append(_dot(mm.astype(BF16), xdtb))
    yd2 = jnp.where(m0, yds[0], yds[1])
    w2 = _pairsel(w_all, p)
    xw = (xdt2 * w2).astype(BF16)
    s2 = _dot_tn(xw, b_g)
    z2 = _dot_nt(c_g, hprev.astype(BF16))
    e2 = _pairsel(e_all, p)
    rowsel = lax.broadcasted_iota(jnp.int32, (LANES, 1), 0) < HD
    cdcol = jnp.where(rowsel, cd[:, 2 * p:2 * p + 1], cd[:, 2 * p + 1:2 * p + 2])
    y2 = yd2 + z2 * e2 + dsk_ref[:, lanes] * x2
    return dict(x2=x2, dt2=dt2, xdt2=xdt2, xdtb=xdtb, lms=lms, ms=ms, yd2=yd2, w2=w2, xw=xw, s2=s2, z2=z2, e2=e2,
                cdcol=cdcol, y2=y2)


def _ssm_specs(S, rev):
    nc = S // LCH
    ch = (lambda c: nc - 1 - c) if rev else (lambda c: c)
    prev = pl.BlockSpec((HALO, 2 * D), lambda b, c: (jnp.maximum(b * (S // HALO) + ch(c) * (LCH // HALO) - 1, 0), 0))
    cur = pl.BlockSpec((LCH, 2 * D), lambda b, c: (b * nc + ch(c), 0))
    zed = pl.BlockSpec((LCH, D), lambda b, c: (b * nc + ch(c), OFF_AZ // D))
    row = pl.BlockSpec((LCH, D), lambda b, c: (b * nc + ch(c), 0))
    psb = pl.BlockSpec((LCH, LANES), lambda b, c: (b * nc + ch(c), 0))
    hpb = pl.BlockSpec((1, 1, NH // 2, LANES, NST), lambda b, c: (b, ch(c), 0, 0, 0))
    const = lambda r, w: pl.BlockSpec((r, w), lambda b, c: (0, 0))
    return nc, prev, cur, zed, row, psb, hpb, const


def _ssm_fwd(proj, ps, cw, cb, dtb, alog, dsk, nw, S, li):
    T = proj.shape[0]
    B = T // S
    nc, prev, cur, zed, row, psb, hpb, const = _ssm_specs(S, False)

    def body(prev_ref, cur_ref, z_ref, ps_ref, sdn_ref, cw_ref, cb_ref, dtb_ref, alog_ref, dsk_ref, nw_ref,
             ya_ref, hp_ref, h_scr):
        c = pl.program_id(1)

        @pl.when(c == 0)
        def _():
            h_scr[...] = jnp.zeros_like(h_scr)

        pre, sg, dt, a, acum, _ = _ssm_chunk_pre(prev_ref[...], cur_ref[...], c == 0, sdn_ref, cw_ref, cb_ref,
                                                 ps_ref[...], dtb_ref[...], alog_ref[...])
        act = pre * sg
        acum_t = acum.T
        e_all = jnp.exp(acum)
        last = acum[LCH - 1:LCH, :]
        w_all = jnp.exp(last - acum)
        cd = jnp.exp(last)
        x = act[:, :D]
        for g in range(NGRP):
            b_g = act[:, D + NST * g:D + NST * (g + 1)].astype(BF16)
            c_g = act[:, D + NGRP * NST + NST * g:D + NGRP * NST + NST * (g + 1)].astype(BF16)
            cb_g = _dot_nt(c_g, b_g)
            ygs = []
            for p in (2 * g, 2 * g + 1):
                hprev = h_scr[p]
                hp_ref[0, 0, p] = hprev
                f = _ssm_pair_fwd(p, x, dt, acum, acum_t, e_all, w_all, cd, cb_g, b_g, c_g, hprev, dsk_ref)
                h_scr[p] = hprev * f["cdcol"] + f["s2"]
                z2 = z_ref[:, LANES * p:LANES * (p + 1)].astype(F32)
                ygs.append(f["y2"] * z2 * _sigmoid(z2))
            yg = jnp.concatenate(ygs, axis=1)
            r = lax.rsqrt(jnp.mean(yg * yg, axis=1, keepdims=True) + EPS)
            ya_ref[:, 2 * LANES * g:2 * LANES * (g + 1)] = (yg * r * nw_ref[:, 2 * LANES * g:2 * LANES * (g + 1)]).astype(BF16)

    return pl.pallas_call(
        body, name=f"ssm_fwd_{li}", grid=(B, nc),
        in_specs=[prev, cur, zed, psb, const(3 * LCH, LCH + HALO), const(4, 2 * D), const(1, 2 * D), const(1, LANES),
                  const(1, LANES), const(1, D), const(1, D)],
        out_specs=[row, hpb],
        out_shape=[jax.ShapeDtypeStruct((T, D), BF16), jax.ShapeDtypeStruct((B, nc, NH // 2, LANES, NST), F32)],
        scratch_shapes=[pltpu.VMEM((NH // 2, LANES, NST), F32)],
        compiler_params=_cparams(("arbitrary", "arbitrary"), VMEM_LIMIT),
    )(proj, proj, proj, ps, _shift_matrices()[0], cw, cb, dtb, alog, dsk, nw)


def _ssm_bwd(proj, ps, hp, dya, cw, cb, dtb, alog, dsk, nw, S, li, comm=None):
    T = proj.shape[0]
    B = T // S
    nc, prev, cur, zed, row, psb, hpb, const = _ssm_specs(S, True)

    def body(prev_ref, cur_ref, z_ref, ps_ref, hp_ref, dy_ref, sdn_ref, sup_ref, cw_ref, cb_ref, dtb_ref, alog_ref,
             dsk_ref, nw_ref, dxbc_ref, dz_ref, dps_ref, pgw_ref, pg1_ref, pgh_ref, dh_scr, dhead, dact):
        b, cc = pl.program_id(0), pl.program_id(1)
        c = nc - 1 - cc

        @pl.when(jnp.logical_and(b == 0, cc == 0))
        def _():
            pgw_ref[...] = jnp.zeros_like(pgw_ref)
            pg1_ref[...] = jnp.zeros_like(pg1_ref)
            pgh_ref[...] = jnp.zeros_like(pgh_ref)

        @pl.when(cc == 0)
        def _():
            dh_scr[...] = jnp.zeros_like(dh_scr)
            dhead[...] = jnp.zeros_like(dhead)

        psv = ps_ref[...]
        cur16 = cur_ref[...]
        pre, sg, dt, a, acum, sh = _ssm_chunk_pre(prev_ref[...], cur16, c == 0, sdn_ref, cw_ref, cb_ref, psv,
                                                  dtb_ref[...], alog_ref[...])
        act = pre * sg
        acum_t = acum.T
        e_all = jnp.exp(acum)
        last = acum[LCH - 1:LCH, :]
        w_all = jnp.exp(last - acum)
        cd = jnp.exp(last)
        x = act[:, :D]
        lane = _lane_iota()
        m0 = lane < HD
        rowsel = lax.broadcasted_iota(jnp.int32, (LANES, 1), 0) < HD
        is_last_row = lax.broadcasted_iota(jnp.int32, (LCH, 1), 0) == LCH - 1
        dacum_all = jnp.zeros((LCH, LANES), F32)
        ddt_all = jnp.zeros((LCH, LANES), F32)
        dd_row = jnp.zeros((1, LANES), F32)
        for g in range(NGRP):
            b_g = act[:, D + NST * g:D + NST * (g + 1)].astype(BF16)
            c_g = act[:, D + NGRP * NST + NST * g:D + NGRP * NST + NST * (g + 1)].astype(BF16)
            cb_g = _dot_nt(c_g, b_g)
            pairs = (2 * g, 2 * g + 1)
            fs, hps, zs, ygs = [], [], [], []
            for p in pairs:
                hprev = hp_ref[0, 0, p]
                f = _ssm_pair_fwd(p, x, dt, acum, acum_t, e_all, w_all, cd, cb_g, b_g, c_g, hprev, dsk_ref)
                z2 = z_ref[:, LANES * p:LANES * (p + 1)].astype(F32)
                fs.append(f)
                hps.append(hprev)
                zs.append(z2)
                ygs.append(f["y2"] * z2 * _sigmoid(z2))
            gl = slice(2 * LANES * g, 2 * LANES * (g + 1))
            yg = jnp.concatenate(ygs, axis=1)
            r = lax.rsqrt(jnp.mean(yg * yg, axis=1, keepdims=True) + EPS)
            dyn = dy_ref[:, gl].astype(F32)
            gg = dyn * nw_ref[:, gl]
            dyg = r * gg - yg * (r * r * r) * jnp.mean(gg * yg, axis=1, keepdims=True)
            pg1_ref[0:1, gl] += jnp.sum(dyn * yg * r, axis=0, keepdims=True)
            dg_g = jnp.zeros((LCH, LCH), F32)
            db_g = jnp.zeros((LCH, NST), F32)
            dc_g = jnp.zeros((LCH, NST), F32)
            for idx, p in enumerate(pairs):
                f, hprev, z2 = fs[idx], hps[idx], zs[idx]
                lanes = slice(LANES * p, LANES * (p + 1))
                dyg2 = dyg[:, LANES * idx:LANES * (idx + 1)]
                sgz = _sigmoid(z2)
                dy2 = dyg2 * z2 * sgz
                dz_ref[:, lanes] = (dyg2 * f["y2"] * sgz * (1.0 + z2 * (1.0 - sgz))).astype(BF16)
                x2, dt2, xdt2, xdtb, w2, e2, z2m = f["x2"], f["dt2"], f["xdt2"], f["xdtb"], f["w2"], f["e2"], f["z2"]
                dx2 = dsk_ref[:, lanes] * dy2
                dyx = dy2 * x2
                dxdt2 = jnp.zeros((LCH, LANES), F32)
                diag_cols = []
                for hh in range(2):
                    sel = m0 if hh == 0 else jnp.logical_not(m0)
                    dyb = jnp.where(sel, dy2, 0.0).astype(BF16)
                    dm = _dot_nt(dyb, xdtb)
                    dg_g = dg_g + dm * f["lms"][hh]
                    dxdt2 = dxdt2 + _dot_tn(f["ms"][hh].astype(BF16), dyb)
                    em = dm * f["ms"][hh]
                    diag_cols.append(jnp.sum(em, axis=1, keepdims=True) - jnp.sum(em.T, axis=1, keepdims=True))
                dz2m = dy2 * e2
                t_off = dz2m * z2m
                dc_g = dc_g + _dot(dz2m.astype(BF16), hprev.astype(BF16))
                dhprev = _dot_tn(dz2m.astype(BF16), c_g)
                dhn = dh_scr[p]
                dhnb = dhn.astype(BF16)
                dhprev = dhprev + dhn * f["cdcol"]
                t_h = dhn * hprev
                dxw2 = _dot_nt(b_g, dhnb)
                db_g = db_g + _dot(f["xw"], dhnb)
                dxdt2 = dxdt2 + dxw2 * w2
                t_w = dxw2 * xdt2
                dx2 = dx2 + dxdt2 * dt2
                t_dt = dxdt2 * x2
                for hh in range(2):
                    h = 2 * p + hh
                    onehot = (lane == h).astype(F32)
                    w_col = w_all[:, h:h + 1]
                    dw_col = _head_sum(t_w, hh) * w_col
                    rs = rowsel if hh == 0 else jnp.logical_not(rowsel)
                    dlast = (jnp.sum(jnp.where(rs, t_h, 0.0), keepdims=True) * cd[:, h:h + 1]
                             + jnp.sum(dw_col, keepdims=True))
                    dacum_col = diag_cols[hh] + _head_sum(t_off, hh) - dw_col + jnp.where(is_last_row, dlast, 0.0)
                    dacum_all = dacum_all + dacum_col * onehot
                    ddt_all = ddt_all + _head_sum(t_dt, hh) * onehot
                    sel = m0 if hh == 0 else jnp.logical_not(m0)
                    dd_row = dd_row + jnp.sum(jnp.where(sel, dyx, 0.0), keepdims=True) * onehot
                dh_scr[p] = dhprev
                dact[:, lanes] = dx2
            dgb = dg_g.astype(BF16)
            dc_g = dc_g + _dot(dgb, b_g)
            db_g = db_g + _dot_tn(dgb, c_g)
            dact[:, D + NST * g:D + NST * (g + 1)] = db_g
            dact[:, D + NGRP * NST + NST * g:D + NGRP * NST + NST * (g + 1)] = dc_g
        rr = lax.broadcasted_iota(jnp.int32, (LCH, LCH), 0)
        cc2 = lax.broadcasted_iota(jnp.int32, (LCH, LCH), 1)
        dadt = _dot_hi((cc2 >= rr).astype(F32), dacum_all)
        ddt_all = ddt_all + dadt * a
        heads = lane < NH
        da = jnp.sum(dadt * dt, axis=0, keepdims=True)
        dr = jnp.where(heads, ddt_all * _sigmoid(psv + dtb_ref[...]), 0.0)
        dps_ref[...] = dr
        pgh_ref[0:1, :] += jnp.sum(dr, axis=0, keepdims=True)
        pgh_ref[1:2, :] += jnp.where(heads, da * a, 0.0)
        pgh_ref[2:3, :] += dd_row
        dpre = dact[...] * sg * (1.0 + pre * (1.0 - sg))
        extd = jnp.concatenate([dpre, dhead[...]], axis=0)
        hi = extd.astype(BF16)
        lo = (extd - hi.astype(F32)).astype(BF16)
        up = _dot(sup_ref[...], hi) + _dot(sup_ref[...], lo)
        du = cw_ref[3:4, :] * dpre
        pgw_ref[3:4, :] += jnp.sum(dpre * cur16.astype(F32), axis=0, keepdims=True)
        for d in range(1, 4):
            du = du + cw_ref[3 - d:4 - d, :] * up[LCH * (d - 1):LCH * d]
            pgw_ref[3 - d:4 - d, :] += jnp.sum(dpre * sh[LCH * (d - 1):LCH * d], axis=0, keepdims=True)
        pgw_ref[4:5, :] += jnp.sum(dpre, axis=0, keepdims=True)
        dxbc_ref[...] = du.astype(BF16)
        dhead[...] = dpre[0:HALO, :]

    xbc_out = pl.BlockSpec((LCH, 2 * D), lambda b, c: (b * nc + nc - 1 - c, 0))
    acc = lambda w: pl.BlockSpec((8, w), lambda b, c: (0, 0))
    sdn, sup = _shift_matrices()
    return _hosted_call(
        body, comm, f"ssm_bwd_{li}", (B, nc),
        in_specs=[prev, cur, zed, psb, hpb, row, const(3 * LCH, LCH + HALO), const(3 * LCH, LCH + HALO),
                  const(4, 2 * D), const(1, 2 * D), const(1, LANES), const(1, LANES), const(1, D), const(1, D)],
        out_specs=[xbc_out, row, psb, acc(2 * D), acc(D), acc(LANES)],
        out_shape=[jax.ShapeDtypeStruct((T, 2 * D), BF16), jax.ShapeDtypeStruct((T, D), BF16),
                   jax.ShapeDtypeStruct((T, LANES), F32), jax.ShapeDtypeStruct((8, 2 * D), F32),
                   jax.ShapeDtypeStruct((8, D), F32), jax.ShapeDtypeStruct((8, LANES), F32)],
        scratch=[pltpu.VMEM((NH // 2, LANES, NST), F32), pltpu.VMEM((HALO, 2 * D), F32),
                 pltpu.VMEM((LCH, 2 * D), F32)],
        dims=("arbitrary", "arbitrary"),
        operands=(proj, proj, proj, ps, hp, dya, sdn, sup, cw, cb, dtb, alog, dsk, nw))


def _lane_row(v, offset):
    return jnp.pad(v.astype(F32), (offset, LANES - offset - v.shape[0]))[None]


def _pack_rows(arrays):
    parts = []
    for a in arrays:
        flat = a.reshape(-1).astype(F32)
        pad = (-flat.shape[0]) % LANES
        parts.append(jnp.pad(flat, (0, pad)))
    flat = jnp.concatenate(parts)
    pad = (-flat.shape[0]) % (8 * LANES)
    return jnp.pad(flat, (0, pad)).reshape(-1, LANES)


def _unpack_rows(pack, shapes):
    flat = pack.reshape(-1)
    out, pos = [], 0
    for shp in shapes:
        n = math.prod(shp)
        out.append(flat[pos:pos + n].reshape(shp))
        pos += n + (-n) % LANES
    return out


def _split_w_in(w):
    main = jnp.concatenate([w[:, 0:3072], w[:, 3088:4112], w[:, 4624:5648], w[:, 5648:8720], w[:, 8736:12832],
                            w[:, 4112:4624]], axis=1)
    small = jnp.concatenate([w[:, 3072:3088], w[:, 8720:8736], jnp.zeros((D, LANES - 2 * NH), w.dtype)], axis=1)
    return main, small


def _join_w_in(dw, ds):
    xbc, az, bq, bz, cq, ck, cv, cz, gates, bk, bv = dw
    return jnp.concatenate([xbc, az, ds[:, 0:NH], bq, bk, bv, bz, cq, ck, cv, ds[:, NH:2 * NH], cz, gates], axis=1)


def kernel(x, norm_w, w_in, conv_w, conv_b, dt_bias, a_log, d_skip, ssm_norm_w, sinks, f_bias, gate_bias, w_proj, w_out, final_norm_w, loss_target, m_norm_w, m_w_in, m_conv_w, m_conv_b, m_dt_bias, m_a_log, m_d_skip, m_ssm_norm_w, m_sinks, m_f_bias, m_gate_bias, m_w_proj, m_w_out, m_final_norm_w, v_norm_w, v_w_in, v_conv_w, v_conv_b, v_dt_bias, v_a_log, v_d_skip, v_ssm_norm_w, v_sinks, v_f_bias, v_gate_bias, v_w_proj, v_w_out, v_final_norm_w):
    Bl, S, _ = x.shape
    T = Bl * S
    depth = norm_w.shape[0]
    me = 4 * lax.axis_index("x") + 2 * lax.axis_index("y") + lax.axis_index("c")
    csh, gsh = conv_w.shape[2], gate_bias.shape[2]

    def gather_plan(l):
        small = jnp.concatenate([conv_w[l].reshape(-1), gate_bias[l].reshape(-1)]).reshape(-1, LANES)
        return _Comm("gather", [w_in[l].astype(BF16), w_proj[l].astype(BF16), w_out[l].astype(BF16), small])

    def unpack_weights(res):
        g_win, g_wp, g_wo, g_small = res
        flat = g_small.reshape(NDEV, -1)
        return (g_win.transpose(1, 0, 2).reshape(D, NIN),
                g_wp.transpose(1, 0, 2, 3).reshape(3, D, D),
                g_wo.reshape(D, D),
                flat[:, :4 * csh].reshape(NDEV, 4, csh).transpose(1, 0, 2).reshape(4, 2 * D),
                flat[:, 4 * csh:].reshape(NDEV, 3, gsh).transpose(1, 0, 2).reshape(3, D))

    def scatter_plan(gw_in, gw_p, gw_o):
        return _Comm("scatter", [gw_in.astype(BF16).reshape(D, NDEV, NSH).transpose(1, 0, 2),
                                 gw_p.astype(BF16).reshape(3, NDEV, D // NDEV, D).transpose(1, 0, 2, 3),
                                 gw_o.astype(BF16).reshape(NDEV, D // NDEV, D)])

    pos = jnp.arange(S, dtype=F32)
    inv_freq = ROPE_THETA ** (-jnp.arange(0, HD, 2, dtype=F32) / HD)
    ang = pos[:, None] * inv_freq[None, :]
    cos128 = jnp.tile(jnp.cos(ang), (1, 4))
    sign = jnp.where((jnp.arange(LANES) % HD) < HD // 2, -1.0, 1.0).astype(F32)
    sin128 = jnp.tile(jnp.sin(ang), (1, 4)) * sign[None, :]

    bq, nq = _fox_blocks(S)
    x2 = x.reshape(T, D)
    tgt2 = loss_target.reshape(T, D)

    saved = []
    xcur = x2
    weights = [None] * depth
    weights[0] = unpack_weights(_gather_two_level(gather_plan(0).arrays, "gather_weights_0"))
    for l in range(depth):
        win_l, wp_l, wo_l, cw_l, gb_l = weights[l]
        wmain, wsmall = _split_w_in(win_l)
        comm = gather_plan(l + 1) if l + 1 < depth else None
        res = _inproj_fwd(xcur, norm_w[l][None], wmain, wsmall, cos128, sin128, S, l, comm)
        proj, ps, h_t = res[:3]
        if comm is not None:
            weights[l + 1] = unpack_weights(res[3:])
        dtb = _lane_row(dt_bias[l], 0)
        alog = _lane_row(a_log[l], 0)
        fb = _lane_row(f_bias[l], NH)
        dsk = jnp.repeat(d_skip[l], HD)[None]
        ya, hp = _ssm_fwd(proj, ps, cw_l, conv_b[l][None], dtb, alog, dsk, ssm_norm_w[l][None], S, l)
        yb, ob, lse_b = _swa_fwd(proj, sinks[l], S, l)
        cum = _fox_cum(ps, fb, S, l)
        cumh = cum[:, NH:2 * NH].reshape(Bl, S, NH).transpose(0, 2, 1)
        cum_col = cumh[..., None]
        yc, oc, lse_c = _fox_fwd(proj, cum_col, S, l)
        xnext, br, y_t = _merge_fwd(ya, yb, yc, proj, gb_l, wp_l, wo_l, xcur, l)
        saved.append(dict(x=xcur, wmain=wmain, wsmall=wsmall, proj=proj, ps=ps, h_t=h_t, dtb=dtb, alog=alog, fb=fb,
                          dsk=dsk, hp=hp, ob=ob, lse_b=lse_b, cum_col=cum_col, oc=oc, lse_c=lse_c, br=br, y_t=y_t))
        xcur = xnext

    dx, dx16, st = _final_loss(xcur, tgt2, final_norm_w[None])
    loss_part = st[2, 0]
    g_final = st[0]

    gsm = {k: [None] * depth for k in ("norm_w", "conv_w", "conv_b", "dt_bias", "a_log", "d_skip", "ssm_norm_w",
                                      "sinks", "f_bias", "gate_bias")}
    parts = [None] * depth
    pending = None
    for l in reversed(range(depth)):
        sv = saved[l]
        proj, ps = sv["proj"], sv["ps"]
        _, wp_l, wo_l, cw_l, gb_l = weights[l]
        dbr, dgates, merged_t, dgb, dy_a, do_b, dbz, do_c, dcz = _merge_bwd(dx16, wo_l, wp_l, sv["br"], proj, gb_l,
                                                                            sv["ob"], sv["oc"], l)
        g_wo = _matmul(merged_t, dx16, F32, f"dwout_{l}")
        g_wp = _matmul_batched(sv["y_t"], dbr, F32, f"dwproj_{l}")
        gsm["gate_bias"][l] = dgb[0:3]
        res = _ssm_bwd(proj, ps, sv["hp"], dy_a, cw_l, conv_b[l][None], sv["dtb"], sv["alog"], sv["dsk"],
                       ssm_norm_w[l][None], S, l, pending)
        dxbc, daz, dps_a, pgw, pg1, pgh = res[:6]
        if pending is not None:
            parts[l + 1] = res[6:]
        gsm["conv_w"][l], gsm["conv_b"][l] = pgw[0:4], pgw[4]
        gsm["ssm_norm_w"][l] = pg1[0]
        gsm["dt_bias"][l], gsm["a_log"][l], gsm["d_skip"][l] = pgh[0, :NH], pgh[1, :NH], pgh[2, :NH]
        dq_b, dsk_b = _swa_bwd_dq(proj, do_b, sv["ob"], sv["lse_b"], sinks[l], cos128, sin128, S, l)
        dk_b, dv_b = _swa_bwd_dkv(proj, do_b, sv["ob"], sv["lse_b"], cos128, sin128, S, l)
        gsm["sinks"][l] = dsk_b[:, :, 0].reshape(NH)
        dq_c, dk_c, dv_c, dcum_k, dcum_q = _fox_bwd(proj, do_c, sv["oc"], sv["cum_col"], sv["lse_c"], S, l)
        dcum_tm = (dcum_k.reshape(Bl, NH, S) + dcum_q.reshape(Bl, NH, S)).transpose(0, 2, 1).reshape(T, NH)
        dcum_pad = jnp.pad(dcum_tm, ((0, 0), (NH, LANES - 2 * NH)))
        df, dfb = _fox_cum_bwd(dcum_pad, ps, sv["fb"], S, l)
        gsm["f_bias"][l] = dfb[0, NH:2 * NH]
        dps16 = (dps_a + df).astype(BF16)
        pieces = (dxbc, daz, dq_b, dbz, dq_c, dk_c, dv_c, dcz, dgates, dk_b, dv_b)
        dw_pieces = [_matmul(sv["h_t"], pc, F32, f"dwin_{l}_{i}") for i, pc in enumerate(pieces)]
        dws = _matmul(sv["h_t"], dps16, F32, f"dwin_small_{l}")
        plan = scatter_plan(_join_w_in(dw_pieces, dws), g_wp, g_wo)
        dkv_b = jnp.concatenate([dk_b, dv_b], axis=1)
        dh = _inproj_bwd_dx([(dxbc, OFF_XBC), (daz, OFF_AZ), (dq_b, OFF_BQ), (dbz, OFF_BZ)], sv["wmain"],
                            ("narrow", dps16, sv["wsmall"]), None, f"inproj_bwd_dh1_{l}")[0]
        dh = _inproj_bwd_dx([(dq_c, OFF_CQ), (dk_c, OFF_CK), (dv_c, OFF_CV), (dcz, OFF_CZ)], sv["wmain"],
                            ("acc", dh), None, f"inproj_bwd_dh2_{l}")[0]
        res = _inproj_bwd_dx([(dgates, OFF_G), (dkv_b, OFF_BK)], sv["wmain"], ("acc", dh),
                             (sv["x"], norm_w[l][None], dx), f"inproj_bwd_dx_{l}", plan if l == 0 else None)
        dx, dx16, dnw = res[:3]
        if l == 0:
            parts[0] = res[3:]
        else:
            pending = plan
        gsm["norm_w"][l] = dnw[0]

    big = {}
    for idx, (name, w, m, v) in enumerate((("w_in", w_in, m_w_in, v_w_in), ("w_proj", w_proj, m_w_proj, v_w_proj),
                                          ("w_out", w_out, m_w_out, v_w_out))):
        cols = w.shape[-1]
        res = _sum_adamw([parts[l][idx].reshape(NDEV, -1, cols) for l in range(depth)], w.reshape(depth, -1, cols),
                         m.reshape(depth, -1, cols), v.reshape(depth, -1, cols), f"adamw_{name}")
        big[name] = [r.reshape(w.shape) for r in res]

    small_names = ("norm_w", "conv_b", "dt_bias", "a_log", "d_skip", "ssm_norm_w", "sinks", "f_bias")
    small_parts = [jnp.stack(gsm[k]) for k in small_names] + [g_final, jnp.stack(gsm["conv_w"]),
                                                              jnp.stack(gsm["gate_bias"]), loss_part.reshape(1)]
    shapes = [a.shape for a in small_parts]
    summed = _unpack_rows(_all_reduce_small(_pack_rows(small_parts)), shapes)
    g_small = dict(zip(small_names, summed[:len(small_names)]))
    g_small["final_norm_w"] = summed[len(small_names)]
    g_small["conv_w"] = lax.dynamic_slice_in_dim(summed[len(small_names) + 1], me * csh, csh, axis=2)
    g_small["gate_bias"] = lax.dynamic_slice_in_dim(summed[len(small_names) + 2], me * gsh, gsh, axis=2)
    loss = summed[len(small_names) + 3][0]

    ws = dict(norm_w=norm_w, conv_w=conv_w, conv_b=conv_b, dt_bias=dt_bias, a_log=a_log, d_skip=d_skip,
              ssm_norm_w=ssm_norm_w, sinks=sinks, f_bias=f_bias, gate_bias=gate_bias, final_norm_w=final_norm_w)
    ms = dict(norm_w=m_norm_w, conv_w=m_conv_w, conv_b=m_conv_b, dt_bias=m_dt_bias, a_log=m_a_log, d_skip=m_d_skip,
              ssm_norm_w=m_ssm_norm_w, sinks=m_sinks, f_bias=m_f_bias, gate_bias=m_gate_bias,
              final_norm_w=m_final_norm_w)
    vs = dict(norm_w=v_norm_w, conv_w=v_conv_w, conv_b=v_conv_b, dt_bias=v_dt_bias, a_log=v_a_log, d_skip=v_d_skip,
              ssm_norm_w=v_ssm_norm_w, sinks=v_sinks, f_bias=v_f_bias, gate_bias=v_gate_bias,
              final_norm_w=v_final_norm_w)
    order = list(ws)
    oshapes = [ws[k].shape for k in order]
    res = _adamw_small(_pack_rows([g_small[k] for k in order]), _pack_rows([ws[k] for k in order]),
                       _pack_rows([ms[k] for k in order]), _pack_rows([vs[k] for k in order]))
    d_s, m_s, v_s = (dict(zip(order, _unpack_rows(r, oshapes))) for r in res)

    names = ("norm_w", "w_in", "conv_w", "conv_b", "dt_bias", "a_log", "d_skip", "ssm_norm_w", "sinks", "f_bias",
             "gate_bias", "w_proj", "w_out", "final_norm_w")
    grads, deltas, new_m, new_v = [], [], [], []
    for k in names:
        if k in big:
            g, d_, m_, v_ = big[k]
        else:
            g, d_, m_, v_ = g_small[k], d_s[k], m_s[k], v_s[k]
        grads.append(g)
        deltas.append(d_)
        new_m.append(m_)
        new_v.append(v_)
    return (loss, dx.reshape(Bl, S, D), *grads, *deltas, *new_m, *new_v)
```

```python
import functools
import math

import jax
import jax.numpy as jnp
from jax import lax
from jax.experimental import pallas as pl
from jax.experimental.pallas import tpu as pltpu

F32 = jnp.float32
BF16 = jnp.bfloat16
MESH = pl.DeviceIdType.MESH
NDEV = 8

D = 1024
NH = 16
HD = 64
NST = 128
NGRP = 4
LCH = 128
EPS = 1e-6
ROPE_THETA = 10000.0
SCALE = HD ** -0.5
NEG = -1e30

LANES = 128
VMEM_LIMIT = 56 * 1024 * 1024

OFF_XBC, OFF_AZ, OFF_BQ, OFF_BZ, OFF_CQ, OFF_CK, OFF_CV, OFF_CZ, OFF_G, OFF_BK, OFF_BV = (
    0, 2048, 3072, 4096, 5120, 6144, 7168, 8192, 9216, 12288, 12544)
NMAIN = 12800
NIN = 12832
NSH = NIN // NDEV

ROW_CHUNKS = ((0, 384), (384, 768), (768, 1024))

ADAM_LR, ADAM_B1, ADAM_B2, ADAM_EPS, ADAM_WD, ADAM_STEP = 0.001, 0.9, 0.999, 1e-08, 0.01, 10


def _cparams(dims=None, vmem=None):
    return pltpu.CompilerParams(dimension_semantics=dims, vmem_limit_bytes=vmem)


def _dot(a, b):
    return jnp.dot(a, b, preferred_element_type=F32)


def _dot_nt(a, b):
    return lax.dot_general(a, b, (((1,), (1,)), ((), ())), preferred_element_type=F32)


def _dot_tn(a, b):
    return lax.dot_general(a, b, (((0,), (0,)), ((), ())), preferred_element_type=F32)


def _dot_hi(a, b):
    return jnp.dot(a, b, precision=lax.Precision.HIGHEST, preferred_element_type=F32)


def _sigmoid(x):
    return 1.0 / (1.0 + jnp.exp(-x))


def _softplus(x):
    return jnp.maximum(x, 0.0) + jnp.log(1.0 + jnp.exp(-jnp.abs(x)))


def _lane_iota(n=LANES):
    return lax.broadcasted_iota(jnp.int32, (1, n), 1)


def _rot_half(x):
    first = (_lane_iota() % HD) < (HD // 2)
    return jnp.where(first, pltpu.roll(x, LANES - HD // 2, 1), pltpu.roll(x, HD // 2, 1))


def _head_sum(x, head):
    m = (_lane_iota() < HD) if head == 0 else (_lane_iota() >= HD)
    return jnp.sum(jnp.where(m, x, 0.0), axis=1, keepdims=True)


def _me_and_peers():
    x, y, c = lax.axis_index("x"), lax.axis_index("y"), lax.axis_index("c")
    me = 4 * x + 2 * y + c
    peers = []
    for k in range(1, NDEV):
        kx, ky, kc = (k >> 2) & 1, (k >> 1) & 1, k & 1
        px, py, pc = x ^ kx, y ^ ky, c ^ kc
        peers.append(((px, py, pc), 4 * px + 2 * py + pc))
    return me, peers


class _Comm:
    def __init__(self, kind, arrays):
        self.kind, self.arrays, self.n = kind, list(arrays), len(arrays)
        any_spec = pl.BlockSpec(memory_space=pl.ANY)
        self.in_specs = [any_spec] * self.n
        self.out_specs = [any_spec] * self.n
        self.out_shape = [jax.ShapeDtypeStruct(((NDEV,) + a.shape) if kind == "gather" else a.shape, a.dtype)
                          for a in self.arrays]
        self.scratch = [pltpu.SemaphoreType.DMA((self.n, NDEV - 1)), pltpu.SemaphoreType.DMA((self.n, NDEV - 1)),
                        pltpu.SemaphoreType.DMA((self.n,))]

    def copies(self, ins, outs, sems):
        send_sems, recv_sems, local_sems = sems
        me, peers = _me_and_peers()
        out = []
        for a in range(self.n):
            mine = ins[a] if self.kind == "gather" else ins[a].at[me]
            out.append(pltpu.make_async_copy(mine, outs[a].at[me], local_sems.at[a]))
            for k, (peer, pidx) in enumerate(peers):
                src = ins[a] if self.kind == "gather" else ins[a].at[pidx]
                out.append(pltpu.make_async_remote_copy(
                    src_ref=src, dst_ref=outs[a].at[me], send_sem=send_sems.at[a, k], recv_sem=recv_sems.at[a, k],
                    device_id=peer, device_id_type=MESH))
        return out

    def call(self, name):
        def body(*refs):
            cps = self.copies(refs[:self.n], refs[self.n:2 * self.n], refs[2 * self.n:])
            for cp in cps:
                cp.start()
            for cp in cps:
                cp.wait()

        return pl.pallas_call(body, name=name, out_shape=self.out_shape, in_specs=self.in_specs,
                              out_specs=self.out_specs, scratch_shapes=self.scratch)(*self.arrays)


def _gather_two_level(arrays, name):
    n = len(arrays)

    def body(*refs):
        ins, outs = refs[:n], refs[n:2 * n]
        send_sems, recv_sems, local_sems = refs[2 * n:]
        x, y, c = lax.axis_index("x"), lax.axis_index("y"), lax.axis_index("c")
        me, sibling = (x, y, c), (x, y, 1 - c)
        chips = [(1 - x, y), (x, 1 - y), (1 - x, 1 - y)]

        def slot(a, dev):
            return outs[a].at[4 * dev[0] + 2 * dev[1] + dev[2]]

        def copy(a, k, block, to, src=None):
            return pltpu.make_async_remote_copy(
                src_ref=slot(a, block) if src is None else src, dst_ref=slot(a, block),
                send_sem=send_sems.at[a, k], recv_sem=recv_sems.at[a, k], device_id=to, device_id_type=MESH)

        mine = [pltpu.make_async_copy(ins[a], slot(a, me), local_sems.at[a]) for a in range(n)]
        for cp in mine:
            cp.start()
        first = []
        for a in range(n):
            first.append(copy(a, 0, me, sibling, src=ins[a]))
            first += [copy(a, 1 + j, me, (*chip, c), src=ins[a]) for j, chip in enumerate(chips)]
        for cp in first:
            cp.start()
        passed = []
        for j, chip in enumerate(chips):
            for a in range(n):
                copy(a, 1 + j, (*chip, c), me).wait_recv()
                fwd = copy(a, 4 + j, (*chip, c), sibling)
                fwd.start()
                passed.append(fwd)
        for a in range(n):
            copy(a, 0, sibling, me).wait_recv()
            for j, chip in enumerate(chips):
                copy(a, 4 + j, (*chip, 1 - c), me).wait_recv()
        for cp in first + passed:
            cp.wait_send()
        for cp in mine:
            cp.wait()

    any_spec = pl.BlockSpec(memory_space=pl.ANY)
    return pl.pallas_call(
        body, name=name, out_shape=[jax.ShapeDtypeStruct((NDEV,) + a.shape, a.dtype) for a in arrays],
        in_specs=[any_spec] * n, out_specs=[any_spec] * n,
        scratch_shapes=[pltpu.SemaphoreType.DMA((n, NDEV - 1)), pltpu.SemaphoreType.DMA((n, NDEV - 1)),
                        pltpu.SemaphoreType.DMA((n,))])(*arrays)


def _hosted_call(body, comm, name, grid, in_specs, out_specs, out_shape, scratch, dims, operands):
    if comm is None:
        return pl.pallas_call(body, name=name, grid=grid, in_specs=in_specs, out_specs=out_specs, out_shape=out_shape,
                              scratch_shapes=scratch, compiler_params=_cparams(dims, VMEM_LIMIT))(*operands)
    n_in, n_out, n_scr, n = len(in_specs), len(out_specs), len(scratch), comm.n

    def hosted(*refs):
        hin, cin = refs[:n_in], refs[n_in:n_in + n]
        hout = refs[n_in + n:n_in + n + n_out]
        cout = refs[n_in + n + n_out:n_in + 2 * n + n_out]
        hscr = refs[n_in + 2 * n + n_out:n_in + 2 * n + n_out + n_scr]
        sems = refs[n_in + 2 * n + n_out + n_scr:]
        ids = [pl.program_id(a) for a in range(len(grid))]
        first = functools.reduce(jnp.logical_and, [i == 0 for i in ids])
        last = functools.reduce(jnp.logical_and, [i == g - 1 for i, g in zip(ids, grid)])

        @pl.when(first)
        def _():
            for cp in comm.copies(cin, cout, sems):
                cp.start()

        body(*hin, *hout, *hscr)

        @pl.when(last)
        def _():
            for cp in comm.copies(cin, cout, sems):
                cp.wait()

    return pl.pallas_call(
        hosted, name=name, grid=grid, in_specs=list(in_specs) + comm.in_specs,
        out_specs=list(out_specs) + comm.out_specs, out_shape=list(out_shape) + comm.out_shape,
        scratch_shapes=list(scratch) + comm.scratch,
        compiler_params=_cparams(("arbitrary",) * len(grid), VMEM_LIMIT))(*operands, *comm.arrays)


def _all_reduce_small(v):
    rows = v.shape[0]

    def body(v_ref, sum_ref, all_ref, send_sems, recv_sems):
        me, peers = _me_and_peers()
        all_ref[me] = v_ref[...]
        copies = []
        for k, (peer, _) in enumerate(peers):
            cp = pltpu.make_async_remote_copy(
                src_ref=v_ref, dst_ref=all_ref.at[me],
                send_sem=send_sems.at[k], recv_sem=recv_sems.at[k],
                device_id=peer, device_id_type=MESH)
            cp.start()
            copies.append(cp)
        for cp in copies:
            cp.wait()
        acc = all_ref[0]
        for d in range(1, NDEV):
            acc = acc + all_ref[d]
        sum_ref[...] = acc

    vm = pl.BlockSpec(memory_space=pltpu.VMEM)
    return pl.pallas_call(
        body, name="all_reduce_small",
        out_shape=jax.ShapeDtypeStruct((rows, LANES), F32),
        in_specs=[vm], out_specs=vm,
        scratch_shapes=[pltpu.VMEM((NDEV, rows, LANES), F32),
                        pltpu.SemaphoreType.DMA((NDEV - 1,)), pltpu.SemaphoreType.DMA((NDEV - 1,))],
    )(v)


def _adamw_math(w, g, m, v):
    m = ADAM_B1 * m + (1.0 - ADAM_B1) * g
    v = ADAM_B2 * v + (1.0 - ADAM_B2) * jnp.square(g)
    m_hat = m / (1.0 - ADAM_B1 ** ADAM_STEP)
    v_hat = v / (1.0 - ADAM_B2 ** ADAM_STEP)
    delta = -ADAM_LR * (m_hat / (jnp.sqrt(v_hat) + ADAM_EPS) + ADAM_WD * w)
    return delta, m, v


def _sum_adamw(parts, w, m, v, name):
    depth, rows, cols = w.shape
    tr = next(c for c in (256, 128, 64, 32, 16) if rows % c == 0)
    nb = rows // tr

    def body(*refs):
        p_refs, (w_ref, m_ref, v_ref, g_ref, d_ref, nm_ref, nv_ref) = refs[:depth], refs[depth:]
        l = pl.program_id(0)
        for ll in range(depth):
            @pl.when(l == ll)
            def _(ll=ll):
                g = p_refs[ll][0].astype(F32)
                for d in range(1, NDEV):
                    g = g + p_refs[ll][d].astype(F32)
                delta, nm, nv = _adamw_math(w_ref[0], g, m_ref[0], v_ref[0])
                g_ref[0] = g
                d_ref[0] = delta
                nm_ref[0] = nm
                nv_ref[0] = nv

    part = lambda ll: pl.BlockSpec((NDEV, tr, cols), lambda l, i, ll=ll: (0, jnp.where(l == ll, i, jnp.where(l < ll, 0, nb - 1)), 0))
    blk = pl.BlockSpec((1, tr, cols), lambda l, i: (l, i, 0))
    sds = jax.ShapeDtypeStruct((depth, rows, cols), F32)
    return pl.pallas_call(
        body, name=name, grid=(depth, nb),
        in_specs=[part(ll) for ll in range(depth)] + [blk, blk, blk],
        out_specs=[blk, blk, blk, blk], out_shape=[sds, sds, sds, sds],
        compiler_params=_cparams(("arbitrary", "arbitrary"), VMEM_LIMIT),
    )(*parts, w, m, v)


def _adamw_small(g, w, m, v):
    def body(g_ref, w_ref, m_ref, v_ref, d_ref, nm_ref, nv_ref):
        delta, nm, nv = _adamw_math(w_ref[...], g_ref[...], m_ref[...], v_ref[...])
        d_ref[...] = delta
        nm_ref[...] = nm
        nv_ref[...] = nv

    sds = jax.ShapeDtypeStruct(g.shape, F32)
    return pl.pallas_call(body, name="adamw_small", out_shape=[sds, sds, sds])(g, w, m, v)


def _matmul(a, b, out_dtype, name, tm=1024, tn=1024, tk=512):
    M, K = a.shape
    N = b.shape[1]
    tm, tn, tk = min(tm, M), min(tn, N), min(tk, K)
    nk = K // tk

    def body(a_ref, b_ref, o_ref, acc):
        k = pl.program_id(2)

        @pl.when(k == 0)
        def _():
            acc[...] = jnp.zeros_like(acc)

        acc[...] += _dot(a_ref[...], b_ref[...])

        @pl.when(k == nk - 1)
        def _():
            o_ref[...] = acc[...].astype(out_dtype)

    return pl.pallas_call(
        body, name=name, grid=(M // tm, N // tn, nk),
        in_specs=[pl.BlockSpec((tm, tk), lambda i, j, k: (i, k)), pl.BlockSpec((tk, tn), lambda i, j, k: (k, j))],
        out_specs=pl.BlockSpec((tm, tn), lambda i, j, k: (i, j)),
        out_shape=jax.ShapeDtypeStruct((M, N), out_dtype),
        scratch_shapes=[pltpu.VMEM((tm, tn), F32)],
        compiler_params=_cparams(("parallel", "parallel", "arbitrary"), VMEM_LIMIT),
    )(a, b)


def _matmul_batched(a, b, out_dtype, name, tm=1024, tn=1024, tk=512):
    G, M, K = a.shape
    N = b.shape[2]
    tm, tn, tk = min(tm, M), min(tn, N), min(tk, K)
    nk = K // tk

    def body(a_ref, b_ref, o_ref, acc):
        k = pl.program_id(3)

        @pl.when(k == 0)
        def _():
            acc[...] = jnp.zeros_like(acc)

        acc[...] += _dot(a_ref[0], b_ref[0])

        @pl.when(k == nk - 1)
        def _():
            o_ref[0] = acc[...].astype(out_dtype)

    return pl.pallas_call(
        body, name=name, grid=(G, M // tm, N // tn, nk),
        in_specs=[pl.BlockSpec((1, tm, tk), lambda g, i, j, k: (g, i, k)),
                  pl.BlockSpec((1, tk, tn), lambda g, i, j, k: (g, k, j))],
        out_specs=pl.BlockSpec((1, tm, tn), lambda g, i, j, k: (g, i, j)),
        out_shape=jax.ShapeDtypeStruct((G, M, N), out_dtype),
        scratch_shapes=[pltpu.VMEM((tm, tn), F32)],
        compiler_params=_cparams(("parallel", "parallel", "parallel", "arbitrary"), VMEM_LIMIT),
    )(a, b)


def _inproj_fwd(x2, nw, wmain, wsmall, cos128, sin128, S, li, comm=None):
    T = x2.shape[0]
    tm, tn = min(2048, S), 512
    nj, npos = NMAIN // tn, S // tm
    jq0, jk = OFF_BQ // tn, OFF_BK // tn

    def body(x_ref, nw_ref, w_ref, ws_ref, cos_ref, sin_ref, proj_ref, ps_ref, ht_ref, h_scr):
        j = pl.program_id(1)

        @pl.when(j == 0)
        def _():
            x = x_ref[...]
            r = lax.rsqrt(jnp.mean(x * x, axis=-1, keepdims=True) + EPS)
            h = (x * r * nw_ref[...]).astype(BF16)
            h_scr[...] = h
            ht_ref[...] = h.T
            ps_ref[...] = _dot(h, ws_ref[...])

        acc = _dot(h_scr[...], w_ref[...])

        def roped(c):
            xc = acc[:, LANES * c:LANES * (c + 1)]
            return (xc * cos_ref[...] + _rot_half(xc) * sin_ref[...]).astype(BF16)

        def plain(c):
            return acc[:, LANES * c:LANES * (c + 1)].astype(BF16)

        is_q = jnp.logical_or(j == jq0, j == jq0 + 1)
        is_k = j == jk

        @pl.when(is_q)
        def _():
            for c in range(4):
                proj_ref[:, LANES * c:LANES * (c + 1)] = roped(c)

        @pl.when(is_k)
        def _():
            for c in range(4):
                proj_ref[:, LANES * c:LANES * (c + 1)] = roped(c) if c < 2 else plain(c)

        @pl.when(jnp.logical_not(jnp.logical_or(is_q, is_k)))
        def _():
            proj_ref[...] = acc.astype(BF16)

    return _hosted_call(
        body, comm, f"inproj_fwd_{li}", (T // tm, nj),
        in_specs=[pl.BlockSpec((tm, D), lambda i, j: (i, 0)),
                  pl.BlockSpec((1, D), lambda i, j: (0, 0)),
                  pl.BlockSpec((D, tn), lambda i, j: (0, j)),
                  pl.BlockSpec((D, LANES), lambda i, j: (0, 0)),
                  pl.BlockSpec((tm, LANES), lambda i, j: (i % npos, 0)),
                  pl.BlockSpec((tm, LANES), lambda i, j: (i % npos, 0))],
        out_specs=[pl.BlockSpec((tm, tn), lambda i, j: (i, j)),
                   pl.BlockSpec((tm, LANES), lambda i, j: (i, 0)),
                   pl.BlockSpec((D, tm), lambda i, j: (0, i))],
        out_shape=[jax.ShapeDtypeStruct((T, NMAIN), BF16), jax.ShapeDtypeStruct((T, LANES), F32),
                   jax.ShapeDtypeStruct((D, T), BF16)],
        scratch=[pltpu.VMEM((tm, D), BF16)], dims=("parallel", "arbitrary"),
        operands=(x2, nw, wmain, wsmall, cos128, sin128))


def _inproj_bwd_dx(segs, wmain, init, final, name, comm=None):
    T = segs[0][0].shape[0]
    tm, tk = min(1024, T), 512
    ni = T // tm
    k0s, nks, c0s = [], [], []
    for arr, col0 in segs:
        k0s.append(sum(nks))
        nks.append(arr.shape[1] // tk)
        c0s.append(col0 // tk)
    nk = sum(nks)
    ns = len(segs)

    def in_range(k, s):
        return jnp.logical_and(k >= k0s[s], k < k0s[s] + nks[s])

    def wcol(i, k):
        g = 0
        for s in range(ns):
            g = g + jnp.where(in_range(k, s), c0s[s] + k - k0s[s], 0)
        return (0, g)

    n_init = 2 if init[0] == "narrow" else 1

    def body(*refs):
        seg_refs, w_ref = refs[:ns], refs[ns]
        init_refs = refs[ns + 1:ns + 1 + n_init]
        rest = refs[ns + 1 + n_init:]
        i, k = pl.program_id(0), pl.program_id(1)
        acc = rest[-1]

        @pl.when(k == 0)
        def _():
            if init[0] == "narrow":
                acc[...] = _dot_nt(init_refs[0][...], init_refs[1][...])
            else:
                acc[...] = init_refs[0][...]

        for s in range(ns):
            @pl.when(in_range(k, s))
            def _(s=s):
                acc[...] += _dot_nt(seg_refs[s][...], w_ref[...])

        if final is None:
            @pl.when(k == nk - 1)
            def _():
                rest[0][...] = acc[...]
        else:
            x_ref, nw_ref, dxo_ref, dx_ref, dx16_ref, dnw_ref = rest[:6]

            @pl.when(jnp.logical_and(i == 0, k == 0))
            def _():
                dnw_ref[...] = jnp.zeros_like(dnw_ref)

            @pl.when(k == nk - 1)
            def _():
                x = x_ref[...]
                r = lax.rsqrt(jnp.mean(x * x, axis=-1, keepdims=True) + EPS)
                dh = acc[...]
                g = dh * nw_ref[...]
                dx = dxo_ref[...] + r * g - x * (r * r * r) * jnp.mean(g * x, axis=-1, keepdims=True)
                dx_ref[...] = dx
                dx16_ref[...] = dx.astype(BF16)
                dnw_ref[0:1, :] += jnp.sum(dh * x * r, axis=0, keepdims=True)

    row = pl.BlockSpec((tm, D), lambda i, k: (i, 0))
    in_specs = [pl.BlockSpec((tm, tk), lambda i, k, s=s: (i, jnp.clip(k - k0s[s], 0, nks[s] - 1))) for s in range(ns)]
    in_specs.append(pl.BlockSpec((D, tk), wcol))
    operands = [a for a, _ in segs] + [wmain]
    if init[0] == "narrow":
        in_specs += [pl.BlockSpec((tm, LANES), lambda i, k: (i, 0)), pl.BlockSpec((D, LANES), lambda i, k: (0, 0))]
    else:
        in_specs.append(row)
    operands += list(init[1:])
    if final is None:
        out_specs, out_shape = [row], [jax.ShapeDtypeStruct((T, D), F32)]
    else:
        in_specs += [row, pl.BlockSpec((1, D), lambda i, k: (0, 0)), row]
        operands += list(final)
        out_specs = [row, row, pl.BlockSpec((8, D), lambda i, k: (0, 0))]
        out_shape = [jax.ShapeDtypeStruct((T, D), F32), jax.ShapeDtypeStruct((T, D), BF16),
                     jax.ShapeDtypeStruct((8, D), F32)]
    return _hosted_call(body, comm, name, (ni, nk), in_specs=in_specs, out_specs=out_specs, out_shape=out_shape,
                        scratch=[pltpu.VMEM((tm, D), F32)], dims=("arbitrary", "arbitrary"), operands=tuple(operands))


def _merge_fwd(ya, yb, yc, proj, gbias, wp, wout, x2, li):
    T = x2.shape[0]
    tm = min(512, T)
    gcol = OFF_G // D

    def body(ya_ref, yb_ref, yc_ref, g0_ref, g1_ref, g2_ref, gb_ref, wp_ref, wo_ref, x_ref, xn_ref, br_ref, yt_ref):
        merged = jnp.zeros((tm, D), F32)
        for i, (y_ref, g_ref) in enumerate(((ya_ref, g0_ref), (yb_ref, g1_ref), (yc_ref, g2_ref))):
            y = y_ref[...]
            yt_ref[i] = y.T
            br = _dot(y, wp_ref[i])
            br_ref[i] = br.astype(BF16)
            gate = _sigmoid(g_ref[...].astype(F32) + gb_ref[i:i + 1, :])
            merged = merged + gate * br
        xn_ref[...] = x_ref[...] + _dot(merged.astype(BF16), wo_ref[...])

    row = lambda c: pl.BlockSpec((tm, D), lambda i, c=c: (i, c))
    return pl.pallas_call(
        body, name=f"merge_fwd_{li}", grid=(T // tm,),
        in_specs=[row(0), row(0), row(0), row(gcol), row(gcol + 1), row(gcol + 2),
                  pl.BlockSpec((3, D), lambda i: (0, 0)),
                  pl.BlockSpec((3, D, D), lambda i: (0, 0, 0)),
                  pl.BlockSpec((D, D), lambda i: (0, 0)),
                  row(0)],
        out_specs=[row(0), pl.BlockSpec((3, tm, D), lambda i: (0, i, 0)), pl.BlockSpec((3, D, tm), lambda i: (0, 0, i))],
        out_shape=[jax.ShapeDtypeStruct((T, D), F32), jax.ShapeDtypeStruct((3, T, D), BF16),
                   jax.ShapeDtypeStruct((3, D, T), BF16)],
        compiler_params=_cparams(("parallel",), VMEM_LIMIT),
    )(ya, yb, yc, proj, proj, proj, gbias, wp, wout, x2)


def _merge_bwd(dxo16, wout, wp, br, proj, gbias, ob, oc, li):
    T = dxo16.shape[0]
    tm = min(256, T)
    gcol = OFF_G // D

    def body(dx_ref, wo_ref, wp_ref, br_ref, g0_ref, g1_ref, g2_ref, gb_ref, ob_ref, oc_ref, zb_ref, zc_ref,
             dbr_ref, dg_ref, mt_ref, dgb_ref, dya_ref, dob_ref, dzb_ref, doc_ref, dzc_ref):
        @pl.when(pl.program_id(0) == 0)
        def _():
            dgb_ref[...] = jnp.zeros_like(dgb_ref)

        dm = _dot_nt(dx_ref[...], wo_ref[...])
        merged = jnp.zeros((tm, D), F32)
        dys = []
        for i, g_ref in enumerate((g0_ref, g1_ref, g2_ref)):
            b = br_ref[i].astype(F32)
            gate = _sigmoid(g_ref[...].astype(F32) + gb_ref[i:i + 1, :])
            merged = merged + gate * b
            dbr = (dm * gate).astype(BF16)
            dbr_ref[i] = dbr
            dgate = dm * b * gate * (1.0 - gate)
            dg_ref[:, D * i:D * (i + 1)] = dgate.astype(BF16)
            dgb_ref[i:i + 1, :] += jnp.sum(dgate, axis=0, keepdims=True)
            dys.append(_dot_nt(dbr, wp_ref[i]))
        mt_ref[...] = merged.astype(BF16).T
        dya_ref[...] = dys[0].astype(BF16)
        for dy, o_ref, z_ref, do_ref, dz_ref in ((dys[1], ob_ref, zb_ref, dob_ref, dzb_ref),
                                                 (dys[2], oc_ref, zc_ref, doc_ref, dzc_ref)):
            z = z_ref[...].astype(F32)
            sg = _sigmoid(z)
            do_ref[...] = (dy * z * sg).astype(BF16)
            dz_ref[...] = (dy * o_ref[...].astype(F32) * sg * (1.0 + z * (1.0 - sg))).astype(BF16)

    row = lambda c: pl.BlockSpec((tm, D), lambda i, c=c: (i, c))
    sds = jax.ShapeDtypeStruct((T, D), BF16)
    return pl.pallas_call(
        body, name=f"merge_bwd_{li}", grid=(T // tm,),
        in_specs=[row(0), pl.BlockSpec((D, D), lambda i: (0, 0)), pl.BlockSpec((3, D, D), lambda i: (0, 0, 0)),
                  pl.BlockSpec((3, tm, D), lambda i: (0, i, 0)),
                  row(gcol), row(gcol + 1), row(gcol + 2),
                  pl.BlockSpec((3, D), lambda i: (0, 0)),
                  row(0), row(0), row(OFF_BZ // D), row(OFF_CZ // D)],
        out_specs=[pl.BlockSpec((3, tm, D), lambda i: (0, i, 0)),
                   pl.BlockSpec((tm, 3 * D), lambda i: (i, 0)),
                   pl.BlockSpec((D, tm), lambda i: (0, i)),
                   pl.BlockSpec((8, D), lambda i: (0, 0)),
                   row(0), row(0), row(0), row(0), row(0)],
        out_shape=[jax.ShapeDtypeStruct((3, T, D), BF16), jax.ShapeDtypeStruct((T, 3 * D), BF16),
                   jax.ShapeDtypeStruct((D, T), BF16), jax.ShapeDtypeStruct((8, D), F32), sds, sds, sds, sds, sds],
        compiler_params=_cparams(("arbitrary",), VMEM_LIMIT),
    )(dxo16, wout, wp, br, proj, proj, proj, gbias, ob, oc, proj, proj)


def _final_loss(x2, tgt, fw):
    T = x2.shape[0]
    tm = min(512, T)
    ni = T // tm

    def body(x_ref, t_ref, w_ref, dx_ref, dx16_ref, st_ref):
        i = pl.program_id(0)

        @pl.when(i == 0)
        def _():
            st_ref[...] = jnp.zeros_like(st_ref)

        x = x_ref[...]
        r = lax.rsqrt(jnp.mean(x * x, axis=-1, keepdims=True) + EPS)
        xh = x * r
        err = xh * w_ref[...] - t_ref[...]
        dy = err * (1.0 / D)
        g = dy * w_ref[...]
        dx = r * g - x * (r * r * r) * jnp.mean(g * x, axis=-1, keepdims=True)
        dx_ref[...] = dx
        dx16_ref[...] = dx.astype(BF16)
        st_ref[0:1, :] += jnp.sum(dy * xh, axis=0, keepdims=True)
        st_ref[1:2, :] += jnp.sum(err * err, axis=0, keepdims=True)

        @pl.when(i == ni - 1)
        def _():
            tot = jnp.sum(st_ref[1:2, :], axis=1, keepdims=True) * (0.5 / D)
            st_ref[2:3, :] = jnp.broadcast_to(tot, (1, D))

    row = pl.BlockSpec((tm, D), lambda i: (i, 0))
    return pl.pallas_call(
        body, name="final_loss", grid=(ni,),
        in_specs=[row, row, pl.BlockSpec((1, D), lambda i: (0, 0))],
        out_specs=[row, row, pl.BlockSpec((8, D), lambda i: (0, 0))],
        out_shape=[jax.ShapeDtypeStruct((T, D), F32), jax.ShapeDtypeStruct((T, D), BF16),
                   jax.ShapeDtypeStruct((8, D), F32)],
        compiler_params=_cparams(("arbitrary",), VMEM_LIMIT),
    )(x2, tgt, fw)


def _fox_cum(ps, fb_row, S, li):
    T = ps.shape[0]
    blk = min(4 * LCH, S)
    nb, nsub = S // blk, blk // LCH

    def body(ps_ref, fb_ref, cum_ref, carry):
        @pl.when(pl.program_id(1) == 0)
        def _():
            carry[...] = jnp.zeros_like(carry)

        r = lax.broadcasted_iota(jnp.int32, (LCH, LCH), 0)
        c = lax.broadcasted_iota(jnp.int32, (LCH, LCH), 1)
        tri = (r >= c).astype(F32)
        run = carry[0:1, :]
        for u in range(nsub):
            rows = slice(LCH * u, LCH * (u + 1))
            logf = -_softplus(-(ps_ref[rows, :] + fb_ref[...]))
            cum = _dot_hi(tri, logf) + run
            cum_ref[rows, :] = cum
            run = cum[LCH - 1:LCH, :]
        carry[0:1, :] = run

    return pl.pallas_call(
        body, name=f"fox_cum_{li}", grid=(T // S, nb),
        in_specs=[pl.BlockSpec((blk, LANES), lambda b, i: (b * nb + i, 0)),
                  pl.BlockSpec((1, LANES), lambda b, i: (0, 0))],
        out_specs=pl.BlockSpec((blk, LANES), lambda b, i: (b * nb + i, 0)),
        out_shape=jax.ShapeDtypeStruct((T, LANES), F32),
        scratch_shapes=[pltpu.VMEM((8, LANES), F32)],
        compiler_params=_cparams(("arbitrary", "arbitrary")),
    )(ps, fb_row)


def _fox_cum_bwd(dcum, ps, fb_row, S, li):
    T = ps.shape[0]
    rows_blk = min(4 * LCH, S)
    nb, nsub = S // rows_blk, rows_blk // LCH

    def body(dc_ref, ps_ref, fb_ref, df_ref, dfb_ref, carry):
        b, i = pl.program_id(0), pl.program_id(1)

        @pl.when(i == 0)
        def _():
            carry[...] = jnp.zeros_like(carry)

        @pl.when(jnp.logical_and(b == 0, i == 0))
        def _():
            dfb_ref[...] = jnp.zeros_like(dfb_ref)

        r = lax.broadcasted_iota(jnp.int32, (LCH, LCH), 0)
        c = lax.broadcasted_iota(jnp.int32, (LCH, LCH), 1)
        tri = (c >= r).astype(F32)
        lane = _lane_iota()
        live = jnp.logical_and(lane >= NH, lane < 2 * NH)
        run = carry[0:1, :]
        dfb = jnp.zeros((1, LANES), F32)
        for u in reversed(range(nsub)):
            rows = slice(LCH * u, LCH * (u + 1))
            dc = dc_ref[rows, :]
            dlogf = _dot_hi(tri, dc) + run
            run = run + jnp.sum(dc, axis=0, keepdims=True)
            df = jnp.where(live, dlogf * _sigmoid(-(ps_ref[rows, :] + fb_ref[...])), 0.0)
            df_ref[rows, :] = df
            dfb = dfb + jnp.sum(df, axis=0, keepdims=True)
        carry[0:1, :] = run
        dfb_ref[0:1, :] += dfb

    blk = pl.BlockSpec((rows_blk, LANES), lambda b, i: (b * nb + nb - 1 - i, 0))
    return pl.pallas_call(
        body, name=f"fox_cum_bwd_{li}", grid=(T // S, nb),
        in_specs=[blk, blk, pl.BlockSpec((1, LANES), lambda b, i: (0, 0))],
        out_specs=[blk, pl.BlockSpec((8, LANES), lambda b, i: (0, 0))],
        out_shape=[jax.ShapeDtypeStruct((T, LANES), F32), jax.ShapeDtypeStruct((8, LANES), F32)],
        scratch_shapes=[pltpu.VMEM((8, LANES), F32)],
        compiler_params=_cparams(("arbitrary", "arbitrary")),
    )(dcum, ps, fb_row)


def _fox_blocks(S):
    bq = min(512, S)
    return bq, S // bq


def _split3(c):
    hi = c.astype(BF16).astype(F32)
    r = c - hi
    mid = r.astype(BF16).astype(F32)
    return hi, mid, (r - mid).astype(BF16).astype(F32)


def _augment(x, parts, key_side, hh):
    lane = _lane_iota()
    b0 = HD if hh == 0 else 0
    p0, o0 = (b0 + 3, b0) if key_side else (b0, b0 + 3)
    out = jnp.where(jnp.logical_and(lane >= o0, lane < o0 + 3), 1.0, x)
    for t in range(3):
        out = jnp.where(lane == p0 + t, parts[t], out)
    return out.astype(BF16)


def _fox_fwd(proj, cum_col, S, li):
    T = proj.shape[0]
    B = T // S
    bq, nq = _fox_blocks(S)
    qc, kc, vc, zc = OFF_CQ // LANES, OFF_CK // LANES, OFF_CV // LANES, OFF_CZ // LANES

    def body(q_ref, k_ref, v_ref, z_ref, cc_ref, y_ref, o_ref, lse_ref, kaug):
        i = pl.program_id(2)
        m0 = _lane_iota() < HD

        @pl.when(i == 0)
        def _():
            kf = k_ref[...].astype(F32)
            for hh in range(2):
                kaug[hh] = _augment(kf, _split3(-cc_ref[0, hh]), True, hh)

        q2 = q_ref[...].astype(F32) * SCALE
        rows_q = pl.ds(pl.multiple_of(i * bq, bq), bq)
        row = lax.broadcasted_iota(jnp.int32, (bq, bq), 0)
        col = lax.broadcasted_iota(jnp.int32, (bq, bq), 1)
        outs = []
        for hh in range(2):
            sel = m0 if hh == 0 else jnp.logical_not(m0)
            qa = _augment(jnp.where(sel, q2, 0.0), _split3(cc_ref[0, hh, rows_q, :]), False, hh)

            def step(j, carry, masked, hh=hh, qa=qa):
                m, l, acc = carry
                start = pl.multiple_of(j * bq, bq)
                v2 = v_ref[pl.ds(start, bq), :]
                s = _dot_nt(qa, kaug[hh, pl.ds(start, bq), :])
                if masked:
                    s = jnp.where(row >= col, s, NEG)
                mn = jnp.maximum(m, jnp.max(s, axis=1, keepdims=True))
                alpha = jnp.exp(m - mn)
                p = jnp.exp(s - mn)
                l = alpha * l + jnp.sum(p, axis=1, keepdims=True)
                acc = alpha * acc + _dot(p.astype(BF16), v2)
                return mn, l, acc

            init = (jnp.full((bq, 1), NEG, F32), jnp.zeros((bq, 1), F32), jnp.zeros((bq, LANES), F32))
            carry = lax.fori_loop(0, i, functools.partial(step, masked=False), init)
            m, l, acc = step(i, carry, True)
            outs.append(acc / l)
            lse_ref[0, hh] = m + jnp.log(l)
        o2 = jnp.where(m0, outs[0], outs[1])
        z = z_ref[...].astype(F32)
        o_ref[...] = o2.astype(BF16)
        y_ref[...] = (o2 * z * _sigmoid(z)).astype(BF16)

    qblk = lambda c: pl.BlockSpec((bq, LANES), lambda b, p, i, c=c: (b * nq + i, c + p))
    sblk = lambda c: pl.BlockSpec((S, LANES), lambda b, p, i, c=c: (b, c + p))
    return pl.pallas_call(
        body, name=f"fox_fwd_{li}", grid=(B, NH // 2, nq),
        in_specs=[qblk(qc), sblk(kc), sblk(vc), qblk(zc),
                  pl.BlockSpec((1, 2, S, 1), lambda b, p, i: (b, p, 0, 0))],
        out_specs=[qblk(0), qblk(0), pl.BlockSpec((1, 2, bq, 1), lambda b, p, i: (b, p, i, 0))],
        out_shape=[jax.ShapeDtypeStruct((T, D), BF16), jax.ShapeDtypeStruct((T, D), BF16),
                   jax.ShapeDtypeStruct((B, NH, S, 1), F32)],
        scratch_shapes=[pltpu.VMEM((2, S, LANES), BF16)],
        compiler_params=_cparams(("parallel", "parallel", "arbitrary"), VMEM_LIMIT),
    )(proj, proj, proj, proj, cum_col)


def _fox_bwd(proj, do, o, cum_col, lse, S, li):
    T = proj.shape[0]
    B = T // S
    bq, nq = _fox_blocks(S)
    qc, kc, vc = OFF_CQ // LANES, OFF_CK // LANES, OFF_CV // LANES

    def body(q_ref, k_ref, v_ref, do_ref, o_ref, cc_ref, lse_ref, dq_ref, dk_ref, dv_ref, dc_ref, dr_ref,
             dq_scr, dr_scr, qaug):
        j = pl.program_id(2)
        m0 = _lane_iota() < HD

        @pl.when(j == 0)
        def _():
            dq_scr[...] = jnp.zeros_like(dq_scr)
            dr_scr[...] = jnp.zeros_like(dr_scr)
            qf = q_ref[...].astype(F32) * SCALE
            for hh in range(2):
                sel = m0 if hh == 0 else jnp.logical_not(m0)
                qaug[hh] = _augment(jnp.where(sel, qf, 0.0), _split3(cc_ref[0, hh] - lse_ref[0, hh]), False, hh)

        k2 = k_ref[...]
        v2 = v_ref[...]
        zk = jnp.zeros_like(k2)
        kh = (jnp.where(m0, k2, zk), jnp.where(m0, zk, k2))
        kf = k2.astype(F32)
        rows_k = pl.ds(pl.multiple_of(j * bq, bq), bq)
        ka = [_augment(kf, _split3(-cc_ref[0, hh, rows_k, :]), True, hh) for hh in range(2)]
        row = lax.broadcasted_iota(jnp.int32, (bq, bq), 0)
        col = lax.broadcasted_iota(jnp.int32, (bq, bq), 1)

        def step(i, carry, masked):
            dk, dv, dc0, dc1 = carry
            dcs = [dc0, dc1]
            start = pl.multiple_of(i * bq, bq)
            q2 = q_ref[pl.ds(start, bq), :]
            do2 = do_ref[pl.ds(start, bq), :]
            prod = do2.astype(F32) * o_ref[pl.ds(start, bq), :].astype(F32)
            zq = jnp.zeros_like(q2)
            dq = jnp.zeros((bq, LANES), F32)
            for hh in range(2):
                sel = m0 if hh == 0 else jnp.logical_not(m0)
                qh = jnp.where(sel, q2, zq)
                doh = jnp.where(sel, do2, zq)
                delta = _head_sum(prod, hh)
                s = _dot_nt(qaug[hh, pl.ds(start, bq), :], ka[hh])
                if masked:
                    s = jnp.where(row >= col, s, NEG)
                p = jnp.exp(s)
                dp = _dot_nt(doh, v2)
                ds = p * (dp - delta)
                dcs[hh] = dcs[hh] - jnp.sum(ds, axis=0, keepdims=True)
                dr_scr[hh, pl.ds(start, bq), :] += jnp.sum(ds, axis=1, keepdims=True)
                dsb = ds.astype(BF16)
                dv = dv + _dot_tn(p.astype(BF16), doh)
                dk = dk + _dot_tn(dsb, qh)
                dq = dq + _dot(dsb, kh[hh])
            dq_scr[pl.ds(start, bq), :] += dq
            return dk, dv, dcs[0], dcs[1]

        zero = jnp.zeros((bq, LANES), F32)
        zrow = jnp.zeros((1, bq), F32)
        carry = step(j, (zero, zero, zrow, zrow), True)
        dk, dv, dc0, dc1 = lax.fori_loop(j + 1, nq, functools.partial(step, masked=False), carry)
        dk_ref[...] = (dk * SCALE).astype(BF16)
        dv_ref[...] = dv.astype(BF16)
        dc_ref[0, 0, 0] = dc0
        dc_ref[0, 1, 0] = dc1

        @pl.when(j == nq - 1)
        def _():
            dq_ref[...] = (dq_scr[...] * SCALE).astype(BF16)
            dr_ref[0] = dr_scr[...]

    sblk = lambda c: pl.BlockSpec((S, LANES), lambda b, p, j, c=c: (b, c + p))
    kblk = lambda c: pl.BlockSpec((bq, LANES), lambda b, p, j, c=c: (b * nq + j, c + p))
    col_spec = pl.BlockSpec((1, 2, S, 1), lambda b, p, j: (b, p, 0, 0))
    return pl.pallas_call(
        body, name=f"fox_bwd_{li}", grid=(B, NH // 2, nq),
        in_specs=[sblk(qc), kblk(kc), kblk(vc), sblk(0), sblk(0), col_spec, col_spec],
        out_specs=[sblk(0), kblk(0), kblk(0), pl.BlockSpec((1, 2, 1, 1, bq), lambda b, p, j: (b, p, j, 0, 0)),
                   col_spec],
        out_shape=[jax.ShapeDtypeStruct((T, D), BF16), jax.ShapeDtypeStruct((T, D), BF16),
                   jax.ShapeDtypeStruct((T, D), BF16), jax.ShapeDtypeStruct((B, NH, nq, 1, bq), F32),
                   jax.ShapeDtypeStruct((B, NH, S, 1), F32)],
        scratch_shapes=[pltpu.VMEM((S, LANES), F32), pltpu.VMEM((2, S, 1), F32), pltpu.VMEM((2, S, LANES), BF16)],
        compiler_params=_cparams(("parallel", "parallel", "arbitrary"), VMEM_LIMIT),
    )(proj, proj, proj, do, o, cum_col, lse)


def _swa_blocks(S):
    bq = min(512, S)
    return bq, S // bq, bq // LCH


def _dup_head(xw, kvl):
    m0 = _lane_iota() < HD
    a = jnp.where(m0 if kvl == 0 else jnp.logical_not(m0), xw, 0.0)
    return (a + pltpu.roll(a, HD, 1)).astype(BF16)


def _band(same_block):
    r = lax.broadcasted_iota(jnp.int32, (LCH, LCH), 0)
    c = lax.broadcasted_iota(jnp.int32, (LCH, LCH), 1)
    return (c <= r) if same_block else (c > r)


def _stack_heads(ref, rows, kvl):
    m0 = _lane_iota() < HD
    parts = []
    for ch in (2 * kvl, 2 * kvl + 1):
        x = ref[rows, LANES * ch:LANES * (ch + 1)]
        parts += [jnp.where(m0, x, jnp.zeros_like(x)), jnp.where(m0, jnp.zeros_like(x), x)]
    return jnp.concatenate(parts, axis=0)


def _stack_delta(do_ref, o_ref, rows, kvl, scale=None):
    parts = []
    for ch in (2 * kvl, 2 * kvl + 1):
        lanes = slice(LANES * ch, LANES * (ch + 1))
        prod = do_ref[rows, lanes].astype(F32) * o_ref[rows, lanes].astype(F32)
        parts += [_head_sum(prod, 0), _head_sum(prod, 1)]
    out = jnp.concatenate(parts, axis=0)
    return out if scale is None else out * scale


def _stack_cols(ref, rows, kvl):
    return jnp.concatenate([ref[0, 4 * kvl + t, rows, :] for t in range(4)], axis=0)


def _swa_fwd(proj, sinks, S, li):
    T = proj.shape[0]
    B = T // S
    bq, nq, nsub = _swa_blocks(S)
    nrow = S // LCH
    qc, zc, kc, vc = OFF_BQ // 512, OFF_BZ // 512, OFF_BK // LANES, OFF_BV // LANES

    def body(sk_ref, q_ref, z_ref, kp_ref, kc_ref, vp_ref, vc_ref, y_ref, o_ref, lse_ref):
        c, i = pl.program_id(0), pl.program_id(2)
        m0 = _lane_iota() < HD
        kw = jnp.concatenate([kp_ref[...].astype(F32), kc_ref[...].astype(F32)], axis=0)
        vw = jnp.concatenate([vp_ref[...].astype(F32), vc_ref[...].astype(F32)], axis=0)
        kd = (_dup_head(kw, 0), _dup_head(kw, 1))
        vd = (_dup_head(vw, 0), _dup_head(vw, 1))
        valid = jnp.concatenate([_band(False), _band(True)], axis=1)
        col = lax.broadcasted_iota(jnp.int32, (LCH, 2 * LCH), 1)
        valid_first = jnp.logical_and(valid, jnp.logical_or(col >= LCH, i > 0))
        valid4 = jnp.concatenate([valid] * 4, axis=0)
        valid4_first = jnp.concatenate([valid_first] * 4, axis=0)
        for r in range(nsub):
            rows = slice(LCH * r, LCH * (r + 1))
            msk = valid4_first if r == 0 else valid4
            for kvl in range(2):
                kwin = kd[kvl][LCH * r:LCH * (r + 2)]
                vwin = vd[kvl][LCH * r:LCH * (r + 2)]
                qs = _stack_heads(q_ref, rows, kvl)
                sink = jnp.concatenate([jnp.full((LCH, 1), sk_ref[8 * c + 4 * kvl + t], F32) for t in range(4)], axis=0)
                s = jnp.where(msk, _dot_nt(qs, kwin) * SCALE, NEG)
                m = jnp.maximum(jnp.max(s, axis=1, keepdims=True), sink)
                p = jnp.exp(s - m)
                l = jnp.sum(p, axis=1, keepdims=True) + jnp.exp(sink - m)
                os_ = _dot(p.astype(BF16), vwin) / l
                lse = m + jnp.log(l)
                for t in range(4):
                    lse_ref[0, 4 * kvl + t, rows, :] = lse[LCH * t:LCH * (t + 1)]
                for u in range(2):
                    lanes = slice(LANES * (2 * kvl + u), LANES * (2 * kvl + u + 1))
                    o2 = jnp.where(m0, os_[LCH * 2 * u:LCH * (2 * u + 1)], os_[LCH * (2 * u + 1):LCH * (2 * u + 2)])
                    z = z_ref[rows, lanes].astype(F32)
                    o_ref[rows, lanes] = o2.astype(BF16)
                    y_ref[rows, lanes] = (o2 * z * _sigmoid(z)).astype(BF16)

    wide = lambda cc: pl.BlockSpec((bq, 512), lambda c, b, i, cc=cc: (b * nq + i, cc + c))
    cur = lambda cc: pl.BlockSpec((bq, LANES), lambda c, b, i, cc=cc: (b * nq + i, cc + c))
    prev = lambda cc: pl.BlockSpec((LCH, LANES), lambda c, b, i, cc=cc: (b * nrow + jnp.maximum(i * nsub - 1, 0), cc + c))
    return pl.pallas_call(
        body, name=f"swa_fwd_{li}", grid=(2, B, nq),
        in_specs=[pl.BlockSpec(memory_space=pltpu.SMEM), wide(qc), wide(zc), prev(kc), cur(kc), prev(vc), cur(vc)],
        out_specs=[wide(0), wide(0), pl.BlockSpec((1, 8, bq, 1), lambda c, b, i: (b, c, i, 0))],
        out_shape=[jax.ShapeDtypeStruct((T, D), BF16), jax.ShapeDtypeStruct((T, D), BF16),
                   jax.ShapeDtypeStruct((B, NH, S, 1), F32)],
        compiler_params=_cparams(("parallel", "parallel", "parallel"), VMEM_LIMIT),
    )(sinks, proj, proj, proj, proj, proj, proj)


def _swa_bwd_dq(proj, do, o, lse, sinks, cos128, sin128, S, li):
    T = proj.shape[0]
    B = T // S
    bq, nq, nsub = _swa_blocks(S)
    nrow = S // LCH
    qc, kc, vc = OFF_BQ // 512, OFF_BK // LANES, OFF_BV // LANES

    def body(sk_ref, q_ref, do_ref, o_ref, lse_ref, kp_ref, kc_ref, vp_ref, vc_ref, cos_ref, sin_ref, dq_ref, dsk_ref):
        c, b, i = pl.program_id(0), pl.program_id(1), pl.program_id(2)

        @pl.when(jnp.logical_and(b == 0, i == 0))
        def _():
            dsk_ref[...] = jnp.zeros_like(dsk_ref)

        m0 = _lane_iota() < HD
        kw = jnp.concatenate([kp_ref[...].astype(F32), kc_ref[...].astype(F32)], axis=0)
        vw = jnp.concatenate([vp_ref[...].astype(F32), vc_ref[...].astype(F32)], axis=0)
        kd = (_dup_head(kw, 0), _dup_head(kw, 1))
        vd = (_dup_head(vw, 0), _dup_head(vw, 1))
        valid = jnp.concatenate([_band(False), _band(True)], axis=1)
        col = lax.broadcasted_iota(jnp.int32, (LCH, 2 * LCH), 1)
        valid_first = jnp.logical_and(valid, jnp.logical_or(col >= LCH, i > 0))
        dsk = [jnp.zeros((1, 1), F32) for _ in range(8)]
        valid4 = jnp.concatenate([valid] * 4, axis=0)
        valid4_first = jnp.concatenate([valid_first] * 4, axis=0)
        for r in range(nsub):
            rows = slice(LCH * r, LCH * (r + 1))
            msk = valid4_first if r == 0 else valid4
            for kvl in range(2):
                kwin = kd[kvl][LCH * r:LCH * (r + 2)]
                vwin = vd[kvl][LCH * r:LCH * (r + 2)]
                qs = _stack_heads(q_ref, rows, kvl)
                dos = _stack_heads(do_ref, rows, kvl)
                delta = _stack_delta(do_ref, o_ref, rows, kvl)
                lse = _stack_cols(lse_ref, rows, kvl)
                sink = jnp.concatenate([jnp.full((LCH, 1), sk_ref[8 * c + 4 * kvl + t], F32) for t in range(4)], axis=0)
                s = jnp.where(msk, _dot_nt(qs, kwin) * SCALE, NEG)
                p = jnp.exp(s - lse)
                ds = p * (_dot_nt(dos, vwin) - delta)
                dqs = _dot(ds.astype(BF16), kwin) * SCALE
                dsink = jnp.exp(sink - lse) * delta
                for t in range(4):
                    hl = 4 * kvl + t
                    dsk[hl] = dsk[hl] - jnp.sum(dsink[LCH * t:LCH * (t + 1)], axis=0, keepdims=True)
                for u in range(2):
                    lanes = slice(LANES * (2 * kvl + u), LANES * (2 * kvl + u + 1))
                    dq2 = jnp.where(m0, dqs[LCH * 2 * u:LCH * (2 * u + 1)], dqs[LCH * (2 * u + 1):LCH * (2 * u + 2)])
                    dq2 = dq2 * cos_ref[rows, :] - _rot_half(dq2) * sin_ref[rows, :]
                    dq_ref[rows, lanes] = dq2.astype(BF16)
        for hl in range(8):
            dsk_ref[0, hl:hl + 1, :] += jnp.broadcast_to(dsk[hl], (1, LANES))

    wide = lambda cc: pl.BlockSpec((bq, 512), lambda c, b, i, cc=cc: (b * nq + i, cc + c))
    cur = lambda cc: pl.BlockSpec((bq, LANES), lambda c, b, i, cc=cc: (b * nq + i, cc + c))
    prev = lambda cc: pl.BlockSpec((LCH, LANES), lambda c, b, i, cc=cc: (b * nrow + jnp.maximum(i * nsub - 1, 0), cc + c))
    pos = pl.BlockSpec((bq, LANES), lambda c, b, i: (i, 0))
    return pl.pallas_call(
        body, name=f"swa_bwd_dq_{li}", grid=(2, B, nq),
        in_specs=[pl.BlockSpec(memory_space=pltpu.SMEM), wide(qc), wide(0), wide(0),
                  pl.BlockSpec((1, 8, bq, 1), lambda c, b, i: (b, c, i, 0)),
                  prev(kc), cur(kc), prev(vc), cur(vc), pos, pos],
        out_specs=[wide(0), pl.BlockSpec((1, 8, LANES), lambda c, b, i: (c, 0, 0))],
        out_shape=[jax.ShapeDtypeStruct((T, D), BF16), jax.ShapeDtypeStruct((2, 8, LANES), F32)],
        compiler_params=_cparams(("arbitrary", "arbitrary", "arbitrary"), VMEM_LIMIT),
    )(sinks, proj, do, o, lse, proj, proj, proj, proj, cos128, sin128)


def _swa_bwd_dkv(proj, do, o, lse, cos128, sin128, S, li):
    T = proj.shape[0]
    B = T // S
    bk, nk, nsub = _swa_blocks(S)
    nrow = S // LCH
    qc, kc, vc = OFF_BQ // 512, OFF_BK // LANES, OFF_BV // LANES

    def body(q_ref, qn_ref, do_ref, don_ref, o_ref, on_ref, lse_ref, lsen_ref, k_ref, v_ref, cos_ref, sin_ref,
             dk_ref, dv_ref):
        j = pl.program_id(2)
        m0 = _lane_iota() < HD
        has_next = (j < nk - 1).astype(F32)
        kf = k_ref[...].astype(F32)
        vf = v_ref[...].astype(F32)
        kd = (_dup_head(kf, 0), _dup_head(kf, 1))
        vd = (_dup_head(vf, 0), _dup_head(vf, 1))
        masks4 = (jnp.concatenate([_band(True)] * 4, axis=0), jnp.concatenate([_band(False)] * 4, axis=0))
        for kr in range(nsub):
            krows = slice(LCH * kr, LCH * (kr + 1))
            dk = jnp.zeros((LCH, LANES), F32)
            dv = jnp.zeros((LCH, LANES), F32)
            for dq_blk in range(2):
                rq = kr + dq_blk
                nxt = rq == nsub
                qrows = slice(0, LCH) if nxt else slice(LCH * rq, LCH * (rq + 1))
                qr, dor, orr, lr = (qn_ref, don_ref, on_ref, lsen_ref) if nxt else (q_ref, do_ref, o_ref, lse_ref)
                for kvl in range(2):
                    qs = _stack_heads(qr, qrows, kvl)
                    dos = _stack_heads(dor, qrows, kvl)
                    delta = _stack_delta(dor, orr, qrows, kvl, has_next if nxt else None)
                    if nxt:
                        dos = (dos.astype(F32) * has_next).astype(BF16)
                    s = jnp.where(masks4[dq_blk], _dot_nt(qs, kd[kvl][krows]) * SCALE, NEG)
                    p = jnp.exp(s - _stack_cols(lr, qrows, kvl))
                    ds = p * (_dot_nt(dos, vd[kvl][krows]) - delta)
                    dvc = _dot_tn(p.astype(BF16), dos)
                    dkc = _dot_tn(ds.astype(BF16), qs) * SCALE
                    own = m0 if kvl == 0 else jnp.logical_not(m0)
                    dv = dv + jnp.where(own, dvc + pltpu.roll(dvc, HD, 1), 0.0)
                    dk = dk + jnp.where(own, dkc + pltpu.roll(dkc, HD, 1), 0.0)
            dk = dk * cos_ref[krows, :] - _rot_half(dk) * sin_ref[krows, :]
            dk_ref[krows, :] = dk.astype(BF16)
            dv_ref[krows, :] = dv.astype(BF16)

    wide = lambda cc: pl.BlockSpec((bk, 512), lambda c, b, j, cc=cc: (b * nk + j, cc + c))
    nxt = lambda cc: pl.BlockSpec((LCH, 512), lambda c, b, j, cc=cc: (b * nrow + jnp.minimum((j + 1) * nsub, nrow - 1), cc + c))
    cur = lambda cc: pl.BlockSpec((bk, LANES), lambda c, b, j, cc=cc: (b * nk + j, cc + c))
    pos = pl.BlockSpec((bk, LANES), lambda c, b, j: (j, 0))
    return pl.pallas_call(
        body, name=f"swa_bwd_dkv_{li}", grid=(2, B, nk),
        in_specs=[wide(qc), nxt(qc), wide(0), nxt(0), wide(0), nxt(0),
                  pl.BlockSpec((1, 8, bk, 1), lambda c, b, j: (b, c, j, 0)),
                  pl.BlockSpec((1, 8, LCH, 1), lambda c, b, j: (b, c, jnp.minimum((j + 1) * nsub, nrow - 1), 0)),
                  cur(kc), cur(vc), pos, pos],
        out_specs=[cur(0), cur(0)],
        out_shape=[jax.ShapeDtypeStruct((T, 2 * LANES), BF16), jax.ShapeDtypeStruct((T, 2 * LANES), BF16)],
        compiler_params=_cparams(("parallel", "parallel", "parallel"), VMEM_LIMIT),
    )(proj, proj, do, do, o, o, lse, lse, proj, proj, cos128, sin128)


HALO = 16


def _shift_matrices():
    r = lax.broadcasted_iota(jnp.int32, (3 * LCH, LCH + HALO), 0)
    c = lax.broadcasted_iota(jnp.int32, (3 * LCH, LCH + HALO), 1)
    t, d = r % LCH, r // LCH + 1
    return (c == HALO + t - d).astype(BF16), (c == t + d).astype(BF16)


def _ssm_chunk_pre(prev16, cur16, first, sdn_ref, cw_ref, cb_ref, ps, dtb, alog):
    ext16 = jnp.concatenate([jnp.where(first, jnp.zeros_like(prev16), prev16), cur16], axis=0)
    sh = _dot(sdn_ref[...], ext16)
    pre = cb_ref[...] + cw_ref[3:4, :] * cur16.astype(F32)
    for d in range(1, 4):
        pre = pre + cw_ref[3 - d:4 - d, :] * sh[LCH * (d - 1):LCH * d]
    sg = _sigmoid(pre)
    dt = _softplus(ps + dtb)
    a = -jnp.exp(alog)
    r = lax.broadcasted_iota(jnp.int32, (LCH, LCH), 0)
    c = lax.broadcasted_iota(jnp.int32, (LCH, LCH), 1)
    acum = _dot_hi((r >= c).astype(F32), dt * a)
    return pre, sg, dt, a, acum, sh


def _pairsel(v, p):
    return jnp.where(_lane_iota() < HD, v[:, 2 * p:2 * p + 1], v[:, 2 * p + 1:2 * p + 2])


def _decay(acum, acum_t, h):
    r = lax.broadcasted_iota(jnp.int32, (LCH, LCH), 0)
    c = lax.broadcasted_iota(jnp.int32, (LCH, LCH), 1)
    causal = r >= c
    seg = acum[:, h:h + 1] - acum_t[h:h + 1, :]
    return jnp.where(causal, jnp.exp(jnp.where(causal, seg, 0.0)), 0.0)


def _ssm_pair_fwd(p, x, dt, acum, acum_t, e_all, w_all, cd, cb_g, b_g, c_g, hprev, dsk_ref):
    m0 = _lane_iota() < HD
    lanes = slice(LANES * p, LANES * (p + 1))
    x2 = x[:, lanes]
    dt2 = _pairsel(dt, p)
    xdt2 = x2 * dt2
    xdtb = xdt2.astype(BF16)
    lms, ms, yds = [], [], []
    for hh in range(2):
        lm = _decay(acum, acum_t, 2 * p + hh)
        mm = cb_g * lm
        lms.append(lm)
        ms.append(mm)
        yds.append(_dot(mm.astype(BF16), xdtb))
    yd2 = jnp.where(m0, yds[0], yds[1])
    w2 = _pairsel(w_all, p)
    xw = (xdt2 * w2).astype(BF16)
    s2 = _dot_tn(xw, b_g)
    z2 = _dot_nt(c_g, hprev.astype(BF16))
    e2 = _pairsel(e_all, p)
    rowsel = lax.broadcasted_iota(jnp.int32, (LANES, 1), 0) < HD
    cdcol = jnp.where(rowsel, cd[:, 2 * p:2 * p + 1], cd[:, 2 * p + 1:2 * p + 2])
    y2 = yd2 + z2 * e2 + dsk_ref[:, lanes] * x2
    return dict(x2=x2, dt2=dt2, xdt2=xdt2, xdtb=xdtb, lms=lms, ms=ms, yd2=yd2, w2=w2, xw=xw, s2=s2, z2=z2, e2=e2,
                cdcol=cdcol, y2=y2)


def _ssm_specs(S, rev):
    nc = S // LCH
    ch = (lambda c: nc - 1 - c) if rev else (lambda c: c)
    prev = pl.BlockSpec((HALO, 2 * D), lambda b, c: (jnp.maximum(b * (S // HALO) + ch(c) * (LCH // HALO) - 1, 0), 0))
    cur = pl.BlockSpec((LCH, 2 * D), lambda b, c: (b * nc + ch(c), 0))
    zed = pl.BlockSpec((LCH, D), lambda b, c: (b * nc + ch(c), OFF_AZ // D))
    row = pl.BlockSpec((LCH, D), lambda b, c: (b * nc + ch(c), 0))
    psb = pl.BlockSpec((LCH, LANES), lambda b, c: (b * nc + ch(c), 0))
    hpb = pl.BlockSpec((1, 1, NH // 2, LANES, NST), lambda b, c: (b, ch(c), 0, 0, 0))
    const = lambda r, w: pl.BlockSpec((r, w), lambda b, c: (0, 0))
    return nc, prev, cur, zed, row, psb, hpb, const


def _ssm_fwd(proj, ps, cw, cb, dtb, alog, dsk, nw, S, li):
    T = proj.shape[0]
    B = T // S
    nc, prev, cur, zed, row, psb, hpb, const = _ssm_specs(S, False)

    def body(prev_ref, cur_ref, z_ref, ps_ref, sdn_ref, cw_ref, cb_ref, dtb_ref, alog_ref, dsk_ref, nw_ref,
             ya_ref, hp_ref, h_scr):
        c = pl.program_id(1)

        @pl.when(c == 0)
        def _():
            h_scr[...] = jnp.zeros_like(h_scr)

        pre, sg, dt, a, acum, _ = _ssm_chunk_pre(prev_ref[...], cur_ref[...], c == 0, sdn_ref, cw_ref, cb_ref,
                                                 ps_ref[...], dtb_ref[...], alog_ref[...])
        act = pre * sg
        acum_t = acum.T
        e_all = jnp.exp(acum)
        last = acum[LCH - 1:LCH, :]
        w_all = jnp.exp(last - acum)
        cd = jnp.exp(last)
        x = act[:, :D]
        for g in range(NGRP):
            b_g = act[:, D + NST * g:D + NST * (g + 1)].astype(BF16)
            c_g = act[:, D + NGRP * NST + NST * g:D + NGRP * NST + NST * (g + 1)].astype(BF16)
            cb_g = _dot_nt(c_g, b_g)
            ygs = []
            for p in (2 * g, 2 * g + 1):
                hprev = h_scr[p]
                hp_ref[0, 0, p] = hprev
                f = _ssm_pair_fwd(p, x, dt, acum, acum_t, e_all, w_all, cd, cb_g, b_g, c_g, hprev, dsk_ref)
                h_scr[p] = hprev * f["cdcol"] + f["s2"]
                z2 = z_ref[:, LANES * p:LANES * (p + 1)].astype(F32)
                ygs.append(f["y2"] * z2 * _sigmoid(z2))
            yg = jnp.concatenate(ygs, axis=1)
            r = lax.rsqrt(jnp.mean(yg * yg, axis=1, keepdims=True) + EPS)
            ya_ref[:, 2 * LANES * g:2 * LANES * (g + 1)] = (yg * r * nw_ref[:, 2 * LANES * g:2 * LANES * (g + 1)]).astype(BF16)

    return pl.pallas_call(
        body, name=f"ssm_fwd_{li}", grid=(B, nc),
        in_specs=[prev, cur, zed, psb, const(3 * LCH, LCH + HALO), const(4, 2 * D), const(1, 2 * D), const(1, LANES),
                  const(1, LANES), const(1, D), const(1, D)],
        out_specs=[row, hpb],
        out_shape=[jax.ShapeDtypeStruct((T, D), BF16), jax.ShapeDtypeStruct((B, nc, NH // 2, LANES, NST), F32)],
        scratch_shapes=[pltpu.VMEM((NH // 2, LANES, NST), F32)],
        compiler_params=_cparams(("arbitrary", "arbitrary"), VMEM_LIMIT),
    )(proj, proj, proj, ps, _shift_matrices()[0], cw, cb, dtb, alog, dsk, nw)


def _ssm_bwd(proj, ps, hp, dya, cw, cb, dtb, alog, dsk, nw, S, li, comm=None):
    T = proj.shape[0]
    B = T // S
    nc, prev, cur, zed, row, psb, hpb, const = _ssm_specs(S, True)

    def body(prev_ref, cur_ref, z_ref, ps_ref, hp_ref, dy_ref, sdn_ref, sup_ref, cw_ref, cb_ref, dtb_ref, alog_ref,
             dsk_ref, nw_ref, dxbc_ref, dz_ref, dps_ref, pgw_ref, pg1_ref, pgh_ref, dh_scr, dhead, dact):
        b, cc = pl.program_id(0), pl.program_id(1)
        c = nc - 1 - cc

        @pl.when(jnp.logical_and(b == 0, cc == 0))
        def _():
            pgw_ref[...] = jnp.zeros_like(pgw_ref)
            pg1_ref[...] = jnp.zeros_like(pg1_ref)
            pgh_ref[...] = jnp.zeros_like(pgh_ref)

        @pl.when(cc == 0)
        def _():
            dh_scr[...] = jnp.zeros_like(dh_scr)
            dhead[...] = jnp.zeros_like(dhead)

        psv = ps_ref[...]
        cur16 = cur_ref[...]
        pre, sg, dt, a, acum, sh = _ssm_chunk_pre(prev_ref[...], cur16, c == 0, sdn_ref, cw_ref, cb_ref, psv,
                                                  dtb_ref[...], alog_ref[...])
        act = pre * sg
        acum_t = acum.T
        e_all = jnp.exp(acum)
        last = acum[LCH - 1:LCH, :]
        w_all = jnp.exp(last - acum)
        cd = jnp.exp(last)
        x = act[:, :D]
        lane = _lane_iota()
        m0 = lane < HD
        rowsel = lax.broadcasted_iota(jnp.int32, (LANES, 1), 0) < HD
        is_last_row = lax.broadcasted_iota(jnp.int32, (LCH, 1), 0) == LCH - 1
        dacum_all = jnp.zeros((LCH, LANES), F32)
        ddt_all = jnp.zeros((LCH, LANES), F32)
        dd_row = jnp.zeros((1, LANES), F32)
        for g in range(NGRP):
            b_g = act[:, D + NST * g:D + NST * (g + 1)].astype(BF16)
            c_g = act[:, D + NGRP * NST + NST * g:D + NGRP * NST + NST * (g + 1)].astype(BF16)
            cb_g = _dot_nt(c_g, b_g)
            pairs = (2 * g, 2 * g + 1)
            fs, hps, zs, ygs = [], [], [], []
            for p in pairs:
                hprev = hp_ref[0, 0, p]
                f = _ssm_pair_fwd(p, x, dt, acum, acum_t, e_all, w_all, cd, cb_g, b_g, c_g, hprev, dsk_ref)
                z2 = z_ref[:, LANES * p:LANES * (p + 1)].astype(F32)
                fs.append(f)
                hps.append(hprev)
                zs.append(z2)
                ygs.append(f["y2"] * z2 * _sigmoid(z2))
            gl = slice(2 * LANES * g, 2 * LANES * (g + 1))
            yg = jnp.concatenate(ygs, axis=1)
            r = lax.rsqrt(jnp.mean(yg * yg, axis=1, keepdims=True) + EPS)
            dyn = dy_ref[:, gl].astype(F32)
            gg = dyn * nw_ref[:, gl]
            dyg = r * gg - yg * (r * r * r) * jnp.mean(gg * yg, axis=1, keepdims=True)
            pg1_ref[0:1, gl] += jnp.sum(dyn * yg * r, axis=0, keepdims=True)
            dg_g = jnp.zeros((LCH, LCH), F32)
            db_g = jnp.zeros((LCH, NST), F32)
            dc_g = jnp.zeros((LCH, NST), F32)
            for idx, p in enumerate(pairs):
                f, hprev, z2 = fs[idx], hps[idx], zs[idx]
                lanes = slice(LANES * p, LANES * (p + 1))
                dyg2 = dyg[:, LANES * idx:LANES * (idx + 1)]
                sgz = _sigmoid(z2)
                dy2 = dyg2 * z2 * sgz
                dz_ref[:, lanes] = (dyg2 * f["y2"] * sgz * (1.0 + z2 * (1.0 - sgz))).astype(BF16)
                x2, dt2, xdt2, xdtb, w2, e2, z2m = f["x2"], f["dt2"], f["xdt2"], f["xdtb"], f["w2"], f["e2"], f["z2"]
                dx2 = dsk_ref[:, lanes] * dy2
                dyx = dy2 * x2
                dxdt2 = jnp.zeros((LCH, LANES), F32)
                diag_cols = []
                for hh in range(2):
                    sel = m0 if hh == 0 else jnp.logical_not(m0)
                    dyb = jnp.where(sel, dy2, 0.0).astype(BF16)
                    dm = _dot_nt(dyb, xdtb)
                    dg_g = dg_g + dm * f["lms"][hh]
                    dxdt2 = dxdt2 + _dot_tn(f["ms"][hh].astype(BF16), dyb)
                    em = dm * f["ms"][hh]
                    diag_cols.append(jnp.sum(em, axis=1, keepdims=True) - jnp.sum(em.T, axis=1, keepdims=True))
                dz2m = dy2 * e2
                t_off = dz2m * z2m
                dc_g = dc_g + _dot(dz2m.astype(BF16), hprev.astype(BF16))
                dhprev = _dot_tn(dz2m.astype(BF16), c_g)
                dhn = dh_scr[p]
                dhnb = dhn.astype(BF16)
                dhprev = dhprev + dhn * f["cdcol"]
                t_h = dhn * hprev
                dxw2 = _dot_nt(b_g, dhnb)
                db_g = db_g + _dot(f["xw"], dhnb)
                dxdt2 = dxdt2 + dxw2 * w2
                t_w = dxw2 * xdt2
                dx2 = dx2 + dxdt2 * dt2
                t_dt = dxdt2 * x2
                for hh in range(2):
                    h = 2 * p + hh
                    onehot = (lane == h).astype(F32)
                    w_col = w_all[:, h:h + 1]
                    dw_col = _head_sum(t_w, hh) * w_col
                    rs = rowsel if hh == 0 else jnp.logical_not(rowsel)
                    dlast = (jnp.sum(jnp.where(rs, t_h, 0.0), keepdims=True) * cd[:, h:h + 1]
                             + jnp.sum(dw_col, keepdims=True))
                    dacum_col = diag_cols[hh] + _head_sum(t_off, hh) - dw_col + jnp.where(is_last_row, dlast, 0.0)
                    dacum_all = dacum_all + dacum_col * onehot
                    ddt_all = ddt_all + _head_sum(t_dt, hh) * onehot
                    sel = m0 if hh == 0 else jnp.logical_not(m0)
                    dd_row = dd_row + jnp.sum(jnp.where(sel, dyx, 0.0), keepdims=True) * onehot
                dh_scr[p] = dhprev
                dact[:, lanes] = dx2
            dgb = dg_g.astype(BF16)
            dc_g = dc_g + _dot(dgb, b_g)
            db_g = db_g + _dot_tn(dgb, c_g)
            dact[:, D + NST * g:D + NST * (g + 1)] = db_g
            dact[:, D + NGRP * NST + NST * g:D + NGRP * NST + NST * (g + 1)] = dc_g
        rr = lax.broadcasted_iota(jnp.int32, (LCH, LCH), 0)
        cc2 = lax.broadcasted_iota(jnp.int32, (LCH, LCH), 1)
        dadt = _dot_hi((cc2 >= rr).astype(F32), dacum_all)
        ddt_all = ddt_all + dadt * a
        heads = lane < NH
        da = jnp.sum(dadt * dt, axis=0, keepdims=True)
        dr = jnp.where(heads, ddt_all * _sigmoid(psv + dtb_ref[...]), 0.0)
        dps_ref[...] = dr
        pgh_ref[0:1, :] += jnp.sum(dr, axis=0, keepdims=True)
        pgh_ref[1:2, :] += jnp.where(heads, da * a, 0.0)
        pgh_ref[2:3, :] += dd_row
        dpre = dact[...] * sg * (1.0 + pre * (1.0 - sg))
        extd = jnp.concatenate([dpre, dhead[...]], axis=0)
        hi = extd.astype(BF16)
        lo = (extd - hi.astype(F32)).astype(BF16)
        up = _dot(sup_ref[...], hi) + _dot(sup_ref[...], lo)
        du = cw_ref[3:4, :] * dpre
        pgw_ref[3:4, :] += jnp.sum(dpre * cur16.astype(F32), axis=0, keepdims=True)
        for d in range(1, 4):
            du = du + cw_ref[3 - d:4 - d, :] * up[LCH * (d - 1):LCH * d]
            pgw_ref[3 - d:4 - d, :] += jnp.sum(dpre * sh[LCH * (d - 1):LCH * d], axis=0, keepdims=True)
        pgw_ref[4:5, :] += jnp.sum(dpre, axis=0, keepdims=True)
        dxbc_ref[...] = du.astype(BF16)
        dhead[...] = dpre[0:HALO, :]

    xbc_out = pl.BlockSpec((LCH, 2 * D), lambda b, c: (b * nc + nc - 1 - c, 0))
    acc = lambda w: pl.BlockSpec((8, w), lambda b, c: (0, 0))
    sdn, sup = _shift_matrices()
    return _hosted_call(
        body, comm, f"ssm_bwd_{li}", (B, nc),
        in_specs=[prev, cur, zed, psb, hpb, row, const(3 * LCH, LCH + HALO), const(3 * LCH, LCH + HALO),
                  const(4, 2 * D), const(1, 2 * D), const(1, LANES), const(1, LANES), const(1, D), const(1, D)],
        out_specs=[xbc_out, row, psb, acc(2 * D), acc(D), acc(LANES)],
        out_shape=[jax.ShapeDtypeStruct((T, 2 * D), BF16), jax.ShapeDtypeStruct((T, D), BF16),
                   jax.ShapeDtypeStruct((T, LANES), F32), jax.ShapeDtypeStruct((8, 2 * D), F32),
                   jax.ShapeDtypeStruct((8, D), F32), jax.ShapeDtypeStruct((8, LANES), F32)],
        scratch=[pltpu.VMEM((NH // 2, LANES, NST), F32), pltpu.VMEM((HALO, 2 * D), F32),
                 pltpu.VMEM((LCH, 2 * D), F32)],
        dims=("arbitrary", "arbitrary"),
        operands=(proj, proj, proj, ps, hp, dya, sdn, sup, cw, cb, dtb, alog, dsk, nw))


def _lane_row(v, offset):
    return jnp.pad(v.astype(F32), (offset, LANES - offset - v.shape[0]))[None]


def _pack_rows(arrays):
    parts = []
    for a in arrays:
        flat = a.reshape(-1).astype(F32)
        pad = (-flat.shape[0]) % LANES
        parts.append(jnp.pad(flat, (0, pad)))
    flat = jnp.concatenate(parts)
    pad = (-flat.shape[0]) % (8 * LANES)
    return jnp.pad(flat, (0, pad)).reshape(-1, LANES)


def _unpack_rows(pack, shapes):
    flat = pack.reshape(-1)
    out, pos = [], 0
    for shp in shapes:
        n = math.prod(shp)
        out.append(flat[pos:pos + n].reshape(shp))
        pos += n + (-n) % LANES
    return out


def _split_w_in(w):
    main = jnp.concatenate([w[:, 0:3072], w[:, 3088:4112], w[:, 4624:5648], w[:, 5648:8720], w[:, 8736:12832],
                            w[:, 4112:4624]], axis=1)
    small = jnp.concatenate([w[:, 3072:3088], w[:, 8720:8736], jnp.zeros((D, LANES - 2 * NH), w.dtype)], axis=1)
    return main, small


def _join_w_in(dw, ds):
    xbc, az, bq, bz, cq, ck, cv, cz, gates, bk, bv = dw
    return jnp.concatenate([xbc, az, ds[:, 0:NH], bq, bk, bv, bz, cq, ck, cv, ds[:, NH:2 * NH], cz, gates], axis=1)


def kernel(x, norm_w, w_in, conv_w, conv_b, dt_bias, a_log, d_skip, ssm_norm_w, sinks, f_bias, gate_bias, w_proj, w_out, final_norm_w, loss_target, m_norm_w, m_w_in, m_conv_w, m_conv_b, m_dt_bias, m_a_log, m_d_skip, m_ssm_norm_w, m_sinks, m_f_bias, m_gate_bias, m_w_proj, m_w_out, m_final_norm_w, v_norm_w, v_w_in, v_conv_w, v_conv_b, v_dt_bias, v_a_log, v_d_skip, v_ssm_norm_w, v_sinks, v_f_bias, v_gate_bias, v_w_proj, v_w_out, v_final_norm_w):
    Bl, S, _ = x.shape
    T = Bl * S
    depth = norm_w.shape[0]
    me = 4 * lax.axis_index("x") + 2 * lax.axis_index("y") + lax.axis_index("c")
    csh, gsh = conv_w.shape[2], gate_bias.shape[2]

    def gather_plan(l):
        small = jnp.concatenate([conv_w[l].reshape(-1), gate_bias[l].reshape(-1)]).reshape(-1, LANES)
        return _Comm("gather", [w_in[l].astype(BF16), w_proj[l].astype(BF16), w_out[l].astype(BF16), small])

    def unpack_weights(res):
        g_win, g_wp, g_wo, g_small = res
        flat = g_small.reshape(NDEV, -1)
        return (g_win.transpose(1, 0, 2).reshape(D, NIN),
                g_wp.transpose(1, 0, 2, 3).reshape(3, D, D),
                g_wo.reshape(D, D),
                flat[:, :4 * csh].reshape(NDEV, 4, csh).transpose(1, 0, 2).reshape(4, 2 * D),
                flat[:, 4 * csh:].reshape(NDEV, 3, gsh).transpose(1, 0, 2).reshape(3, D))

    def scatter_plan(gw_in, gw_p=None, gw_o=None):
        arrays = [gw_in.astype(BF16).reshape(-1, NDEV, NSH).transpose(1, 0, 2)]
        if gw_p is not None:
            arrays += [gw_p.astype(BF16).reshape(3, NDEV, D // NDEV, D).transpose(1, 0, 2, 3),
                       gw_o.astype(BF16).reshape(NDEV, D // NDEV, D)]
        return _Comm("scatter", arrays)

    pos = jnp.arange(S, dtype=F32)
    inv_freq = ROPE_THETA ** (-jnp.arange(0, HD, 2, dtype=F32) / HD)
    ang = pos[:, None] * inv_freq[None, :]
    cos128 = jnp.tile(jnp.cos(ang), (1, 4))
    sign = jnp.where((jnp.arange(LANES) % HD) < HD // 2, -1.0, 1.0).astype(F32)
    sin128 = jnp.tile(jnp.sin(ang), (1, 4)) * sign[None, :]

    bq, nq = _fox_blocks(S)
    x2 = x.reshape(T, D)
    tgt2 = loss_target.reshape(T, D)

    saved = []
    xcur = x2
    weights = [None] * depth
    weights[0] = unpack_weights(_gather_two_level(gather_plan(0).arrays, "gather_weights_0"))
    for l in range(depth):
        win_l, wp_l, wo_l, cw_l, gb_l = weights[l]
        wmain, wsmall = _split_w_in(win_l)
        comm = gather_plan(l + 1) if l + 1 < depth else None
        res = _inproj_fwd(xcur, norm_w[l][None], wmain, wsmall, cos128, sin128, S, l, comm)
        proj, ps, h_t = res[:3]
        if comm is not None:
            weights[l + 1] = unpack_weights(res[3:])
        dtb = _lane_row(dt_bias[l], 0)
        alog = _lane_row(a_log[l], 0)
        fb = _lane_row(f_bias[l], NH)
        dsk = jnp.repeat(d_skip[l], HD)[None]
        ya, hp = _ssm_fwd(proj, ps, cw_l, conv_b[l][None], dtb, alog, dsk, ssm_norm_w[l][None], S, l)
        yb, ob, lse_b = _swa_fwd(proj, sinks[l], S, l)
        cum = _fox_cum(ps, fb, S, l)
        cumh = cum[:, NH:2 * NH].reshape(Bl, S, NH).transpose(0, 2, 1)
        cum_col = cumh[..., None]
        yc, oc, lse_c = _fox_fwd(proj, cum_col, S, l)
        xnext, br, y_t = _merge_fwd(ya, yb, yc, proj, gb_l, wp_l, wo_l, xcur, l)
        saved.append(dict(x=xcur, wmain=wmain, wsmall=wsmall, proj=proj, ps=ps, h_t=h_t, dtb=dtb, alog=alog, fb=fb,
                          dsk=dsk, hp=hp, ob=ob, lse_b=lse_b, cum_col=cum_col, oc=oc, lse_c=lse_c, br=br, y_t=y_t))
        xcur = xnext

    dx, dx16, st = _final_loss(xcur, tgt2, final_norm_w[None])
    loss_part = st[2, 0]
    g_final = st[0]

    gsm = {k: [None] * depth for k in ("norm_w", "conv_w", "conv_b", "dt_bias", "a_log", "d_skip", "ssm_norm_w",
                                      "sinks", "f_bias", "gate_bias")}
    parts = [None] * depth
    pending = None
    for l in reversed(range(depth)):
        sv = saved[l]
        proj, ps = sv["proj"], sv["ps"]
        _, wp_l, wo_l, cw_l, gb_l = weights[l]
        dbr, dgates, merged_t, dgb, dy_a, do_b, dbz, do_c, dcz = _merge_bwd(dx16, wo_l, wp_l, sv["br"], proj, gb_l,
                                                                            sv["ob"], sv["oc"], l)
        g_wo = _matmul(merged_t, dx16, F32, f"dwout_{l}")
        g_wp = _matmul_batched(sv["y_t"], dbr, F32, f"dwproj_{l}")
        gsm["gate_bias"][l] = dgb[0:3]
        res = _ssm_bwd(proj, ps, sv["hp"], dy_a, cw_l, conv_b[l][None], sv["dtb"], sv["alog"], sv["dsk"],
                       ssm_norm_w[l][None], S, l, pending)
        dxbc, daz, dps_a, pgw, pg1, pgh = res[:6]
        if pending is not None:
            parts[l + 1] = res[6:]
        gsm["conv_w"][l], gsm["conv_b"][l] = pgw[0:4], pgw[4]
        gsm["ssm_norm_w"][l] = pg1[0]
        gsm["dt_bias"][l], gsm["a_log"][l], gsm["d_skip"][l] = pgh[0, :NH], pgh[1, :NH], pgh[2, :NH]
        dq_b, dsk_b = _swa_bwd_dq(proj, do_b, sv["ob"], sv["lse_b"], sinks[l], cos128, sin128, S, l)
        dk_b, dv_b = _swa_bwd_dkv(proj, do_b, sv["ob"], sv["lse_b"], cos128, sin128, S, l)
        gsm["sinks"][l] = dsk_b[:, :, 0].reshape(NH)
        dq_c, dk_c, dv_c, dcum_k, dcum_q = _fox_bwd(proj, do_c, sv["oc"], sv["cum_col"], sv["lse_c"], S, l)
        dcum_tm = (dcum_k.reshape(Bl, NH, S) + dcum_q.reshape(Bl, NH, S)).transpose(0, 2, 1).reshape(T, NH)
        dcum_pad = jnp.pad(dcum_tm, ((0, 0), (NH, LANES - 2 * NH)))
        df, dfb = _fox_cum_bwd(dcum_pad, ps, sv["fb"], S, l)
        gsm["f_bias"][l] = dfb[0, NH:2 * NH]
        dps16 = (dps_a + df).astype(BF16)
        pieces = (dxbc, daz, dq_b, dbz, dq_c, dk_c, dv_c, dcz, dgates, dk_b, dv_b)
        dw_pieces = [_matmul(sv["h_t"], pc, F32, f"dwin_{l}_{i}") for i, pc in enumerate(pieces)]
        dws = _matmul(sv["h_t"], dps16, F32, f"dwin_small_{l}")
        g_win = _join_w_in(dw_pieces, dws)
        if l == 0:
            plans = [scatter_plan(g_win[r0:r1], *((g_wp, g_wo) if r0 == 0 else ())) for r0, r1 in ROW_CHUNKS]
        else:
            plans, pending = [None] * len(ROW_CHUNKS), scatter_plan(g_win, g_wp, g_wo)
        dkv_b = jnp.concatenate([dk_b, dv_b], axis=1)
        res1 = _inproj_bwd_dx([(dxbc, OFF_XBC), (daz, OFF_AZ), (dq_b, OFF_BQ), (dbz, OFF_BZ)], sv["wmain"],
                              ("narrow", dps16, sv["wsmall"]), None, f"inproj_bwd_dh1_{l}", plans[0])
        res2 = _inproj_bwd_dx([(dq_c, OFF_CQ), (dk_c, OFF_CK), (dv_c, OFF_CV), (dcz, OFF_CZ)], sv["wmain"],
                              ("acc", res1[0]), None, f"inproj_bwd_dh2_{l}", plans[1])
        res3 = _inproj_bwd_dx([(dgates, OFF_G), (dkv_b, OFF_BK)], sv["wmain"], ("acc", res2[0]),
                              (sv["x"], norm_w[l][None], dx), f"inproj_bwd_dx_{l}", plans[2])
        dx, dx16, dnw = res3[:3]
        if l == 0:
            parts[0] = [jnp.concatenate([res1[1], res2[1], res3[3]], axis=1), res1[2], res1[3]]
        gsm["norm_w"][l] = dnw[0]

    big = {}
    for idx, (name, w, m, v) in enumerate((("w_in", w_in, m_w_in, v_w_in), ("w_proj", w_proj, m_w_proj, v_w_proj),
                                          ("w_out", w_out, m_w_out, v_w_out))):
        cols = w.shape[-1]
        res = _sum_adamw([parts[l][idx].reshape(NDEV, -1, cols) for l in range(depth)], w.reshape(depth, -1, cols),
                         m.reshape(depth, -1, cols), v.reshape(depth, -1, cols), f"adamw_{name}")
        big[name] = [r.reshape(w.shape) for r in res]

    small_names = ("norm_w", "conv_b", "dt_bias", "a_log", "d_skip", "ssm_norm_w", "sinks", "f_bias")
    small_parts = [jnp.stack(gsm[k]) for k in small_names] + [g_final, jnp.stack(gsm["conv_w"]),
                                                              jnp.stack(gsm["gate_bias"]), loss_part.reshape(1)]
    shapes = [a.shape for a in small_parts]
    summed = _unpack_rows(_all_reduce_small(_pack_rows(small_parts)), shapes)
    g_small = dict(zip(small_names, summed[:len(small_names)]))
    g_small["final_norm_w"] = summed[len(small_names)]
    g_small["conv_w"] = lax.dynamic_slice_in_dim(summed[len(small_names) + 1], me * csh, csh, axis=2)
    g_small["gate_bias"] = lax.dynamic_slice_in_dim(summed[len(small_names) + 2], me * gsh, gsh, axis=2)
    loss = summed[len(small_names) + 3][0]

    ws = dict(norm_w=norm_w, conv_w=conv_w, conv_b=conv_b, dt_bias=dt_bias, a_log=a_log, d_skip=d_skip,
              ssm_norm_w=ssm_norm_w, sinks=sinks, f_bias=f_bias, gate_bias=gate_bias, final_norm_w=final_norm_w)
    ms = dict(norm_w=m_norm_w, conv_w=m_conv_w, conv_b=m_conv_b, dt_bias=m_dt_bias, a_log=m_a_log, d_skip=m_d_skip,
              ssm_norm_w=m_ssm_norm_w, sinks=m_sinks, f_bias=m_f_bias, gate_bias=m_gate_bias,
              final_norm_w=m_final_norm_w)
    vs = dict(norm_w=v_norm_w, conv_w=v_conv_w, conv_b=v_conv_b, dt_bias=v_dt_bias, a_log=v_a_log, d_skip=v_d_skip,
              ssm_norm_w=v_ssm_norm_w, sinks=v_sinks, f_bias=v_f_bias, gate_bias=v_gate_bias,
              final_norm_w=v_final_norm_w)
    order = list(ws)
    oshapes = [ws[k].shape for k in order]
    res = _adamw_small(_pack_rows([g_small[k] for k in order]), _pack_rows([ws[k] for k in order]),
                       _pack_rows([ms[k] for k in order]), _pack_rows([vs[k] for k in order]))
    d_s, m_s, v_s = (dict(zip(order, _unpack_rows(r, oshapes))) for r in res)

    names = ("norm_w", "w_in", "conv_w", "conv_b", "dt_bias", "a_log", "d_skip", "ssm_norm_w", "sinks", "f_bias",
             "gate_bias", "w_proj", "w_out", "final_norm_w")
    grads, deltas, new_m, new_v = [], [], [], []
    for k in names:
        if k in big:
            g, d_, m_, v_ = big[k]
        else:
            g, d_, m_, v_ = g_small[k], d_s[k], m_s[k], v_s[k]
        grads.append(g)
        deltas.append(d_)
        new_m.append(m_)
        new_v.append(v_)
    return (loss, dx.reshape(Bl, S, D), *grads, *deltas, *new_m, *new_v)
```

```python
import functools
import math

import jax
import jax.numpy as jnp
from jax import lax
from jax.experimental import pallas as pl
from jax.experimental.pallas import tpu as pltpu

F32 = jnp.float32
BF16 = jnp.bfloat16
MESH = pl.DeviceIdType.MESH
NDEV = 8

D = 1024
NH = 16
HD = 64
NST = 128
NGRP = 4
LCH = 128
EPS = 1e-6
ROPE_THETA = 10000.0
SCALE = HD ** -0.5
NEG = -1e30

LANES = 128
VMEM_LIMIT = 56 * 1024 * 1024

OFF_XBC, OFF_AZ, OFF_BQ, OFF_BZ, OFF_CQ, OFF_CK, OFF_CV, OFF_CZ, OFF_G, OFF_BK, OFF_BV = (
    0, 2048, 3072, 4096, 5120, 6144, 7168, 8192, 9216, 12288, 12544)
NMAIN = 12800
NIN = 12832
NSH = NIN // NDEV

ROW_CHUNKS = ((0, 384), (384, 768), (768, 1024))

ADAM_LR, ADAM_B1, ADAM_B2, ADAM_EPS, ADAM_WD, ADAM_STEP = 0.001, 0.9, 0.999, 1e-08, 0.01, 10


def _cparams(dims=None, vmem=None):
    return pltpu.CompilerParams(dimension_semantics=dims, vmem_limit_bytes=vmem)


def _dot(a, b):
    return jnp.dot(a, b, preferred_element_type=F32)


def _dot_nt(a, b):
    return lax.dot_general(a, b, (((1,), (1,)), ((), ())), preferred_element_type=F32)


def _dot_tn(a, b):
    return lax.dot_general(a, b, (((0,), (0,)), ((), ())), preferred_element_type=F32)


def _dot_hi(a, b):
    return jnp.dot(a, b, precision=lax.Precision.HIGHEST, preferred_element_type=F32)


def _sigmoid(x):
    return 1.0 / (1.0 + jnp.exp(-x))


def _softplus(x):
    return jnp.maximum(x, 0.0) + jnp.log(1.0 + jnp.exp(-jnp.abs(x)))


def _lane_iota(n=LANES):
    return lax.broadcasted_iota(jnp.int32, (1, n), 1)


def _rot_half(x):
    first = (_lane_iota() % HD) < (HD // 2)
    return jnp.where(first, pltpu.roll(x, LANES - HD // 2, 1), pltpu.roll(x, HD // 2, 1))


def _head_sum(x, head):
    m = (_lane_iota() < HD) if head == 0 else (_lane_iota() >= HD)
    return jnp.sum(jnp.where(m, x, 0.0), axis=1, keepdims=True)


def _me_and_peers():
    x, y, c = lax.axis_index("x"), lax.axis_index("y"), lax.axis_index("c")
    me = 4 * x + 2 * y + c
    peers = []
    for k in range(1, NDEV):
        kx, ky, kc = (k >> 2) & 1, (k >> 1) & 1, k & 1
        px, py, pc = x ^ kx, y ^ ky, c ^ kc
        peers.append(((px, py, pc), 4 * px + 2 * py + pc))
    return me, peers


class _Comm:
    def __init__(self, kind, arrays):
        self.kind, self.arrays, self.n = kind, list(arrays), len(arrays)
        any_spec = pl.BlockSpec(memory_space=pl.ANY)
        self.in_specs = [any_spec] * self.n
        self.out_specs = [any_spec] * self.n
        self.out_shape = [jax.ShapeDtypeStruct(((NDEV,) + a.shape) if kind == "gather" else a.shape, a.dtype)
                          for a in self.arrays]
        self.scratch = [pltpu.SemaphoreType.DMA((self.n, NDEV - 1)), pltpu.SemaphoreType.DMA((self.n, NDEV - 1)),
                        pltpu.SemaphoreType.DMA((self.n,))]

    def copies(self, ins, outs, sems):
        send_sems, recv_sems, local_sems = sems
        me, peers = _me_and_peers()
        out = []
        for a in range(self.n):
            mine = ins[a] if self.kind == "gather" else ins[a].at[me]
            out.append(pltpu.make_async_copy(mine, outs[a].at[me], local_sems.at[a]))
            for k, (peer, pidx) in enumerate(peers):
                src = ins[a] if self.kind == "gather" else ins[a].at[pidx]
                out.append(pltpu.make_async_remote_copy(
                    src_ref=src, dst_ref=outs[a].at[me], send_sem=send_sems.at[a, k], recv_sem=recv_sems.at[a, k],
                    device_id=peer, device_id_type=MESH))
        return out

    def call(self, name):
        def body(*refs):
            cps = self.copies(refs[:self.n], refs[self.n:2 * self.n], refs[2 * self.n:])
            for cp in cps:
                cp.start()
            for cp in cps:
                cp.wait()

        return pl.pallas_call(body, name=name, out_shape=self.out_shape, in_specs=self.in_specs,
                              out_specs=self.out_specs, scratch_shapes=self.scratch)(*self.arrays)


def _gather_two_level(arrays, name):
    n = len(arrays)

    def body(*refs):
        ins, outs = refs[:n], refs[n:2 * n]
        send_sems, recv_sems, local_sems = refs[2 * n:]
        x, y, c = lax.axis_index("x"), lax.axis_index("y"), lax.axis_index("c")
        me, sibling = (x, y, c), (x, y, 1 - c)
        chips = [(1 - x, y), (x, 1 - y), (1 - x, 1 - y)]

        def slot(a, dev):
            return outs[a].at[4 * dev[0] + 2 * dev[1] + dev[2]]

        def copy(a, k, block, to, src=None):
            return pltpu.make_async_remote_copy(
                src_ref=slot(a, block) if src is None else src, dst_ref=slot(a, block),
                send_sem=send_sems.at[a, k], recv_sem=recv_sems.at[a, k], device_id=to, device_id_type=MESH)

        mine = [pltpu.make_async_copy(ins[a], slot(a, me), local_sems.at[a]) for a in range(n)]
        for cp in mine:
            cp.start()
        first = []
        for a in range(n):
            first.append(copy(a, 0, me, sibling, src=ins[a]))
            first += [copy(a, 1 + j, me, (*chip, c), src=ins[a]) for j, chip in enumerate(chips)]
        for cp in first:
            cp.start()
        passed = []
        for j, chip in enumerate(chips):
            for a in range(n):
                copy(a, 1 + j, (*chip, c), me).wait_recv()
                fwd = copy(a, 4 + j, (*chip, c), sibling)
                fwd.start()
                passed.append(fwd)
        for a in range(n):
            copy(a, 0, sibling, me).wait_recv()
            for j, chip in enumerate(chips):
                copy(a, 4 + j, (*chip, 1 - c), me).wait_recv()
        for cp in first + passed:
            cp.wait_send()
        for cp in mine:
            cp.wait()

    any_spec = pl.BlockSpec(memory_space=pl.ANY)
    return pl.pallas_call(
        body, name=name, out_shape=[jax.ShapeDtypeStruct((NDEV,) + a.shape, a.dtype) for a in arrays],
        in_specs=[any_spec] * n, out_specs=[any_spec] * n,
        scratch_shapes=[pltpu.SemaphoreType.DMA((n, NDEV - 1)), pltpu.SemaphoreType.DMA((n, NDEV - 1)),
                        pltpu.SemaphoreType.DMA((n,))])(*arrays)


def _hosted_call(body, comm, name, grid, in_specs, out_specs, out_shape, scratch, dims, operands):
    if comm is None:
        return pl.pallas_call(body, name=name, grid=grid, in_specs=in_specs, out_specs=out_specs, out_shape=out_shape,
                              scratch_shapes=scratch, compiler_params=_cparams(dims, VMEM_LIMIT))(*operands)
    n_in, n_out, n_scr, n = len(in_specs), len(out_specs), len(scratch), comm.n

    def hosted(*refs):
        hin, cin = refs[:n_in], refs[n_in:n_in + n]
        hout = refs[n_in + n:n_in + n + n_out]
        cout = refs[n_in + n + n_out:n_in + 2 * n + n_out]
        hscr = refs[n_in + 2 * n + n_out:n_in + 2 * n + n_out + n_scr]
        sems = refs[n_in + 2 * n + n_out + n_scr:]
        ids = [pl.program_id(a) for a in range(len(grid))]
        first = functools.reduce(jnp.logical_and, [i == 0 for i in ids])
        last = functools.reduce(jnp.logical_and, [i == g - 1 for i, g in zip(ids, grid)])

        @pl.when(first)
        def _():
            for cp in comm.copies(cin, cout, sems):
                cp.start()

        body(*hin, *hout, *hscr)

        @pl.when(last)
        def _():
            for cp in comm.copies(cin, cout, sems):
                cp.wait()

    return pl.pallas_call(
        hosted, name=name, grid=grid, in_specs=list(in_specs) + comm.in_specs,
        out_specs=list(out_specs) + comm.out_specs, out_shape=list(out_shape) + comm.out_shape,
        scratch_shapes=list(scratch) + comm.scratch,
        compiler_params=_cparams(("arbitrary",) * len(grid), VMEM_LIMIT))(*operands, *comm.arrays)


def _all_reduce_small(v):
    rows = v.shape[0]

    def body(v_ref, sum_ref, all_ref, send_sems, recv_sems):
        me, peers = _me_and_peers()
        all_ref[me] = v_ref[...]
        copies = []
        for k, (peer, _) in enumerate(peers):
            cp = pltpu.make_async_remote_copy(
                src_ref=v_ref, dst_ref=all_ref.at[me],
                send_sem=send_sems.at[k], recv_sem=recv_sems.at[k],
                device_id=peer, device_id_type=MESH)
            cp.start()
            copies.append(cp)
        for cp in copies:
            cp.wait()
        acc = all_ref[0]
        for d in range(1, NDEV):
            acc = acc + all_ref[d]
        sum_ref[...] = acc

    vm = pl.BlockSpec(memory_space=pltpu.VMEM)
    return pl.pallas_call(
        body, name="all_reduce_small",
        out_shape=jax.ShapeDtypeStruct((rows, LANES), F32),
        in_specs=[vm], out_specs=vm,
        scratch_shapes=[pltpu.VMEM((NDEV, rows, LANES), F32),
                        pltpu.SemaphoreType.DMA((NDEV - 1,)), pltpu.SemaphoreType.DMA((NDEV - 1,))],
    )(v)


def _adamw_math(w, g, m, v):
    m = ADAM_B1 * m + (1.0 - ADAM_B1) * g
    v = ADAM_B2 * v + (1.0 - ADAM_B2) * jnp.square(g)
    m_hat = m / (1.0 - ADAM_B1 ** ADAM_STEP)
    v_hat = v / (1.0 - ADAM_B2 ** ADAM_STEP)
    delta = -ADAM_LR * (m_hat / (jnp.sqrt(v_hat) + ADAM_EPS) + ADAM_WD * w)
    return delta, m, v


def _sum_adamw(parts, w, m, v, name):
    depth, rows, cols = w.shape
    tr = next(c for c in (256, 128, 64, 32, 16) if rows % c == 0)
    nb = rows // tr

    def body(*refs):
        p_refs, (w_ref, m_ref, v_ref, g_ref, d_ref, nm_ref, nv_ref) = refs[:depth], refs[depth:]
        l = pl.program_id(0)
        for ll in range(depth):
            @pl.when(l == ll)
            def _(ll=ll):
                g = p_refs[ll][0].astype(F32)
                for d in range(1, NDEV):
                    g = g + p_refs[ll][d].astype(F32)
                delta, nm, nv = _adamw_math(w_ref[0], g, m_ref[0], v_ref[0])
                g_ref[0] = g
                d_ref[0] = delta
                nm_ref[0] = nm
                nv_ref[0] = nv

    part = lambda ll: pl.BlockSpec((NDEV, tr, cols), lambda l, i, ll=ll: (0, jnp.where(l == ll, i, jnp.where(l < ll, 0, nb - 1)), 0))
    blk = pl.BlockSpec((1, tr, cols), lambda l, i: (l, i, 0))
    sds = jax.ShapeDtypeStruct((depth, rows, cols), F32)
    return pl.pallas_call(
        body, name=name, grid=(depth, nb),
        in_specs=[part(ll) for ll in range(depth)] + [blk, blk, blk],
        out_specs=[blk, blk, blk, blk], out_shape=[sds, sds, sds, sds],
        compiler_params=_cparams(("arbitrary", "arbitrary"), VMEM_LIMIT),
    )(*parts, w, m, v)


def _adamw_small(g, w, m, v):
    def body(g_ref, w_ref, m_ref, v_ref, d_ref, nm_ref, nv_ref):
        delta, nm, nv = _adamw_math(w_ref[...], g_ref[...], m_ref[...], v_ref[...])
        d_ref[...] = delta
        nm_ref[...] = nm
        nv_ref[...] = nv

    sds = jax.ShapeDtypeStruct(g.shape, F32)
    return pl.pallas_call(body, name="adamw_small", out_shape=[sds, sds, sds])(g, w, m, v)


def _matmul(a, b, out_dtype, name, tm=1024, tn=1024, tk=1024):
    M, K = a.shape
    N = b.shape[1]
    tm, tn, tk = min(tm, M), min(tn, N), min(tk, K)
    nk = K // tk

    def body(a_ref, b_ref, o_ref, acc):
        k = pl.program_id(2)

        @pl.when(k == 0)
        def _():
            acc[...] = jnp.zeros_like(acc)

        acc[...] += _dot(a_ref[...], b_ref[...])

        @pl.when(k == nk - 1)
        def _():
            o_ref[...] = acc[...].astype(out_dtype)

    return pl.pallas_call(
        body, name=name, grid=(M // tm, N // tn, nk),
        in_specs=[pl.BlockSpec((tm, tk), lambda i, j, k: (i, k)), pl.BlockSpec((tk, tn), lambda i, j, k: (k, j))],
        out_specs=pl.BlockSpec((tm, tn), lambda i, j, k: (i, j)),
        out_shape=jax.ShapeDtypeStruct((M, N), out_dtype),
        scratch_shapes=[pltpu.VMEM((tm, tn), F32)],
        compiler_params=_cparams(("parallel", "parallel", "arbitrary"), VMEM_LIMIT),
    )(a, b)


def _matmul_batched(a, b, out_dtype, name, tm=1024, tn=1024, tk=512):
    G, M, K = a.shape
    N = b.shape[2]
    tm, tn, tk = min(tm, M), min(tn, N), min(tk, K)
    nk = K // tk

    def body(a_ref, b_ref, o_ref, acc):
        k = pl.program_id(3)

        @pl.when(k == 0)
        def _():
            acc[...] = jnp.zeros_like(acc)

        acc[...] += _dot(a_ref[0], b_ref[0])

        @pl.when(k == nk - 1)
        def _():
            o_ref[0] = acc[...].astype(out_dtype)

    return pl.pallas_call(
        body, name=name, grid=(G, M // tm, N // tn, nk),
        in_specs=[pl.BlockSpec((1, tm, tk), lambda g, i, j, k: (g, i, k)),
                  pl.BlockSpec((1, tk, tn), lambda g, i, j, k: (g, k, j))],
        out_specs=pl.BlockSpec((1, tm, tn), lambda g, i, j, k: (g, i, j)),
        out_shape=jax.ShapeDtypeStruct((G, M, N), out_dtype),
        scratch_shapes=[pltpu.VMEM((tm, tn), F32)],
        compiler_params=_cparams(("parallel", "parallel", "parallel", "arbitrary"), VMEM_LIMIT),
    )(a, b)


def _inproj_fwd(x2, nw, wmain, wsmall, cos128, sin128, S, li, comm=None):
    T = x2.shape[0]
    tm, tn = min(2048, S), 512
    nj, npos = NMAIN // tn, S // tm
    jq0, jk = OFF_BQ // tn, OFF_BK // tn

    def body(x_ref, nw_ref, w_ref, ws_ref, cos_ref, sin_ref, proj_ref, ps_ref, ht_ref, h_scr):
        j = pl.program_id(1)

        @pl.when(j == 0)
        def _():
            x = x_ref[...]
            r = lax.rsqrt(jnp.mean(x * x, axis=-1, keepdims=True) + EPS)
            h = (x * r * nw_ref[...]).astype(BF16)
            h_scr[...] = h
            ht_ref[...] = h.T
            ps_ref[...] = _dot(h, ws_ref[...])

        acc = _dot(h_scr[...], w_ref[...])

        def roped(c):
            xc = acc[:, LANES * c:LANES * (c + 1)]
            return (xc * cos_ref[...] + _rot_half(xc) * sin_ref[...]).astype(BF16)

        def plain(c):
            return acc[:, LANES * c:LANES * (c + 1)].astype(BF16)

        is_q = jnp.logical_or(j == jq0, j == jq0 + 1)
        is_k = j == jk

        @pl.when(is_q)
        def _():
            for c in range(4):
                proj_ref[:, LANES * c:LANES * (c + 1)] = roped(c)

        @pl.when(is_k)
        def _():
            for c in range(4):
                proj_ref[:, LANES * c:LANES * (c + 1)] = roped(c) if c < 2 else plain(c)

        @pl.when(jnp.logical_not(jnp.logical_or(is_q, is_k)))
        def _():
            proj_ref[...] = acc.astype(BF16)

    return _hosted_call(
        body, comm, f"inproj_fwd_{li}", (T // tm, nj),
        in_specs=[pl.BlockSpec((tm, D), lambda i, j: (i, 0)),
                  pl.BlockSpec((1, D), lambda i, j: (0, 0)),
                  pl.BlockSpec((D, tn), lambda i, j: (0, j)),
                  pl.BlockSpec((D, LANES), lambda i, j: (0, 0)),
                  pl.BlockSpec((tm, LANES), lambda i, j: (i % npos, 0)),
                  pl.BlockSpec((tm, LANES), lambda i, j: (i % npos, 0))],
        out_specs=[pl.BlockSpec((tm, tn), lambda i, j: (i, j)),
                   pl.BlockSpec((tm, LANES), lambda i, j: (i, 0)),
                   pl.BlockSpec((D, tm), lambda i, j: (0, i))],
        out_shape=[jax.ShapeDtypeStruct((T, NMAIN), BF16), jax.ShapeDtypeStruct((T, LANES), F32),
                   jax.ShapeDtypeStruct((D, T), BF16)],
        scratch=[pltpu.VMEM((tm, D), BF16)], dims=("parallel", "arbitrary"),
        operands=(x2, nw, wmain, wsmall, cos128, sin128))


def _inproj_bwd_dx(segs, wmain, init, final, name, comm=None):
    T = segs[0][0].shape[0]
    tm = min(1024, T)
    tk = 1024 if all(a.shape[1] % 1024 == 0 and c % 1024 == 0 for a, c in segs) else 512
    ni = T // tm
    k0s, nks, c0s = [], [], []
    for arr, col0 in segs:
        k0s.append(sum(nks))
        nks.append(arr.shape[1] // tk)
        c0s.append(col0 // tk)
    nk = sum(nks)
    ns = len(segs)

    def in_range(k, s):
        return jnp.logical_and(k >= k0s[s], k < k0s[s] + nks[s])

    def wcol(i, k):
        g = 0
        for s in range(ns):
            g = g + jnp.where(in_range(k, s), c0s[s] + k - k0s[s], 0)
        return (0, g)

    n_init = 2 if init[0] == "narrow" else 1

    def body(*refs):
        seg_refs, w_ref = refs[:ns], refs[ns]
        init_refs = refs[ns + 1:ns + 1 + n_init]
        rest = refs[ns + 1 + n_init:]
        i, k = pl.program_id(0), pl.program_id(1)
        acc = rest[-1]

        @pl.when(k == 0)
        def _():
            if init[0] == "narrow":
                acc[...] = _dot_nt(init_refs[0][...], init_refs[1][...])
            else:
                acc[...] = init_refs[0][...]

        for s in range(ns):
            @pl.when(in_range(k, s))
            def _(s=s):
                acc[...] += _dot_nt(seg_refs[s][...], w_ref[...])

        if final is None:
            @pl.when(k == nk - 1)
            def _():
                rest[0][...] = acc[...]
        else:
            x_ref, nw_ref, dxo_ref, dx_ref, dx16_ref, dnw_ref = rest[:6]

            @pl.when(jnp.logical_and(i == 0, k == 0))
            def _():
                dnw_ref[...] = jnp.zeros_like(dnw_ref)

            @pl.when(k == nk - 1)
            def _():
                x = x_ref[...]
                r = lax.rsqrt(jnp.mean(x * x, axis=-1, keepdims=True) + EPS)
                dh = acc[...]
                g = dh * nw_ref[...]
                dx = dxo_ref[...] + r * g - x * (r * r * r) * jnp.mean(g * x, axis=-1, keepdims=True)
                dx_ref[...] = dx
                dx16_ref[...] = dx.astype(BF16)
                dnw_ref[0:1, :] += jnp.sum(dh * x * r, axis=0, keepdims=True)

    row = pl.BlockSpec((tm, D), lambda i, k: (i, 0))
    in_specs = [pl.BlockSpec((tm, tk), lambda i, k, s=s: (i, jnp.clip(k - k0s[s], 0, nks[s] - 1))) for s in range(ns)]
    in_specs.append(pl.BlockSpec((D, tk), wcol))
    operands = [a for a, _ in segs] + [wmain]
    if init[0] == "narrow":
        in_specs += [pl.BlockSpec((tm, LANES), lambda i, k: (i, 0)), pl.BlockSpec((D, LANES), lambda i, k: (0, 0))]
    else:
        in_specs.append(row)
    operands += list(init[1:])
    if final is None:
        out_specs, out_shape = [row], [jax.ShapeDtypeStruct((T, D), F32)]
    else:
        in_specs += [row, pl.BlockSpec((1, D), lambda i, k: (0, 0)), row]
        operands += list(final)
        out_specs = [row, row, pl.BlockSpec((8, D), lambda i, k: (0, 0))]
        out_shape = [jax.ShapeDtypeStruct((T, D), F32), jax.ShapeDtypeStruct((T, D), BF16),
                     jax.ShapeDtypeStruct((8, D), F32)]
    return _hosted_call(body, comm, name, (ni, nk), in_specs=in_specs, out_specs=out_specs, out_shape=out_shape,
                        scratch=[pltpu.VMEM((tm, D), F32)], dims=("arbitrary", "arbitrary"), operands=tuple(operands))


def _merge_fwd(ya, yb, yc, proj, gbias, wp, wout, x2, li):
    T = x2.shape[0]
    tm = min(512, T)
    gcol = OFF_G // D

    def body(ya_ref, yb_ref, yc_ref, g0_ref, g1_ref, g2_ref, gb_ref, wp_ref, wo_ref, x_ref, xn_ref, br_ref, yt_ref):
        merged = jnp.zeros((tm, D), F32)
        for i, (y_ref, g_ref) in enumerate(((ya_ref, g0_ref), (yb_ref, g1_ref), (yc_ref, g2_ref))):
            y = y_ref[...]
            yt_ref[i] = y.T
            br = _dot(y, wp_ref[i])
            br_ref[i] = br.astype(BF16)
            gate = _sigmoid(g_ref[...].astype(F32) + gb_ref[i:i + 1, :])
            merged = merged + gate * br
        xn_ref[...] = x_ref[...] + _dot(merged.astype(BF16), wo_ref[...])

    row = lambda c: pl.BlockSpec((tm, D), lambda i, c=c: (i, c))
    return pl.pallas_call(
        body, name=f"merge_fwd_{li}", grid=(T // tm,),
        in_specs=[row(0), row(0), row(0), row(gcol), row(gcol + 1), row(gcol + 2),
                  pl.BlockSpec((3, D), lambda i: (0, 0)),
                  pl.BlockSpec((3, D, D), lambda i: (0, 0, 0)),
                  pl.BlockSpec((D, D), lambda i: (0, 0)),
                  row(0)],
        out_specs=[row(0), pl.BlockSpec((3, tm, D), lambda i: (0, i, 0)), pl.BlockSpec((3, D, tm), lambda i: (0, 0, i))],
        out_shape=[jax.ShapeDtypeStruct((T, D), F32), jax.ShapeDtypeStruct((3, T, D), BF16),
                   jax.ShapeDtypeStruct((3, D, T), BF16)],
        compiler_params=_cparams(("parallel",), VMEM_LIMIT),
    )(ya, yb, yc, proj, proj, proj, gbias, wp, wout, x2)


def _merge_bwd(dxo16, wout, wp, br, proj, gbias, ob, oc, li):
    T = dxo16.shape[0]
    tm = min(256, T)
    gcol = OFF_G // D

    def body(dx_ref, wo_ref, wp_ref, br_ref, g0_ref, g1_ref, g2_ref, gb_ref, ob_ref, oc_ref, zb_ref, zc_ref,
             dbr_ref, dg_ref, mt_ref, dgb_ref, dya_ref, dob_ref, dzb_ref, doc_ref, dzc_ref):
        @pl.when(pl.program_id(0) == 0)
        def _():
            dgb_ref[...] = jnp.zeros_like(dgb_ref)

        dm = _dot_nt(dx_ref[...], wo_ref[...])
        merged = jnp.zeros((tm, D), F32)
        dys = []
        for i, g_ref in enumerate((g0_ref, g1_ref, g2_ref)):
            b = br_ref[i].astype(F32)
            gate = _sigmoid(g_ref[...].astype(F32) + gb_ref[i:i + 1, :])
            merged = merged + gate * b
            dbr = (dm * gate).astype(BF16)
            dbr_ref[i] = dbr
            dgate = dm * b * gate * (1.0 - gate)
            dg_ref[:, D * i:D * (i + 1)] = dgate.astype(BF16)
            dgb_ref[i:i + 1, :] += jnp.sum(dgate, axis=0, keepdims=True)
            dys.append(_dot_nt(dbr, wp_ref[i]))
        mt_ref[...] = merged.astype(BF16).T
        dya_ref[...] = dys[0].astype(BF16)
        for dy, o_ref, z_ref, do_ref, dz_ref in ((dys[1], ob_ref, zb_ref, dob_ref, dzb_ref),
                                                 (dys[2], oc_ref, zc_ref, doc_ref, dzc_ref)):
            z = z_ref[...].astype(F32)
            sg = _sigmoid(z)
            do_ref[...] = (dy * z * sg).astype(BF16)
            dz_ref[...] = (dy * o_ref[...].astype(F32) * sg * (1.0 + z * (1.0 - sg))).astype(BF16)

    row = lambda c: pl.BlockSpec((tm, D), lambda i, c=c: (i, c))
    sds = jax.ShapeDtypeStruct((T, D), BF16)
    return pl.pallas_call(
        body, name=f"merge_bwd_{li}", grid=(T // tm,),
        in_specs=[row(0), pl.BlockSpec((D, D), lambda i: (0, 0)), pl.BlockSpec((3, D, D), lambda i: (0, 0, 0)),
                  pl.BlockSpec((3, tm, D), lambda i: (0, i, 0)),
                  row(gcol), row(gcol + 1), row(gcol + 2),
                  pl.BlockSpec((3, D), lambda i: (0, 0)),
                  row(0), row(0), row(OFF_BZ // D), row(OFF_CZ // D)],
        out_specs=[pl.BlockSpec((3, tm, D), lambda i: (0, i, 0)),
                   pl.BlockSpec((tm, 3 * D), lambda i: (i, 0)),
                   pl.BlockSpec((D, tm), lambda i: (0, i)),
                   pl.BlockSpec((8, D), lambda i: (0, 0)),
                   row(0), row(0), row(0), row(0), row(0)],
        out_shape=[jax.ShapeDtypeStruct((3, T, D), BF16), jax.ShapeDtypeStruct((T, 3 * D), BF16),
                   jax.ShapeDtypeStruct((D, T), BF16), jax.ShapeDtypeStruct((8, D), F32), sds, sds, sds, sds, sds],
        compiler_params=_cparams(("arbitrary",), VMEM_LIMIT),
    )(dxo16, wout, wp, br, proj, proj, proj, gbias, ob, oc, proj, proj)


def _final_loss(x2, tgt, fw):
    T = x2.shape[0]
    tm = min(512, T)
    ni = T // tm

    def body(x_ref, t_ref, w_ref, dx_ref, dx16_ref, st_ref):
        i = pl.program_id(0)

        @pl.when(i == 0)
        def _():
            st_ref[...] = jnp.zeros_like(st_ref)

        x = x_ref[...]
        r = lax.rsqrt(jnp.mean(x * x, axis=-1, keepdims=True) + EPS)
        xh = x * r
        err = xh * w_ref[...] - t_ref[...]
        dy = err * (1.0 / D)
        g = dy * w_ref[...]
        dx = r * g - x * (r * r * r) * jnp.mean(g * x, axis=-1, keepdims=True)
        dx_ref[...] = dx
        dx16_ref[...] = dx.astype(BF16)
        st_ref[0:1, :] += jnp.sum(dy * xh, axis=0, keepdims=True)
        st_ref[1:2, :] += jnp.sum(err * err, axis=0, keepdims=True)

        @pl.when(i == ni - 1)
        def _():
            tot = jnp.sum(st_ref[1:2, :], axis=1, keepdims=True) * (0.5 / D)
            st_ref[2:3, :] = jnp.broadcast_to(tot, (1, D))

    row = pl.BlockSpec((tm, D), lambda i: (i, 0))
    return pl.pallas_call(
        body, name="final_loss", grid=(ni,),
        in_specs=[row, row, pl.BlockSpec((1, D), lambda i: (0, 0))],
        out_specs=[row, row, pl.BlockSpec((8, D), lambda i: (0, 0))],
        out_shape=[jax.ShapeDtypeStruct((T, D), F32), jax.ShapeDtypeStruct((T, D), BF16),
                   jax.ShapeDtypeStruct((8, D), F32)],
        compiler_params=_cparams(("arbitrary",), VMEM_LIMIT),
    )(x2, tgt, fw)


def _fox_cum(ps, fb_row, S, li):
    T = ps.shape[0]
    blk = min(4 * LCH, S)
    nb, nsub = S // blk, blk // LCH

    def body(ps_ref, fb_ref, cum_ref, carry):
        @pl.when(pl.program_id(1) == 0)
        def _():
            carry[...] = jnp.zeros_like(carry)

        r = lax.broadcasted_iota(jnp.int32, (LCH, LCH), 0)
        c = lax.broadcasted_iota(jnp.int32, (LCH, LCH), 1)
        tri = (r >= c).astype(F32)
        run = carry[0:1, :]
        for u in range(nsub):
            rows = slice(LCH * u, LCH * (u + 1))
            logf = -_softplus(-(ps_ref[rows, :] + fb_ref[...]))
            cum = _dot_hi(tri, logf) + run
            cum_ref[rows, :] = cum
            run = cum[LCH - 1:LCH, :]
        carry[0:1, :] = run

    return pl.pallas_call(
        body, name=f"fox_cum_{li}", grid=(T // S, nb),
        in_specs=[pl.BlockSpec((blk, LANES), lambda b, i: (b * nb + i, 0)),
                  pl.BlockSpec((1, LANES), lambda b, i: (0, 0))],
        out_specs=pl.BlockSpec((blk, LANES), lambda b, i: (b * nb + i, 0)),
        out_shape=jax.ShapeDtypeStruct((T, LANES), F32),
        scratch_shapes=[pltpu.VMEM((8, LANES), F32)],
        compiler_params=_cparams(("arbitrary", "arbitrary")),
    )(ps, fb_row)


def _fox_cum_bwd(dcum, ps, fb_row, S, li):
    T = ps.shape[0]
    rows_blk = min(4 * LCH, S)
    nb, nsub = S // rows_blk, rows_blk // LCH

    def body(dc_ref, ps_ref, fb_ref, df_ref, dfb_ref, carry):
        b, i = pl.program_id(0), pl.program_id(1)

        @pl.when(i == 0)
        def _():
            carry[...] = jnp.zeros_like(carry)

        @pl.when(jnp.logical_and(b == 0, i == 0))
        def _():
            dfb_ref[...] = jnp.zeros_like(dfb_ref)

        r = lax.broadcasted_iota(jnp.int32, (LCH, LCH), 0)
        c = lax.broadcasted_iota(jnp.int32, (LCH, LCH), 1)
        tri = (c >= r).astype(F32)
        lane = _lane_iota()
        live = jnp.logical_and(lane >= NH, lane < 2 * NH)
        run = carry[0:1, :]
        dfb = jnp.zeros((1, LANES), F32)
        for u in reversed(range(nsub)):
            rows = slice(LCH * u, LCH * (u + 1))
            dc = dc_ref[rows, :]
            dlogf = _dot_hi(tri, dc) + run
            run = run + jnp.sum(dc, axis=0, keepdims=True)
            df = jnp.where(live, dlogf * _sigmoid(-(ps_ref[rows, :] + fb_ref[...])), 0.0)
            df_ref[rows, :] = df
            dfb = dfb + jnp.sum(df, axis=0, keepdims=True)
        carry[0:1, :] = run
        dfb_ref[0:1, :] += dfb

    blk = pl.BlockSpec((rows_blk, LANES), lambda b, i: (b * nb + nb - 1 - i, 0))
    return pl.pallas_call(
        body, name=f"fox_cum_bwd_{li}", grid=(T // S, nb),
        in_specs=[blk, blk, pl.BlockSpec((1, LANES), lambda b, i: (0, 0))],
        out_specs=[blk, pl.BlockSpec((8, LANES), lambda b, i: (0, 0))],
        out_shape=[jax.ShapeDtypeStruct((T, LANES), F32), jax.ShapeDtypeStruct((8, LANES), F32)],
        scratch_shapes=[pltpu.VMEM((8, LANES), F32)],
        compiler_params=_cparams(("arbitrary", "arbitrary")),
    )(dcum, ps, fb_row)


def _fox_blocks(S):
    bq = min(512, S)
    return bq, S // bq


def _split3(c):
    hi = c.astype(BF16).astype(F32)
    r = c - hi
    mid = r.astype(BF16).astype(F32)
    return hi, mid, (r - mid).astype(BF16).astype(F32)


def _augment(x, parts, key_side, hh):
    lane = _lane_iota()
    b0 = HD if hh == 0 else 0
    p0, o0 = (b0 + 3, b0) if key_side else (b0, b0 + 3)
    out = jnp.where(jnp.logical_and(lane >= o0, lane < o0 + 3), 1.0, x)
    for t in range(3):
        out = jnp.where(lane == p0 + t, parts[t], out)
    return out.astype(BF16)


def _fox_fwd(proj, cum_col, S, li, comm=None):
    T = proj.shape[0]
    B = T // S
    bq, nq = _fox_blocks(S)
    qc, kc, vc, zc = OFF_CQ // LANES, OFF_CK // LANES, OFF_CV // LANES, OFF_CZ // LANES

    def body(q_ref, k_ref, v_ref, z_ref, cc_ref, y_ref, o_ref, lse_ref, kaug):
        i = pl.program_id(2)
        m0 = _lane_iota() < HD

        @pl.when(i == 0)
        def _():
            kf = k_ref[...].astype(F32)
            for hh in range(2):
                kaug[hh] = _augment(kf, _split3(-cc_ref[0, hh]), True, hh)

        q2 = q_ref[...].astype(F32) * SCALE
        rows_q = pl.ds(pl.multiple_of(i * bq, bq), bq)
        row = lax.broadcasted_iota(jnp.int32, (bq, bq), 0)
        col = lax.broadcasted_iota(jnp.int32, (bq, bq), 1)
        outs = []
        for hh in range(2):
            sel = m0 if hh == 0 else jnp.logical_not(m0)
            qa = _augment(jnp.where(sel, q2, 0.0), _split3(cc_ref[0, hh, rows_q, :]), False, hh)

            def step(j, carry, masked, hh=hh, qa=qa):
                m, l, acc = carry
                start = pl.multiple_of(j * bq, bq)
                v2 = v_ref[pl.ds(start, bq), :]
                s = _dot_nt(qa, kaug[hh, pl.ds(start, bq), :])
                if masked:
                    s = jnp.where(row >= col, s, NEG)
                mn = jnp.maximum(m, jnp.max(s, axis=1, keepdims=True))
                alpha = jnp.exp(m - mn)
                p = jnp.exp(s - mn)
                l = alpha * l + jnp.sum(p, axis=1, keepdims=True)
                acc = alpha * acc + _dot(p.astype(BF16), v2)
                return mn, l, acc

            init = (jnp.full((bq, 1), NEG, F32), jnp.zeros((bq, 1), F32), jnp.zeros((bq, LANES), F32))
            carry = lax.fori_loop(0, i, functools.partial(step, masked=False), init)
            m, l, acc = step(i, carry, True)
            outs.append(acc / l)
            lse_ref[0, hh] = m + jnp.log(l)
        o2 = jnp.where(m0, outs[0], outs[1])
        z = z_ref[...].astype(F32)
        o_ref[...] = o2.astype(BF16)
        y_ref[...] = (o2 * z * _sigmoid(z)).astype(BF16)

    qblk = lambda c: pl.BlockSpec((bq, LANES), lambda b, p, i, c=c: (b * nq + i, c + p))
    sblk = lambda c: pl.BlockSpec((S, LANES), lambda b, p, i, c=c: (b, c + p))
    return _hosted_call(
        body, comm, f"fox_fwd_{li}", (B, NH // 2, nq),
        in_specs=[qblk(qc), sblk(kc), sblk(vc), qblk(zc),
                  pl.BlockSpec((1, 2, S, 1), lambda b, p, i: (b, p, 0, 0))],
        out_specs=[qblk(0), qblk(0), pl.BlockSpec((1, 2, bq, 1), lambda b, p, i: (b, p, i, 0))],
        out_shape=[jax.ShapeDtypeStruct((T, D), BF16), jax.ShapeDtypeStruct((T, D), BF16),
                   jax.ShapeDtypeStruct((B, NH, S, 1), F32)],
        scratch=[pltpu.VMEM((2, S, LANES), BF16)], dims=("parallel", "parallel", "arbitrary"),
        operands=(proj, proj, proj, proj, cum_col))


def _fox_bwd(proj, do, o, cum_col, lse, S, li):
    T = proj.shape[0]
    B = T // S
    bq, nq = _fox_blocks(S)
    qc, kc, vc = OFF_CQ // LANES, OFF_CK // LANES, OFF_CV // LANES

    def body(q_ref, k_ref, v_ref, do_ref, o_ref, cc_ref, lse_ref, dq_ref, dk_ref, dv_ref, dc_ref, dr_ref,
             dq_scr, dr_scr, qaug):
        j = pl.program_id(2)
        m0 = _lane_iota() < HD

        @pl.when(j == 0)
        def _():
            dq_scr[...] = jnp.zeros_like(dq_scr)
            dr_scr[...] = jnp.zeros_like(dr_scr)
            qf = q_ref[...].astype(F32) * SCALE
            for hh in range(2):
                sel = m0 if hh == 0 else jnp.logical_not(m0)
                qaug[hh] = _augment(jnp.where(sel, qf, 0.0), _split3(cc_ref[0, hh] - lse_ref[0, hh]), False, hh)

        k2 = k_ref[...]
        v2 = v_ref[...]
        zk = jnp.zeros_like(k2)
        kh = (jnp.where(m0, k2, zk), jnp.where(m0, zk, k2))
        kf = k2.astype(F32)
        rows_k = pl.ds(pl.multiple_of(j * bq, bq), bq)
        ka = [_augment(kf, _split3(-cc_ref[0, hh, rows_k, :]), True, hh) for hh in range(2)]
        row = lax.broadcasted_iota(jnp.int32, (bq, bq), 0)
        col = lax.broadcasted_iota(jnp.int32, (bq, bq), 1)

        def step(i, carry, masked):
            dk, dv, dc0, dc1 = carry
            dcs = [dc0, dc1]
            start = pl.multiple_of(i * bq, bq)
            q2 = q_ref[pl.ds(start, bq), :]
            do2 = do_ref[pl.ds(start, bq), :]
            prod = do2.astype(F32) * o_ref[pl.ds(start, bq), :].astype(F32)
            zq = jnp.zeros_like(q2)
            dq = jnp.zeros((bq, LANES), F32)
            for hh in range(2):
                sel = m0 if hh == 0 else jnp.logical_not(m0)
                qh = jnp.where(sel, q2, zq)
                doh = jnp.where(sel, do2, zq)
                delta = _head_sum(prod, hh)
                s = _dot_nt(qaug[hh, pl.ds(start, bq), :], ka[hh])
                if masked:
                    s = jnp.where(row >= col, s, NEG)
                p = jnp.exp(s)
                dp = _dot_nt(doh, v2)
                ds = p * (dp - delta)
                dcs[hh] = dcs[hh] - jnp.sum(ds, axis=0, keepdims=True)
                dr_scr[hh, pl.ds(start, bq), :] += jnp.sum(ds, axis=1, keepdims=True)
                dsb = ds.astype(BF16)
                dv = dv + _dot_tn(p.astype(BF16), doh)
                dk = dk + _dot_tn(dsb, qh)
                dq = dq + _dot(dsb, kh[hh])
            dq_scr[pl.ds(start, bq), :] += dq
            return dk, dv, dcs[0], dcs[1]

        zero = jnp.zeros((bq, LANES), F32)
        zrow = jnp.zeros((1, bq), F32)
        carry = step(j, (zero, zero, zrow, zrow), True)
        dk, dv, dc0, dc1 = lax.fori_loop(j + 1, nq, functools.partial(step, masked=False), carry)
        dk_ref[...] = (dk * SCALE).astype(BF16)
        dv_ref[...] = dv.astype(BF16)
        dc_ref[0, 0, 0] = dc0
        dc_ref[0, 1, 0] = dc1

        @pl.when(j == nq - 1)
        def _():
            dq_ref[...] = (dq_scr[...] * SCALE).astype(BF16)
            dr_ref[0] = dr_scr[...]

    sblk = lambda c: pl.BlockSpec((S, LANES), lambda b, p, j, c=c: (b, c + p))
    kblk = lambda c: pl.BlockSpec((bq, LANES), lambda b, p, j, c=c: (b * nq + j, c + p))
    col_spec = pl.BlockSpec((1, 2, S, 1), lambda b, p, j: (b, p, 0, 0))
    return pl.pallas_call(
        body, name=f"fox_bwd_{li}", grid=(B, NH // 2, nq),
        in_specs=[sblk(qc), kblk(kc), kblk(vc), sblk(0), sblk(0), col_spec, col_spec],
        out_specs=[sblk(0), kblk(0), kblk(0), pl.BlockSpec((1, 2, 1, 1, bq), lambda b, p, j: (b, p, j, 0, 0)),
                   col_spec],
        out_shape=[jax.ShapeDtypeStruct((T, D), BF16), jax.ShapeDtypeStruct((T, D), BF16),
                   jax.ShapeDtypeStruct((T, D), BF16), jax.ShapeDtypeStruct((B, NH, nq, 1, bq), F32),
                   jax.ShapeDtypeStruct((B, NH, S, 1), F32)],
        scratch_shapes=[pltpu.VMEM((S, LANES), F32), pltpu.VMEM((2, S, 1), F32), pltpu.VMEM((2, S, LANES), BF16)],
        compiler_params=_cparams(("parallel", "parallel", "arbitrary"), VMEM_LIMIT),
    )(proj, proj, proj, do, o, cum_col, lse)


def _swa_blocks(S):
    bq = min(512, S)
    return bq, S // bq, bq // LCH


def _dup_head(xw, kvl):
    m0 = _lane_iota() < HD
    a = jnp.where(m0 if kvl == 0 else jnp.logical_not(m0), xw, 0.0)
    return (a + pltpu.roll(a, HD, 1)).astype(BF16)


def _band(same_block):
    r = lax.broadcasted_iota(jnp.int32, (LCH, LCH), 0)
    c = lax.broadcasted_iota(jnp.int32, (LCH, LCH), 1)
    return (c <= r) if same_block else (c > r)


def _stack_heads(ref, rows, kvl):
    m0 = _lane_iota() < HD
    parts = []
    for ch in (2 * kvl, 2 * kvl + 1):
        x = ref[rows, LANES * ch:LANES * (ch + 1)]
        parts += [jnp.where(m0, x, jnp.zeros_like(x)), jnp.where(m0, jnp.zeros_like(x), x)]
    return jnp.concatenate(parts, axis=0)


def _stack_delta(do_ref, o_ref, rows, kvl, scale=None):
    parts = []
    for ch in (2 * kvl, 2 * kvl + 1):
        lanes = slice(LANES * ch, LANES * (ch + 1))
        prod = do_ref[rows, lanes].astype(F32) * o_ref[rows, lanes].astype(F32)
        parts += [_head_sum(prod, 0), _head_sum(prod, 1)]
    out = jnp.concatenate(parts, axis=0)
    return out if scale is None else out * scale


def _stack_cols(ref, rows, kvl):
    return jnp.concatenate([ref[0, 4 * kvl + t, rows, :] for t in range(4)], axis=0)


def _swa_fwd(proj, sinks, S, li):
    T = proj.shape[0]
    B = T // S
    bq, nq, nsub = _swa_blocks(S)
    nrow = S // LCH
    qc, zc, kc, vc = OFF_BQ // 512, OFF_BZ // 512, OFF_BK // LANES, OFF_BV // LANES

    def body(sk_ref, q_ref, z_ref, kp_ref, kc_ref, vp_ref, vc_ref, y_ref, o_ref, lse_ref):
        c, i = pl.program_id(0), pl.program_id(2)
        m0 = _lane_iota() < HD
        kw = jnp.concatenate([kp_ref[...].astype(F32), kc_ref[...].astype(F32)], axis=0)
        vw = jnp.concatenate([vp_ref[...].astype(F32), vc_ref[...].astype(F32)], axis=0)
        kd = (_dup_head(kw, 0), _dup_head(kw, 1))
        vd = (_dup_head(vw, 0), _dup_head(vw, 1))
        valid = jnp.concatenate([_band(False), _band(True)], axis=1)
        col = lax.broadcasted_iota(jnp.int32, (LCH, 2 * LCH), 1)
        valid_first = jnp.logical_and(valid, jnp.logical_or(col >= LCH, i > 0))
        valid4 = jnp.concatenate([valid] * 4, axis=0)
        valid4_first = jnp.concatenate([valid_first] * 4, axis=0)
        for r in range(nsub):
            rows = slice(LCH * r, LCH * (r + 1))
            msk = valid4_first if r == 0 else valid4
            for kvl in range(2):
                kwin = kd[kvl][LCH * r:LCH * (r + 2)]
                vwin = vd[kvl][LCH * r:LCH * (r + 2)]
                qs = _stack_heads(q_ref, rows, kvl)
                sink = jnp.concatenate([jnp.full((LCH, 1), sk_ref[8 * c + 4 * kvl + t], F32) for t in range(4)], axis=0)
                s = jnp.where(msk, _dot_nt(qs, kwin) * SCALE, NEG)
                m = jnp.maximum(jnp.max(s, axis=1, keepdims=True), sink)
                p = jnp.exp(s - m)
                l = jnp.sum(p, axis=1, keepdims=True) + jnp.exp(sink - m)
                os_ = _dot(p.astype(BF16), vwin) / l
                lse = m + jnp.log(l)
                for t in range(4):
                    lse_ref[0, 4 * kvl + t, rows, :] = lse[LCH * t:LCH * (t + 1)]
                for u in range(2):
                    lanes = slice(LANES * (2 * kvl + u), LANES * (2 * kvl + u + 1))
                    o2 = jnp.where(m0, os_[LCH * 2 * u:LCH * (2 * u + 1)], os_[LCH * (2 * u + 1):LCH * (2 * u + 2)])
                    z = z_ref[rows, lanes].astype(F32)
                    o_ref[rows, lanes] = o2.astype(BF16)
                    y_ref[rows, lanes] = (o2 * z * _sigmoid(z)).astype(BF16)

    wide = lambda cc: pl.BlockSpec((bq, 512), lambda c, b, i, cc=cc: (b * nq + i, cc + c))
    cur = lambda cc: pl.BlockSpec((bq, LANES), lambda c, b, i, cc=cc: (b * nq + i, cc + c))
    prev = lambda cc: pl.BlockSpec((LCH, LANES), lambda c, b, i, cc=cc: (b * nrow + jnp.maximum(i * nsub - 1, 0), cc + c))
    return pl.pallas_call(
        body, name=f"swa_fwd_{li}", grid=(2, B, nq),
        in_specs=[pl.BlockSpec(memory_space=pltpu.SMEM), wide(qc), wide(zc), prev(kc), cur(kc), prev(vc), cur(vc)],
        out_specs=[wide(0), wide(0), pl.BlockSpec((1, 8, bq, 1), lambda c, b, i: (b, c, i, 0))],
        out_shape=[jax.ShapeDtypeStruct((T, D), BF16), jax.ShapeDtypeStruct((T, D), BF16),
                   jax.ShapeDtypeStruct((B, NH, S, 1), F32)],
        compiler_params=_cparams(("parallel", "parallel", "parallel"), VMEM_LIMIT),
    )(sinks, proj, proj, proj, proj, proj, proj)


def _swa_bwd_dq(proj, do, o, lse, sinks, cos128, sin128, S, li):
    T = proj.shape[0]
    B = T // S
    bq, nq, nsub = _swa_blocks(S)
    nrow = S // LCH
    qc, kc, vc = OFF_BQ // 512, OFF_BK // LANES, OFF_BV // LANES

    def body(sk_ref, q_ref, do_ref, o_ref, lse_ref, kp_ref, kc_ref, vp_ref, vc_ref, cos_ref, sin_ref, dq_ref, dsk_ref):
        c, b, i = pl.program_id(0), pl.program_id(1), pl.program_id(2)

        @pl.when(jnp.logical_and(b == 0, i == 0))
        def _():
            dsk_ref[...] = jnp.zeros_like(dsk_ref)

        m0 = _lane_iota() < HD
        kw = jnp.concatenate([kp_ref[...].astype(F32), kc_ref[...].astype(F32)], axis=0)
        vw = jnp.concatenate([vp_ref[...].astype(F32), vc_ref[...].astype(F32)], axis=0)
        kd = (_dup_head(kw, 0), _dup_head(kw, 1))
        vd = (_dup_head(vw, 0), _dup_head(vw, 1))
        valid = jnp.concatenate([_band(False), _band(True)], axis=1)
        col = lax.broadcasted_iota(jnp.int32, (LCH, 2 * LCH), 1)
        valid_first = jnp.logical_and(valid, jnp.logical_or(col >= LCH, i > 0))
        dsk = [jnp.zeros((1, 1), F32) for _ in range(8)]
        valid4 = jnp.concatenate([valid] * 4, axis=0)
        valid4_first = jnp.concatenate([valid_first] * 4, axis=0)
        for r in range(nsub):
            rows = slice(LCH * r, LCH * (r + 1))
            msk = valid4_first if r == 0 else valid4
            for kvl in range(2):
                kwin = kd[kvl][LCH * r:LCH * (r + 2)]
                vwin = vd[kvl][LCH * r:LCH * (r + 2)]
                qs = _stack_heads(q_ref, rows, kvl)
                dos = _stack_heads(do_ref, rows, kvl)
                delta = _stack_delta(do_ref, o_ref, rows, kvl)
                lse = _stack_cols(lse_ref, rows, kvl)
                sink = jnp.concatenate([jnp.full((LCH, 1), sk_ref[8 * c + 4 * kvl + t], F32) for t in range(4)], axis=0)
                s = jnp.where(msk, _dot_nt(qs, kwin) * SCALE, NEG)
                p = jnp.exp(s - lse)
                ds = p * (_dot_nt(dos, vwin) - delta)
                dqs = _dot(ds.astype(BF16), kwin) * SCALE
                dsink = jnp.exp(sink - lse) * delta
                for t in range(4):
                    hl = 4 * kvl + t
                    dsk[hl] = dsk[hl] - jnp.sum(dsink[LCH * t:LCH * (t + 1)], axis=0, keepdims=True)
                for u in range(2):
                    lanes = slice(LANES * (2 * kvl + u), LANES * (2 * kvl + u + 1))
                    dq2 = jnp.where(m0, dqs[LCH * 2 * u:LCH * (2 * u + 1)], dqs[LCH * (2 * u + 1):LCH * (2 * u + 2)])
                    dq2 = dq2 * cos_ref[rows, :] - _rot_half(dq2) * sin_ref[rows, :]
                    dq_ref[rows, lanes] = dq2.astype(BF16)
        for hl in range(8):
            dsk_ref[0, hl:hl + 1, :] += jnp.broadcast_to(dsk[hl], (1, LANES))

    wide = lambda cc: pl.BlockSpec((bq, 512), lambda c, b, i, cc=cc: (b * nq + i, cc + c))
    cur = lambda cc: pl.BlockSpec((bq, LANES), lambda c, b, i, cc=cc: (b * nq + i, cc + c))
    prev = lambda cc: pl.BlockSpec((LCH, LANES), lambda c, b, i, cc=cc: (b * nrow + jnp.maximum(i * nsub - 1, 0), cc + c))
    pos = pl.BlockSpec((bq, LANES), lambda c, b, i: (i, 0))
    return pl.pallas_call(
        body, name=f"swa_bwd_dq_{li}", grid=(2, B, nq),
        in_specs=[pl.BlockSpec(memory_space=pltpu.SMEM), wide(qc), wide(0), wide(0),
                  pl.BlockSpec((1, 8, bq, 1), lambda c, b, i: (b, c, i, 0)),
                  prev(kc), cur(kc), prev(vc), cur(vc), pos, pos],
        out_specs=[wide(0), pl.BlockSpec((1, 8, LANES), lambda c, b, i: (c, 0, 0))],
        out_shape=[jax.ShapeDtypeStruct((T, D), BF16), jax.ShapeDtypeStruct((2, 8, LANES), F32)],
        compiler_params=_cparams(("arbitrary", "arbitrary", "arbitrary"), VMEM_LIMIT),
    )(sinks, proj, do, o, lse, proj, proj, proj, proj, cos128, sin128)


def _swa_bwd_dkv(proj, do, o, lse, cos128, sin128, S, li):
    T = proj.shape[0]
    B = T // S
    bk, nk, nsub = _swa_blocks(S)
    nrow = S // LCH
    qc, kc, vc = OFF_BQ // 512, OFF_BK // LANES, OFF_BV // LANES

    def body(q_ref, qn_ref, do_ref, don_ref, o_ref, on_ref, lse_ref, lsen_ref, k_ref, v_ref, cos_ref, sin_ref,
             dk_ref, dv_ref):
        j = pl.program_id(2)
        m0 = _lane_iota() < HD
        has_next = (j < nk - 1).astype(F32)
        kf = k_ref[...].astype(F32)
        vf = v_ref[...].astype(F32)
        kd = (_dup_head(kf, 0), _dup_head(kf, 1))
        vd = (_dup_head(vf, 0), _dup_head(vf, 1))
        masks4 = (jnp.concatenate([_band(True)] * 4, axis=0), jnp.concatenate([_band(False)] * 4, axis=0))
        for kr in range(nsub):
            krows = slice(LCH * kr, LCH * (kr + 1))
            dk = jnp.zeros((LCH, LANES), F32)
            dv = jnp.zeros((LCH, LANES), F32)
            for dq_blk in range(2):
                rq = kr + dq_blk
                nxt = rq == nsub
                qrows = slice(0, LCH) if nxt else slice(LCH * rq, LCH * (rq + 1))
                qr, dor, orr, lr = (qn_ref, don_ref, on_ref, lsen_ref) if nxt else (q_ref, do_ref, o_ref, lse_ref)
                for kvl in range(2):
                    qs = _stack_heads(qr, qrows, kvl)
                    dos = _stack_heads(dor, qrows, kvl)
                    delta = _stack_delta(dor, orr, qrows, kvl, has_next if nxt else None)
                    if nxt:
                        dos = (dos.astype(F32) * has_next).astype(BF16)
                    s = jnp.where(masks4[dq_blk], _dot_nt(qs, kd[kvl][krows]) * SCALE, NEG)
                    p = jnp.exp(s - _stack_cols(lr, qrows, kvl))
                    ds = p * (_dot_nt(dos, vd[kvl][krows]) - delta)
                    dvc = _dot_tn(p.astype(BF16), dos)
                    dkc = _dot_tn(ds.astype(BF16), qs) * SCALE
                    own = m0 if kvl == 0 else jnp.logical_not(m0)
                    dv = dv + jnp.where(own, dvc + pltpu.roll(dvc, HD, 1), 0.0)
                    dk = dk + jnp.where(own, dkc + pltpu.roll(dkc, HD, 1), 0.0)
            dk = dk * cos_ref[krows, :] - _rot_half(dk) * sin_ref[krows, :]
            dk_ref[krows, :] = dk.astype(BF16)
            dv_ref[krows, :] = dv.astype(BF16)

    wide = lambda cc: pl.BlockSpec((bk, 512), lambda c, b, j, cc=cc: (b * nk + j, cc + c))
    nxt = lambda cc: pl.BlockSpec((LCH, 512), lambda c, b, j, cc=cc: (b * nrow + jnp.minimum((j + 1) * nsub, nrow - 1), cc + c))
    cur = lambda cc: pl.BlockSpec((bk, LANES), lambda c, b, j, cc=cc: (b * nk + j, cc + c))
    pos = pl.BlockSpec((bk, LANES), lambda c, b, j: (j, 0))
    return pl.pallas_call(
        body, name=f"swa_bwd_dkv_{li}", grid=(2, B, nk),
        in_specs=[wide(qc), nxt(qc), wide(0), nxt(0), wide(0), nxt(0),
                  pl.BlockSpec((1, 8, bk, 1), lambda c, b, j: (b, c, j, 0)),
                  pl.BlockSpec((1, 8, LCH, 1), lambda c, b, j: (b, c, jnp.minimum((j + 1) * nsub, nrow - 1), 0)),
                  cur(kc), cur(vc), pos, pos],
        out_specs=[cur(0), cur(0)],
        out_shape=[jax.ShapeDtypeStruct((T, 2 * LANES), BF16), jax.ShapeDtypeStruct((T, 2 * LANES), BF16)],
        compiler_params=_cparams(("parallel", "parallel", "parallel"), VMEM_LIMIT),
    )(proj, proj, do, do, o, o, lse, lse, proj, proj, cos128, sin128)


HALO = 16


def _shift_matrices():
    r = lax.broadcasted_iota(jnp.int32, (3 * LCH, LCH + HALO), 0)
    c = lax.broadcasted_iota(jnp.int32, (3 * LCH, LCH + HALO), 1)
    t, d = r % LCH, r // LCH + 1
    return (c == HALO + t - d).astype(BF16), (c == t + d).astype(BF16)


def _ssm_chunk_pre(prev16, cur16, first, sdn_ref, cw_ref, cb_ref, ps, dtb, alog):
    ext16 = jnp.concatenate([jnp.where(first, jnp.zeros_like(prev16), prev16), cur16], axis=0)
    sh = _dot(sdn_ref[...], ext16)
    pre = cb_ref[...] + cw_ref[3:4, :] * cur16.astype(F32)
    for d in range(1, 4):
        pre = pre + cw_ref[3 - d:4 - d, :] * sh[LCH * (d - 1):LCH * d]
    sg = _sigmoid(pre)
    dt = _softplus(ps + dtb)
    a = -jnp.exp(alog)
    r = lax.broadcasted_iota(jnp.int32, (LCH, LCH), 0)
    c = lax.broadcasted_iota(jnp.int32, (LCH, LCH), 1)
    acum = _dot_hi((r >= c).astype(F32), dt * a)
    return pre, sg, dt, a, acum, sh


def _expand_matrix():
    r = lax.broadcasted_iota(jnp.int32, (3 * LANES, D), 0)
    c = lax.broadcasted_iota(jnp.int32, (3 * LANES, D), 1)
    return ((r % LANES) == c // HD).astype(BF16)


def _expand_heads(v, ex_ref):
    return _dot(jnp.concatenate(_split3(v), axis=1).astype(BF16), ex_ref[...])


def _decay(acum, acum_t, h):
    r = lax.broadcasted_iota(jnp.int32, (LCH, LCH), 0)
    c = lax.broadcasted_iota(jnp.int32, (LCH, LCH), 1)
    causal = r >= c
    seg = acum[:, h:h + 1] - acum_t[h:h + 1, :]
    return jnp.where(causal, jnp.exp(jnp.where(causal, seg, 0.0)), 0.0)


def _ssm_pair_fwd(p, x, dt_x, acum, acum_t, e_x, w_x, cd, cb_g, b_g, c_g, hprev, dsk_ref):
    m0 = _lane_iota() < HD
    lanes = slice(LANES * p, LANES * (p + 1))
    x2 = x[:, lanes]
    dt2 = dt_x[:, lanes]
    xdt2 = x2 * dt2
    xdtb = xdt2.astype(BF16)
    lms, ms, yds = [], [], []
    for hh in range(2):
        lm = _decay(acum, acum_t, 2 * p + hh)
        mm = cb_g * lm
        lms.append(lm)
        ms.append(mm)
        yds.append(_dot(mm.astype(BF16), xdtb))
    yd2 = jnp.where(m0, yds[0], yds[1])
    w2 = w_x[:, lanes]
    xw = (xdt2 * w2).astype(BF16)
    s2 = _dot_tn(xw, b_g)
    z2 = _dot_nt(c_g, hprev.astype(BF16))
    e2 = e_x[:, lanes]
    rowsel = lax.broadcasted_iota(jnp.int32, (LANES, 1), 0) < HD
    cdcol = jnp.where(rowsel, cd[:, 2 * p:2 * p + 1], cd[:, 2 * p + 1:2 * p + 2])
    y2 = yd2 + z2 * e2 + dsk_ref[:, lanes] * x2
    return dict(x2=x2, dt2=dt2, xdt2=xdt2, xdtb=xdtb, lms=lms, ms=ms, yd2=yd2, w2=w2, xw=xw, s2=s2, z2=z2, e2=e2,
                cdcol=cdcol, y2=y2)


def _ssm_specs(S, rev):
    nc = S // LCH
    ch = (lambda c: nc - 1 - c) if rev else (lambda c: c)
    prev = pl.BlockSpec((HALO, 2 * D), lambda b, c: (jnp.maximum(b * (S // HALO) + ch(c) * (LCH // HALO) - 1, 0), 0))
    cur = pl.BlockSpec((LCH, 2 * D), lambda b, c: (b * nc + ch(c), 0))
    zed = pl.BlockSpec((LCH, D), lambda b, c: (b * nc + ch(c), OFF_AZ // D))
    row = pl.BlockSpec((LCH, D), lambda b, c: (b * nc + ch(c), 0))
    psb = pl.BlockSpec((LCH, LANES), lambda b, c: (b * nc + ch(c), 0))
    hpb = pl.BlockSpec((1, 1, NH // 2, LANES, NST), lambda b, c: (b, ch(c), 0, 0, 0))
    const = lambda r, w: pl.BlockSpec((r, w), lambda b, c: (0, 0))
    return nc, prev, cur, zed, row, psb, hpb, const


def _ssm_fwd(proj, ps, cw, cb, dtb, alog, dsk, nw, S, li):
    T = proj.shape[0]
    B = T // S
    nc, prev, cur, zed, row, psb, hpb, const = _ssm_specs(S, False)

    def body(prev_ref, cur_ref, z_ref, ps_ref, sdn_ref, ex_ref, cw_ref, cb_ref, dtb_ref, alog_ref, dsk_ref, nw_ref,
             ya_ref, hp_ref, h_scr):
        c = pl.program_id(1)

        @pl.when(c == 0)
        def _():
            h_scr[...] = jnp.zeros_like(h_scr)

        pre, sg, dt, a, acum, _ = _ssm_chunk_pre(prev_ref[...], cur_ref[...], c == 0, sdn_ref, cw_ref, cb_ref,
                                                 ps_ref[...], dtb_ref[...], alog_ref[...])
        act = pre * sg
        acum_t = acum.T
        last = acum[LCH - 1:LCH, :]
        cd = jnp.exp(last)
        dt, e_all, w_all = (_expand_heads(v, ex_ref) for v in (dt, jnp.exp(acum), jnp.exp(last - acum)))
        x = act[:, :D]
        for g in range(NGRP):
            b_g = act[:, D + NST * g:D + NST * (g + 1)].astype(BF16)
            c_g = act[:, D + NGRP * NST + NST * g:D + NGRP * NST + NST * (g + 1)].astype(BF16)
            cb_g = _dot_nt(c_g, b_g)
            ygs = []
            for p in (2 * g, 2 * g + 1):
                hprev = h_scr[p]
                hp_ref[0, 0, p] = hprev
                f = _ssm_pair_fwd(p, x, dt, acum, acum_t, e_all, w_all, cd, cb_g, b_g, c_g, hprev, dsk_ref)
                h_scr[p] = hprev * f["cdcol"] + f["s2"]
                z2 = z_ref[:, LANES * p:LANES * (p + 1)].astype(F32)
                ygs.append(f["y2"] * z2 * _sigmoid(z2))
            yg = jnp.concatenate(ygs, axis=1)
            r = lax.rsqrt(jnp.mean(yg * yg, axis=1, keepdims=True) + EPS)
            ya_ref[:, 2 * LANES * g:2 * LANES * (g + 1)] = (yg * r * nw_ref[:, 2 * LANES * g:2 * LANES * (g + 1)]).astype(BF16)

    return pl.pallas_call(
        body, name=f"ssm_fwd_{li}", grid=(B, nc),
        in_specs=[prev, cur, zed, psb, const(3 * LCH, LCH + HALO), const(3 * LANES, D), const(4, 2 * D),
                  const(1, 2 * D), const(1, LANES), const(1, LANES), const(1, D), const(1, D)],
        out_specs=[row, hpb],
        out_shape=[jax.ShapeDtypeStruct((T, D), BF16), jax.ShapeDtypeStruct((B, nc, NH // 2, LANES, NST), F32)],
        scratch_shapes=[pltpu.VMEM((NH // 2, LANES, NST), F32)],
        compiler_params=_cparams(("arbitrary", "arbitrary"), VMEM_LIMIT),
    )(proj, proj, proj, ps, _shift_matrices()[0], _expand_matrix(), cw, cb, dtb, alog, dsk, nw)


def _ssm_bwd(proj, ps, hp, dya, cw, cb, dtb, alog, dsk, nw, S, li, comm=None):
    T = proj.shape[0]
    B = T // S
    nc, prev, cur, zed, row, psb, hpb, const = _ssm_specs(S, True)

    def body(prev_ref, cur_ref, z_ref, ps_ref, hp_ref, dy_ref, sdn_ref, sup_ref, ex_ref, cw_ref, cb_ref, dtb_ref,
             alog_ref, dsk_ref, nw_ref, dxbc_ref, dz_ref, dps_ref, pgw_ref, pg1_ref, pgh_ref, dh_scr, dhead, dact):
        b, cc = pl.program_id(0), pl.program_id(1)
        c = nc - 1 - cc

        @pl.when(jnp.logical_and(b == 0, cc == 0))
        def _():
            pgw_ref[...] = jnp.zeros_like(pgw_ref)
            pg1_ref[...] = jnp.zeros_like(pg1_ref)
            pgh_ref[...] = jnp.zeros_like(pgh_ref)

        @pl.when(cc == 0)
        def _():
            dh_scr[...] = jnp.zeros_like(dh_scr)
            dhead[...] = jnp.zeros_like(dhead)

        psv = ps_ref[...]
        cur16 = cur_ref[...]
        pre, sg, dt, a, acum, sh = _ssm_chunk_pre(prev_ref[...], cur16, c == 0, sdn_ref, cw_ref, cb_ref, psv,
                                                  dtb_ref[...], alog_ref[...])
        act = pre * sg
        acum_t = acum.T
        last = acum[LCH - 1:LCH, :]
        w_all = jnp.exp(last - acum)
        cd = jnp.exp(last)
        dt_x, e_x, w_x = (_expand_heads(v, ex_ref) for v in (dt, jnp.exp(acum), w_all))
        x = act[:, :D]
        lane = _lane_iota()
        m0 = lane < HD
        head_row = lax.broadcasted_iota(jnp.int32, (LANES, 1), 0)
        rowsel = head_row < HD
        is_last_row = lax.broadcasted_iota(jnp.int32, (LCH, 1), 0) == LCH - 1
        dacum_all = jnp.zeros((LCH, LANES), F32)
        dacum_t = jnp.zeros((LANES, LCH), F32)
        ddt_all = jnp.zeros((LCH, LANES), F32)
        dd_row = jnp.zeros((1, LANES), F32)
        for g in range(NGRP):
            b_g = act[:, D + NST * g:D + NST * (g + 1)].astype(BF16)
            c_g = act[:, D + NGRP * NST + NST * g:D + NGRP * NST + NST * (g + 1)].astype(BF16)
            cb_g = _dot_nt(c_g, b_g)
            pairs = (2 * g, 2 * g + 1)
            fs, hps, zs, ygs = [], [], [], []
            for p in pairs:
                hprev = hp_ref[0, 0, p]
                f = _ssm_pair_fwd(p, x, dt_x, acum, acum_t, e_x, w_x, cd, cb_g, b_g, c_g, hprev, dsk_ref)
                z2 = z_ref[:, LANES * p:LANES * (p + 1)].astype(F32)
                fs.append(f)
                hps.append(hprev)
                zs.append(z2)
                ygs.append(f["y2"] * z2 * _sigmoid(z2))
            gl = slice(2 * LANES * g, 2 * LANES * (g + 1))
            yg = jnp.concatenate(ygs, axis=1)
            r = lax.rsqrt(jnp.mean(yg * yg, axis=1, keepdims=True) + EPS)
            dyn = dy_ref[:, gl].astype(F32)
            gg = dyn * nw_ref[:, gl]
            dyg = r * gg - yg * (r * r * r) * jnp.mean(gg * yg, axis=1, keepdims=True)
            pg1_ref[0:1, gl] += jnp.sum(dyn * yg * r, axis=0, keepdims=True)
            dg_g = jnp.zeros((LCH, LCH), F32)
            db_g = jnp.zeros((LCH, NST), F32)
            dc_g = jnp.zeros((LCH, NST), F32)
            for idx, p in enumerate(pairs):
                f, hprev, z2 = fs[idx], hps[idx], zs[idx]
                lanes = slice(LANES * p, LANES * (p + 1))
                dyg2 = dyg[:, LANES * idx:LANES * (idx + 1)]
                sgz = _sigmoid(z2)
                dy2 = dyg2 * z2 * sgz
                dz_ref[:, lanes] = (dyg2 * f["y2"] * sgz * (1.0 + z2 * (1.0 - sgz))).astype(BF16)
                x2, dt2, xdt2, xdtb, w2, e2, z2m = f["x2"], f["dt2"], f["xdt2"], f["xdtb"], f["w2"], f["e2"], f["z2"]
                dx2 = dsk_ref[:, lanes] * dy2
                dyx = dy2 * x2
                dxdt2 = jnp.zeros((LCH, LANES), F32)
                diag_cols = []
                for hh in range(2):
                    sel = m0 if hh == 0 else jnp.logical_not(m0)
                    dyb = jnp.where(sel, dy2, 0.0).astype(BF16)
                    dm = _dot_nt(dyb, xdtb)
                    dg_g = dg_g + dm * f["lms"][hh]
                    dxdt2 = dxdt2 + _dot_tn(f["ms"][hh].astype(BF16), dyb)
                    em = dm * f["ms"][hh]
                    diag_cols.append(jnp.sum(em, axis=1, keepdims=True))
                    dacum_t = dacum_t - jnp.where(head_row == 2 * p + hh, jnp.sum(em, axis=0, keepdims=True), 0.0)
                dz2m = dy2 * e2
                t_off = dz2m * z2m
                dc_g = dc_g + _dot(dz2m.astype(BF16), hprev.astype(BF16))
                dhprev = _dot_tn(dz2m.astype(BF16), c_g)
                dhn = dh_scr[p]
                dhnb = dhn.astype(BF16)
                dhprev = dhprev + dhn * f["cdcol"]
                t_h = dhn * hprev
                dxw2 = _dot_nt(b_g, dhnb)
                db_g = db_g + _dot(f["xw"], dhnb)
                dxdt2 = dxdt2 + dxw2 * w2
                t_w = dxw2 * xdt2
                dx2 = dx2 + dxdt2 * dt2
                t_dt = dxdt2 * x2
                for hh in range(2):
                    h = 2 * p + hh
                    onehot = (lane == h).astype(F32)
                    w_col = w_all[:, h:h + 1]
                    dw_col = _head_sum(t_w, hh) * w_col
                    rs = rowsel if hh == 0 else jnp.logical_not(rowsel)
                    dlast = (jnp.sum(jnp.where(rs, t_h, 0.0), keepdims=True) * cd[:, h:h + 1]
                             + jnp.sum(dw_col, keepdims=True))
                    dacum_col = diag_cols[hh] + _head_sum(t_off, hh) - dw_col + jnp.where(is_last_row, dlast, 0.0)
                    dacum_all = dacum_all + dacum_col * onehot
                    ddt_all = ddt_all + _head_sum(t_dt, hh) * onehot
                    sel = m0 if hh == 0 else jnp.logical_not(m0)
                    dd_row = dd_row + jnp.sum(jnp.where(sel, dyx, 0.0), keepdims=True) * onehot
                dh_scr[p] = dhprev
                dact[:, lanes] = dx2
            dgb = dg_g.astype(BF16)
            dc_g = dc_g + _dot(dgb, b_g)
            db_g = db_g + _dot_tn(dgb, c_g)
            dact[:, D + NST * g:D + NST * (g + 1)] = db_g
            dact[:, D + NGRP * NST + NST * g:D + NGRP * NST + NST * (g + 1)] = dc_g
        rr = lax.broadcasted_iota(jnp.int32, (LCH, LCH), 0)
        cc2 = lax.broadcasted_iota(jnp.int32, (LCH, LCH), 1)
        dadt = _dot_hi((cc2 >= rr).astype(F32), dacum_all + dacum_t.T)
        ddt_all = ddt_all + dadt * a
        heads = lane < NH
        da = jnp.sum(dadt * dt, axis=0, keepdims=True)
        dr = jnp.where(heads, ddt_all * _sigmoid(psv + dtb_ref[...]), 0.0)
        dps_ref[...] = dr
        pgh_ref[0:1, :] += jnp.sum(dr, axis=0, keepdims=True)
        pgh_ref[1:2, :] += jnp.where(heads, da * a, 0.0)
        pgh_ref[2:3, :] += dd_row
        dpre = dact[...] * sg * (1.0 + pre * (1.0 - sg))
        extd = jnp.concatenate([dpre, dhead[...]], axis=0)
        hi = extd.astype(BF16)
        lo = (extd - hi.astype(F32)).astype(BF16)
        up = _dot(sup_ref[...], hi) + _dot(sup_ref[...], lo)
        du = cw_ref[3:4, :] * dpre
        pgw_ref[3:4, :] += jnp.sum(dpre * cur16.astype(F32), axis=0, keepdims=True)
        for d in range(1, 4):
            du = du + cw_ref[3 - d:4 - d, :] * up[LCH * (d - 1):LCH * d]
            pgw_ref[3 - d:4 - d, :] += jnp.sum(dpre * sh[LCH * (d - 1):LCH * d], axis=0, keepdims=True)
        pgw_ref[4:5, :] += jnp.sum(dpre, axis=0, keepdims=True)
        dxbc_ref[...] = du.astype(BF16)
        dhead[...] = dpre[0:HALO, :]

    xbc_out = pl.BlockSpec((LCH, 2 * D), lambda b, c: (b * nc + nc - 1 - c, 0))
    acc = lambda w: pl.BlockSpec((8, w), lambda b, c: (0, 0))
    sdn, sup = _shift_matrices()
    return _hosted_call(
        body, comm, f"ssm_bwd_{li}", (B, nc),
        in_specs=[prev, cur, zed, psb, hpb, row, const(3 * LCH, LCH + HALO), const(3 * LCH, LCH + HALO),
                  const(3 * LANES, D), const(4, 2 * D), const(1, 2 * D), const(1, LANES), const(1, LANES),
                  const(1, D), const(1, D)],
        out_specs=[xbc_out, row, psb, acc(2 * D), acc(D), acc(LANES)],
        out_shape=[jax.ShapeDtypeStruct((T, 2 * D), BF16), jax.ShapeDtypeStruct((T, D), BF16),
                   jax.ShapeDtypeStruct((T, LANES), F32), jax.ShapeDtypeStruct((8, 2 * D), F32),
                   jax.ShapeDtypeStruct((8, D), F32), jax.ShapeDtypeStruct((8, LANES), F32)],
        scratch=[pltpu.VMEM((NH // 2, LANES, NST), F32), pltpu.VMEM((HALO, 2 * D), F32),
                 pltpu.VMEM((LCH, 2 * D), F32)],
        dims=("arbitrary", "arbitrary"),
        operands=(proj, proj, proj, ps, hp, dya, sdn, sup, _expand_matrix(), cw, cb, dtb, alog, dsk, nw))


def _lane_row(v, offset):
    return jnp.pad(v.astype(F32), (offset, LANES - offset - v.shape[0]))[None]


def _pack_rows(arrays):
    parts = []
    for a in arrays:
        flat = a.reshape(-1).astype(F32)
        pad = (-flat.shape[0]) % LANES
        parts.append(jnp.pad(flat, (0, pad)))
    flat = jnp.concatenate(parts)
    pad = (-flat.shape[0]) % (8 * LANES)
    return jnp.pad(flat, (0, pad)).reshape(-1, LANES)


def _unpack_rows(pack, shapes):
    flat = pack.reshape(-1)
    out, pos = [], 0
    for shp in shapes:
        n = math.prod(shp)
        out.append(flat[pos:pos + n].reshape(shp))
        pos += n + (-n) % LANES
    return out


def _split_w_in(w):
    main = jnp.concatenate([w[:, 0:3072], w[:, 3088:4112], w[:, 4624:5648], w[:, 5648:8720], w[:, 8736:12832],
                            w[:, 4112:4624]], axis=1)
    small = jnp.concatenate([w[:, 3072:3088], w[:, 8720:8736], jnp.zeros((D, LANES - 2 * NH), w.dtype)], axis=1)
    return main, small


def _join_w_in(dw, ds):
    xbc, az, bq, bz, cq, ck, cv, cz, gates, bk, bv = dw
    return jnp.concatenate([xbc, az, ds[:, 0:NH], bq, bk, bv, bz, cq, ck, cv, ds[:, NH:2 * NH], cz, gates], axis=1)


def kernel(x, norm_w, w_in, conv_w, conv_b, dt_bias, a_log, d_skip, ssm_norm_w, sinks, f_bias, gate_bias, w_proj, w_out, final_norm_w, loss_target, m_norm_w, m_w_in, m_conv_w, m_conv_b, m_dt_bias, m_a_log, m_d_skip, m_ssm_norm_w, m_sinks, m_f_bias, m_gate_bias, m_w_proj, m_w_out, m_final_norm_w, v_norm_w, v_w_in, v_conv_w, v_conv_b, v_dt_bias, v_a_log, v_d_skip, v_ssm_norm_w, v_sinks, v_f_bias, v_gate_bias, v_w_proj, v_w_out, v_final_norm_w):
    Bl, S, _ = x.shape
    T = Bl * S
    depth = norm_w.shape[0]
    me = 4 * lax.axis_index("x") + 2 * lax.axis_index("y") + lax.axis_index("c")
    csh, gsh = conv_w.shape[2], gate_bias.shape[2]

    def gather_plan(l):
        small = jnp.concatenate([conv_w[l].reshape(-1), gate_bias[l].reshape(-1)]).reshape(-1, LANES)
        return _Comm("gather", [w_in[l].astype(BF16), w_proj[l].astype(BF16), w_out[l].astype(BF16), small])

    def unpack_weights(res):
        g_win, g_wp, g_wo, g_small = res
        flat = g_small.reshape(NDEV, -1)
        return (g_win.transpose(1, 0, 2).reshape(D, NIN),
                g_wp.transpose(1, 0, 2, 3).reshape(3, D, D),
                g_wo.reshape(D, D),
                flat[:, :4 * csh].reshape(NDEV, 4, csh).transpose(1, 0, 2).reshape(4, 2 * D),
                flat[:, 4 * csh:].reshape(NDEV, 3, gsh).transpose(1, 0, 2).reshape(3, D))

    def scatter_plan(gw_in, gw_p=None, gw_o=None):
        arrays = [gw_in.astype(BF16).reshape(-1, NDEV, NSH).transpose(1, 0, 2)]
        if gw_p is not None:
            arrays += [gw_p.astype(BF16).reshape(3, NDEV, D // NDEV, D).transpose(1, 0, 2, 3),
                       gw_o.astype(BF16).reshape(NDEV, D // NDEV, D)]
        return _Comm("scatter", arrays)

    pos = jnp.arange(S, dtype=F32)
    inv_freq = ROPE_THETA ** (-jnp.arange(0, HD, 2, dtype=F32) / HD)
    ang = pos[:, None] * inv_freq[None, :]
    cos128 = jnp.tile(jnp.cos(ang), (1, 4))
    sign = jnp.where((jnp.arange(LANES) % HD) < HD // 2, -1.0, 1.0).astype(F32)
    sin128 = jnp.tile(jnp.sin(ang), (1, 4)) * sign[None, :]

    bq, nq = _fox_blocks(S)
    x2 = x.reshape(T, D)
    tgt2 = loss_target.reshape(T, D)

    saved = []
    xcur = x2
    weights = [None] * depth
    weights[0] = unpack_weights(_gather_two_level(gather_plan(0).arrays, "gather_weights_0"))
    for l in range(depth):
        win_l, wp_l, wo_l, cw_l, gb_l = weights[l]
        wmain, wsmall = _split_w_in(win_l)
        proj, ps, h_t = _inproj_fwd(xcur, norm_w[l][None], wmain, wsmall, cos128, sin128, S, l)
        dtb = _lane_row(dt_bias[l], 0)
        alog = _lane_row(a_log[l], 0)
        fb = _lane_row(f_bias[l], NH)
        dsk = jnp.repeat(d_skip[l], HD)[None]
        ya, hp = _ssm_fwd(proj, ps, cw_l, conv_b[l][None], dtb, alog, dsk, ssm_norm_w[l][None], S, l)
        yb, ob, lse_b = _swa_fwd(proj, sinks[l], S, l)
        cum = _fox_cum(ps, fb, S, l)
        cumh = cum[:, NH:2 * NH].reshape(Bl, S, NH).transpose(0, 2, 1)
        cum_col = cumh[..., None]
        comm = gather_plan(l + 1) if l + 1 < depth else None
        res = _fox_fwd(proj, cum_col, S, l, comm)
        yc, oc, lse_c = res[:3]
        if comm is not None:
            weights[l + 1] = unpack_weights(res[3:])
        xnext, br, y_t = _merge_fwd(ya, yb, yc, proj, gb_l, wp_l, wo_l, xcur, l)
        saved.append(dict(x=xcur, wmain=wmain, wsmall=wsmall, proj=proj, ps=ps, h_t=h_t, dtb=dtb, alog=alog, fb=fb,
                          dsk=dsk, hp=hp, ob=ob, lse_b=lse_b, cum_col=cum_col, oc=oc, lse_c=lse_c, br=br, y_t=y_t))
        xcur = xnext

    dx, dx16, st = _final_loss(xcur, tgt2, final_norm_w[None])
    loss_part = st[2, 0]
    g_final = st[0]

    gsm = {k: [None] * depth for k in ("norm_w", "conv_w", "conv_b", "dt_bias", "a_log", "d_skip", "ssm_norm_w",
                                      "sinks", "f_bias", "gate_bias")}
    parts = [None] * depth
    pending = None
    for l in reversed(range(depth)):
        sv = saved[l]
        proj, ps = sv["proj"], sv["ps"]
        _, wp_l, wo_l, cw_l, gb_l = weights[l]
        dbr, dgates, merged_t, dgb, dy_a, do_b, dbz, do_c, dcz = _merge_bwd(dx16, wo_l, wp_l, sv["br"], proj, gb_l,
                                                                            sv["ob"], sv["oc"], l)
        g_wo = _matmul(merged_t, dx16, F32, f"dwout_{l}")
        g_wp = _matmul_batched(sv["y_t"], dbr, F32, f"dwproj_{l}")
        gsm["gate_bias"][l] = dgb[0:3]
        res = _ssm_bwd(proj, ps, sv["hp"], dy_a, cw_l, conv_b[l][None], sv["dtb"], sv["alog"], sv["dsk"],
                       ssm_norm_w[l][None], S, l, pending)
        dxbc, daz, dps_a, pgw, pg1, pgh = res[:6]
        if pending is not None:
            parts[l + 1] = res[6:]
        gsm["conv_w"][l], gsm["conv_b"][l] = pgw[0:4], pgw[4]
        gsm["ssm_norm_w"][l] = pg1[0]
        gsm["dt_bias"][l], gsm["a_log"][l], gsm["d_skip"][l] = pgh[0, :NH], pgh[1, :NH], pgh[2, :NH]
        dq_b, dsk_b = _swa_bwd_dq(proj, do_b, sv["ob"], sv["lse_b"], sinks[l], cos128, sin128, S, l)
        dk_b, dv_b = _swa_bwd_dkv(proj, do_b, sv["ob"], sv["lse_b"], cos128, sin128, S, l)
        gsm["sinks"][l] = dsk_b[:, :, 0].reshape(NH)
        dq_c, dk_c, dv_c, dcum_k, dcum_q = _fox_bwd(proj, do_c, sv["oc"], sv["cum_col"], sv["lse_c"], S, l)
        dcum_tm = (dcum_k.reshape(Bl, NH, S) + dcum_q.reshape(Bl, NH, S)).transpose(0, 2, 1).reshape(T, NH)
        dcum_pad = jnp.pad(dcum_tm, ((0, 0), (NH, LANES - 2 * NH)))
        df, dfb = _fox_cum_bwd(dcum_pad, ps, sv["fb"], S, l)
        gsm["f_bias"][l] = dfb[0, NH:2 * NH]
        dps16 = (dps_a + df).astype(BF16)
        pieces = (dxbc, daz, dq_b, dbz, dq_c, dk_c, dv_c, dcz, dgates, dk_b, dv_b)
        dw_pieces = [_matmul(sv["h_t"], pc, F32, f"dwin_{l}_{i}") for i, pc in enumerate(pieces)]
        dws = _matmul(sv["h_t"], dps16, F32, f"dwin_small_{l}")
        g_win = _join_w_in(dw_pieces, dws)
        if l == 0:
            plans = [scatter_plan(g_win[r0:r1], *((g_wp, g_wo) if r0 == 0 else ())) for r0, r1 in ROW_CHUNKS]
        else:
            plans, pending = [None] * len(ROW_CHUNKS), scatter_plan(g_win, g_wp, g_wo)
        dkv_b = jnp.concatenate([dk_b, dv_b], axis=1)
        res1 = _inproj_bwd_dx([(dxbc, OFF_XBC), (daz, OFF_AZ), (dq_b, OFF_BQ), (dbz, OFF_BZ)], sv["wmain"],
                              ("narrow", dps16, sv["wsmall"]), None, f"inproj_bwd_dh1_{l}", plans[0])
        res2 = _inproj_bwd_dx([(dq_c, OFF_CQ), (dk_c, OFF_CK), (dv_c, OFF_CV), (dcz, OFF_CZ)], sv["wmain"],
                              ("acc", res1[0]), None, f"inproj_bwd_dh2_{l}", plans[1])
        res3 = _inproj_bwd_dx([(dgates, OFF_G), (dkv_b, OFF_BK)], sv["wmain"], ("acc", res2[0]),
                              (sv["x"], norm_w[l][None], dx), f"inproj_bwd_dx_{l}", plans[2])
        dx, dx16, dnw = res3[:3]
        if l == 0:
            parts[0] = [jnp.concatenate([res1[1], res2[1], res3[3]], axis=1), res1[2], res1[3]]
        gsm["norm_w"][l] = dnw[0]

    big = {}
    for idx, (name, w, m, v) in enumerate((("w_in", w_in, m_w_in, v_w_in), ("w_proj", w_proj, m_w_proj, v_w_proj),
                                          ("w_out", w_out, m_w_out, v_w_out))):
        cols = w.shape[-1]
        res = _sum_adamw([parts[l][idx].reshape(NDEV, -1, cols) for l in range(depth)], w.reshape(depth, -1, cols),
                         m.reshape(depth, -1, cols), v.reshape(depth, -1, cols), f"adamw_{name}")
        big[name] = [r.reshape(w.shape) for r in res]

    small_names = ("norm_w", "conv_b", "dt_bias", "a_log", "d_skip", "ssm_norm_w", "sinks", "f_bias")
    small_parts = [jnp.stack(gsm[k]) for k in small_names] + [g_final, jnp.stack(gsm["conv_w"]),
                                                              jnp.stack(gsm["gate_bias"]), loss_part.reshape(1)]
    shapes = [a.shape for a in small_parts]
    summed = _unpack_rows(_all_reduce_small(_pack_rows(small_parts)), shapes)
    g_small = dict(zip(small_names, summed[:len(small_names)]))
    g_small["final_norm_w"] = summed[len(small_names)]
    g_small["conv_w"] = lax.dynamic_slice_in_dim(summed[len(small_names) + 1], me * csh, csh, axis=2)
    g_small["gate_bias"] = lax.dynamic_slice_in_dim(summed[len(small_names) + 2], me * gsh, gsh, axis=2)
    loss = summed[len(small_names) + 3][0]

    ws = dict(norm_w=norm_w, conv_w=conv_w, conv_b=conv_b, dt_bias=dt_bias, a_log=a_log, d_skip=d_skip,
              ssm_norm_w=ssm_norm_w, sinks=sinks, f_bias=f_bias, gate_bias=gate_bias, final_norm_w=final_norm_w)
    ms = dict(norm_w=m_norm_w, conv_w=m_conv_w, conv_b=m_conv_b, dt_bias=m_dt_bias, a_log=m_a_log, d_skip=m_d_skip,
              ssm_norm_w=m_ssm_norm_w, sinks=m_sinks, f_bias=m_f_bias, gate_bias=m_gate_bias,
              final_norm_w=m_final_norm_w)
    vs = dict(norm_w=v_norm_w, conv_w=v_conv_w, conv_b=v_conv_b, dt_bias=v_dt_bias, a_log=v_a_log, d_skip=v_d_skip,
              ssm_norm_w=v_ssm_norm_w, sinks=v_sinks, f_bias=v_f_bias, gate_bias=v_gate_bias,
              final_norm_w=v_final_norm_w)
    order = list(ws)
    oshapes = [ws[k].shape for k in order]
    res = _adamw_small(_pack_rows([g_small[k] for k in order]), _pack_rows([ws[k] for k in order]),
                       _pack_rows([ms[k] for k in order]), _pack_rows([vs[k] for k in order]))
    d_s, m_s, v_s = (dict(zip(order, _unpack_rows(r, oshapes))) for r in res)

    names = ("norm_w", "w_in", "conv_w", "conv_b", "dt_bias", "a_log", "d_skip", "ssm_norm_w", "sinks", "f_bias",
             "gate_bias", "w_proj", "w_out", "final_norm_w")
    grads, deltas, new_m, new_v = [], [], [], []
    for k in names:
        if k in big:
            g, d_, m_, v_ = big[k]
        else:
            g, d_, m_, v_ = g_small[k], d_s[k], m_s[k], v_s[k]
        grads.append(g)
        deltas.append(d_)
        new_m.append(m_)
        new_v.append(v_)
    return (loss, dx.reshape(Bl, S, D), *grads, *deltas, *new_m, *new_v)
```

```python
import functools
import math

import jax
import jax.numpy as jnp
from jax import lax
from jax.experimental import pallas as pl
from jax.experimental.pallas import tpu as pltpu

F32 = jnp.float32
BF16 = jnp.bfloat16
MESH = pl.DeviceIdType.MESH
NDEV = 8

D = 1024
NH = 16
HD = 64
NST = 128
NGRP = 4
LCH = 128
EPS = 1e-6
ROPE_THETA = 10000.0
SCALE = HD ** -0.5
NEG = -1e30

LANES = 128
VMEM_LIMIT = 56 * 1024 * 1024

OFF_XBC, OFF_AZ, OFF_BQ, OFF_BZ, OFF_CQ, OFF_CK, OFF_CV, OFF_CZ, OFF_G, OFF_BK, OFF_BV = (
    0, 2048, 3072, 4096, 5120, 6144, 7168, 8192, 9216, 12288, 12544)
NMAIN = 12800
NIN = 12832
NSH = NIN // NDEV

ROW_CHUNKS = ((0, 384), (384, 768), (768, 1024))

ADAM_LR, ADAM_B1, ADAM_B2, ADAM_EPS, ADAM_WD, ADAM_STEP = 0.001, 0.9, 0.999, 1e-08, 0.01, 10


def _cparams(dims=None, vmem=None):
    return pltpu.CompilerParams(dimension_semantics=dims, vmem_limit_bytes=vmem)


def _dot(a, b):
    return jnp.dot(a, b, preferred_element_type=F32)


def _dot_nt(a, b):
    return lax.dot_general(a, b, (((1,), (1,)), ((), ())), preferred_element_type=F32)


def _dot_tn(a, b):
    return lax.dot_general(a, b, (((0,), (0,)), ((), ())), preferred_element_type=F32)


def _dot_hi(a, b):
    return jnp.dot(a, b, precision=lax.Precision.HIGHEST, preferred_element_type=F32)


def _sigmoid(x):
    return 1.0 / (1.0 + jnp.exp(-x))


def _softplus(x):
    return jnp.maximum(x, 0.0) + jnp.log(1.0 + jnp.exp(-jnp.abs(x)))


def _lane_iota(n=LANES):
    return lax.broadcasted_iota(jnp.int32, (1, n), 1)


def _rot_half(x):
    first = (_lane_iota() % HD) < (HD // 2)
    return jnp.where(first, pltpu.roll(x, LANES - HD // 2, 1), pltpu.roll(x, HD // 2, 1))


def _head_sum(x, head):
    m = (_lane_iota() < HD) if head == 0 else (_lane_iota() >= HD)
    return jnp.sum(jnp.where(m, x, 0.0), axis=1, keepdims=True)


def _me_and_peers():
    x, y, c = lax.axis_index("x"), lax.axis_index("y"), lax.axis_index("c")
    me = 4 * x + 2 * y + c
    peers = []
    for k in range(1, NDEV):
        kx, ky, kc = (k >> 2) & 1, (k >> 1) & 1, k & 1
        px, py, pc = x ^ kx, y ^ ky, c ^ kc
        peers.append(((px, py, pc), 4 * px + 2 * py + pc))
    return me, peers


class _Comm:
    def __init__(self, kind, arrays):
        self.kind, self.arrays, self.n = kind, list(arrays), len(arrays)
        any_spec = pl.BlockSpec(memory_space=pl.ANY)
        self.in_specs = [any_spec] * self.n
        self.out_specs = [any_spec] * self.n
        self.out_shape = [jax.ShapeDtypeStruct(((NDEV,) + a.shape) if kind == "gather" else a.shape, a.dtype)
                          for a in self.arrays]
        self.scratch = [pltpu.SemaphoreType.DMA((self.n, NDEV - 1)), pltpu.SemaphoreType.DMA((self.n, NDEV - 1)),
                        pltpu.SemaphoreType.DMA((self.n,))]

    def copies(self, ins, outs, sems):
        send_sems, recv_sems, local_sems = sems
        me, peers = _me_and_peers()
        out = []
        for a in range(self.n):
            mine = ins[a] if self.kind == "gather" else ins[a].at[me]
            out.append(pltpu.make_async_copy(mine, outs[a].at[me], local_sems.at[a]))
            for k, (peer, pidx) in enumerate(peers):
                src = ins[a] if self.kind == "gather" else ins[a].at[pidx]
                out.append(pltpu.make_async_remote_copy(
                    src_ref=src, dst_ref=outs[a].at[me], send_sem=send_sems.at[a, k], recv_sem=recv_sems.at[a, k],
                    device_id=peer, device_id_type=MESH))
        return out

    def call(self, name):
        def body(*refs):
            cps = self.copies(refs[:self.n], refs[self.n:2 * self.n], refs[2 * self.n:])
            for cp in cps:
                cp.start()
            for cp in cps:
                cp.wait()

        return pl.pallas_call(body, name=name, out_shape=self.out_shape, in_specs=self.in_specs,
                              out_specs=self.out_specs, scratch_shapes=self.scratch)(*self.arrays)


def _gather_two_level(arrays, name):
    n = len(arrays)

    def body(*refs):
        ins, outs = refs[:n], refs[n:2 * n]
        send_sems, recv_sems, local_sems = refs[2 * n:]
        x, y, c = lax.axis_index("x"), lax.axis_index("y"), lax.axis_index("c")
        me, sibling = (x, y, c), (x, y, 1 - c)
        chips = [(1 - x, y), (x, 1 - y), (1 - x, 1 - y)]

        def slot(a, dev):
            return outs[a].at[4 * dev[0] + 2 * dev[1] + dev[2]]

        def copy(a, k, block, to, src=None):
            return pltpu.make_async_remote_copy(
                src_ref=slot(a, block) if src is None else src, dst_ref=slot(a, block),
                send_sem=send_sems.at[a, k], recv_sem=recv_sems.at[a, k], device_id=to, device_id_type=MESH)

        mine = [pltpu.make_async_copy(ins[a], slot(a, me), local_sems.at[a]) for a in range(n)]
        for cp in mine:
            cp.start()
        first = []
        for a in range(n):
            first.append(copy(a, 0, me, sibling, src=ins[a]))
            first += [copy(a, 1 + j, me, (*chip, c), src=ins[a]) for j, chip in enumerate(chips)]
        for cp in first:
            cp.start()
        passed = []
        for j, chip in enumerate(chips):
            for a in range(n):
                copy(a, 1 + j, (*chip, c), me).wait_recv()
                fwd = copy(a, 4 + j, (*chip, c), sibling)
                fwd.start()
                passed.append(fwd)
        for a in range(n):
            copy(a, 0, sibling, me).wait_recv()
            for j, chip in enumerate(chips):
                copy(a, 4 + j, (*chip, 1 - c), me).wait_recv()
        for cp in first + passed:
            cp.wait_send()
        for cp in mine:
            cp.wait()

    any_spec = pl.BlockSpec(memory_space=pl.ANY)
    return pl.pallas_call(
        body, name=name, out_shape=[jax.ShapeDtypeStruct((NDEV,) + a.shape, a.dtype) for a in arrays],
        in_specs=[any_spec] * n, out_specs=[any_spec] * n,
        scratch_shapes=[pltpu.SemaphoreType.DMA((n, NDEV - 1)), pltpu.SemaphoreType.DMA((n, NDEV - 1)),
                        pltpu.SemaphoreType.DMA((n,))])(*arrays)


def _hosted_call(body, comm, name, grid, in_specs, out_specs, out_shape, scratch, dims, operands):
    if comm is None:
        return pl.pallas_call(body, name=name, grid=grid, in_specs=in_specs, out_specs=out_specs, out_shape=out_shape,
                              scratch_shapes=scratch, compiler_params=_cparams(dims, VMEM_LIMIT))(*operands)
    n_in, n_out, n_scr, n = len(in_specs), len(out_specs), len(scratch), comm.n

    def hosted(*refs):
        hin, cin = refs[:n_in], refs[n_in:n_in + n]
        hout = refs[n_in + n:n_in + n + n_out]
        cout = refs[n_in + n + n_out:n_in + 2 * n + n_out]
        hscr = refs[n_in + 2 * n + n_out:n_in + 2 * n + n_out + n_scr]
        sems = refs[n_in + 2 * n + n_out + n_scr:]
        ids = [pl.program_id(a) for a in range(len(grid))]
        first = functools.reduce(jnp.logical_and, [i == 0 for i in ids])
        last = functools.reduce(jnp.logical_and, [i == g - 1 for i, g in zip(ids, grid)])

        @pl.when(first)
        def _():
            for cp in comm.copies(cin, cout, sems):
                cp.start()

        body(*hin, *hout, *hscr)

        @pl.when(last)
        def _():
            for cp in comm.copies(cin, cout, sems):
                cp.wait()

    return pl.pallas_call(
        hosted, name=name, grid=grid, in_specs=list(in_specs) + comm.in_specs,
        out_specs=list(out_specs) + comm.out_specs, out_shape=list(out_shape) + comm.out_shape,
        scratch_shapes=list(scratch) + comm.scratch,
        compiler_params=_cparams(("arbitrary",) * len(grid), VMEM_LIMIT))(*operands, *comm.arrays)


def _all_reduce_small(v):
    rows = v.shape[0]

    def body(v_ref, sum_ref, all_ref, send_sems, recv_sems):
        me, peers = _me_and_peers()
        all_ref[me] = v_ref[...]
        copies = []
        for k, (peer, _) in enumerate(peers):
            cp = pltpu.make_async_remote_copy(
                src_ref=v_ref, dst_ref=all_ref.at[me],
                send_sem=send_sems.at[k], recv_sem=recv_sems.at[k],
                device_id=peer, device_id_type=MESH)
            cp.start()
            copies.append(cp)
        for cp in copies:
            cp.wait()
        acc = all_ref[0]
        for d in range(1, NDEV):
            acc = acc + all_ref[d]
        sum_ref[...] = acc

    vm = pl.BlockSpec(memory_space=pltpu.VMEM)
    return pl.pallas_call(
        body, name="all_reduce_small",
        out_shape=jax.ShapeDtypeStruct((rows, LANES), F32),
        in_specs=[vm], out_specs=vm,
        scratch_shapes=[pltpu.VMEM((NDEV, rows, LANES), F32),
                        pltpu.SemaphoreType.DMA((NDEV - 1,)), pltpu.SemaphoreType.DMA((NDEV - 1,))],
    )(v)


def _adamw_math(w, g, m, v):
    m = ADAM_B1 * m + (1.0 - ADAM_B1) * g
    v = ADAM_B2 * v + (1.0 - ADAM_B2) * jnp.square(g)
    m_hat = m / (1.0 - ADAM_B1 ** ADAM_STEP)
    v_hat = v / (1.0 - ADAM_B2 ** ADAM_STEP)
    delta = -ADAM_LR * (m_hat / (jnp.sqrt(v_hat) + ADAM_EPS) + ADAM_WD * w)
    return delta, m, v


def _sum_adamw(parts, w, m, v, name):
    depth, rows, cols = w.shape
    tr = next(c for c in (256, 128, 64, 32, 16) if rows % c == 0)
    nb = rows // tr

    def body(*refs):
        p_refs, (w_ref, m_ref, v_ref, g_ref, d_ref, nm_ref, nv_ref) = refs[:depth], refs[depth:]
        l = pl.program_id(0)
        for ll in range(depth):
            @pl.when(l == ll)
            def _(ll=ll):
                g = p_refs[ll][0].astype(F32)
                for d in range(1, NDEV):
                    g = g + p_refs[ll][d].astype(F32)
                delta, nm, nv = _adamw_math(w_ref[0], g, m_ref[0], v_ref[0])
                g_ref[0] = g
                d_ref[0] = delta
                nm_ref[0] = nm
                nv_ref[0] = nv

    part = lambda ll: pl.BlockSpec((NDEV, tr, cols), lambda l, i, ll=ll: (0, jnp.where(l == ll, i, jnp.where(l < ll, 0, nb - 1)), 0))
    blk = pl.BlockSpec((1, tr, cols), lambda l, i: (l, i, 0))
    sds = jax.ShapeDtypeStruct((depth, rows, cols), F32)
    return pl.pallas_call(
        body, name=name, grid=(depth, nb),
        in_specs=[part(ll) for ll in range(depth)] + [blk, blk, blk],
        out_specs=[blk, blk, blk, blk], out_shape=[sds, sds, sds, sds],
        compiler_params=_cparams(("arbitrary", "arbitrary"), VMEM_LIMIT),
    )(*parts, w, m, v)


def _adamw_small(g, w, m, v):
    def body(g_ref, w_ref, m_ref, v_ref, d_ref, nm_ref, nv_ref):
        delta, nm, nv = _adamw_math(w_ref[...], g_ref[...], m_ref[...], v_ref[...])
        d_ref[...] = delta
        nm_ref[...] = nm
        nv_ref[...] = nv

    sds = jax.ShapeDtypeStruct(g.shape, F32)
    return pl.pallas_call(body, name="adamw_small", out_shape=[sds, sds, sds])(g, w, m, v)


def _matmul(a, b, out_dtype, name, tm=1024, tn=1024, tk=1024):
    M, K = a.shape
    N = b.shape[1]
    tm, tn, tk = min(tm, M), min(tn, N), min(tk, K)
    nk = K // tk

    def body(a_ref, b_ref, o_ref, acc):
        k = pl.program_id(2)

        @pl.when(k == 0)
        def _():
            acc[...] = jnp.zeros_like(acc)

        acc[...] += _dot(a_ref[...], b_ref[...])

        @pl.when(k == nk - 1)
        def _():
            o_ref[...] = acc[...].astype(out_dtype)

    return pl.pallas_call(
        body, name=name, grid=(M // tm, N // tn, nk),
        in_specs=[pl.BlockSpec((tm, tk), lambda i, j, k: (i, k)), pl.BlockSpec((tk, tn), lambda i, j, k: (k, j))],
        out_specs=pl.BlockSpec((tm, tn), lambda i, j, k: (i, j)),
        out_shape=jax.ShapeDtypeStruct((M, N), out_dtype),
        scratch_shapes=[pltpu.VMEM((tm, tn), F32)],
        compiler_params=_cparams(("parallel", "parallel", "arbitrary"), VMEM_LIMIT),
    )(a, b)


def _matmul_batched(a, b, out_dtype, name, tm=1024, tn=1024, tk=512):
    G, M, K = a.shape
    N = b.shape[2]
    tm, tn, tk = min(tm, M), min(tn, N), min(tk, K)
    nk = K // tk

    def body(a_ref, b_ref, o_ref, acc):
        k = pl.program_id(3)

        @pl.when(k == 0)
        def _():
            acc[...] = jnp.zeros_like(acc)

        acc[...] += _dot(a_ref[0], b_ref[0])

        @pl.when(k == nk - 1)
        def _():
            o_ref[0] = acc[...].astype(out_dtype)

    return pl.pallas_call(
        body, name=name, grid=(G, M // tm, N // tn, nk),
        in_specs=[pl.BlockSpec((1, tm, tk), lambda g, i, j, k: (g, i, k)),
                  pl.BlockSpec((1, tk, tn), lambda g, i, j, k: (g, k, j))],
        out_specs=pl.BlockSpec((1, tm, tn), lambda g, i, j, k: (g, i, j)),
        out_shape=jax.ShapeDtypeStruct((G, M, N), out_dtype),
        scratch_shapes=[pltpu.VMEM((tm, tn), F32)],
        compiler_params=_cparams(("parallel", "parallel", "parallel", "arbitrary"), VMEM_LIMIT),
    )(a, b)


def _inproj_fwd(x2, nw, wmain, wsmall, cos128, sin128, S, li, comm=None):
    T = x2.shape[0]
    tm, tn = min(2048, S), 512
    nj, npos = NMAIN // tn, S // tm
    jq0, jk = OFF_BQ // tn, OFF_BK // tn

    def body(x_ref, nw_ref, w_ref, ws_ref, cos_ref, sin_ref, proj_ref, ps_ref, ht_ref, h_scr):
        j = pl.program_id(1)

        @pl.when(j == 0)
        def _():
            x = x_ref[...]
            r = lax.rsqrt(jnp.mean(x * x, axis=-1, keepdims=True) + EPS)
            h = (x * r * nw_ref[...]).astype(BF16)
            h_scr[...] = h
            ht_ref[...] = h.T
            ps_ref[...] = _dot(h, ws_ref[...])

        acc = _dot(h_scr[...], w_ref[...])

        def roped(c):
            xc = acc[:, LANES * c:LANES * (c + 1)]
            return (xc * cos_ref[...] + _rot_half(xc) * sin_ref[...]).astype(BF16)

        def plain(c):
            return acc[:, LANES * c:LANES * (c + 1)].astype(BF16)

        is_q = jnp.logical_or(j == jq0, j == jq0 + 1)
        is_k = j == jk

        @pl.when(is_q)
        def _():
            for c in range(4):
                proj_ref[:, LANES * c:LANES * (c + 1)] = roped(c)

        @pl.when(is_k)
        def _():
            for c in range(4):
                proj_ref[:, LANES * c:LANES * (c + 1)] = roped(c) if c < 2 else plain(c)

        @pl.when(jnp.logical_not(jnp.logical_or(is_q, is_k)))
        def _():
            proj_ref[...] = acc.astype(BF16)

    return _hosted_call(
        body, comm, f"inproj_fwd_{li}", (T // tm, nj),
        in_specs=[pl.BlockSpec((tm, D), lambda i, j: (i, 0)),
                  pl.BlockSpec((1, D), lambda i, j: (0, 0)),
                  pl.BlockSpec((D, tn), lambda i, j: (0, j)),
                  pl.BlockSpec((D, LANES), lambda i, j: (0, 0)),
                  pl.BlockSpec((tm, LANES), lambda i, j: (i % npos, 0)),
                  pl.BlockSpec((tm, LANES), lambda i, j: (i % npos, 0))],
        out_specs=[pl.BlockSpec((tm, tn), lambda i, j: (i, j)),
                   pl.BlockSpec((tm, LANES), lambda i, j: (i, 0)),
                   pl.BlockSpec((D, tm), lambda i, j: (0, i))],
        out_shape=[jax.ShapeDtypeStruct((T, NMAIN), BF16), jax.ShapeDtypeStruct((T, LANES), F32),
                   jax.ShapeDtypeStruct((D, T), BF16)],
        scratch=[pltpu.VMEM((tm, D), BF16)], dims=("parallel", "arbitrary"),
        operands=(x2, nw, wmain, wsmall, cos128, sin128))


def _inproj_bwd_dx(segs, wmain, init, final, name, comm=None):
    T = segs[0][0].shape[0]
    tm = min(1024, T)
    tk = 1024 if all(a.shape[1] % 1024 == 0 and c % 1024 == 0 for a, c in segs) else 512
    ni = T // tm
    k0s, nks, c0s = [], [], []
    for arr, col0 in segs:
        k0s.append(sum(nks))
        nks.append(arr.shape[1] // tk)
        c0s.append(col0 // tk)
    nk = sum(nks)
    ns = len(segs)

    def in_range(k, s):
        return jnp.logical_and(k >= k0s[s], k < k0s[s] + nks[s])

    def wcol(i, k):
        g = 0
        for s in range(ns):
            g = g + jnp.where(in_range(k, s), c0s[s] + k - k0s[s], 0)
        return (0, g)

    n_init = 2 if init[0] == "narrow" else 1

    def body(*refs):
        seg_refs, w_ref = refs[:ns], refs[ns]
        init_refs = refs[ns + 1:ns + 1 + n_init]
        rest = refs[ns + 1 + n_init:]
        i, k = pl.program_id(0), pl.program_id(1)
        acc = rest[-1]

        @pl.when(k == 0)
        def _():
            if init[0] == "narrow":
                acc[...] = _dot_nt(init_refs[0][...], init_refs[1][...])
            else:
                acc[...] = init_refs[0][...]

        for s in range(ns):
            @pl.when(in_range(k, s))
            def _(s=s):
                acc[...] += _dot_nt(seg_refs[s][...], w_ref[...])

        if final is None:
            @pl.when(k == nk - 1)
            def _():
                rest[0][...] = acc[...]
        else:
            x_ref, nw_ref, dxo_ref, dx_ref, dx16_ref, dnw_ref = rest[:6]

            @pl.when(jnp.logical_and(i == 0, k == 0))
            def _():
                dnw_ref[...] = jnp.zeros_like(dnw_ref)

            @pl.when(k == nk - 1)
            def _():
                x = x_ref[...]
                r = lax.rsqrt(jnp.mean(x * x, axis=-1, keepdims=True) + EPS)
                dh = acc[...]
                g = dh * nw_ref[...]
                dx = dxo_ref[...] + r * g - x * (r * r * r) * jnp.mean(g * x, axis=-1, keepdims=True)
                dx_ref[...] = dx
                dx16_ref[...] = dx.astype(BF16)
                dnw_ref[0:1, :] += jnp.sum(dh * x * r, axis=0, keepdims=True)

    row = pl.BlockSpec((tm, D), lambda i, k: (i, 0))
    in_specs = [pl.BlockSpec((tm, tk), lambda i, k, s=s: (i, jnp.clip(k - k0s[s], 0, nks[s] - 1))) for s in range(ns)]
    in_specs.append(pl.BlockSpec((D, tk), wcol))
    operands = [a for a, _ in segs] + [wmain]
    if init[0] == "narrow":
        in_specs += [pl.BlockSpec((tm, LANES), lambda i, k: (i, 0)), pl.BlockSpec((D, LANES), lambda i, k: (0, 0))]
    else:
        in_specs.append(row)
    operands += list(init[1:])
    if final is None:
        out_specs, out_shape = [row], [jax.ShapeDtypeStruct((T, D), F32)]
    else:
        in_specs += [row, pl.BlockSpec((1, D), lambda i, k: (0, 0)), row]
        operands += list(final)
        out_specs = [row, row, pl.BlockSpec((8, D), lambda i, k: (0, 0))]
        out_shape = [jax.ShapeDtypeStruct((T, D), F32), jax.ShapeDtypeStruct((T, D), BF16),
                     jax.ShapeDtypeStruct((8, D), F32)]
    return _hosted_call(body, comm, name, (ni, nk), in_specs=in_specs, out_specs=out_specs, out_shape=out_shape,
                        scratch=[pltpu.VMEM((tm, D), F32)], dims=("arbitrary", "arbitrary"), operands=tuple(operands))


def _merge_fwd(ya, yb, yc, proj, gbias, wp, wout, x2, li):
    T = x2.shape[0]
    tm = min(512, T)
    gcol = OFF_G // D

    def body(ya_ref, yb_ref, yc_ref, g0_ref, g1_ref, g2_ref, gb_ref, wp_ref, wo_ref, x_ref, xn_ref, br_ref, yt_ref):
        merged = jnp.zeros((tm, D), F32)
        for i, (y_ref, g_ref) in enumerate(((ya_ref, g0_ref), (yb_ref, g1_ref), (yc_ref, g2_ref))):
            y = y_ref[...]
            yt_ref[i] = y.T
            br = _dot(y, wp_ref[i])
            br_ref[i] = br.astype(BF16)
            gate = _sigmoid(g_ref[...].astype(F32) + gb_ref[i:i + 1, :])
            merged = merged + gate * br
        xn_ref[...] = x_ref[...] + _dot(merged.astype(BF16), wo_ref[...])

    row = lambda c: pl.BlockSpec((tm, D), lambda i, c=c: (i, c))
    return pl.pallas_call(
        body, name=f"merge_fwd_{li}", grid=(T // tm,),
        in_specs=[row(0), row(0), row(0), row(gcol), row(gcol + 1), row(gcol + 2),
                  pl.BlockSpec((3, D), lambda i: (0, 0)),
                  pl.BlockSpec((3, D, D), lambda i: (0, 0, 0)),
                  pl.BlockSpec((D, D), lambda i: (0, 0)),
                  row(0)],
        out_specs=[row(0), pl.BlockSpec((3, tm, D), lambda i: (0, i, 0)), pl.BlockSpec((3, D, tm), lambda i: (0, 0, i))],
        out_shape=[jax.ShapeDtypeStruct((T, D), F32), jax.ShapeDtypeStruct((3, T, D), BF16),
                   jax.ShapeDtypeStruct((3, D, T), BF16)],
        compiler_params=_cparams(("parallel",), VMEM_LIMIT),
    )(ya, yb, yc, proj, proj, proj, gbias, wp, wout, x2)


def _merge_bwd(dxo16, wout, wp, br, proj, gbias, ob, oc, li):
    T = dxo16.shape[0]
    tm = min(256, T)
    gcol = OFF_G // D

    def body(dx_ref, wo_ref, wp_ref, br_ref, g0_ref, g1_ref, g2_ref, gb_ref, ob_ref, oc_ref, zb_ref, zc_ref,
             dbr_ref, dg_ref, mt_ref, dgb_ref, dya_ref, dob_ref, dzb_ref, doc_ref, dzc_ref):
        @pl.when(pl.program_id(0) == 0)
        def _():
            dgb_ref[...] = jnp.zeros_like(dgb_ref)

        dm = _dot_nt(dx_ref[...], wo_ref[...])
        merged = jnp.zeros((tm, D), F32)
        dys = []
        for i, g_ref in enumerate((g0_ref, g1_ref, g2_ref)):
            b = br_ref[i].astype(F32)
            gate = _sigmoid(g_ref[...].astype(F32) + gb_ref[i:i + 1, :])
            merged = merged + gate * b
            dbr = (dm * gate).astype(BF16)
            dbr_ref[i] = dbr
            dgate = dm * b * gate * (1.0 - gate)
            dg_ref[:, D * i:D * (i + 1)] = dgate.astype(BF16)
            dgb_ref[i:i + 1, :] += jnp.sum(dgate, axis=0, keepdims=True)
            dys.append(_dot_nt(dbr, wp_ref[i]))
        mt_ref[...] = merged.astype(BF16).T
        dya_ref[...] = dys[0].astype(BF16)
        for dy, o_ref, z_ref, do_ref, dz_ref in ((dys[1], ob_ref, zb_ref, dob_ref, dzb_ref),
                                                 (dys[2], oc_ref, zc_ref, doc_ref, dzc_ref)):
            z = z_ref[...].astype(F32)
            sg = _sigmoid(z)
            do_ref[...] = (dy * z * sg).astype(BF16)
            dz_ref[...] = (dy * o_ref[...].astype(F32) * sg * (1.0 + z * (1.0 - sg))).astype(BF16)

    row = lambda c: pl.BlockSpec((tm, D), lambda i, c=c: (i, c))
    sds = jax.ShapeDtypeStruct((T, D), BF16)
    return pl.pallas_call(
        body, name=f"merge_bwd_{li}", grid=(T // tm,),
        in_specs=[row(0), pl.BlockSpec((D, D), lambda i: (0, 0)), pl.BlockSpec((3, D, D), lambda i: (0, 0, 0)),
                  pl.BlockSpec((3, tm, D), lambda i: (0, i, 0)),
                  row(gcol), row(gcol + 1), row(gcol + 2),
                  pl.BlockSpec((3, D), lambda i: (0, 0)),
                  row(0), row(0), row(OFF_BZ // D), row(OFF_CZ // D)],
        out_specs=[pl.BlockSpec((3, tm, D), lambda i: (0, i, 0)),
                   pl.BlockSpec((tm, 3 * D), lambda i: (i, 0)),
                   pl.BlockSpec((D, tm), lambda i: (0, i)),
                   pl.BlockSpec((8, D), lambda i: (0, 0)),
                   row(0), row(0), row(0), row(0), row(0)],
        out_shape=[jax.ShapeDtypeStruct((3, T, D), BF16), jax.ShapeDtypeStruct((T, 3 * D), BF16),
                   jax.ShapeDtypeStruct((D, T), BF16), jax.ShapeDtypeStruct((8, D), F32), sds, sds, sds, sds, sds],
        compiler_params=_cparams(("arbitrary",), VMEM_LIMIT),
    )(dxo16, wout, wp, br, proj, proj, proj, gbias, ob, oc, proj, proj)


def _final_loss(x2, tgt, fw):
    T = x2.shape[0]
    tm = min(512, T)
    ni = T // tm

    def body(x_ref, t_ref, w_ref, dx_ref, dx16_ref, st_ref):
        i = pl.program_id(0)

        @pl.when(i == 0)
        def _():
            st_ref[...] = jnp.zeros_like(st_ref)

        x = x_ref[...]
        r = lax.rsqrt(jnp.mean(x * x, axis=-1, keepdims=True) + EPS)
        xh = x * r
        err = xh * w_ref[...] - t_ref[...]
        dy = err * (1.0 / D)
        g = dy * w_ref[...]
        dx = r * g - x * (r * r * r) * jnp.mean(g * x, axis=-1, keepdims=True)
        dx_ref[...] = dx
        dx16_ref[...] = dx.astype(BF16)
        st_ref[0:1, :] += jnp.sum(dy * xh, axis=0, keepdims=True)
        st_ref[1:2, :] += jnp.sum(err * err, axis=0, keepdims=True)

        @pl.when(i == ni - 1)
        def _():
            tot = jnp.sum(st_ref[1:2, :], axis=1, keepdims=True) * (0.5 / D)
            st_ref[2:3, :] = jnp.broadcast_to(tot, (1, D))

    row = pl.BlockSpec((tm, D), lambda i: (i, 0))
    return pl.pallas_call(
        body, name="final_loss", grid=(ni,),
        in_specs=[row, row, pl.BlockSpec((1, D), lambda i: (0, 0))],
        out_specs=[row, row, pl.BlockSpec((8, D), lambda i: (0, 0))],
        out_shape=[jax.ShapeDtypeStruct((T, D), F32), jax.ShapeDtypeStruct((T, D), BF16),
                   jax.ShapeDtypeStruct((8, D), F32)],
        compiler_params=_cparams(("arbitrary",), VMEM_LIMIT),
    )(x2, tgt, fw)


def _fox_cum(ps, fb_row, S, li):
    T = ps.shape[0]
    blk = min(4 * LCH, S)
    nb, nsub = S // blk, blk // LCH

    def body(ps_ref, fb_ref, cum_ref, carry):
        @pl.when(pl.program_id(1) == 0)
        def _():
            carry[...] = jnp.zeros_like(carry)

        r = lax.broadcasted_iota(jnp.int32, (LCH, LCH), 0)
        c = lax.broadcasted_iota(jnp.int32, (LCH, LCH), 1)
        tri = (r >= c).astype(F32)
        run = carry[0:1, :]
        for u in range(nsub):
            rows = slice(LCH * u, LCH * (u + 1))
            logf = -_softplus(-(ps_ref[rows, :] + fb_ref[...]))
            cum = _dot_hi(tri, logf) + run
            cum_ref[rows, :] = cum
            run = cum[LCH - 1:LCH, :]
        carry[0:1, :] = run

    return pl.pallas_call(
        body, name=f"fox_cum_{li}", grid=(T // S, nb),
        in_specs=[pl.BlockSpec((blk, LANES), lambda b, i: (b * nb + i, 0)),
                  pl.BlockSpec((1, LANES), lambda b, i: (0, 0))],
        out_specs=pl.BlockSpec((blk, LANES), lambda b, i: (b * nb + i, 0)),
        out_shape=jax.ShapeDtypeStruct((T, LANES), F32),
        scratch_shapes=[pltpu.VMEM((8, LANES), F32)],
        compiler_params=_cparams(("arbitrary", "arbitrary")),
    )(ps, fb_row)


def _fox_cum_bwd(dcum, ps, fb_row, S, li):
    T = ps.shape[0]
    rows_blk = min(4 * LCH, S)
    nb, nsub = S // rows_blk, rows_blk // LCH

    def body(dc_ref, ps_ref, fb_ref, df_ref, dfb_ref, carry):
        b, i = pl.program_id(0), pl.program_id(1)

        @pl.when(i == 0)
        def _():
            carry[...] = jnp.zeros_like(carry)

        @pl.when(jnp.logical_and(b == 0, i == 0))
        def _():
            dfb_ref[...] = jnp.zeros_like(dfb_ref)

        r = lax.broadcasted_iota(jnp.int32, (LCH, LCH), 0)
        c = lax.broadcasted_iota(jnp.int32, (LCH, LCH), 1)
        tri = (c >= r).astype(F32)
        lane = _lane_iota()
        live = jnp.logical_and(lane >= NH, lane < 2 * NH)
        run = carry[0:1, :]
        dfb = jnp.zeros((1, LANES), F32)
        for u in reversed(range(nsub)):
            rows = slice(LCH * u, LCH * (u + 1))
            dc = dc_ref[rows, :]
            dlogf = _dot_hi(tri, dc) + run
            run = run + jnp.sum(dc, axis=0, keepdims=True)
            df = jnp.where(live, dlogf * _sigmoid(-(ps_ref[rows, :] + fb_ref[...])), 0.0)
            df_ref[rows, :] = df
            dfb = dfb + jnp.sum(df, axis=0, keepdims=True)
        carry[0:1, :] = run
        dfb_ref[0:1, :] += dfb

    blk = pl.BlockSpec((rows_blk, LANES), lambda b, i: (b * nb + nb - 1 - i, 0))
    return pl.pallas_call(
        body, name=f"fox_cum_bwd_{li}", grid=(T // S, nb),
        in_specs=[blk, blk, pl.BlockSpec((1, LANES), lambda b, i: (0, 0))],
        out_specs=[blk, pl.BlockSpec((8, LANES), lambda b, i: (0, 0))],
        out_shape=[jax.ShapeDtypeStruct((T, LANES), F32), jax.ShapeDtypeStruct((8, LANES), F32)],
        scratch_shapes=[pltpu.VMEM((8, LANES), F32)],
        compiler_params=_cparams(("arbitrary", "arbitrary")),
    )(dcum, ps, fb_row)


def _fox_blocks(S):
    bq = min(512, S)
    return bq, S // bq


def _split3(c):
    hi = c.astype(BF16).astype(F32)
    r = c - hi
    mid = r.astype(BF16).astype(F32)
    return hi, mid, (r - mid).astype(BF16).astype(F32)


def _augment(x, parts, key_side, hh):
    lane = _lane_iota()
    b0 = HD if hh == 0 else 0
    p0, o0 = (b0 + 3, b0) if key_side else (b0, b0 + 3)
    out = jnp.where(jnp.logical_and(lane >= o0, lane < o0 + 3), 1.0, x)
    for t in range(3):
        out = jnp.where(lane == p0 + t, parts[t], out)
    return out.astype(BF16)


def _fox_fwd(proj, cum_col, S, li, comm=None):
    T = proj.shape[0]
    B = T // S
    bq, nq = _fox_blocks(S)
    qc, kc, vc, zc = OFF_CQ // LANES, OFF_CK // LANES, OFF_CV // LANES, OFF_CZ // LANES

    def body(q_ref, k_ref, v_ref, z_ref, cc_ref, y_ref, o_ref, lse_ref, kaug):
        i = pl.program_id(2)
        m0 = _lane_iota() < HD

        @pl.when(i == 0)
        def _():
            kf = k_ref[...].astype(F32)
            for hh in range(2):
                kaug[hh] = _augment(kf, _split3(-cc_ref[0, hh]), True, hh)

        q2 = q_ref[...].astype(F32) * SCALE
        rows_q = pl.ds(pl.multiple_of(i * bq, bq), bq)
        row = lax.broadcasted_iota(jnp.int32, (bq, bq), 0)
        col = lax.broadcasted_iota(jnp.int32, (bq, bq), 1)
        qa = [_augment(jnp.where(m0 if hh == 0 else jnp.logical_not(m0), q2, 0.0),
                       _split3(cc_ref[0, hh, rows_q, :]), False, hh) for hh in range(2)]

        def step(j, carry, masked):
            start = pl.multiple_of(j * bq, bq)
            v2 = v_ref[pl.ds(start, bq), :]
            out = []
            for hh in range(2):
                m, l, acc = carry[3 * hh:3 * hh + 3]
                s = _dot_nt(qa[hh], kaug[hh, pl.ds(start, bq), :])
                if masked:
                    s = jnp.where(row >= col, s, NEG)
                mn = jnp.maximum(m, jnp.max(s, axis=1, keepdims=True))
                alpha = jnp.exp(m - mn)
                p = jnp.exp(s - mn)
                out += [mn, alpha * l + jnp.sum(p, axis=1, keepdims=True), alpha * acc + _dot(p.astype(BF16), v2)]
            return tuple(out)

        init = (jnp.full((bq, 1), NEG, F32), jnp.zeros((bq, 1), F32), jnp.zeros((bq, LANES), F32)) * 2
        carry = step(i, lax.fori_loop(0, i, functools.partial(step, masked=False), init), True)
        outs = []
        for hh in range(2):
            m, l, acc = carry[3 * hh:3 * hh + 3]
            outs.append(acc / l)
            lse_ref[0, hh] = m + jnp.log(l)
        o2 = jnp.where(m0, outs[0], outs[1])
        z = z_ref[...].astype(F32)
        o_ref[...] = o2.astype(BF16)
        y_ref[...] = (o2 * z * _sigmoid(z)).astype(BF16)

    qblk = lambda c: pl.BlockSpec((bq, LANES), lambda b, p, i, c=c: (b * nq + i, c + p))
    sblk = lambda c: pl.BlockSpec((S, LANES), lambda b, p, i, c=c: (b, c + p))
    return _hosted_call(
        body, comm, f"fox_fwd_{li}", (B, NH // 2, nq),
        in_specs=[qblk(qc), sblk(kc), sblk(vc), qblk(zc),
                  pl.BlockSpec((1, 2, S, 1), lambda b, p, i: (b, p, 0, 0))],
        out_specs=[qblk(0), qblk(0), pl.BlockSpec((1, 2, bq, 1), lambda b, p, i: (b, p, i, 0))],
        out_shape=[jax.ShapeDtypeStruct((T, D), BF16), jax.ShapeDtypeStruct((T, D), BF16),
                   jax.ShapeDtypeStruct((B, NH, S, 1), F32)],
        scratch=[pltpu.VMEM((2, S, LANES), BF16)], dims=("parallel", "parallel", "arbitrary"),
        operands=(proj, proj, proj, proj, cum_col))


def _fox_bwd(proj, do, o, cum_col, lse, S, li):
    T = proj.shape[0]
    B = T // S
    bq, nq = _fox_blocks(S)
    qc, kc, vc = OFF_CQ // LANES, OFF_CK // LANES, OFF_CV // LANES

    def body(q_ref, k_ref, v_ref, do_ref, o_ref, cc_ref, lse_ref, dq_ref, dk_ref, dv_ref, dc_ref, dr_ref,
             dq_scr, dr_scr, qaug):
        j = pl.program_id(2)
        m0 = _lane_iota() < HD

        @pl.when(j == 0)
        def _():
            dq_scr[...] = jnp.zeros_like(dq_scr)
            dr_scr[...] = jnp.zeros_like(dr_scr)
            qf = q_ref[...].astype(F32) * SCALE
            for hh in range(2):
                sel = m0 if hh == 0 else jnp.logical_not(m0)
                qaug[hh] = _augment(jnp.where(sel, qf, 0.0), _split3(cc_ref[0, hh] - lse_ref[0, hh]), False, hh)

        k2 = k_ref[...]
        v2 = v_ref[...]
        zk = jnp.zeros_like(k2)
        kh = (jnp.where(m0, k2, zk), jnp.where(m0, zk, k2))
        kf = k2.astype(F32)
        rows_k = pl.ds(pl.multiple_of(j * bq, bq), bq)
        ka = [_augment(kf, _split3(-cc_ref[0, hh, rows_k, :]), True, hh) for hh in range(2)]
        row = lax.broadcasted_iota(jnp.int32, (bq, bq), 0)
        col = lax.broadcasted_iota(jnp.int32, (bq, bq), 1)

        def step(i, carry, masked):
            dk, dv, dc0, dc1 = carry
            dcs = [dc0, dc1]
            start = pl.multiple_of(i * bq, bq)
            q2 = q_ref[pl.ds(start, bq), :]
            do2 = do_ref[pl.ds(start, bq), :]
            prod = do2.astype(F32) * o_ref[pl.ds(start, bq), :].astype(F32)
            zq = jnp.zeros_like(q2)
            dq = jnp.zeros((bq, LANES), F32)
            for hh in range(2):
                sel = m0 if hh == 0 else jnp.logical_not(m0)
                qh = jnp.where(sel, q2, zq)
                doh = jnp.where(sel, do2, zq)
                delta = _head_sum(prod, hh)
                s = _dot_nt(qaug[hh, pl.ds(start, bq), :], ka[hh])
                if masked:
                    s = jnp.where(row >= col, s, NEG)
                p = jnp.exp(s)
                dp = _dot_nt(doh, v2)
                ds = p * (dp - delta)
                dcs[hh] = dcs[hh] - jnp.sum(ds, axis=0, keepdims=True)
                dr_scr[hh, pl.ds(start, bq), :] += jnp.sum(ds, axis=1, keepdims=True)
                dsb = ds.astype(BF16)
                dv = dv + _dot_tn(p.astype(BF16), doh)
                dk = dk + _dot_tn(dsb, qh)
                dq = dq + _dot(dsb, kh[hh])
            dq_scr[pl.ds(start, bq), :] += dq
            return dk, dv, dcs[0], dcs[1]

        zero = jnp.zeros((bq, LANES), F32)
        zrow = jnp.zeros((1, bq), F32)
        carry = step(j, (zero, zero, zrow, zrow), True)
        dk, dv, dc0, dc1 = lax.fori_loop(j + 1, nq, functools.partial(step, masked=False), carry)
        dk_ref[...] = (dk * SCALE).astype(BF16)
        dv_ref[...] = dv.astype(BF16)
        dc_ref[0, 0, 0] = dc0
        dc_ref[0, 1, 0] = dc1

        @pl.when(j == nq - 1)
        def _():
            dq_ref[...] = (dq_scr[...] * SCALE).astype(BF16)
            dr_ref[0] = dr_scr[...]

    sblk = lambda c: pl.BlockSpec((S, LANES), lambda b, p, j, c=c: (b, c + p))
    kblk = lambda c: pl.BlockSpec((bq, LANES), lambda b, p, j, c=c: (b * nq + j, c + p))
    col_spec = pl.BlockSpec((1, 2, S, 1), lambda b, p, j: (b, p, 0, 0))
    return pl.pallas_call(
        body, name=f"fox_bwd_{li}", grid=(B, NH // 2, nq),
        in_specs=[sblk(qc), kblk(kc), kblk(vc), sblk(0), sblk(0), col_spec, col_spec],
        out_specs=[sblk(0), kblk(0), kblk(0), pl.BlockSpec((1, 2, 1, 1, bq), lambda b, p, j: (b, p, j, 0, 0)),
                   col_spec],
        out_shape=[jax.ShapeDtypeStruct((T, D), BF16), jax.ShapeDtypeStruct((T, D), BF16),
                   jax.ShapeDtypeStruct((T, D), BF16), jax.ShapeDtypeStruct((B, NH, nq, 1, bq), F32),
                   jax.ShapeDtypeStruct((B, NH, S, 1), F32)],
        scratch_shapes=[pltpu.VMEM((S, LANES), F32), pltpu.VMEM((2, S, 1), F32), pltpu.VMEM((2, S, LANES), BF16)],
        compiler_params=_cparams(("parallel", "parallel", "arbitrary"), VMEM_LIMIT),
    )(proj, proj, proj, do, o, cum_col, lse)


def _swa_blocks(S):
    bq = min(512, S)
    return bq, S // bq, bq // LCH


def _dup_head(xw, kvl):
    m0 = _lane_iota() < HD
    a = jnp.where(m0 if kvl == 0 else jnp.logical_not(m0), xw, 0.0)
    return (a + pltpu.roll(a, HD, 1)).astype(BF16)


def _band(same_block):
    r = lax.broadcasted_iota(jnp.int32, (LCH, LCH), 0)
    c = lax.broadcasted_iota(jnp.int32, (LCH, LCH), 1)
    return (c <= r) if same_block else (c > r)


def _stack_heads(ref, rows, kvl):
    m0 = _lane_iota() < HD
    parts = []
    for ch in (2 * kvl, 2 * kvl + 1):
        x = ref[rows, LANES * ch:LANES * (ch + 1)]
        parts += [jnp.where(m0, x, jnp.zeros_like(x)), jnp.where(m0, jnp.zeros_like(x), x)]
    return jnp.concatenate(parts, axis=0)


def _stack_delta(do_ref, o_ref, rows, kvl, scale=None):
    parts = []
    for ch in (2 * kvl, 2 * kvl + 1):
        lanes = slice(LANES * ch, LANES * (ch + 1))
        prod = do_ref[rows, lanes].astype(F32) * o_ref[rows, lanes].astype(F32)
        parts += [_head_sum(prod, 0), _head_sum(prod, 1)]
    out = jnp.concatenate(parts, axis=0)
    return out if scale is None else out * scale


def _stack_cols(ref, rows, kvl):
    return jnp.concatenate([ref[0, 4 * kvl + t, rows, :] for t in range(4)], axis=0)


def _swa_fwd(proj, sinks, S, li):
    T = proj.shape[0]
    B = T // S
    bq, nq, nsub = _swa_blocks(S)
    nrow = S // LCH
    qc, zc, kc, vc = OFF_BQ // 512, OFF_BZ // 512, OFF_BK // LANES, OFF_BV // LANES

    def body(sk_ref, q_ref, z_ref, kp_ref, kc_ref, vp_ref, vc_ref, y_ref, o_ref, lse_ref):
        c, i = pl.program_id(0), pl.program_id(2)
        m0 = _lane_iota() < HD
        kw = jnp.concatenate([kp_ref[...].astype(F32), kc_ref[...].astype(F32)], axis=0)
        vw = jnp.concatenate([vp_ref[...].astype(F32), vc_ref[...].astype(F32)], axis=0)
        kd = (_dup_head(kw, 0), _dup_head(kw, 1))
        vd = (_dup_head(vw, 0), _dup_head(vw, 1))
        valid = jnp.concatenate([_band(False), _band(True)], axis=1)
        col = lax.broadcasted_iota(jnp.int32, (LCH, 2 * LCH), 1)
        valid_first = jnp.logical_and(valid, jnp.logical_or(col >= LCH, i > 0))
        valid4 = jnp.concatenate([valid] * 4, axis=0)
        valid4_first = jnp.concatenate([valid_first] * 4, axis=0)
        for r in range(nsub):
            rows = slice(LCH * r, LCH * (r + 1))
            msk = valid4_first if r == 0 else valid4
            for kvl in range(2):
                kwin = kd[kvl][LCH * r:LCH * (r + 2)]
                vwin = vd[kvl][LCH * r:LCH * (r + 2)]
                qs = _stack_heads(q_ref, rows, kvl)
                sink = jnp.concatenate([jnp.full((LCH, 1), sk_ref[8 * c + 4 * kvl + t], F32) for t in range(4)], axis=0)
                s = jnp.where(msk, _dot_nt(qs, kwin) * SCALE, NEG)
                m = jnp.maximum(jnp.max(s, axis=1, keepdims=True), sink)
                p = jnp.exp(s - m)
                l = jnp.sum(p, axis=1, keepdims=True) + jnp.exp(sink - m)
                os_ = _dot(p.astype(BF16), vwin) / l
                lse = m + jnp.log(l)
                for t in range(4):
                    lse_ref[0, 4 * kvl + t, rows, :] = lse[LCH * t:LCH * (t + 1)]
                for u in range(2):
                    lanes = slice(LANES * (2 * kvl + u), LANES * (2 * kvl + u + 1))
                    o2 = jnp.where(m0, os_[LCH * 2 * u:LCH * (2 * u + 1)], os_[LCH * (2 * u + 1):LCH * (2 * u + 2)])
                    z = z_ref[rows, lanes].astype(F32)
                    o_ref[rows, lanes] = o2.astype(BF16)
                    y_ref[rows, lanes] = (o2 * z * _sigmoid(z)).astype(BF16)

    wide = lambda cc: pl.BlockSpec((bq, 512), lambda c, b, i, cc=cc: (b * nq + i, cc + c))
    cur = lambda cc: pl.BlockSpec((bq, LANES), lambda c, b, i, cc=cc: (b * nq + i, cc + c))
    prev = lambda cc: pl.BlockSpec((LCH, LANES), lambda c, b, i, cc=cc: (b * nrow + jnp.maximum(i * nsub - 1, 0), cc + c))
    return pl.pallas_call(
        body, name=f"swa_fwd_{li}", grid=(2, B, nq),
        in_specs=[pl.BlockSpec(memory_space=pltpu.SMEM), wide(qc), wide(zc), prev(kc), cur(kc), prev(vc), cur(vc)],
        out_specs=[wide(0), wide(0), pl.BlockSpec((1, 8, bq, 1), lambda c, b, i: (b, c, i, 0))],
        out_shape=[jax.ShapeDtypeStruct((T, D), BF16), jax.ShapeDtypeStruct((T, D), BF16),
                   jax.ShapeDtypeStruct((B, NH, S, 1), F32)],
        compiler_params=_cparams(("parallel", "parallel", "parallel"), VMEM_LIMIT),
    )(sinks, proj, proj, proj, proj, proj, proj)


def _swa_bwd_dq(proj, do, o, lse, sinks, cos128, sin128, S, li):
    T = proj.shape[0]
    B = T // S
    bq, nq, nsub = _swa_blocks(S)
    nrow = S // LCH
    qc, kc, vc = OFF_BQ // 512, OFF_BK // LANES, OFF_BV // LANES

    def body(sk_ref, q_ref, do_ref, o_ref, lse_ref, kp_ref, kc_ref, vp_ref, vc_ref, cos_ref, sin_ref, dq_ref, dsk_ref):
        c, b, i = pl.program_id(0), pl.program_id(1), pl.program_id(2)

        @pl.when(jnp.logical_and(b == 0, i == 0))
        def _():
            dsk_ref[...] = jnp.zeros_like(dsk_ref)

        m0 = _lane_iota() < HD
        kw = jnp.concatenate([kp_ref[...].astype(F32), kc_ref[...].astype(F32)], axis=0)
        vw = jnp.concatenate([vp_ref[...].astype(F32), vc_ref[...].astype(F32)], axis=0)
        kd = (_dup_head(kw, 0), _dup_head(kw, 1))
        vd = (_dup_head(vw, 0), _dup_head(vw, 1))
        valid = jnp.concatenate([_band(False), _band(True)], axis=1)
        col = lax.broadcasted_iota(jnp.int32, (LCH, 2 * LCH), 1)
        valid_first = jnp.logical_and(valid, jnp.logical_or(col >= LCH, i > 0))
        dsk = [jnp.zeros((1, 1), F32) for _ in range(8)]
        valid4 = jnp.concatenate([valid] * 4, axis=0)
        valid4_first = jnp.concatenate([valid_first] * 4, axis=0)
        for r in range(nsub):
            rows = slice(LCH * r, LCH * (r + 1))
            msk = valid4_first if r == 0 else valid4
            for kvl in range(2):
                kwin = kd[kvl][LCH * r:LCH * (r + 2)]
                vwin = vd[kvl][LCH * r:LCH * (r + 2)]
                qs = _stack_heads(q_ref, rows, kvl)
                dos = _stack_heads(do_ref, rows, kvl)
                delta = _stack_delta(do_ref, o_ref, rows, kvl)
                lse = _stack_cols(lse_ref, rows, kvl)
                sink = jnp.concatenate([jnp.full((LCH, 1), sk_ref[8 * c + 4 * kvl + t], F32) for t in range(4)], axis=0)
                s = jnp.where(msk, _dot_nt(qs, kwin) * SCALE, NEG)
                p = jnp.exp(s - lse)
                ds = p * (_dot_nt(dos, vwin) - delta)
                dqs = _dot(ds.astype(BF16), kwin) * SCALE
                dsink = jnp.exp(sink - lse) * delta
                for t in range(4):
                    hl = 4 * kvl + t
                    dsk[hl] = dsk[hl] - jnp.sum(dsink[LCH * t:LCH * (t + 1)], axis=0, keepdims=True)
                for u in range(2):
                    lanes = slice(LANES * (2 * kvl + u), LANES * (2 * kvl + u + 1))
                    dq2 = jnp.where(m0, dqs[LCH * 2 * u:LCH * (2 * u + 1)], dqs[LCH * (2 * u + 1):LCH * (2 * u + 2)])
                    dq2 = dq2 * cos_ref[rows, :] - _rot_half(dq2) * sin_ref[rows, :]
                    dq_ref[rows, lanes] = dq2.astype(BF16)
        for hl in range(8):
            dsk_ref[0, hl:hl + 1, :] += jnp.broadcast_to(dsk[hl], (1, LANES))

    wide = lambda cc: pl.BlockSpec((bq, 512), lambda c, b, i, cc=cc: (b * nq + i, cc + c))
    cur = lambda cc: pl.BlockSpec((bq, LANES), lambda c, b, i, cc=cc: (b * nq + i, cc + c))
    prev = lambda cc: pl.BlockSpec((LCH, LANES), lambda c, b, i, cc=cc: (b * nrow + jnp.maximum(i * nsub - 1, 0), cc + c))
    pos = pl.BlockSpec((bq, LANES), lambda c, b, i: (i, 0))
    return pl.pallas_call(
        body, name=f"swa_bwd_dq_{li}", grid=(2, B, nq),
        in_specs=[pl.BlockSpec(memory_space=pltpu.SMEM), wide(qc), wide(0), wide(0),
                  pl.BlockSpec((1, 8, bq, 1), lambda c, b, i: (b, c, i, 0)),
                  prev(kc), cur(kc), prev(vc), cur(vc), pos, pos],
        out_specs=[wide(0), pl.BlockSpec((1, 8, LANES), lambda c, b, i: (c, 0, 0))],
        out_shape=[jax.ShapeDtypeStruct((T, D), BF16), jax.ShapeDtypeStruct((2, 8, LANES), F32)],
        compiler_params=_cparams(("arbitrary", "arbitrary", "arbitrary"), VMEM_LIMIT),
    )(sinks, proj, do, o, lse, proj, proj, proj, proj, cos128, sin128)


def _swa_bwd_dkv(proj, do, o, lse, cos128, sin128, S, li):
    T = proj.shape[0]
    B = T // S
    bk, nk, nsub = _swa_blocks(S)
    nrow = S // LCH
    qc, kc, vc = OFF_BQ // 512, OFF_BK // LANES, OFF_BV // LANES

    def body(q_ref, qn_ref, do_ref, don_ref, o_ref, on_ref, lse_ref, lsen_ref, k_ref, v_ref, cos_ref, sin_ref,
             dk_ref, dv_ref):
        j = pl.program_id(2)
        m0 = _lane_iota() < HD
        has_next = (j < nk - 1).astype(F32)
        kf = k_ref[...].astype(F32)
        vf = v_ref[...].astype(F32)
        kd = (_dup_head(kf, 0), _dup_head(kf, 1))
        vd = (_dup_head(vf, 0), _dup_head(vf, 1))
        lane = _lane_iota()

        def stat_rows(lse_r, do_r, o_r, rows, scale):
            a_lse = jnp.zeros((rows, LANES), F32)
            a_del = jnp.zeros((rows, LANES), F32)
            for ch in range(4):
                lanes = slice(LANES * ch, LANES * (ch + 1))
                prod = do_r[:, lanes].astype(F32) * o_r[:, lanes].astype(F32)
                for hh in range(2):
                    h = 2 * ch + hh
                    a_lse = jnp.where(lane == h, lse_r[0, h], a_lse)
                    a_del = jnp.where(lane == h, _head_sum(prod, hh), a_del)
            if scale is not None:
                a_del = a_del * scale
            return a_lse.T, a_del.T

        lse_t, del_t = stat_rows(lse_ref, do_ref, o_ref, bk, None)
        lsen_t, deln_t = stat_rows(lsen_ref, don_ref, on_ref, LCH, has_next)
        r_ = lax.broadcasted_iota(jnp.int32, (LCH, LCH), 0)
        c_ = lax.broadcasted_iota(jnp.int32, (LCH, LCH), 1)
        masks4 = (jnp.concatenate([r_ <= c_] * 4, axis=1), jnp.concatenate([r_ > c_] * 4, axis=1))
        for kr in range(nsub):
            krows = slice(LCH * kr, LCH * (kr + 1))
            dk = jnp.zeros((LCH, LANES), F32)
            dv = jnp.zeros((LCH, LANES), F32)
            for dq_blk in range(2):
                rq = kr + dq_blk
                nxt = rq == nsub
                qrows = slice(0, LCH) if nxt else slice(LCH * rq, LCH * (rq + 1))
                qr, dor = (qn_ref, don_ref) if nxt else (q_ref, do_ref)
                lt, dt_ = (lsen_t, deln_t) if nxt else (lse_t, del_t)
                for kvl in range(2):
                    qs = _stack_heads(qr, qrows, kvl)
                    dos = _stack_heads(dor, qrows, kvl)
                    if nxt:
                        dos = (dos.astype(F32) * has_next).astype(BF16)
                    lse_row = jnp.concatenate([lt[4 * kvl + t:4 * kvl + t + 1, qrows] for t in range(4)], axis=1)
                    del_row = jnp.concatenate([dt_[4 * kvl + t:4 * kvl + t + 1, qrows] for t in range(4)], axis=1)
                    st = jnp.where(masks4[dq_blk], _dot_nt(kd[kvl][krows], qs) * SCALE, NEG)
                    pt = jnp.exp(st - lse_row)
                    dst = pt * (_dot_nt(vd[kvl][krows], dos) - del_row)
                    dvc = _dot(pt.astype(BF16), dos)
                    dkc = _dot(dst.astype(BF16), qs) * SCALE
                    own = m0 if kvl == 0 else jnp.logical_not(m0)
                    dv = dv + jnp.where(own, dvc + pltpu.roll(dvc, HD, 1), 0.0)
                    dk = dk + jnp.where(own, dkc + pltpu.roll(dkc, HD, 1), 0.0)
            dk = dk * cos_ref[krows, :] - _rot_half(dk) * sin_ref[krows, :]
            dk_ref[krows, :] = dk.astype(BF16)
            dv_ref[krows, :] = dv.astype(BF16)

    wide = lambda cc: pl.BlockSpec((bk, 512), lambda c, b, j, cc=cc: (b * nk + j, cc + c))
    nxt = lambda cc: pl.BlockSpec((LCH, 512), lambda c, b, j, cc=cc: (b * nrow + jnp.minimum((j + 1) * nsub, nrow - 1), cc + c))
    cur = lambda cc: pl.BlockSpec((bk, LANES), lambda c, b, j, cc=cc: (b * nk + j, cc + c))
    pos = pl.BlockSpec((bk, LANES), lambda c, b, j: (j, 0))
    return pl.pallas_call(
        body, name=f"swa_bwd_dkv_{li}", grid=(2, B, nk),
        in_specs=[wide(qc), nxt(qc), wide(0), nxt(0), wide(0), nxt(0),
                  pl.BlockSpec((1, 8, bk, 1), lambda c, b, j: (b, c, j, 0)),
                  pl.BlockSpec((1, 8, LCH, 1), lambda c, b, j: (b, c, jnp.minimum((j + 1) * nsub, nrow - 1), 0)),
                  cur(kc), cur(vc), pos, pos],
        out_specs=[cur(0), cur(0)],
        out_shape=[jax.ShapeDtypeStruct((T, 2 * LANES), BF16), jax.ShapeDtypeStruct((T, 2 * LANES), BF16)],
        compiler_params=_cparams(("parallel", "parallel", "parallel"), VMEM_LIMIT),
    )(proj, proj, do, do, o, o, lse, lse, proj, proj, cos128, sin128)


HALO = 16


def _shift_matrices():
    r = lax.broadcasted_iota(jnp.int32, (3 * LCH, LCH + HALO), 0)
    c = lax.broadcasted_iota(jnp.int32, (3 * LCH, LCH + HALO), 1)
    t, d = r % LCH, r // LCH + 1
    return (c == HALO + t - d).astype(BF16), (c == t + d).astype(BF16)


def _ssm_chunk_pre(prev16, cur16, first, sdn_ref, cw_ref, cb_ref, ps, dtb, alog):
    ext16 = jnp.concatenate([jnp.where(first, jnp.zeros_like(prev16), prev16), cur16], axis=0)
    sh = _dot(sdn_ref[...], ext16)
    pre = cb_ref[...] + cw_ref[3:4, :] * cur16.astype(F32)
    for d in range(1, 4):
        pre = pre + cw_ref[3 - d:4 - d, :] * sh[LCH * (d - 1):LCH * d]
    sg = _sigmoid(pre)
    dt = _softplus(ps + dtb)
    a = -jnp.exp(alog)
    r = lax.broadcasted_iota(jnp.int32, (LCH, LCH), 0)
    c = lax.broadcasted_iota(jnp.int32, (LCH, LCH), 1)
    acum = _dot_hi((r >= c).astype(F32), dt * a)
    return pre, sg, dt, a, acum, sh


def _expand_matrix():
    r = lax.broadcasted_iota(jnp.int32, (3 * LANES, D), 0)
    c = lax.broadcasted_iota(jnp.int32, (3 * LANES, D), 1)
    return ((r % LANES) == c // HD).astype(BF16)


def _expand_heads(v, ex_ref):
    return _dot(jnp.concatenate(_split3(v), axis=1).astype(BF16), ex_ref[...])


def _decay(acum, acum_t, h):
    r = lax.broadcasted_iota(jnp.int32, (LCH, LCH), 0)
    c = lax.broadcasted_iota(jnp.int32, (LCH, LCH), 1)
    causal = r >= c
    seg = acum[:, h:h + 1] - acum_t[h:h + 1, :]
    return jnp.where(causal, jnp.exp(jnp.where(causal, seg, 0.0)), 0.0)


def _ssm_pair_fwd(p, x, dt_x, acum, acum_t, e_x, w_x, cd, cb_g, b_g, c_g, hprev, dsk_ref):
    m0 = _lane_iota() < HD
    lanes = slice(LANES * p, LANES * (p + 1))
    x2 = x[:, lanes]
    dt2 = dt_x[:, lanes]
    xdt2 = x2 * dt2
    xdtb = xdt2.astype(BF16)
    lms, ms, yds = [], [], []
    for hh in range(2):
        lm = _decay(acum, acum_t, 2 * p + hh)
        mm = cb_g * lm
        lms.append(lm)
        ms.append(mm)
        yds.append(_dot(mm.astype(BF16), xdtb))
    yd2 = jnp.where(m0, yds[0], yds[1])
    w2 = w_x[:, lanes]
    xw = (xdt2 * w2).astype(BF16)
    s2 = _dot_tn(xw, b_g)
    z2 = _dot_nt(c_g, hprev.astype(BF16))
    e2 = e_x[:, lanes]
    rowsel = lax.broadcasted_iota(jnp.int32, (LANES, 1), 0) < HD
    cdcol = jnp.where(rowsel, cd[:, 2 * p:2 * p + 1], cd[:, 2 * p + 1:2 * p + 2])
    y2 = yd2 + z2 * e2 + dsk_ref[:, lanes] * x2
    return dict(x2=x2, dt2=dt2, xdt2=xdt2, xdtb=xdtb, lms=lms, ms=ms, yd2=yd2, w2=w2, xw=xw, s2=s2, z2=z2, e2=e2,
                cdcol=cdcol, y2=y2)


def _ssm_specs(S, rev):
    nc = S // LCH
    ch = (lambda c: nc - 1 - c) if rev else (lambda c: c)
    prev = pl.BlockSpec((HALO, 2 * D), lambda b, c: (jnp.maximum(b * (S // HALO) + ch(c) * (LCH // HALO) - 1, 0), 0))
    cur = pl.BlockSpec((LCH, 2 * D), lambda b, c: (b * nc + ch(c), 0))
    zed = pl.BlockSpec((LCH, D), lambda b, c: (b * nc + ch(c), OFF_AZ // D))
    row = pl.BlockSpec((LCH, D), lambda b, c: (b * nc + ch(c), 0))
    psb = pl.BlockSpec((LCH, LANES), lambda b, c: (b * nc + ch(c), 0))
    hpb = pl.BlockSpec((1, 1, NH // 2, LANES, NST), lambda b, c: (b, ch(c), 0, 0, 0))
    const = lambda r, w: pl.BlockSpec((r, w), lambda b, c: (0, 0))
    return nc, prev, cur, zed, row, psb, hpb, const


def _ssm_fwd(proj, ps, cw, cb, dtb, alog, dsk, nw, S, li):
    T = proj.shape[0]
    B = T // S
    nc, prev, cur, zed, row, psb, hpb, const = _ssm_specs(S, False)

    def body(prev_ref, cur_ref, z_ref, ps_ref, sdn_ref, ex_ref, cw_ref, cb_ref, dtb_ref, alog_ref, dsk_ref, nw_ref,
             ya_ref, hp_ref, h_scr):
        c = pl.program_id(1)

        @pl.when(c == 0)
        def _():
            h_scr[...] = jnp.zeros_like(h_scr)

        pre, sg, dt, a, acum, _ = _ssm_chunk_pre(prev_ref[...], cur_ref[...], c == 0, sdn_ref, cw_ref, cb_ref,
                                                 ps_ref[...], dtb_ref[...], alog_ref[...])
        act = pre * sg
        acum_t = acum.T
        last = acum[LCH - 1:LCH, :]
        cd = jnp.exp(last)
        dt, e_all, w_all = (_expand_heads(v, ex_ref) for v in (dt, jnp.exp(acum), jnp.exp(last - acum)))
        x = act[:, :D]
        for g in range(NGRP):
            b_g = act[:, D + NST * g:D + NST * (g + 1)].astype(BF16)
            c_g = act[:, D + NGRP * NST + NST * g:D + NGRP * NST + NST * (g + 1)].astype(BF16)
            cb_g = _dot_nt(c_g, b_g)
            ygs = []
            for p in (2 * g, 2 * g + 1):
                hprev = h_scr[p]
                hp_ref[0, 0, p] = hprev
                f = _ssm_pair_fwd(p, x, dt, acum, acum_t, e_all, w_all, cd, cb_g, b_g, c_g, hprev, dsk_ref)
                h_scr[p] = hprev * f["cdcol"] + f["s2"]
                z2 = z_ref[:, LANES * p:LANES * (p + 1)].astype(F32)
                ygs.append(f["y2"] * z2 * _sigmoid(z2))
            yg = jnp.concatenate(ygs, axis=1)
            r = lax.rsqrt(jnp.mean(yg * yg, axis=1, keepdims=True) + EPS)
            ya_ref[:, 2 * LANES * g:2 * LANES * (g + 1)] = (yg * r * nw_ref[:, 2 * LANES * g:2 * LANES * (g + 1)]).astype(BF16)

    return pl.pallas_call(
        body, name=f"ssm_fwd_{li}", grid=(B, nc),
        in_specs=[prev, cur, zed, psb, const(3 * LCH, LCH + HALO), const(3 * LANES, D), const(4, 2 * D),
                  const(1, 2 * D), const(1, LANES), const(1, LANES), const(1, D), const(1, D)],
        out_specs=[row, hpb],
        out_shape=[jax.ShapeDtypeStruct((T, D), BF16), jax.ShapeDtypeStruct((B, nc, NH // 2, LANES, NST), F32)],
        scratch_shapes=[pltpu.VMEM((NH // 2, LANES, NST), F32)],
        compiler_params=_cparams(("arbitrary", "arbitrary"), VMEM_LIMIT),
    )(proj, proj, proj, ps, _shift_matrices()[0], _expand_matrix(), cw, cb, dtb, alog, dsk, nw)


def _ssm_bwd(proj, ps, hp, dya, cw, cb, dtb, alog, dsk, nw, S, li, comm=None):
    T = proj.shape[0]
    B = T // S
    nc, prev, cur, zed, row, psb, hpb, const = _ssm_specs(S, True)

    def body(prev_ref, cur_ref, z_ref, ps_ref, hp_ref, dy_ref, sdn_ref, sup_ref, ex_ref, cw_ref, cb_ref, dtb_ref,
             alog_ref, dsk_ref, nw_ref, dxbc_ref, dz_ref, dps_ref, pgw_ref, pg1_ref, pgh_ref, dh_scr, dhead, dact):
        b, cc = pl.program_id(0), pl.program_id(1)
        c = nc - 1 - cc

        @pl.when(jnp.logical_and(b == 0, cc == 0))
        def _():
            pgw_ref[...] = jnp.zeros_like(pgw_ref)
            pg1_ref[...] = jnp.zeros_like(pg1_ref)
            pgh_ref[...] = jnp.zeros_like(pgh_ref)

        @pl.when(cc == 0)
        def _():
            dh_scr[...] = jnp.zeros_like(dh_scr)
            dhead[...] = jnp.zeros_like(dhead)

        psv = ps_ref[...]
        cur16 = cur_ref[...]
        pre, sg, dt, a, acum, sh = _ssm_chunk_pre(prev_ref[...], cur16, c == 0, sdn_ref, cw_ref, cb_ref, psv,
                                                  dtb_ref[...], alog_ref[...])
        act = pre * sg
        acum_t = acum.T
        last = acum[LCH - 1:LCH, :]
        w_all = jnp.exp(last - acum)
        cd = jnp.exp(last)
        dt_x, e_x, w_x = (_expand_heads(v, ex_ref) for v in (dt, jnp.exp(acum), w_all))
        x = act[:, :D]
        lane = _lane_iota()
        m0 = lane < HD
        head_row = lax.broadcasted_iota(jnp.int32, (LANES, 1), 0)
        rowsel = head_row < HD
        is_last_row = lax.broadcasted_iota(jnp.int32, (LCH, 1), 0) == LCH - 1
        dacum_all = jnp.zeros((LCH, LANES), F32)
        dacum_t = jnp.zeros((LANES, LCH), F32)
        ddt_all = jnp.zeros((LCH, LANES), F32)
        dd_row = jnp.zeros((1, LANES), F32)
        for g in range(NGRP):
            b_g = act[:, D + NST * g:D + NST * (g + 1)].astype(BF16)
            c_g = act[:, D + NGRP * NST + NST * g:D + NGRP * NST + NST * (g + 1)].astype(BF16)
            cb_g = _dot_nt(c_g, b_g)
            pairs = (2 * g, 2 * g + 1)
            fs, hps, zs, ygs = [], [], [], []
            for p in pairs:
                hprev = hp_ref[0, 0, p]
                f = _ssm_pair_fwd(p, x, dt_x, acum, acum_t, e_x, w_x, cd, cb_g, b_g, c_g, hprev, dsk_ref)
                z2 = z_ref[:, LANES * p:LANES * (p + 1)].astype(F32)
                fs.append(f)
                hps.append(hprev)
                zs.append(z2)
                ygs.append(f["y2"] * z2 * _sigmoid(z2))
            gl = slice(2 * LANES * g, 2 * LANES * (g + 1))
            yg = jnp.concatenate(ygs, axis=1)
            r = lax.rsqrt(jnp.mean(yg * yg, axis=1, keepdims=True) + EPS)
            dyn = dy_ref[:, gl].astype(F32)
            gg = dyn * nw_ref[:, gl]
            dyg = r * gg - yg * (r * r * r) * jnp.mean(gg * yg, axis=1, keepdims=True)
            pg1_ref[0:1, gl] += jnp.sum(dyn * yg * r, axis=0, keepdims=True)
            dg_g = jnp.zeros((LCH, LCH), F32)
            db_g = jnp.zeros((LCH, NST), F32)
            dc_g = jnp.zeros((LCH, NST), F32)
            for idx, p in enumerate(pairs):
                f, hprev, z2 = fs[idx], hps[idx], zs[idx]
                lanes = slice(LANES * p, LANES * (p + 1))
                dyg2 = dyg[:, LANES * idx:LANES * (idx + 1)]
                sgz = _sigmoid(z2)
                dy2 = dyg2 * z2 * sgz
                dz_ref[:, lanes] = (dyg2 * f["y2"] * sgz * (1.0 + z2 * (1.0 - sgz))).astype(BF16)
                x2, dt2, xdt2, xdtb, w2, e2, z2m = f["x2"], f["dt2"], f["xdt2"], f["xdtb"], f["w2"], f["e2"], f["z2"]
                dx2 = dsk_ref[:, lanes] * dy2
                dyx = dy2 * x2
                dxdt2 = jnp.zeros((LCH, LANES), F32)
                diag_cols = []
                for hh in range(2):
                    sel = m0 if hh == 0 else jnp.logical_not(m0)
                    dyb = jnp.where(sel, dy2, 0.0).astype(BF16)
                    dm = _dot_nt(dyb, xdtb)
                    dg_g = dg_g + dm * f["lms"][hh]
                    dxdt2 = dxdt2 + _dot_tn(f["ms"][hh].astype(BF16), dyb)
                    em = dm * f["ms"][hh]
                    diag_cols.append(jnp.sum(em, axis=1, keepdims=True))
                    dacum_t = dacum_t - jnp.where(head_row == 2 * p + hh, jnp.sum(em, axis=0, keepdims=True), 0.0)
                dz2m = dy2 * e2
                t_off = dz2m * z2m
                dc_g = dc_g + _dot(dz2m.astype(BF16), hprev.astype(BF16))
                dhprev = _dot_tn(dz2m.astype(BF16), c_g)
                dhn = dh_scr[p]
                dhnb = dhn.astype(BF16)
                dhprev = dhprev + dhn * f["cdcol"]
                t_h = dhn * hprev
                dxw2 = _dot_nt(b_g, dhnb)
                db_g = db_g + _dot(f["xw"], dhnb)
                dxdt2 = dxdt2 + dxw2 * w2
                t_w = dxw2 * xdt2
                dx2 = dx2 + dxdt2 * dt2
                t_dt = dxdt2 * x2
                for hh in range(2):
                    h = 2 * p + hh
                    onehot = (lane == h).astype(F32)
                    w_col = w_all[:, h:h + 1]
                    dw_col = _head_sum(t_w, hh) * w_col
                    rs = rowsel if hh == 0 else jnp.logical_not(rowsel)
                    dlast = (jnp.sum(jnp.where(rs, t_h, 0.0), keepdims=True) * cd[:, h:h + 1]
                             + jnp.sum(dw_col, keepdims=True))
                    dacum_col = diag_cols[hh] + _head_sum(t_off, hh) - dw_col + jnp.where(is_last_row, dlast, 0.0)
                    dacum_all = dacum_all + dacum_col * onehot
                    ddt_all = ddt_all + _head_sum(t_dt, hh) * onehot
                    sel = m0 if hh == 0 else jnp.logical_not(m0)
                    dd_row = dd_row + jnp.sum(jnp.where(sel, dyx, 0.0), keepdims=True) * onehot
                dh_scr[p] = dhprev
                dact[:, lanes] = dx2
            dgb = dg_g.astype(BF16)
            dc_g = dc_g + _dot(dgb, b_g)
            db_g = db_g + _dot_tn(dgb, c_g)
            dact[:, D + NST * g:D + NST * (g + 1)] = db_g
            dact[:, D + NGRP * NST + NST * g:D + NGRP * NST + NST * (g + 1)] = dc_g
        rr = lax.broadcasted_iota(jnp.int32, (LCH, LCH), 0)
        cc2 = lax.broadcasted_iota(jnp.int32, (LCH, LCH), 1)
        dadt = _dot_hi((cc2 >= rr).astype(F32), dacum_all + dacum_t.T)
        ddt_all = ddt_all + dadt * a
        heads = lane < NH
        da = jnp.sum(dadt * dt, axis=0, keepdims=True)
        dr = jnp.where(heads, ddt_all * _sigmoid(psv + dtb_ref[...]), 0.0)
        dps_ref[...] = dr
        pgh_ref[0:1, :] += jnp.sum(dr, axis=0, keepdims=True)
        pgh_ref[1:2, :] += jnp.where(heads, da * a, 0.0)
        pgh_ref[2:3, :] += dd_row
        dpre = dact[...] * sg * (1.0 + pre * (1.0 - sg))
        extd = jnp.concatenate([dpre, dhead[...]], axis=0)
        hi = extd.astype(BF16)
        lo = (extd - hi.astype(F32)).astype(BF16)
        up = _dot(sup_ref[...], hi) + _dot(sup_ref[...], lo)
        du = cw_ref[3:4, :] * dpre
        pgw_ref[3:4, :] += jnp.sum(dpre * cur16.astype(F32), axis=0, keepdims=True)
        for d in range(1, 4):
            du = du + cw_ref[3 - d:4 - d, :] * up[LCH * (d - 1):LCH * d]
            pgw_ref[3 - d:4 - d, :] += jnp.sum(dpre * sh[LCH * (d - 1):LCH * d], axis=0, keepdims=True)
        pgw_ref[4:5, :] += jnp.sum(dpre, axis=0, keepdims=True)
        dxbc_ref[...] = du.astype(BF16)
        dhead[...] = dpre[0:HALO, :]

    xbc_out = pl.BlockSpec((LCH, 2 * D), lambda b, c: (b * nc + nc - 1 - c, 0))
    acc = lambda w: pl.BlockSpec((8, w), lambda b, c: (0, 0))
    sdn, sup = _shift_matrices()
    return _hosted_call(
        body, comm, f"ssm_bwd_{li}", (B, nc),
        in_specs=[prev, cur, zed, psb, hpb, row, const(3 * LCH, LCH + HALO), const(3 * LCH, LCH + HALO),
                  const(3 * LANES, D), const(4, 2 * D), const(1, 2 * D), const(1, LANES), const(1, LANES),
                  const(1, D), const(1, D)],
        out_specs=[xbc_out, row, psb, acc(2 * D), acc(D), acc(LANES)],
        out_shape=[jax.ShapeDtypeStruct((T, 2 * D), BF16), jax.ShapeDtypeStruct((T, D), BF16),
                   jax.ShapeDtypeStruct((T, LANES), F32), jax.ShapeDtypeStruct((8, 2 * D), F32),
                   jax.ShapeDtypeStruct((8, D), F32), jax.ShapeDtypeStruct((8, LANES), F32)],
        scratch=[pltpu.VMEM((NH // 2, LANES, NST), F32), pltpu.VMEM((HALO, 2 * D), F32),
                 pltpu.VMEM((LCH, 2 * D), F32)],
        dims=("arbitrary", "arbitrary"),
        operands=(proj, proj, proj, ps, hp, dya, sdn, sup, _expand_matrix(), cw, cb, dtb, alog, dsk, nw))


def _lane_row(v, offset):
    return jnp.pad(v.astype(F32), (offset, LANES - offset - v.shape[0]))[None]


def _pack_rows(arrays):
    parts = []
    for a in arrays:
        flat = a.reshape(-1).astype(F32)
        pad = (-flat.shape[0]) % LANES
        parts.append(jnp.pad(flat, (0, pad)))
    flat = jnp.concatenate(parts)
    pad = (-flat.shape[0]) % (8 * LANES)
    return jnp.pad(flat, (0, pad)).reshape(-1, LANES)


def _unpack_rows(pack, shapes):
    flat = pack.reshape(-1)
    out, pos = [], 0
    for shp in shapes:
        n = math.prod(shp)
        out.append(flat[pos:pos + n].reshape(shp))
        pos += n + (-n) % LANES
    return out


def _split_w_in(w):
    main = jnp.concatenate([w[:, 0:3072], w[:, 3088:4112], w[:, 4624:5648], w[:, 5648:8720], w[:, 8736:12832],
                            w[:, 4112:4624]], axis=1)
    small = jnp.concatenate([w[:, 3072:3088], w[:, 8720:8736], jnp.zeros((D, LANES - 2 * NH), w.dtype)], axis=1)
    return main, small


def _join_w_in(dw, ds):
    xbc, az, bq, bz, cq, ck, cv, cz, gates, bk, bv = dw
    return jnp.concatenate([xbc, az, ds[:, 0:NH], bq, bk, bv, bz, cq, ck, cv, ds[:, NH:2 * NH], cz, gates], axis=1)


def kernel(x, norm_w, w_in, conv_w, conv_b, dt_bias, a_log, d_skip, ssm_norm_w, sinks, f_bias, gate_bias, w_proj, w_out, final_norm_w, loss_target, m_norm_w, m_w_in, m_conv_w, m_conv_b, m_dt_bias, m_a_log, m_d_skip, m_ssm_norm_w, m_sinks, m_f_bias, m_gate_bias, m_w_proj, m_w_out, m_final_norm_w, v_norm_w, v_w_in, v_conv_w, v_conv_b, v_dt_bias, v_a_log, v_d_skip, v_ssm_norm_w, v_sinks, v_f_bias, v_gate_bias, v_w_proj, v_w_out, v_final_norm_w):
    Bl, S, _ = x.shape
    T = Bl * S
    depth = norm_w.shape[0]
    me = 4 * lax.axis_index("x") + 2 * lax.axis_index("y") + lax.axis_index("c")
    csh, gsh = conv_w.shape[2], gate_bias.shape[2]

    def gather_plan(l):
        small = jnp.concatenate([conv_w[l].reshape(-1), gate_bias[l].reshape(-1)]).reshape(-1, LANES)
        return _Comm("gather", [w_in[l].astype(BF16), w_proj[l].astype(BF16), w_out[l].astype(BF16), small])

    def unpack_weights(res):
        g_win, g_wp, g_wo, g_small = res
        flat = g_small.reshape(NDEV, -1)
        return (g_win.transpose(1, 0, 2).reshape(D, NIN),
                g_wp.transpose(1, 0, 2, 3).reshape(3, D, D),
                g_wo.reshape(D, D),
                flat[:, :4 * csh].reshape(NDEV, 4, csh).transpose(1, 0, 2).reshape(4, 2 * D),
                flat[:, 4 * csh:].reshape(NDEV, 3, gsh).transpose(1, 0, 2).reshape(3, D))

    def scatter_plan(gw_in, gw_p=None, gw_o=None):
        arrays = [gw_in.astype(BF16).reshape(-1, NDEV, NSH).transpose(1, 0, 2)]
        if gw_p is not None:
            arrays += [gw_p.astype(BF16).reshape(3, NDEV, D // NDEV, D).transpose(1, 0, 2, 3),
                       gw_o.astype(BF16).reshape(NDEV, D // NDEV, D)]
        return _Comm("scatter", arrays)

    pos = jnp.arange(S, dtype=F32)
    inv_freq = ROPE_THETA ** (-jnp.arange(0, HD, 2, dtype=F32) / HD)
    ang = pos[:, None] * inv_freq[None, :]
    cos128 = jnp.tile(jnp.cos(ang), (1, 4))
    sign = jnp.where((jnp.arange(LANES) % HD) < HD // 2, -1.0, 1.0).astype(F32)
    sin128 = jnp.tile(jnp.sin(ang), (1, 4)) * sign[None, :]

    bq, nq = _fox_blocks(S)
    x2 = x.reshape(T, D)
    tgt2 = loss_target.reshape(T, D)

    saved = []
    xcur = x2
    weights = [None] * depth
    weights[0] = unpack_weights(_gather_two_level(gather_plan(0).arrays, "gather_weights_0"))
    for l in range(depth):
        win_l, wp_l, wo_l, cw_l, gb_l = weights[l]
        wmain, wsmall = _split_w_in(win_l)
        proj, ps, h_t = _inproj_fwd(xcur, norm_w[l][None], wmain, wsmall, cos128, sin128, S, l)
        dtb = _lane_row(dt_bias[l], 0)
        alog = _lane_row(a_log[l], 0)
        fb = _lane_row(f_bias[l], NH)
        dsk = jnp.repeat(d_skip[l], HD)[None]
        ya, hp = _ssm_fwd(proj, ps, cw_l, conv_b[l][None], dtb, alog, dsk, ssm_norm_w[l][None], S, l)
        yb, ob, lse_b = _swa_fwd(proj, sinks[l], S, l)
        cum = _fox_cum(ps, fb, S, l)
        cumh = cum[:, NH:2 * NH].reshape(Bl, S, NH).transpose(0, 2, 1)
        cum_col = cumh[..., None]
        comm = gather_plan(l + 1) if l + 1 < depth else None
        res = _fox_fwd(proj, cum_col, S, l, comm)
        yc, oc, lse_c = res[:3]
        if comm is not None:
            weights[l + 1] = unpack_weights(res[3:])
        xnext, br, y_t = _merge_fwd(ya, yb, yc, proj, gb_l, wp_l, wo_l, xcur, l)
        saved.append(dict(x=xcur, wmain=wmain, wsmall=wsmall, proj=proj, ps=ps, h_t=h_t, dtb=dtb, alog=alog, fb=fb,
                          dsk=dsk, hp=hp, ob=ob, lse_b=lse_b, cum_col=cum_col, oc=oc, lse_c=lse_c, br=br, y_t=y_t))
        xcur = xnext

    dx, dx16, st = _final_loss(xcur, tgt2, final_norm_w[None])
    loss_part = st[2, 0]
    g_final = st[0]

    gsm = {k: [None] * depth for k in ("norm_w", "conv_w", "conv_b", "dt_bias", "a_log", "d_skip", "ssm_norm_w",
                                      "sinks", "f_bias", "gate_bias")}
    parts = [None] * depth
    pending = None
    for l in reversed(range(depth)):
        sv = saved[l]
        proj, ps = sv["proj"], sv["ps"]
        _, wp_l, wo_l, cw_l, gb_l = weights[l]
        dbr, dgates, merged_t, dgb, dy_a, do_b, dbz, do_c, dcz = _merge_bwd(dx16, wo_l, wp_l, sv["br"], proj, gb_l,
                                                                            sv["ob"], sv["oc"], l)
        g_wo = _matmul(merged_t, dx16, BF16, f"dwout_{l}")
        g_wp = _matmul_batched(sv["y_t"], dbr, BF16, f"dwproj_{l}")
        gsm["gate_bias"][l] = dgb[0:3]
        res = _ssm_bwd(proj, ps, sv["hp"], dy_a, cw_l, conv_b[l][None], sv["dtb"], sv["alog"], sv["dsk"],
                       ssm_norm_w[l][None], S, l, pending)
        dxbc, daz, dps_a, pgw, pg1, pgh = res[:6]
        if pending is not None:
            parts[l + 1] = res[6:]
        gsm["conv_w"][l], gsm["conv_b"][l] = pgw[0:4], pgw[4]
        gsm["ssm_norm_w"][l] = pg1[0]
        gsm["dt_bias"][l], gsm["a_log"][l], gsm["d_skip"][l] = pgh[0, :NH], pgh[1, :NH], pgh[2, :NH]
        dq_b, dsk_b = _swa_bwd_dq(proj, do_b, sv["ob"], sv["lse_b"], sinks[l], cos128, sin128, S, l)
        dk_b, dv_b = _swa_bwd_dkv(proj, do_b, sv["ob"], sv["lse_b"], cos128, sin128, S, l)
        gsm["sinks"][l] = dsk_b[:, :, 0].reshape(NH)
        dq_c, dk_c, dv_c, dcum_k, dcum_q = _fox_bwd(proj, do_c, sv["oc"], sv["cum_col"], sv["lse_c"], S, l)
        dcum_tm = (dcum_k.reshape(Bl, NH, S) + dcum_q.reshape(Bl, NH, S)).transpose(0, 2, 1).reshape(T, NH)
        dcum_pad = jnp.pad(dcum_tm, ((0, 0), (NH, LANES - 2 * NH)))
        df, dfb = _fox_cum_bwd(dcum_pad, ps, sv["fb"], S, l)
        gsm["f_bias"][l] = dfb[0, NH:2 * NH]
        dps16 = (dps_a + df).astype(BF16)
        pieces = (dxbc, daz, dq_b, dbz, dq_c, dk_c, dv_c, dcz, dgates, dk_b, dv_b)
        dw_pieces = [_matmul(sv["h_t"], pc, BF16, f"dwin_{l}_{i}") for i, pc in enumerate(pieces)]
        dws = _matmul(sv["h_t"], dps16, BF16, f"dwin_small_{l}")
        g_win = _join_w_in(dw_pieces, dws)
        if l == 0:
            plans = [scatter_plan(g_win[r0:r1], *((g_wp, g_wo) if r0 == 0 else ())) for r0, r1 in ROW_CHUNKS]
        else:
            plans, pending = [None] * len(ROW_CHUNKS), scatter_plan(g_win, g_wp, g_wo)
        dkv_b = jnp.concatenate([dk_b, dv_b], axis=1)
        res1 = _inproj_bwd_dx([(dxbc, OFF_XBC), (daz, OFF_AZ), (dq_b, OFF_BQ), (dbz, OFF_BZ)], sv["wmain"],
                              ("narrow", dps16, sv["wsmall"]), None, f"inproj_bwd_dh1_{l}", plans[0])
        res2 = _inproj_bwd_dx([(dq_c, OFF_CQ), (dk_c, OFF_CK), (dv_c, OFF_CV), (dcz, OFF_CZ)], sv["wmain"],
                              ("acc", res1[0]), None, f"inproj_bwd_dh2_{l}", plans[1])
        res3 = _inproj_bwd_dx([(dgates, OFF_G), (dkv_b, OFF_BK)], sv["wmain"], ("acc", res2[0]),
                              (sv["x"], norm_w[l][None], dx), f"inproj_bwd_dx_{l}", plans[2])
        dx, dx16, dnw = res3[:3]
        if l == 0:
            parts[0] = [jnp.concatenate([res1[1], res2[1], res3[3]], axis=1), res1[2], res1[3]]
        gsm["norm_w"][l] = dnw[0]

    big = {}
    for idx, (name, w, m, v) in enumerate((("w_in", w_in, m_w_in, v_w_in), ("w_proj", w_proj, m_w_proj, v_w_proj),
                                          ("w_out", w_out, m_w_out, v_w_out))):
        cols = w.shape[-1]
        res = _sum_adamw([parts[l][idx].reshape(NDEV, -1, cols) for l in range(depth)], w.reshape(depth, -1, cols),
                         m.reshape(depth, -1, cols), v.reshape(depth, -1, cols), f"adamw_{name}")
        big[name] = [r.reshape(w.shape) for r in res]

    small_names = ("norm_w", "conv_b", "dt_bias", "a_log", "d_skip", "ssm_norm_w", "sinks", "f_bias")
    small_parts = [jnp.stack(gsm[k]) for k in small_names] + [g_final, jnp.stack(gsm["conv_w"]),
                                                              jnp.stack(gsm["gate_bias"]), loss_part.reshape(1)]
    shapes = [a.shape for a in small_parts]
    summed = _unpack_rows(_all_reduce_small(_pack_rows(small_parts)), shapes)
    g_small = dict(zip(small_names, summed[:len(small_names)]))
    g_small["final_norm_w"] = summed[len(small_names)]
    g_small["conv_w"] = lax.dynamic_slice_in_dim(summed[len(small_names) + 1], me * csh, csh, axis=2)
    g_small["gate_bias"] = lax.dynamic_slice_in_dim(summed[len(small_names) + 2], me * gsh, gsh, axis=2)
    loss = summed[len(small_names) + 3][0]

    ws = dict(norm_w=norm_w, conv_w=conv_w, conv_b=conv_b, dt_bias=dt_bias, a_log=a_log, d_skip=d_skip,
              ssm_norm_w=ssm_norm_w, sinks=sinks, f_bias=f_bias, gate_bias=gate_bias, final_norm_w=final_norm_w)
    ms = dict(norm_w=m_norm_w, conv_w=m_conv_w, conv_b=m_conv_b, dt_bias=m_dt_bias, a_log=m_a_log, d_skip=m_d_skip,
              ssm_norm_w=m_ssm_norm_w, sinks=m_sinks, f_bias=m_f_bias, gate_bias=m_gate_bias,
              final_norm_w=m_final_norm_w)
    vs = dict(norm_w=v_norm_w, conv_w=v_conv_w, conv_b=v_conv_b, dt_bias=v_dt_bias, a_log=v_a_log, d_skip=v_d_skip,
              ssm_norm_w=v_ssm_norm_w, sinks=v_sinks, f_bias=v_f_bias, gate_bias=v_gate_bias,
              final_norm_w=v_final_norm_w)
    order = list(ws)
    oshapes = [ws[k].shape for k in order]
    res = _adamw_small(_pack_rows([g_small[k] for k in order]), _pack_rows([ws[k] for k in order]),
                       _pack_rows([ms[k] for k in order]), _pack_rows([vs[k] for k in order]))
    d_s, m_s, v_s = (dict(zip(order, _unpack_rows(r, oshapes))) for r in res)

    names = ("norm_w", "w_in", "conv_w", "conv_b", "dt_bias", "a_log", "d_skip", "ssm_norm_w", "sinks", "f_bias",
             "gate_bias", "w_proj", "w_out", "final_norm_w")
    grads, deltas, new_m, new_v = [], [], [], []
    for k in names:
        if k in big:
            g, d_, m_, v_ = big[k]
        else:
            g, d_, m_, v_ = g_small[k], d_s[k], m_s[k], v_s[k]
        grads.append(g)
        deltas.append(d_)
        new_m.append(m_)
        new_v.append(v_)
    return (loss, dx.reshape(Bl, S, D), *grads, *deltas, *new_m, *new_v)
```

```python
import functools
import math

import jax
import jax.numpy as jnp
from jax import lax
from jax.experimental import pallas as pl
from jax.experimental.pallas import tpu as pltpu

F32 = jnp.float32
BF16 = jnp.bfloat16
MESH = pl.DeviceIdType.MESH
NDEV = 8

D = 1024
NH = 16
HD = 64
NST = 128
NGRP = 4
LCH = 128
EPS = 1e-6
ROPE_THETA = 10000.0
SCALE = HD ** -0.5
NEG = -1e30

LANES = 128
VMEM_LIMIT = 56 * 1024 * 1024

OFF_XBC, OFF_AZ, OFF_BQ, OFF_BZ, OFF_CQ, OFF_CK, OFF_CV, OFF_CZ, OFF_G, OFF_BK, OFF_BV = (
    0, 2048, 3072, 4096, 5120, 6144, 7168, 8192, 9216, 12288, 12544)
NMAIN = 12800
NIN = 12832
NSH = NIN // NDEV

ROW_CHUNKS = ((0, 384), (384, 768), (768, 1024))

ADAM_LR, ADAM_B1, ADAM_B2, ADAM_EPS, ADAM_WD, ADAM_STEP = 0.001, 0.9, 0.999, 1e-08, 0.01, 10


def _cparams(dims=None, vmem=None):
    return pltpu.CompilerParams(dimension_semantics=dims, vmem_limit_bytes=vmem)


def _dot(a, b):
    return jnp.dot(a, b, preferred_element_type=F32)


def _dot_nt(a, b):
    return lax.dot_general(a, b, (((1,), (1,)), ((), ())), preferred_element_type=F32)


def _dot_tn(a, b):
    return lax.dot_general(a, b, (((0,), (0,)), ((), ())), preferred_element_type=F32)


def _dot_hi(a, b):
    return jnp.dot(a, b, precision=lax.Precision.HIGHEST, preferred_element_type=F32)


def _sigmoid(x):
    return 1.0 / (1.0 + jnp.exp(-x))


def _softplus(x):
    return jnp.maximum(x, 0.0) + jnp.log(1.0 + jnp.exp(-jnp.abs(x)))


def _lane_iota(n=LANES):
    return lax.broadcasted_iota(jnp.int32, (1, n), 1)


def _rot_half(x):
    first = (_lane_iota() % HD) < (HD // 2)
    return jnp.where(first, pltpu.roll(x, LANES - HD // 2, 1), pltpu.roll(x, HD // 2, 1))


def _head_sum(x, head):
    m = (_lane_iota() < HD) if head == 0 else (_lane_iota() >= HD)
    return jnp.sum(jnp.where(m, x, 0.0), axis=1, keepdims=True)


def _me_and_peers():
    x, y, c = lax.axis_index("x"), lax.axis_index("y"), lax.axis_index("c")
    me = 4 * x + 2 * y + c
    peers = []
    for k in range(1, NDEV):
        kx, ky, kc = (k >> 2) & 1, (k >> 1) & 1, k & 1
        px, py, pc = x ^ kx, y ^ ky, c ^ kc
        peers.append(((px, py, pc), 4 * px + 2 * py + pc))
    return me, peers


class _Comm:
    def __init__(self, kind, arrays):
        self.kind, self.arrays, self.n = kind, list(arrays), len(arrays)
        any_spec = pl.BlockSpec(memory_space=pl.ANY)
        self.in_specs = [any_spec] * self.n
        self.out_specs = [any_spec] * self.n
        self.out_shape = [jax.ShapeDtypeStruct(((NDEV,) + a.shape) if kind == "gather" else a.shape, a.dtype)
                          for a in self.arrays]
        self.scratch = [pltpu.SemaphoreType.DMA((self.n, NDEV - 1)), pltpu.SemaphoreType.DMA((self.n, NDEV - 1)),
                        pltpu.SemaphoreType.DMA((self.n,))]

    def copies(self, ins, outs, sems):
        send_sems, recv_sems, local_sems = sems
        me, peers = _me_and_peers()
        out = []
        for a in range(self.n):
            mine = ins[a] if self.kind == "gather" else ins[a].at[me]
            out.append(pltpu.make_async_copy(mine, outs[a].at[me], local_sems.at[a]))
            for k, (peer, pidx) in enumerate(peers):
                src = ins[a] if self.kind == "gather" else ins[a].at[pidx]
                out.append(pltpu.make_async_remote_copy(
                    src_ref=src, dst_ref=outs[a].at[me], send_sem=send_sems.at[a, k], recv_sem=recv_sems.at[a, k],
                    device_id=peer, device_id_type=MESH))
        return out

    def call(self, name):
        def body(*refs):
            cps = self.copies(refs[:self.n], refs[self.n:2 * self.n], refs[2 * self.n:])
            for cp in cps:
                cp.start()
            for cp in cps:
                cp.wait()

        return pl.pallas_call(body, name=name, out_shape=self.out_shape, in_specs=self.in_specs,
                              out_specs=self.out_specs, scratch_shapes=self.scratch)(*self.arrays)


def _gather_two_level(arrays, name):
    n = len(arrays)

    def body(*refs):
        ins, outs = refs[:n], refs[n:2 * n]
        send_sems, recv_sems, local_sems = refs[2 * n:]
        x, y, c = lax.axis_index("x"), lax.axis_index("y"), lax.axis_index("c")
        me, sibling = (x, y, c), (x, y, 1 - c)
        chips = [(1 - x, y), (x, 1 - y), (1 - x, 1 - y)]

        def slot(a, dev):
            return outs[a].at[4 * dev[0] + 2 * dev[1] + dev[2]]

        def copy(a, k, block, to, src=None):
            return pltpu.make_async_remote_copy(
                src_ref=slot(a, block) if src is None else src, dst_ref=slot(a, block),
                send_sem=send_sems.at[a, k], recv_sem=recv_sems.at[a, k], device_id=to, device_id_type=MESH)

        mine = [pltpu.make_async_copy(ins[a], slot(a, me), local_sems.at[a]) for a in range(n)]
        for cp in mine:
            cp.start()
        first = []
        for a in range(n):
            first.append(copy(a, 0, me, sibling, src=ins[a]))
            first += [copy(a, 1 + j, me, (*chip, c), src=ins[a]) for j, chip in enumerate(chips)]
        for cp in first:
            cp.start()
        passed = []
        for j, chip in enumerate(chips):
            for a in range(n):
                copy(a, 1 + j, (*chip, c), me).wait_recv()
                fwd = copy(a, 4 + j, (*chip, c), sibling)
                fwd.start()
                passed.append(fwd)
        for a in range(n):
            copy(a, 0, sibling, me).wait_recv()
            for j, chip in enumerate(chips):
                copy(a, 4 + j, (*chip, 1 - c), me).wait_recv()
        for cp in first + passed:
            cp.wait_send()
        for cp in mine:
            cp.wait()

    any_spec = pl.BlockSpec(memory_space=pl.ANY)
    return pl.pallas_call(
        body, name=name, out_shape=[jax.ShapeDtypeStruct((NDEV,) + a.shape, a.dtype) for a in arrays],
        in_specs=[any_spec] * n, out_specs=[any_spec] * n,
        scratch_shapes=[pltpu.SemaphoreType.DMA((n, NDEV - 1)), pltpu.SemaphoreType.DMA((n, NDEV - 1)),
                        pltpu.SemaphoreType.DMA((n,))])(*arrays)


def _hosted_call(body, comm, name, grid, in_specs, out_specs, out_shape, scratch, dims, operands):
    if comm is None:
        return pl.pallas_call(body, name=name, grid=grid, in_specs=in_specs, out_specs=out_specs, out_shape=out_shape,
                              scratch_shapes=scratch, compiler_params=_cparams(dims, VMEM_LIMIT))(*operands)
    n_in, n_out, n_scr, n = len(in_specs), len(out_specs), len(scratch), comm.n

    def hosted(*refs):
        hin, cin = refs[:n_in], refs[n_in:n_in + n]
        hout = refs[n_in + n:n_in + n + n_out]
        cout = refs[n_in + n + n_out:n_in + 2 * n + n_out]
        hscr = refs[n_in + 2 * n + n_out:n_in + 2 * n + n_out + n_scr]
        sems = refs[n_in + 2 * n + n_out + n_scr:]
        ids = [pl.program_id(a) for a in range(len(grid))]
        first = functools.reduce(jnp.logical_and, [i == 0 for i in ids])
        last = functools.reduce(jnp.logical_and, [i == g - 1 for i, g in zip(ids, grid)])

        @pl.when(first)
        def _():
            for cp in comm.copies(cin, cout, sems):
                cp.start()

        body(*hin, *hout, *hscr)

        @pl.when(last)
        def _():
            for cp in comm.copies(cin, cout, sems):
                cp.wait()

    return pl.pallas_call(
        hosted, name=name, grid=grid, in_specs=list(in_specs) + comm.in_specs,
        out_specs=list(out_specs) + comm.out_specs, out_shape=list(out_shape) + comm.out_shape,
        scratch_shapes=list(scratch) + comm.scratch,
        compiler_params=_cparams(("arbitrary",) * len(grid), VMEM_LIMIT))(*operands, *comm.arrays)


def _all_reduce_small(v):
    rows = v.shape[0]

    def body(v_ref, sum_ref, all_ref, send_sems, recv_sems):
        me, peers = _me_and_peers()
        all_ref[me] = v_ref[...]
        copies = []
        for k, (peer, _) in enumerate(peers):
            cp = pltpu.make_async_remote_copy(
                src_ref=v_ref, dst_ref=all_ref.at[me],
                send_sem=send_sems.at[k], recv_sem=recv_sems.at[k],
                device_id=peer, device_id_type=MESH)
            cp.start()
            copies.append(cp)
        for cp in copies:
            cp.wait()
        acc = all_ref[0]
        for d in range(1, NDEV):
            acc = acc + all_ref[d]
        sum_ref[...] = acc

    vm = pl.BlockSpec(memory_space=pltpu.VMEM)
    return pl.pallas_call(
        body, name="all_reduce_small",
        out_shape=jax.ShapeDtypeStruct((rows, LANES), F32),
        in_specs=[vm], out_specs=vm,
        scratch_shapes=[pltpu.VMEM((NDEV, rows, LANES), F32),
                        pltpu.SemaphoreType.DMA((NDEV - 1,)), pltpu.SemaphoreType.DMA((NDEV - 1,))],
    )(v)


def _adamw_math(w, g, m, v):
    m = ADAM_B1 * m + (1.0 - ADAM_B1) * g
    v = ADAM_B2 * v + (1.0 - ADAM_B2) * jnp.square(g)
    m_hat = m / (1.0 - ADAM_B1 ** ADAM_STEP)
    v_hat = v / (1.0 - ADAM_B2 ** ADAM_STEP)
    delta = -ADAM_LR * (m_hat / (jnp.sqrt(v_hat) + ADAM_EPS) + ADAM_WD * w)
    return delta, m, v


def _sum_adamw(parts, w, m, v, name):
    depth, rows, cols = w.shape
    tr = next(c for c in (256, 128, 64, 32, 16) if rows % c == 0)
    nb = rows // tr

    def body(*refs):
        p_refs, (w_ref, m_ref, v_ref, g_ref, d_ref, nm_ref, nv_ref) = refs[:depth], refs[depth:]
        l = pl.program_id(0)
        for ll in range(depth):
            @pl.when(l == ll)
            def _(ll=ll):
                g = p_refs[ll][0].astype(F32)
                for d in range(1, NDEV):
                    g = g + p_refs[ll][d].astype(F32)
                delta, nm, nv = _adamw_math(w_ref[0], g, m_ref[0], v_ref[0])
                g_ref[0] = g
                d_ref[0] = delta
                nm_ref[0] = nm
                nv_ref[0] = nv

    part = lambda ll: pl.BlockSpec((NDEV, tr, cols), lambda l, i, ll=ll: (0, jnp.where(l == ll, i, jnp.where(l < ll, 0, nb - 1)), 0))
    blk = pl.BlockSpec((1, tr, cols), lambda l, i: (l, i, 0))
    sds = jax.ShapeDtypeStruct((depth, rows, cols), F32)
    return pl.pallas_call(
        body, name=name, grid=(depth, nb),
        in_specs=[part(ll) for ll in range(depth)] + [blk, blk, blk],
        out_specs=[blk, blk, blk, blk], out_shape=[sds, sds, sds, sds],
        compiler_params=_cparams(("arbitrary", "arbitrary"), VMEM_LIMIT),
    )(*parts, w, m, v)


def _adamw_small(g, w, m, v):
    def body(g_ref, w_ref, m_ref, v_ref, d_ref, nm_ref, nv_ref):
        delta, nm, nv = _adamw_math(w_ref[...], g_ref[...], m_ref[...], v_ref[...])
        d_ref[...] = delta
        nm_ref[...] = nm
        nv_ref[...] = nv

    sds = jax.ShapeDtypeStruct(g.shape, F32)
    return pl.pallas_call(body, name="adamw_small", out_shape=[sds, sds, sds])(g, w, m, v)


def _matmul(a, b, out_dtype, name, tm=1024, tn=1024, tk=1024):
    M, K = a.shape
    N = b.shape[1]
    tm, tn, tk = min(tm, M), min(tn, N), min(tk, K)
    nk = K // tk

    def body(a_ref, b_ref, o_ref, acc):
        k = pl.program_id(2)

        @pl.when(k == 0)
        def _():
            acc[...] = jnp.zeros_like(acc)

        acc[...] += _dot(a_ref[...], b_ref[...])

        @pl.when(k == nk - 1)
        def _():
            o_ref[...] = acc[...].astype(out_dtype)

    return pl.pallas_call(
        body, name=name, grid=(M // tm, N // tn, nk),
        in_specs=[pl.BlockSpec((tm, tk), lambda i, j, k: (i, k)), pl.BlockSpec((tk, tn), lambda i, j, k: (k, j))],
        out_specs=pl.BlockSpec((tm, tn), lambda i, j, k: (i, j)),
        out_shape=jax.ShapeDtypeStruct((M, N), out_dtype),
        scratch_shapes=[pltpu.VMEM((tm, tn), F32)],
        compiler_params=_cparams(("parallel", "parallel", "arbitrary"), VMEM_LIMIT),
    )(a, b)


def _matmul_batched(a, b, out_dtype, name, tm=1024, tn=1024, tk=512):
    G, M, K = a.shape
    N = b.shape[2]
    tm, tn, tk = min(tm, M), min(tn, N), min(tk, K)
    nk = K // tk

    def body(a_ref, b_ref, o_ref, acc):
        k = pl.program_id(3)

        @pl.when(k == 0)
        def _():
            acc[...] = jnp.zeros_like(acc)

        acc[...] += _dot(a_ref[0], b_ref[0])

        @pl.when(k == nk - 1)
        def _():
            o_ref[0] = acc[...].astype(out_dtype)

    return pl.pallas_call(
        body, name=name, grid=(G, M // tm, N // tn, nk),
        in_specs=[pl.BlockSpec((1, tm, tk), lambda g, i, j, k: (g, i, k)),
                  pl.BlockSpec((1, tk, tn), lambda g, i, j, k: (g, k, j))],
        out_specs=pl.BlockSpec((1, tm, tn), lambda g, i, j, k: (g, i, j)),
        out_shape=jax.ShapeDtypeStruct((G, M, N), out_dtype),
        scratch_shapes=[pltpu.VMEM((tm, tn), F32)],
        compiler_params=_cparams(("parallel", "parallel", "parallel", "arbitrary"), VMEM_LIMIT),
    )(a, b)


def _inproj_fwd(x2, nw, wmain, wsmall, cos128, sin128, S, li, comm=None):
    T = x2.shape[0]
    tm, tn = min(2048, S), 512
    nj, npos = NMAIN // tn, S // tm
    jq0, jk = OFF_BQ // tn, OFF_BK // tn

    def body(x_ref, nw_ref, w_ref, ws_ref, cos_ref, sin_ref, proj_ref, ps_ref, ht_ref, h_scr):
        j = pl.program_id(1)

        @pl.when(j == 0)
        def _():
            x = x_ref[...]
            r = lax.rsqrt(jnp.mean(x * x, axis=-1, keepdims=True) + EPS)
            h = (x * r * nw_ref[...]).astype(BF16)
            h_scr[...] = h
            ht_ref[...] = h.T
            ps_ref[...] = _dot(h, ws_ref[...])

        acc = _dot(h_scr[...], w_ref[...])

        def roped(c):
            xc = acc[:, LANES * c:LANES * (c + 1)]
            return (xc * cos_ref[...] + _rot_half(xc) * sin_ref[...]).astype(BF16)

        def plain(c):
            return acc[:, LANES * c:LANES * (c + 1)].astype(BF16)

        is_q = jnp.logical_or(j == jq0, j == jq0 + 1)
        is_k = j == jk

        @pl.when(is_q)
        def _():
            for c in range(4):
                proj_ref[:, LANES * c:LANES * (c + 1)] = roped(c)

        @pl.when(is_k)
        def _():
            for c in range(4):
                proj_ref[:, LANES * c:LANES * (c + 1)] = roped(c) if c < 2 else plain(c)

        @pl.when(jnp.logical_not(jnp.logical_or(is_q, is_k)))
        def _():
            proj_ref[...] = acc.astype(BF16)

    return _hosted_call(
        body, comm, f"inproj_fwd_{li}", (T // tm, nj),
        in_specs=[pl.BlockSpec((tm, D), lambda i, j: (i, 0)),
                  pl.BlockSpec((1, D), lambda i, j: (0, 0)),
                  pl.BlockSpec((D, tn), lambda i, j: (0, j)),
                  pl.BlockSpec((D, LANES), lambda i, j: (0, 0)),
                  pl.BlockSpec((tm, LANES), lambda i, j: (i % npos, 0)),
                  pl.BlockSpec((tm, LANES), lambda i, j: (i % npos, 0))],
        out_specs=[pl.BlockSpec((tm, tn), lambda i, j: (i, j)),
                   pl.BlockSpec((tm, LANES), lambda i, j: (i, 0)),
                   pl.BlockSpec((D, tm), lambda i, j: (0, i))],
        out_shape=[jax.ShapeDtypeStruct((T, NMAIN), BF16), jax.ShapeDtypeStruct((T, LANES), F32),
                   jax.ShapeDtypeStruct((D, T), BF16)],
        scratch=[pltpu.VMEM((tm, D), BF16)], dims=("parallel", "arbitrary"),
        operands=(x2, nw, wmain, wsmall, cos128, sin128))


def _inproj_bwd_dx(segs, wmain, init, final, name, comm=None):
    T = segs[0][0].shape[0]
    tm = min(1024, T)
    tk = 1024 if all(a.shape[1] % 1024 == 0 and c % 1024 == 0 for a, c in segs) else 512
    ni = T // tm
    k0s, nks, c0s = [], [], []
    for arr, col0 in segs:
        k0s.append(sum(nks))
        nks.append(arr.shape[1] // tk)
        c0s.append(col0 // tk)
    nk = sum(nks)
    ns = len(segs)

    def in_range(k, s):
        return jnp.logical_and(k >= k0s[s], k < k0s[s] + nks[s])

    def wcol(i, k):
        g = 0
        for s in range(ns):
            g = g + jnp.where(in_range(k, s), c0s[s] + k - k0s[s], 0)
        return (0, g)

    n_init = 2 if init[0] == "narrow" else 1

    def body(*refs):
        seg_refs, w_ref = refs[:ns], refs[ns]
        init_refs = refs[ns + 1:ns + 1 + n_init]
        rest = refs[ns + 1 + n_init:]
        i, k = pl.program_id(0), pl.program_id(1)
        acc = rest[-1]

        @pl.when(k == 0)
        def _():
            if init[0] == "narrow":
                acc[...] = _dot_nt(init_refs[0][...], init_refs[1][...])
            else:
                acc[...] = init_refs[0][...]

        for s in range(ns):
            @pl.when(in_range(k, s))
            def _(s=s):
                acc[...] += _dot_nt(seg_refs[s][...], w_ref[...])

        if final is None:
            @pl.when(k == nk - 1)
            def _():
                rest[0][...] = acc[...]
        else:
            x_ref, nw_ref, dxo_ref, dx_ref, dx16_ref, dnw_ref = rest[:6]

            @pl.when(jnp.logical_and(i == 0, k == 0))
            def _():
                dnw_ref[...] = jnp.zeros_like(dnw_ref)

            @pl.when(k == nk - 1)
            def _():
                x = x_ref[...]
                r = lax.rsqrt(jnp.mean(x * x, axis=-1, keepdims=True) + EPS)
                dh = acc[...]
                g = dh * nw_ref[...]
                dx = dxo_ref[...] + r * g - x * (r * r * r) * jnp.mean(g * x, axis=-1, keepdims=True)
                dx_ref[...] = dx
                dx16_ref[...] = dx.astype(BF16)
                dnw_ref[0:1, :] += jnp.sum(dh * x * r, axis=0, keepdims=True)

    row = pl.BlockSpec((tm, D), lambda i, k: (i, 0))
    in_specs = [pl.BlockSpec((tm, tk), lambda i, k, s=s: (i, jnp.clip(k - k0s[s], 0, nks[s] - 1))) for s in range(ns)]
    in_specs.append(pl.BlockSpec((D, tk), wcol))
    operands = [a for a, _ in segs] + [wmain]
    if init[0] == "narrow":
        in_specs += [pl.BlockSpec((tm, LANES), lambda i, k: (i, 0)), pl.BlockSpec((D, LANES), lambda i, k: (0, 0))]
    else:
        in_specs.append(row)
    operands += list(init[1:])
    if final is None:
        out_specs, out_shape = [row], [jax.ShapeDtypeStruct((T, D), F32)]
    else:
        in_specs += [row, pl.BlockSpec((1, D), lambda i, k: (0, 0)), row]
        operands += list(final)
        out_specs = [row, row, pl.BlockSpec((8, D), lambda i, k: (0, 0))]
        out_shape = [jax.ShapeDtypeStruct((T, D), F32), jax.ShapeDtypeStruct((T, D), BF16),
                     jax.ShapeDtypeStruct((8, D), F32)]
    return _hosted_call(body, comm, name, (ni, nk), in_specs=in_specs, out_specs=out_specs, out_shape=out_shape,
                        scratch=[pltpu.VMEM((tm, D), F32)], dims=("arbitrary", "arbitrary"), operands=tuple(operands))


def _merge_fwd(ya, yb, yc, proj, gbias, wp, wout, x2, li):
    T = x2.shape[0]
    tm = min(512, T)
    gcol = OFF_G // D

    def body(ya_ref, yb_ref, yc_ref, g0_ref, g1_ref, g2_ref, gb_ref, wp_ref, wo_ref, x_ref, xn_ref, br_ref, yt_ref):
        merged = jnp.zeros((tm, D), F32)
        for i, (y_ref, g_ref) in enumerate(((ya_ref, g0_ref), (yb_ref, g1_ref), (yc_ref, g2_ref))):
            y = y_ref[...]
            yt_ref[i] = y.T
            br = _dot(y, wp_ref[i])
            br_ref[i] = br.astype(BF16)
            gate = _sigmoid(g_ref[...].astype(F32) + gb_ref[i:i + 1, :])
            merged = merged + gate * br
        xn_ref[...] = x_ref[...] + _dot(merged.astype(BF16), wo_ref[...])

    row = lambda c: pl.BlockSpec((tm, D), lambda i, c=c: (i, c))
    return pl.pallas_call(
        body, name=f"merge_fwd_{li}", grid=(T // tm,),
        in_specs=[row(0), row(0), row(0), row(gcol), row(gcol + 1), row(gcol + 2),
                  pl.BlockSpec((3, D), lambda i: (0, 0)),
                  pl.BlockSpec((3, D, D), lambda i: (0, 0, 0)),
                  pl.BlockSpec((D, D), lambda i: (0, 0)),
                  row(0)],
        out_specs=[row(0), pl.BlockSpec((3, tm, D), lambda i: (0, i, 0)), pl.BlockSpec((3, D, tm), lambda i: (0, 0, i))],
        out_shape=[jax.ShapeDtypeStruct((T, D), F32), jax.ShapeDtypeStruct((3, T, D), BF16),
                   jax.ShapeDtypeStruct((3, D, T), BF16)],
        compiler_params=_cparams(("parallel",), VMEM_LIMIT),
    )(ya, yb, yc, proj, proj, proj, gbias, wp, wout, x2)


def _merge_bwd(dxo16, wout, wp, br, proj, gbias, ob, oc, li):
    T = dxo16.shape[0]
    tm = min(256, T)
    gcol = OFF_G // D

    def body(dx_ref, wo_ref, wp_ref, br_ref, g0_ref, g1_ref, g2_ref, gb_ref, ob_ref, oc_ref, zb_ref, zc_ref,
             dbr_ref, dg_ref, mt_ref, dgb_ref, dya_ref, dob_ref, dzb_ref, doc_ref, dzc_ref):
        @pl.when(pl.program_id(0) == 0)
        def _():
            dgb_ref[...] = jnp.zeros_like(dgb_ref)

        dm = _dot_nt(dx_ref[...], wo_ref[...])
        merged = jnp.zeros((tm, D), F32)
        dys = []
        for i, g_ref in enumerate((g0_ref, g1_ref, g2_ref)):
            b = br_ref[i].astype(F32)
            gate = _sigmoid(g_ref[...].astype(F32) + gb_ref[i:i + 1, :])
            merged = merged + gate * b
            dbr = (dm * gate).astype(BF16)
            dbr_ref[i] = dbr
            dgate = dm * b * gate * (1.0 - gate)
            dg_ref[:, D * i:D * (i + 1)] = dgate.astype(BF16)
            dgb_ref[i:i + 1, :] += jnp.sum(dgate, axis=0, keepdims=True)
            dys.append(_dot_nt(dbr, wp_ref[i]))
        mt_ref[...] = merged.astype(BF16).T
        dya_ref[...] = dys[0].astype(BF16)
        for dy, o_ref, z_ref, do_ref, dz_ref in ((dys[1], ob_ref, zb_ref, dob_ref, dzb_ref),
                                                 (dys[2], oc_ref, zc_ref, doc_ref, dzc_ref)):
            z = z_ref[...].astype(F32)
            sg = _sigmoid(z)
            do_ref[...] = (dy * z * sg).astype(BF16)
            dz_ref[...] = (dy * o_ref[...].astype(F32) * sg * (1.0 + z * (1.0 - sg))).astype(BF16)

    row = lambda c: pl.BlockSpec((tm, D), lambda i, c=c: (i, c))
    sds = jax.ShapeDtypeStruct((T, D), BF16)
    return pl.pallas_call(
        body, name=f"merge_bwd_{li}", grid=(T // tm,),
        in_specs=[row(0), pl.BlockSpec((D, D), lambda i: (0, 0)), pl.BlockSpec((3, D, D), lambda i: (0, 0, 0)),
                  pl.BlockSpec((3, tm, D), lambda i: (0, i, 0)),
                  row(gcol), row(gcol + 1), row(gcol + 2),
                  pl.BlockSpec((3, D), lambda i: (0, 0)),
                  row(0), row(0), row(OFF_BZ // D), row(OFF_CZ // D)],
        out_specs=[pl.BlockSpec((3, tm, D), lambda i: (0, i, 0)),
                   pl.BlockSpec((tm, 3 * D), lambda i: (i, 0)),
                   pl.BlockSpec((D, tm), lambda i: (0, i)),
                   pl.BlockSpec((8, D), lambda i: (0, 0)),
                   row(0), row(0), row(0), row(0), row(0)],
        out_shape=[jax.ShapeDtypeStruct((3, T, D), BF16), jax.ShapeDtypeStruct((T, 3 * D), BF16),
                   jax.ShapeDtypeStruct((D, T), BF16), jax.ShapeDtypeStruct((8, D), F32), sds, sds, sds, sds, sds],
        compiler_params=_cparams(("arbitrary",), VMEM_LIMIT),
    )(dxo16, wout, wp, br, proj, proj, proj, gbias, ob, oc, proj, proj)


def _final_loss(x2, tgt, fw):
    T = x2.shape[0]
    tm = min(512, T)
    ni = T // tm

    def body(x_ref, t_ref, w_ref, dx_ref, dx16_ref, st_ref):
        i = pl.program_id(0)

        @pl.when(i == 0)
        def _():
            st_ref[...] = jnp.zeros_like(st_ref)

        x = x_ref[...]
        r = lax.rsqrt(jnp.mean(x * x, axis=-1, keepdims=True) + EPS)
        xh = x * r
        err = xh * w_ref[...] - t_ref[...]
        dy = err * (1.0 / D)
        g = dy * w_ref[...]
        dx = r * g - x * (r * r * r) * jnp.mean(g * x, axis=-1, keepdims=True)
        dx_ref[...] = dx
        dx16_ref[...] = dx.astype(BF16)
        st_ref[0:1, :] += jnp.sum(dy * xh, axis=0, keepdims=True)
        st_ref[1:2, :] += jnp.sum(err * err, axis=0, keepdims=True)

        @pl.when(i == ni - 1)
        def _():
            tot = jnp.sum(st_ref[1:2, :], axis=1, keepdims=True) * (0.5 / D)
            st_ref[2:3, :] = jnp.broadcast_to(tot, (1, D))

    row = pl.BlockSpec((tm, D), lambda i: (i, 0))
    return pl.pallas_call(
        body, name="final_loss", grid=(ni,),
        in_specs=[row, row, pl.BlockSpec((1, D), lambda i: (0, 0))],
        out_specs=[row, row, pl.BlockSpec((8, D), lambda i: (0, 0))],
        out_shape=[jax.ShapeDtypeStruct((T, D), F32), jax.ShapeDtypeStruct((T, D), BF16),
                   jax.ShapeDtypeStruct((8, D), F32)],
        compiler_params=_cparams(("arbitrary",), VMEM_LIMIT),
    )(x2, tgt, fw)


def _fox_cum(ps, fb_row, S, li):
    T = ps.shape[0]
    blk = min(4 * LCH, S)
    nb, nsub = S // blk, blk // LCH

    def body(ps_ref, fb_ref, cum_ref, carry):
        @pl.when(pl.program_id(1) == 0)
        def _():
            carry[...] = jnp.zeros_like(carry)

        r = lax.broadcasted_iota(jnp.int32, (LCH, LCH), 0)
        c = lax.broadcasted_iota(jnp.int32, (LCH, LCH), 1)
        tri = (r >= c).astype(F32)
        run = carry[0:1, :]
        for u in range(nsub):
            rows = slice(LCH * u, LCH * (u + 1))
            logf = -_softplus(-(ps_ref[rows, :] + fb_ref[...]))
            cum = _dot_hi(tri, logf) + run
            cum_ref[rows, :] = cum
            run = cum[LCH - 1:LCH, :]
        carry[0:1, :] = run

    return pl.pallas_call(
        body, name=f"fox_cum_{li}", grid=(T // S, nb),
        in_specs=[pl.BlockSpec((blk, LANES), lambda b, i: (b * nb + i, 0)),
                  pl.BlockSpec((1, LANES), lambda b, i: (0, 0))],
        out_specs=pl.BlockSpec((blk, LANES), lambda b, i: (b * nb + i, 0)),
        out_shape=jax.ShapeDtypeStruct((T, LANES), F32),
        scratch_shapes=[pltpu.VMEM((8, LANES), F32)],
        compiler_params=_cparams(("arbitrary", "arbitrary")),
    )(ps, fb_row)


def _fox_cum_bwd(dcum, ps, fb_row, S, li):
    T = ps.shape[0]
    rows_blk = min(4 * LCH, S)
    nb, nsub = S // rows_blk, rows_blk // LCH

    def body(dc_ref, ps_ref, fb_ref, df_ref, dfb_ref, carry):
        b, i = pl.program_id(0), pl.program_id(1)

        @pl.when(i == 0)
        def _():
            carry[...] = jnp.zeros_like(carry)

        @pl.when(jnp.logical_and(b == 0, i == 0))
        def _():
            dfb_ref[...] = jnp.zeros_like(dfb_ref)

        r = lax.broadcasted_iota(jnp.int32, (LCH, LCH), 0)
        c = lax.broadcasted_iota(jnp.int32, (LCH, LCH), 1)
        tri = (c >= r).astype(F32)
        lane = _lane_iota()
        live = jnp.logical_and(lane >= NH, lane < 2 * NH)
        run = carry[0:1, :]
        dfb = jnp.zeros((1, LANES), F32)
        for u in reversed(range(nsub)):
            rows = slice(LCH * u, LCH * (u + 1))
            dc = dc_ref[rows, :]
            dlogf = _dot_hi(tri, dc) + run
            run = run + jnp.sum(dc, axis=0, keepdims=True)
            df = jnp.where(live, dlogf * _sigmoid(-(ps_ref[rows, :] + fb_ref[...])), 0.0)
            df_ref[rows, :] = df
            dfb = dfb + jnp.sum(df, axis=0, keepdims=True)
        carry[0:1, :] = run
        dfb_ref[0:1, :] += dfb

    blk = pl.BlockSpec((rows_blk, LANES), lambda b, i: (b * nb + nb - 1 - i, 0))
    return pl.pallas_call(
        body, name=f"fox_cum_bwd_{li}", grid=(T // S, nb),
        in_specs=[blk, blk, pl.BlockSpec((1, LANES), lambda b, i: (0, 0))],
        out_specs=[blk, pl.BlockSpec((8, LANES), lambda b, i: (0, 0))],
        out_shape=[jax.ShapeDtypeStruct((T, LANES), F32), jax.ShapeDtypeStruct((8, LANES), F32)],
        scratch_shapes=[pltpu.VMEM((8, LANES), F32)],
        compiler_params=_cparams(("arbitrary", "arbitrary")),
    )(dcum, ps, fb_row)


def _fox_blocks(S):
    bq = min(512, S)
    return bq, S // bq


def _split3(c):
    hi = c.astype(BF16).astype(F32)
    r = c - hi
    mid = r.astype(BF16).astype(F32)
    return hi, mid, (r - mid).astype(BF16).astype(F32)


def _augment(x, parts, key_side, hh):
    lane = _lane_iota()
    b0 = HD if hh == 0 else 0
    p0, o0 = (b0 + 3, b0) if key_side else (b0, b0 + 3)
    out = jnp.where(jnp.logical_and(lane >= o0, lane < o0 + 3), 1.0, x)
    for t in range(3):
        out = jnp.where(lane == p0 + t, parts[t], out)
    return out.astype(BF16)


def _fox_fwd(proj, cum_col, S, li, comm=None):
    T = proj.shape[0]
    B = T // S
    bq, nq = _fox_blocks(S)
    qc, kc, vc, zc = OFF_CQ // LANES, OFF_CK // LANES, OFF_CV // LANES, OFF_CZ // LANES

    def body(q_ref, k_ref, v_ref, z_ref, cc_ref, y_ref, o_ref, lse_ref, kaug):
        i = pl.program_id(2)
        m0 = _lane_iota() < HD

        @pl.when(i == 0)
        def _():
            kf = k_ref[...].astype(F32)
            for hh in range(2):
                kaug[hh] = _augment(kf, _split3(-cc_ref[0, hh]), True, hh)

        q2 = q_ref[...].astype(F32) * SCALE
        rows_q = pl.ds(pl.multiple_of(i * bq, bq), bq)
        half = bq // 2
        tri = lax.broadcasted_iota(jnp.int32, (half, half), 0) >= lax.broadcasted_iota(jnp.int32, (half, half), 1)
        qa = [_augment(jnp.where(m0 if hh == 0 else jnp.logical_not(m0), q2, 0.0),
                       _split3(cc_ref[0, hh, rows_q, :]), False, hh) for hh in range(2)]

        def online(m, l, acc, scores, values):
            mn = m
            for s in scores:
                mn = jnp.maximum(mn, jnp.max(s, axis=1, keepdims=True))
            alpha = jnp.exp(m - mn)
            l, acc = alpha * l, alpha * acc
            for s, v in zip(scores, values):
                p = jnp.exp(s - mn)
                l = l + jnp.sum(p, axis=1, keepdims=True)
                acc = acc + _dot(p.astype(BF16), v)
            return mn, l, acc

        def step(j, carry):
            start = pl.multiple_of(j * bq, bq)
            v2 = v_ref[pl.ds(start, bq), :]
            out = []
            for hh in range(2):
                s = _dot_nt(qa[hh], kaug[hh, pl.ds(start, bq), :])
                out += online(*carry[3 * hh:3 * hh + 3], [s], [v2])
            return tuple(out)

        def diagonal(carry):
            start = pl.multiple_of(i * bq, bq)
            lo, hi = slice(0, half), slice(half, bq)
            v_lo, v_hi = v_ref[pl.ds(start, half), :], v_ref[pl.ds(start + half, half), :]
            out = []
            for hh in range(2):
                m, l, acc = carry[3 * hh:3 * hh + 3]
                k_lo, k_hi = kaug[hh, pl.ds(start, half), :], kaug[hh, pl.ds(start + half, half), :]
                s_ll = jnp.where(tri, _dot_nt(qa[hh][lo], k_lo), NEG)
                s_hh = jnp.where(tri, _dot_nt(qa[hh][hi], k_hi), NEG)
                r_lo = online(m[lo], l[lo], acc[lo], [s_ll], [v_lo])
                r_hi = online(m[hi], l[hi], acc[hi], [_dot_nt(qa[hh][hi], k_lo), s_hh], [v_lo, v_hi])
                out += [jnp.concatenate([a, b], axis=0) for a, b in zip(r_lo, r_hi)]
            return tuple(out)

        init = (jnp.full((bq, 1), NEG, F32), jnp.zeros((bq, 1), F32), jnp.zeros((bq, LANES), F32)) * 2
        carry = diagonal(lax.fori_loop(0, i, step, init))
        outs = []
        for hh in range(2):
            m, l, acc = carry[3 * hh:3 * hh + 3]
            outs.append(acc / l)
            lse_ref[0, hh] = m + jnp.log(l)
        o2 = jnp.where(m0, outs[0], outs[1])
        z = z_ref[...].astype(F32)
        o_ref[...] = o2.astype(BF16)
        y_ref[...] = (o2 * z * _sigmoid(z)).astype(BF16)

    qblk = lambda c: pl.BlockSpec((bq, LANES), lambda b, p, i, c=c: (b * nq + i, c + p))
    sblk = lambda c: pl.BlockSpec((S, LANES), lambda b, p, i, c=c: (b, c + p))
    return _hosted_call(
        body, comm, f"fox_fwd_{li}", (B, NH // 2, nq),
        in_specs=[qblk(qc), sblk(kc), sblk(vc), qblk(zc),
                  pl.BlockSpec((1, 2, S, 1), lambda b, p, i: (b, p, 0, 0))],
        out_specs=[qblk(0), qblk(0), pl.BlockSpec((1, 2, bq, 1), lambda b, p, i: (b, p, i, 0))],
        out_shape=[jax.ShapeDtypeStruct((T, D), BF16), jax.ShapeDtypeStruct((T, D), BF16),
                   jax.ShapeDtypeStruct((B, NH, S, 1), F32)],
        scratch=[pltpu.VMEM((2, S, LANES), BF16)], dims=("parallel", "parallel", "arbitrary"),
        operands=(proj, proj, proj, proj, cum_col))


def _fox_bwd(proj, do, o, cum_col, lse, S, li):
    T = proj.shape[0]
    B = T // S
    bq, nq = _fox_blocks(S)
    qc, kc, vc = OFF_CQ // LANES, OFF_CK // LANES, OFF_CV // LANES

    def body(q_ref, k_ref, v_ref, do_ref, o_ref, cc_ref, lse_ref, dq_ref, dk_ref, dv_ref, dc_ref, dr_ref,
             dq_scr, dr_scr, qaug):
        j = pl.program_id(2)
        m0 = _lane_iota() < HD

        @pl.when(j == 0)
        def _():
            dq_scr[...] = jnp.zeros_like(dq_scr)
            dr_scr[...] = jnp.zeros_like(dr_scr)
            qf = q_ref[...].astype(F32) * SCALE
            for hh in range(2):
                sel = m0 if hh == 0 else jnp.logical_not(m0)
                qaug[hh] = _augment(jnp.where(sel, qf, 0.0), _split3(cc_ref[0, hh] - lse_ref[0, hh]), False, hh)

        k2 = k_ref[...]
        v2 = v_ref[...]
        zk = jnp.zeros_like(k2)
        kh = (jnp.where(m0, k2, zk), jnp.where(m0, zk, k2))
        kf = k2.astype(F32)
        rows_k = pl.ds(pl.multiple_of(j * bq, bq), bq)
        ka = [_augment(kf, _split3(-cc_ref[0, hh, rows_k, :]), True, hh) for hh in range(2)]
        half = bq // 2
        tri = lax.broadcasted_iota(jnp.int32, (half, half), 0) >= lax.broadcasted_iota(jnp.int32, (half, half), 1)

        def tile(qstart, nrows, ksl, mask, dk, dv, dcs):
            rows = pl.ds(qstart, nrows)
            q2 = q_ref[rows, :]
            do2 = do_ref[rows, :]
            prod = do2.astype(F32) * o_ref[rows, :].astype(F32)
            zq = jnp.zeros_like(q2)
            dq = jnp.zeros((nrows, LANES), F32)
            dcs = list(dcs)
            for hh in range(2):
                sel = m0 if hh == 0 else jnp.logical_not(m0)
                qh = jnp.where(sel, q2, zq)
                doh = jnp.where(sel, do2, zq)
                delta = _head_sum(prod, hh)
                s = _dot_nt(qaug[hh, rows, :], ka[hh][ksl])
                if mask is not None:
                    s = jnp.where(mask, s, NEG)
                p = jnp.exp(s)
                dp = _dot_nt(doh, v2[ksl])
                ds = p * (dp - delta)
                dcs[hh] = dcs[hh] - jnp.sum(ds, axis=0, keepdims=True)
                dr_scr[hh, rows, :] += jnp.sum(ds, axis=1, keepdims=True)
                dsb = ds.astype(BF16)
                dv = dv + _dot_tn(p.astype(BF16), doh)
                dk = dk + _dot_tn(dsb, qh)
                dq = dq + _dot(dsb, kh[hh][ksl])
            dq_scr[rows, :] += dq
            return dk, dv, dcs

        def step(i, carry):
            dk, dv, dc0, dc1 = carry
            dk, dv, dcs = tile(pl.multiple_of(i * bq, bq), bq, slice(None), None, dk, dv, (dc0, dc1))
            return dk, dv, dcs[0], dcs[1]

        start = pl.multiple_of(j * bq, bq)
        zero = jnp.zeros((half, LANES), F32)
        zrow = jnp.zeros((1, half), F32)
        lo, hi = slice(0, half), slice(half, bq)
        dk_lo, dv_lo, dc_lo = tile(start, half, lo, tri, zero, zero, (zrow, zrow))
        dk_lo, dv_lo, dc_lo = tile(start + half, half, lo, None, dk_lo, dv_lo, dc_lo)
        dk_hi, dv_hi, dc_hi = tile(start + half, half, hi, tri, zero, zero, (zrow, zrow))
        carry = (jnp.concatenate([dk_lo, dk_hi], axis=0), jnp.concatenate([dv_lo, dv_hi], axis=0),
                 jnp.concatenate([dc_lo[0], dc_hi[0]], axis=1), jnp.concatenate([dc_lo[1], dc_hi[1]], axis=1))
        dk, dv, dc0, dc1 = lax.fori_loop(j + 1, nq, step, carry)
        dk_ref[...] = (dk * SCALE).astype(BF16)
        dv_ref[...] = dv.astype(BF16)
        dc_ref[0, 0, 0] = dc0
        dc_ref[0, 1, 0] = dc1

        @pl.when(j == nq - 1)
        def _():
            dq_ref[...] = (dq_scr[...] * SCALE).astype(BF16)
            dr_ref[0] = dr_scr[...]

    sblk = lambda c: pl.BlockSpec((S, LANES), lambda b, p, j, c=c: (b, c + p))
    kblk = lambda c: pl.BlockSpec((bq, LANES), lambda b, p, j, c=c: (b * nq + j, c + p))
    col_spec = pl.BlockSpec((1, 2, S, 1), lambda b, p, j: (b, p, 0, 0))
    return pl.pallas_call(
        body, name=f"fox_bwd_{li}", grid=(B, NH // 2, nq),
        in_specs=[sblk(qc), kblk(kc), kblk(vc), sblk(0), sblk(0), col_spec, col_spec],
        out_specs=[sblk(0), kblk(0), kblk(0), pl.BlockSpec((1, 2, 1, 1, bq), lambda b, p, j: (b, p, j, 0, 0)),
                   col_spec],
        out_shape=[jax.ShapeDtypeStruct((T, D), BF16), jax.ShapeDtypeStruct((T, D), BF16),
                   jax.ShapeDtypeStruct((T, D), BF16), jax.ShapeDtypeStruct((B, NH, nq, 1, bq), F32),
                   jax.ShapeDtypeStruct((B, NH, S, 1), F32)],
        scratch_shapes=[pltpu.VMEM((S, LANES), F32), pltpu.VMEM((2, S, 1), F32), pltpu.VMEM((2, S, LANES), BF16)],
        compiler_params=_cparams(("parallel", "parallel", "arbitrary"), VMEM_LIMIT),
    )(proj, proj, proj, do, o, cum_col, lse)


def _swa_blocks(S):
    bq = min(512, S)
    return bq, S // bq, bq // LCH


def _dup_head(xw, kvl):
    m0 = _lane_iota() < HD
    a = jnp.where(m0 if kvl == 0 else jnp.logical_not(m0), xw, 0.0)
    return (a + pltpu.roll(a, HD, 1)).astype(BF16)


def _band(same_block):
    r = lax.broadcasted_iota(jnp.int32, (LCH, LCH), 0)
    c = lax.broadcasted_iota(jnp.int32, (LCH, LCH), 1)
    return (c <= r) if same_block else (c > r)


def _stack_heads(ref, rows, kvl):
    m0 = _lane_iota() < HD
    parts = []
    for ch in (2 * kvl, 2 * kvl + 1):
        x = ref[rows, LANES * ch:LANES * (ch + 1)]
        parts += [jnp.where(m0, x, jnp.zeros_like(x)), jnp.where(m0, jnp.zeros_like(x), x)]
    return jnp.concatenate(parts, axis=0)


def _stack_delta(do_ref, o_ref, rows, kvl, scale=None):
    parts = []
    for ch in (2 * kvl, 2 * kvl + 1):
        lanes = slice(LANES * ch, LANES * (ch + 1))
        prod = do_ref[rows, lanes].astype(F32) * o_ref[rows, lanes].astype(F32)
        parts += [_head_sum(prod, 0), _head_sum(prod, 1)]
    out = jnp.concatenate(parts, axis=0)
    return out if scale is None else out * scale


def _stack_cols(ref, rows, kvl):
    return jnp.concatenate([ref[0, 4 * kvl + t, rows, :] for t in range(4)], axis=0)


def _swa_fwd(proj, sinks, S, li):
    T = proj.shape[0]
    B = T // S
    bq, nq, nsub = _swa_blocks(S)
    nrow = S // LCH
    qc, zc, kc, vc = OFF_BQ // 512, OFF_BZ // 512, OFF_BK // LANES, OFF_BV // LANES

    def body(sk_ref, q_ref, z_ref, kp_ref, kc_ref, vp_ref, vc_ref, y_ref, o_ref, lse_ref):
        c, i = pl.program_id(0), pl.program_id(2)
        m0 = _lane_iota() < HD
        kw = jnp.concatenate([kp_ref[...].astype(F32), kc_ref[...].astype(F32)], axis=0)
        vw = jnp.concatenate([vp_ref[...].astype(F32), vc_ref[...].astype(F32)], axis=0)
        kd = (_dup_head(kw, 0), _dup_head(kw, 1))
        vd = (_dup_head(vw, 0), _dup_head(vw, 1))
        valid = jnp.concatenate([_band(False), _band(True)], axis=1)
        col = lax.broadcasted_iota(jnp.int32, (LCH, 2 * LCH), 1)
        valid_first = jnp.logical_and(valid, jnp.logical_or(col >= LCH, i > 0))
        valid4 = jnp.concatenate([valid] * 4, axis=0)
        valid4_first = jnp.concatenate([valid_first] * 4, axis=0)
        for r in range(nsub):
            rows = slice(LCH * r, LCH * (r + 1))
            msk = valid4_first if r == 0 else valid4
            for kvl in range(2):
                kwin = kd[kvl][LCH * r:LCH * (r + 2)]
                vwin = vd[kvl][LCH * r:LCH * (r + 2)]
                qs = _stack_heads(q_ref, rows, kvl)
                sink = jnp.concatenate([jnp.full((LCH, 1), sk_ref[8 * c + 4 * kvl + t], F32) for t in range(4)], axis=0)
                s = jnp.where(msk, _dot_nt(qs, kwin) * SCALE, NEG)
                m = jnp.maximum(jnp.max(s, axis=1, keepdims=True), sink)
                p = jnp.exp(s - m)
                l = jnp.sum(p, axis=1, keepdims=True) + jnp.exp(sink - m)
                os_ = _dot(p.astype(BF16), vwin) / l
                lse = m + jnp.log(l)
                for t in range(4):
                    lse_ref[0, 4 * kvl + t, rows, :] = lse[LCH * t:LCH * (t + 1)]
                for u in range(2):
                    lanes = slice(LANES * (2 * kvl + u), LANES * (2 * kvl + u + 1))
                    o2 = jnp.where(m0, os_[LCH * 2 * u:LCH * (2 * u + 1)], os_[LCH * (2 * u + 1):LCH * (2 * u + 2)])
                    z = z_ref[rows, lanes].astype(F32)
                    o_ref[rows, lanes] = o2.astype(BF16)
                    y_ref[rows, lanes] = (o2 * z * _sigmoid(z)).astype(BF16)

    wide = lambda cc: pl.BlockSpec((bq, 512), lambda c, b, i, cc=cc: (b * nq + i, cc + c))
    cur = lambda cc: pl.BlockSpec((bq, LANES), lambda c, b, i, cc=cc: (b * nq + i, cc + c))
    prev = lambda cc: pl.BlockSpec((LCH, LANES), lambda c, b, i, cc=cc: (b * nrow + jnp.maximum(i * nsub - 1, 0), cc + c))
    return pl.pallas_call(
        body, name=f"swa_fwd_{li}", grid=(2, B, nq),
        in_specs=[pl.BlockSpec(memory_space=pltpu.SMEM), wide(qc), wide(zc), prev(kc), cur(kc), prev(vc), cur(vc)],
        out_specs=[wide(0), wide(0), pl.BlockSpec((1, 8, bq, 1), lambda c, b, i: (b, c, i, 0))],
        out_shape=[jax.ShapeDtypeStruct((T, D), BF16), jax.ShapeDtypeStruct((T, D), BF16),
                   jax.ShapeDtypeStruct((B, NH, S, 1), F32)],
        compiler_params=_cparams(("parallel", "parallel", "parallel"), VMEM_LIMIT),
    )(sinks, proj, proj, proj, proj, proj, proj)


def _swa_bwd_dq(proj, do, o, lse, sinks, cos128, sin128, S, li):
    T = proj.shape[0]
    B = T // S
    bq, nq, nsub = _swa_blocks(S)
    nrow = S // LCH
    qc, kc, vc = OFF_BQ // 512, OFF_BK // LANES, OFF_BV // LANES

    def body(sk_ref, q_ref, do_ref, o_ref, lse_ref, kp_ref, kc_ref, vp_ref, vc_ref, cos_ref, sin_ref, dq_ref, dsk_ref):
        c, b, i = pl.program_id(0), pl.program_id(1), pl.program_id(2)

        @pl.when(jnp.logical_and(b == 0, i == 0))
        def _():
            dsk_ref[...] = jnp.zeros_like(dsk_ref)

        m0 = _lane_iota() < HD
        kw = jnp.concatenate([kp_ref[...].astype(F32), kc_ref[...].astype(F32)], axis=0)
        vw = jnp.concatenate([vp_ref[...].astype(F32), vc_ref[...].astype(F32)], axis=0)
        kd = (_dup_head(kw, 0), _dup_head(kw, 1))
        vd = (_dup_head(vw, 0), _dup_head(vw, 1))
        valid = jnp.concatenate([_band(False), _band(True)], axis=1)
        col = lax.broadcasted_iota(jnp.int32, (LCH, 2 * LCH), 1)
        valid_first = jnp.logical_and(valid, jnp.logical_or(col >= LCH, i > 0))
        dsk = [jnp.zeros((1, 1), F32) for _ in range(8)]
        valid4 = jnp.concatenate([valid] * 4, axis=0)
        valid4_first = jnp.concatenate([valid_first] * 4, axis=0)
        for r in range(nsub):
            rows = slice(LCH * r, LCH * (r + 1))
            msk = valid4_first if r == 0 else valid4
            for kvl in range(2):
                kwin = kd[kvl][LCH * r:LCH * (r + 2)]
                vwin = vd[kvl][LCH * r:LCH * (r + 2)]
                qs = _stack_heads(q_ref, rows, kvl)
                dos = _stack_heads(do_ref, rows, kvl)
                delta = _stack_delta(do_ref, o_ref, rows, kvl)
                lse = _stack_cols(lse_ref, rows, kvl)
                sink = jnp.concatenate([jnp.full((LCH, 1), sk_ref[8 * c + 4 * kvl + t], F32) for t in range(4)], axis=0)
                s = jnp.where(msk, _dot_nt(qs, kwin) * SCALE, NEG)
                p = jnp.exp(s - lse)
                ds = p * (_dot_nt(dos, vwin) - delta)
                dqs = _dot(ds.astype(BF16), kwin) * SCALE
                dsink = jnp.exp(sink - lse) * delta
                for t in range(4):
                    hl = 4 * kvl + t
                    dsk[hl] = dsk[hl] - jnp.sum(dsink[LCH * t:LCH * (t + 1)], axis=0, keepdims=True)
                for u in range(2):
                    lanes = slice(LANES * (2 * kvl + u), LANES * (2 * kvl + u + 1))
                    dq2 = jnp.where(m0, dqs[LCH * 2 * u:LCH * (2 * u + 1)], dqs[LCH * (2 * u + 1):LCH * (2 * u + 2)])
                    dq2 = dq2 * cos_ref[rows, :] - _rot_half(dq2) * sin_ref[rows, :]
                    dq_ref[rows, lanes] = dq2.astype(BF16)
        for hl in range(8):
            dsk_ref[0, hl:hl + 1, :] += jnp.broadcast_to(dsk[hl], (1, LANES))

    wide = lambda cc: pl.BlockSpec((bq, 512), lambda c, b, i, cc=cc: (b * nq + i, cc + c))
    cur = lambda cc: pl.BlockSpec((bq, LANES), lambda c, b, i, cc=cc: (b * nq + i, cc + c))
    prev = lambda cc: pl.BlockSpec((LCH, LANES), lambda c, b, i, cc=cc: (b * nrow + jnp.maximum(i * nsub - 1, 0), cc + c))
    pos = pl.BlockSpec((bq, LANES), lambda c, b, i: (i, 0))
    return pl.pallas_call(
        body, name=f"swa_bwd_dq_{li}", grid=(2, B, nq),
        in_specs=[pl.BlockSpec(memory_space=pltpu.SMEM), wide(qc), wide(0), wide(0),
                  pl.BlockSpec((1, 8, bq, 1), lambda c, b, i: (b, c, i, 0)),
                  prev(kc), cur(kc), prev(vc), cur(vc), pos, pos],
        out_specs=[wide(0), pl.BlockSpec((1, 8, LANES), lambda c, b, i: (c, 0, 0))],
        out_shape=[jax.ShapeDtypeStruct((T, D), BF16), jax.ShapeDtypeStruct((2, 8, LANES), F32)],
        compiler_params=_cparams(("arbitrary", "arbitrary", "arbitrary"), VMEM_LIMIT),
    )(sinks, proj, do, o, lse, proj, proj, proj, proj, cos128, sin128)


def _swa_bwd_dkv(proj, do, o, lse, cos128, sin128, S, li):
    T = proj.shape[0]
    B = T // S
    bk, nk, nsub = _swa_blocks(S)
    nrow = S // LCH
    qc, kc, vc = OFF_BQ // 512, OFF_BK // LANES, OFF_BV // LANES

    def body(q_ref, qn_ref, do_ref, don_ref, o_ref, on_ref, lse_ref, lsen_ref, k_ref, v_ref, cos_ref, sin_ref,
             dk_ref, dv_ref):
        j = pl.program_id(2)
        m0 = _lane_iota() < HD
        has_next = (j < nk - 1).astype(F32)
        kf = k_ref[...].astype(F32)
        vf = v_ref[...].astype(F32)
        kd = (_dup_head(kf, 0), _dup_head(kf, 1))
        vd = (_dup_head(vf, 0), _dup_head(vf, 1))
        lane = _lane_iota()

        def stat_rows(lse_r, do_r, o_r, rows, scale):
            a_lse = jnp.zeros((rows, LANES), F32)
            a_del = jnp.zeros((rows, LANES), F32)
            for ch in range(4):
                lanes = slice(LANES * ch, LANES * (ch + 1))
                prod = do_r[:, lanes].astype(F32) * o_r[:, lanes].astype(F32)
                for hh in range(2):
                    h = 2 * ch + hh
                    a_lse = jnp.where(lane == h, lse_r[0, h], a_lse)
                    a_del = jnp.where(lane == h, _head_sum(prod, hh), a_del)
            if scale is not None:
                a_del = a_del * scale
            return a_lse.T, a_del.T

        lse_t, del_t = stat_rows(lse_ref, do_ref, o_ref, bk, None)
        lsen_t, deln_t = stat_rows(lsen_ref, don_ref, on_ref, LCH, has_next)
        r_ = lax.broadcasted_iota(jnp.int32, (LCH, LCH), 0)
        c_ = lax.broadcasted_iota(jnp.int32, (LCH, LCH), 1)
        masks4 = (jnp.concatenate([r_ <= c_] * 4, axis=1), jnp.concatenate([r_ > c_] * 4, axis=1))
        for kr in range(nsub):
            krows = slice(LCH * kr, LCH * (kr + 1))
            dk = jnp.zeros((LCH, LANES), F32)
            dv = jnp.zeros((LCH, LANES), F32)
            for dq_blk in range(2):
                rq = kr + dq_blk
                nxt = rq == nsub
                qrows = slice(0, LCH) if nxt else slice(LCH * rq, LCH * (rq + 1))
                qr, dor = (qn_ref, don_ref) if nxt else (q_ref, do_ref)
                lt, dt_ = (lsen_t, deln_t) if nxt else (lse_t, del_t)
                for kvl in range(2):
                    qs = _stack_heads(qr, qrows, kvl)
                    dos = _stack_heads(dor, qrows, kvl)
                    if nxt:
                        dos = (dos.astype(F32) * has_next).astype(BF16)
                    lse_row = jnp.concatenate([lt[4 * kvl + t:4 * kvl + t + 1, qrows] for t in range(4)], axis=1)
                    del_row = jnp.concatenate([dt_[4 * kvl + t:4 * kvl + t + 1, qrows] for t in range(4)], axis=1)
                    st = jnp.where(masks4[dq_blk], _dot_nt(kd[kvl][krows], qs) * SCALE, NEG)
                    pt = jnp.exp(st - lse_row)
                    dst = pt * (_dot_nt(vd[kvl][krows], dos) - del_row)
                    dvc = _dot(pt.astype(BF16), dos)
                    dkc = _dot(dst.astype(BF16), qs) * SCALE
                    own = m0 if kvl == 0 else jnp.logical_not(m0)
                    dv = dv + jnp.where(own, dvc + pltpu.roll(dvc, HD, 1), 0.0)
                    dk = dk + jnp.where(own, dkc + pltpu.roll(dkc, HD, 1), 0.0)
            dk = dk * cos_ref[krows, :] - _rot_half(dk) * sin_ref[krows, :]
            dk_ref[krows, :] = dk.astype(BF16)
            dv_ref[krows, :] = dv.astype(BF16)

    wide = lambda cc: pl.BlockSpec((bk, 512), lambda c, b, j, cc=cc: (b * nk + j, cc + c))
    nxt = lambda cc: pl.BlockSpec((LCH, 512), lambda c, b, j, cc=cc: (b * nrow + jnp.minimum((j + 1) * nsub, nrow - 1), cc + c))
    cur = lambda cc: pl.BlockSpec((bk, LANES), lambda c, b, j, cc=cc: (b * nk + j, cc + c))
    pos = pl.BlockSpec((bk, LANES), lambda c, b, j: (j, 0))
    return pl.pallas_call(
        body, name=f"swa_bwd_dkv_{li}", grid=(2, B, nk),
        in_specs=[wide(qc), nxt(qc), wide(0), nxt(0), wide(0), nxt(0),
                  pl.BlockSpec((1, 8, bk, 1), lambda c, b, j: (b, c, j, 0)),
                  pl.BlockSpec((1, 8, LCH, 1), lambda c, b, j: (b, c, jnp.minimum((j + 1) * nsub, nrow - 1), 0)),
                  cur(kc), cur(vc), pos, pos],
        out_specs=[cur(0), cur(0)],
        out_shape=[jax.ShapeDtypeStruct((T, 2 * LANES), BF16), jax.ShapeDtypeStruct((T, 2 * LANES), BF16)],
        compiler_params=_cparams(("parallel", "parallel", "parallel"), VMEM_LIMIT),
    )(proj, proj, do, do, o, o, lse, lse, proj, proj, cos128, sin128)


HALO = 16


def _shift_matrices():
    r = lax.broadcasted_iota(jnp.int32, (3 * LCH, LCH + HALO), 0)
    c = lax.broadcasted_iota(jnp.int32, (3 * LCH, LCH + HALO), 1)
    t, d = r % LCH, r // LCH + 1
    return (c == HALO + t - d).astype(BF16), (c == t + d).astype(BF16)


def _ssm_chunk_pre(prev16, cur16, first, sdn_ref, cw_ref, cb_ref, ps, dtb, alog):
    ext16 = jnp.concatenate([jnp.where(first, jnp.zeros_like(prev16), prev16), cur16], axis=0)
    sh = _dot(sdn_ref[...], ext16)
    pre = cb_ref[...] + cw_ref[3:4, :] * cur16.astype(F32)
    for d in range(1, 4):
        pre = pre + cw_ref[3 - d:4 - d, :] * sh[LCH * (d - 1):LCH * d]
    sg = _sigmoid(pre)
    dt = _softplus(ps + dtb)
    a = -jnp.exp(alog)
    r = lax.broadcasted_iota(jnp.int32, (LCH, LCH), 0)
    c = lax.broadcasted_iota(jnp.int32, (LCH, LCH), 1)
    acum = _dot_hi((r >= c).astype(F32), dt * a)
    return pre, sg, dt, a, acum, sh


def _expand_matrix():
    r = lax.broadcasted_iota(jnp.int32, (3 * LANES, D), 0)
    c = lax.broadcasted_iota(jnp.int32, (3 * LANES, D), 1)
    return ((r % LANES) == c // HD).astype(BF16)


def _expand_heads(v, ex_ref):
    return _dot(jnp.concatenate(_split3(v), axis=1).astype(BF16), ex_ref[...])


def _decay(acum, acum_t, h):
    r = lax.broadcasted_iota(jnp.int32, (LCH, LCH), 0)
    c = lax.broadcasted_iota(jnp.int32, (LCH, LCH), 1)
    causal = r >= c
    seg = acum[:, h:h + 1] - acum_t[h:h + 1, :]
    return jnp.where(causal, jnp.exp(jnp.where(causal, seg, 0.0)), 0.0)


def _ssm_pair_fwd(p, x, dt_x, acum, acum_t, e_x, w_x, cd, cb_g, b_g, c_g, hprev, dsk_ref):
    m0 = _lane_iota() < HD
    lanes = slice(LANES * p, LANES * (p + 1))
    x2 = x[:, lanes]
    dt2 = dt_x[:, lanes]
    xdt2 = x2 * dt2
    xdtb = xdt2.astype(BF16)
    lms, ms, yds = [], [], []
    for hh in range(2):
        lm = _decay(acum, acum_t, 2 * p + hh)
        mm = cb_g * lm
        lms.append(lm)
        ms.append(mm)
        yds.append(_dot(mm.astype(BF16), xdtb))
    yd2 = jnp.where(m0, yds[0], yds[1])
    w2 = w_x[:, lanes]
    xw = (xdt2 * w2).astype(BF16)
    s2 = _dot_tn(xw, b_g)
    z2 = _dot_nt(c_g, hprev.astype(BF16))
    e2 = e_x[:, lanes]
    rowsel = lax.broadcasted_iota(jnp.int32, (LANES, 1), 0) < HD
    cdcol = jnp.where(rowsel, cd[:, 2 * p:2 * p + 1], cd[:, 2 * p + 1:2 * p + 2])
    y2 = yd2 + z2 * e2 + dsk_ref[:, lanes] * x2
    return dict(x2=x2, dt2=dt2, xdt2=xdt2, xdtb=xdtb, lms=lms, ms=ms, yd2=yd2, w2=w2, xw=xw, s2=s2, z2=z2, e2=e2,
                cdcol=cdcol, y2=y2)


def _ssm_specs(S, rev):
    nc = S // LCH
    ch = (lambda c: nc - 1 - c) if rev else (lambda c: c)
    prev = pl.BlockSpec((HALO, 2 * D), lambda b, c: (jnp.maximum(b * (S // HALO) + ch(c) * (LCH // HALO) - 1, 0), 0))
    cur = pl.BlockSpec((LCH, 2 * D), lambda b, c: (b * nc + ch(c), 0))
    zed = pl.BlockSpec((LCH, D), lambda b, c: (b * nc + ch(c), OFF_AZ // D))
    row = pl.BlockSpec((LCH, D), lambda b, c: (b * nc + ch(c), 0))
    psb = pl.BlockSpec((LCH, LANES), lambda b, c: (b * nc + ch(c), 0))
    hpb = pl.BlockSpec((1, 1, NH // 2, LANES, NST), lambda b, c: (b, ch(c), 0, 0, 0))
    const = lambda r, w: pl.BlockSpec((r, w), lambda b, c: (0, 0))
    return nc, prev, cur, zed, row, psb, hpb, const


def _ssm_fwd(proj, ps, cw, cb, dtb, alog, dsk, nw, S, li):
    T = proj.shape[0]
    B = T // S
    nc, prev, cur, zed, row, psb, hpb, const = _ssm_specs(S, False)

    def body(prev_ref, cur_ref, z_ref, ps_ref, sdn_ref, ex_ref, cw_ref, cb_ref, dtb_ref, alog_ref, dsk_ref, nw_ref,
             ya_ref, hp_ref, h_scr):
        c = pl.program_id(1)

        @pl.when(c == 0)
        def _():
            h_scr[...] = jnp.zeros_like(h_scr)

        pre, sg, dt, a, acum, _ = _ssm_chunk_pre(prev_ref[...], cur_ref[...], c == 0, sdn_ref, cw_ref, cb_ref,
                                                 ps_ref[...], dtb_ref[...], alog_ref[...])
        act = pre * sg
        acum_t = acum.T
        last = acum[LCH - 1:LCH, :]
        cd = jnp.exp(last)
        dt, e_all, w_all = (_expand_heads(v, ex_ref) for v in (dt, jnp.exp(acum), jnp.exp(last - acum)))
        x = act[:, :D]
        for g in range(NGRP):
            b_g = act[:, D + NST * g:D + NST * (g + 1)].astype(BF16)
            c_g = act[:, D + NGRP * NST + NST * g:D + NGRP * NST + NST * (g + 1)].astype(BF16)
            cb_g = _dot_nt(c_g, b_g)
            ygs = []
            for p in (2 * g, 2 * g + 1):
                hprev = h_scr[p]
                hp_ref[0, 0, p] = hprev
                f = _ssm_pair_fwd(p, x, dt, acum, acum_t, e_all, w_all, cd, cb_g, b_g, c_g, hprev, dsk_ref)
                h_scr[p] = hprev * f["cdcol"] + f["s2"]
                z2 = z_ref[:, LANES * p:LANES * (p + 1)].astype(F32)
                ygs.append(f["y2"] * z2 * _sigmoid(z2))
            yg = jnp.concatenate(ygs, axis=1)
            r = lax.rsqrt(jnp.mean(yg * yg, axis=1, keepdims=True) + EPS)
            ya_ref[:, 2 * LANES * g:2 * LANES * (g + 1)] = (yg * r * nw_ref[:, 2 * LANES * g:2 * LANES * (g + 1)]).astype(BF16)

    return pl.pallas_call(
        body, name=f"ssm_fwd_{li}", grid=(B, nc),
        in_specs=[prev, cur, zed, psb, const(3 * LCH, LCH + HALO), const(3 * LANES, D), const(4, 2 * D),
                  const(1, 2 * D), const(1, LANES), const(1, LANES), const(1, D), const(1, D)],
        out_specs=[row, hpb],
        out_shape=[jax.ShapeDtypeStruct((T, D), BF16), jax.ShapeDtypeStruct((B, nc, NH // 2, LANES, NST), F32)],
        scratch_shapes=[pltpu.VMEM((NH // 2, LANES, NST), F32)],
        compiler_params=_cparams(("arbitrary", "arbitrary"), VMEM_LIMIT),
    )(proj, proj, proj, ps, _shift_matrices()[0], _expand_matrix(), cw, cb, dtb, alog, dsk, nw)


def _ssm_bwd(proj, ps, hp, dya, cw, cb, dtb, alog, dsk, nw, S, li, comm=None):
    T = proj.shape[0]
    B = T // S
    nc, prev, cur, zed, row, psb, hpb, const = _ssm_specs(S, True)

    def body(prev_ref, cur_ref, z_ref, ps_ref, hp_ref, dy_ref, sdn_ref, sup_ref, ex_ref, cw_ref, cb_ref, dtb_ref,
             alog_ref, dsk_ref, nw_ref, dxbc_ref, dz_ref, dps_ref, pgw_ref, pg1_ref, pgh_ref, dh_scr, dhead, dact):
        b, cc = pl.program_id(0), pl.program_id(1)
        c = nc - 1 - cc

        @pl.when(jnp.logical_and(b == 0, cc == 0))
        def _():
            pgw_ref[...] = jnp.zeros_like(pgw_ref)
            pg1_ref[...] = jnp.zeros_like(pg1_ref)
            pgh_ref[...] = jnp.zeros_like(pgh_ref)

        @pl.when(cc == 0)
        def _():
            dh_scr[...] = jnp.zeros_like(dh_scr)
            dhead[...] = jnp.zeros_like(dhead)

        psv = ps_ref[...]
        cur16 = cur_ref[...]
        pre, sg, dt, a, acum, sh = _ssm_chunk_pre(prev_ref[...], cur16, c == 0, sdn_ref, cw_ref, cb_ref, psv,
                                                  dtb_ref[...], alog_ref[...])
        act = pre * sg
        acum_t = acum.T
        last = acum[LCH - 1:LCH, :]
        w_all = jnp.exp(last - acum)
        cd = jnp.exp(last)
        dt_x, e_x, w_x = (_expand_heads(v, ex_ref) for v in (dt, jnp.exp(acum), w_all))
        x = act[:, :D]
        lane = _lane_iota()
        m0 = lane < HD
        head_row = lax.broadcasted_iota(jnp.int32, (LANES, 1), 0)
        rowsel = head_row < HD
        is_last_row = lax.broadcasted_iota(jnp.int32, (LCH, 1), 0) == LCH - 1
        dacum_all = jnp.zeros((LCH, LANES), F32)
        dacum_t = jnp.zeros((LANES, LCH), F32)
        ddt_all = jnp.zeros((LCH, LANES), F32)
        dd_row = jnp.zeros((1, LANES), F32)
        for g in range(NGRP):
            b_g = act[:, D + NST * g:D + NST * (g + 1)].astype(BF16)
            c_g = act[:, D + NGRP * NST + NST * g:D + NGRP * NST + NST * (g + 1)].astype(BF16)
            cb_g = _dot_nt(c_g, b_g)
            pairs = (2 * g, 2 * g + 1)
            fs, hps, zs, ygs = [], [], [], []
            for p in pairs:
                hprev = hp_ref[0, 0, p]
                f = _ssm_pair_fwd(p, x, dt_x, acum, acum_t, e_x, w_x, cd, cb_g, b_g, c_g, hprev, dsk_ref)
                z2 = z_ref[:, LANES * p:LANES * (p + 1)].astype(F32)
                fs.append(f)
                hps.append(hprev)
                zs.append(z2)
                ygs.append(f["y2"] * z2 * _sigmoid(z2))
            gl = slice(2 * LANES * g, 2 * LANES * (g + 1))
            yg = jnp.concatenate(ygs, axis=1)
            r = lax.rsqrt(jnp.mean(yg * yg, axis=1, keepdims=True) + EPS)
            dyn = dy_ref[:, gl].astype(F32)
            gg = dyn * nw_ref[:, gl]
            dyg = r * gg - yg * (r * r * r) * jnp.mean(gg * yg, axis=1, keepdims=True)
            pg1_ref[0:1, gl] += jnp.sum(dyn * yg * r, axis=0, keepdims=True)
            dg_g = jnp.zeros((LCH, LCH), F32)
            db_g = jnp.zeros((LCH, NST), F32)
            dc_g = jnp.zeros((LCH, NST), F32)
            for idx, p in enumerate(pairs):
                f, hprev, z2 = fs[idx], hps[idx], zs[idx]
                lanes = slice(LANES * p, LANES * (p + 1))
                dyg2 = dyg[:, LANES * idx:LANES * (idx + 1)]
                sgz = _sigmoid(z2)
                dy2 = dyg2 * z2 * sgz
                dz_ref[:, lanes] = (dyg2 * f["y2"] * sgz * (1.0 + z2 * (1.0 - sgz))).astype(BF16)
                x2, dt2, xdt2, xdtb, w2, e2, z2m = f["x2"], f["dt2"], f["xdt2"], f["xdtb"], f["w2"], f["e2"], f["z2"]
                dx2 = dsk_ref[:, lanes] * dy2
                dyx = dy2 * x2
                dxdt2 = jnp.zeros((LCH, LANES), F32)
                diag_cols = []
                for hh in range(2):
                    sel = m0 if hh == 0 else jnp.logical_not(m0)
                    dyb = jnp.where(sel, dy2, 0.0).astype(BF16)
                    dm = _dot_nt(dyb, xdtb)
                    dg_g = dg_g + dm * f["lms"][hh]
                    dxdt2 = dxdt2 + _dot_tn(f["ms"][hh].astype(BF16), dyb)
                    em = dm * f["ms"][hh]
                    diag_cols.append(jnp.sum(em, axis=1, keepdims=True))
                    dacum_t = dacum_t - jnp.where(head_row == 2 * p + hh, jnp.sum(em, axis=0, keepdims=True), 0.0)
                dz2m = dy2 * e2
                t_off = dz2m * z2m
                dc_g = dc_g + _dot(dz2m.astype(BF16), hprev.astype(BF16))
                dhprev = _dot_tn(dz2m.astype(BF16), c_g)
                dhn = dh_scr[p]
                dhnb = dhn.astype(BF16)
                dhprev = dhprev + dhn * f["cdcol"]
                t_h = dhn * hprev
                dxw2 = _dot_nt(b_g, dhnb)
                db_g = db_g + _dot(f["xw"], dhnb)
                dxdt2 = dxdt2 + dxw2 * w2
                t_w = dxw2 * xdt2
                dx2 = dx2 + dxdt2 * dt2
                t_dt = dxdt2 * x2
                for hh in range(2):
                    h = 2 * p + hh
                    onehot = (lane == h).astype(F32)
                    w_col = w_all[:, h:h + 1]
                    dw_col = _head_sum(t_w, hh) * w_col
                    rs = rowsel if hh == 0 else jnp.logical_not(rowsel)
                    dlast = (jnp.sum(jnp.where(rs, t_h, 0.0), keepdims=True) * cd[:, h:h + 1]
                             + jnp.sum(dw_col, keepdims=True))
                    dacum_col = diag_cols[hh] + _head_sum(t_off, hh) - dw_col + jnp.where(is_last_row, dlast, 0.0)
                    dacum_all = dacum_all + dacum_col * onehot
                    ddt_all = ddt_all + _head_sum(t_dt, hh) * onehot
                    sel = m0 if hh == 0 else jnp.logical_not(m0)
                    dd_row = dd_row + jnp.sum(jnp.where(sel, dyx, 0.0), keepdims=True) * onehot
                dh_scr[p] = dhprev
                dact[:, lanes] = dx2
            dgb = dg_g.astype(BF16)
            dc_g = dc_g + _dot(dgb, b_g)
            db_g = db_g + _dot_tn(dgb, c_g)
            dact[:, D + NST * g:D + NST * (g + 1)] = db_g
            dact[:, D + NGRP * NST + NST * g:D + NGRP * NST + NST * (g + 1)] = dc_g
        rr = lax.broadcasted_iota(jnp.int32, (LCH, LCH), 0)
        cc2 = lax.broadcasted_iota(jnp.int32, (LCH, LCH), 1)
        dadt = _dot_hi((cc2 >= rr).astype(F32), dacum_all + dacum_t.T)
        ddt_all = ddt_all + dadt * a
        heads = lane < NH
        da = jnp.sum(dadt * dt, axis=0, keepdims=True)
        dr = jnp.where(heads, ddt_all * _sigmoid(psv + dtb_ref[...]), 0.0)
        dps_ref[...] = dr
        pgh_ref[0:1, :] += jnp.sum(dr, axis=0, keepdims=True)
        pgh_ref[1:2, :] += jnp.where(heads, da * a, 0.0)
        pgh_ref[2:3, :] += dd_row
        dpre = dact[...] * sg * (1.0 + pre * (1.0 - sg))
        extd = jnp.concatenate([dpre, dhead[...]], axis=0)
        hi = extd.astype(BF16)
        lo = (extd - hi.astype(F32)).astype(BF16)
        up = _dot(sup_ref[...], hi) + _dot(sup_ref[...], lo)
        du = cw_ref[3:4, :] * dpre
        pgw_ref[3:4, :] += jnp.sum(dpre * cur16.astype(F32), axis=0, keepdims=True)
        for d in range(1, 4):
            du = du + cw_ref[3 - d:4 - d, :] * up[LCH * (d - 1):LCH * d]
            pgw_ref[3 - d:4 - d, :] += jnp.sum(dpre * sh[LCH * (d - 1):LCH * d], axis=0, keepdims=True)
        pgw_ref[4:5, :] += jnp.sum(dpre, axis=0, keepdims=True)
        dxbc_ref[...] = du.astype(BF16)
        dhead[...] = dpre[0:HALO, :]

    xbc_out = pl.BlockSpec((LCH, 2 * D), lambda b, c: (b * nc + nc - 1 - c, 0))
    acc = lambda w: pl.BlockSpec((8, w), lambda b, c: (0, 0))
    sdn, sup = _shift_matrices()
    return _hosted_call(
        body, comm, f"ssm_bwd_{li}", (B, nc),
        in_specs=[prev, cur, zed, psb, hpb, row, const(3 * LCH, LCH + HALO), const(3 * LCH, LCH + HALO),
                  const(3 * LANES, D), const(4, 2 * D), const(1, 2 * D), const(1, LANES), const(1, LANES),
                  const(1, D), const(1, D)],
        out_specs=[xbc_out, row, psb, acc(2 * D), acc(D), acc(LANES)],
        out_shape=[jax.ShapeDtypeStruct((T, 2 * D), BF16), jax.ShapeDtypeStruct((T, D), BF16),
                   jax.ShapeDtypeStruct((T, LANES), F32), jax.ShapeDtypeStruct((8, 2 * D), F32),
                   jax.ShapeDtypeStruct((8, D), F32), jax.ShapeDtypeStruct((8, LANES), F32)],
        scratch=[pltpu.VMEM((NH // 2, LANES, NST), F32), pltpu.VMEM((HALO, 2 * D), F32),
                 pltpu.VMEM((LCH, 2 * D), F32)],
        dims=("arbitrary", "arbitrary"),
        operands=(proj, proj, proj, ps, hp, dya, sdn, sup, _expand_matrix(), cw, cb, dtb, alog, dsk, nw))


def _lane_row(v, offset):
    return jnp.pad(v.astype(F32), (offset, LANES - offset - v.shape[0]))[None]


def _pack_rows(arrays):
    parts = []
    for a in arrays:
        flat = a.reshape(-1).astype(F32)
        pad = (-flat.shape[0]) % LANES
        parts.append(jnp.pad(flat, (0, pad)))
    flat = jnp.concatenate(parts)
    pad = (-flat.shape[0]) % (8 * LANES)
    return jnp.pad(flat, (0, pad)).reshape(-1, LANES)


def _unpack_rows(pack, shapes):
    flat = pack.reshape(-1)
    out, pos = [], 0
    for shp in shapes:
        n = math.prod(shp)
        out.append(flat[pos:pos + n].reshape(shp))
        pos += n + (-n) % LANES
    return out


def _split_w_in(w):
    main = jnp.concatenate([w[:, 0:3072], w[:, 3088:4112], w[:, 4624:5648], w[:, 5648:8720], w[:, 8736:12832],
                            w[:, 4112:4624]], axis=1)
    small = jnp.concatenate([w[:, 3072:3088], w[:, 8720:8736], jnp.zeros((D, LANES - 2 * NH), w.dtype)], axis=1)
    return main, small


def _join_w_in(dw, ds):
    xbc, az, bq, bz, cq, ck, cv, cz, gates, bk, bv = dw
    return jnp.concatenate([xbc, az, ds[:, 0:NH], bq, bk, bv, bz, cq, ck, cv, ds[:, NH:2 * NH], cz, gates], axis=1)


def kernel(x, norm_w, w_in, conv_w, conv_b, dt_bias, a_log, d_skip, ssm_norm_w, sinks, f_bias, gate_bias, w_proj, w_out, final_norm_w, loss_target, m_norm_w, m_w_in, m_conv_w, m_conv_b, m_dt_bias, m_a_log, m_d_skip, m_ssm_norm_w, m_sinks, m_f_bias, m_gate_bias, m_w_proj, m_w_out, m_final_norm_w, v_norm_w, v_w_in, v_conv_w, v_conv_b, v_dt_bias, v_a_log, v_d_skip, v_ssm_norm_w, v_sinks, v_f_bias, v_gate_bias, v_w_proj, v_w_out, v_final_norm_w):
    Bl, S, _ = x.shape
    T = Bl * S
    depth = norm_w.shape[0]
    me = 4 * lax.axis_index("x") + 2 * lax.axis_index("y") + lax.axis_index("c")
    csh, gsh = conv_w.shape[2], gate_bias.shape[2]

    def gather_plan(l):
        small = jnp.concatenate([conv_w[l].reshape(-1), gate_bias[l].reshape(-1)]).reshape(-1, LANES)
        return _Comm("gather", [w_in[l].astype(BF16), w_proj[l].astype(BF16), w_out[l].astype(BF16), small])

    def unpack_weights(res):
        g_win, g_wp, g_wo, g_small = res
        flat = g_small.reshape(NDEV, -1)
        return (g_win.transpose(1, 0, 2).reshape(D, NIN),
                g_wp.transpose(1, 0, 2, 3).reshape(3, D, D),
                g_wo.reshape(D, D),
                flat[:, :4 * csh].reshape(NDEV, 4, csh).transpose(1, 0, 2).reshape(4, 2 * D),
                flat[:, 4 * csh:].reshape(NDEV, 3, gsh).transpose(1, 0, 2).reshape(3, D))

    def scatter_plan(gw_in, gw_p=None, gw_o=None):
        arrays = [gw_in.astype(BF16).reshape(-1, NDEV, NSH).transpose(1, 0, 2)]
        if gw_p is not None:
            arrays += [gw_p.astype(BF16).reshape(3, NDEV, D // NDEV, D).transpose(1, 0, 2, 3),
                       gw_o.astype(BF16).reshape(NDEV, D // NDEV, D)]
        return _Comm("scatter", arrays)

    pos = jnp.arange(S, dtype=F32)
    inv_freq = ROPE_THETA ** (-jnp.arange(0, HD, 2, dtype=F32) / HD)
    ang = pos[:, None] * inv_freq[None, :]
    cos128 = jnp.tile(jnp.cos(ang), (1, 4))
    sign = jnp.where((jnp.arange(LANES) % HD) < HD // 2, -1.0, 1.0).astype(F32)
    sin128 = jnp.tile(jnp.sin(ang), (1, 4)) * sign[None, :]

    bq, nq = _fox_blocks(S)
    x2 = x.reshape(T, D)
    tgt2 = loss_target.reshape(T, D)

    saved = []
    xcur = x2
    weights = [None] * depth
    weights[0] = unpack_weights(_gather_two_level(gather_plan(0).arrays, "gather_weights_0"))
    for l in range(depth):
        win_l, wp_l, wo_l, cw_l, gb_l = weights[l]
        wmain, wsmall = _split_w_in(win_l)
        proj, ps, h_t = _inproj_fwd(xcur, norm_w[l][None], wmain, wsmall, cos128, sin128, S, l)
        dtb = _lane_row(dt_bias[l], 0)
        alog = _lane_row(a_log[l], 0)
        fb = _lane_row(f_bias[l], NH)
        dsk = jnp.repeat(d_skip[l], HD)[None]
        ya, hp = _ssm_fwd(proj, ps, cw_l, conv_b[l][None], dtb, alog, dsk, ssm_norm_w[l][None], S, l)
        yb, ob, lse_b = _swa_fwd(proj, sinks[l], S, l)
        cum = _fox_cum(ps, fb, S, l)
        cumh = cum[:, NH:2 * NH].reshape(Bl, S, NH).transpose(0, 2, 1)
        cum_col = cumh[..., None]
        comm = gather_plan(l + 1) if l + 1 < depth else None
        res = _fox_fwd(proj, cum_col, S, l, comm)
        yc, oc, lse_c = res[:3]
        if comm is not None:
            weights[l + 1] = unpack_weights(res[3:])
        xnext, br, y_t = _merge_fwd(ya, yb, yc, proj, gb_l, wp_l, wo_l, xcur, l)
        saved.append(dict(x=xcur, wmain=wmain, wsmall=wsmall, proj=proj, ps=ps, h_t=h_t, dtb=dtb, alog=alog, fb=fb,
                          dsk=dsk, hp=hp, ob=ob, lse_b=lse_b, cum_col=cum_col, oc=oc, lse_c=lse_c, br=br, y_t=y_t))
        xcur = xnext

    dx, dx16, st = _final_loss(xcur, tgt2, final_norm_w[None])
    loss_part = st[2, 0]
    g_final = st[0]

    gsm = {k: [None] * depth for k in ("norm_w", "conv_w", "conv_b", "dt_bias", "a_log", "d_skip", "ssm_norm_w",
                                      "sinks", "f_bias", "gate_bias")}
    parts = [None] * depth
    pending = None
    for l in reversed(range(depth)):
        sv = saved[l]
        proj, ps = sv["proj"], sv["ps"]
        _, wp_l, wo_l, cw_l, gb_l = weights[l]
        dbr, dgates, merged_t, dgb, dy_a, do_b, dbz, do_c, dcz = _merge_bwd(dx16, wo_l, wp_l, sv["br"], proj, gb_l,
                                                                            sv["ob"], sv["oc"], l)
        g_wo = _matmul(merged_t, dx16, BF16, f"dwout_{l}")
        g_wp = _matmul_batched(sv["y_t"], dbr, BF16, f"dwproj_{l}")
        gsm["gate_bias"][l] = dgb[0:3]
        res = _ssm_bwd(proj, ps, sv["hp"], dy_a, cw_l, conv_b[l][None], sv["dtb"], sv["alog"], sv["dsk"],
                       ssm_norm_w[l][None], S, l, pending)
        dxbc, daz, dps_a, pgw, pg1, pgh = res[:6]
        if pending is not None:
            parts[l + 1] = res[6:]
        gsm["conv_w"][l], gsm["conv_b"][l] = pgw[0:4], pgw[4]
        gsm["ssm_norm_w"][l] = pg1[0]
        gsm["dt_bias"][l], gsm["a_log"][l], gsm["d_skip"][l] = pgh[0, :NH], pgh[1, :NH], pgh[2, :NH]
        dq_b, dsk_b = _swa_bwd_dq(proj, do_b, sv["ob"], sv["lse_b"], sinks[l], cos128, sin128, S, l)
        dk_b, dv_b = _swa_bwd_dkv(proj, do_b, sv["ob"], sv["lse_b"], cos128, sin128, S, l)
        gsm["sinks"][l] = dsk_b[:, :, 0].reshape(NH)
        dq_c, dk_c, dv_c, dcum_k, dcum_q = _fox_bwd(proj, do_c, sv["oc"], sv["cum_col"], sv["lse_c"], S, l)
        dcum_tm = (dcum_k.reshape(Bl, NH, S) + dcum_q.reshape(Bl, NH, S)).transpose(0, 2, 1).reshape(T, NH)
        dcum_pad = jnp.pad(dcum_tm, ((0, 0), (NH, LANES - 2 * NH)))
        df, dfb = _fox_cum_bwd(dcum_pad, ps, sv["fb"], S, l)
        gsm["f_bias"][l] = dfb[0, NH:2 * NH]
        dps16 = (dps_a + df).astype(BF16)
        pieces = (dxbc, daz, dq_b, dbz, dq_c, dk_c, dv_c, dcz, dgates, dk_b, dv_b)
        dw_pieces = [_matmul(sv["h_t"], pc, BF16, f"dwin_{l}_{i}") for i, pc in enumerate(pieces)]
        dws = _matmul(sv["h_t"], dps16, BF16, f"dwin_small_{l}")
        g_win = _join_w_in(dw_pieces, dws)
        if l == 0:
            plans = [scatter_plan(g_win[r0:r1], *((g_wp, g_wo) if r0 == 0 else ())) for r0, r1 in ROW_CHUNKS]
        else:
            plans, pending = [None] * len(ROW_CHUNKS), scatter_plan(g_win, g_wp, g_wo)
        dkv_b = jnp.concatenate([dk_b, dv_b], axis=1)
        res1 = _inproj_bwd_dx([(dxbc, OFF_XBC), (daz, OFF_AZ), (dq_b, OFF_BQ), (dbz, OFF_BZ)], sv["wmain"],
                              ("narrow", dps16, sv["wsmall"]), None, f"inproj_bwd_dh1_{l}", plans[0])
        res2 = _inproj_bwd_dx([(dq_c, OFF_CQ), (dk_c, OFF_CK), (dv_c, OFF_CV), (dcz, OFF_CZ)], sv["wmain"],
                              ("acc", res1[0]), None, f"inproj_bwd_dh2_{l}", plans[1])
        res3 = _inproj_bwd_dx([(dgates, OFF_G), (dkv_b, OFF_BK)], sv["wmain"], ("acc", res2[0]),
                              (sv["x"], norm_w[l][None], dx), f"inproj_bwd_dx_{l}", plans[2])
        dx, dx16, dnw = res3[:3]
        if l == 0:
            parts[0] = [jnp.concatenate([res1[1], res2[1], res3[3]], axis=1), res1[2], res1[3]]
        gsm["norm_w"][l] = dnw[0]

    big = {}
    for idx, (name, w, m, v) in enumerate((("w_in", w_in, m_w_in, v_w_in), ("w_proj", w_proj, m_w_proj, v_w_proj),
                                          ("w_out", w_out, m_w_out, v_w_out))):
        cols = w.shape[-1]
        res = _sum_adamw([parts[l][idx].reshape(NDEV, -1, cols) for l in range(depth)], w.reshape(depth, -1, cols),
                         m.reshape(depth, -1, cols), v.reshape(depth, -1, cols), f"adamw_{name}")
        big[name] = [r.reshape(w.shape) for r in res]

    small_names = ("norm_w", "conv_b", "dt_bias", "a_log", "d_skip", "ssm_norm_w", "sinks", "f_bias")
    small_parts = [jnp.stack(gsm[k]) for k in small_names] + [g_final, jnp.stack(gsm["conv_w"]),
                                                              jnp.stack(gsm["gate_bias"]), loss_part.reshape(1)]
    shapes = [a.shape for a in small_parts]
    summed = _unpack_rows(_all_reduce_small(_pack_rows(small_parts)), shapes)
    g_small = dict(zip(small_names, summed[:len(small_names)]))
    g_small["final_norm_w"] = summed[len(small_names)]
    g_small["conv_w"] = lax.dynamic_slice_in_dim(summed[len(small_names) + 1], me * csh, csh, axis=2)
    g_small["gate_bias"] = lax.dynamic_slice_in_dim(summed[len(small_names) + 2], me * gsh, gsh, axis=2)
    loss = summed[len(small_names) + 3][0]

    ws = dict(norm_w=norm_w, conv_w=conv_w, conv_b=conv_b, dt_bias=dt_bias, a_log=a_log, d_skip=d_skip,
              ssm_norm_w=ssm_norm_w, sinks=sinks, f_bias=f_bias, gate_bias=gate_bias, final_norm_w=final_norm_w)
    ms = dict(norm_w=m_norm_w, conv_w=m_conv_w, conv_b=m_conv_b, dt_bias=m_dt_bias, a_log=m_a_log, d_skip=m_d_skip,
              ssm_norm_w=m_ssm_norm_w, sinks=m_sinks, f_bias=m_f_bias, gate_bias=m_gate_bias,
              final_norm_w=m_final_norm_w)
    vs = dict(norm_w=v_norm_w, conv_w=v_conv_w, conv_b=v_conv_b, dt_bias=v_dt_bias, a_log=v_a_log, d_skip=v_d_skip,
              ssm_norm_w=v_ssm_norm_w, sinks=v_sinks, f_bias=v_f_bias, gate_bias=v_gate_bias,
              final_norm_w=v_final_norm_w)
    order = list(ws)
    oshapes = [ws[k].shape for k in order]
    res = _adamw_small(_pack_rows([g_small[k] for k in order]), _pack_rows([ws[k] for k in order]),
                       _pack_rows([ms[k] for k in order]), _pack_rows([vs[k] for k in order]))
    d_s, m_s, v_s = (dict(zip(order, _unpack_rows(r, oshapes))) for r in res)

    names = ("norm_w", "w_in", "conv_w", "conv_b", "dt_bias", "a_log", "d_skip", "ssm_norm_w", "sinks", "f_bias",
             "gate_bias", "w_proj", "w_out", "final_norm_w")
    grads, deltas, new_m, new_v = [], [], [], []
    for k in names:
        if k in big:
            g, d_, m_, v_ = big[k]
        else:
            g, d_, m_, v_ = g_small[k], d_s[k], m_s[k], v_s[k]
        grads.append(g)
        deltas.append(d_)
        new_m.append(m_)
        new_v.append(v_)
    return (loss, dx.reshape(Bl, S, D), *grads, *deltas, *new_m, *new_v)
```

```python
import functools
import math

import jax
import jax.numpy as jnp
from jax import lax
from jax.experimental import pallas as pl
from jax.experimental.pallas import tpu as pltpu

F32 = jnp.float32
BF16 = jnp.bfloat16
MESH = pl.DeviceIdType.MESH
NDEV = 8

D = 1024
NH = 16
HD = 64
NST = 128
NGRP = 4
LCH = 128
EPS = 1e-6
ROPE_THETA = 10000.0
SCALE = HD ** -0.5
NEG = -1e30

LANES = 128
VMEM_LIMIT = 56 * 1024 * 1024

OFF_XBC, OFF_AZ, OFF_BQ, OFF_BZ, OFF_CQ, OFF_CK, OFF_CV, OFF_CZ, OFF_G, OFF_BK, OFF_BV = (
    0, 2048, 3072, 4096, 5120, 6144, 7168, 8192, 9216, 12288, 12544)
NMAIN = 12800
NIN = 12832
NSH = NIN // NDEV

ROW_CHUNKS = ((0, 384), (384, 768), (768, 1024))

ADAM_LR, ADAM_B1, ADAM_B2, ADAM_EPS, ADAM_WD, ADAM_STEP = 0.001, 0.9, 0.999, 1e-08, 0.01, 10


def _cparams(dims=None, vmem=None):
    return pltpu.CompilerParams(dimension_semantics=dims, vmem_limit_bytes=vmem)


def _dot(a, b):
    return jnp.dot(a, b, preferred_element_type=F32)


def _dot_nt(a, b):
    return lax.dot_general(a, b, (((1,), (1,)), ((), ())), preferred_element_type=F32)


def _dot_tn(a, b):
    return lax.dot_general(a, b, (((0,), (0,)), ((), ())), preferred_element_type=F32)


def _dot_hi(a, b):
    return jnp.dot(a, b, precision=lax.Precision.HIGHEST, preferred_element_type=F32)


def _sigmoid(x):
    return 0.5 * jnp.tanh(0.5 * x) + 0.5


def _softplus(x):
    return jnp.maximum(x, 0.0) + jnp.log(1.0 + jnp.exp(-jnp.abs(x)))


def _lane_iota(n=LANES):
    return lax.broadcasted_iota(jnp.int32, (1, n), 1)


def _rot_half(x):
    first = (_lane_iota() % HD) < (HD // 2)
    return jnp.where(first, pltpu.roll(x, LANES - HD // 2, 1), pltpu.roll(x, HD // 2, 1))


def _head_sum(x, head):
    m = (_lane_iota() < HD) if head == 0 else (_lane_iota() >= HD)
    return jnp.sum(jnp.where(m, x, 0.0), axis=1, keepdims=True)


def _me_and_peers():
    x, y, c = lax.axis_index("x"), lax.axis_index("y"), lax.axis_index("c")
    me = 4 * x + 2 * y + c
    peers = []
    for k in range(1, NDEV):
        kx, ky, kc = (k >> 2) & 1, (k >> 1) & 1, k & 1
        px, py, pc = x ^ kx, y ^ ky, c ^ kc
        peers.append(((px, py, pc), 4 * px + 2 * py + pc))
    return me, peers


class _Comm:
    def __init__(self, kind, arrays):
        self.kind, self.arrays, self.n = kind, list(arrays), len(arrays)
        any_spec = pl.BlockSpec(memory_space=pl.ANY)
        self.in_specs = [any_spec] * self.n
        self.out_specs = [any_spec] * self.n
        self.out_shape = [jax.ShapeDtypeStruct(((NDEV,) + a.shape) if kind == "gather" else a.shape, a.dtype)
                          for a in self.arrays]
        self.scratch = [pltpu.SemaphoreType.DMA((self.n, NDEV - 1)), pltpu.SemaphoreType.DMA((self.n, NDEV - 1)),
                        pltpu.SemaphoreType.DMA((self.n,))]

    def copies(self, ins, outs, sems):
        send_sems, recv_sems, local_sems = sems
        me, peers = _me_and_peers()
        out = []
        for a in range(self.n):
            mine = ins[a] if self.kind == "gather" else ins[a].at[me]
            out.append(pltpu.make_async_copy(mine, outs[a].at[me], local_sems.at[a]))
            for k, (peer, pidx) in enumerate(peers):
                src = ins[a] if self.kind == "gather" else ins[a].at[pidx]
                out.append(pltpu.make_async_remote_copy(
                    src_ref=src, dst_ref=outs[a].at[me], send_sem=send_sems.at[a, k], recv_sem=recv_sems.at[a, k],
                    device_id=peer, device_id_type=MESH))
        return out

    def call(self, name):
        def body(*refs):
            cps = self.copies(refs[:self.n], refs[self.n:2 * self.n], refs[2 * self.n:])
            for cp in cps:
                cp.start()
            for cp in cps:
                cp.wait()

        return pl.pallas_call(body, name=name, out_shape=self.out_shape, in_specs=self.in_specs,
                              out_specs=self.out_specs, scratch_shapes=self.scratch)(*self.arrays)


def _gather_two_level(arrays, name):
    n = len(arrays)

    def body(*refs):
        ins, outs = refs[:n], refs[n:2 * n]
        send_sems, recv_sems, local_sems = refs[2 * n:]
        x, y, c = lax.axis_index("x"), lax.axis_index("y"), lax.axis_index("c")
        me, sibling = (x, y, c), (x, y, 1 - c)
        chips = [(1 - x, y), (x, 1 - y), (1 - x, 1 - y)]

        def slot(a, dev):
            return outs[a].at[4 * dev[0] + 2 * dev[1] + dev[2]]

        def copy(a, k, block, to, src=None):
            return pltpu.make_async_remote_copy(
                src_ref=slot(a, block) if src is None else src, dst_ref=slot(a, block),
                send_sem=send_sems.at[a, k], recv_sem=recv_sems.at[a, k], device_id=to, device_id_type=MESH)

        mine = [pltpu.make_async_copy(ins[a], slot(a, me), local_sems.at[a]) for a in range(n)]
        for cp in mine:
            cp.start()
        first = []
        for a in range(n):
            first.append(copy(a, 0, me, sibling, src=ins[a]))
            first += [copy(a, 1 + j, me, (*chip, c), src=ins[a]) for j, chip in enumerate(chips)]
        for cp in first:
            cp.start()
        passed = []
        for j, chip in enumerate(chips):
            for a in range(n):
                copy(a, 1 + j, (*chip, c), me).wait_recv()
                fwd = copy(a, 4 + j, (*chip, c), sibling)
                fwd.start()
                passed.append(fwd)
        for a in range(n):
            copy(a, 0, sibling, me).wait_recv()
            for j, chip in enumerate(chips):
                copy(a, 4 + j, (*chip, 1 - c), me).wait_recv()
        for cp in first + passed:
            cp.wait_send()
        for cp in mine:
            cp.wait()

    any_spec = pl.BlockSpec(memory_space=pl.ANY)
    return pl.pallas_call(
        body, name=name, out_shape=[jax.ShapeDtypeStruct((NDEV,) + a.shape, a.dtype) for a in arrays],
        in_specs=[any_spec] * n, out_specs=[any_spec] * n,
        scratch_shapes=[pltpu.SemaphoreType.DMA((n, NDEV - 1)), pltpu.SemaphoreType.DMA((n, NDEV - 1)),
                        pltpu.SemaphoreType.DMA((n,))])(*arrays)


def _hosted_call(body, comm, name, grid, in_specs, out_specs, out_shape, scratch, dims, operands):
    if comm is None:
        return pl.pallas_call(body, name=name, grid=grid, in_specs=in_specs, out_specs=out_specs, out_shape=out_shape,
                              scratch_shapes=scratch, compiler_params=_cparams(dims, VMEM_LIMIT))(*operands)
    n_in, n_out, n_scr, n = len(in_specs), len(out_specs), len(scratch), comm.n

    def hosted(*refs):
        hin, cin = refs[:n_in], refs[n_in:n_in + n]
        hout = refs[n_in + n:n_in + n + n_out]
        cout = refs[n_in + n + n_out:n_in + 2 * n + n_out]
        hscr = refs[n_in + 2 * n + n_out:n_in + 2 * n + n_out + n_scr]
        sems = refs[n_in + 2 * n + n_out + n_scr:]
        ids = [pl.program_id(a) for a in range(len(grid))]
        first = functools.reduce(jnp.logical_and, [i == 0 for i in ids])
        last = functools.reduce(jnp.logical_and, [i == g - 1 for i, g in zip(ids, grid)])

        @pl.when(first)
        def _():
            for cp in comm.copies(cin, cout, sems):
                cp.start()

        body(*hin, *hout, *hscr)

        @pl.when(last)
        def _():
            for cp in comm.copies(cin, cout, sems):
                cp.wait()

    return pl.pallas_call(
        hosted, name=name, grid=grid, in_specs=list(in_specs) + comm.in_specs,
        out_specs=list(out_specs) + comm.out_specs, out_shape=list(out_shape) + comm.out_shape,
        scratch_shapes=list(scratch) + comm.scratch,
        compiler_params=_cparams(("arbitrary",) * len(grid), VMEM_LIMIT))(*operands, *comm.arrays)


def _all_reduce_small(v):
    rows = v.shape[0]

    def body(v_ref, sum_ref, all_ref, send_sems, recv_sems):
        me, peers = _me_and_peers()
        all_ref[me] = v_ref[...]
        copies = []
        for k, (peer, _) in enumerate(peers):
            cp = pltpu.make_async_remote_copy(
                src_ref=v_ref, dst_ref=all_ref.at[me],
                send_sem=send_sems.at[k], recv_sem=recv_sems.at[k],
                device_id=peer, device_id_type=MESH)
            cp.start()
            copies.append(cp)
        for cp in copies:
            cp.wait()
        acc = all_ref[0]
        for d in range(1, NDEV):
            acc = acc + all_ref[d]
        sum_ref[...] = acc

    vm = pl.BlockSpec(memory_space=pltpu.VMEM)
    return pl.pallas_call(
        body, name="all_reduce_small",
        out_shape=jax.ShapeDtypeStruct((rows, LANES), F32),
        in_specs=[vm], out_specs=vm,
        scratch_shapes=[pltpu.VMEM((NDEV, rows, LANES), F32),
                        pltpu.SemaphoreType.DMA((NDEV - 1,)), pltpu.SemaphoreType.DMA((NDEV - 1,))],
    )(v)


def _adamw_math(w, g, m, v):
    m = ADAM_B1 * m + (1.0 - ADAM_B1) * g
    v = ADAM_B2 * v + (1.0 - ADAM_B2) * jnp.square(g)
    m_hat = m / (1.0 - ADAM_B1 ** ADAM_STEP)
    v_hat = v / (1.0 - ADAM_B2 ** ADAM_STEP)
    delta = -ADAM_LR * (m_hat / (jnp.sqrt(v_hat) + ADAM_EPS) + ADAM_WD * w)
    return delta, m, v


def _sum_adamw(parts, w, m, v, name):
    depth, rows, cols = w.shape
    tr = next(c for c in (256, 128, 64, 32, 16) if rows % c == 0)
    nb = rows // tr

    def body(*refs):
        p_refs, (w_ref, m_ref, v_ref, g_ref, d_ref, nm_ref, nv_ref) = refs[:depth], refs[depth:]
        l = pl.program_id(0)
        for ll in range(depth):
            @pl.when(l == ll)
            def _(ll=ll):
                g = p_refs[ll][0].astype(F32)
                for d in range(1, NDEV):
                    g = g + p_refs[ll][d].astype(F32)
                delta, nm, nv = _adamw_math(w_ref[0], g, m_ref[0], v_ref[0])
                g_ref[0] = g
                d_ref[0] = delta
                nm_ref[0] = nm
                nv_ref[0] = nv

    part = lambda ll: pl.BlockSpec((NDEV, tr, cols), lambda l, i, ll=ll: (0, jnp.where(l == ll, i, jnp.where(l < ll, 0, nb - 1)), 0))
    blk = pl.BlockSpec((1, tr, cols), lambda l, i: (l, i, 0))
    sds = jax.ShapeDtypeStruct((depth, rows, cols), F32)
    return pl.pallas_call(
        body, name=name, grid=(depth, nb),
        in_specs=[part(ll) for ll in range(depth)] + [blk, blk, blk],
        out_specs=[blk, blk, blk, blk], out_shape=[sds, sds, sds, sds],
        compiler_params=_cparams(("arbitrary", "arbitrary"), VMEM_LIMIT),
    )(*parts, w, m, v)


def _adamw_small(g, w, m, v):
    def body(g_ref, w_ref, m_ref, v_ref, d_ref, nm_ref, nv_ref):
        delta, nm, nv = _adamw_math(w_ref[...], g_ref[...], m_ref[...], v_ref[...])
        d_ref[...] = delta
        nm_ref[...] = nm
        nv_ref[...] = nv

    sds = jax.ShapeDtypeStruct(g.shape, F32)
    return pl.pallas_call(body, name="adamw_small", out_shape=[sds, sds, sds])(g, w, m, v)


def _matmul(a, b, out_dtype, name, tm=1024, tn=1024, tk=1024):
    M, K = a.shape
    N = b.shape[1]
    tm, tn, tk = min(tm, M), min(tn, N), min(tk, K)
    nk = K // tk

    def body(a_ref, b_ref, o_ref, acc):
        k = pl.program_id(2)

        @pl.when(k == 0)
        def _():
            acc[...] = jnp.zeros_like(acc)

        acc[...] += _dot(a_ref[...], b_ref[...])

        @pl.when(k == nk - 1)
        def _():
            o_ref[...] = acc[...].astype(out_dtype)

    return pl.pallas_call(
        body, name=name, grid=(M // tm, N // tn, nk),
        in_specs=[pl.BlockSpec((tm, tk), lambda i, j, k: (i, k)), pl.BlockSpec((tk, tn), lambda i, j, k: (k, j))],
        out_specs=pl.BlockSpec((tm, tn), lambda i, j, k: (i, j)),
        out_shape=jax.ShapeDtypeStruct((M, N), out_dtype),
        scratch_shapes=[pltpu.VMEM((tm, tn), F32)],
        compiler_params=_cparams(("parallel", "parallel", "arbitrary"), VMEM_LIMIT),
    )(a, b)


def _matmul_batched(a, b, out_dtype, name, tm=1024, tn=1024, tk=512):
    G, M, K = a.shape
    N = b.shape[2]
    tm, tn, tk = min(tm, M), min(tn, N), min(tk, K)
    nk = K // tk

    def body(a_ref, b_ref, o_ref, acc):
        k = pl.program_id(3)

        @pl.when(k == 0)
        def _():
            acc[...] = jnp.zeros_like(acc)

        acc[...] += _dot(a_ref[0], b_ref[0])

        @pl.when(k == nk - 1)
        def _():
            o_ref[0] = acc[...].astype(out_dtype)

    return pl.pallas_call(
        body, name=name, grid=(G, M // tm, N // tn, nk),
        in_specs=[pl.BlockSpec((1, tm, tk), lambda g, i, j, k: (g, i, k)),
                  pl.BlockSpec((1, tk, tn), lambda g, i, j, k: (g, k, j))],
        out_specs=pl.BlockSpec((1, tm, tn), lambda g, i, j, k: (g, i, j)),
        out_shape=jax.ShapeDtypeStruct((G, M, N), out_dtype),
        scratch_shapes=[pltpu.VMEM((tm, tn), F32)],
        compiler_params=_cparams(("parallel", "parallel", "parallel", "arbitrary"), VMEM_LIMIT),
    )(a, b)


def _inproj_fwd(x2, nw, wmain, wsmall, cos128, sin128, S, li, comm=None):
    T = x2.shape[0]
    tm, tn = min(2048, S), 512
    nj, npos = NMAIN // tn, S // tm
    jq0, jk = OFF_BQ // tn, OFF_BK // tn

    def body(x_ref, nw_ref, w_ref, ws_ref, cos_ref, sin_ref, proj_ref, ps_ref, ht_ref, h_scr):
        j = pl.program_id(1)

        @pl.when(j == 0)
        def _():
            x = x_ref[...]
            r = lax.rsqrt(jnp.mean(x * x, axis=-1, keepdims=True) + EPS)
            h = (x * r * nw_ref[...]).astype(BF16)
            h_scr[...] = h
            ht_ref[...] = h.T
            ps_ref[...] = _dot(h, ws_ref[...])

        acc = _dot(h_scr[...], w_ref[...])

        def roped(c):
            xc = acc[:, LANES * c:LANES * (c + 1)]
            return (xc * cos_ref[...] + _rot_half(xc) * sin_ref[...]).astype(BF16)

        def plain(c):
            return acc[:, LANES * c:LANES * (c + 1)].astype(BF16)

        is_q = jnp.logical_or(j == jq0, j == jq0 + 1)
        is_k = j == jk

        @pl.when(is_q)
        def _():
            for c in range(4):
                proj_ref[:, LANES * c:LANES * (c + 1)] = roped(c)

        @pl.when(is_k)
        def _():
            for c in range(4):
                proj_ref[:, LANES * c:LANES * (c + 1)] = roped(c) if c < 2 else plain(c)

        @pl.when(jnp.logical_not(jnp.logical_or(is_q, is_k)))
        def _():
            proj_ref[...] = acc.astype(BF16)

    return _hosted_call(
        body, comm, f"inproj_fwd_{li}", (T // tm, nj),
        in_specs=[pl.BlockSpec((tm, D), lambda i, j: (i, 0)),
                  pl.BlockSpec((1, D), lambda i, j: (0, 0)),
                  pl.BlockSpec((D, tn), lambda i, j: (0, j)),
                  pl.BlockSpec((D, LANES), lambda i, j: (0, 0)),
                  pl.BlockSpec((tm, LANES), lambda i, j: (i % npos, 0)),
                  pl.BlockSpec((tm, LANES), lambda i, j: (i % npos, 0))],
        out_specs=[pl.BlockSpec((tm, tn), lambda i, j: (i, j)),
                   pl.BlockSpec((tm, LANES), lambda i, j: (i, 0)),
                   pl.BlockSpec((D, tm), lambda i, j: (0, i))],
        out_shape=[jax.ShapeDtypeStruct((T, NMAIN), BF16), jax.ShapeDtypeStruct((T, LANES), F32),
                   jax.ShapeDtypeStruct((D, T), BF16)],
        scratch=[pltpu.VMEM((tm, D), BF16)], dims=("parallel", "arbitrary"),
        operands=(x2, nw, wmain, wsmall, cos128, sin128))


def _inproj_bwd_dx(segs, wmain, init, final, name, comm=None):
    T = segs[0][0].shape[0]
    tm = min(1024, T)
    tk = 1024 if all(a.shape[1] % 1024 == 0 and c % 1024 == 0 for a, c in segs) else 512
    ni = T // tm
    k0s, nks, c0s = [], [], []
    for arr, col0 in segs:
        k0s.append(sum(nks))
        nks.append(arr.shape[1] // tk)
        c0s.append(col0 // tk)
    nk = sum(nks)
    ns = len(segs)

    def in_range(k, s):
        return jnp.logical_and(k >= k0s[s], k < k0s[s] + nks[s])

    def wcol(i, k):
        g = 0
        for s in range(ns):
            g = g + jnp.where(in_range(k, s), c0s[s] + k - k0s[s], 0)
        return (0, g)

    n_init = 2 if init[0] == "narrow" else 1

    def body(*refs):
        seg_refs, w_ref = refs[:ns], refs[ns]
        init_refs = refs[ns + 1:ns + 1 + n_init]
        rest = refs[ns + 1 + n_init:]
        i, k = pl.program_id(0), pl.program_id(1)
        acc = rest[-1]

        @pl.when(k == 0)
        def _():
            if init[0] == "narrow":
                acc[...] = _dot_nt(init_refs[0][...], init_refs[1][...])
            else:
                acc[...] = init_refs[0][...]

        for s in range(ns):
            @pl.when(in_range(k, s))
            def _(s=s):
                acc[...] += _dot_nt(seg_refs[s][...], w_ref[...])

        if final is None:
            @pl.when(k == nk - 1)
            def _():
                rest[0][...] = acc[...]
        else:
            x_ref, nw_ref, dxo_ref, dx_ref, dx16_ref, dnw_ref = rest[:6]

            @pl.when(jnp.logical_and(i == 0, k == 0))
            def _():
                dnw_ref[...] = jnp.zeros_like(dnw_ref)

            @pl.when(k == nk - 1)
            def _():
                x = x_ref[...]
                r = lax.rsqrt(jnp.mean(x * x, axis=-1, keepdims=True) + EPS)
                dh = acc[...]
                g = dh * nw_ref[...]
                dx = dxo_ref[...] + r * g - x * (r * r * r) * jnp.mean(g * x, axis=-1, keepdims=True)
                dx_ref[...] = dx
                dx16_ref[...] = dx.astype(BF16)
                dnw_ref[0:1, :] += jnp.sum(dh * x * r, axis=0, keepdims=True)

    row = pl.BlockSpec((tm, D), lambda i, k: (i, 0))
    in_specs = [pl.BlockSpec((tm, tk), lambda i, k, s=s: (i, jnp.clip(k - k0s[s], 0, nks[s] - 1))) for s in range(ns)]
    in_specs.append(pl.BlockSpec((D, tk), wcol))
    operands = [a for a, _ in segs] + [wmain]
    if init[0] == "narrow":
        in_specs += [pl.BlockSpec((tm, LANES), lambda i, k: (i, 0)), pl.BlockSpec((D, LANES), lambda i, k: (0, 0))]
    else:
        in_specs.append(row)
    operands += list(init[1:])
    if final is None:
        out_specs, out_shape = [row], [jax.ShapeDtypeStruct((T, D), F32)]
    else:
        in_specs += [row, pl.BlockSpec((1, D), lambda i, k: (0, 0)), row]
        operands += list(final)
        out_specs = [row, row, pl.BlockSpec((8, D), lambda i, k: (0, 0))]
        out_shape = [jax.ShapeDtypeStruct((T, D), F32), jax.ShapeDtypeStruct((T, D), BF16),
                     jax.ShapeDtypeStruct((8, D), F32)]
    return _hosted_call(body, comm, name, (ni, nk), in_specs=in_specs, out_specs=out_specs, out_shape=out_shape,
                        scratch=[pltpu.VMEM((tm, D), F32)], dims=("arbitrary", "arbitrary"), operands=tuple(operands))


def _merge_fwd(ya, yb, yc, proj, gbias, wp, wout, x2, li):
    T = x2.shape[0]
    tm = min(512, T)
    gcol = OFF_G // D

    def body(ya_ref, yb_ref, yc_ref, g0_ref, g1_ref, g2_ref, gb_ref, wp_ref, wo_ref, x_ref, xn_ref, br_ref, yt_ref):
        merged = jnp.zeros((tm, D), F32)
        for i, (y_ref, g_ref) in enumerate(((ya_ref, g0_ref), (yb_ref, g1_ref), (yc_ref, g2_ref))):
            y = y_ref[...]
            yt_ref[i] = y.T
            br = _dot(y, wp_ref[i])
            br_ref[i] = br.astype(BF16)
            gate = _sigmoid(g_ref[...].astype(F32) + gb_ref[i:i + 1, :])
            merged = merged + gate * br
        xn_ref[...] = x_ref[...] + _dot(merged.astype(BF16), wo_ref[...])

    row = lambda c: pl.BlockSpec((tm, D), lambda i, c=c: (i, c))
    return pl.pallas_call(
        body, name=f"merge_fwd_{li}", grid=(T // tm,),
        in_specs=[row(0), row(0), row(0), row(gcol), row(gcol + 1), row(gcol + 2),
                  pl.BlockSpec((3, D), lambda i: (0, 0)),
                  pl.BlockSpec((3, D, D), lambda i: (0, 0, 0)),
                  pl.BlockSpec((D, D), lambda i: (0, 0)),
                  row(0)],
        out_specs=[row(0), pl.BlockSpec((3, tm, D), lambda i: (0, i, 0)), pl.BlockSpec((3, D, tm), lambda i: (0, 0, i))],
        out_shape=[jax.ShapeDtypeStruct((T, D), F32), jax.ShapeDtypeStruct((3, T, D), BF16),
                   jax.ShapeDtypeStruct((3, D, T), BF16)],
        compiler_params=_cparams(("parallel",), VMEM_LIMIT),
    )(ya, yb, yc, proj, proj, proj, gbias, wp, wout, x2)


def _merge_bwd(dxo16, wout, wp, br, proj, gbias, ob, oc, li):
    T = dxo16.shape[0]
    tm = min(256, T)
    gcol = OFF_G // D

    def body(dx_ref, wo_ref, wp_ref, br_ref, g0_ref, g1_ref, g2_ref, gb_ref, ob_ref, oc_ref, zb_ref, zc_ref,
             dbr_ref, dg_ref, mt_ref, dgb_ref, dya_ref, dob_ref, dzb_ref, doc_ref, dzc_ref):
        @pl.when(pl.program_id(0) == 0)
        def _():
            dgb_ref[...] = jnp.zeros_like(dgb_ref)

        dm = _dot_nt(dx_ref[...], wo_ref[...])
        merged = jnp.zeros((tm, D), F32)
        dys = []
        for i, g_ref in enumerate((g0_ref, g1_ref, g2_ref)):
            b = br_ref[i].astype(F32)
            gate = _sigmoid(g_ref[...].astype(F32) + gb_ref[i:i + 1, :])
            merged = merged + gate * b
            dbr = (dm * gate).astype(BF16)
            dbr_ref[i] = dbr
            dgate = dm * b * gate * (1.0 - gate)
            dg_ref[:, D * i:D * (i + 1)] = dgate.astype(BF16)
            dgb_ref[i:i + 1, :] += jnp.sum(dgate, axis=0, keepdims=True)
            dys.append(_dot_nt(dbr, wp_ref[i]))
        mt_ref[...] = merged.astype(BF16).T
        dya_ref[...] = dys[0].astype(BF16)
        for dy, o_ref, z_ref, do_ref, dz_ref in ((dys[1], ob_ref, zb_ref, dob_ref, dzb_ref),
                                                 (dys[2], oc_ref, zc_ref, doc_ref, dzc_ref)):
            z = z_ref[...].astype(F32)
            sg = _sigmoid(z)
            do_ref[...] = (dy * z * sg).astype(BF16)
            dz_ref[...] = (dy * o_ref[...].astype(F32) * sg * (1.0 + z * (1.0 - sg))).astype(BF16)

    row = lambda c: pl.BlockSpec((tm, D), lambda i, c=c: (i, c))
    sds = jax.ShapeDtypeStruct((T, D), BF16)
    return pl.pallas_call(
        body, name=f"merge_bwd_{li}", grid=(T // tm,),
        in_specs=[row(0), pl.BlockSpec((D, D), lambda i: (0, 0)), pl.BlockSpec((3, D, D), lambda i: (0, 0, 0)),
                  pl.BlockSpec((3, tm, D), lambda i: (0, i, 0)),
                  row(gcol), row(gcol + 1), row(gcol + 2),
                  pl.BlockSpec((3, D), lambda i: (0, 0)),
                  row(0), row(0), row(OFF_BZ // D), row(OFF_CZ // D)],
        out_specs=[pl.BlockSpec((3, tm, D), lambda i: (0, i, 0)),
                   pl.BlockSpec((tm, 3 * D), lambda i: (i, 0)),
                   pl.BlockSpec((D, tm), lambda i: (0, i)),
                   pl.BlockSpec((8, D), lambda i: (0, 0)),
                   row(0), row(0), row(0), row(0), row(0)],
        out_shape=[jax.ShapeDtypeStruct((3, T, D), BF16), jax.ShapeDtypeStruct((T, 3 * D), BF16),
                   jax.ShapeDtypeStruct((D, T), BF16), jax.ShapeDtypeStruct((8, D), F32), sds, sds, sds, sds, sds],
        compiler_params=_cparams(("arbitrary",), VMEM_LIMIT),
    )(dxo16, wout, wp, br, proj, proj, proj, gbias, ob, oc, proj, proj)


def _final_loss(x2, tgt, fw):
    T = x2.shape[0]
    tm = min(512, T)
    ni = T // tm

    def body(x_ref, t_ref, w_ref, dx_ref, dx16_ref, st_ref):
        i = pl.program_id(0)

        @pl.when(i == 0)
        def _():
            st_ref[...] = jnp.zeros_like(st_ref)

        x = x_ref[...]
        r = lax.rsqrt(jnp.mean(x * x, axis=-1, keepdims=True) + EPS)
        xh = x * r
        err = xh * w_ref[...] - t_ref[...]
        dy = err * (1.0 / D)
        g = dy * w_ref[...]
        dx = r * g - x * (r * r * r) * jnp.mean(g * x, axis=-1, keepdims=True)
        dx_ref[...] = dx
        dx16_ref[...] = dx.astype(BF16)
        st_ref[0:1, :] += jnp.sum(dy * xh, axis=0, keepdims=True)
        st_ref[1:2, :] += jnp.sum(err * err, axis=0, keepdims=True)

        @pl.when(i == ni - 1)
        def _():
            tot = jnp.sum(st_ref[1:2, :], axis=1, keepdims=True) * (0.5 / D)
            st_ref[2:3, :] = jnp.broadcast_to(tot, (1, D))

    row = pl.BlockSpec((tm, D), lambda i: (i, 0))
    return pl.pallas_call(
        body, name="final_loss", grid=(ni,),
        in_specs=[row, row, pl.BlockSpec((1, D), lambda i: (0, 0))],
        out_specs=[row, row, pl.BlockSpec((8, D), lambda i: (0, 0))],
        out_shape=[jax.ShapeDtypeStruct((T, D), F32), jax.ShapeDtypeStruct((T, D), BF16),
                   jax.ShapeDtypeStruct((8, D), F32)],
        compiler_params=_cparams(("arbitrary",), VMEM_LIMIT),
    )(x2, tgt, fw)


def _fox_cum(ps, fb_row, S, li):
    T = ps.shape[0]
    blk = min(4 * LCH, S)
    nb, nsub = S // blk, blk // LCH

    def body(ps_ref, fb_ref, cum_ref, carry):
        @pl.when(pl.program_id(1) == 0)
        def _():
            carry[...] = jnp.zeros_like(carry)

        r = lax.broadcasted_iota(jnp.int32, (LCH, LCH), 0)
        c = lax.broadcasted_iota(jnp.int32, (LCH, LCH), 1)
        tri = (r >= c).astype(F32)
        run = carry[0:1, :]
        for u in range(nsub):
            rows = slice(LCH * u, LCH * (u + 1))
            logf = -_softplus(-(ps_ref[rows, :] + fb_ref[...]))
            cum = _dot_hi(tri, logf) + run
            cum_ref[rows, :] = cum
            run = cum[LCH - 1:LCH, :]
        carry[0:1, :] = run

    return pl.pallas_call(
        body, name=f"fox_cum_{li}", grid=(T // S, nb),
        in_specs=[pl.BlockSpec((blk, LANES), lambda b, i: (b * nb + i, 0)),
                  pl.BlockSpec((1, LANES), lambda b, i: (0, 0))],
        out_specs=pl.BlockSpec((blk, LANES), lambda b, i: (b * nb + i, 0)),
        out_shape=jax.ShapeDtypeStruct((T, LANES), F32),
        scratch_shapes=[pltpu.VMEM((8, LANES), F32)],
        compiler_params=_cparams(("arbitrary", "arbitrary")),
    )(ps, fb_row)


def _fox_cum_bwd(dcum, ps, fb_row, S, li):
    T = ps.shape[0]
    rows_blk = min(4 * LCH, S)
    nb, nsub = S // rows_blk, rows_blk // LCH

    def body(dc_ref, ps_ref, fb_ref, df_ref, dfb_ref, carry):
        b, i = pl.program_id(0), pl.program_id(1)

        @pl.when(i == 0)
        def _():
            carry[...] = jnp.zeros_like(carry)

        @pl.when(jnp.logical_and(b == 0, i == 0))
        def _():
            dfb_ref[...] = jnp.zeros_like(dfb_ref)

        r = lax.broadcasted_iota(jnp.int32, (LCH, LCH), 0)
        c = lax.broadcasted_iota(jnp.int32, (LCH, LCH), 1)
        tri = (c >= r).astype(F32)
        lane = _lane_iota()
        live = jnp.logical_and(lane >= NH, lane < 2 * NH)
        run = carry[0:1, :]
        dfb = jnp.zeros((1, LANES), F32)
        for u in reversed(range(nsub)):
            rows = slice(LCH * u, LCH * (u + 1))
            dc = dc_ref[rows, :]
            dlogf = _dot_hi(tri, dc) + run
            run = run + jnp.sum(dc, axis=0, keepdims=True)
            df = jnp.where(live, dlogf * _sigmoid(-(ps_ref[rows, :] + fb_ref[...])), 0.0)
            df_ref[rows, :] = df
            dfb = dfb + jnp.sum(df, axis=0, keepdims=True)
        carry[0:1, :] = run
        dfb_ref[0:1, :] += dfb

    blk = pl.BlockSpec((rows_blk, LANES), lambda b, i: (b * nb + nb - 1 - i, 0))
    return pl.pallas_call(
        body, name=f"fox_cum_bwd_{li}", grid=(T // S, nb),
        in_specs=[blk, blk, pl.BlockSpec((1, LANES), lambda b, i: (0, 0))],
        out_specs=[blk, pl.BlockSpec((8, LANES), lambda b, i: (0, 0))],
        out_shape=[jax.ShapeDtypeStruct((T, LANES), F32), jax.ShapeDtypeStruct((8, LANES), F32)],
        scratch_shapes=[pltpu.VMEM((8, LANES), F32)],
        compiler_params=_cparams(("arbitrary", "arbitrary")),
    )(dcum, ps, fb_row)


def _fox_blocks(S):
    bq = min(512, S)
    return bq, S // bq


def _split3(c):
    hi = c.astype(BF16).astype(F32)
    r = c - hi
    mid = r.astype(BF16).astype(F32)
    return hi, mid, (r - mid).astype(BF16).astype(F32)


def _augment(x, parts, key_side, hh):
    lane = _lane_iota()
    b0 = HD if hh == 0 else 0
    p0, o0 = (b0 + 3, b0) if key_side else (b0, b0 + 3)
    out = jnp.where(jnp.logical_and(lane >= o0, lane < o0 + 3), 1.0, x)
    for t in range(3):
        out = jnp.where(lane == p0 + t, parts[t], out)
    return out.astype(BF16)


def _fox_fwd(proj, cum_col, S, li, comm=None):
    T = proj.shape[0]
    B = T // S
    bq, nq = _fox_blocks(S)
    qc, kc, vc, zc = OFF_CQ // LANES, OFF_CK // LANES, OFF_CV // LANES, OFF_CZ // LANES

    def body(q_ref, k_ref, v_ref, z_ref, cc_ref, y_ref, o_ref, lse_ref, kaug):
        i = pl.program_id(2)
        m0 = _lane_iota() < HD

        @pl.when(i == 0)
        def _():
            kf = k_ref[...].astype(F32)
            for hh in range(2):
                kaug[hh] = _augment(kf, _split3(-cc_ref[0, hh]), True, hh)

        q2 = q_ref[...].astype(F32) * SCALE
        rows_q = pl.ds(pl.multiple_of(i * bq, bq), bq)
        row = lax.broadcasted_iota(jnp.int32, (bq, bq), 0)
        col = lax.broadcasted_iota(jnp.int32, (bq, bq), 1)
        qa = [_augment(jnp.where(m0 if hh == 0 else jnp.logical_not(m0), q2, 0.0),
                       _split3(cc_ref[0, hh, rows_q, :]), False, hh) for hh in range(2)]

        def step(j, carry, masked):
            start = pl.multiple_of(j * bq, bq)
            v2 = v_ref[pl.ds(start, bq), :]
            out = []
            for hh in range(2):
                m, l, acc = carry[3 * hh:3 * hh + 3]
                s = _dot_nt(qa[hh], kaug[hh, pl.ds(start, bq), :])
                if masked:
                    s = jnp.where(row >= col, s, NEG)
                mn = jnp.maximum(m, jnp.max(s, axis=1, keepdims=True))
                alpha = jnp.exp(m - mn)
                p = jnp.exp(s - mn)
                out += [mn, alpha * l + jnp.sum(p, axis=1, keepdims=True), alpha * acc + _dot(p.astype(BF16), v2)]
            return tuple(out)

        init = (jnp.full((bq, 1), NEG, F32), jnp.zeros((bq, 1), F32), jnp.zeros((bq, LANES), F32)) * 2
        carry = step(i, lax.fori_loop(0, i, functools.partial(step, masked=False), init), True)
        outs = []
        for hh in range(2):
            m, l, acc = carry[3 * hh:3 * hh + 3]
            outs.append(acc / l)
            lse_ref[0, hh] = m + jnp.log(l)
        o2 = jnp.where(m0, outs[0], outs[1])
        z = z_ref[...].astype(F32)
        o_ref[...] = o2.astype(BF16)
        y_ref[...] = (o2 * z * _sigmoid(z)).astype(BF16)

    qblk = lambda c: pl.BlockSpec((bq, LANES), lambda b, p, i, c=c: (b * nq + i, c + p))
    sblk = lambda c: pl.BlockSpec((S, LANES), lambda b, p, i, c=c: (b, c + p))
    return _hosted_call(
        body, comm, f"fox_fwd_{li}", (B, NH // 2, nq),
        in_specs=[qblk(qc), sblk(kc), sblk(vc), qblk(zc),
                  pl.BlockSpec((1, 2, S, 1), lambda b, p, i: (b, p, 0, 0))],
        out_specs=[qblk(0), qblk(0), pl.BlockSpec((1, 2, bq, 1), lambda b, p, i: (b, p, i, 0))],
        out_shape=[jax.ShapeDtypeStruct((T, D), BF16), jax.ShapeDtypeStruct((T, D), BF16),
                   jax.ShapeDtypeStruct((B, NH, S, 1), F32)],
        scratch=[pltpu.VMEM((2, S, LANES), BF16)], dims=("parallel", "parallel", "arbitrary"),
        operands=(proj, proj, proj, proj, cum_col))


def _fox_bwd(proj, do, o, cum_col, lse, S, li, comm=None):
    T = proj.shape[0]
    B = T // S
    bq, nq = _fox_blocks(S)
    qc, kc, vc = OFF_CQ // LANES, OFF_CK // LANES, OFF_CV // LANES

    def body(q_ref, k_ref, v_ref, do_ref, o_ref, cc_ref, lse_ref, dq_ref, dk_ref, dv_ref, dc_ref, dr_ref,
             dq_scr, dr_scr, qaug):
        j = pl.program_id(2)
        m0 = _lane_iota() < HD

        @pl.when(j == 0)
        def _():
            dq_scr[...] = jnp.zeros_like(dq_scr)
            dr_scr[...] = jnp.zeros_like(dr_scr)
            qf = q_ref[...].astype(F32) * SCALE
            for hh in range(2):
                sel = m0 if hh == 0 else jnp.logical_not(m0)
                qaug[hh] = _augment(jnp.where(sel, qf, 0.0), _split3(cc_ref[0, hh] - lse_ref[0, hh]), False, hh)

        k2 = k_ref[...]
        v2 = v_ref[...]
        zk = jnp.zeros_like(k2)
        kh = (jnp.where(m0, k2, zk), jnp.where(m0, zk, k2))
        kf = k2.astype(F32)
        rows_k = pl.ds(pl.multiple_of(j * bq, bq), bq)
        ka = [_augment(kf, _split3(-cc_ref[0, hh, rows_k, :]), True, hh) for hh in range(2)]
        row = lax.broadcasted_iota(jnp.int32, (bq, bq), 0)
        col = lax.broadcasted_iota(jnp.int32, (bq, bq), 1)

        def step(i, carry, masked):
            dk, dv, dc0, dc1 = carry
            dcs = [dc0, dc1]
            start = pl.multiple_of(i * bq, bq)
            q2 = q_ref[pl.ds(start, bq), :]
            do2 = do_ref[pl.ds(start, bq), :]
            prod = do2.astype(F32) * o_ref[pl.ds(start, bq), :].astype(F32)
            zq = jnp.zeros_like(q2)
            dq = jnp.zeros((bq, LANES), F32)
            for hh in range(2):
                sel = m0 if hh == 0 else jnp.logical_not(m0)
                qh = jnp.where(sel, q2, zq)
                doh = jnp.where(sel, do2, zq)
                delta = _head_sum(prod, hh)
                s = _dot_nt(qaug[hh, pl.ds(start, bq), :], ka[hh])
                if masked:
                    s = jnp.where(row >= col, s, NEG)
                p = jnp.exp(s)
                dp = _dot_nt(doh, v2)
                ds = p * (dp - delta)
                dcs[hh] = dcs[hh] - jnp.sum(ds, axis=0, keepdims=True)
                dr_scr[hh, pl.ds(start, bq), :] += jnp.sum(ds, axis=1, keepdims=True)
                dsb = ds.astype(BF16)
                dv = dv + _dot_tn(p.astype(BF16), doh)
                dk = dk + _dot_tn(dsb, qh)
                dq = dq + _dot(dsb, kh[hh])
            dq_scr[pl.ds(start, bq), :] += dq
            return dk, dv, dcs[0], dcs[1]

        zero = jnp.zeros((bq, LANES), F32)
        zrow = jnp.zeros((1, bq), F32)
        carry = step(j, (zero, zero, zrow, zrow), True)
        dk, dv, dc0, dc1 = lax.fori_loop(j + 1, nq, functools.partial(step, masked=False), carry)
        dk_ref[...] = (dk * SCALE).astype(BF16)
        dv_ref[...] = dv.astype(BF16)
        dc_ref[0, 0, 0] = dc0
        dc_ref[0, 1, 0] = dc1

        @pl.when(j == nq - 1)
        def _():
            dq_ref[...] = (dq_scr[...] * SCALE).astype(BF16)
            dr_ref[0] = dr_scr[...]

    sblk = lambda c: pl.BlockSpec((S, LANES), lambda b, p, j, c=c: (b, c + p))
    kblk = lambda c: pl.BlockSpec((bq, LANES), lambda b, p, j, c=c: (b * nq + j, c + p))
    col_spec = pl.BlockSpec((1, 2, S, 1), lambda b, p, j: (b, p, 0, 0))
    return _hosted_call(
        body, comm, f"fox_bwd_{li}", (B, NH // 2, nq),
        in_specs=[sblk(qc), kblk(kc), kblk(vc), sblk(0), sblk(0), col_spec, col_spec],
        out_specs=[sblk(0), kblk(0), kblk(0), pl.BlockSpec((1, 2, 1, 1, bq), lambda b, p, j: (b, p, j, 0, 0)),
                   col_spec],
        out_shape=[jax.ShapeDtypeStruct((T, D), BF16), jax.ShapeDtypeStruct((T, D), BF16),
                   jax.ShapeDtypeStruct((T, D), BF16), jax.ShapeDtypeStruct((B, NH, nq, 1, bq), F32),
                   jax.ShapeDtypeStruct((B, NH, S, 1), F32)],
        scratch=[pltpu.VMEM((S, LANES), F32), pltpu.VMEM((2, S, 1), F32), pltpu.VMEM((2, S, LANES), BF16)],
        dims=("parallel", "parallel", "arbitrary"), operands=(proj, proj, proj, do, o, cum_col, lse))


def _swa_blocks(S):
    bq = min(512, S)
    return bq, S // bq, bq // LCH


def _dup_head(xw, kvl):
    m0 = _lane_iota() < HD
    a = jnp.where(m0 if kvl == 0 else jnp.logical_not(m0), xw, 0.0)
    return (a + pltpu.roll(a, HD, 1)).astype(BF16)


def _band(same_block):
    r = lax.broadcasted_iota(jnp.int32, (LCH, LCH), 0)
    c = lax.broadcasted_iota(jnp.int32, (LCH, LCH), 1)
    return (c <= r) if same_block else (c > r)


def _stack_heads(ref, rows, kvl):
    m0 = _lane_iota() < HD
    parts = []
    for ch in (2 * kvl, 2 * kvl + 1):
        x = ref[rows, LANES * ch:LANES * (ch + 1)]
        parts += [jnp.where(m0, x, jnp.zeros_like(x)), jnp.where(m0, jnp.zeros_like(x), x)]
    return jnp.concatenate(parts, axis=0)


def _stack_delta(do_ref, o_ref, rows, kvl, scale=None):
    parts = []
    for ch in (2 * kvl, 2 * kvl + 1):
        lanes = slice(LANES * ch, LANES * (ch + 1))
        prod = do_ref[rows, lanes].astype(F32) * o_ref[rows, lanes].astype(F32)
        parts += [_head_sum(prod, 0), _head_sum(prod, 1)]
    out = jnp.concatenate(parts, axis=0)
    return out if scale is None else out * scale


def _stack_cols(ref, rows, kvl):
    return jnp.concatenate([ref[0, 4 * kvl + t, rows, :] for t in range(4)], axis=0)


def _swa_fwd(proj, sinks, S, li):
    T = proj.shape[0]
    B = T // S
    bq, nq, nsub = _swa_blocks(S)
    nrow = S // LCH
    qc, zc, kc, vc = OFF_BQ // 512, OFF_BZ // 512, OFF_BK // LANES, OFF_BV // LANES

    def body(sk_ref, q_ref, z_ref, kp_ref, kc_ref, vp_ref, vc_ref, y_ref, o_ref, lse_ref):
        c, i = pl.program_id(0), pl.program_id(2)
        m0 = _lane_iota() < HD
        kw = jnp.concatenate([kp_ref[...].astype(F32), kc_ref[...].astype(F32)], axis=0)
        vw = jnp.concatenate([vp_ref[...].astype(F32), vc_ref[...].astype(F32)], axis=0)
        kd = (_dup_head(kw, 0), _dup_head(kw, 1))
        vd = (_dup_head(vw, 0), _dup_head(vw, 1))
        valid = jnp.concatenate([_band(False), _band(True)], axis=1)
        col = lax.broadcasted_iota(jnp.int32, (LCH, 2 * LCH), 1)
        valid_first = jnp.logical_and(valid, jnp.logical_or(col >= LCH, i > 0))
        valid4 = jnp.concatenate([valid] * 4, axis=0)
        valid4_first = jnp.concatenate([valid_first] * 4, axis=0)
        for r in range(nsub):
            rows = slice(LCH * r, LCH * (r + 1))
            msk = valid4_first if r == 0 else valid4
            for kvl in range(2):
                kwin = kd[kvl][LCH * r:LCH * (r + 2)]
                vwin = vd[kvl][LCH * r:LCH * (r + 2)]
                qs = _stack_heads(q_ref, rows, kvl)
                sink = jnp.concatenate([jnp.full((LCH, 1), sk_ref[8 * c + 4 * kvl + t], F32) for t in range(4)], axis=0)
                s = jnp.where(msk, _dot_nt(qs, kwin) * SCALE, NEG)
                m = jnp.maximum(jnp.max(s, axis=1, keepdims=True), sink)
                p = jnp.exp(s - m)
                l = jnp.sum(p, axis=1, keepdims=True) + jnp.exp(sink - m)
                os_ = _dot(p.astype(BF16), vwin) / l
                lse = m + jnp.log(l)
                for t in range(4):
                    lse_ref[0, 4 * kvl + t, rows, :] = lse[LCH * t:LCH * (t + 1)]
                for u in range(2):
                    lanes = slice(LANES * (2 * kvl + u), LANES * (2 * kvl + u + 1))
                    o2 = jnp.where(m0, os_[LCH * 2 * u:LCH * (2 * u + 1)], os_[LCH * (2 * u + 1):LCH * (2 * u + 2)])
                    z = z_ref[rows, lanes].astype(F32)
                    o_ref[rows, lanes] = o2.astype(BF16)
                    y_ref[rows, lanes] = (o2 * z * _sigmoid(z)).astype(BF16)

    wide = lambda cc: pl.BlockSpec((bq, 512), lambda c, b, i, cc=cc: (b * nq + i, cc + c))
    cur = lambda cc: pl.BlockSpec((bq, LANES), lambda c, b, i, cc=cc: (b * nq + i, cc + c))
    prev = lambda cc: pl.BlockSpec((LCH, LANES), lambda c, b, i, cc=cc: (b * nrow + jnp.maximum(i * nsub - 1, 0), cc + c))
    return pl.pallas_call(
        body, name=f"swa_fwd_{li}", grid=(2, B, nq),
        in_specs=[pl.BlockSpec(memory_space=pltpu.SMEM), wide(qc), wide(zc), prev(kc), cur(kc), prev(vc), cur(vc)],
        out_specs=[wide(0), wide(0), pl.BlockSpec((1, 8, bq, 1), lambda c, b, i: (b, c, i, 0))],
        out_shape=[jax.ShapeDtypeStruct((T, D), BF16), jax.ShapeDtypeStruct((T, D), BF16),
                   jax.ShapeDtypeStruct((B, NH, S, 1), F32)],
        compiler_params=_cparams(("parallel", "parallel", "parallel"), VMEM_LIMIT),
    )(sinks, proj, proj, proj, proj, proj, proj)


def _swa_bwd_dq(proj, do, o, lse, sinks, cos128, sin128, S, li):
    T = proj.shape[0]
    B = T // S
    bq, nq, nsub = _swa_blocks(S)
    nrow = S // LCH
    qc, kc, vc = OFF_BQ // 512, OFF_BK // LANES, OFF_BV // LANES

    def body(sk_ref, q_ref, do_ref, o_ref, lse_ref, kp_ref, kc_ref, vp_ref, vc_ref, cos_ref, sin_ref, dq_ref, dsk_ref):
        c, b, i = pl.program_id(0), pl.program_id(1), pl.program_id(2)

        @pl.when(jnp.logical_and(b == 0, i == 0))
        def _():
            dsk_ref[...] = jnp.zeros_like(dsk_ref)

        m0 = _lane_iota() < HD
        kw = jnp.concatenate([kp_ref[...].astype(F32), kc_ref[...].astype(F32)], axis=0)
        vw = jnp.concatenate([vp_ref[...].astype(F32), vc_ref[...].astype(F32)], axis=0)
        kd = (_dup_head(kw, 0), _dup_head(kw, 1))
        vd = (_dup_head(vw, 0), _dup_head(vw, 1))
        valid = jnp.concatenate([_band(False), _band(True)], axis=1)
        col = lax.broadcasted_iota(jnp.int32, (LCH, 2 * LCH), 1)
        valid_first = jnp.logical_and(valid, jnp.logical_or(col >= LCH, i > 0))
        dsk = [jnp.zeros((1, 1), F32) for _ in range(8)]
        valid4 = jnp.concatenate([valid] * 4, axis=0)
        valid4_first = jnp.concatenate([valid_first] * 4, axis=0)
        for r in range(nsub):
            rows = slice(LCH * r, LCH * (r + 1))
            msk = valid4_first if r == 0 else valid4
            for kvl in range(2):
                kwin = kd[kvl][LCH * r:LCH * (r + 2)]
                vwin = vd[kvl][LCH * r:LCH * (r + 2)]
                qs = _stack_heads(q_ref, rows, kvl)
                dos = _stack_heads(do_ref, rows, kvl)
                delta = _stack_delta(do_ref, o_ref, rows, kvl)
                lse = _stack_cols(lse_ref, rows, kvl)
                sink = jnp.concatenate([jnp.full((LCH, 1), sk_ref[8 * c + 4 * kvl + t], F32) for t in range(4)], axis=0)
                s = jnp.where(msk, _dot_nt(qs, kwin) * SCALE, NEG)
                p = jnp.exp(s - lse)
                ds = p * (_dot_nt(dos, vwin) - delta)
                dqs = _dot(ds.astype(BF16), kwin) * SCALE
                dsink = jnp.exp(sink - lse) * delta
                for t in range(4):
                    hl = 4 * kvl + t
                    dsk[hl] = dsk[hl] - jnp.sum(dsink[LCH * t:LCH * (t + 1)], axis=0, keepdims=True)
                for u in range(2):
                    lanes = slice(LANES * (2 * kvl + u), LANES * (2 * kvl + u + 1))
                    dq2 = jnp.where(m0, dqs[LCH * 2 * u:LCH * (2 * u + 1)], dqs[LCH * (2 * u + 1):LCH * (2 * u + 2)])
                    dq2 = dq2 * cos_ref[rows, :] - _rot_half(dq2) * sin_ref[rows, :]
                    dq_ref[rows, lanes] = dq2.astype(BF16)
        for hl in range(8):
            dsk_ref[0, hl:hl + 1, :] += jnp.broadcast_to(dsk[hl], (1, LANES))

    wide = lambda cc: pl.BlockSpec((bq, 512), lambda c, b, i, cc=cc: (b * nq + i, cc + c))
    cur = lambda cc: pl.BlockSpec((bq, LANES), lambda c, b, i, cc=cc: (b * nq + i, cc + c))
    prev = lambda cc: pl.BlockSpec((LCH, LANES), lambda c, b, i, cc=cc: (b * nrow + jnp.maximum(i * nsub - 1, 0), cc + c))
    pos = pl.BlockSpec((bq, LANES), lambda c, b, i: (i, 0))
    return pl.pallas_call(
        body, name=f"swa_bwd_dq_{li}", grid=(2, B, nq),
        in_specs=[pl.BlockSpec(memory_space=pltpu.SMEM), wide(qc), wide(0), wide(0),
                  pl.BlockSpec((1, 8, bq, 1), lambda c, b, i: (b, c, i, 0)),
                  prev(kc), cur(kc), prev(vc), cur(vc), pos, pos],
        out_specs=[wide(0), pl.BlockSpec((1, 8, LANES), lambda c, b, i: (c, 0, 0))],
        out_shape=[jax.ShapeDtypeStruct((T, D), BF16), jax.ShapeDtypeStruct((2, 8, LANES), F32)],
        compiler_params=_cparams(("arbitrary", "arbitrary", "arbitrary"), VMEM_LIMIT),
    )(sinks, proj, do, o, lse, proj, proj, proj, proj, cos128, sin128)


def _swa_bwd_dkv(proj, do, o, lse, cos128, sin128, S, li):
    T = proj.shape[0]
    B = T // S
    bk, nk, nsub = _swa_blocks(S)
    nrow = S // LCH
    qc, kc, vc = OFF_BQ // 512, OFF_BK // LANES, OFF_BV // LANES

    def body(q_ref, qn_ref, do_ref, don_ref, o_ref, on_ref, lse_ref, lsen_ref, k_ref, v_ref, cos_ref, sin_ref,
             dk_ref, dv_ref):
        j = pl.program_id(2)
        m0 = _lane_iota() < HD
        has_next = (j < nk - 1).astype(F32)
        kf = k_ref[...].astype(F32)
        vf = v_ref[...].astype(F32)
        kd = (_dup_head(kf, 0), _dup_head(kf, 1))
        vd = (_dup_head(vf, 0), _dup_head(vf, 1))
        lane = _lane_iota()

        def stat_rows(lse_r, do_r, o_r, rows, scale):
            a_lse = jnp.zeros((rows, LANES), F32)
            a_del = jnp.zeros((rows, LANES), F32)
            for ch in range(4):
                lanes = slice(LANES * ch, LANES * (ch + 1))
                prod = do_r[:, lanes].astype(F32) * o_r[:, lanes].astype(F32)
                for hh in range(2):
                    h = 2 * ch + hh
                    a_lse = jnp.where(lane == h, lse_r[0, h], a_lse)
                    a_del = jnp.where(lane == h, _head_sum(prod, hh), a_del)
            if scale is not None:
                a_del = a_del * scale
            return a_lse.T, a_del.T

        lse_t, del_t = stat_rows(lse_ref, do_ref, o_ref, bk, None)
        lsen_t, deln_t = stat_rows(lsen_ref, don_ref, on_ref, LCH, has_next)
        r_ = lax.broadcasted_iota(jnp.int32, (LCH, LCH), 0)
        c_ = lax.broadcasted_iota(jnp.int32, (LCH, LCH), 1)
        masks4 = (jnp.concatenate([r_ <= c_] * 4, axis=1), jnp.concatenate([r_ > c_] * 4, axis=1))
        for kr in range(nsub):
            krows = slice(LCH * kr, LCH * (kr + 1))
            dk = jnp.zeros((LCH, LANES), F32)
            dv = jnp.zeros((LCH, LANES), F32)
            for dq_blk in range(2):
                rq = kr + dq_blk
                nxt = rq == nsub
                qrows = slice(0, LCH) if nxt else slice(LCH * rq, LCH * (rq + 1))
                qr, dor = (qn_ref, don_ref) if nxt else (q_ref, do_ref)
                lt, dt_ = (lsen_t, deln_t) if nxt else (lse_t, del_t)
                for kvl in range(2):
                    qs = _stack_heads(qr, qrows, kvl)
                    dos = _stack_heads(dor, qrows, kvl)
                    if nxt:
                        dos = (dos.astype(F32) * has_next).astype(BF16)
                    lse_row = jnp.concatenate([lt[4 * kvl + t:4 * kvl + t + 1, qrows] for t in range(4)], axis=1)
                    del_row = jnp.concatenate([dt_[4 * kvl + t:4 * kvl + t + 1, qrows] for t in range(4)], axis=1)
                    st = jnp.where(masks4[dq_blk], _dot_nt(kd[kvl][krows], qs) * SCALE, NEG)
                    pt = jnp.exp(st - lse_row)
                    dst = pt * (_dot_nt(vd[kvl][krows], dos) - del_row)
                    dvc = _dot(pt.astype(BF16), dos)
                    dkc = _dot(dst.astype(BF16), qs) * SCALE
                    own = m0 if kvl == 0 else jnp.logical_not(m0)
                    dv = dv + jnp.where(own, dvc + pltpu.roll(dvc, HD, 1), 0.0)
                    dk = dk + jnp.where(own, dkc + pltpu.roll(dkc, HD, 1), 0.0)
            dk = dk * cos_ref[krows, :] - _rot_half(dk) * sin_ref[krows, :]
            dk_ref[krows, :] = dk.astype(BF16)
            dv_ref[krows, :] = dv.astype(BF16)

    wide = lambda cc: pl.BlockSpec((bk, 512), lambda c, b, j, cc=cc: (b * nk + j, cc + c))
    nxt = lambda cc: pl.BlockSpec((LCH, 512), lambda c, b, j, cc=cc: (b * nrow + jnp.minimum((j + 1) * nsub, nrow - 1), cc + c))
    cur = lambda cc: pl.BlockSpec((bk, LANES), lambda c, b, j, cc=cc: (b * nk + j, cc + c))
    pos = pl.BlockSpec((bk, LANES), lambda c, b, j: (j, 0))
    return pl.pallas_call(
        body, name=f"swa_bwd_dkv_{li}", grid=(2, B, nk),
        in_specs=[wide(qc), nxt(qc), wide(0), nxt(0), wide(0), nxt(0),
                  pl.BlockSpec((1, 8, bk, 1), lambda c, b, j: (b, c, j, 0)),
                  pl.BlockSpec((1, 8, LCH, 1), lambda c, b, j: (b, c, jnp.minimum((j + 1) * nsub, nrow - 1), 0)),
                  cur(kc), cur(vc), pos, pos],
        out_specs=[cur(0), cur(0)],
        out_shape=[jax.ShapeDtypeStruct((T, 2 * LANES), BF16), jax.ShapeDtypeStruct((T, 2 * LANES), BF16)],
        compiler_params=_cparams(("parallel", "parallel", "parallel"), VMEM_LIMIT),
    )(proj, proj, do, do, o, o, lse, lse, proj, proj, cos128, sin128)


HALO = 16


def _shift_matrices():
    r = lax.broadcasted_iota(jnp.int32, (3 * LCH, LCH + HALO), 0)
    c = lax.broadcasted_iota(jnp.int32, (3 * LCH, LCH + HALO), 1)
    t, d = r % LCH, r // LCH + 1
    return (c == HALO + t - d).astype(BF16), (c == t + d).astype(BF16)


def _ssm_chunk_pre(prev16, cur16, first, sdn_ref, cw_ref, cb_ref, ps, dtb, alog):
    ext16 = jnp.concatenate([jnp.where(first, jnp.zeros_like(prev16), prev16), cur16], axis=0)
    sh = _dot(sdn_ref[...], ext16)
    pre = cb_ref[...] + cw_ref[3:4, :] * cur16.astype(F32)
    for d in range(1, 4):
        pre = pre + cw_ref[3 - d:4 - d, :] * sh[LCH * (d - 1):LCH * d]
    sg = _sigmoid(pre)
    dt = _softplus(ps + dtb)
    a = -jnp.exp(alog)
    r = lax.broadcasted_iota(jnp.int32, (LCH, LCH), 0)
    c = lax.broadcasted_iota(jnp.int32, (LCH, LCH), 1)
    acum = _dot_hi((r >= c).astype(F32), dt * a)
    return pre, sg, dt, a, acum, sh


def _expand_matrix():
    r = lax.broadcasted_iota(jnp.int32, (3 * LANES, D), 0)
    c = lax.broadcasted_iota(jnp.int32, (3 * LANES, D), 1)
    return ((r % LANES) == c // HD).astype(BF16)


def _expand_heads(v, ex_ref):
    return _dot(jnp.concatenate(_split3(v), axis=1).astype(BF16), ex_ref[...])


def _decay(acum, acum_t, h):
    r = lax.broadcasted_iota(jnp.int32, (LCH, LCH), 0)
    c = lax.broadcasted_iota(jnp.int32, (LCH, LCH), 1)
    causal = r >= c
    seg = acum[:, h:h + 1] - acum_t[h:h + 1, :]
    return jnp.where(causal, jnp.exp(jnp.where(causal, seg, 0.0)), 0.0)


def _ssm_pair_fwd(p, x, dt_x, acum, acum_t, e_x, w_x, cd, cb_g, b_g, c_g, hprev, dsk_ref):
    m0 = _lane_iota() < HD
    lanes = slice(LANES * p, LANES * (p + 1))
    x2 = x[:, lanes]
    dt2 = dt_x[:, lanes]
    xdt2 = x2 * dt2
    xdtb = xdt2.astype(BF16)
    lms, ms, yds = [], [], []
    for hh in range(2):
        lm = _decay(acum, acum_t, 2 * p + hh)
        mm = cb_g * lm
        lms.append(lm)
        ms.append(mm)
        yds.append(_dot(mm.astype(BF16), xdtb))
    yd2 = jnp.where(m0, yds[0], yds[1])
    w2 = w_x[:, lanes]
    xw = (xdt2 * w2).astype(BF16)
    s2 = _dot_tn(xw, b_g)
    z2 = _dot_nt(c_g, hprev.astype(BF16))
    e2 = e_x[:, lanes]
    rowsel = lax.broadcasted_iota(jnp.int32, (LANES, 1), 0) < HD
    cdcol = jnp.where(rowsel, cd[:, 2 * p:2 * p + 1], cd[:, 2 * p + 1:2 * p + 2])
    y2 = yd2 + z2 * e2 + dsk_ref[:, lanes] * x2
    return dict(x2=x2, dt2=dt2, xdt2=xdt2, xdtb=xdtb, lms=lms, ms=ms, yd2=yd2, w2=w2, xw=xw, s2=s2, z2=z2, e2=e2,
                cdcol=cdcol, y2=y2)


def _ssm_specs(S, rev):
    nc = S // LCH
    ch = (lambda c: nc - 1 - c) if rev else (lambda c: c)
    prev = pl.BlockSpec((HALO, 2 * D), lambda b, c: (jnp.maximum(b * (S // HALO) + ch(c) * (LCH // HALO) - 1, 0), 0))
    cur = pl.BlockSpec((LCH, 2 * D), lambda b, c: (b * nc + ch(c), 0))
    zed = pl.BlockSpec((LCH, D), lambda b, c: (b * nc + ch(c), OFF_AZ // D))
    row = pl.BlockSpec((LCH, D), lambda b, c: (b * nc + ch(c), 0))
    psb = pl.BlockSpec((LCH, LANES), lambda b, c: (b * nc + ch(c), 0))
    hpb = pl.BlockSpec((1, 1, NH // 2, LANES, NST), lambda b, c: (b, ch(c), 0, 0, 0))
    const = lambda r, w: pl.BlockSpec((r, w), lambda b, c: (0, 0))
    return nc, prev, cur, zed, row, psb, hpb, const


def _ssm_fwd(proj, ps, cw, cb, dtb, alog, dsk, nw, S, li):
    T = proj.shape[0]
    B = T // S
    nc, prev, cur, zed, row, psb, hpb, const = _ssm_specs(S, False)

    def body(prev_ref, cur_ref, z_ref, ps_ref, sdn_ref, ex_ref, cw_ref, cb_ref, dtb_ref, alog_ref, dsk_ref, nw_ref,
             ya_ref, hp_ref, h_scr):
        c = pl.program_id(1)

        @pl.when(c == 0)
        def _():
            h_scr[...] = jnp.zeros_like(h_scr)

        pre, sg, dt, a, acum, _ = _ssm_chunk_pre(prev_ref[...], cur_ref[...], c == 0, sdn_ref, cw_ref, cb_ref,
                                                 ps_ref[...], dtb_ref[...], alog_ref[...])
        act = pre * sg
        acum_t = acum.T
        last = acum[LCH - 1:LCH, :]
        cd = jnp.exp(last)
        dt, e_all, w_all = (_expand_heads(v, ex_ref) for v in (dt, jnp.exp(acum), jnp.exp(last - acum)))
        x = act[:, :D]
        for g in range(NGRP):
            b_g = act[:, D + NST * g:D + NST * (g + 1)].astype(BF16)
            c_g = act[:, D + NGRP * NST + NST * g:D + NGRP * NST + NST * (g + 1)].astype(BF16)
            cb_g = _dot_nt(c_g, b_g)
            ygs = []
            for p in (2 * g, 2 * g + 1):
                hprev = h_scr[p]
                hp_ref[0, 0, p] = hprev
                f = _ssm_pair_fwd(p, x, dt, acum, acum_t, e_all, w_all, cd, cb_g, b_g, c_g, hprev, dsk_ref)
                h_scr[p] = hprev * f["cdcol"] + f["s2"]
                z2 = z_ref[:, LANES * p:LANES * (p + 1)].astype(F32)
                ygs.append(f["y2"] * z2 * _sigmoid(z2))
            yg = jnp.concatenate(ygs, axis=1)
            r = lax.rsqrt(jnp.mean(yg * yg, axis=1, keepdims=True) + EPS)
            ya_ref[:, 2 * LANES * g:2 * LANES * (g + 1)] = (yg * r * nw_ref[:, 2 * LANES * g:2 * LANES * (g + 1)]).astype(BF16)

    return pl.pallas_call(
        body, name=f"ssm_fwd_{li}", grid=(B, nc),
        in_specs=[prev, cur, zed, psb, const(3 * LCH, LCH + HALO), const(3 * LANES, D), const(4, 2 * D),
                  const(1, 2 * D), const(1, LANES), const(1, LANES), const(1, D), const(1, D)],
        out_specs=[row, hpb],
        out_shape=[jax.ShapeDtypeStruct((T, D), BF16), jax.ShapeDtypeStruct((B, nc, NH // 2, LANES, NST), F32)],
        scratch_shapes=[pltpu.VMEM((NH // 2, LANES, NST), F32)],
        compiler_params=_cparams(("arbitrary", "arbitrary"), VMEM_LIMIT),
    )(proj, proj, proj, ps, _shift_matrices()[0], _expand_matrix(), cw, cb, dtb, alog, dsk, nw)


def _ssm_bwd(proj, ps, hp, dya, cw, cb, dtb, alog, dsk, nw, S, li, comm=None):
    T = proj.shape[0]
    B = T // S
    nc, prev, cur, zed, row, psb, hpb, const = _ssm_specs(S, True)

    def body(prev_ref, cur_ref, z_ref, ps_ref, hp_ref, dy_ref, sdn_ref, sup_ref, ex_ref, cw_ref, cb_ref, dtb_ref,
             alog_ref, dsk_ref, nw_ref, dxbc_ref, dz_ref, dps_ref, pgw_ref, pg1_ref, pgh_ref, dh_scr, dhead, dact):
        b, cc = pl.program_id(0), pl.program_id(1)
        c = nc - 1 - cc

        @pl.when(jnp.logical_and(b == 0, cc == 0))
        def _():
            pgw_ref[...] = jnp.zeros_like(pgw_ref)
            pg1_ref[...] = jnp.zeros_like(pg1_ref)
            pgh_ref[...] = jnp.zeros_like(pgh_ref)

        @pl.when(cc == 0)
        def _():
            dh_scr[...] = jnp.zeros_like(dh_scr)
            dhead[...] = jnp.zeros_like(dhead)

        psv = ps_ref[...]
        cur16 = cur_ref[...]
        pre, sg, dt, a, acum, sh = _ssm_chunk_pre(prev_ref[...], cur16, c == 0, sdn_ref, cw_ref, cb_ref, psv,
                                                  dtb_ref[...], alog_ref[...])
        act = pre * sg
        acum_t = acum.T
        last = acum[LCH - 1:LCH, :]
        w_all = jnp.exp(last - acum)
        cd = jnp.exp(last)
        dt_x, e_x, w_x = (_expand_heads(v, ex_ref) for v in (dt, jnp.exp(acum), w_all))
        x = act[:, :D]
        lane = _lane_iota()
        m0 = lane < HD
        head_row = lax.broadcasted_iota(jnp.int32, (LANES, 1), 0)
        rowsel = head_row < HD
        is_last_row = lax.broadcasted_iota(jnp.int32, (LCH, 1), 0) == LCH - 1
        dacum_all = jnp.zeros((LCH, LANES), F32)
        dacum_t = jnp.zeros((LANES, LCH), F32)
        ddt_all = jnp.zeros((LCH, LANES), F32)
        dd_row = jnp.zeros((1, LANES), F32)
        for g in range(NGRP):
            b_g = act[:, D + NST * g:D + NST * (g + 1)].astype(BF16)
            c_g = act[:, D + NGRP * NST + NST * g:D + NGRP * NST + NST * (g + 1)].astype(BF16)
            cb_g = _dot_nt(c_g, b_g)
            pairs = (2 * g, 2 * g + 1)
            fs, hps, zs, ygs = [], [], [], []
            for p in pairs:
                hprev = hp_ref[0, 0, p]
                f = _ssm_pair_fwd(p, x, dt_x, acum, acum_t, e_x, w_x, cd, cb_g, b_g, c_g, hprev, dsk_ref)
                z2 = z_ref[:, LANES * p:LANES * (p + 1)].astype(F32)
                fs.append(f)
                hps.append(hprev)
                zs.append(z2)
                ygs.append(f["y2"] * z2 * _sigmoid(z2))
            gl = slice(2 * LANES * g, 2 * LANES * (g + 1))
            yg = jnp.concatenate(ygs, axis=1)
            r = lax.rsqrt(jnp.mean(yg * yg, axis=1, keepdims=True) + EPS)
            dyn = dy_ref[:, gl].astype(F32)
            gg = dyn * nw_ref[:, gl]
            dyg = r * gg - yg * (r * r * r) * jnp.mean(gg * yg, axis=1, keepdims=True)
            pg1_ref[0:1, gl] += jnp.sum(dyn * yg * r, axis=0, keepdims=True)
            dg_g = jnp.zeros((LCH, LCH), F32)
            db_g = jnp.zeros((LCH, NST), F32)
            dc_g = jnp.zeros((LCH, NST), F32)
            for idx, p in enumerate(pairs):
                f, hprev, z2 = fs[idx], hps[idx], zs[idx]
                lanes = slice(LANES * p, LANES * (p + 1))
                dyg2 = dyg[:, LANES * idx:LANES * (idx + 1)]
                sgz = _sigmoid(z2)
                dy2 = dyg2 * z2 * sgz
                dz_ref[:, lanes] = (dyg2 * f["y2"] * sgz * (1.0 + z2 * (1.0 - sgz))).astype(BF16)
                x2, dt2, xdt2, xdtb, w2, e2, z2m = f["x2"], f["dt2"], f["xdt2"], f["xdtb"], f["w2"], f["e2"], f["z2"]
                dx2 = dsk_ref[:, lanes] * dy2
                dyx = dy2 * x2
                dxdt2 = jnp.zeros((LCH, LANES), F32)
                diag_cols = []
                for hh in range(2):
                    sel = m0 if hh == 0 else jnp.logical_not(m0)
                    dyb = jnp.where(sel, dy2, 0.0).astype(BF16)
                    dm = _dot_nt(dyb, xdtb)
                    dg_g = dg_g + dm * f["lms"][hh]
                    dxdt2 = dxdt2 + _dot_tn(f["ms"][hh].astype(BF16), dyb)
                    em = dm * f["ms"][hh]
                    diag_cols.append(jnp.sum(em, axis=1, keepdims=True))
                    dacum_t = dacum_t - jnp.where(head_row == 2 * p + hh, jnp.sum(em, axis=0, keepdims=True), 0.0)
                dz2m = dy2 * e2
                t_off = dz2m * z2m
                dc_g = dc_g + _dot(dz2m.astype(BF16), hprev.astype(BF16))
                dhprev = _dot_tn(dz2m.astype(BF16), c_g)
                dhn = dh_scr[p]
                dhnb = dhn.astype(BF16)
                dhprev = dhprev + dhn * f["cdcol"]
                t_h = dhn * hprev
                dxw2 = _dot_nt(b_g, dhnb)
                db_g = db_g + _dot(f["xw"], dhnb)
                dxdt2 = dxdt2 + dxw2 * w2
                t_w = dxw2 * xdt2
                dx2 = dx2 + dxdt2 * dt2
                t_dt = dxdt2 * x2
                for hh in range(2):
                    h = 2 * p + hh
                    onehot = (lane == h).astype(F32)
                    w_col = w_all[:, h:h + 1]
                    dw_col = _head_sum(t_w, hh) * w_col
                    rs = rowsel if hh == 0 else jnp.logical_not(rowsel)
                    dlast = (jnp.sum(jnp.where(rs, t_h, 0.0), keepdims=True) * cd[:, h:h + 1]
                             + jnp.sum(dw_col, keepdims=True))
                    dacum_col = diag_cols[hh] + _head_sum(t_off, hh) - dw_col + jnp.where(is_last_row, dlast, 0.0)
                    dacum_all = dacum_all + dacum_col * onehot
                    ddt_all = ddt_all + _head_sum(t_dt, hh) * onehot
                    sel = m0 if hh == 0 else jnp.logical_not(m0)
                    dd_row = dd_row + jnp.sum(jnp.where(sel, dyx, 0.0), keepdims=True) * onehot
                dh_scr[p] = dhprev
                dact[:, lanes] = dx2
            dgb = dg_g.astype(BF16)
            dc_g = dc_g + _dot(dgb, b_g)
            db_g = db_g + _dot_tn(dgb, c_g)
            dact[:, D + NST * g:D + NST * (g + 1)] = db_g
            dact[:, D + NGRP * NST + NST * g:D + NGRP * NST + NST * (g + 1)] = dc_g
        rr = lax.broadcasted_iota(jnp.int32, (LCH, LCH), 0)
        cc2 = lax.broadcasted_iota(jnp.int32, (LCH, LCH), 1)
        dadt = _dot_hi((cc2 >= rr).astype(F32), dacum_all + dacum_t.T)
        ddt_all = ddt_all + dadt * a
        heads = lane < NH
        da = jnp.sum(dadt * dt, axis=0, keepdims=True)
        dr = jnp.where(heads, ddt_all * _sigmoid(psv + dtb_ref[...]), 0.0)
        dps_ref[...] = dr
        pgh_ref[0:1, :] += jnp.sum(dr, axis=0, keepdims=True)
        pgh_ref[1:2, :] += jnp.where(heads, da * a, 0.0)
        pgh_ref[2:3, :] += dd_row
        dpre = dact[...] * sg * (1.0 + pre * (1.0 - sg))
        extd = jnp.concatenate([dpre, dhead[...]], axis=0)
        hi = extd.astype(BF16)
        lo = (extd - hi.astype(F32)).astype(BF16)
        up = _dot(sup_ref[...], hi) + _dot(sup_ref[...], lo)
        du = cw_ref[3:4, :] * dpre
        pgw_ref[3:4, :] += jnp.sum(dpre * cur16.astype(F32), axis=0, keepdims=True)
        for d in range(1, 4):
            du = du + cw_ref[3 - d:4 - d, :] * up[LCH * (d - 1):LCH * d]
            pgw_ref[3 - d:4 - d, :] += jnp.sum(dpre * sh[LCH * (d - 1):LCH * d], axis=0, keepdims=True)
        pgw_ref[4:5, :] += jnp.sum(dpre, axis=0, keepdims=True)
        dxbc_ref[...] = du.astype(BF16)
        dhead[...] = dpre[0:HALO, :]

    xbc_out = pl.BlockSpec((LCH, 2 * D), lambda b, c: (b * nc + nc - 1 - c, 0))
    acc = lambda w: pl.BlockSpec((8, w), lambda b, c: (0, 0))
    sdn, sup = _shift_matrices()
    return _hosted_call(
        body, comm, f"ssm_bwd_{li}", (B, nc),
        in_specs=[prev, cur, zed, psb, hpb, row, const(3 * LCH, LCH + HALO), const(3 * LCH, LCH + HALO),
                  const(3 * LANES, D), const(4, 2 * D), const(1, 2 * D), const(1, LANES), const(1, LANES),
                  const(1, D), const(1, D)],
        out_specs=[xbc_out, row, psb, acc(2 * D), acc(D), acc(LANES)],
        out_shape=[jax.ShapeDtypeStruct((T, 2 * D), BF16), jax.ShapeDtypeStruct((T, D), BF16),
                   jax.ShapeDtypeStruct((T, LANES), F32), jax.ShapeDtypeStruct((8, 2 * D), F32),
                   jax.ShapeDtypeStruct((8, D), F32), jax.ShapeDtypeStruct((8, LANES), F32)],
        scratch=[pltpu.VMEM((NH // 2, LANES, NST), F32), pltpu.VMEM((HALO, 2 * D), F32),
                 pltpu.VMEM((LCH, 2 * D), F32)],
        dims=("arbitrary", "arbitrary"),
        operands=(proj, proj, proj, ps, hp, dya, sdn, sup, _expand_matrix(), cw, cb, dtb, alog, dsk, nw))


def _lane_row(v, offset):
    return jnp.pad(v.astype(F32), (offset, LANES - offset - v.shape[0]))[None]


def _pack_rows(arrays):
    parts = []
    for a in arrays:
        flat = a.reshape(-1).astype(F32)
        pad = (-flat.shape[0]) % LANES
        parts.append(jnp.pad(flat, (0, pad)))
    flat = jnp.concatenate(parts)
    pad = (-flat.shape[0]) % (8 * LANES)
    return jnp.pad(flat, (0, pad)).reshape(-1, LANES)


def _unpack_rows(pack, shapes):
    flat = pack.reshape(-1)
    out, pos = [], 0
    for shp in shapes:
        n = math.prod(shp)
        out.append(flat[pos:pos + n].reshape(shp))
        pos += n + (-n) % LANES
    return out


def _split_w_in(w):
    main = jnp.concatenate([w[:, 0:3072], w[:, 3088:4112], w[:, 4624:5648], w[:, 5648:8720], w[:, 8736:12832],
                            w[:, 4112:4624]], axis=1)
    small = jnp.concatenate([w[:, 3072:3088], w[:, 8720:8736], jnp.zeros((D, LANES - 2 * NH), w.dtype)], axis=1)
    return main, small


def _join_w_in(dw, ds):
    xbc, az, bq, bz, cq, ck, cv, cz, gates, bk, bv = dw
    return jnp.concatenate([xbc, az, ds[:, 0:NH], bq, bk, bv, bz, cq, ck, cv, ds[:, NH:2 * NH], cz, gates], axis=1)


def kernel(x, norm_w, w_in, conv_w, conv_b, dt_bias, a_log, d_skip, ssm_norm_w, sinks, f_bias, gate_bias, w_proj, w_out, final_norm_w, loss_target, m_norm_w, m_w_in, m_conv_w, m_conv_b, m_dt_bias, m_a_log, m_d_skip, m_ssm_norm_w, m_sinks, m_f_bias, m_gate_bias, m_w_proj, m_w_out, m_final_norm_w, v_norm_w, v_w_in, v_conv_w, v_conv_b, v_dt_bias, v_a_log, v_d_skip, v_ssm_norm_w, v_sinks, v_f_bias, v_gate_bias, v_w_proj, v_w_out, v_final_norm_w):
    Bl, S, _ = x.shape
    T = Bl * S
    depth = norm_w.shape[0]
    me = 4 * lax.axis_index("x") + 2 * lax.axis_index("y") + lax.axis_index("c")
    csh, gsh = conv_w.shape[2], gate_bias.shape[2]

    def gather_plan(l):
        small = jnp.concatenate([conv_w[l].reshape(-1), gate_bias[l].reshape(-1)]).reshape(-1, LANES)
        return _Comm("gather", [w_in[l].astype(BF16), w_proj[l].astype(BF16), w_out[l].astype(BF16), small])

    def unpack_weights(res):
        g_win, g_wp, g_wo, g_small = res
        flat = g_small.reshape(NDEV, -1)
        return (g_win.transpose(1, 0, 2).reshape(D, NIN),
                g_wp.transpose(1, 0, 2, 3).reshape(3, D, D),
                g_wo.reshape(D, D),
                flat[:, :4 * csh].reshape(NDEV, 4, csh).transpose(1, 0, 2).reshape(4, 2 * D),
                flat[:, 4 * csh:].reshape(NDEV, 3, gsh).transpose(1, 0, 2).reshape(3, D))

    def scatter_plan(gw_in=None, gw_p=None, gw_o=None):
        arrays = [] if gw_in is None else [gw_in.astype(BF16).reshape(-1, NDEV, NSH).transpose(1, 0, 2)]
        if gw_p is not None:
            arrays += [gw_p.astype(BF16).reshape(3, NDEV, D // NDEV, D).transpose(1, 0, 2, 3),
                       gw_o.astype(BF16).reshape(NDEV, D // NDEV, D)]
        return _Comm("scatter", arrays)

    pos = jnp.arange(S, dtype=F32)
    inv_freq = ROPE_THETA ** (-jnp.arange(0, HD, 2, dtype=F32) / HD)
    ang = pos[:, None] * inv_freq[None, :]
    cos128 = jnp.tile(jnp.cos(ang), (1, 4))
    sign = jnp.where((jnp.arange(LANES) % HD) < HD // 2, -1.0, 1.0).astype(F32)
    sin128 = jnp.tile(jnp.sin(ang), (1, 4)) * sign[None, :]

    bq, nq = _fox_blocks(S)
    x2 = x.reshape(T, D)
    tgt2 = loss_target.reshape(T, D)

    saved = []
    xcur = x2
    weights = [None] * depth
    weights[0] = unpack_weights(_gather_two_level(gather_plan(0).arrays, "gather_weights_0"))
    for l in range(depth):
        win_l, wp_l, wo_l, cw_l, gb_l = weights[l]
        wmain, wsmall = _split_w_in(win_l)
        proj, ps, h_t = _inproj_fwd(xcur, norm_w[l][None], wmain, wsmall, cos128, sin128, S, l)
        dtb = _lane_row(dt_bias[l], 0)
        alog = _lane_row(a_log[l], 0)
        fb = _lane_row(f_bias[l], NH)
        dsk = jnp.repeat(d_skip[l], HD)[None]
        ya, hp = _ssm_fwd(proj, ps, cw_l, conv_b[l][None], dtb, alog, dsk, ssm_norm_w[l][None], S, l)
        yb, ob, lse_b = _swa_fwd(proj, sinks[l], S, l)
        cum = _fox_cum(ps, fb, S, l)
        cumh = cum[:, NH:2 * NH].reshape(Bl, S, NH).transpose(0, 2, 1)
        cum_col = cumh[..., None]
        comm = gather_plan(l + 1) if l + 1 < depth else None
        res = _fox_fwd(proj, cum_col, S, l, comm)
        yc, oc, lse_c = res[:3]
        if comm is not None:
            weights[l + 1] = unpack_weights(res[3:])
        xnext, br, y_t = _merge_fwd(ya, yb, yc, proj, gb_l, wp_l, wo_l, xcur, l)
        saved.append(dict(x=xcur, wmain=wmain, wsmall=wsmall, proj=proj, ps=ps, h_t=h_t, dtb=dtb, alog=alog, fb=fb,
                          dsk=dsk, hp=hp, ob=ob, lse_b=lse_b, cum_col=cum_col, oc=oc, lse_c=lse_c, br=br, y_t=y_t))
        xcur = xnext

    dx, dx16, st = _final_loss(xcur, tgt2, final_norm_w[None])
    loss_part = st[2, 0]
    g_final = st[0]

    gsm = {k: [None] * depth for k in ("norm_w", "conv_w", "conv_b", "dt_bias", "a_log", "d_skip", "ssm_norm_w",
                                      "sinks", "f_bias", "gate_bias")}
    parts = [None] * depth
    pending = None
    for l in reversed(range(depth)):
        sv = saved[l]
        proj, ps = sv["proj"], sv["ps"]
        _, wp_l, wo_l, cw_l, gb_l = weights[l]
        dbr, dgates, merged_t, dgb, dy_a, do_b, dbz, do_c, dcz = _merge_bwd(dx16, wo_l, wp_l, sv["br"], proj, gb_l,
                                                                            sv["ob"], sv["oc"], l)
        g_wo = _matmul(merged_t, dx16, BF16, f"dwout_{l}")
        g_wp = _matmul_batched(sv["y_t"], dbr, BF16, f"dwproj_{l}")
        gsm["gate_bias"][l] = dgb[0:3]
        res = _ssm_bwd(proj, ps, sv["hp"], dy_a, cw_l, conv_b[l][None], sv["dtb"], sv["alog"], sv["dsk"],
                       ssm_norm_w[l][None], S, l, pending)
        dxbc, daz, dps_a, pgw, pg1, pgh = res[:6]
        if pending is not None:
            parts[l + 1] = res[6:]
        gsm["conv_w"][l], gsm["conv_b"][l] = pgw[0:4], pgw[4]
        gsm["ssm_norm_w"][l] = pg1[0]
        gsm["dt_bias"][l], gsm["a_log"][l], gsm["d_skip"][l] = pgh[0, :NH], pgh[1, :NH], pgh[2, :NH]
        dq_b, dsk_b = _swa_bwd_dq(proj, do_b, sv["ob"], sv["lse_b"], sinks[l], cos128, sin128, S, l)
        dk_b, dv_b = _swa_bwd_dkv(proj, do_b, sv["ob"], sv["lse_b"], cos128, sin128, S, l)
        gsm["sinks"][l] = dsk_b[:, :, 0].reshape(NH)
        plan_po = scatter_plan(None, g_wp, g_wo) if l == 0 else None
        res = _fox_bwd(proj, do_c, sv["oc"], sv["cum_col"], sv["lse_c"], S, l, plan_po)
        dq_c, dk_c, dv_c, dcum_k, dcum_q = res[:5]
        dcum_tm = (dcum_k.reshape(Bl, NH, S) + dcum_q.reshape(Bl, NH, S)).transpose(0, 2, 1).reshape(T, NH)
        dcum_pad = jnp.pad(dcum_tm, ((0, 0), (NH, LANES - 2 * NH)))
        df, dfb = _fox_cum_bwd(dcum_pad, ps, sv["fb"], S, l)
        gsm["f_bias"][l] = dfb[0, NH:2 * NH]
        dps16 = (dps_a + df).astype(BF16)
        pieces = (dxbc, daz, dq_b, dbz, dq_c, dk_c, dv_c, dcz, dgates, dk_b, dv_b)
        dw_pieces = [_matmul(sv["h_t"], pc, BF16, f"dwin_{l}_{i}") for i, pc in enumerate(pieces)]
        dws = _matmul(sv["h_t"], dps16, BF16, f"dwin_small_{l}")
        g_win = _join_w_in(dw_pieces, dws)
        if l == 0:
            plans = [scatter_plan(g_win[r0:r1]) for r0, r1 in ROW_CHUNKS]
            parts_po = res[5:]
        else:
            plans, pending = [None] * len(ROW_CHUNKS), scatter_plan(g_win, g_wp, g_wo)
        dkv_b = jnp.concatenate([dk_b, dv_b], axis=1)
        res1 = _inproj_bwd_dx([(dxbc, OFF_XBC), (daz, OFF_AZ), (dq_b, OFF_BQ), (dbz, OFF_BZ)], sv["wmain"],
                              ("narrow", dps16, sv["wsmall"]), None, f"inproj_bwd_dh1_{l}", plans[0])
        res2 = _inproj_bwd_dx([(dq_c, OFF_CQ), (dk_c, OFF_CK), (dv_c, OFF_CV), (dcz, OFF_CZ)], sv["wmain"],
                              ("acc", res1[0]), None, f"inproj_bwd_dh2_{l}", plans[1])
        res3 = _inproj_bwd_dx([(dgates, OFF_G), (dkv_b, OFF_BK)], sv["wmain"], ("acc", res2[0]),
                              (sv["x"], norm_w[l][None], dx), f"inproj_bwd_dx_{l}", plans[2])
        dx, dx16, dnw = res3[:3]
        if l == 0:
            parts[0] = [jnp.concatenate([res1[1], res2[1], res3[3]], axis=1), *parts_po]
        gsm["norm_w"][l] = dnw[0]

    big = {}
    for idx, (name, w, m, v) in enumerate((("w_in", w_in, m_w_in, v_w_in), ("w_proj", w_proj, m_w_proj, v_w_proj),
                                          ("w_out", w_out, m_w_out, v_w_out))):
        cols = w.shape[-1]
        res = _sum_adamw([parts[l][idx].reshape(NDEV, -1, cols) for l in range(depth)], w.reshape(depth, -1, cols),
                         m.reshape(depth, -1, cols), v.reshape(depth, -1, cols), f"adamw_{name}")
        big[name] = [r.reshape(w.shape) for r in res]

    small_names = ("norm_w", "conv_b", "dt_bias", "a_log", "d_skip", "ssm_norm_w", "sinks", "f_bias")
    small_parts = [jnp.stack(gsm[k]) for k in small_names] + [g_final, jnp.stack(gsm["conv_w"]),
                                                              jnp.stack(gsm["gate_bias"]), loss_part.reshape(1)]
    shapes = [a.shape for a in small_parts]
    summed = _unpack_rows(_all_reduce_small(_pack_rows(small_parts)), shapes)
    g_small = dict(zip(small_names, summed[:len(small_names)]))
    g_small["final_norm_w"] = summed[len(small_names)]
    g_small["conv_w"] = lax.dynamic_slice_in_dim(summed[len(small_names) + 1], me * csh, csh, axis=2)
    g_small["gate_bias"] = lax.dynamic_slice_in_dim(summed[len(small_names) + 2], me * gsh, gsh, axis=2)
    loss = summed[len(small_names) + 3][0]

    ws = dict(norm_w=norm_w, conv_w=conv_w, conv_b=conv_b, dt_bias=dt_bias, a_log=a_log, d_skip=d_skip,
              ssm_norm_w=ssm_norm_w, sinks=sinks, f_bias=f_bias, gate_bias=gate_bias, final_norm_w=final_norm_w)
    ms = dict(norm_w=m_norm_w, conv_w=m_conv_w, conv_b=m_conv_b, dt_bias=m_dt_bias, a_log=m_a_log, d_skip=m_d_skip,
              ssm_norm_w=m_ssm_norm_w, sinks=m_sinks, f_bias=m_f_bias, gate_bias=m_gate_bias,
              final_norm_w=m_final_norm_w)
    vs = dict(norm_w=v_norm_w, conv_w=v_conv_w, conv_b=v_conv_b, dt_bias=v_dt_bias, a_log=v_a_log, d_skip=v_d_skip,
              ssm_norm_w=v_ssm_norm_w, sinks=v_sinks, f_bias=v_f_bias, gate_bias=v_gate_bias,
              final_norm_w=v_final_norm_w)
    order = list(ws)
    oshapes = [ws[k].shape for k in order]
    res = _adamw_small(_pack_rows([g_small[k] for k in order]), _pack_rows([ws[k] for k in order]),
                       _pack_rows([ms[k] for k in order]), _pack_rows([vs[k] for k in order]))
    d_s, m_s, v_s = (dict(zip(order, _unpack_rows(r, oshapes))) for r in res)

    names = ("norm_w", "w_in", "conv_w", "conv_b", "dt_bias", "a_log", "d_skip", "ssm_norm_w", "sinks", "f_bias",
             "gate_bias", "w_proj", "w_out", "final_norm_w")
    grads, deltas, new_m, new_v = [], [], [], []
    for k in names:
        if k in big:
            g, d_, m_, v_ = big[k]
        else:
            g, d_, m_, v_ = g_small[k], d_s[k], m_s[k], v_s[k]
        grads.append(g)
        deltas.append(d_)
        new_m.append(m_)
        new_v.append(v_)
    return (loss, dx.reshape(Bl, S, D), *grads, *deltas, *new_m, *new_v)
```

```python
import functools
import math

import jax
import jax.numpy as jnp
from jax import lax
from jax.experimental import pallas as pl
from jax.experimental.pallas import tpu as pltpu

F32 = jnp.float32
BF16 = jnp.bfloat16
MESH = pl.DeviceIdType.MESH
NDEV = 8

D = 1024
NH = 16
HD = 64
NST = 128
NGRP = 4
LCH = 128
EPS = 1e-6
ROPE_THETA = 10000.0
SCALE = HD ** -0.5
NEG = -1e30

LANES = 128
VMEM_LIMIT = 56 * 1024 * 1024

OFF_XBC, OFF_AZ, OFF_BQ, OFF_BZ, OFF_CQ, OFF_CK, OFF_CV, OFF_CZ, OFF_G, OFF_BK, OFF_BV = (
    0, 2048, 3072, 4096, 5120, 6144, 7168, 8192, 9216, 12288, 12544)
NMAIN = 12800
NIN = 12832
NSH = NIN // NDEV

ROW_CHUNKS = ((0, 512), (512, 1024))

ADAM_LR, ADAM_B1, ADAM_B2, ADAM_EPS, ADAM_WD, ADAM_STEP = 0.001, 0.9, 0.999, 1e-08, 0.01, 10


def _cparams(dims=None, vmem=None):
    return pltpu.CompilerParams(dimension_semantics=dims, vmem_limit_bytes=vmem)


def _dot(a, b):
    return jnp.dot(a, b, preferred_element_type=F32)


def _dot_nt(a, b):
    return lax.dot_general(a, b, (((1,), (1,)), ((), ())), preferred_element_type=F32)


def _dot_tn(a, b):
    return lax.dot_general(a, b, (((0,), (0,)), ((), ())), preferred_element_type=F32)


def _dot_hi(a, b):
    return jnp.dot(a, b, precision=lax.Precision.HIGHEST, preferred_element_type=F32)


def _sigmoid(x):
    return 0.5 * jnp.tanh(0.5 * x) + 0.5


def _softplus(x):
    return jnp.maximum(x, 0.0) + jnp.log(1.0 + jnp.exp(-jnp.abs(x)))


def _lane_iota(n=LANES):
    return lax.broadcasted_iota(jnp.int32, (1, n), 1)


def _rot_half(x):
    first = (_lane_iota() % HD) < (HD // 2)
    return jnp.where(first, pltpu.roll(x, LANES - HD // 2, 1), pltpu.roll(x, HD // 2, 1))


def _head_sum(x, head):
    m = (_lane_iota() < HD) if head == 0 else (_lane_iota() >= HD)
    return jnp.sum(jnp.where(m, x, 0.0), axis=1, keepdims=True)


def _me_and_peers():
    x, y, c = lax.axis_index("x"), lax.axis_index("y"), lax.axis_index("c")
    me = 4 * x + 2 * y + c
    peers = []
    for k in range(1, NDEV):
        kx, ky, kc = (k >> 2) & 1, (k >> 1) & 1, k & 1
        px, py, pc = x ^ kx, y ^ ky, c ^ kc
        peers.append(((px, py, pc), 4 * px + 2 * py + pc))
    return me, peers


class _Comm:
    def __init__(self, kind, arrays):
        self.kind, self.arrays, self.n = kind, list(arrays), len(arrays)
        any_spec = pl.BlockSpec(memory_space=pl.ANY)
        self.in_specs = [any_spec] * self.n
        self.out_specs = [any_spec] * self.n
        self.out_shape = [jax.ShapeDtypeStruct(((NDEV,) + a.shape) if kind == "gather" else a.shape, a.dtype)
                          for a in self.arrays]
        self.scratch = [pltpu.SemaphoreType.DMA((self.n, NDEV - 1)), pltpu.SemaphoreType.DMA((self.n, NDEV - 1)),
                        pltpu.SemaphoreType.DMA((self.n,))]

    def copies(self, ins, outs, sems):
        send_sems, recv_sems, local_sems = sems
        me, peers = _me_and_peers()
        out = []
        for a in range(self.n):
            mine = ins[a] if self.kind == "gather" else ins[a].at[me]
            out.append(pltpu.make_async_copy(mine, outs[a].at[me], local_sems.at[a]))
            for k, (peer, pidx) in enumerate(peers):
                src = ins[a] if self.kind == "gather" else ins[a].at[pidx]
                out.append(pltpu.make_async_remote_copy(
                    src_ref=src, dst_ref=outs[a].at[me], send_sem=send_sems.at[a, k], recv_sem=recv_sems.at[a, k],
                    device_id=peer, device_id_type=MESH))
        return out


def _gather_two_level(arrays, name):
    n = len(arrays)

    def body(*refs):
        ins, outs = refs[:n], refs[n:2 * n]
        send_sems, recv_sems, local_sems = refs[2 * n:]
        x, y, c = lax.axis_index("x"), lax.axis_index("y"), lax.axis_index("c")
        me, sibling = (x, y, c), (x, y, 1 - c)
        chips = [(1 - x, y), (x, 1 - y), (1 - x, 1 - y)]

        def slot(a, dev):
            return outs[a].at[4 * dev[0] + 2 * dev[1] + dev[2]]

        def copy(a, k, block, to, src=None):
            return pltpu.make_async_remote_copy(
                src_ref=slot(a, block) if src is None else src, dst_ref=slot(a, block),
                send_sem=send_sems.at[a, k], recv_sem=recv_sems.at[a, k], device_id=to, device_id_type=MESH)

        mine = [pltpu.make_async_copy(ins[a], slot(a, me), local_sems.at[a]) for a in range(n)]
        for cp in mine:
            cp.start()
        first = []
        for a in range(n):
            first.append(copy(a, 0, me, sibling, src=ins[a]))
            first += [copy(a, 1 + j, me, (*chip, c), src=ins[a]) for j, chip in enumerate(chips)]
        for cp in first:
            cp.start()
        passed = []
        for j, chip in enumerate(chips):
            for a in range(n):
                copy(a, 1 + j, (*chip, c), me).wait_recv()
                fwd = copy(a, 4 + j, (*chip, c), sibling)
                fwd.start()
                passed.append(fwd)
        for a in range(n):
            copy(a, 0, sibling, me).wait_recv()
            for j, chip in enumerate(chips):
                copy(a, 4 + j, (*chip, 1 - c), me).wait_recv()
        for cp in first + passed:
            cp.wait_send()
        for cp in mine:
            cp.wait()

    any_spec = pl.BlockSpec(memory_space=pl.ANY)
    return pl.pallas_call(
        body, name=name, out_shape=[jax.ShapeDtypeStruct((NDEV,) + a.shape, a.dtype) for a in arrays],
        in_specs=[any_spec] * n, out_specs=[any_spec] * n,
        scratch_shapes=[pltpu.SemaphoreType.DMA((n, NDEV - 1)), pltpu.SemaphoreType.DMA((n, NDEV - 1)),
                        pltpu.SemaphoreType.DMA((n,))])(*arrays)


def _hosted_call(body, comm, name, grid, in_specs, out_specs, out_shape, scratch, dims, operands):
    if comm is None:
        return pl.pallas_call(body, name=name, grid=grid, in_specs=in_specs, out_specs=out_specs, out_shape=out_shape,
                              scratch_shapes=scratch, compiler_params=_cparams(dims, VMEM_LIMIT))(*operands)
    n_in, n_out, n_scr, n = len(in_specs), len(out_specs), len(scratch), comm.n

    def hosted(*refs):
        hin, cin = refs[:n_in], refs[n_in:n_in + n]
        hout = refs[n_in + n:n_in + n + n_out]
        cout = refs[n_in + n + n_out:n_in + 2 * n + n_out]
        hscr = refs[n_in + 2 * n + n_out:n_in + 2 * n + n_out + n_scr]
        sems = refs[n_in + 2 * n + n_out + n_scr:]
        ids = [pl.program_id(a) for a in range(len(grid))]
        first = functools.reduce(jnp.logical_and, [i == 0 for i in ids])
        last = functools.reduce(jnp.logical_and, [i == g - 1 for i, g in zip(ids, grid)])

        @pl.when(first)
        def _():
            for cp in comm.copies(cin, cout, sems):
                cp.start()

        body(*hin, *hout, *hscr)

        @pl.when(last)
        def _():
            for cp in comm.copies(cin, cout, sems):
                cp.wait()

    return pl.pallas_call(
        hosted, name=name, grid=grid, in_specs=list(in_specs) + comm.in_specs,
        out_specs=list(out_specs) + comm.out_specs, out_shape=list(out_shape) + comm.out_shape,
        scratch_shapes=list(scratch) + comm.scratch,
        compiler_params=_cparams(("arbitrary",) * len(grid), VMEM_LIMIT))(*operands, *comm.arrays)


def _all_reduce_small(v):
    rows = v.shape[0]

    def body(v_ref, sum_ref, all_ref, send_sems, recv_sems):
        me, peers = _me_and_peers()
        all_ref[me] = v_ref[...]
        copies = []
        for k, (peer, _) in enumerate(peers):
            cp = pltpu.make_async_remote_copy(
                src_ref=v_ref, dst_ref=all_ref.at[me],
                send_sem=send_sems.at[k], recv_sem=recv_sems.at[k],
                device_id=peer, device_id_type=MESH)
            cp.start()
            copies.append(cp)
        for cp in copies:
            cp.wait()
        acc = all_ref[0]
        for d in range(1, NDEV):
            acc = acc + all_ref[d]
        sum_ref[...] = acc

    vm = pl.BlockSpec(memory_space=pltpu.VMEM)
    return pl.pallas_call(
        body, name="all_reduce_small",
        out_shape=jax.ShapeDtypeStruct((rows, LANES), F32),
        in_specs=[vm], out_specs=vm,
        scratch_shapes=[pltpu.VMEM((NDEV, rows, LANES), F32),
                        pltpu.SemaphoreType.DMA((NDEV - 1,)), pltpu.SemaphoreType.DMA((NDEV - 1,))],
    )(v)


def _adamw_math(w, g, m, v):
    m = ADAM_B1 * m + (1.0 - ADAM_B1) * g
    v = ADAM_B2 * v + (1.0 - ADAM_B2) * jnp.square(g)
    m_hat = m / (1.0 - ADAM_B1 ** ADAM_STEP)
    v_hat = v / (1.0 - ADAM_B2 ** ADAM_STEP)
    delta = -ADAM_LR * (m_hat / (jnp.sqrt(v_hat) + ADAM_EPS) + ADAM_WD * w)
    return delta, m, v


def _sum_adamw(parts, w, m, v, name):
    depth, rows, cols = w.shape
    tr = next(c for c in (256, 128, 64, 32, 16) if rows % c == 0)
    nb = rows // tr

    def body(*refs):
        p_refs, (w_ref, m_ref, v_ref, g_ref, d_ref, nm_ref, nv_ref) = refs[:depth], refs[depth:]
        l = pl.program_id(0)
        for ll in range(depth):
            @pl.when(l == ll)
            def _(ll=ll):
                g = p_refs[ll][0].astype(F32)
                for d in range(1, NDEV):
                    g = g + p_refs[ll][d].astype(F32)
                delta, nm, nv = _adamw_math(w_ref[0], g, m_ref[0], v_ref[0])
                g_ref[0] = g
                d_ref[0] = delta
                nm_ref[0] = nm
                nv_ref[0] = nv

    part = lambda ll: pl.BlockSpec((NDEV, tr, cols), lambda l, i, ll=ll: (0, jnp.where(l == ll, i, jnp.where(l < ll, 0, nb - 1)), 0))
    blk = pl.BlockSpec((1, tr, cols), lambda l, i: (l, i, 0))
    sds = jax.ShapeDtypeStruct((depth, rows, cols), F32)
    return pl.pallas_call(
        body, name=name, grid=(depth, nb),
        in_specs=[part(ll) for ll in range(depth)] + [blk, blk, blk],
        out_specs=[blk, blk, blk, blk], out_shape=[sds, sds, sds, sds],
        compiler_params=_cparams(("arbitrary", "arbitrary"), VMEM_LIMIT),
    )(*parts, w, m, v)


def _adamw_small(g, w, m, v):
    def body(g_ref, w_ref, m_ref, v_ref, d_ref, nm_ref, nv_ref):
        delta, nm, nv = _adamw_math(w_ref[...], g_ref[...], m_ref[...], v_ref[...])
        d_ref[...] = delta
        nm_ref[...] = nm
        nv_ref[...] = nv

    sds = jax.ShapeDtypeStruct(g.shape, F32)
    return pl.pallas_call(body, name="adamw_small", out_shape=[sds, sds, sds])(g, w, m, v)


def _matmul(a, b, out_dtype, name, tm=1024, tn=1024, tk=1024):
    M, K = a.shape
    N = b.shape[1]
    tm, tn, tk = min(tm, M), min(tn, N), min(tk, K)
    nk = K // tk

    def body(a_ref, b_ref, o_ref, acc):
        k = pl.program_id(2)

        @pl.when(k == 0)
        def _():
            acc[...] = jnp.zeros_like(acc)

        acc[...] += _dot(a_ref[...], b_ref[...])

        @pl.when(k == nk - 1)
        def _():
            o_ref[...] = acc[...].astype(out_dtype)

    return pl.pallas_call(
        body, name=name, grid=(M // tm, N // tn, nk),
        in_specs=[pl.BlockSpec((tm, tk), lambda i, j, k: (i, k)), pl.BlockSpec((tk, tn), lambda i, j, k: (k, j))],
        out_specs=pl.BlockSpec((tm, tn), lambda i, j, k: (i, j)),
        out_shape=jax.ShapeDtypeStruct((M, N), out_dtype),
        scratch_shapes=[pltpu.VMEM((tm, tn), F32)],
        compiler_params=_cparams(("parallel", "parallel", "arbitrary"), VMEM_LIMIT),
    )(a, b)


def _matmul_batched(a, b, out_dtype, name, tm=1024, tn=1024, tk=512):
    G, M, K = a.shape
    N = b.shape[2]
    tm, tn, tk = min(tm, M), min(tn, N), min(tk, K)
    nk = K // tk

    def body(a_ref, b_ref, o_ref, acc):
        k = pl.program_id(3)

        @pl.when(k == 0)
        def _():
            acc[...] = jnp.zeros_like(acc)

        acc[...] += _dot(a_ref[0], b_ref[0])

        @pl.when(k == nk - 1)
        def _():
            o_ref[0] = acc[...].astype(out_dtype)

    return pl.pallas_call(
        body, name=name, grid=(G, M // tm, N // tn, nk),
        in_specs=[pl.BlockSpec((1, tm, tk), lambda g, i, j, k: (g, i, k)),
                  pl.BlockSpec((1, tk, tn), lambda g, i, j, k: (g, k, j))],
        out_specs=pl.BlockSpec((1, tm, tn), lambda g, i, j, k: (g, i, j)),
        out_shape=jax.ShapeDtypeStruct((G, M, N), out_dtype),
        scratch_shapes=[pltpu.VMEM((tm, tn), F32)],
        compiler_params=_cparams(("parallel", "parallel", "parallel", "arbitrary"), VMEM_LIMIT),
    )(a, b)


def _inproj_fwd(x2, nw, wmain, wsmall, cos128, sin128, S, li, comm=None):
    T = x2.shape[0]
    tm, tn = min(2048, S), 512
    nj, npos = NMAIN // tn, S // tm
    jq0, jk = OFF_BQ // tn, OFF_BK // tn

    def body(x_ref, nw_ref, w_ref, ws_ref, cos_ref, sin_ref, proj_ref, ps_ref, ht_ref, h_scr):
        j = pl.program_id(1)

        @pl.when(j == 0)
        def _():
            x = x_ref[...]
            r = lax.rsqrt(jnp.mean(x * x, axis=-1, keepdims=True) + EPS)
            h = (x * r * nw_ref[...]).astype(BF16)
            h_scr[...] = h
            ht_ref[...] = h.T
            ps_ref[...] = _dot(h, ws_ref[...])

        acc = _dot(h_scr[...], w_ref[...])

        def roped(c):
            xc = acc[:, LANES * c:LANES * (c + 1)]
            return (xc * cos_ref[...] + _rot_half(xc) * sin_ref[...]).astype(BF16)

        def plain(c):
            return acc[:, LANES * c:LANES * (c + 1)].astype(BF16)

        is_q = jnp.logical_or(j == jq0, j == jq0 + 1)
        is_k = j == jk

        @pl.when(is_q)
        def _():
            for c in range(4):
                proj_ref[:, LANES * c:LANES * (c + 1)] = roped(c)

        @pl.when(is_k)
        def _():
            for c in range(4):
                proj_ref[:, LANES * c:LANES * (c + 1)] = roped(c) if c < 2 else plain(c)

        @pl.when(jnp.logical_not(jnp.logical_or(is_q, is_k)))
        def _():
            proj_ref[...] = acc.astype(BF16)

    return _hosted_call(
        body, comm, f"inproj_fwd_{li}", (T // tm, nj),
        in_specs=[pl.BlockSpec((tm, D), lambda i, j: (i, 0)),
                  pl.BlockSpec((1, D), lambda i, j: (0, 0)),
                  pl.BlockSpec((D, tn), lambda i, j: (0, j)),
                  pl.BlockSpec((D, LANES), lambda i, j: (0, 0)),
                  pl.BlockSpec((tm, LANES), lambda i, j: (i % npos, 0)),
                  pl.BlockSpec((tm, LANES), lambda i, j: (i % npos, 0))],
        out_specs=[pl.BlockSpec((tm, tn), lambda i, j: (i, j)),
                   pl.BlockSpec((tm, LANES), lambda i, j: (i, 0)),
                   pl.BlockSpec((D, tm), lambda i, j: (0, i))],
        out_shape=[jax.ShapeDtypeStruct((T, NMAIN), BF16), jax.ShapeDtypeStruct((T, LANES), F32),
                   jax.ShapeDtypeStruct((D, T), BF16)],
        scratch=[pltpu.VMEM((tm, D), BF16)], dims=("parallel", "arbitrary"),
        operands=(x2, nw, wmain, wsmall, cos128, sin128))


def _inproj_bwd_dx(segs, wmain, init, final, name, comm=None):
    T = segs[0][0].shape[0]
    tm = min(1024, T)
    tk = 1024 if all(a.shape[1] % 1024 == 0 and c % 1024 == 0 for a, c in segs) else 512
    ni = T // tm
    k0s, nks, c0s = [], [], []
    for arr, col0 in segs:
        k0s.append(sum(nks))
        nks.append(arr.shape[1] // tk)
        c0s.append(col0 // tk)
    nk = sum(nks)
    ns = len(segs)

    def in_range(k, s):
        return jnp.logical_and(k >= k0s[s], k < k0s[s] + nks[s])

    def wcol(i, k):
        g = 0
        for s in range(ns):
            g = g + jnp.where(in_range(k, s), c0s[s] + k - k0s[s], 0)
        return (0, g)

    n_init = 2 if init[0] == "narrow" else 1

    def body(*refs):
        seg_refs, w_ref = refs[:ns], refs[ns]
        init_refs = refs[ns + 1:ns + 1 + n_init]
        rest = refs[ns + 1 + n_init:]
        i, k = pl.program_id(0), pl.program_id(1)
        acc = rest[-1]

        @pl.when(k == 0)
        def _():
            if init[0] == "narrow":
                acc[...] = _dot_nt(init_refs[0][...], init_refs[1][...])
            else:
                acc[...] = init_refs[0][...]

        for s in range(ns):
            @pl.when(in_range(k, s))
            def _(s=s):
                acc[...] += _dot_nt(seg_refs[s][...], w_ref[...])

        if final is None:
            @pl.when(k == nk - 1)
            def _():
                rest[0][...] = acc[...]
        else:
            x_ref, nw_ref, dxo_ref, dx_ref, dx16_ref, dnw_ref = rest[:6]

            @pl.when(jnp.logical_and(i == 0, k == 0))
            def _():
                dnw_ref[...] = jnp.zeros_like(dnw_ref)

            @pl.when(k == nk - 1)
            def _():
                x = x_ref[...]
                r = lax.rsqrt(jnp.mean(x * x, axis=-1, keepdims=True) + EPS)
                dh = acc[...]
                g = dh * nw_ref[...]
                dx = dxo_ref[...] + r * g - x * (r * r * r) * jnp.mean(g * x, axis=-1, keepdims=True)
                dx_ref[...] = dx
                dx16_ref[...] = dx.astype(BF16)
                dnw_ref[0:1, :] += jnp.sum(dh * x * r, axis=0, keepdims=True)

    row = pl.BlockSpec((tm, D), lambda i, k: (i, 0))
    in_specs = [pl.BlockSpec((tm, tk), lambda i, k, s=s: (i, jnp.clip(k - k0s[s], 0, nks[s] - 1))) for s in range(ns)]
    in_specs.append(pl.BlockSpec((D, tk), wcol))
    operands = [a for a, _ in segs] + [wmain]
    if init[0] == "narrow":
        in_specs += [pl.BlockSpec((tm, LANES), lambda i, k: (i, 0)), pl.BlockSpec((D, LANES), lambda i, k: (0, 0))]
    else:
        in_specs.append(row)
    operands += list(init[1:])
    if final is None:
        out_specs, out_shape = [row], [jax.ShapeDtypeStruct((T, D), F32)]
    else:
        in_specs += [row, pl.BlockSpec((1, D), lambda i, k: (0, 0)), row]
        operands += list(final)
        out_specs = [row, row, pl.BlockSpec((8, D), lambda i, k: (0, 0))]
        out_shape = [jax.ShapeDtypeStruct((T, D), F32), jax.ShapeDtypeStruct((T, D), BF16),
                     jax.ShapeDtypeStruct((8, D), F32)]
    return _hosted_call(body, comm, name, (ni, nk), in_specs=in_specs, out_specs=out_specs, out_shape=out_shape,
                        scratch=[pltpu.VMEM((tm, D), F32)], dims=("arbitrary", "arbitrary"), operands=tuple(operands))


def _merge_fwd(ya, yb, yc, proj, gbias, wp, wout, x2, li):
    T = x2.shape[0]
    tm = min(512, T)
    gcol = OFF_G // D

    def body(ya_ref, yb_ref, yc_ref, g0_ref, g1_ref, g2_ref, gb_ref, wp_ref, wo_ref, x_ref, xn_ref, br_ref, yt_ref):
        merged = jnp.zeros((tm, D), F32)
        for i, (y_ref, g_ref) in enumerate(((ya_ref, g0_ref), (yb_ref, g1_ref), (yc_ref, g2_ref))):
            y = y_ref[...]
            yt_ref[i] = y.T
            br = _dot(y, wp_ref[i])
            br_ref[i] = br.astype(BF16)
            gate = _sigmoid(g_ref[...].astype(F32) + gb_ref[i:i + 1, :])
            merged = merged + gate * br
        xn_ref[...] = x_ref[...] + _dot(merged.astype(BF16), wo_ref[...])

    row = lambda c: pl.BlockSpec((tm, D), lambda i, c=c: (i, c))
    return pl.pallas_call(
        body, name=f"merge_fwd_{li}", grid=(T // tm,),
        in_specs=[row(0), row(0), row(0), row(gcol), row(gcol + 1), row(gcol + 2),
                  pl.BlockSpec((3, D), lambda i: (0, 0)),
                  pl.BlockSpec((3, D, D), lambda i: (0, 0, 0)),
                  pl.BlockSpec((D, D), lambda i: (0, 0)),
                  row(0)],
        out_specs=[row(0), pl.BlockSpec((3, tm, D), lambda i: (0, i, 0)), pl.BlockSpec((3, D, tm), lambda i: (0, 0, i))],
        out_shape=[jax.ShapeDtypeStruct((T, D), F32), jax.ShapeDtypeStruct((3, T, D), BF16),
                   jax.ShapeDtypeStruct((3, D, T), BF16)],
        compiler_params=_cparams(("parallel",), VMEM_LIMIT),
    )(ya, yb, yc, proj, proj, proj, gbias, wp, wout, x2)


def _merge_bwd(dxo16, wout, wp, br, proj, gbias, ob, oc, li):
    T = dxo16.shape[0]
    tm = min(256, T)
    gcol = OFF_G // D

    def body(dx_ref, wo_ref, wp_ref, br_ref, g0_ref, g1_ref, g2_ref, gb_ref, ob_ref, oc_ref, zb_ref, zc_ref,
             dbr_ref, dg_ref, mt_ref, dgb_ref, dya_ref, dob_ref, dzb_ref, doc_ref, dzc_ref):
        @pl.when(pl.program_id(0) == 0)
        def _():
            dgb_ref[...] = jnp.zeros_like(dgb_ref)

        dm = _dot_nt(dx_ref[...], wo_ref[...])
        merged = jnp.zeros((tm, D), F32)
        dys = []
        for i, g_ref in enumerate((g0_ref, g1_ref, g2_ref)):
            b = br_ref[i].astype(F32)
            gate = _sigmoid(g_ref[...].astype(F32) + gb_ref[i:i + 1, :])
            merged = merged + gate * b
            dbr = (dm * gate).astype(BF16)
            dbr_ref[i] = dbr
            dgate = dm * b * gate * (1.0 - gate)
            dg_ref[:, D * i:D * (i + 1)] = dgate.astype(BF16)
            dgb_ref[i:i + 1, :] += jnp.sum(dgate, axis=0, keepdims=True)
            dys.append(_dot_nt(dbr, wp_ref[i]))
        mt_ref[...] = merged.astype(BF16).T
        dya_ref[...] = dys[0].astype(BF16)
        for dy, o_ref, z_ref, do_ref, dz_ref in ((dys[1], ob_ref, zb_ref, dob_ref, dzb_ref),
                                                 (dys[2], oc_ref, zc_ref, doc_ref, dzc_ref)):
            z = z_ref[...].astype(F32)
            sg = _sigmoid(z)
            do_ref[...] = (dy * z * sg).astype(BF16)
            dz_ref[...] = (dy * o_ref[...].astype(F32) * sg * (1.0 + z * (1.0 - sg))).astype(BF16)

    row = lambda c: pl.BlockSpec((tm, D), lambda i, c=c: (i, c))
    sds = jax.ShapeDtypeStruct((T, D), BF16)
    return pl.pallas_call(
        body, name=f"merge_bwd_{li}", grid=(T // tm,),
        in_specs=[row(0), pl.BlockSpec((D, D), lambda i: (0, 0)), pl.BlockSpec((3, D, D), lambda i: (0, 0, 0)),
                  pl.BlockSpec((3, tm, D), lambda i: (0, i, 0)),
                  row(gcol), row(gcol + 1), row(gcol + 2),
                  pl.BlockSpec((3, D), lambda i: (0, 0)),
                  row(0), row(0), row(OFF_BZ // D), row(OFF_CZ // D)],
        out_specs=[pl.BlockSpec((3, tm, D), lambda i: (0, i, 0)),
                   pl.BlockSpec((tm, 3 * D), lambda i: (i, 0)),
                   pl.BlockSpec((D, tm), lambda i: (0, i)),
                   pl.BlockSpec((8, D), lambda i: (0, 0)),
                   row(0), row(0), row(0), row(0), row(0)],
        out_shape=[jax.ShapeDtypeStruct((3, T, D), BF16), jax.ShapeDtypeStruct((T, 3 * D), BF16),
                   jax.ShapeDtypeStruct((D, T), BF16), jax.ShapeDtypeStruct((8, D), F32), sds, sds, sds, sds, sds],
        compiler_params=_cparams(("arbitrary",), VMEM_LIMIT),
    )(dxo16, wout, wp, br, proj, proj, proj, gbias, ob, oc, proj, proj)


def _final_loss(x2, tgt, fw):
    T = x2.shape[0]
    tm = min(512, T)
    ni = T // tm

    def body(x_ref, t_ref, w_ref, dx_ref, dx16_ref, st_ref):
        i = pl.program_id(0)

        @pl.when(i == 0)
        def _():
            st_ref[...] = jnp.zeros_like(st_ref)

        x = x_ref[...]
        r = lax.rsqrt(jnp.mean(x * x, axis=-1, keepdims=True) + EPS)
        xh = x * r
        err = xh * w_ref[...] - t_ref[...]
        dy = err * (1.0 / D)
        g = dy * w_ref[...]
        dx = r * g - x * (r * r * r) * jnp.mean(g * x, axis=-1, keepdims=True)
        dx_ref[...] = dx
        dx16_ref[...] = dx.astype(BF16)
        st_ref[0:1, :] += jnp.sum(dy * xh, axis=0, keepdims=True)
        st_ref[1:2, :] += jnp.sum(err * err, axis=0, keepdims=True)

        @pl.when(i == ni - 1)
        def _():
            tot = jnp.sum(st_ref[1:2, :], axis=1, keepdims=True) * (0.5 / D)
            st_ref[2:3, :] = jnp.broadcast_to(tot, (1, D))

    row = pl.BlockSpec((tm, D), lambda i: (i, 0))
    return pl.pallas_call(
        body, name="final_loss", grid=(ni,),
        in_specs=[row, row, pl.BlockSpec((1, D), lambda i: (0, 0))],
        out_specs=[row, row, pl.BlockSpec((8, D), lambda i: (0, 0))],
        out_shape=[jax.ShapeDtypeStruct((T, D), F32), jax.ShapeDtypeStruct((T, D), BF16),
                   jax.ShapeDtypeStruct((8, D), F32)],
        compiler_params=_cparams(("arbitrary",), VMEM_LIMIT),
    )(x2, tgt, fw)


def _fox_cum(ps, fb_row, S, li):
    T = ps.shape[0]
    blk = min(4 * LCH, S)
    nb, nsub = S // blk, blk // LCH

    def body(ps_ref, fb_ref, cum_ref, carry):
        @pl.when(pl.program_id(1) == 0)
        def _():
            carry[...] = jnp.zeros_like(carry)

        r = lax.broadcasted_iota(jnp.int32, (LCH, LCH), 0)
        c = lax.broadcasted_iota(jnp.int32, (LCH, LCH), 1)
        tri = (r >= c).astype(F32)
        run = carry[0:1, :]
        for u in range(nsub):
            rows = slice(LCH * u, LCH * (u + 1))
            logf = -_softplus(-(ps_ref[rows, :] + fb_ref[...]))
            cum = _dot_hi(tri, logf) + run
            cum_ref[rows, :] = cum
            run = cum[LCH - 1:LCH, :]
        carry[0:1, :] = run

    return pl.pallas_call(
        body, name=f"fox_cum_{li}", grid=(T // S, nb),
        in_specs=[pl.BlockSpec((blk, LANES), lambda b, i: (b * nb + i, 0)),
                  pl.BlockSpec((1, LANES), lambda b, i: (0, 0))],
        out_specs=pl.BlockSpec((blk, LANES), lambda b, i: (b * nb + i, 0)),
        out_shape=jax.ShapeDtypeStruct((T, LANES), F32),
        scratch_shapes=[pltpu.VMEM((8, LANES), F32)],
        compiler_params=_cparams(("arbitrary", "arbitrary")),
    )(ps, fb_row)


def _fox_cum_bwd(dcum, ps, fb_row, S, li):
    T = ps.shape[0]
    rows_blk = min(4 * LCH, S)
    nb, nsub = S // rows_blk, rows_blk // LCH

    def body(dc_ref, ps_ref, fb_ref, df_ref, dfb_ref, carry):
        b, i = pl.program_id(0), pl.program_id(1)

        @pl.when(i == 0)
        def _():
            carry[...] = jnp.zeros_like(carry)

        @pl.when(jnp.logical_and(b == 0, i == 0))
        def _():
            dfb_ref[...] = jnp.zeros_like(dfb_ref)

        r = lax.broadcasted_iota(jnp.int32, (LCH, LCH), 0)
        c = lax.broadcasted_iota(jnp.int32, (LCH, LCH), 1)
        tri = (c >= r).astype(F32)
        lane = _lane_iota()
        live = jnp.logical_and(lane >= NH, lane < 2 * NH)
        run = carry[0:1, :]
        dfb = jnp.zeros((1, LANES), F32)
        for u in reversed(range(nsub)):
            rows = slice(LCH * u, LCH * (u + 1))
            dc = dc_ref[rows, :]
            dlogf = _dot_hi(tri, dc) + run
            run = run + jnp.sum(dc, axis=0, keepdims=True)
            df = jnp.where(live, dlogf * _sigmoid(-(ps_ref[rows, :] + fb_ref[...])), 0.0)
            df_ref[rows, :] = df
            dfb = dfb + jnp.sum(df, axis=0, keepdims=True)
        carry[0:1, :] = run
        dfb_ref[0:1, :] += dfb

    blk = pl.BlockSpec((rows_blk, LANES), lambda b, i: (b * nb + nb - 1 - i, 0))
    return pl.pallas_call(
        body, name=f"fox_cum_bwd_{li}", grid=(T // S, nb),
        in_specs=[blk, blk, pl.BlockSpec((1, LANES), lambda b, i: (0, 0))],
        out_specs=[blk, pl.BlockSpec((8, LANES), lambda b, i: (0, 0))],
        out_shape=[jax.ShapeDtypeStruct((T, LANES), F32), jax.ShapeDtypeStruct((8, LANES), F32)],
        scratch_shapes=[pltpu.VMEM((8, LANES), F32)],
        compiler_params=_cparams(("arbitrary", "arbitrary")),
    )(dcum, ps, fb_row)


def _fox_blocks(S):
    bq = min(512, S)
    return bq, S // bq


def _split3(c):
    hi = c.astype(BF16).astype(F32)
    r = c - hi
    mid = r.astype(BF16).astype(F32)
    return hi, mid, (r - mid).astype(BF16).astype(F32)


def _augment(x, parts, key_side, hh):
    lane = _lane_iota()
    b0 = HD if hh == 0 else 0
    p0, o0 = (b0 + 3, b0) if key_side else (b0, b0 + 3)
    out = jnp.where(jnp.logical_and(lane >= o0, lane < o0 + 3), 1.0, x)
    for t in range(3):
        out = jnp.where(lane == p0 + t, parts[t], out)
    return out.astype(BF16)


def _row_to_col(row_ref_slice, col_scr, hh, S):
    step = min(4 * LANES, S)
    for t in range(S // step):
        r = row_ref_slice[:, step * t:step * (t + 1)]
        col_scr[hh, step * t:step * (t + 1), :] = jnp.broadcast_to(r, (LANES, step)).T[:, 0:1]


def _col_to_row(col):
    return jnp.broadcast_to(col, (col.shape[0], LANES)).T[0:1, :]


def _fox_fwd(proj, cum_row, S, li, comm=None):
    T = proj.shape[0]
    B = T // S
    bq, nq = _fox_blocks(S)
    qc, kc, vc, zc = OFF_CQ // LANES, OFF_CK // LANES, OFF_CV // LANES, OFF_CZ // LANES

    def body(q_ref, k_ref, v_ref, z_ref, cr_ref, y_ref, o_ref, lse_ref, kaug, cc_ref):
        i = pl.program_id(2)
        m0 = _lane_iota() < HD

        @pl.when(i == 0)
        def _():
            kf = k_ref[...].astype(F32)
            for hh in range(2):
                _row_to_col(cr_ref[0, hh], cc_ref, hh, S)
                kaug[hh] = _augment(kf, _split3(-cc_ref[hh]), True, hh)

        q2 = q_ref[...].astype(F32) * SCALE
        rows_q = pl.ds(pl.multiple_of(i * bq, bq), bq)
        row = lax.broadcasted_iota(jnp.int32, (bq, bq), 0)
        col = lax.broadcasted_iota(jnp.int32, (bq, bq), 1)
        qa = [_augment(jnp.where(m0 if hh == 0 else jnp.logical_not(m0), q2, 0.0),
                       _split3(cc_ref[hh, rows_q, :]), False, hh) for hh in range(2)]

        def step(j, carry, masked):
            start = pl.multiple_of(j * bq, bq)
            v2 = v_ref[pl.ds(start, bq), :]
            out = []
            for hh in range(2):
                m, l, acc = carry[3 * hh:3 * hh + 3]
                s = _dot_nt(qa[hh], kaug[hh, pl.ds(start, bq), :])
                if masked:
                    s = jnp.where(row >= col, s, NEG)
                mn = jnp.maximum(m, jnp.max(s, axis=1, keepdims=True))
                alpha = jnp.exp(m - mn)
                p = jnp.exp(s - mn)
                out += [mn, alpha * l + jnp.sum(p, axis=1, keepdims=True), alpha * acc + _dot(p.astype(BF16), v2)]
            return tuple(out)

        init = (jnp.full((bq, 1), NEG, F32), jnp.zeros((bq, 1), F32), jnp.zeros((bq, LANES), F32)) * 2
        carry = step(i, lax.fori_loop(0, i, functools.partial(step, masked=False), init), True)
        outs = []
        for hh in range(2):
            m, l, acc = carry[3 * hh:3 * hh + 3]
            outs.append(acc / l)
            lse_ref[0, hh] = m + jnp.log(l)
        o2 = jnp.where(m0, outs[0], outs[1])
        z = z_ref[...].astype(F32)
        o_ref[...] = o2.astype(BF16)
        y_ref[...] = (o2 * z * _sigmoid(z)).astype(BF16)

    qblk = lambda c: pl.BlockSpec((bq, LANES), lambda b, p, i, c=c: (b * nq + i, c + p))
    sblk = lambda c: pl.BlockSpec((S, LANES), lambda b, p, i, c=c: (b, c + p))
    return _hosted_call(
        body, comm, f"fox_fwd_{li}", (B, NH // 2, nq),
        in_specs=[qblk(qc), sblk(kc), sblk(vc), qblk(zc),
                  pl.BlockSpec((1, 2, 1, S), lambda b, p, i: (b, p, 0, 0))],
        out_specs=[qblk(0), qblk(0), pl.BlockSpec((1, 2, bq, 1), lambda b, p, i: (b, p, i, 0))],
        out_shape=[jax.ShapeDtypeStruct((T, D), BF16), jax.ShapeDtypeStruct((T, D), BF16),
                   jax.ShapeDtypeStruct((B, NH, S, 1), F32)],
        scratch=[pltpu.VMEM((2, S, LANES), BF16), pltpu.VMEM((2, S, 1), F32)],
        dims=("parallel", "parallel", "arbitrary"), operands=(proj, proj, proj, proj, cum_row))


def _fox_bwd(proj, do, o, cum_row, lse, S, li, comm=None):
    T = proj.shape[0]
    B = T // S
    bq, nq = _fox_blocks(S)
    qc, kc, vc = OFF_CQ // LANES, OFF_CK // LANES, OFF_CV // LANES

    def body(q_ref, k_ref, v_ref, do_ref, o_ref, cr_ref, lse_ref, dq_ref, dk_ref, dv_ref, dc_ref, dr_ref,
             dq_scr, dr_scr, qaug, cc_ref):
        j = pl.program_id(2)
        m0 = _lane_iota() < HD

        @pl.when(j == 0)
        def _():
            dq_scr[...] = jnp.zeros_like(dq_scr)
            dr_scr[...] = jnp.zeros_like(dr_scr)
            qf = q_ref[...].astype(F32) * SCALE
            for hh in range(2):
                sel = m0 if hh == 0 else jnp.logical_not(m0)
                _row_to_col(cr_ref[0, hh], cc_ref, hh, S)
                qaug[hh] = _augment(jnp.where(sel, qf, 0.0), _split3(cc_ref[hh] - lse_ref[0, hh]), False, hh)

        k2 = k_ref[...]
        v2 = v_ref[...]
        zk = jnp.zeros_like(k2)
        kh = (jnp.where(m0, k2, zk), jnp.where(m0, zk, k2))
        kf = k2.astype(F32)
        rows_k = pl.ds(pl.multiple_of(j * bq, bq), bq)
        ka = [_augment(kf, _split3(-cc_ref[hh, rows_k, :]), True, hh) for hh in range(2)]
        row = lax.broadcasted_iota(jnp.int32, (bq, bq), 0)
        col = lax.broadcasted_iota(jnp.int32, (bq, bq), 1)

        def step(i, carry, masked):
            dk, dv, dc0, dc1 = carry
            dcs = [dc0, dc1]
            start = pl.multiple_of(i * bq, bq)
            q2 = q_ref[pl.ds(start, bq), :]
            do2 = do_ref[pl.ds(start, bq), :]
            prod = do2.astype(F32) * o_ref[pl.ds(start, bq), :].astype(F32)
            zq = jnp.zeros_like(q2)
            dq = jnp.zeros((bq, LANES), F32)
            for hh in range(2):
                sel = m0 if hh == 0 else jnp.logical_not(m0)
                qh = jnp.where(sel, q2, zq)
                doh = jnp.where(sel, do2, zq)
                delta = _head_sum(prod, hh)
                s = _dot_nt(qaug[hh, pl.ds(start, bq), :], ka[hh])
                if masked:
                    s = jnp.where(row >= col, s, NEG)
                p = jnp.exp(s)
                dp = _dot_nt(doh, v2)
                ds = p * (dp - delta)
                dcs[hh] = dcs[hh] - jnp.sum(ds, axis=0, keepdims=True)
                dr_scr[hh, pl.ds(start, bq), :] += jnp.sum(ds, axis=1, keepdims=True)
                dsb = ds.astype(BF16)
                dv = dv + _dot_tn(p.astype(BF16), doh)
                dk = dk + _dot_tn(dsb, qh)
                dq = dq + _dot(dsb, kh[hh])
            dq_scr[pl.ds(start, bq), :] += dq
            return dk, dv, dcs[0], dcs[1]

        zero = jnp.zeros((bq, LANES), F32)
        zrow = jnp.zeros((1, bq), F32)
        carry = step(j, (zero, zero, zrow, zrow), True)
        dk, dv, dc0, dc1 = lax.fori_loop(j + 1, nq, functools.partial(step, masked=False), carry)
        dk_ref[...] = (dk * SCALE).astype(BF16)
        dv_ref[...] = dv.astype(BF16)
        dc_ref[0, 0, 0] = dc0
        dc_ref[0, 1, 0] = dc1

        @pl.when(j == nq - 1)
        def _():
            dq_ref[...] = (dq_scr[...] * SCALE).astype(BF16)
            step_r = min(4 * LANES, S)
            for hh in range(2):
                for t in range(S // step_r):
                    dr_ref[0, hh, :, step_r * t:step_r * (t + 1)] = _col_to_row(dr_scr[hh, step_r * t:step_r * (t + 1), :])

    sblk = lambda c: pl.BlockSpec((S, LANES), lambda b, p, j, c=c: (b, c + p))
    kblk = lambda c: pl.BlockSpec((bq, LANES), lambda b, p, j, c=c: (b * nq + j, c + p))
    col_spec = pl.BlockSpec((1, 2, S, 1), lambda b, p, j: (b, p, 0, 0))
    row_spec = pl.BlockSpec((1, 2, 1, S), lambda b, p, j: (b, p, 0, 0))
    return _hosted_call(
        body, comm, f"fox_bwd_{li}", (B, NH // 2, nq),
        in_specs=[sblk(qc), kblk(kc), kblk(vc), sblk(0), sblk(0), row_spec, col_spec],
        out_specs=[sblk(0), kblk(0), kblk(0), pl.BlockSpec((1, 2, 1, 1, bq), lambda b, p, j: (b, p, j, 0, 0)),
                   row_spec],
        out_shape=[jax.ShapeDtypeStruct((T, D), BF16), jax.ShapeDtypeStruct((T, D), BF16),
                   jax.ShapeDtypeStruct((T, D), BF16), jax.ShapeDtypeStruct((B, NH, nq, 1, bq), F32),
                   jax.ShapeDtypeStruct((B, NH, 1, S), F32)],
        scratch=[pltpu.VMEM((S, LANES), F32), pltpu.VMEM((2, S, 1), F32), pltpu.VMEM((2, S, LANES), BF16),
                 pltpu.VMEM((2, S, 1), F32)],
        dims=("parallel", "parallel", "arbitrary"), operands=(proj, proj, proj, do, o, cum_row, lse))


def _swa_blocks(S):
    bq = min(512, S)
    return bq, S // bq, bq // LCH


def _dup_head(xw, kvl):
    m0 = _lane_iota() < HD
    a = jnp.where(m0 if kvl == 0 else jnp.logical_not(m0), xw, 0.0)
    return (a + pltpu.roll(a, HD, 1)).astype(BF16)


def _band(same_block):
    r = lax.broadcasted_iota(jnp.int32, (LCH, LCH), 0)
    c = lax.broadcasted_iota(jnp.int32, (LCH, LCH), 1)
    return (c <= r) if same_block else (c > r)


def _stack_heads(ref, rows, kvl):
    m0 = _lane_iota() < HD
    parts = []
    for ch in (2 * kvl, 2 * kvl + 1):
        x = ref[rows, LANES * ch:LANES * (ch + 1)]
        parts += [jnp.where(m0, x, jnp.zeros_like(x)), jnp.where(m0, jnp.zeros_like(x), x)]
    return jnp.concatenate(parts, axis=0)


def _stack_delta(do_ref, o_ref, rows, kvl, scale=None):
    parts = []
    for ch in (2 * kvl, 2 * kvl + 1):
        lanes = slice(LANES * ch, LANES * (ch + 1))
        prod = do_ref[rows, lanes].astype(F32) * o_ref[rows, lanes].astype(F32)
        parts += [_head_sum(prod, 0), _head_sum(prod, 1)]
    out = jnp.concatenate(parts, axis=0)
    return out if scale is None else out * scale


def _stack_cols(ref, rows, kvl):
    return jnp.concatenate([ref[0, 4 * kvl + t, rows, :] for t in range(4)], axis=0)


def _swa_fwd(proj, sinks, S, li):
    T = proj.shape[0]
    B = T // S
    bq, nq, nsub = _swa_blocks(S)
    nrow = S // LCH
    qc, zc, kc, vc = OFF_BQ // 512, OFF_BZ // 512, OFF_BK // LANES, OFF_BV // LANES

    def body(sk_ref, q_ref, z_ref, kp_ref, kc_ref, vp_ref, vc_ref, y_ref, o_ref, lse_ref):
        c, i = pl.program_id(0), pl.program_id(2)
        m0 = _lane_iota() < HD
        kw = jnp.concatenate([kp_ref[...].astype(F32), kc_ref[...].astype(F32)], axis=0)
        vw = jnp.concatenate([vp_ref[...].astype(F32), vc_ref[...].astype(F32)], axis=0)
        kd = (_dup_head(kw, 0), _dup_head(kw, 1))
        vd = (_dup_head(vw, 0), _dup_head(vw, 1))
        valid = jnp.concatenate([_band(False), _band(True)], axis=1)
        col = lax.broadcasted_iota(jnp.int32, (LCH, 2 * LCH), 1)
        valid_first = jnp.logical_and(valid, jnp.logical_or(col >= LCH, i > 0))
        valid4 = jnp.concatenate([valid] * 4, axis=0)
        valid4_first = jnp.concatenate([valid_first] * 4, axis=0)
        for r in range(nsub):
            rows = slice(LCH * r, LCH * (r + 1))
            msk = valid4_first if r == 0 else valid4
            for kvl in range(2):
                kwin = kd[kvl][LCH * r:LCH * (r + 2)]
                vwin = vd[kvl][LCH * r:LCH * (r + 2)]
                qs = _stack_heads(q_ref, rows, kvl)
                sink = jnp.concatenate([jnp.full((LCH, 1), sk_ref[8 * c + 4 * kvl + t], F32) for t in range(4)], axis=0)
                s = jnp.where(msk, _dot_nt(qs, kwin) * SCALE, NEG)
                m = jnp.maximum(jnp.max(s, axis=1, keepdims=True), sink)
                p = jnp.exp(s - m)
                l = jnp.sum(p, axis=1, keepdims=True) + jnp.exp(sink - m)
                os_ = _dot(p.astype(BF16), vwin) / l
                lse = m + jnp.log(l)
                for t in range(4):
                    lse_ref[0, 4 * kvl + t, rows, :] = lse[LCH * t:LCH * (t + 1)]
                for u in range(2):
                    lanes = slice(LANES * (2 * kvl + u), LANES * (2 * kvl + u + 1))
                    o2 = jnp.where(m0, os_[LCH * 2 * u:LCH * (2 * u + 1)], os_[LCH * (2 * u + 1):LCH * (2 * u + 2)])
                    z = z_ref[rows, lanes].astype(F32)
                    o_ref[rows, lanes] = o2.astype(BF16)
                    y_ref[rows, lanes] = (o2 * z * _sigmoid(z)).astype(BF16)

    wide = lambda cc: pl.BlockSpec((bq, 512), lambda c, b, i, cc=cc: (b * nq + i, cc + c))
    cur = lambda cc: pl.BlockSpec((bq, LANES), lambda c, b, i, cc=cc: (b * nq + i, cc + c))
    prev = lambda cc: pl.BlockSpec((LCH, LANES), lambda c, b, i, cc=cc: (b * nrow + jnp.maximum(i * nsub - 1, 0), cc + c))
    return pl.pallas_call(
        body, name=f"swa_fwd_{li}", grid=(2, B, nq),
        in_specs=[pl.BlockSpec(memory_space=pltpu.SMEM), wide(qc), wide(zc), prev(kc), cur(kc), prev(vc), cur(vc)],
        out_specs=[wide(0), wide(0), pl.BlockSpec((1, 8, bq, 1), lambda c, b, i: (b, c, i, 0))],
        out_shape=[jax.ShapeDtypeStruct((T, D), BF16), jax.ShapeDtypeStruct((T, D), BF16),
                   jax.ShapeDtypeStruct((B, NH, S, 1), F32)],
        compiler_params=_cparams(("parallel", "parallel", "parallel"), VMEM_LIMIT),
    )(sinks, proj, proj, proj, proj, proj, proj)


def _swa_bwd_dq(proj, do, o, lse, sinks, cos128, sin128, S, li):
    T = proj.shape[0]
    B = T // S
    bq, nq, nsub = _swa_blocks(S)
    nrow = S // LCH
    qc, kc, vc = OFF_BQ // 512, OFF_BK // LANES, OFF_BV // LANES

    def body(sk_ref, q_ref, do_ref, o_ref, lse_ref, kp_ref, kc_ref, vp_ref, vc_ref, cos_ref, sin_ref, dq_ref, dsk_ref):
        c, b, i = pl.program_id(0), pl.program_id(1), pl.program_id(2)

        @pl.when(jnp.logical_and(b == 0, i == 0))
        def _():
            dsk_ref[...] = jnp.zeros_like(dsk_ref)

        m0 = _lane_iota() < HD
        kw = jnp.concatenate([kp_ref[...].astype(F32), kc_ref[...].astype(F32)], axis=0)
        vw = jnp.concatenate([vp_ref[...].astype(F32), vc_ref[...].astype(F32)], axis=0)
        kd = (_dup_head(kw, 0), _dup_head(kw, 1))
        vd = (_dup_head(vw, 0), _dup_head(vw, 1))
        valid = jnp.concatenate([_band(False), _band(True)], axis=1)
        col = lax.broadcasted_iota(jnp.int32, (LCH, 2 * LCH), 1)
        valid_first = jnp.logical_and(valid, jnp.logical_or(col >= LCH, i > 0))
        dsk = [jnp.zeros((1, 1), F32) for _ in range(8)]
        valid4 = jnp.concatenate([valid] * 4, axis=0)
        valid4_first = jnp.concatenate([valid_first] * 4, axis=0)
        for r in range(nsub):
            rows = slice(LCH * r, LCH * (r + 1))
            msk = valid4_first if r == 0 else valid4
            for kvl in range(2):
                kwin = kd[kvl][LCH * r:LCH * (r + 2)]
                vwin = vd[kvl][LCH * r:LCH * (r + 2)]
                qs = _stack_heads(q_ref, rows, kvl)
                dos = _stack_heads(do_ref, rows, kvl)
                delta = _stack_delta(do_ref, o_ref, rows, kvl)
                lse = _stack_cols(lse_ref, rows, kvl)
                sink = jnp.concatenate([jnp.full((LCH, 1), sk_ref[8 * c + 4 * kvl + t], F32) for t in range(4)], axis=0)
                s = jnp.where(msk, _dot_nt(qs, kwin) * SCALE, NEG)
                p = jnp.exp(s - lse)
                ds = p * (_dot_nt(dos, vwin) - delta)
                dqs = _dot(ds.astype(BF16), kwin) * SCALE
                dsink = jnp.exp(sink - lse) * delta
                for t in range(4):
                    hl = 4 * kvl + t
                    dsk[hl] = dsk[hl] - jnp.sum(dsink[LCH * t:LCH * (t + 1)], axis=0, keepdims=True)
                for u in range(2):
                    lanes = slice(LANES * (2 * kvl + u), LANES * (2 * kvl + u + 1))
                    dq2 = jnp.where(m0, dqs[LCH * 2 * u:LCH * (2 * u + 1)], dqs[LCH * (2 * u + 1):LCH * (2 * u + 2)])
                    dq2 = dq2 * cos_ref[rows, :] - _rot_half(dq2) * sin_ref[rows, :]
                    dq_ref[rows, lanes] = dq2.astype(BF16)
        for hl in range(8):
            dsk_ref[0, hl:hl + 1, :] += jnp.broadcast_to(dsk[hl], (1, LANES))

    wide = lambda cc: pl.BlockSpec((bq, 512), lambda c, b, i, cc=cc: (b * nq + i, cc + c))
    cur = lambda cc: pl.BlockSpec((bq, LANES), lambda c, b, i, cc=cc: (b * nq + i, cc + c))
    prev = lambda cc: pl.BlockSpec((LCH, LANES), lambda c, b, i, cc=cc: (b * nrow + jnp.maximum(i * nsub - 1, 0), cc + c))
    pos = pl.BlockSpec((bq, LANES), lambda c, b, i: (i, 0))
    return pl.pallas_call(
        body, name=f"swa_bwd_dq_{li}", grid=(2, B, nq),
        in_specs=[pl.BlockSpec(memory_space=pltpu.SMEM), wide(qc), wide(0), wide(0),
                  pl.BlockSpec((1, 8, bq, 1), lambda c, b, i: (b, c, i, 0)),
                  prev(kc), cur(kc), prev(vc), cur(vc), pos, pos],
        out_specs=[wide(0), pl.BlockSpec((1, 8, LANES), lambda c, b, i: (c, 0, 0))],
        out_shape=[jax.ShapeDtypeStruct((T, D), BF16), jax.ShapeDtypeStruct((2, 8, LANES), F32)],
        compiler_params=_cparams(("arbitrary", "arbitrary", "arbitrary"), VMEM_LIMIT),
    )(sinks, proj, do, o, lse, proj, proj, proj, proj, cos128, sin128)


def _swa_bwd_dkv(proj, do, o, lse, cos128, sin128, S, li):
    T = proj.shape[0]
    B = T // S
    bk, nk, nsub = _swa_blocks(S)
    nrow = S // LCH
    qc, kc, vc = OFF_BQ // 512, OFF_BK // LANES, OFF_BV // LANES

    def body(q_ref, qn_ref, do_ref, don_ref, o_ref, on_ref, lse_ref, lsen_ref, k_ref, v_ref, cos_ref, sin_ref,
             dk_ref, dv_ref):
        j = pl.program_id(2)
        m0 = _lane_iota() < HD
        has_next = (j < nk - 1).astype(F32)
        kf = k_ref[...].astype(F32)
        vf = v_ref[...].astype(F32)
        kd = (_dup_head(kf, 0), _dup_head(kf, 1))
        vd = (_dup_head(vf, 0), _dup_head(vf, 1))
        lane = _lane_iota()

        def stat_rows(lse_r, do_r, o_r, rows, scale):
            a_lse = jnp.zeros((rows, LANES), F32)
            a_del = jnp.zeros((rows, LANES), F32)
            for ch in range(4):
                lanes = slice(LANES * ch, LANES * (ch + 1))
                prod = do_r[:, lanes].astype(F32) * o_r[:, lanes].astype(F32)
                for hh in range(2):
                    h = 2 * ch + hh
                    a_lse = jnp.where(lane == h, lse_r[0, h], a_lse)
                    a_del = jnp.where(lane == h, _head_sum(prod, hh), a_del)
            if scale is not None:
                a_del = a_del * scale
            return a_lse.T, a_del.T

        lse_t, del_t = stat_rows(lse_ref, do_ref, o_ref, bk, None)
        lsen_t, deln_t = stat_rows(lsen_ref, don_ref, on_ref, LCH, has_next)
        r_ = lax.broadcasted_iota(jnp.int32, (LCH, LCH), 0)
        c_ = lax.broadcasted_iota(jnp.int32, (LCH, LCH), 1)
        masks4 = (jnp.concatenate([r_ <= c_] * 4, axis=1), jnp.concatenate([r_ > c_] * 4, axis=1))
        for kr in range(nsub):
            krows = slice(LCH * kr, LCH * (kr + 1))
            dk = jnp.zeros((LCH, LANES), F32)
            dv = jnp.zeros((LCH, LANES), F32)
            for dq_blk in range(2):
                rq = kr + dq_blk
                nxt = rq == nsub
                qrows = slice(0, LCH) if nxt else slice(LCH * rq, LCH * (rq + 1))
                qr, dor = (qn_ref, don_ref) if nxt else (q_ref, do_ref)
                lt, dt_ = (lsen_t, deln_t) if nxt else (lse_t, del_t)
                for kvl in range(2):
                    qs = _stack_heads(qr, qrows, kvl)
                    dos = _stack_heads(dor, qrows, kvl)
                    if nxt:
                        dos = (dos.astype(F32) * has_next).astype(BF16)
                    lse_row = jnp.concatenate([lt[4 * kvl + t:4 * kvl + t + 1, qrows] for t in range(4)], axis=1)
                    del_row = jnp.concatenate([dt_[4 * kvl + t:4 * kvl + t + 1, qrows] for t in range(4)], axis=1)
                    st = jnp.where(masks4[dq_blk], _dot_nt(kd[kvl][krows], qs) * SCALE, NEG)
                    pt = jnp.exp(st - lse_row)
                    dst = pt * (_dot_nt(vd[kvl][krows], dos) - del_row)
                    dvc = _dot(pt.astype(BF16), dos)
                    dkc = _dot(dst.astype(BF16), qs) * SCALE
                    own = m0 if kvl == 0 else jnp.logical_not(m0)
                    dv = dv + jnp.where(own, dvc + pltpu.roll(dvc, HD, 1), 0.0)
                    dk = dk + jnp.where(own, dkc + pltpu.roll(dkc, HD, 1), 0.0)
            dk = dk * cos_ref[krows, :] - _rot_half(dk) * sin_ref[krows, :]
            dk_ref[krows, :] = dk.astype(BF16)
            dv_ref[krows, :] = dv.astype(BF16)

    wide = lambda cc: pl.BlockSpec((bk, 512), lambda c, b, j, cc=cc: (b * nk + j, cc + c))
    nxt = lambda cc: pl.BlockSpec((LCH, 512), lambda c, b, j, cc=cc: (b * nrow + jnp.minimum((j + 1) * nsub, nrow - 1), cc + c))
    cur = lambda cc: pl.BlockSpec((bk, LANES), lambda c, b, j, cc=cc: (b * nk + j, cc + c))
    pos = pl.BlockSpec((bk, LANES), lambda c, b, j: (j, 0))
    return pl.pallas_call(
        body, name=f"swa_bwd_dkv_{li}", grid=(2, B, nk),
        in_specs=[wide(qc), nxt(qc), wide(0), nxt(0), wide(0), nxt(0),
                  pl.BlockSpec((1, 8, bk, 1), lambda c, b, j: (b, c, j, 0)),
                  pl.BlockSpec((1, 8, LCH, 1), lambda c, b, j: (b, c, jnp.minimum((j + 1) * nsub, nrow - 1), 0)),
                  cur(kc), cur(vc), pos, pos],
        out_specs=[cur(0), cur(0)],
        out_shape=[jax.ShapeDtypeStruct((T, 2 * LANES), BF16), jax.ShapeDtypeStruct((T, 2 * LANES), BF16)],
        compiler_params=_cparams(("parallel", "parallel", "parallel"), VMEM_LIMIT),
    )(proj, proj, do, do, o, o, lse, lse, proj, proj, cos128, sin128)


HALO = 16


def _shift_matrices():
    r = lax.broadcasted_iota(jnp.int32, (3 * LCH, LCH + HALO), 0)
    c = lax.broadcasted_iota(jnp.int32, (3 * LCH, LCH + HALO), 1)
    t, d = r % LCH, r // LCH + 1
    return (c == HALO + t - d).astype(BF16), (c == t + d).astype(BF16)


def _ssm_chunk_pre(prev16, cur16, first, sdn_ref, cw_ref, cb_ref, ps, dtb, alog):
    ext16 = jnp.concatenate([jnp.where(first, jnp.zeros_like(prev16), prev16), cur16], axis=0)
    sh = _dot(sdn_ref[...], ext16)
    pre = cb_ref[...] + cw_ref[3:4, :] * cur16.astype(F32)
    for d in range(1, 4):
        pre = pre + cw_ref[3 - d:4 - d, :] * sh[LCH * (d - 1):LCH * d]
    sg = _sigmoid(pre)
    dt = _softplus(ps + dtb)
    a = -jnp.exp(alog)
    r = lax.broadcasted_iota(jnp.int32, (LCH, LCH), 0)
    c = lax.broadcasted_iota(jnp.int32, (LCH, LCH), 1)
    acum = _dot_hi((r >= c).astype(F32), dt * a)
    return pre, sg, dt, a, acum, sh


def _expand_matrix():
    r = lax.broadcasted_iota(jnp.int32, (3 * LANES, D), 0)
    c = lax.broadcasted_iota(jnp.int32, (3 * LANES, D), 1)
    return ((r % LANES) == c // HD).astype(BF16)


def _expand_heads(v, ex_ref):
    return _dot(jnp.concatenate(_split3(v), axis=1).astype(BF16), ex_ref[...])


def _decay(acum, acum_t, h):
    r = lax.broadcasted_iota(jnp.int32, (LCH, LCH), 0)
    c = lax.broadcasted_iota(jnp.int32, (LCH, LCH), 1)
    causal = r >= c
    seg = acum[:, h:h + 1] - acum_t[h:h + 1, :]
    return jnp.where(causal, jnp.exp(jnp.where(causal, seg, 0.0)), 0.0)


def _ssm_pair_fwd(p, x, dt_x, acum, acum_t, e_x, w_x, cd, cb_g, b_g, c_g, hprev, dsk_ref):
    m0 = _lane_iota() < HD
    lanes = slice(LANES * p, LANES * (p + 1))
    x2 = x[:, lanes]
    dt2 = dt_x[:, lanes]
    xdt2 = x2 * dt2
    xdtb = xdt2.astype(BF16)
    lms, ms, yds = [], [], []
    for hh in range(2):
        lm = _decay(acum, acum_t, 2 * p + hh)
        mm = cb_g * lm
        lms.append(lm)
        ms.append(mm)
        yds.append(_dot(mm.astype(BF16), xdtb))
    yd2 = jnp.where(m0, yds[0], yds[1])
    w2 = w_x[:, lanes]
    xw = (xdt2 * w2).astype(BF16)
    s2 = _dot_tn(xw, b_g)
    z2 = _dot_nt(c_g, hprev.astype(BF16))
    e2 = e_x[:, lanes]
    rowsel = lax.broadcasted_iota(jnp.int32, (LANES, 1), 0) < HD
    cdcol = jnp.where(rowsel, cd[:, 2 * p:2 * p + 1], cd[:, 2 * p + 1:2 * p + 2])
    y2 = yd2 + z2 * e2 + dsk_ref[:, lanes] * x2
    return dict(x2=x2, dt2=dt2, xdt2=xdt2, xdtb=xdtb, lms=lms, ms=ms, yd2=yd2, w2=w2, xw=xw, s2=s2, z2=z2, e2=e2,
                cdcol=cdcol, y2=y2)


def _ssm_specs(S, rev):
    nc = S // LCH
    ch = (lambda c: nc - 1 - c) if rev else (lambda c: c)
    prev = pl.BlockSpec((HALO, 2 * D), lambda b, c: (jnp.maximum(b * (S // HALO) + ch(c) * (LCH // HALO) - 1, 0), 0))
    cur = pl.BlockSpec((LCH, 2 * D), lambda b, c: (b * nc + ch(c), 0))
    zed = pl.BlockSpec((LCH, D), lambda b, c: (b * nc + ch(c), OFF_AZ // D))
    row = pl.BlockSpec((LCH, D), lambda b, c: (b * nc + ch(c), 0))
    psb = pl.BlockSpec((LCH, LANES), lambda b, c: (b * nc + ch(c), 0))
    hpb = pl.BlockSpec((1, 1, NH // 2, LANES, NST), lambda b, c: (b, ch(c), 0, 0, 0))
    const = lambda r, w: pl.BlockSpec((r, w), lambda b, c: (0, 0))
    return nc, prev, cur, zed, row, psb, hpb, const


def _ssm_fwd(proj, ps, cw, cb, dtb, alog, dsk, nw, S, li):
    T = proj.shape[0]
    B = T // S
    nc, prev, cur, zed, row, psb, hpb, const = _ssm_specs(S, False)

    def body(prev_ref, cur_ref, z_ref, ps_ref, sdn_ref, ex_ref, cw_ref, cb_ref, dtb_ref, alog_ref, dsk_ref, nw_ref,
             ya_ref, hp_ref, h_scr):
        c = pl.program_id(1)

        @pl.when(c == 0)
        def _():
            h_scr[...] = jnp.zeros_like(h_scr)

        pre, sg, dt, a, acum, _ = _ssm_chunk_pre(prev_ref[...], cur_ref[...], c == 0, sdn_ref, cw_ref, cb_ref,
                                                 ps_ref[...], dtb_ref[...], alog_ref[...])
        act = pre * sg
        acum_t = acum.T
        last = acum[LCH - 1:LCH, :]
        cd = jnp.exp(last)
        dt, e_all, w_all = (_expand_heads(v, ex_ref) for v in (dt, jnp.exp(acum), jnp.exp(last - acum)))
        x = act[:, :D]
        for g in range(NGRP):
            b_g = act[:, D + NST * g:D + NST * (g + 1)].astype(BF16)
            c_g = act[:, D + NGRP * NST + NST * g:D + NGRP * NST + NST * (g + 1)].astype(BF16)
            cb_g = _dot_nt(c_g, b_g)
            ygs = []
            for p in (2 * g, 2 * g + 1):
                hprev = h_scr[p]
                hp_ref[0, 0, p] = hprev
                f = _ssm_pair_fwd(p, x, dt, acum, acum_t, e_all, w_all, cd, cb_g, b_g, c_g, hprev, dsk_ref)
                h_scr[p] = hprev * f["cdcol"] + f["s2"]
                z2 = z_ref[:, LANES * p:LANES * (p + 1)].astype(F32)
                ygs.append(f["y2"] * z2 * _sigmoid(z2))
            yg = jnp.concatenate(ygs, axis=1)
            r = lax.rsqrt(jnp.mean(yg * yg, axis=1, keepdims=True) + EPS)
            ya_ref[:, 2 * LANES * g:2 * LANES * (g + 1)] = (yg * r * nw_ref[:, 2 * LANES * g:2 * LANES * (g + 1)]).astype(BF16)

    return pl.pallas_call(
        body, name=f"ssm_fwd_{li}", grid=(B, nc),
        in_specs=[prev, cur, zed, psb, const(3 * LCH, LCH + HALO), const(3 * LANES, D), const(4, 2 * D),
                  const(1, 2 * D), const(1, LANES), const(1, LANES), const(1, D), const(1, D)],
        out_specs=[row, hpb],
        out_shape=[jax.ShapeDtypeStruct((T, D), BF16), jax.ShapeDtypeStruct((B, nc, NH // 2, LANES, NST), F32)],
        scratch_shapes=[pltpu.VMEM((NH // 2, LANES, NST), F32)],
        compiler_params=_cparams(("arbitrary", "arbitrary"), VMEM_LIMIT),
    )(proj, proj, proj, ps, _shift_matrices()[0], _expand_matrix(), cw, cb, dtb, alog, dsk, nw)


def _ssm_bwd(proj, ps, hp, dya, cw, cb, dtb, alog, dsk, nw, S, li, comm=None):
    T = proj.shape[0]
    B = T // S
    nc, prev, cur, zed, row, psb, hpb, const = _ssm_specs(S, True)

    def body(prev_ref, cur_ref, z_ref, ps_ref, hp_ref, dy_ref, sdn_ref, sup_ref, ex_ref, cw_ref, cb_ref, dtb_ref,
             alog_ref, dsk_ref, nw_ref, dxbc_ref, dz_ref, dps_ref, pgw_ref, pg1_ref, pgh_ref, dh_scr, dhead, dact):
        b, cc = pl.program_id(0), pl.program_id(1)
        c = nc - 1 - cc

        @pl.when(jnp.logical_and(b == 0, cc == 0))
        def _():
            pgw_ref[...] = jnp.zeros_like(pgw_ref)
            pg1_ref[...] = jnp.zeros_like(pg1_ref)
            pgh_ref[...] = jnp.zeros_like(pgh_ref)

        @pl.when(cc == 0)
        def _():
            dh_scr[...] = jnp.zeros_like(dh_scr)
            dhead[...] = jnp.zeros_like(dhead)

        psv = ps_ref[...]
        cur16 = cur_ref[...]
        pre, sg, dt, a, acum, sh = _ssm_chunk_pre(prev_ref[...], cur16, c == 0, sdn_ref, cw_ref, cb_ref, psv,
                                                  dtb_ref[...], alog_ref[...])
        act = pre * sg
        acum_t = acum.T
        last = acum[LCH - 1:LCH, :]
        w_all = jnp.exp(last - acum)
        cd = jnp.exp(last)
        dt_x, e_x, w_x = (_expand_heads(v, ex_ref) for v in (dt, jnp.exp(acum), w_all))
        x = act[:, :D]
        lane = _lane_iota()
        m0 = lane < HD
        head_row = lax.broadcasted_iota(jnp.int32, (LANES, 1), 0)
        rowsel = head_row < HD
        is_last_row = lax.broadcasted_iota(jnp.int32, (LCH, 1), 0) == LCH - 1
        dacum_all = jnp.zeros((LCH, LANES), F32)
        dacum_t = jnp.zeros((LANES, LCH), F32)
        ddt_all = jnp.zeros((LCH, LANES), F32)
        dd_row = jnp.zeros((1, LANES), F32)
        for g in range(NGRP):
            b_g = act[:, D + NST * g:D + NST * (g + 1)].astype(BF16)
            c_g = act[:, D + NGRP * NST + NST * g:D + NGRP * NST + NST * (g + 1)].astype(BF16)
            cb_g = _dot_nt(c_g, b_g)
            pairs = (2 * g, 2 * g + 1)
            fs, hps, zs, ygs = [], [], [], []
            for p in pairs:
                hprev = hp_ref[0, 0, p]
                f = _ssm_pair_fwd(p, x, dt_x, acum, acum_t, e_x, w_x, cd, cb_g, b_g, c_g, hprev, dsk_ref)
                z2 = z_ref[:, LANES * p:LANES * (p + 1)].astype(F32)
                fs.append(f)
                hps.append(hprev)
                zs.append(z2)
                ygs.append(f["y2"] * z2 * _sigmoid(z2))
            gl = slice(2 * LANES * g, 2 * LANES * (g + 1))
            yg = jnp.concatenate(ygs, axis=1)
            r = lax.rsqrt(jnp.mean(yg * yg, axis=1, keepdims=True) + EPS)
            dyn = dy_ref[:, gl].astype(F32)
            gg = dyn * nw_ref[:, gl]
            dyg = r * gg - yg * (r * r * r) * jnp.mean(gg * yg, axis=1, keepdims=True)
            pg1_ref[0:1, gl] += jnp.sum(dyn * yg * r, axis=0, keepdims=True)
            dg_g = jnp.zeros((LCH, LCH), F32)
            db_g = jnp.zeros((LCH, NST), F32)
            dc_g = jnp.zeros((LCH, NST), F32)
            for idx, p in enumerate(pairs):
                f, hprev, z2 = fs[idx], hps[idx], zs[idx]
                lanes = slice(LANES * p, LANES * (p + 1))
                dyg2 = dyg[:, LANES * idx:LANES * (idx + 1)]
                sgz = _sigmoid(z2)
                dy2 = dyg2 * z2 * sgz
                dz_ref[:, lanes] = (dyg2 * f["y2"] * sgz * (1.0 + z2 * (1.0 - sgz))).astype(BF16)
                x2, dt2, xdt2, xdtb, w2, e2, z2m = f["x2"], f["dt2"], f["xdt2"], f["xdtb"], f["w2"], f["e2"], f["z2"]
                dx2 = dsk_ref[:, lanes] * dy2
                dyx = dy2 * x2
                dxdt2 = jnp.zeros((LCH, LANES), F32)
                diag_cols = []
                for hh in range(2):
                    sel = m0 if hh == 0 else jnp.logical_not(m0)
                    dyb = jnp.where(sel, dy2, 0.0).astype(BF16)
                    dm = _dot_nt(dyb, xdtb)
                    dg_g = dg_g + dm * f["lms"][hh]
                    dxdt2 = dxdt2 + _dot_tn(f["ms"][hh].astype(BF16), dyb)
                    em = dm * f["ms"][hh]
                    diag_cols.append(jnp.sum(em, axis=1, keepdims=True))
                    dacum_t = dacum_t - jnp.where(head_row == 2 * p + hh, jnp.sum(em, axis=0, keepdims=True), 0.0)
                dz2m = dy2 * e2
                t_off = dz2m * z2m
                dc_g = dc_g + _dot(dz2m.astype(BF16), hprev.astype(BF16))
                dhprev = _dot_tn(dz2m.astype(BF16), c_g)
                dhn = dh_scr[p]
                dhnb = dhn.astype(BF16)
                dhprev = dhprev + dhn * f["cdcol"]
                t_h = dhn * hprev
                dxw2 = _dot_nt(b_g, dhnb)
                db_g = db_g + _dot(f["xw"], dhnb)
                dxdt2 = dxdt2 + dxw2 * w2
                t_w = dxw2 * xdt2
                dx2 = dx2 + dxdt2 * dt2
                t_dt = dxdt2 * x2
                for hh in range(2):
                    h = 2 * p + hh
                    onehot = (lane == h).astype(F32)
                    w_col = w_all[:, h:h + 1]
                    dw_col = _head_sum(t_w, hh) * w_col
                    rs = rowsel if hh == 0 else jnp.logical_not(rowsel)
                    dlast = (jnp.sum(jnp.where(rs, t_h, 0.0), keepdims=True) * cd[:, h:h + 1]
                             + jnp.sum(dw_col, keepdims=True))
                    dacum_col = diag_cols[hh] + _head_sum(t_off, hh) - dw_col + jnp.where(is_last_row, dlast, 0.0)
                    dacum_all = dacum_all + dacum_col * onehot
                    ddt_all = ddt_all + _head_sum(t_dt, hh) * onehot
                    sel = m0 if hh == 0 else jnp.logical_not(m0)
                    dd_row = dd_row + jnp.sum(jnp.where(sel, dyx, 0.0), keepdims=True) * onehot
                dh_scr[p] = dhprev
                dact[:, lanes] = dx2
            dgb = dg_g.astype(BF16)
            dc_g = dc_g + _dot(dgb, b_g)
            db_g = db_g + _dot_tn(dgb, c_g)
            dact[:, D + NST * g:D + NST * (g + 1)] = db_g
            dact[:, D + NGRP * NST + NST * g:D + NGRP * NST + NST * (g + 1)] = dc_g
        rr = lax.broadcasted_iota(jnp.int32, (LCH, LCH), 0)
        cc2 = lax.broadcasted_iota(jnp.int32, (LCH, LCH), 1)
        dadt = _dot_hi((cc2 >= rr).astype(F32), dacum_all + dacum_t.T)
        ddt_all = ddt_all + dadt * a
        heads = lane < NH
        da = jnp.sum(dadt * dt, axis=0, keepdims=True)
        dr = jnp.where(heads, ddt_all * _sigmoid(psv + dtb_ref[...]), 0.0)
        dps_ref[...] = dr
        pgh_ref[0:1, :] += jnp.sum(dr, axis=0, keepdims=True)
        pgh_ref[1:2, :] += jnp.where(heads, da * a, 0.0)
        pgh_ref[2:3, :] += dd_row
        dpre = dact[...] * sg * (1.0 + pre * (1.0 - sg))
        extd = jnp.concatenate([dpre, dhead[...]], axis=0)
        hi = extd.astype(BF16)
        lo = (extd - hi.astype(F32)).astype(BF16)
        up = _dot(sup_ref[...], hi) + _dot(sup_ref[...], lo)
        du = cw_ref[3:4, :] * dpre
        pgw_ref[3:4, :] += jnp.sum(dpre * cur16.astype(F32), axis=0, keepdims=True)
        for d in range(1, 4):
            du = du + cw_ref[3 - d:4 - d, :] * up[LCH * (d - 1):LCH * d]
            pgw_ref[3 - d:4 - d, :] += jnp.sum(dpre * sh[LCH * (d - 1):LCH * d], axis=0, keepdims=True)
        pgw_ref[4:5, :] += jnp.sum(dpre, axis=0, keepdims=True)
        dxbc_ref[...] = du.astype(BF16)
        dhead[...] = dpre[0:HALO, :]

    xbc_out = pl.BlockSpec((LCH, 2 * D), lambda b, c: (b * nc + nc - 1 - c, 0))
    acc = lambda w: pl.BlockSpec((8, w), lambda b, c: (0, 0))
    sdn, sup = _shift_matrices()
    return _hosted_call(
        body, comm, f"ssm_bwd_{li}", (B, nc),
        in_specs=[prev, cur, zed, psb, hpb, row, const(3 * LCH, LCH + HALO), const(3 * LCH, LCH + HALO),
                  const(3 * LANES, D), const(4, 2 * D), const(1, 2 * D), const(1, LANES), const(1, LANES),
                  const(1, D), const(1, D)],
        out_specs=[xbc_out, row, psb, acc(2 * D), acc(D), acc(LANES)],
        out_shape=[jax.ShapeDtypeStruct((T, 2 * D), BF16), jax.ShapeDtypeStruct((T, D), BF16),
                   jax.ShapeDtypeStruct((T, LANES), F32), jax.ShapeDtypeStruct((8, 2 * D), F32),
                   jax.ShapeDtypeStruct((8, D), F32), jax.ShapeDtypeStruct((8, LANES), F32)],
        scratch=[pltpu.VMEM((NH // 2, LANES, NST), F32), pltpu.VMEM((HALO, 2 * D), F32),
                 pltpu.VMEM((LCH, 2 * D), F32)],
        dims=("arbitrary", "arbitrary"),
        operands=(proj, proj, proj, ps, hp, dya, sdn, sup, _expand_matrix(), cw, cb, dtb, alog, dsk, nw))


def _lane_row(v, offset):
    return jnp.pad(v.astype(F32), (offset, LANES - offset - v.shape[0]))[None]


def _pack_rows(arrays):
    parts = []
    for a in arrays:
        flat = a.reshape(-1).astype(F32)
        pad = (-flat.shape[0]) % LANES
        parts.append(jnp.pad(flat, (0, pad)))
    flat = jnp.concatenate(parts)
    pad = (-flat.shape[0]) % (8 * LANES)
    return jnp.pad(flat, (0, pad)).reshape(-1, LANES)


def _unpack_rows(pack, shapes):
    flat = pack.reshape(-1)
    out, pos = [], 0
    for shp in shapes:
        n = math.prod(shp)
        out.append(flat[pos:pos + n].reshape(shp))
        pos += n + (-n) % LANES
    return out


def _split_w_in(w):
    main = jnp.concatenate([w[:, 0:3072], w[:, 3088:4112], w[:, 4624:5648], w[:, 5648:8720], w[:, 8736:12832],
                            w[:, 4112:4624]], axis=1)
    small = jnp.concatenate([w[:, 3072:3088], w[:, 8720:8736], jnp.zeros((D, LANES - 2 * NH), w.dtype)], axis=1)
    return main, small


def _join_w_in(dw, ds):
    xbc, az, bq, bz, cq, ck, cv, cz, gates, bk, bv = dw
    return jnp.concatenate([xbc, az, ds[:, 0:NH], bq, bk, bv, bz, cq, ck, cv, ds[:, NH:2 * NH], cz, gates], axis=1)


def kernel(x, norm_w, w_in, conv_w, conv_b, dt_bias, a_log, d_skip, ssm_norm_w, sinks, f_bias, gate_bias, w_proj, w_out, final_norm_w, loss_target, m_norm_w, m_w_in, m_conv_w, m_conv_b, m_dt_bias, m_a_log, m_d_skip, m_ssm_norm_w, m_sinks, m_f_bias, m_gate_bias, m_w_proj, m_w_out, m_final_norm_w, v_norm_w, v_w_in, v_conv_w, v_conv_b, v_dt_bias, v_a_log, v_d_skip, v_ssm_norm_w, v_sinks, v_f_bias, v_gate_bias, v_w_proj, v_w_out, v_final_norm_w):
    Bl, S, _ = x.shape
    T = Bl * S
    depth = norm_w.shape[0]
    me = 4 * lax.axis_index("x") + 2 * lax.axis_index("y") + lax.axis_index("c")
    csh, gsh = conv_w.shape[2], gate_bias.shape[2]

    def gather_plan(l):
        small = jnp.concatenate([conv_w[l].reshape(-1), gate_bias[l].reshape(-1)]).reshape(-1, LANES)
        return _Comm("gather", [w_in[l].astype(BF16), w_proj[l].astype(BF16), w_out[l].astype(BF16), small])

    def unpack_weights(res):
        g_win, g_wp, g_wo, g_small = res
        flat = g_small.reshape(NDEV, -1)
        return (g_win.transpose(1, 0, 2).reshape(D, NIN),
                g_wp.transpose(1, 0, 2, 3).reshape(3, D, D),
                g_wo.reshape(D, D),
                flat[:, :4 * csh].reshape(NDEV, 4, csh).transpose(1, 0, 2).reshape(4, 2 * D),
                flat[:, 4 * csh:].reshape(NDEV, 3, gsh).transpose(1, 0, 2).reshape(3, D))

    def scatter_plan(gw_in=None, gw_p=None, gw_o=None):
        arrays = [] if gw_in is None else [gw_in.astype(BF16).reshape(-1, NDEV, NSH).transpose(1, 0, 2)]
        if gw_p is not None:
            arrays += [gw_p.astype(BF16).reshape(3, NDEV, D // NDEV, D).transpose(1, 0, 2, 3),
                       gw_o.astype(BF16).reshape(NDEV, D // NDEV, D)]
        return _Comm("scatter", arrays)

    pos = jnp.arange(S, dtype=F32)
    inv_freq = ROPE_THETA ** (-jnp.arange(0, HD, 2, dtype=F32) / HD)
    ang = pos[:, None] * inv_freq[None, :]
    cos128 = jnp.tile(jnp.cos(ang), (1, 4))
    sign = jnp.where((jnp.arange(LANES) % HD) < HD // 2, -1.0, 1.0).astype(F32)
    sin128 = jnp.tile(jnp.sin(ang), (1, 4)) * sign[None, :]

    x2 = x.reshape(T, D)
    tgt2 = loss_target.reshape(T, D)

    saved = []
    xcur = x2
    weights = [None] * depth
    weights[0] = unpack_weights(_gather_two_level(gather_plan(0).arrays, "gather_weights_0"))
    for l in range(depth):
        win_l, wp_l, wo_l, cw_l, gb_l = weights[l]
        wmain, wsmall = _split_w_in(win_l)
        proj, ps, h_t = _inproj_fwd(xcur, norm_w[l][None], wmain, wsmall, cos128, sin128, S, l)
        dtb = _lane_row(dt_bias[l], 0)
        alog = _lane_row(a_log[l], 0)
        fb = _lane_row(f_bias[l], NH)
        dsk = jnp.repeat(d_skip[l], HD)[None]
        ya, hp = _ssm_fwd(proj, ps, cw_l, conv_b[l][None], dtb, alog, dsk, ssm_norm_w[l][None], S, l)
        yb, ob, lse_b = _swa_fwd(proj, sinks[l], S, l)
        cum = _fox_cum(ps, fb, S, l)
        cumh = cum[:, NH:2 * NH].reshape(Bl, S, NH).transpose(0, 2, 1)
        cum_row = cumh[:, :, None, :]
        comm = gather_plan(l + 1) if l + 1 < depth else None
        res = _fox_fwd(proj, cum_row, S, l, comm)
        yc, oc, lse_c = res[:3]
        if comm is not None:
            weights[l + 1] = unpack_weights(res[3:])
        xnext, br, y_t = _merge_fwd(ya, yb, yc, proj, gb_l, wp_l, wo_l, xcur, l)
        saved.append(dict(x=xcur, wmain=wmain, wsmall=wsmall, proj=proj, ps=ps, h_t=h_t, dtb=dtb, alog=alog, fb=fb,
                          dsk=dsk, hp=hp, ob=ob, lse_b=lse_b, cum_row=cum_row, oc=oc, lse_c=lse_c, br=br, y_t=y_t))
        xcur = xnext

    dx, dx16, st = _final_loss(xcur, tgt2, final_norm_w[None])
    loss_part = st[2, 0]
    g_final = st[0]

    gsm = {k: [None] * depth for k in ("norm_w", "conv_w", "conv_b", "dt_bias", "a_log", "d_skip", "ssm_norm_w",
                                      "sinks", "f_bias", "gate_bias")}
    parts = [None] * depth
    pending = None
    for l in reversed(range(depth)):
        sv = saved[l]
        proj, ps = sv["proj"], sv["ps"]
        _, wp_l, wo_l, cw_l, gb_l = weights[l]
        dbr, dgates, merged_t, dgb, dy_a, do_b, dbz, do_c, dcz = _merge_bwd(dx16, wo_l, wp_l, sv["br"], proj, gb_l,
                                                                            sv["ob"], sv["oc"], l)
        g_wo = _matmul(merged_t, dx16, BF16, f"dwout_{l}")
        g_wp = _matmul_batched(sv["y_t"], dbr, BF16, f"dwproj_{l}")
        gsm["gate_bias"][l] = dgb[0:3]
        res = _ssm_bwd(proj, ps, sv["hp"], dy_a, cw_l, conv_b[l][None], sv["dtb"], sv["alog"], sv["dsk"],
                       ssm_norm_w[l][None], S, l, pending)
        dxbc, daz, dps_a, pgw, pg1, pgh = res[:6]
        if pending is not None:
            parts[l + 1] = res[6:]
        gsm["conv_w"][l], gsm["conv_b"][l] = pgw[0:4], pgw[4]
        gsm["ssm_norm_w"][l] = pg1[0]
        gsm["dt_bias"][l], gsm["a_log"][l], gsm["d_skip"][l] = pgh[0, :NH], pgh[1, :NH], pgh[2, :NH]
        dq_b, dsk_b = _swa_bwd_dq(proj, do_b, sv["ob"], sv["lse_b"], sinks[l], cos128, sin128, S, l)
        dk_b, dv_b = _swa_bwd_dkv(proj, do_b, sv["ob"], sv["lse_b"], cos128, sin128, S, l)
        gsm["sinks"][l] = dsk_b[:, :, 0].reshape(NH)
        plan_po = scatter_plan(None, g_wp, g_wo) if l == 0 else None
        res = _fox_bwd(proj, do_c, sv["oc"], sv["cum_row"], sv["lse_c"], S, l, plan_po)
        dq_c, dk_c, dv_c, dcum_k, dcum_q = res[:5]
        dcum_tm = (dcum_k.reshape(Bl, NH, S) + dcum_q.reshape(Bl, NH, S)).transpose(0, 2, 1).reshape(T, NH)
        dcum_pad = jnp.pad(dcum_tm, ((0, 0), (NH, LANES - 2 * NH)))
        df, dfb = _fox_cum_bwd(dcum_pad, ps, sv["fb"], S, l)
        gsm["f_bias"][l] = dfb[0, NH:2 * NH]
        dps16 = (dps_a + df).astype(BF16)
        pieces = (dxbc, daz, dq_b, dbz, dq_c, dk_c, dv_c, dcz, dgates, dk_b, dv_b)
        dw_pieces = [_matmul(sv["h_t"], pc, BF16, f"dwin_{l}_{i}") for i, pc in enumerate(pieces)]
        dws = _matmul(sv["h_t"], dps16, BF16, f"dwin_small_{l}")
        g_win = _join_w_in(dw_pieces, dws)
        if l == 0:
            plans = [scatter_plan(g_win[r0:r1]) for r0, r1 in ROW_CHUNKS]
            parts_po = res[5:]
        else:
            plans, pending = [None] * len(ROW_CHUNKS), scatter_plan(g_win, g_wp, g_wo)
        dkv_b = jnp.concatenate([dk_b, dv_b], axis=1)
        res1 = _inproj_bwd_dx([(dxbc, OFF_XBC), (daz, OFF_AZ), (dq_b, OFF_BQ), (dbz, OFF_BZ)], sv["wmain"],
                              ("narrow", dps16, sv["wsmall"]), None, f"inproj_bwd_dh1_{l}", plans[0])
        res2 = _inproj_bwd_dx([(dq_c, OFF_CQ), (dk_c, OFF_CK), (dv_c, OFF_CV), (dcz, OFF_CZ)], sv["wmain"],
                              ("acc", res1[0]), None, f"inproj_bwd_dh2_{l}", plans[1])
        dx, dx16, dnw = _inproj_bwd_dx([(dgates, OFF_G), (dkv_b, OFF_BK)], sv["wmain"], ("acc", res2[0]),
                                       (sv["x"], norm_w[l][None], dx), f"inproj_bwd_dx_{l}")
        if l == 0:
            parts[0] = [jnp.concatenate([res1[1], res2[1]], axis=1), *parts_po]
        gsm["norm_w"][l] = dnw[0]

    big = {}
    for idx, (name, w, m, v) in enumerate((("w_in", w_in, m_w_in, v_w_in), ("w_proj", w_proj, m_w_proj, v_w_proj),
                                          ("w_out", w_out, m_w_out, v_w_out))):
        cols = w.shape[-1]
        res = _sum_adamw([parts[l][idx].reshape(NDEV, -1, cols) for l in range(depth)], w.reshape(depth, -1, cols),
                         m.reshape(depth, -1, cols), v.reshape(depth, -1, cols), f"adamw_{name}")
        big[name] = [r.reshape(w.shape) for r in res]

    small_names = ("norm_w", "conv_b", "dt_bias", "a_log", "d_skip", "ssm_norm_w", "sinks", "f_bias")
    small_parts = [jnp.stack(gsm[k]) for k in small_names] + [g_final, jnp.stack(gsm["conv_w"]),
                                                              jnp.stack(gsm["gate_bias"]), loss_part.reshape(1)]
    shapes = [a.shape for a in small_parts]
    summed = _unpack_rows(_all_reduce_small(_pack_rows(small_parts)), shapes)
    g_small = dict(zip(small_names, summed[:len(small_names)]))
    g_small["final_norm_w"] = summed[len(small_names)]
    g_small["conv_w"] = lax.dynamic_slice_in_dim(summed[len(small_names) + 1], me * csh, csh, axis=2)
    g_small["gate_bias"] = lax.dynamic_slice_in_dim(summed[len(small_names) + 2], me * gsh, gsh, axis=2)
    loss = summed[len(small_names) + 3][0]

    ws = dict(norm_w=norm_w, conv_w=conv_w, conv_b=conv_b, dt_bias=dt_bias, a_log=a_log, d_skip=d_skip,
              ssm_norm_w=ssm_norm_w, sinks=sinks, f_bias=f_bias, gate_bias=gate_bias, final_norm_w=final_norm_w)
    ms = dict(norm_w=m_norm_w, conv_w=m_conv_w, conv_b=m_conv_b, dt_bias=m_dt_bias, a_log=m_a_log, d_skip=m_d_skip,
              ssm_norm_w=m_ssm_norm_w, sinks=m_sinks, f_bias=m_f_bias, gate_bias=m_gate_bias,
              final_norm_w=m_final_norm_w)
    vs = dict(norm_w=v_norm_w, conv_w=v_conv_w, conv_b=v_conv_b, dt_bias=v_dt_bias, a_log=v_a_log, d_skip=v_d_skip,
              ssm_norm_w=v_ssm_norm_w, sinks=v_sinks, f_bias=v_f_bias, gate_bias=v_gate_bias,
              final_norm_w=v_final_norm_w)
    order = list(ws)
    oshapes = [ws[k].shape for k in order]
    res = _adamw_small(_pack_rows([g_small[k] for k in order]), _pack_rows([ws[k] for k in order]),
                       _pack_rows([ms[k] for k in order]), _pack_rows([vs[k] for k in order]))
    d_s, m_s, v_s = (dict(zip(order, _unpack_rows(r, oshapes))) for r in res)

    names = ("norm_w", "w_in", "conv_w", "conv_b", "dt_bias", "a_log", "d_skip", "ssm_norm_w", "sinks", "f_bias",
             "gate_bias", "w_proj", "w_out", "final_norm_w")
    grads, deltas, new_m, new_v = [], [], [], []
    for k in names:
        if k in big:
            g, d_, m_, v_ = big[k]
        else:
            g, d_, m_, v_ = g_small[k], d_s[k], m_s[k], v_s[k]
        grads.append(g)
        deltas.append(d_)
        new_m.append(m_)
        new_v.append(v_)
    return (loss, dx.reshape(Bl, S, D), *grads, *deltas, *new_m, *new_v)
```

```python
import functools
import math

import jax
import jax.numpy as jnp
from jax import lax
from jax.experimental import pallas as pl
from jax.experimental.pallas import tpu as pltpu

F32 = jnp.float32
BF16 = jnp.bfloat16
MESH = pl.DeviceIdType.MESH
NDEV = 8

D = 1024
NH = 16
HD = 64
NST = 128
NGRP = 4
LCH = 128
EPS = 1e-6
ROPE_THETA = 10000.0
SCALE = HD ** -0.5
NEG = -1e30

LANES = 128
VMEM_LIMIT = 56 * 1024 * 1024

OFF_XBC, OFF_AZ, OFF_BQ, OFF_BZ, OFF_CQ, OFF_CK, OFF_CV, OFF_CZ, OFF_G, OFF_BK, OFF_BV = (
    0, 2048, 3072, 4096, 5120, 6144, 7168, 8192, 9216, 12288, 12544)
NMAIN = 12800
NIN = 12832
NSH = NIN // NDEV

ROW_CHUNKS = ((0, 512), (512, 1024))

ADAM_LR, ADAM_B1, ADAM_B2, ADAM_EPS, ADAM_WD, ADAM_STEP = 0.001, 0.9, 0.999, 1e-08, 0.01, 10


def _cparams(dims=None, vmem=None):
    return pltpu.CompilerParams(dimension_semantics=dims, vmem_limit_bytes=vmem)


def _dot(a, b):
    return jnp.dot(a, b, preferred_element_type=F32)


def _dot_nt(a, b):
    return lax.dot_general(a, b, (((1,), (1,)), ((), ())), preferred_element_type=F32)


def _dot_tn(a, b):
    return lax.dot_general(a, b, (((0,), (0,)), ((), ())), preferred_element_type=F32)


def _dot_hi(a, b):
    return jnp.dot(a, b, precision=lax.Precision.HIGHEST, preferred_element_type=F32)


def _sigmoid(x):
    return 0.5 * jnp.tanh(0.5 * x) + 0.5


def _softplus(x):
    return jnp.maximum(x, 0.0) + jnp.log(1.0 + jnp.exp(-jnp.abs(x)))


def _lane_iota(n=LANES):
    return lax.broadcasted_iota(jnp.int32, (1, n), 1)


def _rot_half(x):
    first = (_lane_iota() % HD) < (HD // 2)
    return jnp.where(first, pltpu.roll(x, LANES - HD // 2, 1), pltpu.roll(x, HD // 2, 1))


def _head_sum(x, head):
    m = (_lane_iota() < HD) if head == 0 else (_lane_iota() >= HD)
    return jnp.sum(jnp.where(m, x, 0.0), axis=1, keepdims=True)


def _me_and_peers():
    x, y, c = lax.axis_index("x"), lax.axis_index("y"), lax.axis_index("c")
    me = 4 * x + 2 * y + c
    peers = []
    for k in range(1, NDEV):
        kx, ky, kc = (k >> 2) & 1, (k >> 1) & 1, k & 1
        px, py, pc = x ^ kx, y ^ ky, c ^ kc
        peers.append(((px, py, pc), 4 * px + 2 * py + pc))
    return me, peers


class _Comm:
    def __init__(self, kind, arrays):
        self.kind, self.arrays, self.n = kind, list(arrays), len(arrays)
        any_spec = pl.BlockSpec(memory_space=pl.ANY)
        self.in_specs = [any_spec] * self.n
        self.out_specs = [any_spec] * self.n
        self.out_shape = [jax.ShapeDtypeStruct(((NDEV,) + a.shape) if kind == "gather" else a.shape, a.dtype)
                          for a in self.arrays]
        self.scratch = [pltpu.SemaphoreType.DMA((self.n, NDEV - 1)), pltpu.SemaphoreType.DMA((self.n, NDEV - 1)),
                        pltpu.SemaphoreType.DMA((self.n,))]

    def copies(self, ins, outs, sems):
        send_sems, recv_sems, local_sems = sems
        me, peers = _me_and_peers()
        out = []
        for a in range(self.n):
            mine = ins[a] if self.kind == "gather" else ins[a].at[me]
            out.append(pltpu.make_async_copy(mine, outs[a].at[me], local_sems.at[a]))
            for k, (peer, pidx) in enumerate(peers):
                src = ins[a] if self.kind == "gather" else ins[a].at[pidx]
                out.append(pltpu.make_async_remote_copy(
                    src_ref=src, dst_ref=outs[a].at[me], send_sem=send_sems.at[a, k], recv_sem=recv_sems.at[a, k],
                    device_id=peer, device_id_type=MESH))
        return out


def _gather_two_level(arrays, name):
    n = len(arrays)

    def body(*refs):
        ins, outs = refs[:n], refs[n:2 * n]
        send_sems, recv_sems, local_sems = refs[2 * n:]
        x, y, c = lax.axis_index("x"), lax.axis_index("y"), lax.axis_index("c")
        me, sibling = (x, y, c), (x, y, 1 - c)
        chips = [(1 - x, y), (x, 1 - y), (1 - x, 1 - y)]

        def slot(a, dev):
            return outs[a].at[4 * dev[0] + 2 * dev[1] + dev[2]]

        def copy(a, k, block, to, src=None):
            return pltpu.make_async_remote_copy(
                src_ref=slot(a, block) if src is None else src, dst_ref=slot(a, block),
                send_sem=send_sems.at[a, k], recv_sem=recv_sems.at[a, k], device_id=to, device_id_type=MESH)

        mine = [pltpu.make_async_copy(ins[a], slot(a, me), local_sems.at[a]) for a in range(n)]
        for cp in mine:
            cp.start()
        first = []
        for a in range(n):
            first.append(copy(a, 0, me, sibling, src=ins[a]))
            first += [copy(a, 1 + j, me, (*chip, c), src=ins[a]) for j, chip in enumerate(chips)]
        for cp in first:
            cp.start()
        passed = []
        for j, chip in enumerate(chips):
            for a in range(n):
                copy(a, 1 + j, (*chip, c), me).wait_recv()
                fwd = copy(a, 4 + j, (*chip, c), sibling)
                fwd.start()
                passed.append(fwd)
        for a in range(n):
            copy(a, 0, sibling, me).wait_recv()
            for j, chip in enumerate(chips):
                copy(a, 4 + j, (*chip, 1 - c), me).wait_recv()
        for cp in first + passed:
            cp.wait_send()
        for cp in mine:
            cp.wait()

    any_spec = pl.BlockSpec(memory_space=pl.ANY)
    return pl.pallas_call(
        body, name=name, out_shape=[jax.ShapeDtypeStruct((NDEV,) + a.shape, a.dtype) for a in arrays],
        in_specs=[any_spec] * n, out_specs=[any_spec] * n,
        scratch_shapes=[pltpu.SemaphoreType.DMA((n, NDEV - 1)), pltpu.SemaphoreType.DMA((n, NDEV - 1)),
                        pltpu.SemaphoreType.DMA((n,))])(*arrays)


def _hosted_call(body, comm, name, grid, in_specs, out_specs, out_shape, scratch, dims, operands):
    if comm is None:
        return pl.pallas_call(body, name=name, grid=grid, in_specs=in_specs, out_specs=out_specs, out_shape=out_shape,
                              scratch_shapes=scratch, compiler_params=_cparams(dims, VMEM_LIMIT))(*operands)
    n_in, n_out, n_scr, n = len(in_specs), len(out_specs), len(scratch), comm.n

    def hosted(*refs):
        hin, cin = refs[:n_in], refs[n_in:n_in + n]
        hout = refs[n_in + n:n_in + n + n_out]
        cout = refs[n_in + n + n_out:n_in + 2 * n + n_out]
        hscr = refs[n_in + 2 * n + n_out:n_in + 2 * n + n_out + n_scr]
        sems = refs[n_in + 2 * n + n_out + n_scr:]
        ids = [pl.program_id(a) for a in range(len(grid))]
        first = functools.reduce(jnp.logical_and, [i == 0 for i in ids])
        last = functools.reduce(jnp.logical_and, [i == g - 1 for i, g in zip(ids, grid)])

        @pl.when(first)
        def _():
            for cp in comm.copies(cin, cout, sems):
                cp.start()

        body(*hin, *hout, *hscr)

        @pl.when(last)
        def _():
            for cp in comm.copies(cin, cout, sems):
                cp.wait()

    return pl.pallas_call(
        hosted, name=name, grid=grid, in_specs=list(in_specs) + comm.in_specs,
        out_specs=list(out_specs) + comm.out_specs, out_shape=list(out_shape) + comm.out_shape,
        scratch_shapes=list(scratch) + comm.scratch,
        compiler_params=_cparams(("arbitrary",) * len(grid), VMEM_LIMIT))(*operands, *comm.arrays)


def _all_reduce_small(v):
    rows = v.shape[0]

    def body(v_ref, sum_ref, all_ref, send_sems, recv_sems):
        me, peers = _me_and_peers()
        all_ref[me] = v_ref[...]
        copies = []
        for k, (peer, _) in enumerate(peers):
            cp = pltpu.make_async_remote_copy(
                src_ref=v_ref, dst_ref=all_ref.at[me],
                send_sem=send_sems.at[k], recv_sem=recv_sems.at[k],
                device_id=peer, device_id_type=MESH)
            cp.start()
            copies.append(cp)
        for cp in copies:
            cp.wait()
        acc = all_ref[0]
        for d in range(1, NDEV):
            acc = acc + all_ref[d]
        sum_ref[...] = acc

    vm = pl.BlockSpec(memory_space=pltpu.VMEM)
    return pl.pallas_call(
        body, name="all_reduce_small",
        out_shape=jax.ShapeDtypeStruct((rows, LANES), F32),
        in_specs=[vm], out_specs=vm,
        scratch_shapes=[pltpu.VMEM((NDEV, rows, LANES), F32),
                        pltpu.SemaphoreType.DMA((NDEV - 1,)), pltpu.SemaphoreType.DMA((NDEV - 1,))],
    )(v)


def _adamw_math(w, g, m, v):
    m = ADAM_B1 * m + (1.0 - ADAM_B1) * g
    v = ADAM_B2 * v + (1.0 - ADAM_B2) * jnp.square(g)
    m_hat = m / (1.0 - ADAM_B1 ** ADAM_STEP)
    v_hat = v / (1.0 - ADAM_B2 ** ADAM_STEP)
    delta = -ADAM_LR * (m_hat / (jnp.sqrt(v_hat) + ADAM_EPS) + ADAM_WD * w)
    return delta, m, v


def _sum_adamw(parts, w, m, v, name):
    depth, rows, cols = w.shape
    tr = next(c for c in (256, 128, 64, 32, 16) if rows % c == 0)
    nb = rows // tr

    def body(*refs):
        p_refs, (w_ref, m_ref, v_ref, g_ref, d_ref, nm_ref, nv_ref) = refs[:depth], refs[depth:]
        l = pl.program_id(0)
        for ll in range(depth):
            @pl.when(l == ll)
            def _(ll=ll):
                g = p_refs[ll][0].astype(F32)
                for d in range(1, NDEV):
                    g = g + p_refs[ll][d].astype(F32)
                delta, nm, nv = _adamw_math(w_ref[0], g, m_ref[0], v_ref[0])
                g_ref[0] = g
                d_ref[0] = delta
                nm_ref[0] = nm
                nv_ref[0] = nv

    part = lambda ll: pl.BlockSpec((NDEV, tr, cols), lambda l, i, ll=ll: (0, jnp.where(l == ll, i, jnp.where(l < ll, 0, nb - 1)), 0))
    blk = pl.BlockSpec((1, tr, cols), lambda l, i: (l, i, 0))
    sds = jax.ShapeDtypeStruct((depth, rows, cols), F32)
    return pl.pallas_call(
        body, name=name, grid=(depth, nb),
        in_specs=[part(ll) for ll in range(depth)] + [blk, blk, blk],
        out_specs=[blk, blk, blk, blk], out_shape=[sds, sds, sds, sds],
        compiler_params=_cparams(("arbitrary", "arbitrary"), VMEM_LIMIT),
    )(*parts, w, m, v)


def _adamw_small(g, w, m, v):
    def body(g_ref, w_ref, m_ref, v_ref, d_ref, nm_ref, nv_ref):
        delta, nm, nv = _adamw_math(w_ref[...], g_ref[...], m_ref[...], v_ref[...])
        d_ref[...] = delta
        nm_ref[...] = nm
        nv_ref[...] = nv

    sds = jax.ShapeDtypeStruct(g.shape, F32)
    return pl.pallas_call(body, name="adamw_small", out_shape=[sds, sds, sds])(g, w, m, v)


def _matmul(a, b, out_dtype, name, tm=1024, tn=1024, tk=1024):
    M, K = a.shape
    N = b.shape[1]
    tm, tn, tk = min(tm, M), min(tn, N), min(tk, K)
    nk = K // tk

    def body(a_ref, b_ref, o_ref, acc):
        k = pl.program_id(2)

        @pl.when(k == 0)
        def _():
            acc[...] = jnp.zeros_like(acc)

        acc[...] += _dot(a_ref[...], b_ref[...])

        @pl.when(k == nk - 1)
        def _():
            o_ref[...] = acc[...].astype(out_dtype)

    return pl.pallas_call(
        body, name=name, grid=(M // tm, N // tn, nk),
        in_specs=[pl.BlockSpec((tm, tk), lambda i, j, k: (i, k)), pl.BlockSpec((tk, tn), lambda i, j, k: (k, j))],
        out_specs=pl.BlockSpec((tm, tn), lambda i, j, k: (i, j)),
        out_shape=jax.ShapeDtypeStruct((M, N), out_dtype),
        scratch_shapes=[pltpu.VMEM((tm, tn), F32)],
        compiler_params=_cparams(("parallel", "parallel", "arbitrary"), VMEM_LIMIT),
    )(a, b)


def _matmul_batched(a, b, out_dtype, name, tm=1024, tn=1024, tk=512):
    G, M, K = a.shape
    N = b.shape[2]
    tm, tn, tk = min(tm, M), min(tn, N), min(tk, K)
    nk = K // tk

    def body(a_ref, b_ref, o_ref, acc):
        k = pl.program_id(3)

        @pl.when(k == 0)
        def _():
            acc[...] = jnp.zeros_like(acc)

        acc[...] += _dot(a_ref[0], b_ref[0])

        @pl.when(k == nk - 1)
        def _():
            o_ref[0] = acc[...].astype(out_dtype)

    return pl.pallas_call(
        body, name=name, grid=(G, M // tm, N // tn, nk),
        in_specs=[pl.BlockSpec((1, tm, tk), lambda g, i, j, k: (g, i, k)),
                  pl.BlockSpec((1, tk, tn), lambda g, i, j, k: (g, k, j))],
        out_specs=pl.BlockSpec((1, tm, tn), lambda g, i, j, k: (g, i, j)),
        out_shape=jax.ShapeDtypeStruct((G, M, N), out_dtype),
        scratch_shapes=[pltpu.VMEM((tm, tn), F32)],
        compiler_params=_cparams(("parallel", "parallel", "parallel", "arbitrary"), VMEM_LIMIT),
    )(a, b)


def _inproj_fwd(x2, nw, wmain, wsmall, cos128, sin128, S, li, comm=None):
    T = x2.shape[0]
    tm, tn = min(2048, S), 512
    nj, npos = NMAIN // tn, S // tm
    jq0, jk = OFF_BQ // tn, OFF_BK // tn

    def body(x_ref, nw_ref, w_ref, ws_ref, cos_ref, sin_ref, proj_ref, ps_ref, ht_ref, h_scr):
        j = pl.program_id(1)

        @pl.when(j == 0)
        def _():
            x = x_ref[...]
            r = lax.rsqrt(jnp.mean(x * x, axis=-1, keepdims=True) + EPS)
            h = (x * r * nw_ref[...]).astype(BF16)
            h_scr[...] = h
            ht_ref[...] = h.T
            ps_ref[...] = _dot(h, ws_ref[...])

        acc = _dot(h_scr[...], w_ref[...])

        def roped(c):
            xc = acc[:, LANES * c:LANES * (c + 1)]
            return (xc * cos_ref[...] + _rot_half(xc) * sin_ref[...]).astype(BF16)

        def plain(c):
            return acc[:, LANES * c:LANES * (c + 1)].astype(BF16)

        is_q = jnp.logical_or(j == jq0, j == jq0 + 1)
        is_k = j == jk

        @pl.when(is_q)
        def _():
            for c in range(4):
                proj_ref[:, LANES * c:LANES * (c + 1)] = roped(c)

        @pl.when(is_k)
        def _():
            for c in range(4):
                proj_ref[:, LANES * c:LANES * (c + 1)] = roped(c) if c < 2 else plain(c)

        @pl.when(jnp.logical_not(jnp.logical_or(is_q, is_k)))
        def _():
            proj_ref[...] = acc.astype(BF16)

    return _hosted_call(
        body, comm, f"inproj_fwd_{li}", (T // tm, nj),
        in_specs=[pl.BlockSpec((tm, D), lambda i, j: (i, 0)),
                  pl.BlockSpec((1, D), lambda i, j: (0, 0)),
                  pl.BlockSpec((D, tn), lambda i, j: (0, j)),
                  pl.BlockSpec((D, LANES), lambda i, j: (0, 0)),
                  pl.BlockSpec((tm, LANES), lambda i, j: (i % npos, 0)),
                  pl.BlockSpec((tm, LANES), lambda i, j: (i % npos, 0))],
        out_specs=[pl.BlockSpec((tm, tn), lambda i, j: (i, j)),
                   pl.BlockSpec((tm, LANES), lambda i, j: (i, 0)),
                   pl.BlockSpec((D, tm), lambda i, j: (0, i))],
        out_shape=[jax.ShapeDtypeStruct((T, NMAIN), BF16), jax.ShapeDtypeStruct((T, LANES), F32),
                   jax.ShapeDtypeStruct((D, T), BF16)],
        scratch=[pltpu.VMEM((tm, D), BF16)], dims=("parallel", "arbitrary"),
        operands=(x2, nw, wmain, wsmall, cos128, sin128))


def _inproj_bwd_dx(segs, wmain, init, final, name, comm=None):
    T = segs[0][0].shape[0]
    tm = min(1024, T)
    tk = 1024 if all(a.shape[1] % 1024 == 0 and c % 1024 == 0 for a, c in segs) else 512
    ni = T // tm
    k0s, nks, c0s = [], [], []
    for arr, col0 in segs:
        k0s.append(sum(nks))
        nks.append(arr.shape[1] // tk)
        c0s.append(col0 // tk)
    nk = sum(nks)
    ns = len(segs)

    def in_range(k, s):
        return jnp.logical_and(k >= k0s[s], k < k0s[s] + nks[s])

    def wcol(i, k):
        g = 0
        for s in range(ns):
            g = g + jnp.where(in_range(k, s), c0s[s] + k - k0s[s], 0)
        return (0, g)

    n_init = 2 if init[0] == "narrow" else 1

    def body(*refs):
        seg_refs, w_ref = refs[:ns], refs[ns]
        init_refs = refs[ns + 1:ns + 1 + n_init]
        rest = refs[ns + 1 + n_init:]
        i, k = pl.program_id(0), pl.program_id(1)
        acc = rest[-1]

        @pl.when(k == 0)
        def _():
            if init[0] == "narrow":
                acc[...] = _dot_nt(init_refs[0][...], init_refs[1][...])
            else:
                acc[...] = init_refs[0][...]

        for s in range(ns):
            @pl.when(in_range(k, s))
            def _(s=s):
                acc[...] += _dot_nt(seg_refs[s][...], w_ref[...])

        if final is None:
            @pl.when(k == nk - 1)
            def _():
                rest[0][...] = acc[...]
        else:
            x_ref, nw_ref, dxo_ref, dx_ref, dx16_ref, dnw_ref = rest[:6]

            @pl.when(jnp.logical_and(i == 0, k == 0))
            def _():
                dnw_ref[...] = jnp.zeros_like(dnw_ref)

            @pl.when(k == nk - 1)
            def _():
                x = x_ref[...]
                r = lax.rsqrt(jnp.mean(x * x, axis=-1, keepdims=True) + EPS)
                dh = acc[...]
                g = dh * nw_ref[...]
                dx = dxo_ref[...] + r * g - x * (r * r * r) * jnp.mean(g * x, axis=-1, keepdims=True)
                dx_ref[...] = dx
                dx16_ref[...] = dx.astype(BF16)
                dnw_ref[0:1, :] += jnp.sum(dh * x * r, axis=0, keepdims=True)

    row = pl.BlockSpec((tm, D), lambda i, k: (i, 0))
    in_specs = [pl.BlockSpec((tm, tk), lambda i, k, s=s: (i, jnp.clip(k - k0s[s], 0, nks[s] - 1))) for s in range(ns)]
    in_specs.append(pl.BlockSpec((D, tk), wcol))
    operands = [a for a, _ in segs] + [wmain]
    if init[0] == "narrow":
        in_specs += [pl.BlockSpec((tm, LANES), lambda i, k: (i, 0)), pl.BlockSpec((D, LANES), lambda i, k: (0, 0))]
    else:
        in_specs.append(row)
    operands += list(init[1:])
    if final is None:
        out_specs, out_shape = [row], [jax.ShapeDtypeStruct((T, D), F32)]
    else:
        in_specs += [row, pl.BlockSpec((1, D), lambda i, k: (0, 0)), row]
        operands += list(final)
        out_specs = [row, row, pl.BlockSpec((8, D), lambda i, k: (0, 0))]
        out_shape = [jax.ShapeDtypeStruct((T, D), F32), jax.ShapeDtypeStruct((T, D), BF16),
                     jax.ShapeDtypeStruct((8, D), F32)]
    return _hosted_call(body, comm, name, (ni, nk), in_specs=in_specs, out_specs=out_specs, out_shape=out_shape,
                        scratch=[pltpu.VMEM((tm, D), F32)], dims=("arbitrary", "arbitrary"), operands=tuple(operands))


def _merge_fwd(ya, yb, yc, proj, gbias, wp, wout, x2, li):
    T = x2.shape[0]
    tm = min(512, T)
    gcol = OFF_G // D

    def body(ya_ref, yb_ref, yc_ref, g0_ref, g1_ref, g2_ref, gb_ref, wp_ref, wo_ref, x_ref, xn_ref, br_ref, yt_ref):
        merged = jnp.zeros((tm, D), F32)
        for i, (y_ref, g_ref) in enumerate(((ya_ref, g0_ref), (yb_ref, g1_ref), (yc_ref, g2_ref))):
            y = y_ref[...]
            yt_ref[i] = y.T
            br = _dot(y, wp_ref[i])
            br_ref[i] = br.astype(BF16)
            gate = _sigmoid(g_ref[...].astype(F32) + gb_ref[i:i + 1, :])
            merged = merged + gate * br
        xn_ref[...] = x_ref[...] + _dot(merged.astype(BF16), wo_ref[...])

    row = lambda c: pl.BlockSpec((tm, D), lambda i, c=c: (i, c))
    return pl.pallas_call(
        body, name=f"merge_fwd_{li}", grid=(T // tm,),
        in_specs=[row(0), row(0), row(0), row(gcol), row(gcol + 1), row(gcol + 2),
                  pl.BlockSpec((3, D), lambda i: (0, 0)),
                  pl.BlockSpec((3, D, D), lambda i: (0, 0, 0)),
                  pl.BlockSpec((D, D), lambda i: (0, 0)),
                  row(0)],
        out_specs=[row(0), pl.BlockSpec((3, tm, D), lambda i: (0, i, 0)), pl.BlockSpec((3, D, tm), lambda i: (0, 0, i))],
        out_shape=[jax.ShapeDtypeStruct((T, D), F32), jax.ShapeDtypeStruct((3, T, D), BF16),
                   jax.ShapeDtypeStruct((3, D, T), BF16)],
        compiler_params=_cparams(("parallel",), VMEM_LIMIT),
    )(ya, yb, yc, proj, proj, proj, gbias, wp, wout, x2)


def _merge_bwd(dxo16, wout, wp, br, proj, gbias, ob, oc, li):
    T = dxo16.shape[0]
    tm = min(256, T)
    gcol = OFF_G // D

    def body(dx_ref, wo_ref, wp_ref, br_ref, g0_ref, g1_ref, g2_ref, gb_ref, ob_ref, oc_ref, zb_ref, zc_ref,
             dbr_ref, dg_ref, mt_ref, dgb_ref, dya_ref, dob_ref, dzb_ref, doc_ref, dzc_ref):
        @pl.when(pl.program_id(0) == 0)
        def _():
            dgb_ref[...] = jnp.zeros_like(dgb_ref)

        dm = _dot_nt(dx_ref[...], wo_ref[...])
        merged = jnp.zeros((tm, D), F32)
        dys = []
        for i, g_ref in enumerate((g0_ref, g1_ref, g2_ref)):
            b = br_ref[i].astype(F32)
            gate = _sigmoid(g_ref[...].astype(F32) + gb_ref[i:i + 1, :])
            merged = merged + gate * b
            dbr = (dm * gate).astype(BF16)
            dbr_ref[i] = dbr
            dgate = dm * b * gate * (1.0 - gate)
            dg_ref[:, D * i:D * (i + 1)] = dgate.astype(BF16)
            dgb_ref[i:i + 1, :] += jnp.sum(dgate, axis=0, keepdims=True)
            dys.append(_dot_nt(dbr, wp_ref[i]))
        mt_ref[...] = merged.astype(BF16).T
        dya_ref[...] = dys[0].astype(BF16)
        for dy, o_ref, z_ref, do_ref, dz_ref in ((dys[1], ob_ref, zb_ref, dob_ref, dzb_ref),
                                                 (dys[2], oc_ref, zc_ref, doc_ref, dzc_ref)):
            z = z_ref[...].astype(F32)
            sg = _sigmoid(z)
            do_ref[...] = (dy * z * sg).astype(BF16)
            dz_ref[...] = (dy * o_ref[...].astype(F32) * sg * (1.0 + z * (1.0 - sg))).astype(BF16)

    row = lambda c: pl.BlockSpec((tm, D), lambda i, c=c: (i, c))
    sds = jax.ShapeDtypeStruct((T, D), BF16)
    return pl.pallas_call(
        body, name=f"merge_bwd_{li}", grid=(T // tm,),
        in_specs=[row(0), pl.BlockSpec((D, D), lambda i: (0, 0)), pl.BlockSpec((3, D, D), lambda i: (0, 0, 0)),
                  pl.BlockSpec((3, tm, D), lambda i: (0, i, 0)),
                  row(gcol), row(gcol + 1), row(gcol + 2),
                  pl.BlockSpec((3, D), lambda i: (0, 0)),
                  row(0), row(0), row(OFF_BZ // D), row(OFF_CZ // D)],
        out_specs=[pl.BlockSpec((3, tm, D), lambda i: (0, i, 0)),
                   pl.BlockSpec((tm, 3 * D), lambda i: (i, 0)),
                   pl.BlockSpec((D, tm), lambda i: (0, i)),
                   pl.BlockSpec((8, D), lambda i: (0, 0)),
                   row(0), row(0), row(0), row(0), row(0)],
        out_shape=[jax.ShapeDtypeStruct((3, T, D), BF16), jax.ShapeDtypeStruct((T, 3 * D), BF16),
                   jax.ShapeDtypeStruct((D, T), BF16), jax.ShapeDtypeStruct((8, D), F32), sds, sds, sds, sds, sds],
        compiler_params=_cparams(("arbitrary",), VMEM_LIMIT),
    )(dxo16, wout, wp, br, proj, proj, proj, gbias, ob, oc, proj, proj)


def _final_loss(x2, tgt, fw):
    T = x2.shape[0]
    tm = min(512, T)
    ni = T // tm

    def body(x_ref, t_ref, w_ref, dx_ref, dx16_ref, st_ref):
        i = pl.program_id(0)

        @pl.when(i == 0)
        def _():
            st_ref[...] = jnp.zeros_like(st_ref)

        x = x_ref[...]
        r = lax.rsqrt(jnp.mean(x * x, axis=-1, keepdims=True) + EPS)
        xh = x * r
        err = xh * w_ref[...] - t_ref[...]
        dy = err * (1.0 / D)
        g = dy * w_ref[...]
        dx = r * g - x * (r * r * r) * jnp.mean(g * x, axis=-1, keepdims=True)
        dx_ref[...] = dx
        dx16_ref[...] = dx.astype(BF16)
        st_ref[0:1, :] += jnp.sum(dy * xh, axis=0, keepdims=True)
        st_ref[1:2, :] += jnp.sum(err * err, axis=0, keepdims=True)

        @pl.when(i == ni - 1)
        def _():
            tot = jnp.sum(st_ref[1:2, :], axis=1, keepdims=True) * (0.5 / D)
            st_ref[2:3, :] = jnp.broadcast_to(tot, (1, D))

    row = pl.BlockSpec((tm, D), lambda i: (i, 0))
    return pl.pallas_call(
        body, name="final_loss", grid=(ni,),
        in_specs=[row, row, pl.BlockSpec((1, D), lambda i: (0, 0))],
        out_specs=[row, row, pl.BlockSpec((8, D), lambda i: (0, 0))],
        out_shape=[jax.ShapeDtypeStruct((T, D), F32), jax.ShapeDtypeStruct((T, D), BF16),
                   jax.ShapeDtypeStruct((8, D), F32)],
        compiler_params=_cparams(("arbitrary",), VMEM_LIMIT),
    )(x2, tgt, fw)


def _fox_cum(ps, fb_row, S, li):
    T = ps.shape[0]
    blk = min(4 * LCH, S)
    nb, nsub = S // blk, blk // LCH

    def body(ps_ref, fb_ref, cum_ref, carry):
        @pl.when(pl.program_id(1) == 0)
        def _():
            carry[...] = jnp.zeros_like(carry)

        r = lax.broadcasted_iota(jnp.int32, (LCH, LCH), 0)
        c = lax.broadcasted_iota(jnp.int32, (LCH, LCH), 1)
        tri = (r >= c).astype(F32)
        run = carry[0:1, :]
        for u in range(nsub):
            rows = slice(LCH * u, LCH * (u + 1))
            logf = -_softplus(-(ps_ref[rows, :] + fb_ref[...]))
            cum = _dot_hi(tri, logf) + run
            cum_ref[rows, :] = cum
            run = cum[LCH - 1:LCH, :]
        carry[0:1, :] = run

    return pl.pallas_call(
        body, name=f"fox_cum_{li}", grid=(T // S, nb),
        in_specs=[pl.BlockSpec((blk, LANES), lambda b, i: (b * nb + i, 0)),
                  pl.BlockSpec((1, LANES), lambda b, i: (0, 0))],
        out_specs=pl.BlockSpec((blk, LANES), lambda b, i: (b * nb + i, 0)),
        out_shape=jax.ShapeDtypeStruct((T, LANES), F32),
        scratch_shapes=[pltpu.VMEM((8, LANES), F32)],
        compiler_params=_cparams(("arbitrary", "arbitrary")),
    )(ps, fb_row)


def _fox_cum_bwd(dcum, ps, fb_row, S, li):
    T = ps.shape[0]
    rows_blk = min(4 * LCH, S)
    nb, nsub = S // rows_blk, rows_blk // LCH

    def body(dc_ref, ps_ref, fb_ref, df_ref, dfb_ref, carry):
        b, i = pl.program_id(0), pl.program_id(1)

        @pl.when(i == 0)
        def _():
            carry[...] = jnp.zeros_like(carry)

        @pl.when(jnp.logical_and(b == 0, i == 0))
        def _():
            dfb_ref[...] = jnp.zeros_like(dfb_ref)

        r = lax.broadcasted_iota(jnp.int32, (LCH, LCH), 0)
        c = lax.broadcasted_iota(jnp.int32, (LCH, LCH), 1)
        tri = (c >= r).astype(F32)
        lane = _lane_iota()
        live = jnp.logical_and(lane >= NH, lane < 2 * NH)
        run = carry[0:1, :]
        dfb = jnp.zeros((1, LANES), F32)
        for u in reversed(range(nsub)):
            rows = slice(LCH * u, LCH * (u + 1))
            dc = dc_ref[rows, :]
            dlogf = _dot_hi(tri, dc) + run
            run = run + jnp.sum(dc, axis=0, keepdims=True)
            df = jnp.where(live, dlogf * _sigmoid(-(ps_ref[rows, :] + fb_ref[...])), 0.0)
            df_ref[rows, :] = df
            dfb = dfb + jnp.sum(df, axis=0, keepdims=True)
        carry[0:1, :] = run
        dfb_ref[0:1, :] += dfb

    blk = pl.BlockSpec((rows_blk, LANES), lambda b, i: (b * nb + nb - 1 - i, 0))
    return pl.pallas_call(
        body, name=f"fox_cum_bwd_{li}", grid=(T // S, nb),
        in_specs=[blk, blk, pl.BlockSpec((1, LANES), lambda b, i: (0, 0))],
        out_specs=[blk, pl.BlockSpec((8, LANES), lambda b, i: (0, 0))],
        out_shape=[jax.ShapeDtypeStruct((T, LANES), F32), jax.ShapeDtypeStruct((8, LANES), F32)],
        scratch_shapes=[pltpu.VMEM((8, LANES), F32)],
        compiler_params=_cparams(("arbitrary", "arbitrary")),
    )(dcum, ps, fb_row)


def _fox_blocks(S):
    bq = min(512, S)
    return bq, S // bq


def _split3(c):
    hi = c.astype(BF16).astype(F32)
    r = c - hi
    mid = r.astype(BF16).astype(F32)
    return hi, mid, (r - mid).astype(BF16).astype(F32)


def _augment(x, parts, key_side, hh):
    lane = _lane_iota()
    b0 = HD if hh == 0 else 0
    p0, o0 = (b0 + 3, b0) if key_side else (b0, b0 + 3)
    out = jnp.where(jnp.logical_and(lane >= o0, lane < o0 + 3), 1.0, x)
    for t in range(3):
        out = jnp.where(lane == p0 + t, parts[t], out)
    return out.astype(BF16)


def _row_to_col(row_ref_slice, col_scr, hh, S):
    step = min(4 * LANES, S)
    for t in range(S // step):
        r = row_ref_slice[:, step * t:step * (t + 1)]
        col_scr[hh, step * t:step * (t + 1), :] = jnp.broadcast_to(r, (LANES, step)).T[:, 0:1]


def _col_to_row(col):
    return jnp.broadcast_to(col, (col.shape[0], LANES)).T[0:1, :]


def _fox_fwd(proj, cum_row, S, li, comm=None):
    T = proj.shape[0]
    B = T // S
    bq, nq = _fox_blocks(S)
    qc, kc, vc, zc = OFF_CQ // LANES, OFF_CK // LANES, OFF_CV // LANES, OFF_CZ // LANES

    def body(q_ref, k_ref, v_ref, z_ref, cr_ref, y_ref, o_ref, lse_ref, kaug, cc_ref, vaug):
        i = pl.program_id(2)
        m0 = _lane_iota() < HD

        @pl.when(i == 0)
        def _():
            kf = k_ref[...].astype(F32)
            vf = v_ref[...]
            for hh in range(2):
                _row_to_col(cr_ref[0, hh], cc_ref, hh, S)
                kaug[hh] = _augment(kf, _split3(-cc_ref[hh]), True, hh)
                vaug[hh] = jnp.where(m0 if hh == 0 else jnp.logical_not(m0), vf, jnp.ones_like(vf))

        q2 = q_ref[...].astype(F32) * SCALE
        rows_q = pl.ds(pl.multiple_of(i * bq, bq), bq)
        row = lax.broadcasted_iota(jnp.int32, (bq, bq), 0)
        col = lax.broadcasted_iota(jnp.int32, (bq, bq), 1)
        qa = [_augment(jnp.where(m0 if hh == 0 else jnp.logical_not(m0), q2, 0.0),
                       _split3(cc_ref[hh, rows_q, :]), False, hh) for hh in range(2)]

        def step(j, carry, masked):
            start = pl.multiple_of(j * bq, bq)
            out = []
            for hh in range(2):
                m, acc = carry[2 * hh:2 * hh + 2]
                s = _dot_nt(qa[hh], kaug[hh, pl.ds(start, bq), :])
                if masked:
                    s = jnp.where(row >= col, s, NEG)
                mn = jnp.maximum(m, jnp.max(s, axis=1, keepdims=True))
                p = jnp.exp(s - mn)
                out += [mn, jnp.exp(m - mn) * acc + _dot(p.astype(BF16), vaug[hh, pl.ds(start, bq), :])]
            return tuple(out)

        init = (jnp.full((bq, 1), NEG, F32), jnp.zeros((bq, LANES), F32)) * 2
        carry = step(i, lax.fori_loop(0, i, functools.partial(step, masked=False), init), True)
        outs = []
        for hh in range(2):
            m, acc = carry[2 * hh:2 * hh + 2]
            other = HD if hh == 0 else 0
            l = acc[:, other:other + 1]
            outs.append(acc / l)
            lse_ref[0, hh] = m + jnp.log(l)
        o2 = jnp.where(m0, outs[0], outs[1])
        z = z_ref[...].astype(F32)
        o_ref[...] = o2.astype(BF16)
        y_ref[...] = (o2 * z * _sigmoid(z)).astype(BF16)

    qblk = lambda c: pl.BlockSpec((bq, LANES), lambda b, p, i, c=c: (b * nq + i, c + p))
    sblk = lambda c: pl.BlockSpec((S, LANES), lambda b, p, i, c=c: (b, c + p))
    return _hosted_call(
        body, comm, f"fox_fwd_{li}", (B, NH // 2, nq),
        in_specs=[qblk(qc), sblk(kc), sblk(vc), qblk(zc),
                  pl.BlockSpec((1, 2, 1, S), lambda b, p, i: (b, p, 0, 0))],
        out_specs=[qblk(0), qblk(0), pl.BlockSpec((1, 2, bq, 1), lambda b, p, i: (b, p, i, 0))],
        out_shape=[jax.ShapeDtypeStruct((T, D), BF16), jax.ShapeDtypeStruct((T, D), BF16),
                   jax.ShapeDtypeStruct((B, NH, S, 1), F32)],
        scratch=[pltpu.VMEM((2, S, LANES), BF16), pltpu.VMEM((2, S, 1), F32), pltpu.VMEM((2, S, LANES), BF16)],
        dims=("parallel", "parallel", "arbitrary"), operands=(proj, proj, proj, proj, cum_row))


def _fox_bwd(proj, do, o, cum_row, lse, S, li, comm=None):
    T = proj.shape[0]
    B = T // S
    bq, nq = _fox_blocks(S)
    qc, kc, vc = OFF_CQ // LANES, OFF_CK // LANES, OFF_CV // LANES

    def body(q_ref, k_ref, v_ref, do_ref, o_ref, cr_ref, lse_ref, dq_ref, dk_ref, dv_ref, dc_ref, dr_ref,
             dq_scr, dr_scr, qaug, cc_ref):
        j = pl.program_id(2)
        m0 = _lane_iota() < HD

        @pl.when(j == 0)
        def _():
            dq_scr[...] = jnp.zeros_like(dq_scr)
            dr_scr[...] = jnp.zeros_like(dr_scr)
            qf = q_ref[...].astype(F32) * SCALE
            for hh in range(2):
                sel = m0 if hh == 0 else jnp.logical_not(m0)
                _row_to_col(cr_ref[0, hh], cc_ref, hh, S)
                qaug[hh] = _augment(jnp.where(sel, qf, 0.0), _split3(cc_ref[hh] - lse_ref[0, hh]), False, hh)

        k2 = k_ref[...]
        v2 = v_ref[...]
        zk = jnp.zeros_like(k2)
        kh = (jnp.where(m0, k2, zk), jnp.where(m0, zk, k2))
        kf = k2.astype(F32)
        rows_k = pl.ds(pl.multiple_of(j * bq, bq), bq)
        ka = [_augment(kf, _split3(-cc_ref[hh, rows_k, :]), True, hh) for hh in range(2)]
        row = lax.broadcasted_iota(jnp.int32, (bq, bq), 0)
        col = lax.broadcasted_iota(jnp.int32, (bq, bq), 1)

        def step(i, carry, masked):
            dk, dv, dc0, dc1 = carry
            dcs = [dc0, dc1]
            start = pl.multiple_of(i * bq, bq)
            q2 = q_ref[pl.ds(start, bq), :]
            do2 = do_ref[pl.ds(start, bq), :]
            prod = do2.astype(F32) * o_ref[pl.ds(start, bq), :].astype(F32)
            zq = jnp.zeros_like(q2)
            dq = jnp.zeros((bq, LANES), F32)
            for hh in range(2):
                sel = m0 if hh == 0 else jnp.logical_not(m0)
                qh = jnp.where(sel, q2, zq)
                doh = jnp.where(sel, do2, zq)
                delta = _head_sum(prod, hh)
                s = _dot_nt(qaug[hh, pl.ds(start, bq), :], ka[hh])
                if masked:
                    s = jnp.where(row >= col, s, NEG)
                p = jnp.exp(s)
                dp = _dot_nt(doh, v2)
                ds = p * (dp - delta)
                dcs[hh] = dcs[hh] - jnp.sum(ds, axis=0, keepdims=True)
                dr_scr[hh, pl.ds(start, bq), :] += jnp.sum(ds, axis=1, keepdims=True)
                dsb = ds.astype(BF16)
                dv = dv + _dot_tn(p.astype(BF16), doh)
                dk = dk + _dot_tn(dsb, qh)
                dq = dq + _dot(dsb, kh[hh])
            dq_scr[pl.ds(start, bq), :] += dq
            return dk, dv, dcs[0], dcs[1]

        zero = jnp.zeros((bq, LANES), F32)
        zrow = jnp.zeros((1, bq), F32)
        carry = step(j, (zero, zero, zrow, zrow), True)
        dk, dv, dc0, dc1 = lax.fori_loop(j + 1, nq, functools.partial(step, masked=False), carry)
        dk_ref[...] = (dk * SCALE).astype(BF16)
        dv_ref[...] = dv.astype(BF16)
        dc_ref[0, 0, 0] = dc0
        dc_ref[0, 1, 0] = dc1

        @pl.when(j == nq - 1)
        def _():
            dq_ref[...] = (dq_scr[...] * SCALE).astype(BF16)
            step_r = min(4 * LANES, S)
            for hh in range(2):
                for t in range(S // step_r):
                    dr_ref[0, hh, :, step_r * t:step_r * (t + 1)] = _col_to_row(dr_scr[hh, step_r * t:step_r * (t + 1), :])

    sblk = lambda c: pl.BlockSpec((S, LANES), lambda b, p, j, c=c: (b, c + p))
    kblk = lambda c: pl.BlockSpec((bq, LANES), lambda b, p, j, c=c: (b * nq + j, c + p))
    col_spec = pl.BlockSpec((1, 2, S, 1), lambda b, p, j: (b, p, 0, 0))
    row_spec = pl.BlockSpec((1, 2, 1, S), lambda b, p, j: (b, p, 0, 0))
    return _hosted_call(
        body, comm, f"fox_bwd_{li}", (B, NH // 2, nq),
        in_specs=[sblk(qc), kblk(kc), kblk(vc), sblk(0), sblk(0), row_spec, col_spec],
        out_specs=[sblk(0), kblk(0), kblk(0), pl.BlockSpec((1, 2, 1, 1, bq), lambda b, p, j: (b, p, j, 0, 0)),
                   row_spec],
        out_shape=[jax.ShapeDtypeStruct((T, D), BF16), jax.ShapeDtypeStruct((T, D), BF16),
                   jax.ShapeDtypeStruct((T, D), BF16), jax.ShapeDtypeStruct((B, NH, nq, 1, bq), F32),
                   jax.ShapeDtypeStruct((B, NH, 1, S), F32)],
        scratch=[pltpu.VMEM((S, LANES), F32), pltpu.VMEM((2, S, 1), F32), pltpu.VMEM((2, S, LANES), BF16),
                 pltpu.VMEM((2, S, 1), F32)],
        dims=("parallel", "parallel", "arbitrary"), operands=(proj, proj, proj, do, o, cum_row, lse))


def _swa_blocks(S):
    bq = min(512, S)
    return bq, S // bq, bq // LCH


def _dup_head(xw, kvl):
    m0 = _lane_iota() < HD
    a = jnp.where(m0 if kvl == 0 else jnp.logical_not(m0), xw, 0.0)
    return (a + pltpu.roll(a, HD, 1)).astype(BF16)


def _band(same_block):
    r = lax.broadcasted_iota(jnp.int32, (LCH, LCH), 0)
    c = lax.broadcasted_iota(jnp.int32, (LCH, LCH), 1)
    return (c <= r) if same_block else (c > r)


def _stack_heads(ref, rows, kvl):
    m0 = _lane_iota() < HD
    parts = []
    for ch in (2 * kvl, 2 * kvl + 1):
        x = ref[rows, LANES * ch:LANES * (ch + 1)]
        parts += [jnp.where(m0, x, jnp.zeros_like(x)), jnp.where(m0, jnp.zeros_like(x), x)]
    return jnp.concatenate(parts, axis=0)


def _stack_delta(do_ref, o_ref, rows, kvl, scale=None):
    parts = []
    for ch in (2 * kvl, 2 * kvl + 1):
        lanes = slice(LANES * ch, LANES * (ch + 1))
        prod = do_ref[rows, lanes].astype(F32) * o_ref[rows, lanes].astype(F32)
        parts += [_head_sum(prod, 0), _head_sum(prod, 1)]
    out = jnp.concatenate(parts, axis=0)
    return out if scale is None else out * scale


def _stack_cols(ref, rows, kvl):
    return jnp.concatenate([ref[0, 4 * kvl + t, rows, :] for t in range(4)], axis=0)


def _swa_fwd(proj, sinks, S, li):
    T = proj.shape[0]
    B = T // S
    bq, nq, nsub = _swa_blocks(S)
    nrow = S // LCH
    qc, zc, kc, vc = OFF_BQ // 512, OFF_BZ // 512, OFF_BK // LANES, OFF_BV // LANES

    def body(sk_ref, q_ref, z_ref, kp_ref, kc_ref, vp_ref, vc_ref, y_ref, o_ref, lse_ref):
        c, i = pl.program_id(0), pl.program_id(2)
        m0 = _lane_iota() < HD
        kw = jnp.concatenate([kp_ref[...].astype(F32), kc_ref[...].astype(F32)], axis=0)
        vw = jnp.concatenate([vp_ref[...].astype(F32), vc_ref[...].astype(F32)], axis=0)
        kd = (_dup_head(kw, 0), _dup_head(kw, 1))
        vd = (_dup_head(vw, 0), _dup_head(vw, 1))
        valid = jnp.concatenate([_band(False), _band(True)], axis=1)
        col = lax.broadcasted_iota(jnp.int32, (LCH, 2 * LCH), 1)
        valid_first = jnp.logical_and(valid, jnp.logical_or(col >= LCH, i > 0))
        valid4 = jnp.concatenate([valid] * 4, axis=0)
        valid4_first = jnp.concatenate([valid_first] * 4, axis=0)
        for r in range(nsub):
            rows = slice(LCH * r, LCH * (r + 1))
            msk = valid4_first if r == 0 else valid4
            for kvl in range(2):
                kwin = kd[kvl][LCH * r:LCH * (r + 2)]
                vwin = vd[kvl][LCH * r:LCH * (r + 2)]
                qs = _stack_heads(q_ref, rows, kvl)
                sink = jnp.concatenate([jnp.full((LCH, 1), sk_ref[8 * c + 4 * kvl + t], F32) for t in range(4)], axis=0)
                s = jnp.where(msk, _dot_nt(qs, kwin) * SCALE, NEG)
                m = jnp.maximum(jnp.max(s, axis=1, keepdims=True), sink)
                p = jnp.exp(s - m)
                l = jnp.sum(p, axis=1, keepdims=True) + jnp.exp(sink - m)
                os_ = _dot(p.astype(BF16), vwin) / l
                lse = m + jnp.log(l)
                for t in range(4):
                    lse_ref[0, 4 * kvl + t, rows, :] = lse[LCH * t:LCH * (t + 1)]
                for u in range(2):
                    lanes = slice(LANES * (2 * kvl + u), LANES * (2 * kvl + u + 1))
                    o2 = jnp.where(m0, os_[LCH * 2 * u:LCH * (2 * u + 1)], os_[LCH * (2 * u + 1):LCH * (2 * u + 2)])
                    z = z_ref[rows, lanes].astype(F32)
                    o_ref[rows, lanes] = o2.astype(BF16)
                    y_ref[rows, lanes] = (o2 * z * _sigmoid(z)).astype(BF16)

    wide = lambda cc: pl.BlockSpec((bq, 512), lambda c, b, i, cc=cc: (b * nq + i, cc + c))
    cur = lambda cc: pl.BlockSpec((bq, LANES), lambda c, b, i, cc=cc: (b * nq + i, cc + c))
    prev = lambda cc: pl.BlockSpec((LCH, LANES), lambda c, b, i, cc=cc: (b * nrow + jnp.maximum(i * nsub - 1, 0), cc + c))
    return pl.pallas_call(
        body, name=f"swa_fwd_{li}", grid=(2, B, nq),
        in_specs=[pl.BlockSpec(memory_space=pltpu.SMEM), wide(qc), wide(zc), prev(kc), cur(kc), prev(vc), cur(vc)],
        out_specs=[wide(0), wide(0), pl.BlockSpec((1, 8, bq, 1), lambda c, b, i: (b, c, i, 0))],
        out_shape=[jax.ShapeDtypeStruct((T, D), BF16), jax.ShapeDtypeStruct((T, D), BF16),
                   jax.ShapeDtypeStruct((B, NH, S, 1), F32)],
        compiler_params=_cparams(("parallel", "parallel", "parallel"), VMEM_LIMIT),
    )(sinks, proj, proj, proj, proj, proj, proj)


def _swa_bwd_dq(proj, do, o, lse, sinks, cos128, sin128, S, li):
    T = proj.shape[0]
    B = T // S
    bq, nq, nsub = _swa_blocks(S)
    nrow = S // LCH
    qc, kc, vc = OFF_BQ // 512, OFF_BK // LANES, OFF_BV // LANES

    def body(sk_ref, q_ref, do_ref, o_ref, lse_ref, kp_ref, kc_ref, vp_ref, vc_ref, cos_ref, sin_ref, dq_ref, dsk_ref):
        c, b, i = pl.program_id(0), pl.program_id(1), pl.program_id(2)

        @pl.when(jnp.logical_and(b == 0, i == 0))
        def _():
            dsk_ref[...] = jnp.zeros_like(dsk_ref)

        m0 = _lane_iota() < HD
        kw = jnp.concatenate([kp_ref[...].astype(F32), kc_ref[...].astype(F32)], axis=0)
        vw = jnp.concatenate([vp_ref[...].astype(F32), vc_ref[...].astype(F32)], axis=0)
        kd = (_dup_head(kw, 0), _dup_head(kw, 1))
        vd = (_dup_head(vw, 0), _dup_head(vw, 1))
        valid = jnp.concatenate([_band(False), _band(True)], axis=1)
        col = lax.broadcasted_iota(jnp.int32, (LCH, 2 * LCH), 1)
        valid_first = jnp.logical_and(valid, jnp.logical_or(col >= LCH, i > 0))
        dsk = [jnp.zeros((1, 1), F32) for _ in range(8)]
        valid4 = jnp.concatenate([valid] * 4, axis=0)
        valid4_first = jnp.concatenate([valid_first] * 4, axis=0)
        for r in range(nsub):
            rows = slice(LCH * r, LCH * (r + 1))
            msk = valid4_first if r == 0 else valid4
            for kvl in range(2):
                kwin = kd[kvl][LCH * r:LCH * (r + 2)]
                vwin = vd[kvl][LCH * r:LCH * (r + 2)]
                qs = _stack_heads(q_ref, rows, kvl)
                dos = _stack_heads(do_ref, rows, kvl)
                delta = _stack_delta(do_ref, o_ref, rows, kvl)
                lse = _stack_cols(lse_ref, rows, kvl)
                sink = jnp.concatenate([jnp.full((LCH, 1), sk_ref[8 * c + 4 * kvl + t], F32) for t in range(4)], axis=0)
                s = jnp.where(msk, _dot_nt(qs, kwin) * SCALE, NEG)
                p = jnp.exp(s - lse)
                ds = p * (_dot_nt(dos, vwin) - delta)
                dqs = _dot(ds.astype(BF16), kwin) * SCALE
                dsink = jnp.exp(sink - lse) * delta
                for t in range(4):
                    hl = 4 * kvl + t
                    dsk[hl] = dsk[hl] - jnp.sum(dsink[LCH * t:LCH * (t + 1)], axis=0, keepdims=True)
                for u in range(2):
                    lanes = slice(LANES * (2 * kvl + u), LANES * (2 * kvl + u + 1))
                    dq2 = jnp.where(m0, dqs[LCH * 2 * u:LCH * (2 * u + 1)], dqs[LCH * (2 * u + 1):LCH * (2 * u + 2)])
                    dq2 = dq2 * cos_ref[rows, :] - _rot_half(dq2) * sin_ref[rows, :]
                    dq_ref[rows, lanes] = dq2.astype(BF16)
        for hl in range(8):
            dsk_ref[0, hl:hl + 1, :] += jnp.broadcast_to(dsk[hl], (1, LANES))

    wide = lambda cc: pl.BlockSpec((bq, 512), lambda c, b, i, cc=cc: (b * nq + i, cc + c))
    cur = lambda cc: pl.BlockSpec((bq, LANES), lambda c, b, i, cc=cc: (b * nq + i, cc + c))
    prev = lambda cc: pl.BlockSpec((LCH, LANES), lambda c, b, i, cc=cc: (b * nrow + jnp.maximum(i * nsub - 1, 0), cc + c))
    pos = pl.BlockSpec((bq, LANES), lambda c, b, i: (i, 0))
    return pl.pallas_call(
        body, name=f"swa_bwd_dq_{li}", grid=(2, B, nq),
        in_specs=[pl.BlockSpec(memory_space=pltpu.SMEM), wide(qc), wide(0), wide(0),
                  pl.BlockSpec((1, 8, bq, 1), lambda c, b, i: (b, c, i, 0)),
                  prev(kc), cur(kc), prev(vc), cur(vc), pos, pos],
        out_specs=[wide(0), pl.BlockSpec((1, 8, LANES), lambda c, b, i: (c, 0, 0))],
        out_shape=[jax.ShapeDtypeStruct((T, D), BF16), jax.ShapeDtypeStruct((2, 8, LANES), F32)],
        compiler_params=_cparams(("arbitrary", "arbitrary", "arbitrary"), VMEM_LIMIT),
    )(sinks, proj, do, o, lse, proj, proj, proj, proj, cos128, sin128)


def _swa_bwd_dkv(proj, do, o, lse, cos128, sin128, S, li):
    T = proj.shape[0]
    B = T // S
    bk, nk, nsub = _swa_blocks(S)
    nrow = S // LCH
    qc, kc, vc = OFF_BQ // 512, OFF_BK // LANES, OFF_BV // LANES

    def body(q_ref, qn_ref, do_ref, don_ref, o_ref, on_ref, lse_ref, lsen_ref, k_ref, v_ref, cos_ref, sin_ref,
             dk_ref, dv_ref):
        j = pl.program_id(2)
        m0 = _lane_iota() < HD
        has_next = (j < nk - 1).astype(F32)
        kf = k_ref[...].astype(F32)
        vf = v_ref[...].astype(F32)
        kd = (_dup_head(kf, 0), _dup_head(kf, 1))
        vd = (_dup_head(vf, 0), _dup_head(vf, 1))
        lane = _lane_iota()

        def stat_rows(lse_r, do_r, o_r, rows, scale):
            a_lse = jnp.zeros((rows, LANES), F32)
            a_del = jnp.zeros((rows, LANES), F32)
            for ch in range(4):
                lanes = slice(LANES * ch, LANES * (ch + 1))
                prod = do_r[:, lanes].astype(F32) * o_r[:, lanes].astype(F32)
                for hh in range(2):
                    h = 2 * ch + hh
                    a_lse = jnp.where(lane == h, lse_r[0, h], a_lse)
                    a_del = jnp.where(lane == h, _head_sum(prod, hh), a_del)
            if scale is not None:
                a_del = a_del * scale
            return a_lse.T, a_del.T

        lse_t, del_t = stat_rows(lse_ref, do_ref, o_ref, bk, None)
        lsen_t, deln_t = stat_rows(lsen_ref, don_ref, on_ref, LCH, has_next)
        r_ = lax.broadcasted_iota(jnp.int32, (LCH, LCH), 0)
        c_ = lax.broadcasted_iota(jnp.int32, (LCH, LCH), 1)
        masks4 = (jnp.concatenate([r_ <= c_] * 4, axis=1), jnp.concatenate([r_ > c_] * 4, axis=1))
        for kr in range(nsub):
            krows = slice(LCH * kr, LCH * (kr + 1))
            dk = jnp.zeros((LCH, LANES), F32)
            dv = jnp.zeros((LCH, LANES), F32)
            for dq_blk in range(2):
                rq = kr + dq_blk
                nxt = rq == nsub
                qrows = slice(0, LCH) if nxt else slice(LCH * rq, LCH * (rq + 1))
                qr, dor = (qn_ref, don_ref) if nxt else (q_ref, do_ref)
                lt, dt_ = (lsen_t, deln_t) if nxt else (lse_t, del_t)
                for kvl in range(2):
                    qs = _stack_heads(qr, qrows, kvl)
                    dos = _stack_heads(dor, qrows, kvl)
                    if nxt:
                        dos = (dos.astype(F32) * has_next).astype(BF16)
                    lse_row = jnp.concatenate([lt[4 * kvl + t:4 * kvl + t + 1, qrows] for t in range(4)], axis=1)
                    del_row = jnp.concatenate([dt_[4 * kvl + t:4 * kvl + t + 1, qrows] for t in range(4)], axis=1)
                    st = jnp.where(masks4[dq_blk], _dot_nt(kd[kvl][krows], qs) * SCALE, NEG)
                    pt = jnp.exp(st - lse_row)
                    dst = pt * (_dot_nt(vd[kvl][krows], dos) - del_row)
                    dvc = _dot(pt.astype(BF16), dos)
                    dkc = _dot(dst.astype(BF16), qs) * SCALE
                    own = m0 if kvl == 0 else jnp.logical_not(m0)
                    dv = dv + jnp.where(own, dvc + pltpu.roll(dvc, HD, 1), 0.0)
                    dk = dk + jnp.where(own, dkc + pltpu.roll(dkc, HD, 1), 0.0)
            dk = dk * cos_ref[krows, :] - _rot_half(dk) * sin_ref[krows, :]
            dk_ref[krows, :] = dk.astype(BF16)
            dv_ref[krows, :] = dv.astype(BF16)

    wide = lambda cc: pl.BlockSpec((bk, 512), lambda c, b, j, cc=cc: (b * nk + j, cc + c))
    nxt = lambda cc: pl.BlockSpec((LCH, 512), lambda c, b, j, cc=cc: (b * nrow + jnp.minimum((j + 1) * nsub, nrow - 1), cc + c))
    cur = lambda cc: pl.BlockSpec((bk, LANES), lambda c, b, j, cc=cc: (b * nk + j, cc + c))
    pos = pl.BlockSpec((bk, LANES), lambda c, b, j: (j, 0))
    return pl.pallas_call(
        body, name=f"swa_bwd_dkv_{li}", grid=(2, B, nk),
        in_specs=[wide(qc), nxt(qc), wide(0), nxt(0), wide(0), nxt(0),
                  pl.BlockSpec((1, 8, bk, 1), lambda c, b, j: (b, c, j, 0)),
                  pl.BlockSpec((1, 8, LCH, 1), lambda c, b, j: (b, c, jnp.minimum((j + 1) * nsub, nrow - 1), 0)),
                  cur(kc), cur(vc), pos, pos],
        out_specs=[cur(0), cur(0)],
        out_shape=[jax.ShapeDtypeStruct((T, 2 * LANES), BF16), jax.ShapeDtypeStruct((T, 2 * LANES), BF16)],
        compiler_params=_cparams(("parallel", "parallel", "parallel"), VMEM_LIMIT),
    )(proj, proj, do, do, o, o, lse, lse, proj, proj, cos128, sin128)


HALO = 16


def _shift_matrices():
    r = lax.broadcasted_iota(jnp.int32, (3 * LCH, LCH + HALO), 0)
    c = lax.broadcasted_iota(jnp.int32, (3 * LCH, LCH + HALO), 1)
    t, d = r % LCH, r // LCH + 1
    return (c == HALO + t - d).astype(BF16), (c == t + d).astype(BF16)


def _ssm_chunk_pre(prev16, cur16, first, sdn_ref, cw_ref, cb_ref, ps, dtb, alog):
    ext16 = jnp.concatenate([jnp.where(first, jnp.zeros_like(prev16), prev16), cur16], axis=0)
    sh = _dot(sdn_ref[...], ext16)
    pre = cb_ref[...] + cw_ref[3:4, :] * cur16.astype(F32)
    for d in range(1, 4):
        pre = pre + cw_ref[3 - d:4 - d, :] * sh[LCH * (d - 1):LCH * d]
    sg = _sigmoid(pre)
    dt = _softplus(ps + dtb)
    a = -jnp.exp(alog)
    r = lax.broadcasted_iota(jnp.int32, (LCH, LCH), 0)
    c = lax.broadcasted_iota(jnp.int32, (LCH, LCH), 1)
    acum = _dot_hi((r >= c).astype(F32), dt * a)
    return pre, sg, dt, a, acum, sh


def _expand_matrix():
    r = lax.broadcasted_iota(jnp.int32, (3 * LANES, D), 0)
    c = lax.broadcasted_iota(jnp.int32, (3 * LANES, D), 1)
    return ((r % LANES) == c // HD).astype(BF16)


def _expand_heads(v, ex_ref):
    return _dot(jnp.concatenate(_split3(v), axis=1).astype(BF16), ex_ref[...])


def _decay(acum, acum_t, h):
    r = lax.broadcasted_iota(jnp.int32, (LCH, LCH), 0)
    c = lax.broadcasted_iota(jnp.int32, (LCH, LCH), 1)
    causal = r >= c
    seg = acum[:, h:h + 1] - acum_t[h:h + 1, :]
    return jnp.where(causal, jnp.exp(jnp.where(causal, seg, 0.0)), 0.0)


def _ssm_pair_fwd(p, x, dt_x, acum, acum_t, e_x, w_x, cd, cb_g, b_g, c_g, hprev, dsk_ref):
    m0 = _lane_iota() < HD
    lanes = slice(LANES * p, LANES * (p + 1))
    x2 = x[:, lanes]
    dt2 = dt_x[:, lanes]
    xdt2 = x2 * dt2
    xdtb = xdt2.astype(BF16)
    lms, ms, yds = [], [], []
    for hh in range(2):
        lm = _decay(acum, acum_t, 2 * p + hh)
        mm = cb_g * lm
        lms.append(lm)
        ms.append(mm)
        yds.append(_dot(mm.astype(BF16), xdtb))
    yd2 = jnp.where(m0, yds[0], yds[1])
    w2 = w_x[:, lanes]
    xw = (xdt2 * w2).astype(BF16)
    s2 = _dot_tn(xw, b_g)
    z2 = _dot_nt(c_g, hprev.astype(BF16))
    e2 = e_x[:, lanes]
    rowsel = lax.broadcasted_iota(jnp.int32, (LANES, 1), 0) < HD
    cdcol = jnp.where(rowsel, cd[:, 2 * p:2 * p + 1], cd[:, 2 * p + 1:2 * p + 2])
    y2 = yd2 + z2 * e2 + dsk_ref[:, lanes] * x2
    return dict(x2=x2, dt2=dt2, xdt2=xdt2, xdtb=xdtb, lms=lms, ms=ms, yd2=yd2, w2=w2, xw=xw, s2=s2, z2=z2, e2=e2,
                cdcol=cdcol, y2=y2)


def _ssm_specs(S, rev):
    nc = S // LCH
    ch = (lambda c: nc - 1 - c) if rev else (lambda c: c)
    prev = pl.BlockSpec((HALO, 2 * D), lambda b, c: (jnp.maximum(b * (S // HALO) + ch(c) * (LCH // HALO) - 1, 0), 0))
    cur = pl.BlockSpec((LCH, 2 * D), lambda b, c: (b * nc + ch(c), 0))
    zed = pl.BlockSpec((LCH, D), lambda b, c: (b * nc + ch(c), OFF_AZ // D))
    row = pl.BlockSpec((LCH, D), lambda b, c: (b * nc + ch(c), 0))
    psb = pl.BlockSpec((LCH, LANES), lambda b, c: (b * nc + ch(c), 0))
    hpb = pl.BlockSpec((1, 1, NH // 2, LANES, NST), lambda b, c: (b, ch(c), 0, 0, 0))
    const = lambda r, w: pl.BlockSpec((r, w), lambda b, c: (0, 0))
    return nc, prev, cur, zed, row, psb, hpb, const


def _ssm_fwd(proj, ps, cw, cb, dtb, alog, dsk, nw, S, li):
    T = proj.shape[0]
    B = T // S
    nc, prev, cur, zed, row, psb, hpb, const = _ssm_specs(S, False)

    def body(prev_ref, cur_ref, z_ref, ps_ref, sdn_ref, ex_ref, cw_ref, cb_ref, dtb_ref, alog_ref, dsk_ref, nw_ref,
             ya_ref, hp_ref, h_scr):
        c = pl.program_id(1)

        @pl.when(c == 0)
        def _():
            h_scr[...] = jnp.zeros_like(h_scr)

        pre, sg, dt, a, acum, _ = _ssm_chunk_pre(prev_ref[...], cur_ref[...], c == 0, sdn_ref, cw_ref, cb_ref,
                                                 ps_ref[...], dtb_ref[...], alog_ref[...])
        act = pre * sg
        acum_t = acum.T
        last = acum[LCH - 1:LCH, :]
        cd = jnp.exp(last)
        dt, e_all, w_all = (_expand_heads(v, ex_ref) for v in (dt, jnp.exp(acum), jnp.exp(last - acum)))
        x = act[:, :D]
        for g in range(NGRP):
            b_g = act[:, D + NST * g:D + NST * (g + 1)].astype(BF16)
            c_g = act[:, D + NGRP * NST + NST * g:D + NGRP * NST + NST * (g + 1)].astype(BF16)
            cb_g = _dot_nt(c_g, b_g)
            ygs = []
            for p in (2 * g, 2 * g + 1):
                hprev = h_scr[p]
                hp_ref[0, 0, p] = hprev
                f = _ssm_pair_fwd(p, x, dt, acum, acum_t, e_all, w_all, cd, cb_g, b_g, c_g, hprev, dsk_ref)
                h_scr[p] = hprev * f["cdcol"] + f["s2"]
                z2 = z_ref[:, LANES * p:LANES * (p + 1)].astype(F32)
                ygs.append(f["y2"] * z2 * _sigmoid(z2))
            yg = jnp.concatenate(ygs, axis=1)
            r = lax.rsqrt(jnp.mean(yg * yg, axis=1, keepdims=True) + EPS)
            ya_ref[:, 2 * LANES * g:2 * LANES * (g + 1)] = (yg * r * nw_ref[:, 2 * LANES * g:2 * LANES * (g + 1)]).astype(BF16)

    return pl.pallas_call(
        body, name=f"ssm_fwd_{li}", grid=(B, nc),
        in_specs=[prev, cur, zed, psb, const(3 * LCH, LCH + HALO), const(3 * LANES, D), const(4, 2 * D),
                  const(1, 2 * D), const(1, LANES), const(1, LANES), const(1, D), const(1, D)],
        out_specs=[row, hpb],
        out_shape=[jax.ShapeDtypeStruct((T, D), BF16), jax.ShapeDtypeStruct((B, nc, NH // 2, LANES, NST), F32)],
        scratch_shapes=[pltpu.VMEM((NH // 2, LANES, NST), F32)],
        compiler_params=_cparams(("arbitrary", "arbitrary"), VMEM_LIMIT),
    )(proj, proj, proj, ps, _shift_matrices()[0], _expand_matrix(), cw, cb, dtb, alog, dsk, nw)


def _ssm_bwd(proj, ps, hp, dya, cw, cb, dtb, alog, dsk, nw, S, li, comm=None):
    T = proj.shape[0]
    B = T // S
    nc, prev, cur, zed, row, psb, hpb, const = _ssm_specs(S, True)

    def body(prev_ref, cur_ref, z_ref, ps_ref, hp_ref, dy_ref, sdn_ref, sup_ref, ex_ref, cw_ref, cb_ref, dtb_ref,
             alog_ref, dsk_ref, nw_ref, dxbc_ref, dz_ref, dps_ref, pgw_ref, pg1_ref, pgh_ref, dh_scr, dhead, dact):
        b, cc = pl.program_id(0), pl.program_id(1)
        c = nc - 1 - cc

        @pl.when(jnp.logical_and(b == 0, cc == 0))
        def _():
            pgw_ref[...] = jnp.zeros_like(pgw_ref)
            pg1_ref[...] = jnp.zeros_like(pg1_ref)
            pgh_ref[...] = jnp.zeros_like(pgh_ref)

        @pl.when(cc == 0)
        def _():
            dh_scr[...] = jnp.zeros_like(dh_scr)
            dhead[...] = jnp.zeros_like(dhead)

        psv = ps_ref[...]
        cur16 = cur_ref[...]
        pre, sg, dt, a, acum, sh = _ssm_chunk_pre(prev_ref[...], cur16, c == 0, sdn_ref, cw_ref, cb_ref, psv,
                                                  dtb_ref[...], alog_ref[...])
        act = pre * sg
        acum_t = acum.T
        last = acum[LCH - 1:LCH, :]
        w_all = jnp.exp(last - acum)
        cd = jnp.exp(last)
        dt_x, e_x, w_x = (_expand_heads(v, ex_ref) for v in (dt, jnp.exp(acum), w_all))
        x = act[:, :D]
        lane = _lane_iota()
        m0 = lane < HD
        head_row = lax.broadcasted_iota(jnp.int32, (LANES, 1), 0)
        rowsel = head_row < HD
        is_last_row = lax.broadcasted_iota(jnp.int32, (LCH, 1), 0) == LCH - 1
        dacum_all = jnp.zeros((LCH, LANES), F32)
        dacum_t = jnp.zeros((LANES, LCH), F32)
        ddt_all = jnp.zeros((LCH, LANES), F32)
        dd_row = jnp.zeros((1, LANES), F32)
        for g in range(NGRP):
            b_g = act[:, D + NST * g:D + NST * (g + 1)].astype(BF16)
            c_g = act[:, D + NGRP * NST + NST * g:D + NGRP * NST + NST * (g + 1)].astype(BF16)
            cb_g = _dot_nt(c_g, b_g)
            pairs = (2 * g, 2 * g + 1)
            fs, hps, zs, ygs = [], [], [], []
            for p in pairs:
                hprev = hp_ref[0, 0, p]
                f = _ssm_pair_fwd(p, x, dt_x, acum, acum_t, e_x, w_x, cd, cb_g, b_g, c_g, hprev, dsk_ref)
                z2 = z_ref[:, LANES * p:LANES * (p + 1)].astype(F32)
                fs.append(f)
                hps.append(hprev)
                zs.append(z2)
                ygs.append(f["y2"] * z2 * _sigmoid(z2))
            gl = slice(2 * LANES * g, 2 * LANES * (g + 1))
            yg = jnp.concatenate(ygs, axis=1)
            r = lax.rsqrt(jnp.mean(yg * yg, axis=1, keepdims=True) + EPS)
            dyn = dy_ref[:, gl].astype(F32)
            gg = dyn * nw_ref[:, gl]
            dyg = r * gg - yg * (r * r * r) * jnp.mean(gg * yg, axis=1, keepdims=True)
            pg1_ref[0:1, gl] += jnp.sum(dyn * yg * r, axis=0, keepdims=True)
            dg_g = jnp.zeros((LCH, LCH), F32)
            db_g = jnp.zeros((LCH, NST), F32)
            dc_g = jnp.zeros((LCH, NST), F32)
            for idx, p in enumerate(pairs):
                f, hprev, z2 = fs[idx], hps[idx], zs[idx]
                lanes = slice(LANES * p, LANES * (p + 1))
                dyg2 = dyg[:, LANES * idx:LANES * (idx + 1)]
                sgz = _sigmoid(z2)
                dy2 = dyg2 * z2 * sgz
                dz_ref[:, lanes] = (dyg2 * f["y2"] * sgz * (1.0 + z2 * (1.0 - sgz))).astype(BF16)
                x2, dt2, xdt2, xdtb, w2, e2, z2m = f["x2"], f["dt2"], f["xdt2"], f["xdtb"], f["w2"], f["e2"], f["z2"]
                dx2 = dsk_ref[:, lanes] * dy2
                dyx = dy2 * x2
                dxdt2 = jnp.zeros((LCH, LANES), F32)
                diag_cols = []
                for hh in range(2):
                    sel = m0 if hh == 0 else jnp.logical_not(m0)
                    dyb = jnp.where(sel, dy2, 0.0).astype(BF16)
                    dm = _dot_nt(dyb, xdtb)
                    dg_g = dg_g + dm * f["lms"][hh]
                    dxdt2 = dxdt2 + _dot_tn(f["ms"][hh].astype(BF16), dyb)
                    em = dm * f["ms"][hh]
                    diag_cols.append(jnp.sum(em, axis=1, keepdims=True))
                    dacum_t = dacum_t - jnp.where(head_row == 2 * p + hh, jnp.sum(em, axis=0, keepdims=True), 0.0)
                dz2m = dy2 * e2
                t_off = dz2m * z2m
                dc_g = dc_g + _dot(dz2m.astype(BF16), hprev.astype(BF16))
                dhprev = _dot_tn(dz2m.astype(BF16), c_g)
                dhn = dh_scr[p]
                dhnb = dhn.astype(BF16)
                dhprev = dhprev + dhn * f["cdcol"]
                t_h = dhn * hprev
                dxw2 = _dot_nt(b_g, dhnb)
                db_g = db_g + _dot(f["xw"], dhnb)
                dxdt2 = dxdt2 + dxw2 * w2
                t_w = dxw2 * xdt2
                dx2 = dx2 + dxdt2 * dt2
                t_dt = dxdt2 * x2
                for hh in range(2):
                    h = 2 * p + hh
                    onehot = (lane == h).astype(F32)
                    w_col = w_all[:, h:h + 1]
                    dw_col = _head_sum(t_w, hh) * w_col
                    rs = rowsel if hh == 0 else jnp.logical_not(rowsel)
                    dlast = (jnp.sum(jnp.where(rs, t_h, 0.0), keepdims=True) * cd[:, h:h + 1]
                             + jnp.sum(dw_col, keepdims=True))
                    dacum_col = diag_cols[hh] + _head_sum(t_off, hh) - dw_col + jnp.where(is_last_row, dlast, 0.0)
                    dacum_all = dacum_all + dacum_col * onehot
                    ddt_all = ddt_all + _head_sum(t_dt, hh) * onehot
                    sel = m0 if hh == 0 else jnp.logical_not(m0)
                    dd_row = dd_row + jnp.sum(jnp.where(sel, dyx, 0.0), keepdims=True) * onehot
                dh_scr[p] = dhprev
                dact[:, lanes] = dx2
            dgb = dg_g.astype(BF16)
            dc_g = dc_g + _dot(dgb, b_g)
            db_g = db_g + _dot_tn(dgb, c_g)
            dact[:, D + NST * g:D + NST * (g + 1)] = db_g
            dact[:, D + NGRP * NST + NST * g:D + NGRP * NST + NST * (g + 1)] = dc_g
        rr = lax.broadcasted_iota(jnp.int32, (LCH, LCH), 0)
        cc2 = lax.broadcasted_iota(jnp.int32, (LCH, LCH), 1)
        dadt = _dot_hi((cc2 >= rr).astype(F32), dacum_all + dacum_t.T)
        ddt_all = ddt_all + dadt * a
        heads = lane < NH
        da = jnp.sum(dadt * dt, axis=0, keepdims=True)
        dr = jnp.where(heads, ddt_all * _sigmoid(psv + dtb_ref[...]), 0.0)
        dps_ref[...] = dr
        pgh_ref[0:1, :] += jnp.sum(dr, axis=0, keepdims=True)
        pgh_ref[1:2, :] += jnp.where(heads, da * a, 0.0)
        pgh_ref[2:3, :] += dd_row
        dpre = dact[...] * sg * (1.0 + pre * (1.0 - sg))
        extd = jnp.concatenate([dpre, dhead[...]], axis=0)
        hi = extd.astype(BF16)
        lo = (extd - hi.astype(F32)).astype(BF16)
        up = _dot(sup_ref[...], hi) + _dot(sup_ref[...], lo)
        du = cw_ref[3:4, :] * dpre
        pgw_ref[3:4, :] += jnp.sum(dpre * cur16.astype(F32), axis=0, keepdims=True)
        for d in range(1, 4):
            du = du + cw_ref[3 - d:4 - d, :] * up[LCH * (d - 1):LCH * d]
            pgw_ref[3 - d:4 - d, :] += jnp.sum(dpre * sh[LCH * (d - 1):LCH * d], axis=0, keepdims=True)
        pgw_ref[4:5, :] += jnp.sum(dpre, axis=0, keepdims=True)
        dxbc_ref[...] = du.astype(BF16)
        dhead[...] = dpre[0:HALO, :]

    xbc_out = pl.BlockSpec((LCH, 2 * D), lambda b, c: (b * nc + nc - 1 - c, 0))
    acc = lambda w: pl.BlockSpec((8, w), lambda b, c: (0, 0))
    sdn, sup = _shift_matrices()
    return _hosted_call(
        body, comm, f"ssm_bwd_{li}", (B, nc),
        in_specs=[prev, cur, zed, psb, hpb, row, const(3 * LCH, LCH + HALO), const(3 * LCH, LCH + HALO),
                  const(3 * LANES, D), const(4, 2 * D), const(1, 2 * D), const(1, LANES), const(1, LANES),
                  const(1, D), const(1, D)],
        out_specs=[xbc_out, row, psb, acc(2 * D), acc(D), acc(LANES)],
        out_shape=[jax.ShapeDtypeStruct((T, 2 * D), BF16), jax.ShapeDtypeStruct((T, D), BF16),
                   jax.ShapeDtypeStruct((T, LANES), F32), jax.ShapeDtypeStruct((8, 2 * D), F32),
                   jax.ShapeDtypeStruct((8, D), F32), jax.ShapeDtypeStruct((8, LANES), F32)],
        scratch=[pltpu.VMEM((NH // 2, LANES, NST), F32), pltpu.VMEM((HALO, 2 * D), F32),
                 pltpu.VMEM((LCH, 2 * D), F32)],
        dims=("arbitrary", "arbitrary"),
        operands=(proj, proj, proj, ps, hp, dya, sdn, sup, _expand_matrix(), cw, cb, dtb, alog, dsk, nw))


def _lane_row(v, offset):
    return jnp.pad(v.astype(F32), (offset, LANES - offset - v.shape[0]))[None]


def _pack_rows(arrays):
    parts = []
    for a in arrays:
        flat = a.reshape(-1).astype(F32)
        pad = (-flat.shape[0]) % LANES
        parts.append(jnp.pad(flat, (0, pad)))
    flat = jnp.concatenate(parts)
    pad = (-flat.shape[0]) % (8 * LANES)
    return jnp.pad(flat, (0, pad)).reshape(-1, LANES)


def _unpack_rows(pack, shapes):
    flat = pack.reshape(-1)
    out, pos = [], 0
    for shp in shapes:
        n = math.prod(shp)
        out.append(flat[pos:pos + n].reshape(shp))
        pos += n + (-n) % LANES
    return out


def _split_w_in(w):
    main = jnp.concatenate([w[:, 0:3072], w[:, 3088:4112], w[:, 4624:5648], w[:, 5648:8720], w[:, 8736:12832],
                            w[:, 4112:4624]], axis=1)
    small = jnp.concatenate([w[:, 3072:3088], w[:, 8720:8736], jnp.zeros((D, LANES - 2 * NH), w.dtype)], axis=1)
    return main, small


def _join_w_in(dw, ds):
    xbc, az, bq, bz, cq, ck, cv, cz, gates, bk, bv = dw
    return jnp.concatenate([xbc, az, ds[:, 0:NH], bq, bk, bv, bz, cq, ck, cv, ds[:, NH:2 * NH], cz, gates], axis=1)


def kernel(x, norm_w, w_in, conv_w, conv_b, dt_bias, a_log, d_skip, ssm_norm_w, sinks, f_bias, gate_bias, w_proj, w_out, final_norm_w, loss_target, m_norm_w, m_w_in, m_conv_w, m_conv_b, m_dt_bias, m_a_log, m_d_skip, m_ssm_norm_w, m_sinks, m_f_bias, m_gate_bias, m_w_proj, m_w_out, m_final_norm_w, v_norm_w, v_w_in, v_conv_w, v_conv_b, v_dt_bias, v_a_log, v_d_skip, v_ssm_norm_w, v_sinks, v_f_bias, v_gate_bias, v_w_proj, v_w_out, v_final_norm_w):
    Bl, S, _ = x.shape
    T = Bl * S
    depth = norm_w.shape[0]
    me = 4 * lax.axis_index("x") + 2 * lax.axis_index("y") + lax.axis_index("c")
    csh, gsh = conv_w.shape[2], gate_bias.shape[2]

    def gather_plan(l):
        small = jnp.concatenate([conv_w[l].reshape(-1), gate_bias[l].reshape(-1)]).reshape(-1, LANES)
        return _Comm("gather", [w_in[l].astype(BF16), w_proj[l].astype(BF16), w_out[l].astype(BF16), small])

    def unpack_weights(res):
        g_win, g_wp, g_wo, g_small = res
        flat = g_small.reshape(NDEV, -1)
        return (g_win.transpose(1, 0, 2).reshape(D, NIN),
                g_wp.transpose(1, 0, 2, 3).reshape(3, D, D),
                g_wo.reshape(D, D),
                flat[:, :4 * csh].reshape(NDEV, 4, csh).transpose(1, 0, 2).reshape(4, 2 * D),
                flat[:, 4 * csh:].reshape(NDEV, 3, gsh).transpose(1, 0, 2).reshape(3, D))

    def scatter_plan(gw_in=None, gw_p=None, gw_o=None):
        arrays = [] if gw_in is None else [gw_in.astype(BF16).reshape(-1, NDEV, NSH).transpose(1, 0, 2)]
        if gw_p is not None:
            arrays += [gw_p.astype(BF16).reshape(3, NDEV, D // NDEV, D).transpose(1, 0, 2, 3),
                       gw_o.astype(BF16).reshape(NDEV, D // NDEV, D)]
        return _Comm("scatter", arrays)

    pos = jnp.arange(S, dtype=F32)
    inv_freq = ROPE_THETA ** (-jnp.arange(0, HD, 2, dtype=F32) / HD)
    ang = pos[:, None] * inv_freq[None, :]
    cos128 = jnp.tile(jnp.cos(ang), (1, 4))
    sign = jnp.where((jnp.arange(LANES) % HD) < HD // 2, -1.0, 1.0).astype(F32)
    sin128 = jnp.tile(jnp.sin(ang), (1, 4)) * sign[None, :]

    x2 = x.reshape(T, D)
    tgt2 = loss_target.reshape(T, D)

    saved = []
    xcur = x2
    weights = [None] * depth
    weights[0] = unpack_weights(_gather_two_level(gather_plan(0).arrays, "gather_weights_0"))
    for l in range(depth):
        win_l, wp_l, wo_l, cw_l, gb_l = weights[l]
        wmain, wsmall = _split_w_in(win_l)
        proj, ps, h_t = _inproj_fwd(xcur, norm_w[l][None], wmain, wsmall, cos128, sin128, S, l)
        dtb = _lane_row(dt_bias[l], 0)
        alog = _lane_row(a_log[l], 0)
        fb = _lane_row(f_bias[l], NH)
        dsk = jnp.repeat(d_skip[l], HD)[None]
        ya, hp = _ssm_fwd(proj, ps, cw_l, conv_b[l][None], dtb, alog, dsk, ssm_norm_w[l][None], S, l)
        yb, ob, lse_b = _swa_fwd(proj, sinks[l], S, l)
        cum = _fox_cum(ps, fb, S, l)
        cumh = cum[:, NH:2 * NH].reshape(Bl, S, NH).transpose(0, 2, 1)
        cum_row = cumh[:, :, None, :]
        comm = gather_plan(l + 1) if l + 1 < depth else None
        res = _fox_fwd(proj, cum_row, S, l, comm)
        yc, oc, lse_c = res[:3]
        if comm is not None:
            weights[l + 1] = unpack_weights(res[3:])
        xnext, br, y_t = _merge_fwd(ya, yb, yc, proj, gb_l, wp_l, wo_l, xcur, l)
        saved.append(dict(x=xcur, wmain=wmain, wsmall=wsmall, proj=proj, ps=ps, h_t=h_t, dtb=dtb, alog=alog, fb=fb,
                          dsk=dsk, hp=hp, ob=ob, lse_b=lse_b, cum_row=cum_row, oc=oc, lse_c=lse_c, br=br, y_t=y_t))
        xcur = xnext

    dx, dx16, st = _final_loss(xcur, tgt2, final_norm_w[None])
    loss_part = st[2, 0]
    g_final = st[0]

    gsm = {k: [None] * depth for k in ("norm_w", "conv_w", "conv_b", "dt_bias", "a_log", "d_skip", "ssm_norm_w",
                                      "sinks", "f_bias", "gate_bias")}
    parts = [None] * depth
    pending = None
    for l in reversed(range(depth)):
        sv = saved[l]
        proj, ps = sv["proj"], sv["ps"]
        _, wp_l, wo_l, cw_l, gb_l = weights[l]
        dbr, dgates, merged_t, dgb, dy_a, do_b, dbz, do_c, dcz = _merge_bwd(dx16, wo_l, wp_l, sv["br"], proj, gb_l,
                                                                            sv["ob"], sv["oc"], l)
        g_wo = _matmul(merged_t, dx16, BF16, f"dwout_{l}")
        g_wp = _matmul_batched(sv["y_t"], dbr, BF16, f"dwproj_{l}")
        gsm["gate_bias"][l] = dgb[0:3]
        hosted = ([] if pending is None else pending.arrays) + (scatter_plan(None, g_wp, g_wo).arrays if l == 0 else [])
        res = _ssm_bwd(proj, ps, sv["hp"], dy_a, cw_l, conv_b[l][None], sv["dtb"], sv["alog"], sv["dsk"],
                       ssm_norm_w[l][None], S, l, _Comm("scatter", hosted) if hosted else None)
        dxbc, daz, dps_a, pgw, pg1, pgh = res[:6]
        if pending is not None:
            parts[l + 1] = res[6:9]
        if l == 0:
            parts_po = res[len(res) - 2:]
        gsm["conv_w"][l], gsm["conv_b"][l] = pgw[0:4], pgw[4]
        gsm["ssm_norm_w"][l] = pg1[0]
        gsm["dt_bias"][l], gsm["a_log"][l], gsm["d_skip"][l] = pgh[0, :NH], pgh[1, :NH], pgh[2, :NH]
        dq_b, dsk_b = _swa_bwd_dq(proj, do_b, sv["ob"], sv["lse_b"], sinks[l], cos128, sin128, S, l)
        dk_b, dv_b = _swa_bwd_dkv(proj, do_b, sv["ob"], sv["lse_b"], cos128, sin128, S, l)
        gsm["sinks"][l] = dsk_b[:, :, 0].reshape(NH)
        dq_c, dk_c, dv_c, dcum_k, dcum_q = _fox_bwd(proj, do_c, sv["oc"], sv["cum_row"], sv["lse_c"], S, l)
        dcum_tm = (dcum_k.reshape(Bl, NH, S) + dcum_q.reshape(Bl, NH, S)).transpose(0, 2, 1).reshape(T, NH)
        dcum_pad = jnp.pad(dcum_tm, ((0, 0), (NH, LANES - 2 * NH)))
        df, dfb = _fox_cum_bwd(dcum_pad, ps, sv["fb"], S, l)
        gsm["f_bias"][l] = dfb[0, NH:2 * NH]
        dps16 = (dps_a + df).astype(BF16)
        pieces = (dxbc, daz, dq_b, dbz, dq_c, dk_c, dv_c, dcz, dgates, dk_b, dv_b)
        dw_pieces = [_matmul(sv["h_t"], pc, BF16, f"dwin_{l}_{i}") for i, pc in enumerate(pieces)]
        dws = _matmul(sv["h_t"], dps16, BF16, f"dwin_small_{l}")
        g_win = _join_w_in(dw_pieces, dws)
        if l == 0:
            plans = [scatter_plan(g_win[r0:r1]) for r0, r1 in ROW_CHUNKS]
        else:
            plans, pending = [None] * len(ROW_CHUNKS), scatter_plan(g_win, g_wp, g_wo)
        dkv_b = jnp.concatenate([dk_b, dv_b], axis=1)
        res1 = _inproj_bwd_dx([(dxbc, OFF_XBC), (daz, OFF_AZ), (dq_b, OFF_BQ), (dbz, OFF_BZ)], sv["wmain"],
                              ("narrow", dps16, sv["wsmall"]), None, f"inproj_bwd_dh1_{l}", plans[0])
        res2 = _inproj_bwd_dx([(dq_c, OFF_CQ), (dk_c, OFF_CK), (dv_c, OFF_CV), (dcz, OFF_CZ)], sv["wmain"],
                              ("acc", res1[0]), None, f"inproj_bwd_dh2_{l}", plans[1])
        dx, dx16, dnw = _inproj_bwd_dx([(dgates, OFF_G), (dkv_b, OFF_BK)], sv["wmain"], ("acc", res2[0]),
                                       (sv["x"], norm_w[l][None], dx), f"inproj_bwd_dx_{l}")
        if l == 0:
            parts[0] = [jnp.concatenate([res1[1], res2[1]], axis=1), *parts_po]
        gsm["norm_w"][l] = dnw[0]

    big = {}
    for idx, (name, w, m, v) in enumerate((("w_in", w_in, m_w_in, v_w_in), ("w_proj", w_proj, m_w_proj, v_w_proj),
                                          ("w_out", w_out, m_w_out, v_w_out))):
        cols = w.shape[-1]
        res = _sum_adamw([parts[l][idx].reshape(NDEV, -1, cols) for l in range(depth)], w.reshape(depth, -1, cols),
                         m.reshape(depth, -1, cols), v.reshape(depth, -1, cols), f"adamw_{name}")
        big[name] = [r.reshape(w.shape) for r in res]

    small_names = ("norm_w", "conv_b", "dt_bias", "a_log", "d_skip", "ssm_norm_w", "sinks", "f_bias")
    small_parts = [jnp.stack(gsm[k]) for k in small_names] + [g_final, jnp.stack(gsm["conv_w"]),
                                                              jnp.stack(gsm["gate_bias"]), loss_part.reshape(1)]
    shapes = [a.shape for a in small_parts]
    summed = _unpack_rows(_all_reduce_small(_pack_rows(small_parts)), shapes)
    g_small = dict(zip(small_names, summed[:len(small_names)]))
    g_small["final_norm_w"] = summed[len(small_names)]
    g_small["conv_w"] = lax.dynamic_slice_in_dim(summed[len(small_names) + 1], me * csh, csh, axis=2)
    g_small["gate_bias"] = lax.dynamic_slice_in_dim(summed[len(small_names) + 2], me * gsh, gsh, axis=2)
    loss = summed[len(small_names) + 3][0]

    ws = dict(norm_w=norm_w, conv_w=conv_w, conv_b=conv_b, dt_bias=dt_bias, a_log=a_log, d_skip=d_skip,
              ssm_norm_w=ssm_norm_w, sinks=sinks, f_bias=f_bias, gate_bias=gate_bias, final_norm_w=final_norm_w)
    ms = dict(norm_w=m_norm_w, conv_w=m_conv_w, conv_b=m_conv_b, dt_bias=m_dt_bias, a_log=m_a_log, d_skip=m_d_skip,
              ssm_norm_w=m_ssm_norm_w, sinks=m_sinks, f_bias=m_f_bias, gate_bias=m_gate_bias,
              final_norm_w=m_final_norm_w)
    vs = dict(norm_w=v_norm_w, conv_w=v_conv_w, conv_b=v_conv_b, dt_bias=v_dt_bias, a_log=v_a_log, d_skip=v_d_skip,
              ssm_norm_w=v_ssm_norm_w, sinks=v_sinks, f_bias=v_f_bias, gate_bias=v_gate_bias,
              final_norm_w=v_final_norm_w)
    order = list(ws)
    oshapes = [ws[k].shape for k in order]
    res = _adamw_small(_pack_rows([g_small[k] for k in order]), _pack_rows([ws[k] for k in order]),
                       _pack_rows([ms[k] for k in order]), _pack_rows([vs[k] for k in order]))
    d_s, m_s, v_s = (dict(zip(order, _unpack_rows(r, oshapes))) for r in res)

    names = ("norm_w", "w_in", "conv_w", "conv_b", "dt_bias", "a_log", "d_skip", "ssm_norm_w", "sinks", "f_bias",
             "gate_bias", "w_proj", "w_out", "final_norm_w")
    grads, deltas, new_m, new_v = [], [], [], []
    for k in names:
        if k in big:
            g, d_, m_, v_ = big[k]
        else:
            g, d_, m_, v_ = g_small[k], d_s[k], m_s[k], v_s[k]
        grads.append(g)
        deltas.append(d_)
        new_m.append(m_)
        new_v.append(v_)
    return (loss, dx.reshape(Bl, S, D), *grads, *deltas, *new_m, *new_v)
```

```python
import functools
import math

import jax
import jax.numpy as jnp
from jax import lax
from jax.experimental import pallas as pl
from jax.experimental.pallas import tpu as pltpu

F32 = jnp.float32
BF16 = jnp.bfloat16
MESH = pl.DeviceIdType.MESH
NDEV = 8

D = 1024
NH = 16
HD = 64
NST = 128
NGRP = 4
LCH = 128
EPS = 1e-6
ROPE_THETA = 10000.0
SCALE = HD ** -0.5
NEG = -1e30

LANES = 128
VMEM_LIMIT = 56 * 1024 * 1024

OFF_XBC, OFF_AZ, OFF_BQ, OFF_BZ, OFF_CQ, OFF_CK, OFF_CV, OFF_CZ, OFF_G, OFF_BK, OFF_BV = (
    0, 2048, 3072, 4096, 5120, 6144, 7168, 8192, 9216, 12288, 12544)
NMAIN = 12800
NIN = 12832
NSH = NIN // NDEV

ROW_CHUNKS = ((0, 512), (512, 1024))

ADAM_LR, ADAM_B1, ADAM_B2, ADAM_EPS, ADAM_WD, ADAM_STEP = 0.001, 0.9, 0.999, 1e-08, 0.01, 10


def _cparams(dims=None, vmem=None):
    return pltpu.CompilerParams(dimension_semantics=dims, vmem_limit_bytes=vmem)


def _dot(a, b):
    return jnp.dot(a, b, preferred_element_type=F32)


def _dot_nt(a, b):
    return lax.dot_general(a, b, (((1,), (1,)), ((), ())), preferred_element_type=F32)


def _dot_tn(a, b):
    return lax.dot_general(a, b, (((0,), (0,)), ((), ())), preferred_element_type=F32)


def _dot_hi(a, b):
    return jnp.dot(a, b, precision=lax.Precision.HIGHEST, preferred_element_type=F32)


def _sigmoid(x):
    return 0.5 * jnp.tanh(0.5 * x) + 0.5


def _softplus(x):
    return jnp.maximum(x, 0.0) + jnp.log(1.0 + jnp.exp(-jnp.abs(x)))


def _lane_iota(n=LANES):
    return lax.broadcasted_iota(jnp.int32, (1, n), 1)


def _rot_half(x):
    first = (_lane_iota() % HD) < (HD // 2)
    return jnp.where(first, pltpu.roll(x, LANES - HD // 2, 1), pltpu.roll(x, HD // 2, 1))


def _head_sum(x, head):
    m = (_lane_iota() < HD) if head == 0 else (_lane_iota() >= HD)
    return jnp.sum(jnp.where(m, x, 0.0), axis=1, keepdims=True)


def _me_and_peers():
    x, y, c = lax.axis_index("x"), lax.axis_index("y"), lax.axis_index("c")
    me = 4 * x + 2 * y + c
    peers = []
    for k in range(1, NDEV):
        kx, ky, kc = (k >> 2) & 1, (k >> 1) & 1, k & 1
        px, py, pc = x ^ kx, y ^ ky, c ^ kc
        peers.append(((px, py, pc), 4 * px + 2 * py + pc))
    return me, peers


class _Comm:
    def __init__(self, kind, arrays):
        self.kind, self.arrays, self.n = kind, list(arrays), len(arrays)
        any_spec = pl.BlockSpec(memory_space=pl.ANY)
        self.in_specs = [any_spec] * self.n
        self.out_specs = [any_spec] * self.n
        self.out_shape = [jax.ShapeDtypeStruct(((NDEV,) + a.shape) if kind == "gather" else a.shape, a.dtype)
                          for a in self.arrays]
        self.scratch = [pltpu.SemaphoreType.DMA((self.n, NDEV - 1)), pltpu.SemaphoreType.DMA((self.n, NDEV - 1)),
                        pltpu.SemaphoreType.DMA((self.n,))]

    def copies(self, ins, outs, sems):
        send_sems, recv_sems, local_sems = sems
        me, peers = _me_and_peers()
        out = []
        for a in range(self.n):
            mine = ins[a] if self.kind == "gather" else ins[a].at[me]
            out.append(pltpu.make_async_copy(mine, outs[a].at[me], local_sems.at[a]))
            for k, (peer, pidx) in enumerate(peers):
                src = ins[a] if self.kind == "gather" else ins[a].at[pidx]
                out.append(pltpu.make_async_remote_copy(
                    src_ref=src, dst_ref=outs[a].at[me], send_sem=send_sems.at[a, k], recv_sem=recv_sems.at[a, k],
                    device_id=peer, device_id_type=MESH))
        return out


def _gather_two_level(arrays, name):
    n = len(arrays)

    def body(*refs):
        ins, outs = refs[:n], refs[n:2 * n]
        send_sems, recv_sems, local_sems = refs[2 * n:]
        x, y, c = lax.axis_index("x"), lax.axis_index("y"), lax.axis_index("c")
        me, sibling = (x, y, c), (x, y, 1 - c)
        chips = [(1 - x, y), (x, 1 - y), (1 - x, 1 - y)]

        def slot(a, dev):
            return outs[a].at[4 * dev[0] + 2 * dev[1] + dev[2]]

        def copy(a, k, block, to, src=None):
            return pltpu.make_async_remote_copy(
                src_ref=slot(a, block) if src is None else src, dst_ref=slot(a, block),
                send_sem=send_sems.at[a, k], recv_sem=recv_sems.at[a, k], device_id=to, device_id_type=MESH)

        mine = [pltpu.make_async_copy(ins[a], slot(a, me), local_sems.at[a]) for a in range(n)]
        for cp in mine:
            cp.start()
        first = []
        for a in range(n):
            first.append(copy(a, 0, me, sibling, src=ins[a]))
            first += [copy(a, 1 + j, me, (*chip, c), src=ins[a]) for j, chip in enumerate(chips)]
        for cp in first:
            cp.start()
        passed = []
        for j, chip in enumerate(chips):
            for a in range(n):
                copy(a, 1 + j, (*chip, c), me).wait_recv()
                fwd = copy(a, 4 + j, (*chip, c), sibling)
                fwd.start()
                passed.append(fwd)
        for a in range(n):
            copy(a, 0, sibling, me).wait_recv()
            for j, chip in enumerate(chips):
                copy(a, 4 + j, (*chip, 1 - c), me).wait_recv()
        for cp in first + passed:
            cp.wait_send()
        for cp in mine:
            cp.wait()

    any_spec = pl.BlockSpec(memory_space=pl.ANY)
    return pl.pallas_call(
        body, name=name, out_shape=[jax.ShapeDtypeStruct((NDEV,) + a.shape, a.dtype) for a in arrays],
        in_specs=[any_spec] * n, out_specs=[any_spec] * n,
        scratch_shapes=[pltpu.SemaphoreType.DMA((n, NDEV - 1)), pltpu.SemaphoreType.DMA((n, NDEV - 1)),
                        pltpu.SemaphoreType.DMA((n,))])(*arrays)


def _hosted_call(body, comm, name, grid, in_specs, out_specs, out_shape, scratch, dims, operands):
    if comm is None:
        return pl.pallas_call(body, name=name, grid=grid, in_specs=in_specs, out_specs=out_specs, out_shape=out_shape,
                              scratch_shapes=scratch, compiler_params=_cparams(dims, VMEM_LIMIT))(*operands)
    n_in, n_out, n_scr, n = len(in_specs), len(out_specs), len(scratch), comm.n

    def hosted(*refs):
        hin, cin = refs[:n_in], refs[n_in:n_in + n]
        hout = refs[n_in + n:n_in + n + n_out]
        cout = refs[n_in + n + n_out:n_in + 2 * n + n_out]
        hscr = refs[n_in + 2 * n + n_out:n_in + 2 * n + n_out + n_scr]
        sems = refs[n_in + 2 * n + n_out + n_scr:]
        ids = [pl.program_id(a) for a in range(len(grid))]
        first = functools.reduce(jnp.logical_and, [i == 0 for i in ids])
        last = functools.reduce(jnp.logical_and, [i == g - 1 for i, g in zip(ids, grid)])

        @pl.when(first)
        def _():
            for cp in comm.copies(cin, cout, sems):
                cp.start()

        body(*hin, *hout, *hscr)

        @pl.when(last)
        def _():
            for cp in comm.copies(cin, cout, sems):
                cp.wait()

    return pl.pallas_call(
        hosted, name=name, grid=grid, in_specs=list(in_specs) + comm.in_specs,
        out_specs=list(out_specs) + comm.out_specs, out_shape=list(out_shape) + comm.out_shape,
        scratch_shapes=list(scratch) + comm.scratch,
        compiler_params=_cparams(("arbitrary",) * len(grid), VMEM_LIMIT))(*operands, *comm.arrays)


def _all_reduce_small(v):
    rows = v.shape[0]

    def body(v_ref, sum_ref, all_ref, send_sems, recv_sems):
        me, peers = _me_and_peers()
        all_ref[me] = v_ref[...]
        copies = []
        for k, (peer, _) in enumerate(peers):
            cp = pltpu.make_async_remote_copy(
                src_ref=v_ref, dst_ref=all_ref.at[me],
                send_sem=send_sems.at[k], recv_sem=recv_sems.at[k],
                device_id=peer, device_id_type=MESH)
            cp.start()
            copies.append(cp)
        for cp in copies:
            cp.wait()
        acc = all_ref[0]
        for d in range(1, NDEV):
            acc = acc + all_ref[d]
        sum_ref[...] = acc

    vm = pl.BlockSpec(memory_space=pltpu.VMEM)
    return pl.pallas_call(
        body, name="all_reduce_small",
        out_shape=jax.ShapeDtypeStruct((rows, LANES), F32),
        in_specs=[vm], out_specs=vm,
        scratch_shapes=[pltpu.VMEM((NDEV, rows, LANES), F32),
                        pltpu.SemaphoreType.DMA((NDEV - 1,)), pltpu.SemaphoreType.DMA((NDEV - 1,))],
    )(v)


def _adamw_math(w, g, m, v):
    m = ADAM_B1 * m + (1.0 - ADAM_B1) * g
    v = ADAM_B2 * v + (1.0 - ADAM_B2) * jnp.square(g)
    m_hat = m / (1.0 - ADAM_B1 ** ADAM_STEP)
    v_hat = v / (1.0 - ADAM_B2 ** ADAM_STEP)
    delta = -ADAM_LR * (m_hat / (jnp.sqrt(v_hat) + ADAM_EPS) + ADAM_WD * w)
    return delta, m, v


def _sum_adamw(parts, w, m, v, name):
    depth, rows, cols = w.shape
    tr = next(c for c in (256, 128, 64, 32, 16) if rows % c == 0)
    nb = rows // tr

    def body(*refs):
        p_refs, (w_ref, m_ref, v_ref, g_ref, d_ref, nm_ref, nv_ref) = refs[:depth], refs[depth:]
        l = pl.program_id(0)
        for ll in range(depth):
            @pl.when(l == ll)
            def _(ll=ll):
                g = p_refs[ll][0].astype(F32)
                for d in range(1, NDEV):
                    g = g + p_refs[ll][d].astype(F32)
                delta, nm, nv = _adamw_math(w_ref[0], g, m_ref[0], v_ref[0])
                g_ref[0] = g
                d_ref[0] = delta
                nm_ref[0] = nm
                nv_ref[0] = nv

    part = lambda ll: pl.BlockSpec((NDEV, tr, cols), lambda l, i, ll=ll: (0, jnp.where(l == ll, i, jnp.where(l < ll, 0, nb - 1)), 0))
    blk = pl.BlockSpec((1, tr, cols), lambda l, i: (l, i, 0))
    sds = jax.ShapeDtypeStruct((depth, rows, cols), F32)
    return pl.pallas_call(
        body, name=name, grid=(depth, nb),
        in_specs=[part(ll) for ll in range(depth)] + [blk, blk, blk],
        out_specs=[blk, blk, blk, blk], out_shape=[sds, sds, sds, sds],
        compiler_params=_cparams(("arbitrary", "arbitrary"), VMEM_LIMIT),
    )(*parts, w, m, v)


def _adamw_small(g, w, m, v):
    def body(g_ref, w_ref, m_ref, v_ref, d_ref, nm_ref, nv_ref):
        delta, nm, nv = _adamw_math(w_ref[...], g_ref[...], m_ref[...], v_ref[...])
        d_ref[...] = delta
        nm_ref[...] = nm
        nv_ref[...] = nv

    sds = jax.ShapeDtypeStruct(g.shape, F32)
    return pl.pallas_call(body, name="adamw_small", out_shape=[sds, sds, sds])(g, w, m, v)


def _matmul(a, b, out_dtype, name, tm=1024, tn=1024, tk=1024):
    M, K = a.shape
    N = b.shape[1]
    tm, tn, tk = min(tm, M), min(tn, N), min(tk, K)
    nk = K // tk

    def body(a_ref, b_ref, o_ref, acc):
        k = pl.program_id(2)

        @pl.when(k == 0)
        def _():
            acc[...] = jnp.zeros_like(acc)

        acc[...] += _dot(a_ref[...], b_ref[...])

        @pl.when(k == nk - 1)
        def _():
            o_ref[...] = acc[...].astype(out_dtype)

    return pl.pallas_call(
        body, name=name, grid=(M // tm, N // tn, nk),
        in_specs=[pl.BlockSpec((tm, tk), lambda i, j, k: (i, k)), pl.BlockSpec((tk, tn), lambda i, j, k: (k, j))],
        out_specs=pl.BlockSpec((tm, tn), lambda i, j, k: (i, j)),
        out_shape=jax.ShapeDtypeStruct((M, N), out_dtype),
        scratch_shapes=[pltpu.VMEM((tm, tn), F32)],
        compiler_params=_cparams(("parallel", "parallel", "arbitrary"), VMEM_LIMIT),
    )(a, b)


def _matmul_batched(a, b, out_dtype, name, tm=1024, tn=1024, tk=512):
    G, M, K = a.shape
    N = b.shape[2]
    tm, tn, tk = min(tm, M), min(tn, N), min(tk, K)
    nk = K // tk

    def body(a_ref, b_ref, o_ref, acc):
        k = pl.program_id(3)

        @pl.when(k == 0)
        def _():
            acc[...] = jnp.zeros_like(acc)

        acc[...] += _dot(a_ref[0], b_ref[0])

        @pl.when(k == nk - 1)
        def _():
            o_ref[0] = acc[...].astype(out_dtype)

    return pl.pallas_call(
        body, name=name, grid=(G, M // tm, N // tn, nk),
        in_specs=[pl.BlockSpec((1, tm, tk), lambda g, i, j, k: (g, i, k)),
                  pl.BlockSpec((1, tk, tn), lambda g, i, j, k: (g, k, j))],
        out_specs=pl.BlockSpec((1, tm, tn), lambda g, i, j, k: (g, i, j)),
        out_shape=jax.ShapeDtypeStruct((G, M, N), out_dtype),
        scratch_shapes=[pltpu.VMEM((tm, tn), F32)],
        compiler_params=_cparams(("parallel", "parallel", "parallel", "arbitrary"), VMEM_LIMIT),
    )(a, b)


def _inproj_fwd(x2, nw, wmain, wsmall, cos128, sin128, S, li, comm=None):
    T = x2.shape[0]
    tm, tn = min(2048, S), 512
    nj, npos = NMAIN // tn, S // tm
    jq0, jk = OFF_BQ // tn, OFF_BK // tn

    def body(x_ref, nw_ref, w_ref, ws_ref, cos_ref, sin_ref, proj_ref, ps_ref, ht_ref, h_scr):
        j = pl.program_id(1)

        @pl.when(j == 0)
        def _():
            x = x_ref[...]
            r = lax.rsqrt(jnp.mean(x * x, axis=-1, keepdims=True) + EPS)
            h = (x * r * nw_ref[...]).astype(BF16)
            h_scr[...] = h
            ht_ref[...] = h.T
            ps_ref[...] = _dot(h, ws_ref[...])

        acc = _dot(h_scr[...], w_ref[...])

        def roped(c):
            xc = acc[:, LANES * c:LANES * (c + 1)]
            return (xc * cos_ref[...] + _rot_half(xc) * sin_ref[...]).astype(BF16)

        def plain(c):
            return acc[:, LANES * c:LANES * (c + 1)].astype(BF16)

        is_q = jnp.logical_or(j == jq0, j == jq0 + 1)
        is_k = j == jk

        @pl.when(is_q)
        def _():
            for c in range(4):
                proj_ref[:, LANES * c:LANES * (c + 1)] = roped(c)

        @pl.when(is_k)
        def _():
            for c in range(4):
                proj_ref[:, LANES * c:LANES * (c + 1)] = roped(c) if c < 2 else plain(c)

        @pl.when(jnp.logical_not(jnp.logical_or(is_q, is_k)))
        def _():
            proj_ref[...] = acc.astype(BF16)

    return _hosted_call(
        body, comm, f"inproj_fwd_{li}", (T // tm, nj),
        in_specs=[pl.BlockSpec((tm, D), lambda i, j: (i, 0)),
                  pl.BlockSpec((1, D), lambda i, j: (0, 0)),
                  pl.BlockSpec((D, tn), lambda i, j: (0, j)),
                  pl.BlockSpec((D, LANES), lambda i, j: (0, 0)),
                  pl.BlockSpec((tm, LANES), lambda i, j: (i % npos, 0)),
                  pl.BlockSpec((tm, LANES), lambda i, j: (i % npos, 0))],
        out_specs=[pl.BlockSpec((tm, tn), lambda i, j: (i, j)),
                   pl.BlockSpec((tm, LANES), lambda i, j: (i, 0)),
                   pl.BlockSpec((D, tm), lambda i, j: (0, i))],
        out_shape=[jax.ShapeDtypeStruct((T, NMAIN), BF16), jax.ShapeDtypeStruct((T, LANES), F32),
                   jax.ShapeDtypeStruct((D, T), BF16)],
        scratch=[pltpu.VMEM((tm, D), BF16)], dims=("parallel", "arbitrary"),
        operands=(x2, nw, wmain, wsmall, cos128, sin128))


def _inproj_bwd_dx(segs, wmain, init, final, name, comm=None):
    T = segs[0][0].shape[0]
    tm = min(1024, T)
    tk = 1024 if all(a.shape[1] % 1024 == 0 and c % 1024 == 0 for a, c in segs) else 512
    ni = T // tm
    k0s, nks, c0s = [], [], []
    for arr, col0 in segs:
        k0s.append(sum(nks))
        nks.append(arr.shape[1] // tk)
        c0s.append(col0 // tk)
    nk = sum(nks)
    ns = len(segs)

    def in_range(k, s):
        return jnp.logical_and(k >= k0s[s], k < k0s[s] + nks[s])

    def wcol(i, k):
        g = 0
        for s in range(ns):
            g = g + jnp.where(in_range(k, s), c0s[s] + k - k0s[s], 0)
        return (0, g)

    n_init = 2 if init[0] == "narrow" else 1

    def body(*refs):
        seg_refs, w_ref = refs[:ns], refs[ns]
        init_refs = refs[ns + 1:ns + 1 + n_init]
        rest = refs[ns + 1 + n_init:]
        i, k = pl.program_id(0), pl.program_id(1)
        acc = rest[-1]

        @pl.when(k == 0)
        def _():
            if init[0] == "narrow":
                acc[...] = _dot_nt(init_refs[0][...], init_refs[1][...])
            else:
                acc[...] = init_refs[0][...]

        for s in range(ns):
            @pl.when(in_range(k, s))
            def _(s=s):
                acc[...] += _dot_nt(seg_refs[s][...], w_ref[...])

        if final is None:
            @pl.when(k == nk - 1)
            def _():
                rest[0][...] = acc[...]
        else:
            x_ref, nw_ref, dxo_ref, dx_ref, dx16_ref, dnw_ref = rest[:6]

            @pl.when(jnp.logical_and(i == 0, k == 0))
            def _():
                dnw_ref[...] = jnp.zeros_like(dnw_ref)

            @pl.when(k == nk - 1)
            def _():
                x = x_ref[...]
                r = lax.rsqrt(jnp.mean(x * x, axis=-1, keepdims=True) + EPS)
                dh = acc[...]
                g = dh * nw_ref[...]
                dx = dxo_ref[...] + r * g - x * (r * r * r) * jnp.mean(g * x, axis=-1, keepdims=True)
                dx_ref[...] = dx
                dx16_ref[...] = dx.astype(BF16)
                dnw_ref[0:1, :] += jnp.sum(dh * x * r, axis=0, keepdims=True)

    row = pl.BlockSpec((tm, D), lambda i, k: (i, 0))
    in_specs = [pl.BlockSpec((tm, tk), lambda i, k, s=s: (i, jnp.clip(k - k0s[s], 0, nks[s] - 1))) for s in range(ns)]
    in_specs.append(pl.BlockSpec((D, tk), wcol))
    operands = [a for a, _ in segs] + [wmain]
    if init[0] == "narrow":
        in_specs += [pl.BlockSpec((tm, LANES), lambda i, k: (i, 0)), pl.BlockSpec((D, LANES), lambda i, k: (0, 0))]
    else:
        in_specs.append(row)
    operands += list(init[1:])
    if final is None:
        out_specs, out_shape = [row], [jax.ShapeDtypeStruct((T, D), F32)]
    else:
        in_specs += [row, pl.BlockSpec((1, D), lambda i, k: (0, 0)), row]
        operands += list(final)
        out_specs = [row, row, pl.BlockSpec((8, D), lambda i, k: (0, 0))]
        out_shape = [jax.ShapeDtypeStruct((T, D), F32), jax.ShapeDtypeStruct((T, D), BF16),
                     jax.ShapeDtypeStruct((8, D), F32)]
    return _hosted_call(body, comm, name, (ni, nk), in_specs=in_specs, out_specs=out_specs, out_shape=out_shape,
                        scratch=[pltpu.VMEM((tm, D), F32)], dims=("arbitrary", "arbitrary"), operands=tuple(operands))


def _merge_fwd(ya, yb, yc, proj, gbias, wp, wout, x2, li):
    T = x2.shape[0]
    tm = min(512, T)
    gcol = OFF_G // D

    def body(ya_ref, yb_ref, yc_ref, g0_ref, g1_ref, g2_ref, gb_ref, wp_ref, wo_ref, x_ref, xn_ref, br_ref, yt_ref):
        merged = jnp.zeros((tm, D), F32)
        for i, (y_ref, g_ref) in enumerate(((ya_ref, g0_ref), (yb_ref, g1_ref), (yc_ref, g2_ref))):
            y = y_ref[...]
            yt_ref[i] = y.T
            br = _dot(y, wp_ref[i])
            br_ref[i] = br.astype(BF16)
            gate = _sigmoid(g_ref[...].astype(F32) + gb_ref[i:i + 1, :])
            merged = merged + gate * br
        xn_ref[...] = x_ref[...] + _dot(merged.astype(BF16), wo_ref[...])

    row = lambda c: pl.BlockSpec((tm, D), lambda i, c=c: (i, c))
    return pl.pallas_call(
        body, name=f"merge_fwd_{li}", grid=(T // tm,),
        in_specs=[row(0), row(0), row(0), row(gcol), row(gcol + 1), row(gcol + 2),
                  pl.BlockSpec((3, D), lambda i: (0, 0)),
                  pl.BlockSpec((3, D, D), lambda i: (0, 0, 0)),
                  pl.BlockSpec((D, D), lambda i: (0, 0)),
                  row(0)],
        out_specs=[row(0), pl.BlockSpec((3, tm, D), lambda i: (0, i, 0)), pl.BlockSpec((3, D, tm), lambda i: (0, 0, i))],
        out_shape=[jax.ShapeDtypeStruct((T, D), F32), jax.ShapeDtypeStruct((3, T, D), BF16),
                   jax.ShapeDtypeStruct((3, D, T), BF16)],
        compiler_params=_cparams(("parallel",), VMEM_LIMIT),
    )(ya, yb, yc, proj, proj, proj, gbias, wp, wout, x2)


def _merge_bwd(dxo16, wout, wp, br, proj, gbias, ob, oc, li):
    T = dxo16.shape[0]
    tm = min(256, T)
    gcol = OFF_G // D

    def body(dx_ref, wo_ref, wp_ref, br_ref, g0_ref, g1_ref, g2_ref, gb_ref, ob_ref, oc_ref, zb_ref, zc_ref,
             dbr_ref, dg_ref, mt_ref, dgb_ref, dya_ref, dob_ref, dzb_ref, doc_ref, dzc_ref):
        @pl.when(pl.program_id(0) == 0)
        def _():
            dgb_ref[...] = jnp.zeros_like(dgb_ref)

        dm = _dot_nt(dx_ref[...], wo_ref[...])
        merged = jnp.zeros((tm, D), F32)
        dys = []
        for i, g_ref in enumerate((g0_ref, g1_ref, g2_ref)):
            b = br_ref[i].astype(F32)
            gate = _sigmoid(g_ref[...].astype(F32) + gb_ref[i:i + 1, :])
            merged = merged + gate * b
            dbr = (dm * gate).astype(BF16)
            dbr_ref[i] = dbr
            dgate = dm * b * gate * (1.0 - gate)
            dg_ref[:, D * i:D * (i + 1)] = dgate.astype(BF16)
            dgb_ref[i:i + 1, :] += jnp.sum(dgate, axis=0, keepdims=True)
            dys.append(_dot_nt(dbr, wp_ref[i]))
        mt_ref[...] = merged.astype(BF16).T
        dya_ref[...] = dys[0].astype(BF16)
        for dy, o_ref, z_ref, do_ref, dz_ref in ((dys[1], ob_ref, zb_ref, dob_ref, dzb_ref),
                                                 (dys[2], oc_ref, zc_ref, doc_ref, dzc_ref)):
            z = z_ref[...].astype(F32)
            sg = _sigmoid(z)
            do_ref[...] = (dy * z * sg).astype(BF16)
            dz_ref[...] = (dy * o_ref[...].astype(F32) * sg * (1.0 + z * (1.0 - sg))).astype(BF16)

    row = lambda c: pl.BlockSpec((tm, D), lambda i, c=c: (i, c))
    sds = jax.ShapeDtypeStruct((T, D), BF16)
    return pl.pallas_call(
        body, name=f"merge_bwd_{li}", grid=(T // tm,),
        in_specs=[row(0), pl.BlockSpec((D, D), lambda i: (0, 0)), pl.BlockSpec((3, D, D), lambda i: (0, 0, 0)),
                  pl.BlockSpec((3, tm, D), lambda i: (0, i, 0)),
                  row(gcol), row(gcol + 1), row(gcol + 2),
                  pl.BlockSpec((3, D), lambda i: (0, 0)),
                  row(0), row(0), row(OFF_BZ // D), row(OFF_CZ // D)],
        out_specs=[pl.BlockSpec((3, tm, D), lambda i: (0, i, 0)),
                   pl.BlockSpec((tm, 3 * D), lambda i: (i, 0)),
                   pl.BlockSpec((D, tm), lambda i: (0, i)),
                   pl.BlockSpec((8, D), lambda i: (0, 0)),
                   row(0), row(0), row(0), row(0), row(0)],
        out_shape=[jax.ShapeDtypeStruct((3, T, D), BF16), jax.ShapeDtypeStruct((T, 3 * D), BF16),
                   jax.ShapeDtypeStruct((D, T), BF16), jax.ShapeDtypeStruct((8, D), F32), sds, sds, sds, sds, sds],
        compiler_params=_cparams(("arbitrary",), VMEM_LIMIT),
    )(dxo16, wout, wp, br, proj, proj, proj, gbias, ob, oc, proj, proj)


def _final_loss(x2, tgt, fw):
    T = x2.shape[0]
    tm = min(512, T)
    ni = T // tm

    def body(x_ref, t_ref, w_ref, dx_ref, dx16_ref, st_ref):
        i = pl.program_id(0)

        @pl.when(i == 0)
        def _():
            st_ref[...] = jnp.zeros_like(st_ref)

        x = x_ref[...]
        r = lax.rsqrt(jnp.mean(x * x, axis=-1, keepdims=True) + EPS)
        xh = x * r
        err = xh * w_ref[...] - t_ref[...]
        dy = err * (1.0 / D)
        g = dy * w_ref[...]
        dx = r * g - x * (r * r * r) * jnp.mean(g * x, axis=-1, keepdims=True)
        dx_ref[...] = dx
        dx16_ref[...] = dx.astype(BF16)
        st_ref[0:1, :] += jnp.sum(dy * xh, axis=0, keepdims=True)
        st_ref[1:2, :] += jnp.sum(err * err, axis=0, keepdims=True)

        @pl.when(i == ni - 1)
        def _():
            tot = jnp.sum(st_ref[1:2, :], axis=1, keepdims=True) * (0.5 / D)
            st_ref[2:3, :] = jnp.broadcast_to(tot, (1, D))

    row = pl.BlockSpec((tm, D), lambda i: (i, 0))
    return pl.pallas_call(
        body, name="final_loss", grid=(ni,),
        in_specs=[row, row, pl.BlockSpec((1, D), lambda i: (0, 0))],
        out_specs=[row, row, pl.BlockSpec((8, D), lambda i: (0, 0))],
        out_shape=[jax.ShapeDtypeStruct((T, D), F32), jax.ShapeDtypeStruct((T, D), BF16),
                   jax.ShapeDtypeStruct((8, D), F32)],
        compiler_params=_cparams(("arbitrary",), VMEM_LIMIT),
    )(x2, tgt, fw)


def _fox_cum(ps, fb_row, S, li):
    T = ps.shape[0]
    blk = min(4 * LCH, S)
    nb, nsub = S // blk, blk // LCH

    def body(ps_ref, fb_ref, cum_ref, carry):
        @pl.when(pl.program_id(1) == 0)
        def _():
            carry[...] = jnp.zeros_like(carry)

        r = lax.broadcasted_iota(jnp.int32, (LCH, LCH), 0)
        c = lax.broadcasted_iota(jnp.int32, (LCH, LCH), 1)
        tri = (r >= c).astype(F32)
        run = carry[0:1, :]
        for u in range(nsub):
            rows = slice(LCH * u, LCH * (u + 1))
            logf = -_softplus(-(ps_ref[rows, :] + fb_ref[...]))
            cum = _dot_hi(tri, logf) + run
            cum_ref[rows, :] = cum
            run = cum[LCH - 1:LCH, :]
        carry[0:1, :] = run

    return pl.pallas_call(
        body, name=f"fox_cum_{li}", grid=(T // S, nb),
        in_specs=[pl.BlockSpec((blk, LANES), lambda b, i: (b * nb + i, 0)),
                  pl.BlockSpec((1, LANES), lambda b, i: (0, 0))],
        out_specs=pl.BlockSpec((blk, LANES), lambda b, i: (b * nb + i, 0)),
        out_shape=jax.ShapeDtypeStruct((T, LANES), F32),
        scratch_shapes=[pltpu.VMEM((8, LANES), F32)],
        compiler_params=_cparams(("arbitrary", "arbitrary")),
    )(ps, fb_row)


def _fox_cum_bwd(dcum, ps, fb_row, S, li):
    T = ps.shape[0]
    rows_blk = min(4 * LCH, S)
    nb, nsub = S // rows_blk, rows_blk // LCH

    def body(dc_ref, ps_ref, fb_ref, df_ref, dfb_ref, carry):
        b, i = pl.program_id(0), pl.program_id(1)

        @pl.when(i == 0)
        def _():
            carry[...] = jnp.zeros_like(carry)

        @pl.when(jnp.logical_and(b == 0, i == 0))
        def _():
            dfb_ref[...] = jnp.zeros_like(dfb_ref)

        r = lax.broadcasted_iota(jnp.int32, (LCH, LCH), 0)
        c = lax.broadcasted_iota(jnp.int32, (LCH, LCH), 1)
        tri = (c >= r).astype(F32)
        lane = _lane_iota()
        live = jnp.logical_and(lane >= NH, lane < 2 * NH)
        run = carry[0:1, :]
        dfb = jnp.zeros((1, LANES), F32)
        for u in reversed(range(nsub)):
            rows = slice(LCH * u, LCH * (u + 1))
            dc = dc_ref[rows, :]
            dlogf = _dot_hi(tri, dc) + run
            run = run + jnp.sum(dc, axis=0, keepdims=True)
            df = jnp.where(live, dlogf * _sigmoid(-(ps_ref[rows, :] + fb_ref[...])), 0.0)
            df_ref[rows, :] = df
            dfb = dfb + jnp.sum(df, axis=0, keepdims=True)
        carry[0:1, :] = run
        dfb_ref[0:1, :] += dfb

    blk = pl.BlockSpec((rows_blk, LANES), lambda b, i: (b * nb + nb - 1 - i, 0))
    return pl.pallas_call(
        body, name=f"fox_cum_bwd_{li}", grid=(T // S, nb),
        in_specs=[blk, blk, pl.BlockSpec((1, LANES), lambda b, i: (0, 0))],
        out_specs=[blk, pl.BlockSpec((8, LANES), lambda b, i: (0, 0))],
        out_shape=[jax.ShapeDtypeStruct((T, LANES), F32), jax.ShapeDtypeStruct((8, LANES), F32)],
        scratch_shapes=[pltpu.VMEM((8, LANES), F32)],
        compiler_params=_cparams(("arbitrary", "arbitrary")),
    )(dcum, ps, fb_row)


def _fox_blocks(S):
    bq = min(512, S)
    return bq, S // bq


def _split3(c):
    hi = c.astype(BF16).astype(F32)
    r = c - hi
    mid = r.astype(BF16).astype(F32)
    return hi, mid, (r - mid).astype(BF16).astype(F32)


def _augment(x, parts, key_side, hh):
    lane = _lane_iota()
    b0 = HD if hh == 0 else 0
    p0, o0 = (b0 + 3, b0) if key_side else (b0, b0 + 3)
    out = jnp.where(jnp.logical_and(lane >= o0, lane < o0 + 3), 1.0, x)
    for t in range(3):
        out = jnp.where(lane == p0 + t, parts[t], out)
    return out.astype(BF16)


def _row_to_col(row_ref_slice, col_scr, hh, S):
    step = min(4 * LANES, S)
    for t in range(S // step):
        r = row_ref_slice[:, step * t:step * (t + 1)]
        col_scr[hh, step * t:step * (t + 1), :] = jnp.broadcast_to(r, (LANES, step)).T[:, 0:1]


def _col_to_row(col):
    return jnp.broadcast_to(col, (col.shape[0], LANES)).T[0:1, :]


def _fox_fwd(proj, cum_row, S, li, comm=None):
    T = proj.shape[0]
    B = T // S
    bq, nq = _fox_blocks(S)
    qc, kc, vc, zc = OFF_CQ // LANES, OFF_CK // LANES, OFF_CV // LANES, OFF_CZ // LANES

    def body(q_ref, k_ref, v_ref, z_ref, cr_ref, y_ref, o_ref, lse_ref, kaug, cc_ref, vaug):
        i = pl.program_id(2)
        m0 = _lane_iota() < HD

        @pl.when(i == 0)
        def _():
            kf = k_ref[...].astype(F32)
            vf = v_ref[...]
            for hh in range(2):
                _row_to_col(cr_ref[0, hh], cc_ref, hh, S)
                kaug[hh] = _augment(kf, _split3(-cc_ref[hh]), True, hh)
                vaug[hh] = jnp.where(m0 if hh == 0 else jnp.logical_not(m0), vf, jnp.ones_like(vf))

        q2 = q_ref[...].astype(F32) * SCALE
        rows_q = pl.ds(pl.multiple_of(i * bq, bq), bq)
        row = lax.broadcasted_iota(jnp.int32, (bq, bq), 0)
        col = lax.broadcasted_iota(jnp.int32, (bq, bq), 1)
        qa = [_augment(jnp.where(m0 if hh == 0 else jnp.logical_not(m0), q2, 0.0),
                       _split3(cc_ref[hh, rows_q, :]), False, hh) for hh in range(2)]

        def step(j, carry, masked):
            start = pl.multiple_of(j * bq, bq)
            out = []
            for hh in range(2):
                m, acc = carry[2 * hh:2 * hh + 2]
                s = _dot_nt(qa[hh], kaug[hh, pl.ds(start, bq), :])
                if masked:
                    s = jnp.where(row >= col, s, NEG)
                mn = jnp.maximum(m, jnp.max(s, axis=1, keepdims=True))
                p = jnp.exp(s - mn)
                out += [mn, jnp.exp(m - mn) * acc + _dot(p.astype(BF16), vaug[hh, pl.ds(start, bq), :])]
            return tuple(out)

        init = (jnp.full((bq, 1), NEG, F32), jnp.zeros((bq, LANES), F32)) * 2
        carry = step(i, lax.fori_loop(0, i, functools.partial(step, masked=False), init), True)
        outs = []
        for hh in range(2):
            m, acc = carry[2 * hh:2 * hh + 2]
            other = HD if hh == 0 else 0
            l = acc[:, other:other + 1]
            outs.append(acc / l)
            lse_ref[0, hh] = m + jnp.log(l)
        o2 = jnp.where(m0, outs[0], outs[1])
        z = z_ref[...].astype(F32)
        o_ref[...] = o2.astype(BF16)
        y_ref[...] = (o2 * z * _sigmoid(z)).astype(BF16)

    qblk = lambda c: pl.BlockSpec((bq, LANES), lambda b, p, i, c=c: (b * nq + i, c + p))
    sblk = lambda c: pl.BlockSpec((S, LANES), lambda b, p, i, c=c: (b, c + p))
    return _hosted_call(
        body, comm, f"fox_fwd_{li}", (B, NH // 2, nq),
        in_specs=[qblk(qc), sblk(kc), sblk(vc), qblk(zc),
                  pl.BlockSpec((1, 2, 1, S), lambda b, p, i: (b, p, 0, 0))],
        out_specs=[qblk(0), qblk(0), pl.BlockSpec((1, 2, bq, 1), lambda b, p, i: (b, p, i, 0))],
        out_shape=[jax.ShapeDtypeStruct((T, D), BF16), jax.ShapeDtypeStruct((T, D), BF16),
                   jax.ShapeDtypeStruct((B, NH, S, 1), F32)],
        scratch=[pltpu.VMEM((2, S, LANES), BF16), pltpu.VMEM((2, S, 1), F32), pltpu.VMEM((2, S, LANES), BF16)],
        dims=("parallel", "parallel", "arbitrary"), operands=(proj, proj, proj, proj, cum_row))


def _fox_bwd(proj, do, o, cum_row, lse, S, li, comm=None):
    T = proj.shape[0]
    B = T // S
    bq, nq = _fox_blocks(S)
    qc, kc, vc = OFF_CQ // LANES, OFF_CK // LANES, OFF_CV // LANES

    def body(q_ref, k_ref, v_ref, do_ref, o_ref, cr_ref, lse_ref, dq_ref, dk_ref, dv_ref, dc_ref, dr_ref,
             dq_scr, dr_scr, qaug, cc_ref):
        j = pl.program_id(2)
        m0 = _lane_iota() < HD

        @pl.when(j == 0)
        def _():
            dq_scr[...] = jnp.zeros_like(dq_scr)
            dr_scr[...] = jnp.zeros_like(dr_scr)
            qf = q_ref[...].astype(F32) * SCALE
            for hh in range(2):
                sel = m0 if hh == 0 else jnp.logical_not(m0)
                _row_to_col(cr_ref[0, hh], cc_ref, hh, S)
                qaug[hh] = _augment(jnp.where(sel, qf, 0.0), _split3(cc_ref[hh] - lse_ref[0, hh]), False, hh)

        k2 = k_ref[...]
        v2 = v_ref[...]
        zk = jnp.zeros_like(k2)
        kh = (jnp.where(m0, k2, zk), jnp.where(m0, zk, k2))
        kf = k2.astype(F32)
        rows_k = pl.ds(pl.multiple_of(j * bq, bq), bq)
        ka = [_augment(kf, _split3(-cc_ref[hh, rows_k, :]), True, hh) for hh in range(2)]
        row = lax.broadcasted_iota(jnp.int32, (bq, bq), 0)
        col = lax.broadcasted_iota(jnp.int32, (bq, bq), 1)

        def step(i, carry, masked):
            dk, dv, dc0, dc1 = carry
            dcs = [dc0, dc1]
            start = pl.multiple_of(i * bq, bq)
            q2 = q_ref[pl.ds(start, bq), :]
            do2 = do_ref[pl.ds(start, bq), :]
            prod = do2.astype(F32) * o_ref[pl.ds(start, bq), :].astype(F32)
            zq = jnp.zeros_like(q2)
            dq = jnp.zeros((bq, LANES), F32)
            for hh in range(2):
                sel = m0 if hh == 0 else jnp.logical_not(m0)
                qh = jnp.where(sel, q2, zq)
                doh = jnp.where(sel, do2, zq)
                delta = _head_sum(prod, hh)
                s = _dot_nt(qaug[hh, pl.ds(start, bq), :], ka[hh])
                if masked:
                    s = jnp.where(row >= col, s, NEG)
                p = jnp.exp(s)
                dp = _dot_nt(doh, v2)
                ds = p * (dp - delta)
                dcs[hh] = dcs[hh] - jnp.sum(ds, axis=0, keepdims=True)
                dr_scr[hh, pl.ds(start, bq), :] += jnp.sum(ds, axis=1, keepdims=True)
                dsb = ds.astype(BF16)
                dv = dv + _dot_tn(p.astype(BF16), doh)
                dk = dk + _dot_tn(dsb, qh)
                dq = dq + _dot(dsb, kh[hh])
            dq_scr[pl.ds(start, bq), :] += dq
            return dk, dv, dcs[0], dcs[1]

        zero = jnp.zeros((bq, LANES), F32)
        zrow = jnp.zeros((1, bq), F32)
        carry = step(j, (zero, zero, zrow, zrow), True)
        dk, dv, dc0, dc1 = lax.fori_loop(j + 1, nq, functools.partial(step, masked=False), carry)
        dk_ref[...] = (dk * SCALE).astype(BF16)
        dv_ref[...] = dv.astype(BF16)
        dc_ref[0, 0, 0] = dc0
        dc_ref[0, 1, 0] = dc1

        @pl.when(j == nq - 1)
        def _():
            dq_ref[...] = (dq_scr[...] * SCALE).astype(BF16)
            step_r = min(4 * LANES, S)
            for hh in range(2):
                for t in range(S // step_r):
                    dr_ref[0, hh, :, step_r * t:step_r * (t + 1)] = _col_to_row(dr_scr[hh, step_r * t:step_r * (t + 1), :])

    sblk = lambda c: pl.BlockSpec((S, LANES), lambda b, p, j, c=c: (b, c + p))
    kblk = lambda c: pl.BlockSpec((bq, LANES), lambda b, p, j, c=c: (b * nq + j, c + p))
    col_spec = pl.BlockSpec((1, 2, S, 1), lambda b, p, j: (b, p, 0, 0))
    row_spec = pl.BlockSpec((1, 2, 1, S), lambda b, p, j: (b, p, 0, 0))
    return _hosted_call(
        body, comm, f"fox_bwd_{li}", (B, NH // 2, nq),
        in_specs=[sblk(qc), kblk(kc), kblk(vc), sblk(0), sblk(0), row_spec, col_spec],
        out_specs=[sblk(0), kblk(0), kblk(0), pl.BlockSpec((1, 2, 1, 1, bq), lambda b, p, j: (b, p, j, 0, 0)),
                   row_spec],
        out_shape=[jax.ShapeDtypeStruct((T, D), BF16), jax.ShapeDtypeStruct((T, D), BF16),
                   jax.ShapeDtypeStruct((T, D), BF16), jax.ShapeDtypeStruct((B, NH, nq, 1, bq), F32),
                   jax.ShapeDtypeStruct((B, NH, 1, S), F32)],
        scratch=[pltpu.VMEM((S, LANES), F32), pltpu.VMEM((2, S, 1), F32), pltpu.VMEM((2, S, LANES), BF16),
                 pltpu.VMEM((2, S, 1), F32)],
        dims=("parallel", "parallel", "arbitrary"), operands=(proj, proj, proj, do, o, cum_row, lse))


def _swa_blocks(S):
    bq = min(512, S)
    return bq, S // bq, bq // LCH


def _dup_head(xw, kvl):
    m0 = _lane_iota() < HD
    a = jnp.where(m0 if kvl == 0 else jnp.logical_not(m0), xw, 0.0)
    return (a + pltpu.roll(a, HD, 1)).astype(BF16)


def _band(same_block):
    r = lax.broadcasted_iota(jnp.int32, (LCH, LCH), 0)
    c = lax.broadcasted_iota(jnp.int32, (LCH, LCH), 1)
    return (c <= r) if same_block else (c > r)


def _stack_heads(ref, rows, kvl):
    m0 = _lane_iota() < HD
    parts = []
    for ch in (2 * kvl, 2 * kvl + 1):
        x = ref[rows, LANES * ch:LANES * (ch + 1)]
        parts += [jnp.where(m0, x, jnp.zeros_like(x)), jnp.where(m0, jnp.zeros_like(x), x)]
    return jnp.concatenate(parts, axis=0)


def _stack_delta(do_ref, o_ref, rows, kvl, scale=None):
    parts = []
    for ch in (2 * kvl, 2 * kvl + 1):
        lanes = slice(LANES * ch, LANES * (ch + 1))
        prod = do_ref[rows, lanes].astype(F32) * o_ref[rows, lanes].astype(F32)
        parts += [_head_sum(prod, 0), _head_sum(prod, 1)]
    out = jnp.concatenate(parts, axis=0)
    return out if scale is None else out * scale


def _stack_cols(ref, rows, kvl):
    return jnp.concatenate([ref[0, 4 * kvl + t, rows, :] for t in range(4)], axis=0)


def _swa_fwd(proj, sinks, S, li):
    T = proj.shape[0]
    B = T // S
    bq, nq, nsub = _swa_blocks(S)
    nrow = S // LCH
    qc, zc, kc, vc = OFF_BQ // 512, OFF_BZ // 512, OFF_BK // LANES, OFF_BV // LANES

    def body(sk_ref, q_ref, z_ref, kp_ref, kc_ref, vp_ref, vc_ref, y_ref, o_ref, lse_ref):
        c, i = pl.program_id(0), pl.program_id(2)
        m0 = _lane_iota() < HD
        kw = jnp.concatenate([kp_ref[...].astype(F32), kc_ref[...].astype(F32)], axis=0)
        vw = jnp.concatenate([vp_ref[...].astype(F32), vc_ref[...].astype(F32)], axis=0)
        kd = (_dup_head(kw, 0), _dup_head(kw, 1))
        vd = (_dup_head(vw, 0), _dup_head(vw, 1))
        valid = jnp.concatenate([_band(False), _band(True)], axis=1)
        col = lax.broadcasted_iota(jnp.int32, (LCH, 2 * LCH), 1)
        valid_first = jnp.logical_and(valid, jnp.logical_or(col >= LCH, i > 0))
        valid4 = jnp.concatenate([valid] * 4, axis=0)
        valid4_first = jnp.concatenate([valid_first] * 4, axis=0)
        for r in range(nsub):
            rows = slice(LCH * r, LCH * (r + 1))
            msk = valid4_first if r == 0 else valid4
            for kvl in range(2):
                kwin = kd[kvl][LCH * r:LCH * (r + 2)]
                vwin = vd[kvl][LCH * r:LCH * (r + 2)]
                qs = _stack_heads(q_ref, rows, kvl)
                sink = jnp.concatenate([jnp.full((LCH, 1), sk_ref[8 * c + 4 * kvl + t], F32) for t in range(4)], axis=0)
                s = jnp.where(msk, _dot_nt(qs, kwin) * SCALE, NEG)
                m = jnp.maximum(jnp.max(s, axis=1, keepdims=True), sink)
                p = jnp.exp(s - m)
                l = jnp.sum(p, axis=1, keepdims=True) + jnp.exp(sink - m)
                os_ = _dot(p.astype(BF16), vwin) / l
                lse = m + jnp.log(l)
                for t in range(4):
                    lse_ref[0, 4 * kvl + t, rows, :] = lse[LCH * t:LCH * (t + 1)]
                for u in range(2):
                    lanes = slice(LANES * (2 * kvl + u), LANES * (2 * kvl + u + 1))
                    o2 = jnp.where(m0, os_[LCH * 2 * u:LCH * (2 * u + 1)], os_[LCH * (2 * u + 1):LCH * (2 * u + 2)])
                    z = z_ref[rows, lanes].astype(F32)
                    o_ref[rows, lanes] = o2.astype(BF16)
                    y_ref[rows, lanes] = (o2 * z * _sigmoid(z)).astype(BF16)

    wide = lambda cc: pl.BlockSpec((bq, 512), lambda c, b, i, cc=cc: (b * nq + i, cc + c))
    cur = lambda cc: pl.BlockSpec((bq, LANES), lambda c, b, i, cc=cc: (b * nq + i, cc + c))
    prev = lambda cc: pl.BlockSpec((LCH, LANES), lambda c, b, i, cc=cc: (b * nrow + jnp.maximum(i * nsub - 1, 0), cc + c))
    return pl.pallas_call(
        body, name=f"swa_fwd_{li}", grid=(2, B, nq),
        in_specs=[pl.BlockSpec(memory_space=pltpu.SMEM), wide(qc), wide(zc), prev(kc), cur(kc), prev(vc), cur(vc)],
        out_specs=[wide(0), wide(0), pl.BlockSpec((1, 8, bq, 1), lambda c, b, i: (b, c, i, 0))],
        out_shape=[jax.ShapeDtypeStruct((T, D), BF16), jax.ShapeDtypeStruct((T, D), BF16),
                   jax.ShapeDtypeStruct((B, NH, S, 1), F32)],
        compiler_params=_cparams(("parallel", "parallel", "parallel"), VMEM_LIMIT),
    )(sinks, proj, proj, proj, proj, proj, proj)


def _swa_bwd_dq(proj, do, o, lse, sinks, cos128, sin128, S, li):
    T = proj.shape[0]
    B = T // S
    bq, nq, nsub = _swa_blocks(S)
    nrow = S // LCH
    qc, kc, vc = OFF_BQ // 512, OFF_BK // LANES, OFF_BV // LANES

    def body(sk_ref, q_ref, do_ref, o_ref, lse_ref, kp_ref, kc_ref, vp_ref, vc_ref, cos_ref, sin_ref, dq_ref, dsk_ref):
        c, b, i = pl.program_id(0), pl.program_id(1), pl.program_id(2)

        @pl.when(jnp.logical_and(b == 0, i == 0))
        def _():
            dsk_ref[...] = jnp.zeros_like(dsk_ref)

        m0 = _lane_iota() < HD
        kw = jnp.concatenate([kp_ref[...].astype(F32), kc_ref[...].astype(F32)], axis=0)
        vw = jnp.concatenate([vp_ref[...].astype(F32), vc_ref[...].astype(F32)], axis=0)
        kd = (_dup_head(kw, 0), _dup_head(kw, 1))
        vd = (_dup_head(vw, 0), _dup_head(vw, 1))
        valid = jnp.concatenate([_band(False), _band(True)], axis=1)
        col = lax.broadcasted_iota(jnp.int32, (LCH, 2 * LCH), 1)
        valid_first = jnp.logical_and(valid, jnp.logical_or(col >= LCH, i > 0))
        dsk = [jnp.zeros((1, 1), F32) for _ in range(8)]
        valid4 = jnp.concatenate([valid] * 4, axis=0)
        valid4_first = jnp.concatenate([valid_first] * 4, axis=0)
        for r in range(nsub):
            rows = slice(LCH * r, LCH * (r + 1))
            msk = valid4_first if r == 0 else valid4
            for kvl in range(2):
                kwin = kd[kvl][LCH * r:LCH * (r + 2)]
                vwin = vd[kvl][LCH * r:LCH * (r + 2)]
                qs = _stack_heads(q_ref, rows, kvl)
                dos = _stack_heads(do_ref, rows, kvl)
                delta = _stack_delta(do_ref, o_ref, rows, kvl)
                lse = _stack_cols(lse_ref, rows, kvl)
                sink = jnp.concatenate([jnp.full((LCH, 1), sk_ref[8 * c + 4 * kvl + t], F32) for t in range(4)], axis=0)
                s = jnp.where(msk, _dot_nt(qs, kwin) * SCALE, NEG)
                p = jnp.exp(s - lse)
                ds = p * (_dot_nt(dos, vwin) - delta)
                dqs = _dot(ds.astype(BF16), kwin) * SCALE
                dsink = jnp.exp(sink - lse) * delta
                for t in range(4):
                    hl = 4 * kvl + t
                    dsk[hl] = dsk[hl] - jnp.sum(dsink[LCH * t:LCH * (t + 1)], axis=0, keepdims=True)
                for u in range(2):
                    lanes = slice(LANES * (2 * kvl + u), LANES * (2 * kvl + u + 1))
                    dq2 = jnp.where(m0, dqs[LCH * 2 * u:LCH * (2 * u + 1)], dqs[LCH * (2 * u + 1):LCH * (2 * u + 2)])
                    dq2 = dq2 * cos_ref[rows, :] - _rot_half(dq2) * sin_ref[rows, :]
                    dq_ref[rows, lanes] = dq2.astype(BF16)
        for hl in range(8):
            dsk_ref[0, hl:hl + 1, :] += jnp.broadcast_to(dsk[hl], (1, LANES))

    wide = lambda cc: pl.BlockSpec((bq, 512), lambda c, b, i, cc=cc: (b * nq + i, cc + c))
    cur = lambda cc: pl.BlockSpec((bq, LANES), lambda c, b, i, cc=cc: (b * nq + i, cc + c))
    prev = lambda cc: pl.BlockSpec((LCH, LANES), lambda c, b, i, cc=cc: (b * nrow + jnp.maximum(i * nsub - 1, 0), cc + c))
    pos = pl.BlockSpec((bq, LANES), lambda c, b, i: (i, 0))
    return pl.pallas_call(
        body, name=f"swa_bwd_dq_{li}", grid=(2, B, nq),
        in_specs=[pl.BlockSpec(memory_space=pltpu.SMEM), wide(qc), wide(0), wide(0),
                  pl.BlockSpec((1, 8, bq, 1), lambda c, b, i: (b, c, i, 0)),
                  prev(kc), cur(kc), prev(vc), cur(vc), pos, pos],
        out_specs=[wide(0), pl.BlockSpec((1, 8, LANES), lambda c, b, i: (c, 0, 0))],
        out_shape=[jax.ShapeDtypeStruct((T, D), BF16), jax.ShapeDtypeStruct((2, 8, LANES), F32)],
        compiler_params=_cparams(("arbitrary", "arbitrary", "arbitrary"), VMEM_LIMIT),
    )(sinks, proj, do, o, lse, proj, proj, proj, proj, cos128, sin128)


def _swa_bwd_dkv(proj, do, o, lse, cos128, sin128, S, li):
    T = proj.shape[0]
    B = T // S
    bk, nk, nsub = _swa_blocks(S)
    nrow = S // LCH
    qc, kc, vc = OFF_BQ // 512, OFF_BK // LANES, OFF_BV // LANES

    def body(q_ref, qn_ref, do_ref, don_ref, o_ref, on_ref, lse_ref, lsen_ref, k_ref, v_ref, cos_ref, sin_ref,
             dk_ref, dv_ref):
        j = pl.program_id(2)
        m0 = _lane_iota() < HD
        has_next = (j < nk - 1).astype(F32)
        kf = k_ref[...].astype(F32)
        vf = v_ref[...].astype(F32)
        kd = (_dup_head(kf, 0), _dup_head(kf, 1))
        vd = (_dup_head(vf, 0), _dup_head(vf, 1))
        lane = _lane_iota()

        def stat_rows(lse_r, do_r, o_r, rows, scale):
            a_lse = jnp.zeros((rows, LANES), F32)
            a_del = jnp.zeros((rows, LANES), F32)
            for ch in range(4):
                lanes = slice(LANES * ch, LANES * (ch + 1))
                prod = do_r[:, lanes].astype(F32) * o_r[:, lanes].astype(F32)
                for hh in range(2):
                    h = 2 * ch + hh
                    a_lse = jnp.where(lane == h, lse_r[0, h], a_lse)
                    a_del = jnp.where(lane == h, _head_sum(prod, hh), a_del)
            if scale is not None:
                a_del = a_del * scale
            return a_lse.T, a_del.T

        lse_t, del_t = stat_rows(lse_ref, do_ref, o_ref, bk, None)
        lsen_t, deln_t = stat_rows(lsen_ref, don_ref, on_ref, LCH, has_next)
        r_ = lax.broadcasted_iota(jnp.int32, (LCH, LCH), 0)
        c_ = lax.broadcasted_iota(jnp.int32, (LCH, LCH), 1)
        masks4 = (jnp.concatenate([r_ <= c_] * 4, axis=1), jnp.concatenate([r_ > c_] * 4, axis=1))
        for kr in range(nsub):
            krows = slice(LCH * kr, LCH * (kr + 1))
            dk = jnp.zeros((LCH, LANES), F32)
            dv = jnp.zeros((LCH, LANES), F32)
            for dq_blk in range(2):
                rq = kr + dq_blk
                nxt = rq == nsub
                qrows = slice(0, LCH) if nxt else slice(LCH * rq, LCH * (rq + 1))
                qr, dor = (qn_ref, don_ref) if nxt else (q_ref, do_ref)
                lt, dt_ = (lsen_t, deln_t) if nxt else (lse_t, del_t)
                for kvl in range(2):
                    qs = _stack_heads(qr, qrows, kvl)
                    dos = _stack_heads(dor, qrows, kvl)
                    if nxt:
                        dos = (dos.astype(F32) * has_next).astype(BF16)
                    lse_row = jnp.concatenate([lt[4 * kvl + t:4 * kvl + t + 1, qrows] for t in range(4)], axis=1)
                    del_row = jnp.concatenate([dt_[4 * kvl + t:4 * kvl + t + 1, qrows] for t in range(4)], axis=1)
                    st = jnp.where(masks4[dq_blk], _dot_nt(kd[kvl][krows], qs) * SCALE, NEG)
                    pt = jnp.exp(st - lse_row)
                    dst = pt * (_dot_nt(vd[kvl][krows], dos) - del_row)
                    dvc = _dot(pt.astype(BF16), dos)
                    dkc = _dot(dst.astype(BF16), qs) * SCALE
                    own = m0 if kvl == 0 else jnp.logical_not(m0)
                    dv = dv + jnp.where(own, dvc + pltpu.roll(dvc, HD, 1), 0.0)
                    dk = dk + jnp.where(own, dkc + pltpu.roll(dkc, HD, 1), 0.0)
            dk = dk * cos_ref[krows, :] - _rot_half(dk) * sin_ref[krows, :]
            dk_ref[krows, :] = dk.astype(BF16)
            dv_ref[krows, :] = dv.astype(BF16)

    wide = lambda cc: pl.BlockSpec((bk, 512), lambda c, b, j, cc=cc: (b * nk + j, cc + c))
    nxt = lambda cc: pl.BlockSpec((LCH, 512), lambda c, b, j, cc=cc: (b * nrow + jnp.minimum((j + 1) * nsub, nrow - 1), cc + c))
    cur = lambda cc: pl.BlockSpec((bk, LANES), lambda c, b, j, cc=cc: (b * nk + j, cc + c))
    pos = pl.BlockSpec((bk, LANES), lambda c, b, j: (j, 0))
    return pl.pallas_call(
        body, name=f"swa_bwd_dkv_{li}", grid=(2, B, nk),
        in_specs=[wide(qc), nxt(qc), wide(0), nxt(0), wide(0), nxt(0),
                  pl.BlockSpec((1, 8, bk, 1), lambda c, b, j: (b, c, j, 0)),
                  pl.BlockSpec((1, 8, LCH, 1), lambda c, b, j: (b, c, jnp.minimum((j + 1) * nsub, nrow - 1), 0)),
                  cur(kc), cur(vc), pos, pos],
        out_specs=[cur(0), cur(0)],
        out_shape=[jax.ShapeDtypeStruct((T, 2 * LANES), BF16), jax.ShapeDtypeStruct((T, 2 * LANES), BF16)],
        compiler_params=_cparams(("parallel", "parallel", "parallel"), VMEM_LIMIT),
    )(proj, proj, do, do, o, o, lse, lse, proj, proj, cos128, sin128)


HALO = 16


def _shift_matrices():
    r = lax.broadcasted_iota(jnp.int32, (3 * LCH, LCH + HALO), 0)
    c = lax.broadcasted_iota(jnp.int32, (3 * LCH, LCH + HALO), 1)
    t, d = r % LCH, r // LCH + 1
    return (c == HALO + t - d).astype(BF16), (c == t + d).astype(BF16)


def _ssm_chunk_pre(prev16, cur16, first, sdn_ref, cw_ref, cb_ref, ps, dtb, alog):
    ext16 = jnp.concatenate([jnp.where(first, jnp.zeros_like(prev16), prev16), cur16], axis=0)
    sh = _dot(sdn_ref[...], ext16)
    pre = cb_ref[...] + cw_ref[3:4, :] * cur16.astype(F32)
    for d in range(1, 4):
        pre = pre + cw_ref[3 - d:4 - d, :] * sh[LCH * (d - 1):LCH * d]
    sg = _sigmoid(pre)
    dt = _softplus(ps + dtb)
    a = -jnp.exp(alog)
    r = lax.broadcasted_iota(jnp.int32, (LCH, LCH), 0)
    c = lax.broadcasted_iota(jnp.int32, (LCH, LCH), 1)
    acum = _dot_hi((r >= c).astype(F32), dt * a)
    return pre, sg, dt, a, acum, sh


def _expand_matrix():
    r = lax.broadcasted_iota(jnp.int32, (3 * LANES, D), 0)
    c = lax.broadcasted_iota(jnp.int32, (3 * LANES, D), 1)
    return ((r % LANES) == c // HD).astype(BF16)


def _expand_heads(v, ex_ref):
    return _dot(jnp.concatenate(_split3(v), axis=1).astype(BF16), ex_ref[...])


def _decay(acum, acum_t, h):
    r = lax.broadcasted_iota(jnp.int32, (LCH, LCH), 0)
    c = lax.broadcasted_iota(jnp.int32, (LCH, LCH), 1)
    causal = r >= c
    seg = acum[:, h:h + 1] - acum_t[h:h + 1, :]
    return jnp.where(causal, jnp.exp(jnp.where(causal, seg, 0.0)), 0.0)


def _ssm_pair_fwd(p, x, dt_x, acum, acum_t, e_x, w_x, cd, cb_g, b_g, c_g, hprev, dsk_ref):
    m0 = _lane_iota() < HD
    lanes = slice(LANES * p, LANES * (p + 1))
    x2 = x[:, lanes]
    dt2 = dt_x[:, lanes]
    xdt2 = x2 * dt2
    xdtb = xdt2.astype(BF16)
    lms, ms, yds = [], [], []
    for hh in range(2):
        lm = _decay(acum, acum_t, 2 * p + hh)
        mm = cb_g * lm
        lms.append(lm)
        ms.append(mm)
        yds.append(_dot(mm.astype(BF16), xdtb))
    yd2 = jnp.where(m0, yds[0], yds[1])
    w2 = w_x[:, lanes]
    xw = (xdt2 * w2).astype(BF16)
    s2 = _dot_tn(xw, b_g)
    z2 = _dot_nt(c_g, hprev.astype(BF16))
    e2 = e_x[:, lanes]
    rowsel = lax.broadcasted_iota(jnp.int32, (LANES, 1), 0) < HD
    cdcol = jnp.where(rowsel, cd[:, 2 * p:2 * p + 1], cd[:, 2 * p + 1:2 * p + 2])
    y2 = yd2 + z2 * e2 + dsk_ref[:, lanes] * x2
    return dict(x2=x2, dt2=dt2, xdt2=xdt2, xdtb=xdtb, lms=lms, ms=ms, yd2=yd2, w2=w2, xw=xw, s2=s2, z2=z2, e2=e2,
                cdcol=cdcol, y2=y2)


def _ssm_specs(S, rev):
    nc = S // LCH
    ch = (lambda c: nc - 1 - c) if rev else (lambda c: c)
    prev = pl.BlockSpec((HALO, 2 * D), lambda b, c: (jnp.maximum(b * (S // HALO) + ch(c) * (LCH // HALO) - 1, 0), 0))
    cur = pl.BlockSpec((LCH, 2 * D), lambda b, c: (b * nc + ch(c), 0))
    zed = pl.BlockSpec((LCH, D), lambda b, c: (b * nc + ch(c), OFF_AZ // D))
    row = pl.BlockSpec((LCH, D), lambda b, c: (b * nc + ch(c), 0))
    psb = pl.BlockSpec((LCH, LANES), lambda b, c: (b * nc + ch(c), 0))
    hpb = pl.BlockSpec((1, 1, NH // 2, LANES, NST), lambda b, c: (b, ch(c), 0, 0, 0))
    const = lambda r, w: pl.BlockSpec((r, w), lambda b, c: (0, 0))
    return nc, prev, cur, zed, row, psb, hpb, const


def _ssm_fwd(proj, ps, cw, cb, dtb, alog, dsk, nw, S, li):
    T = proj.shape[0]
    B = T // S
    nc, prev, cur, zed, row, psb, hpb, const = _ssm_specs(S, False)

    def body(prev_ref, cur_ref, z_ref, ps_ref, sdn_ref, ex_ref, cw_ref, cb_ref, dtb_ref, alog_ref, dsk_ref, nw_ref,
             ya_ref, hp_ref, h_scr):
        c = pl.program_id(1)

        @pl.when(c == 0)
        def _():
            h_scr[...] = jnp.zeros_like(h_scr)

        pre, sg, dt, a, acum, _ = _ssm_chunk_pre(prev_ref[...], cur_ref[...], c == 0, sdn_ref, cw_ref, cb_ref,
                                                 ps_ref[...], dtb_ref[...], alog_ref[...])
        act = pre * sg
        acum_t = acum.T
        last = acum[LCH - 1:LCH, :]
        cd = jnp.exp(last)
        dt, e_all, w_all = (_expand_heads(v, ex_ref) for v in (dt, jnp.exp(acum), jnp.exp(last - acum)))
        x = act[:, :D]
        for g in range(NGRP):
            b_g = act[:, D + NST * g:D + NST * (g + 1)].astype(BF16)
            c_g = act[:, D + NGRP * NST + NST * g:D + NGRP * NST + NST * (g + 1)].astype(BF16)
            cb_g = _dot_nt(c_g, b_g)
            ygs = []
            for p in (2 * g, 2 * g + 1):
                hprev = h_scr[p]
                hp_ref[0, 0, p] = hprev
                f = _ssm_pair_fwd(p, x, dt, acum, acum_t, e_all, w_all, cd, cb_g, b_g, c_g, hprev, dsk_ref)
                h_scr[p] = hprev * f["cdcol"] + f["s2"]
                z2 = z_ref[:, LANES * p:LANES * (p + 1)].astype(F32)
                ygs.append(f["y2"] * z2 * _sigmoid(z2))
            yg = jnp.concatenate(ygs, axis=1)
            r = lax.rsqrt(jnp.mean(yg * yg, axis=1, keepdims=True) + EPS)
            ya_ref[:, 2 * LANES * g:2 * LANES * (g + 1)] = (yg * r * nw_ref[:, 2 * LANES * g:2 * LANES * (g + 1)]).astype(BF16)

    return pl.pallas_call(
        body, name=f"ssm_fwd_{li}", grid=(B, nc),
        in_specs=[prev, cur, zed, psb, const(3 * LCH, LCH + HALO), const(3 * LANES, D), const(4, 2 * D),
                  const(1, 2 * D), const(1, LANES), const(1, LANES), const(1, D), const(1, D)],
        out_specs=[row, hpb],
        out_shape=[jax.ShapeDtypeStruct((T, D), BF16), jax.ShapeDtypeStruct((B, nc, NH // 2, LANES, NST), F32)],
        scratch_shapes=[pltpu.VMEM((NH // 2, LANES, NST), F32)],
        compiler_params=_cparams(("arbitrary", "arbitrary"), VMEM_LIMIT),
    )(proj, proj, proj, ps, _shift_matrices()[0], _expand_matrix(), cw, cb, dtb, alog, dsk, nw)


def _ssm_bwd(proj, ps, hp, dya, cw, cb, dtb, alog, dsk, nw, S, li, comm=None):
    T = proj.shape[0]
    B = T // S
    nc, prev, cur, zed, row, psb, hpb, const = _ssm_specs(S, True)

    def body(prev_ref, cur_ref, z_ref, ps_ref, hp_ref, dy_ref, sdn_ref, sup_ref, ex_ref, cw_ref, cb_ref, dtb_ref,
             alog_ref, dsk_ref, nw_ref, dxbc_ref, dz_ref, dps_ref, pgw_ref, pg1_ref, pgh_ref, dh_scr, dhead, dact):
        b, cc = pl.program_id(0), pl.program_id(1)
        c = nc - 1 - cc

        @pl.when(jnp.logical_and(b == 0, cc == 0))
        def _():
            pgw_ref[...] = jnp.zeros_like(pgw_ref)
            pg1_ref[...] = jnp.zeros_like(pg1_ref)
            pgh_ref[...] = jnp.zeros_like(pgh_ref)

        @pl.when(cc == 0)
        def _():
            dh_scr[...] = jnp.zeros_like(dh_scr)
            dhead[...] = jnp.zeros_like(dhead)

        psv = ps_ref[...]
        cur16 = cur_ref[...]
        pre, sg, dt, a, acum, sh = _ssm_chunk_pre(prev_ref[...], cur16, c == 0, sdn_ref, cw_ref, cb_ref, psv,
                                                  dtb_ref[...], alog_ref[...])
        act = pre * sg
        acum_t = acum.T
        last = acum[LCH - 1:LCH, :]
        w_all = jnp.exp(last - acum)
        cd = jnp.exp(last)
        dt_x, e_x, w_x = (_expand_heads(v, ex_ref) for v in (dt, jnp.exp(acum), w_all))
        x = act[:, :D]
        lane = _lane_iota()
        m0 = lane < HD
        head_row = lax.broadcasted_iota(jnp.int32, (LANES, 1), 0)
        rowsel = head_row < HD
        is_last_row = lax.broadcasted_iota(jnp.int32, (LCH, 1), 0) == LCH - 1
        dacum_all = jnp.zeros((LCH, LANES), F32)
        dacum_t = jnp.zeros((LANES, LCH), F32)
        ddt_all = jnp.zeros((LCH, LANES), F32)
        dd_row = jnp.zeros((1, LANES), F32)
        for g in range(NGRP):
            b_g = act[:, D + NST * g:D + NST * (g + 1)].astype(BF16)
            c_g = act[:, D + NGRP * NST + NST * g:D + NGRP * NST + NST * (g + 1)].astype(BF16)
            cb_g = _dot_nt(c_g, b_g)
            pairs = (2 * g, 2 * g + 1)
            fs, hps, zs, ygs = [], [], [], []
            for p in pairs:
                hprev = hp_ref[0, 0, p]
                f = _ssm_pair_fwd(p, x, dt_x, acum, acum_t, e_x, w_x, cd, cb_g, b_g, c_g, hprev, dsk_ref)
                z2 = z_ref[:, LANES * p:LANES * (p + 1)].astype(F32)
                fs.append(f)
                hps.append(hprev)
                zs.append(z2)
                ygs.append(f["y2"] * z2 * _sigmoid(z2))
            gl = slice(2 * LANES * g, 2 * LANES * (g + 1))
            yg = jnp.concatenate(ygs, axis=1)
            r = lax.rsqrt(jnp.mean(yg * yg, axis=1, keepdims=True) + EPS)
            dyn = dy_ref[:, gl].astype(F32)
            gg = dyn * nw_ref[:, gl]
            dyg = r * gg - yg * (r * r * r) * jnp.mean(gg * yg, axis=1, keepdims=True)
            pg1_ref[0:1, gl] += jnp.sum(dyn * yg * r, axis=0, keepdims=True)
            dg_g = jnp.zeros((LCH, LCH), F32)
            db_g = jnp.zeros((LCH, NST), F32)
            dc_g = jnp.zeros((LCH, NST), F32)
            for idx, p in enumerate(pairs):
                f, hprev, z2 = fs[idx], hps[idx], zs[idx]
                lanes = slice(LANES * p, LANES * (p + 1))
                dyg2 = dyg[:, LANES * idx:LANES * (idx + 1)]
                sgz = _sigmoid(z2)
                dy2 = dyg2 * z2 * sgz
                dz_ref[:, lanes] = (dyg2 * f["y2"] * sgz * (1.0 + z2 * (1.0 - sgz))).astype(BF16)
                x2, dt2, xdt2, xdtb, w2, e2, z2m = f["x2"], f["dt2"], f["xdt2"], f["xdtb"], f["w2"], f["e2"], f["z2"]
                dx2 = dsk_ref[:, lanes] * dy2
                dyx = dy2 * x2
                dxdt2 = jnp.zeros((LCH, LANES), F32)
                diag_cols = []
                for hh in range(2):
                    sel = m0 if hh == 0 else jnp.logical_not(m0)
                    dyb = jnp.where(sel, dy2, 0.0).astype(BF16)
                    dm = _dot_nt(dyb, xdtb)
                    dg_g = dg_g + dm * f["lms"][hh]
                    dxdt2 = dxdt2 + _dot_tn(f["ms"][hh].astype(BF16), dyb)
                    em = dm * f["ms"][hh]
                    diag_cols.append(jnp.sum(em, axis=1, keepdims=True))
                    dacum_t = dacum_t - jnp.where(head_row == 2 * p + hh, jnp.sum(em, axis=0, keepdims=True), 0.0)
                dz2m = dy2 * e2
                t_off = dz2m * z2m
                dc_g = dc_g + _dot(dz2m.astype(BF16), hprev.astype(BF16))
                dhprev = _dot_tn(dz2m.astype(BF16), c_g)
                dhn = dh_scr[p]
                dhnb = dhn.astype(BF16)
                dhprev = dhprev + dhn * f["cdcol"]
                t_h = dhn * hprev
                dxw2 = _dot_nt(b_g, dhnb)
                db_g = db_g + _dot(f["xw"], dhnb)
                dxdt2 = dxdt2 + dxw2 * w2
                t_w = dxw2 * xdt2
                dx2 = dx2 + dxdt2 * dt2
                t_dt = dxdt2 * x2
                for hh in range(2):
                    h = 2 * p + hh
                    onehot = (lane == h).astype(F32)
                    w_col = w_all[:, h:h + 1]
                    dw_col = _head_sum(t_w, hh) * w_col
                    rs = rowsel if hh == 0 else jnp.logical_not(rowsel)
                    dlast = (jnp.sum(jnp.where(rs, t_h, 0.0), keepdims=True) * cd[:, h:h + 1]
                             + jnp.sum(dw_col, keepdims=True))
                    dacum_col = diag_cols[hh] + _head_sum(t_off, hh) - dw_col + jnp.where(is_last_row, dlast, 0.0)
                    dacum_all = dacum_all + dacum_col * onehot
                    ddt_all = ddt_all + _head_sum(t_dt, hh) * onehot
                    sel = m0 if hh == 0 else jnp.logical_not(m0)
                    dd_row = dd_row + jnp.sum(jnp.where(sel, dyx, 0.0), keepdims=True) * onehot
                dh_scr[p] = dhprev
                dact[:, lanes] = dx2
            dgb = dg_g.astype(BF16)
            dc_g = dc_g + _dot(dgb, b_g)
            db_g = db_g + _dot_tn(dgb, c_g)
            dact[:, D + NST * g:D + NST * (g + 1)] = db_g
            dact[:, D + NGRP * NST + NST * g:D + NGRP * NST + NST * (g + 1)] = dc_g
        rr = lax.broadcasted_iota(jnp.int32, (LCH, LCH), 0)
        cc2 = lax.broadcasted_iota(jnp.int32, (LCH, LCH), 1)
        dadt = _dot_hi((cc2 >= rr).astype(F32), dacum_all + dacum_t.T)
        ddt_all = ddt_all + dadt * a
        heads = lane < NH
        da = jnp.sum(dadt * dt, axis=0, keepdims=True)
        dr = jnp.where(heads, ddt_all * _sigmoid(psv + dtb_ref[...]), 0.0)
        dps_ref[...] = dr
        pgh_ref[0:1, :] += jnp.sum(dr, axis=0, keepdims=True)
        pgh_ref[1:2, :] += jnp.where(heads, da * a, 0.0)
        pgh_ref[2:3, :] += dd_row
        dpre = dact[...] * sg * (1.0 + pre * (1.0 - sg))
        extd = jnp.concatenate([dpre, dhead[...]], axis=0)
        hi = extd.astype(BF16)
        lo = (extd - hi.astype(F32)).astype(BF16)
        up = _dot(sup_ref[...], hi) + _dot(sup_ref[...], lo)
        du = cw_ref[3:4, :] * dpre
        pgw_ref[3:4, :] += jnp.sum(dpre * cur16.astype(F32), axis=0, keepdims=True)
        for d in range(1, 4):
            du = du + cw_ref[3 - d:4 - d, :] * up[LCH * (d - 1):LCH * d]
            pgw_ref[3 - d:4 - d, :] += jnp.sum(dpre * sh[LCH * (d - 1):LCH * d], axis=0, keepdims=True)
        pgw_ref[4:5, :] += jnp.sum(dpre, axis=0, keepdims=True)
        dxbc_ref[...] = du.astype(BF16)
        dhead[...] = dpre[0:HALO, :]

    xbc_out = pl.BlockSpec((LCH, 2 * D), lambda b, c: (b * nc + nc - 1 - c, 0))
    acc = lambda w: pl.BlockSpec((8, w), lambda b, c: (0, 0))
    sdn, sup = _shift_matrices()
    return _hosted_call(
        body, comm, f"ssm_bwd_{li}", (B, nc),
        in_specs=[prev, cur, zed, psb, hpb, row, const(3 * LCH, LCH + HALO), const(3 * LCH, LCH + HALO),
                  const(3 * LANES, D), const(4, 2 * D), const(1, 2 * D), const(1, LANES), const(1, LANES),
                  const(1, D), const(1, D)],
        out_specs=[xbc_out, row, psb, acc(2 * D), acc(D), acc(LANES)],
        out_shape=[jax.ShapeDtypeStruct((T, 2 * D), BF16), jax.ShapeDtypeStruct((T, D), BF16),
                   jax.ShapeDtypeStruct((T, LANES), F32), jax.ShapeDtypeStruct((8, 2 * D), F32),
                   jax.ShapeDtypeStruct((8, D), F32), jax.ShapeDtypeStruct((8, LANES), F32)],
        scratch=[pltpu.VMEM((NH // 2, LANES, NST), F32), pltpu.VMEM((HALO, 2 * D), F32),
                 pltpu.VMEM((LCH, 2 * D), F32)],
        dims=("arbitrary", "arbitrary"),
        operands=(proj, proj, proj, ps, hp, dya, sdn, sup, _expand_matrix(), cw, cb, dtb, alog, dsk, nw))


def _lane_row(v, offset):
    return jnp.pad(v.astype(F32), (offset, LANES - offset - v.shape[0]))[None]


def _pack_rows(arrays):
    parts = []
    for a in arrays:
        flat = a.reshape(-1).astype(F32)
        pad = (-flat.shape[0]) % LANES
        parts.append(jnp.pad(flat, (0, pad)))
    flat = jnp.concatenate(parts)
    pad = (-flat.shape[0]) % (8 * LANES)
    return jnp.pad(flat, (0, pad)).reshape(-1, LANES)


def _unpack_rows(pack, shapes):
    flat = pack.reshape(-1)
    out, pos = [], 0
    for shp in shapes:
        n = math.prod(shp)
        out.append(flat[pos:pos + n].reshape(shp))
        pos += n + (-n) % LANES
    return out


def _split_w_in(blocks):
    def cols(a, b):
        out = []
        for d in range(NDEV):
            lo, hi = max(a, d * NSH), min(b, (d + 1) * NSH)
            if lo < hi:
                out.append(blocks[d, :, lo - d * NSH:hi - d * NSH])
        return out

    main = jnp.concatenate(cols(0, 3072) + cols(3088, 4112) + cols(4624, 5648) + cols(5648, 8720)
                           + cols(8736, 12832) + cols(4112, 4624), axis=1)
    small = jnp.concatenate(cols(3072, 3088) + cols(8720, 8736) + [jnp.zeros((D, LANES - 2 * NH), blocks.dtype)],
                            axis=1)
    return main, small


def _w_in_blocks(dw, ds, r0, r1):
    xbc, az, bq, bz, cq, ck, cv, cz, gates, bk, bv = dw
    order = [xbc, az, ds[:, 0:NH], bq, bk, bv, bz, cq, ck, cv, ds[:, NH:2 * NH], cz, gates]
    blocks, pos = [[] for _ in range(NDEV)], 0
    for seg in order:
        w = seg.shape[1]
        for d in range(NDEV):
            lo, hi = max(pos, d * NSH), min(pos + w, (d + 1) * NSH)
            if lo < hi:
                blocks[d].append(seg[r0:r1, lo - pos:hi - pos])
        pos += w
    return jnp.stack([jnp.concatenate(b, axis=1) for b in blocks])


def kernel(x, norm_w, w_in, conv_w, conv_b, dt_bias, a_log, d_skip, ssm_norm_w, sinks, f_bias, gate_bias, w_proj, w_out, final_norm_w, loss_target, m_norm_w, m_w_in, m_conv_w, m_conv_b, m_dt_bias, m_a_log, m_d_skip, m_ssm_norm_w, m_sinks, m_f_bias, m_gate_bias, m_w_proj, m_w_out, m_final_norm_w, v_norm_w, v_w_in, v_conv_w, v_conv_b, v_dt_bias, v_a_log, v_d_skip, v_ssm_norm_w, v_sinks, v_f_bias, v_gate_bias, v_w_proj, v_w_out, v_final_norm_w):
    Bl, S, _ = x.shape
    T = Bl * S
    depth = norm_w.shape[0]
    me = 4 * lax.axis_index("x") + 2 * lax.axis_index("y") + lax.axis_index("c")
    csh, gsh = conv_w.shape[2], gate_bias.shape[2]

    def gather_plan(l):
        small = jnp.concatenate([conv_w[l].reshape(-1), gate_bias[l].reshape(-1)]).reshape(-1, LANES)
        return _Comm("gather", [w_in[l].astype(BF16), w_proj[l].astype(BF16), w_out[l].astype(BF16), small])

    def unpack_weights(res):
        g_win, g_wp, g_wo, g_small = res
        flat = g_small.reshape(NDEV, -1)
        return (_split_w_in(g_win),
                g_wp.transpose(1, 0, 2, 3).reshape(3, D, D),
                g_wo.reshape(D, D),
                flat[:, :4 * csh].reshape(NDEV, 4, csh).transpose(1, 0, 2).reshape(4, 2 * D),
                flat[:, 4 * csh:].reshape(NDEV, 3, gsh).transpose(1, 0, 2).reshape(3, D))

    def scatter_plan(gw_in_blocks=None, gw_p=None, gw_o=None):
        arrays = [] if gw_in_blocks is None else [gw_in_blocks]
        if gw_p is not None:
            arrays += [gw_p.astype(BF16).reshape(3, NDEV, D // NDEV, D).transpose(1, 0, 2, 3),
                       gw_o.astype(BF16).reshape(NDEV, D // NDEV, D)]
        return _Comm("scatter", arrays)

    pos = jnp.arange(S, dtype=F32)
    inv_freq = ROPE_THETA ** (-jnp.arange(0, HD, 2, dtype=F32) / HD)
    ang = pos[:, None] * inv_freq[None, :]
    cos128 = jnp.tile(jnp.cos(ang), (1, 4))
    sign = jnp.where((jnp.arange(LANES) % HD) < HD // 2, -1.0, 1.0).astype(F32)
    sin128 = jnp.tile(jnp.sin(ang), (1, 4)) * sign[None, :]

    x2 = x.reshape(T, D)
    tgt2 = loss_target.reshape(T, D)

    saved = []
    xcur = x2
    weights = [None] * depth
    weights[0] = unpack_weights(_gather_two_level(gather_plan(0).arrays, "gather_weights_0"))
    for l in range(depth):
        (wmain, wsmall), wp_l, wo_l, cw_l, gb_l = weights[l]
        proj, ps, h_t = _inproj_fwd(xcur, norm_w[l][None], wmain, wsmall, cos128, sin128, S, l)
        dtb = _lane_row(dt_bias[l], 0)
        alog = _lane_row(a_log[l], 0)
        fb = _lane_row(f_bias[l], NH)
        dsk = jnp.repeat(d_skip[l], HD)[None]
        ya, hp = _ssm_fwd(proj, ps, cw_l, conv_b[l][None], dtb, alog, dsk, ssm_norm_w[l][None], S, l)
        yb, ob, lse_b = _swa_fwd(proj, sinks[l], S, l)
        cum = _fox_cum(ps, fb, S, l)
        cumh = cum[:, NH:2 * NH].reshape(Bl, S, NH).transpose(0, 2, 1)
        cum_row = cumh[:, :, None, :]
        comm = gather_plan(l + 1) if l + 1 < depth else None
        res = _fox_fwd(proj, cum_row, S, l, comm)
        yc, oc, lse_c = res[:3]
        if comm is not None:
            weights[l + 1] = unpack_weights(res[3:])
        xnext, br, y_t = _merge_fwd(ya, yb, yc, proj, gb_l, wp_l, wo_l, xcur, l)
        saved.append(dict(x=xcur, wmain=wmain, wsmall=wsmall, proj=proj, ps=ps, h_t=h_t, dtb=dtb, alog=alog, fb=fb,
                          dsk=dsk, hp=hp, ob=ob, lse_b=lse_b, cum_row=cum_row, oc=oc, lse_c=lse_c, br=br, y_t=y_t))
        xcur = xnext

    dx, dx16, st = _final_loss(xcur, tgt2, final_norm_w[None])
    loss_part = st[2, 0]
    g_final = st[0]

    gsm = {k: [None] * depth for k in ("norm_w", "conv_w", "conv_b", "dt_bias", "a_log", "d_skip", "ssm_norm_w",
                                      "sinks", "f_bias", "gate_bias")}
    parts = [None] * depth
    pending = None
    for l in reversed(range(depth)):
        sv = saved[l]
        proj, ps = sv["proj"], sv["ps"]
        _, wp_l, wo_l, cw_l, gb_l = weights[l]
        dbr, dgates, merged_t, dgb, dy_a, do_b, dbz, do_c, dcz = _merge_bwd(dx16, wo_l, wp_l, sv["br"], proj, gb_l,
                                                                            sv["ob"], sv["oc"], l)
        g_wo = _matmul(merged_t, dx16, BF16, f"dwout_{l}")
        g_wp = _matmul_batched(sv["y_t"], dbr, BF16, f"dwproj_{l}")
        gsm["gate_bias"][l] = dgb[0:3]
        hosted = ([] if pending is None else pending.arrays) + (scatter_plan(None, g_wp, g_wo).arrays if l == 0 else [])
        res = _ssm_bwd(proj, ps, sv["hp"], dy_a, cw_l, conv_b[l][None], sv["dtb"], sv["alog"], sv["dsk"],
                       ssm_norm_w[l][None], S, l, _Comm("scatter", hosted) if hosted else None)
        dxbc, daz, dps_a, pgw, pg1, pgh = res[:6]
        if pending is not None:
            parts[l + 1] = res[6:9]
        if l == 0:
            parts_po = res[len(res) - 2:]
        gsm["conv_w"][l], gsm["conv_b"][l] = pgw[0:4], pgw[4]
        gsm["ssm_norm_w"][l] = pg1[0]
        gsm["dt_bias"][l], gsm["a_log"][l], gsm["d_skip"][l] = pgh[0, :NH], pgh[1, :NH], pgh[2, :NH]
        dq_b, dsk_b = _swa_bwd_dq(proj, do_b, sv["ob"], sv["lse_b"], sinks[l], cos128, sin128, S, l)
        dk_b, dv_b = _swa_bwd_dkv(proj, do_b, sv["ob"], sv["lse_b"], cos128, sin128, S, l)
        gsm["sinks"][l] = dsk_b[:, :, 0].reshape(NH)
        dq_c, dk_c, dv_c, dcum_k, dcum_q = _fox_bwd(proj, do_c, sv["oc"], sv["cum_row"], sv["lse_c"], S, l)
        dcum_tm = (dcum_k.reshape(Bl, NH, S) + dcum_q.reshape(Bl, NH, S)).transpose(0, 2, 1).reshape(T, NH)
        dcum_pad = jnp.pad(dcum_tm, ((0, 0), (NH, LANES - 2 * NH)))
        df, dfb = _fox_cum_bwd(dcum_pad, ps, sv["fb"], S, l)
        gsm["f_bias"][l] = dfb[0, NH:2 * NH]
        dps16 = (dps_a + df).astype(BF16)
        pieces = (dxbc, daz, dq_b, dbz, dq_c, dk_c, dv_c, dcz, dgates, dk_b, dv_b)
        dw_pieces = [_matmul(sv["h_t"], pc, BF16, f"dwin_{l}_{i}") for i, pc in enumerate(pieces)]
        dws = _matmul(sv["h_t"], dps16, BF16, f"dwin_small_{l}")
        if l == 0:
            plans = [scatter_plan(_w_in_blocks(dw_pieces, dws, r0, r1)) for r0, r1 in ROW_CHUNKS]
        else:
            plans, pending = [None] * len(ROW_CHUNKS), scatter_plan(_w_in_blocks(dw_pieces, dws, 0, D), g_wp, g_wo)
        dkv_b = jnp.concatenate([dk_b, dv_b], axis=1)
        res1 = _inproj_bwd_dx([(dxbc, OFF_XBC), (daz, OFF_AZ), (dq_b, OFF_BQ), (dbz, OFF_BZ)], sv["wmain"],
                              ("narrow", dps16, sv["wsmall"]), None, f"inproj_bwd_dh1_{l}", plans[0])
        res2 = _inproj_bwd_dx([(dq_c, OFF_CQ), (dk_c, OFF_CK), (dv_c, OFF_CV), (dcz, OFF_CZ)], sv["wmain"],
                              ("acc", res1[0]), None, f"inproj_bwd_dh2_{l}", plans[1])
        dx, dx16, dnw = _inproj_bwd_dx([(dgates, OFF_G), (dkv_b, OFF_BK)], sv["wmain"], ("acc", res2[0]),
                                       (sv["x"], norm_w[l][None], dx), f"inproj_bwd_dx_{l}")
        if l == 0:
            parts[0] = [jnp.concatenate([res1[1], res2[1]], axis=1), *parts_po]
        gsm["norm_w"][l] = dnw[0]

    big = {}
    for idx, (name, w, m, v) in enumerate((("w_in", w_in, m_w_in, v_w_in), ("w_proj", w_proj, m_w_proj, v_w_proj),
                                          ("w_out", w_out, m_w_out, v_w_out))):
        cols = w.shape[-1]
        res = _sum_adamw([parts[l][idx].reshape(NDEV, -1, cols) for l in range(depth)], w.reshape(depth, -1, cols),
                         m.reshape(depth, -1, cols), v.reshape(depth, -1, cols), f"adamw_{name}")
        big[name] = [r.reshape(w.shape) for r in res]

    small_names = ("norm_w", "conv_b", "dt_bias", "a_log", "d_skip", "ssm_norm_w", "sinks", "f_bias")
    small_parts = [jnp.stack(gsm[k]) for k in small_names] + [g_final, jnp.stack(gsm["conv_w"]),
                                                              jnp.stack(gsm["gate_bias"]), loss_part.reshape(1)]
    shapes = [a.shape for a in small_parts]
    summed = _unpack_rows(_all_reduce_small(_pack_rows(small_parts)), shapes)
    g_small = dict(zip(small_names, summed[:len(small_names)]))
    g_small["final_norm_w"] = summed[len(small_names)]
    g_small["conv_w"] = lax.dynamic_slice_in_dim(summed[len(small_names) + 1], me * csh, csh, axis=2)
    g_small["gate_bias"] = lax.dynamic_slice_in_dim(summed[len(small_names) + 2], me * gsh, gsh, axis=2)
    loss = summed[len(small_names) + 3][0]

    ws = dict(norm_w=norm_w, conv_w=conv_w, conv_b=conv_b, dt_bias=dt_bias, a_log=a_log, d_skip=d_skip,
              ssm_norm_w=ssm_norm_w, sinks=sinks, f_bias=f_bias, gate_bias=gate_bias, final_norm_w=final_norm_w)
    ms = dict(norm_w=m_norm_w, conv_w=m_conv_w, conv_b=m_conv_b, dt_bias=m_dt_bias, a_log=m_a_log, d_skip=m_d_skip,
              ssm_norm_w=m_ssm_norm_w, sinks=m_sinks, f_bias=m_f_bias, gate_bias=m_gate_bias,
              final_norm_w=m_final_norm_w)
    vs = dict(norm_w=v_norm_w, conv_w=v_conv_w, conv_b=v_conv_b, dt_bias=v_dt_bias, a_log=v_a_log, d_skip=v_d_skip,
              ssm_norm_w=v_ssm_norm_w, sinks=v_sinks, f_bias=v_f_bias, gate_bias=v_gate_bias,
              final_norm_w=v_final_norm_w)
    order = list(ws)
    oshapes = [ws[k].shape for k in order]
    res = _adamw_small(_pack_rows([g_small[k] for k in order]), _pack_rows([ws[k] for k in order]),
                       _pack_rows([ms[k] for k in order]), _pack_rows([vs[k] for k in order]))
    d_s, m_s, v_s = (dict(zip(order, _unpack_rows(r, oshapes))) for r in res)

    names = ("norm_w", "w_in", "conv_w", "conv_b", "dt_bias", "a_log", "d_skip", "ssm_norm_w", "sinks", "f_bias",
             "gate_bias", "w_proj", "w_out", "final_norm_w")
    grads, deltas, new_m, new_v = [], [], [], []
    for k in names:
        if k in big:
            g, d_, m_, v_ = big[k]
        else:
            g, d_, m_, v_ = g_small[k], d_s[k], m_s[k], v_s[k]
        grads.append(g)
        deltas.append(d_)
        new_m.append(m_)
        new_v.append(v_)
    return (loss, dx.reshape(Bl, S, D), *grads, *deltas, *new_m, *new_v)
```

```python
import functools
import math

import jax
import jax.numpy as jnp
from jax import lax
from jax.experimental import pallas as pl
from jax.experimental.pallas import tpu as pltpu

F32 = jnp.float32
BF16 = jnp.bfloat16
MESH = pl.DeviceIdType.MESH
NDEV = 8

D = 1024
NH = 16
HD = 64
NST = 128
NGRP = 4
LCH = 128
EPS = 1e-6
ROPE_THETA = 10000.0
SCALE = HD ** -0.5
NEG = -1e30

LANES = 128
VMEM_LIMIT = 56 * 1024 * 1024

OFF_XBC, OFF_AZ, OFF_BQ, OFF_BZ, OFF_CQ, OFF_CK, OFF_CV, OFF_CZ, OFF_G, OFF_BK, OFF_BV = (
    0, 2048, 3072, 4096, 5120, 6144, 7168, 8192, 9216, 12288, 12544)
NMAIN = 12800
NIN = 12832
NSH = NIN // NDEV

ROW_CHUNKS = ((0, 512), (512, 1024))

ADAM_LR, ADAM_B1, ADAM_B2, ADAM_EPS, ADAM_WD, ADAM_STEP = 0.001, 0.9, 0.999, 1e-08, 0.01, 10


def _cparams(dims=None, vmem=None):
    return pltpu.CompilerParams(dimension_semantics=dims, vmem_limit_bytes=vmem)


def _dot(a, b):
    return jnp.dot(a, b, preferred_element_type=F32)


def _dot_nt(a, b):
    return lax.dot_general(a, b, (((1,), (1,)), ((), ())), preferred_element_type=F32)


def _dot_tn(a, b):
    return lax.dot_general(a, b, (((0,), (0,)), ((), ())), preferred_element_type=F32)


def _dot_hi(a, b):
    return jnp.dot(a, b, precision=lax.Precision.HIGHEST, preferred_element_type=F32)


def _sigmoid(x):
    return 0.5 * jnp.tanh(0.5 * x) + 0.5


def _softplus(x):
    return jnp.maximum(x, 0.0) + jnp.log(1.0 + jnp.exp(-jnp.abs(x)))


def _lane_iota(n=LANES):
    return lax.broadcasted_iota(jnp.int32, (1, n), 1)


def _rot_half(x):
    first = (_lane_iota() % HD) < (HD // 2)
    return jnp.where(first, pltpu.roll(x, LANES - HD // 2, 1), pltpu.roll(x, HD // 2, 1))


def _head_sum(x, head):
    m = (_lane_iota() < HD) if head == 0 else (_lane_iota() >= HD)
    return jnp.sum(jnp.where(m, x, 0.0), axis=1, keepdims=True)


def _me_and_peers():
    x, y, c = lax.axis_index("x"), lax.axis_index("y"), lax.axis_index("c")
    me = 4 * x + 2 * y + c
    peers = []
    for k in range(1, NDEV):
        kx, ky, kc = (k >> 2) & 1, (k >> 1) & 1, k & 1
        px, py, pc = x ^ kx, y ^ ky, c ^ kc
        peers.append(((px, py, pc), 4 * px + 2 * py + pc))
    return me, peers


class _Comm:
    def __init__(self, kind, arrays):
        self.kind, self.arrays, self.n = kind, list(arrays), len(arrays)
        any_spec = pl.BlockSpec(memory_space=pl.ANY)
        self.in_specs = [any_spec] * self.n
        self.out_specs = [any_spec] * self.n
        self.out_shape = [jax.ShapeDtypeStruct(((NDEV,) + a.shape) if kind == "gather" else a.shape, a.dtype)
                          for a in self.arrays]
        self.scratch = [pltpu.SemaphoreType.DMA((self.n, NDEV - 1)), pltpu.SemaphoreType.DMA((self.n, NDEV - 1)),
                        pltpu.SemaphoreType.DMA((self.n,))]

    def copies(self, ins, outs, sems):
        send_sems, recv_sems, local_sems = sems
        me, peers = _me_and_peers()
        out = []
        for a in range(self.n):
            mine = ins[a] if self.kind == "gather" else ins[a].at[me]
            out.append(pltpu.make_async_copy(mine, outs[a].at[me], local_sems.at[a]))
            for k, (peer, pidx) in enumerate(peers):
                src = ins[a] if self.kind == "gather" else ins[a].at[pidx]
                out.append(pltpu.make_async_remote_copy(
                    src_ref=src, dst_ref=outs[a].at[me], send_sem=send_sems.at[a, k], recv_sem=recv_sems.at[a, k],
                    device_id=peer, device_id_type=MESH))
        return out


def _gather_two_level(arrays, name):
    n = len(arrays)

    def body(*refs):
        ins, outs = refs[:n], refs[n:2 * n]
        send_sems, recv_sems, local_sems = refs[2 * n:]
        x, y, c = lax.axis_index("x"), lax.axis_index("y"), lax.axis_index("c")
        me, sibling = (x, y, c), (x, y, 1 - c)
        chips = [(1 - x, y), (x, 1 - y), (1 - x, 1 - y)]

        def slot(a, dev):
            return outs[a].at[4 * dev[0] + 2 * dev[1] + dev[2]]

        def copy(a, k, block, to, src=None):
            return pltpu.make_async_remote_copy(
                src_ref=slot(a, block) if src is None else src, dst_ref=slot(a, block),
                send_sem=send_sems.at[a, k], recv_sem=recv_sems.at[a, k], device_id=to, device_id_type=MESH)

        mine = [pltpu.make_async_copy(ins[a], slot(a, me), local_sems.at[a]) for a in range(n)]
        for cp in mine:
            cp.start()
        first = []
        for a in range(n):
            first.append(copy(a, 0, me, sibling, src=ins[a]))
            first += [copy(a, 1 + j, me, (*chip, c), src=ins[a]) for j, chip in enumerate(chips)]
        for cp in first:
            cp.start()
        passed = []
        for j, chip in enumerate(chips):
            for a in range(n):
                copy(a, 1 + j, (*chip, c), me).wait_recv()
                fwd = copy(a, 4 + j, (*chip, c), sibling)
                fwd.start()
                passed.append(fwd)
        for a in range(n):
            copy(a, 0, sibling, me).wait_recv()
            for j, chip in enumerate(chips):
                copy(a, 4 + j, (*chip, 1 - c), me).wait_recv()
        for cp in first + passed:
            cp.wait_send()
        for cp in mine:
            cp.wait()

    any_spec = pl.BlockSpec(memory_space=pl.ANY)
    return pl.pallas_call(
        body, name=name, out_shape=[jax.ShapeDtypeStruct((NDEV,) + a.shape, a.dtype) for a in arrays],
        in_specs=[any_spec] * n, out_specs=[any_spec] * n,
        scratch_shapes=[pltpu.SemaphoreType.DMA((n, NDEV - 1)), pltpu.SemaphoreType.DMA((n, NDEV - 1)),
                        pltpu.SemaphoreType.DMA((n,))])(*arrays)


def _hosted_call(body, comm, name, grid, in_specs, out_specs, out_shape, scratch, dims, operands):
    if comm is None:
        return pl.pallas_call(body, name=name, grid=grid, in_specs=in_specs, out_specs=out_specs, out_shape=out_shape,
                              scratch_shapes=scratch, compiler_params=_cparams(dims, VMEM_LIMIT))(*operands)
    n_in, n_out, n_scr, n = len(in_specs), len(out_specs), len(scratch), comm.n

    def hosted(*refs):
        hin, cin = refs[:n_in], refs[n_in:n_in + n]
        hout = refs[n_in + n:n_in + n + n_out]
        cout = refs[n_in + n + n_out:n_in + 2 * n + n_out]
        hscr = refs[n_in + 2 * n + n_out:n_in + 2 * n + n_out + n_scr]
        sems = refs[n_in + 2 * n + n_out + n_scr:]
        ids = [pl.program_id(a) for a in range(len(grid))]
        first = functools.reduce(jnp.logical_and, [i == 0 for i in ids])
        last = functools.reduce(jnp.logical_and, [i == g - 1 for i, g in zip(ids, grid)])

        @pl.when(first)
        def _():
            for cp in comm.copies(cin, cout, sems):
                cp.start()

        body(*hin, *hout, *hscr)

        @pl.when(last)
        def _():
            for cp in comm.copies(cin, cout, sems):
                cp.wait()

    return pl.pallas_call(
        hosted, name=name, grid=grid, in_specs=list(in_specs) + comm.in_specs,
        out_specs=list(out_specs) + comm.out_specs, out_shape=list(out_shape) + comm.out_shape,
        scratch_shapes=list(scratch) + comm.scratch,
        compiler_params=_cparams(("arbitrary",) * len(grid), VMEM_LIMIT))(*operands, *comm.arrays)


def _all_reduce_small(v):
    rows = v.shape[0]

    def body(v_ref, sum_ref, all_ref, send_sems, recv_sems):
        me, peers = _me_and_peers()
        all_ref[me] = v_ref[...]
        copies = []
        for k, (peer, _) in enumerate(peers):
            cp = pltpu.make_async_remote_copy(
                src_ref=v_ref, dst_ref=all_ref.at[me],
                send_sem=send_sems.at[k], recv_sem=recv_sems.at[k],
                device_id=peer, device_id_type=MESH)
            cp.start()
            copies.append(cp)
        for cp in copies:
            cp.wait()
        acc = all_ref[0]
        for d in range(1, NDEV):
            acc = acc + all_ref[d]
        sum_ref[...] = acc

    vm = pl.BlockSpec(memory_space=pltpu.VMEM)
    return pl.pallas_call(
        body, name="all_reduce_small",
        out_shape=jax.ShapeDtypeStruct((rows, LANES), F32),
        in_specs=[vm], out_specs=vm,
        scratch_shapes=[pltpu.VMEM((NDEV, rows, LANES), F32),
                        pltpu.SemaphoreType.DMA((NDEV - 1,)), pltpu.SemaphoreType.DMA((NDEV - 1,))],
    )(v)


def _adamw_math(w, g, m, v):
    m = ADAM_B1 * m + (1.0 - ADAM_B1) * g
    v = ADAM_B2 * v + (1.0 - ADAM_B2) * jnp.square(g)
    m_hat = m / (1.0 - ADAM_B1 ** ADAM_STEP)
    v_hat = v / (1.0 - ADAM_B2 ** ADAM_STEP)
    delta = -ADAM_LR * (m_hat / (jnp.sqrt(v_hat) + ADAM_EPS) + ADAM_WD * w)
    return delta, m, v


def _sum_adamw(parts, w, m, v, name):
    depth, rows, cols = w.shape
    tr = next(c for c in (256, 128, 64, 32, 16) if rows % c == 0)
    nb = rows // tr

    def body(*refs):
        p_refs, (w_ref, m_ref, v_ref, g_ref, d_ref, nm_ref, nv_ref) = refs[:depth], refs[depth:]
        l = pl.program_id(0)
        for ll in range(depth):
            @pl.when(l == ll)
            def _(ll=ll):
                g = p_refs[ll][0].astype(F32)
                for d in range(1, NDEV):
                    g = g + p_refs[ll][d].astype(F32)
                delta, nm, nv = _adamw_math(w_ref[0], g, m_ref[0], v_ref[0])
                g_ref[0] = g
                d_ref[0] = delta
                nm_ref[0] = nm
                nv_ref[0] = nv

    part = lambda ll: pl.BlockSpec((NDEV, tr, cols), lambda l, i, ll=ll: (0, jnp.where(l == ll, i, jnp.where(l < ll, 0, nb - 1)), 0))
    blk = pl.BlockSpec((1, tr, cols), lambda l, i: (l, i, 0))
    sds = jax.ShapeDtypeStruct((depth, rows, cols), F32)
    return pl.pallas_call(
        body, name=name, grid=(depth, nb),
        in_specs=[part(ll) for ll in range(depth)] + [blk, blk, blk],
        out_specs=[blk, blk, blk, blk], out_shape=[sds, sds, sds, sds],
        compiler_params=_cparams(("arbitrary", "arbitrary"), VMEM_LIMIT),
    )(*parts, w, m, v)


def _adamw_small(g, w, m, v):
    def body(g_ref, w_ref, m_ref, v_ref, d_ref, nm_ref, nv_ref):
        delta, nm, nv = _adamw_math(w_ref[...], g_ref[...], m_ref[...], v_ref[...])
        d_ref[...] = delta
        nm_ref[...] = nm
        nv_ref[...] = nv

    sds = jax.ShapeDtypeStruct(g.shape, F32)
    return pl.pallas_call(body, name="adamw_small", out_shape=[sds, sds, sds])(g, w, m, v)


def _matmul(a, b, out_dtype, name, tm=1024, tn=1024, tk=1024):
    M, K = a.shape
    N = b.shape[1]
    tm, tn, tk = min(tm, M), min(tn, N), min(tk, K)
    nk = K // tk

    def body(a_ref, b_ref, o_ref, acc):
        k = pl.program_id(2)

        @pl.when(k == 0)
        def _():
            acc[...] = jnp.zeros_like(acc)

        acc[...] += _dot(a_ref[...], b_ref[...])

        @pl.when(k == nk - 1)
        def _():
            o_ref[...] = acc[...].astype(out_dtype)

    return pl.pallas_call(
        body, name=name, grid=(M // tm, N // tn, nk),
        in_specs=[pl.BlockSpec((tm, tk), lambda i, j, k: (i, k)), pl.BlockSpec((tk, tn), lambda i, j, k: (k, j))],
        out_specs=pl.BlockSpec((tm, tn), lambda i, j, k: (i, j)),
        out_shape=jax.ShapeDtypeStruct((M, N), out_dtype),
        scratch_shapes=[pltpu.VMEM((tm, tn), F32)],
        compiler_params=_cparams(("parallel", "parallel", "arbitrary"), VMEM_LIMIT),
    )(a, b)


def _matmul_batched(a, b, out_dtype, name, tm=1024, tn=1024, tk=512):
    G, M, K = a.shape
    N = b.shape[2]
    tm, tn, tk = min(tm, M), min(tn, N), min(tk, K)
    nk = K // tk

    def body(a_ref, b_ref, o_ref, acc):
        k = pl.program_id(3)

        @pl.when(k == 0)
        def _():
            acc[...] = jnp.zeros_like(acc)

        acc[...] += _dot(a_ref[0], b_ref[0])

        @pl.when(k == nk - 1)
        def _():
            o_ref[0] = acc[...].astype(out_dtype)

    return pl.pallas_call(
        body, name=name, grid=(G, M // tm, N // tn, nk),
        in_specs=[pl.BlockSpec((1, tm, tk), lambda g, i, j, k: (g, i, k)),
                  pl.BlockSpec((1, tk, tn), lambda g, i, j, k: (g, k, j))],
        out_specs=pl.BlockSpec((1, tm, tn), lambda g, i, j, k: (g, i, j)),
        out_shape=jax.ShapeDtypeStruct((G, M, N), out_dtype),
        scratch_shapes=[pltpu.VMEM((tm, tn), F32)],
        compiler_params=_cparams(("parallel", "parallel", "parallel", "arbitrary"), VMEM_LIMIT),
    )(a, b)


def _inproj_fwd(x2, nw, wmain, wsmall, cos128, sin128, S, li, comm=None):
    T = x2.shape[0]
    tm, tn = min(2048, S), 512
    nj, npos = NMAIN // tn, S // tm
    jq0, jk = OFF_BQ // tn, OFF_BK // tn

    def body(x_ref, nw_ref, w_ref, ws_ref, cos_ref, sin_ref, proj_ref, ps_ref, ht_ref, h_scr):
        j = pl.program_id(1)

        @pl.when(j == 0)
        def _():
            x = x_ref[...]
            r = lax.rsqrt(jnp.mean(x * x, axis=-1, keepdims=True) + EPS)
            h = (x * r * nw_ref[...]).astype(BF16)
            h_scr[...] = h
            ht_ref[...] = h.T
            ps_ref[...] = _dot(h, ws_ref[...])

        acc = _dot(h_scr[...], w_ref[...])

        def roped(c):
            xc = acc[:, LANES * c:LANES * (c + 1)]
            return (xc * cos_ref[...] + _rot_half(xc) * sin_ref[...]).astype(BF16)

        def plain(c):
            return acc[:, LANES * c:LANES * (c + 1)].astype(BF16)

        is_q = jnp.logical_or(j == jq0, j == jq0 + 1)
        is_k = j == jk

        @pl.when(is_q)
        def _():
            for c in range(4):
                proj_ref[:, LANES * c:LANES * (c + 1)] = roped(c)

        @pl.when(is_k)
        def _():
            for c in range(4):
                proj_ref[:, LANES * c:LANES * (c + 1)] = roped(c) if c < 2 else plain(c)

        @pl.when(jnp.logical_not(jnp.logical_or(is_q, is_k)))
        def _():
            proj_ref[...] = acc.astype(BF16)

    return _hosted_call(
        body, comm, f"inproj_fwd_{li}", (T // tm, nj),
        in_specs=[pl.BlockSpec((tm, D), lambda i, j: (i, 0)),
                  pl.BlockSpec((1, D), lambda i, j: (0, 0)),
                  pl.BlockSpec((D, tn), lambda i, j: (0, j)),
                  pl.BlockSpec((D, LANES), lambda i, j: (0, 0)),
                  pl.BlockSpec((tm, LANES), lambda i, j: (i % npos, 0)),
                  pl.BlockSpec((tm, LANES), lambda i, j: (i % npos, 0))],
        out_specs=[pl.BlockSpec((tm, tn), lambda i, j: (i, j)),
                   pl.BlockSpec((tm, LANES), lambda i, j: (i, 0)),
                   pl.BlockSpec((D, tm), lambda i, j: (0, i))],
        out_shape=[jax.ShapeDtypeStruct((T, NMAIN), BF16), jax.ShapeDtypeStruct((T, LANES), F32),
                   jax.ShapeDtypeStruct((D, T), BF16)],
        scratch=[pltpu.VMEM((tm, D), BF16)], dims=("parallel", "arbitrary"),
        operands=(x2, nw, wmain, wsmall, cos128, sin128))


def _inproj_bwd_dx(segs, wmain, init, final, name, comm=None):
    T = segs[0][0].shape[0]
    tm = min(1024, T)
    tk = 1024 if all(a.shape[1] % 1024 == 0 and c % 1024 == 0 for a, c in segs) else 512
    ni = T // tm
    k0s, nks, c0s = [], [], []
    for arr, col0 in segs:
        k0s.append(sum(nks))
        nks.append(arr.shape[1] // tk)
        c0s.append(col0 // tk)
    nk = sum(nks)
    ns = len(segs)

    def in_range(k, s):
        return jnp.logical_and(k >= k0s[s], k < k0s[s] + nks[s])

    def wcol(i, k):
        g = 0
        for s in range(ns):
            g = g + jnp.where(in_range(k, s), c0s[s] + k - k0s[s], 0)
        return (0, g)

    n_init = 2 if init[0] == "narrow" else 1

    def body(*refs):
        seg_refs, w_ref = refs[:ns], refs[ns]
        init_refs = refs[ns + 1:ns + 1 + n_init]
        rest = refs[ns + 1 + n_init:]
        i, k = pl.program_id(0), pl.program_id(1)
        acc = rest[-1]

        @pl.when(k == 0)
        def _():
            if init[0] == "narrow":
                acc[...] = _dot_nt(init_refs[0][...], init_refs[1][...])
            else:
                acc[...] = init_refs[0][...]

        for s in range(ns):
            @pl.when(in_range(k, s))
            def _(s=s):
                acc[...] += _dot_nt(seg_refs[s][...], w_ref[...])

        if final is None:
            @pl.when(k == nk - 1)
            def _():
                rest[0][...] = acc[...]
        else:
            x_ref, nw_ref, dxo_ref, dx_ref, dx16_ref, dnw_ref = rest[:6]

            @pl.when(jnp.logical_and(i == 0, k == 0))
            def _():
                dnw_ref[...] = jnp.zeros_like(dnw_ref)

            @pl.when(k == nk - 1)
            def _():
                x = x_ref[...]
                r = lax.rsqrt(jnp.mean(x * x, axis=-1, keepdims=True) + EPS)
                dh = acc[...]
                g = dh * nw_ref[...]
                dx = dxo_ref[...] + r * g - x * (r * r * r) * jnp.mean(g * x, axis=-1, keepdims=True)
                dx_ref[...] = dx
                dx16_ref[...] = dx.astype(BF16)
                dnw_ref[0:1, :] += jnp.sum(dh * x * r, axis=0, keepdims=True)

    row = pl.BlockSpec((tm, D), lambda i, k: (i, 0))
    in_specs = [pl.BlockSpec((tm, tk), lambda i, k, s=s: (i, jnp.clip(k - k0s[s], 0, nks[s] - 1))) for s in range(ns)]
    in_specs.append(pl.BlockSpec((D, tk), wcol))
    operands = [a for a, _ in segs] + [wmain]
    if init[0] == "narrow":
        in_specs += [pl.BlockSpec((tm, LANES), lambda i, k: (i, 0)), pl.BlockSpec((D, LANES), lambda i, k: (0, 0))]
    else:
        in_specs.append(row)
    operands += list(init[1:])
    if final is None:
        out_specs, out_shape = [row], [jax.ShapeDtypeStruct((T, D), F32)]
    else:
        in_specs += [row, pl.BlockSpec((1, D), lambda i, k: (0, 0)), row]
        operands += list(final)
        out_specs = [row, row, pl.BlockSpec((8, D), lambda i, k: (0, 0))]
        out_shape = [jax.ShapeDtypeStruct((T, D), F32), jax.ShapeDtypeStruct((T, D), BF16),
                     jax.ShapeDtypeStruct((8, D), F32)]
    return _hosted_call(body, comm, name, (ni, nk), in_specs=in_specs, out_specs=out_specs, out_shape=out_shape,
                        scratch=[pltpu.VMEM((tm, D), F32)], dims=("arbitrary", "arbitrary"), operands=tuple(operands))


def _merge_fwd(ya, yb, yc, proj, gbias, wp, wout, x2, li):
    T = x2.shape[0]
    tm = min(512, T)
    gcol = OFF_G // D

    def body(ya_ref, yb_ref, yc_ref, g0_ref, g1_ref, g2_ref, gb_ref, wp_ref, wo_ref, x_ref, xn_ref, br_ref, yt_ref):
        merged = jnp.zeros((tm, D), F32)
        for i, (y_ref, g_ref) in enumerate(((ya_ref, g0_ref), (yb_ref, g1_ref), (yc_ref, g2_ref))):
            y = y_ref[...]
            yt_ref[i] = y.T
            br = _dot(y, wp_ref[i])
            br_ref[i] = br.astype(BF16)
            gate = _sigmoid(g_ref[...].astype(F32) + gb_ref[i:i + 1, :])
            merged = merged + gate * br
        xn_ref[...] = x_ref[...] + _dot(merged.astype(BF16), wo_ref[...])

    row = lambda c: pl.BlockSpec((tm, D), lambda i, c=c: (i, c))
    return pl.pallas_call(
        body, name=f"merge_fwd_{li}", grid=(T // tm,),
        in_specs=[row(0), row(0), row(0), row(gcol), row(gcol + 1), row(gcol + 2),
                  pl.BlockSpec((3, D), lambda i: (0, 0)),
                  pl.BlockSpec((3, D, D), lambda i: (0, 0, 0)),
                  pl.BlockSpec((D, D), lambda i: (0, 0)),
                  row(0)],
        out_specs=[row(0), pl.BlockSpec((3, tm, D), lambda i: (0, i, 0)), pl.BlockSpec((3, D, tm), lambda i: (0, 0, i))],
        out_shape=[jax.ShapeDtypeStruct((T, D), F32), jax.ShapeDtypeStruct((3, T, D), BF16),
                   jax.ShapeDtypeStruct((3, D, T), BF16)],
        compiler_params=_cparams(("parallel",), VMEM_LIMIT),
    )(ya, yb, yc, proj, proj, proj, gbias, wp, wout, x2)


def _merge_bwd(dxo16, wout, wp, br, proj, gbias, ob, oc, li):
    T = dxo16.shape[0]
    tm = min(256, T)
    gcol = OFF_G // D

    def body(dx_ref, wo_ref, wp_ref, br_ref, g0_ref, g1_ref, g2_ref, gb_ref, ob_ref, oc_ref, zb_ref, zc_ref,
             dbr_ref, dg_ref, mt_ref, dgb_ref, dya_ref, dob_ref, dzb_ref, doc_ref, dzc_ref):
        @pl.when(pl.program_id(0) == 0)
        def _():
            dgb_ref[...] = jnp.zeros_like(dgb_ref)

        dm = _dot_nt(dx_ref[...], wo_ref[...])
        merged = jnp.zeros((tm, D), F32)
        dys = []
        for i, g_ref in enumerate((g0_ref, g1_ref, g2_ref)):
            b = br_ref[i].astype(F32)
            gate = _sigmoid(g_ref[...].astype(F32) + gb_ref[i:i + 1, :])
            merged = merged + gate * b
            dbr = (dm * gate).astype(BF16)
            dbr_ref[i] = dbr
            dgate = dm * b * gate * (1.0 - gate)
            dg_ref[:, D * i:D * (i + 1)] = dgate.astype(BF16)
            dgb_ref[i:i + 1, :] += jnp.sum(dgate, axis=0, keepdims=True)
            dys.append(_dot_nt(dbr, wp_ref[i]))
        mt_ref[...] = merged.astype(BF16).T
        dya_ref[...] = dys[0].astype(BF16)
        for dy, o_ref, z_ref, do_ref, dz_ref in ((dys[1], ob_ref, zb_ref, dob_ref, dzb_ref),
                                                 (dys[2], oc_ref, zc_ref, doc_ref, dzc_ref)):
            z = z_ref[...].astype(F32)
            sg = _sigmoid(z)
            do_ref[...] = (dy * z * sg).astype(BF16)
            dz_ref[...] = (dy * o_ref[...].astype(F32) * sg * (1.0 + z * (1.0 - sg))).astype(BF16)

    row = lambda c: pl.BlockSpec((tm, D), lambda i, c=c: (i, c))
    sds = jax.ShapeDtypeStruct((T, D), BF16)
    return pl.pallas_call(
        body, name=f"merge_bwd_{li}", grid=(T // tm,),
        in_specs=[row(0), pl.BlockSpec((D, D), lambda i: (0, 0)), pl.BlockSpec((3, D, D), lambda i: (0, 0, 0)),
                  pl.BlockSpec((3, tm, D), lambda i: (0, i, 0)),
                  row(gcol), row(gcol + 1), row(gcol + 2),
                  pl.BlockSpec((3, D), lambda i: (0, 0)),
                  row(0), row(0), row(OFF_BZ // D), row(OFF_CZ // D)],
        out_specs=[pl.BlockSpec((3, tm, D), lambda i: (0, i, 0)),
                   pl.BlockSpec((tm, 3 * D), lambda i: (i, 0)),
                   pl.BlockSpec((D, tm), lambda i: (0, i)),
                   pl.BlockSpec((8, D), lambda i: (0, 0)),
                   row(0), row(0), row(0), row(0), row(0)],
        out_shape=[jax.ShapeDtypeStruct((3, T, D), BF16), jax.ShapeDtypeStruct((T, 3 * D), BF16),
                   jax.ShapeDtypeStruct((D, T), BF16), jax.ShapeDtypeStruct((8, D), F32), sds, sds, sds, sds, sds],
        compiler_params=_cparams(("arbitrary",), VMEM_LIMIT),
    )(dxo16, wout, wp, br, proj, proj, proj, gbias, ob, oc, proj, proj)


def _final_loss(x2, tgt, fw):
    T = x2.shape[0]
    tm = min(512, T)
    ni = T // tm

    def body(x_ref, t_ref, w_ref, dx_ref, dx16_ref, st_ref):
        i = pl.program_id(0)

        @pl.when(i == 0)
        def _():
            st_ref[...] = jnp.zeros_like(st_ref)

        x = x_ref[...]
        r = lax.rsqrt(jnp.mean(x * x, axis=-1, keepdims=True) + EPS)
        xh = x * r
        err = xh * w_ref[...] - t_ref[...]
        dy = err * (1.0 / D)
        g = dy * w_ref[...]
        dx = r * g - x * (r * r * r) * jnp.mean(g * x, axis=-1, keepdims=True)
        dx_ref[...] = dx
        dx16_ref[...] = dx.astype(BF16)
        st_ref[0:1, :] += jnp.sum(dy * xh, axis=0, keepdims=True)
        st_ref[1:2, :] += jnp.sum(err * err, axis=0, keepdims=True)

        @pl.when(i == ni - 1)
        def _():
            tot = jnp.sum(st_ref[1:2, :], axis=1, keepdims=True) * (0.5 / D)
            st_ref[2:3, :] = jnp.broadcast_to(tot, (1, D))

    row = pl.BlockSpec((tm, D), lambda i: (i, 0))
    return pl.pallas_call(
        body, name="final_loss", grid=(ni,),
        in_specs=[row, row, pl.BlockSpec((1, D), lambda i: (0, 0))],
        out_specs=[row, row, pl.BlockSpec((8, D), lambda i: (0, 0))],
        out_shape=[jax.ShapeDtypeStruct((T, D), F32), jax.ShapeDtypeStruct((T, D), BF16),
                   jax.ShapeDtypeStruct((8, D), F32)],
        compiler_params=_cparams(("arbitrary",), VMEM_LIMIT),
    )(x2, tgt, fw)


def _fox_cum(ps, fb_row, S, li):
    T = ps.shape[0]
    blk = min(4 * LCH, S)
    nb, nsub = S // blk, blk // LCH

    def body(ps_ref, fb_ref, cum_ref, carry):
        @pl.when(pl.program_id(1) == 0)
        def _():
            carry[...] = jnp.zeros_like(carry)

        r = lax.broadcasted_iota(jnp.int32, (LCH, LCH), 0)
        c = lax.broadcasted_iota(jnp.int32, (LCH, LCH), 1)
        tri = (r >= c).astype(F32)
        run = carry[0:1, :]
        for u in range(nsub):
            rows = slice(LCH * u, LCH * (u + 1))
            logf = -_softplus(-(ps_ref[rows, :] + fb_ref[...]))
            cum = _dot_hi(tri, logf) + run
            cum_ref[rows, :] = cum
            run = cum[LCH - 1:LCH, :]
        carry[0:1, :] = run

    return pl.pallas_call(
        body, name=f"fox_cum_{li}", grid=(T // S, nb),
        in_specs=[pl.BlockSpec((blk, LANES), lambda b, i: (b * nb + i, 0)),
                  pl.BlockSpec((1, LANES), lambda b, i: (0, 0))],
        out_specs=pl.BlockSpec((blk, LANES), lambda b, i: (b * nb + i, 0)),
        out_shape=jax.ShapeDtypeStruct((T, LANES), F32),
        scratch_shapes=[pltpu.VMEM((8, LANES), F32)],
        compiler_params=_cparams(("arbitrary", "arbitrary")),
    )(ps, fb_row)


def _fox_cum_bwd(dcum, ps, fb_row, S, li):
    T = ps.shape[0]
    rows_blk = min(4 * LCH, S)
    nb, nsub = S // rows_blk, rows_blk // LCH

    def body(dc_ref, ps_ref, fb_ref, df_ref, dfb_ref, carry):
        b, i = pl.program_id(0), pl.program_id(1)

        @pl.when(i == 0)
        def _():
            carry[...] = jnp.zeros_like(carry)

        @pl.when(jnp.logical_and(b == 0, i == 0))
        def _():
            dfb_ref[...] = jnp.zeros_like(dfb_ref)

        r = lax.broadcasted_iota(jnp.int32, (LCH, LCH), 0)
        c = lax.broadcasted_iota(jnp.int32, (LCH, LCH), 1)
        tri = (c >= r).astype(F32)
        lane = _lane_iota()
        live = jnp.logical_and(lane >= NH, lane < 2 * NH)
        run = carry[0:1, :]
        dfb = jnp.zeros((1, LANES), F32)
        for u in reversed(range(nsub)):
            rows = slice(LCH * u, LCH * (u + 1))
            dc = dc_ref[rows, :]
            dlogf = _dot_hi(tri, dc) + run
            run = run + jnp.sum(dc, axis=0, keepdims=True)
            df = jnp.where(live, dlogf * _sigmoid(-(ps_ref[rows, :] + fb_ref[...])), 0.0)
            df_ref[rows, :] = df
            dfb = dfb + jnp.sum(df, axis=0, keepdims=True)
        carry[0:1, :] = run
        dfb_ref[0:1, :] += dfb

    blk = pl.BlockSpec((rows_blk, LANES), lambda b, i: (b * nb + nb - 1 - i, 0))
    return pl.pallas_call(
        body, name=f"fox_cum_bwd_{li}", grid=(T // S, nb),
        in_specs=[blk, blk, pl.BlockSpec((1, LANES), lambda b, i: (0, 0))],
        out_specs=[blk, pl.BlockSpec((8, LANES), lambda b, i: (0, 0))],
        out_shape=[jax.ShapeDtypeStruct((T, LANES), F32), jax.ShapeDtypeStruct((8, LANES), F32)],
        scratch_shapes=[pltpu.VMEM((8, LANES), F32)],
        compiler_params=_cparams(("arbitrary", "arbitrary")),
    )(dcum, ps, fb_row)


def _fox_blocks(S):
    bq = min(512, S)
    return bq, S // bq


def _split3(c):
    hi = c.astype(BF16).astype(F32)
    r = c - hi
    mid = r.astype(BF16).astype(F32)
    return hi, mid, (r - mid).astype(BF16).astype(F32)


def _augment(x, bias_row, key_side, hh):
    n = x.shape[0]
    b0 = HD if hh == 0 else 0
    hi, mid, lo = _split3(bias_row)
    one = jnp.ones_like(bias_row)
    six = (one, one, one, hi, mid, lo) if key_side else (hi, mid, lo, one, one, one)
    sub = lax.broadcasted_iota(jnp.int32, (LANES, 1), 0)
    a = jnp.zeros((LANES, n), F32)
    for t, r in enumerate(six):
        a = jnp.where(sub == b0 + t, r, a)
    lane = _lane_iota()
    return jnp.where(jnp.logical_and(lane >= b0, lane < b0 + 6), a.T, x).astype(BF16)


def _col_to_row(col):
    return jnp.broadcast_to(col, (col.shape[0], LANES)).T[0:1, :]


def _fox_fwd(proj, cum_row, S, li, comm=None):
    T = proj.shape[0]
    B = T // S
    bq, nq = _fox_blocks(S)
    qc, kc, vc, zc = OFF_CQ // LANES, OFF_CK // LANES, OFF_CV // LANES, OFF_CZ // LANES

    def body(q_ref, k_ref, v_ref, z_ref, cr_ref, y_ref, o_ref, lse_ref, kaug, vaug):
        i = pl.program_id(2)
        m0 = _lane_iota() < HD

        @pl.when(i == 0)
        def _():
            vf = v_ref[...]
            for hh in range(2):
                for t in range(nq):
                    rows = slice(bq * t, bq * (t + 1))
                    kaug[hh, rows, :] = _augment(k_ref[rows, :].astype(F32), -cr_ref[0, hh, t], True, hh)
                vaug[hh] = jnp.where(m0 if hh == 0 else jnp.logical_not(m0), vf, jnp.ones_like(vf))

        q2 = q_ref[...].astype(F32) * SCALE
        row = lax.broadcasted_iota(jnp.int32, (bq, bq), 0)
        col = lax.broadcasted_iota(jnp.int32, (bq, bq), 1)
        qa = [_augment(jnp.where(m0 if hh == 0 else jnp.logical_not(m0), q2, 0.0), cr_ref[0, hh, i], False, hh)
              for hh in range(2)]

        def step(j, carry, masked):
            start = pl.multiple_of(j * bq, bq)
            out = []
            for hh in range(2):
                m, acc = carry[2 * hh:2 * hh + 2]
                s = _dot_nt(qa[hh], kaug[hh, pl.ds(start, bq), :])
                if masked:
                    s = jnp.where(row >= col, s, NEG)
                mn = jnp.maximum(m, jnp.max(s, axis=1, keepdims=True))
                p = jnp.exp(s - mn)
                out += [mn, jnp.exp(m - mn) * acc + _dot(p.astype(BF16), vaug[hh, pl.ds(start, bq), :])]
            return tuple(out)

        init = (jnp.full((bq, 1), NEG, F32), jnp.zeros((bq, LANES), F32)) * 2
        carry = step(i, lax.fori_loop(0, i, functools.partial(step, masked=False), init), True)
        outs = []
        for hh in range(2):
            m, acc = carry[2 * hh:2 * hh + 2]
            other = HD if hh == 0 else 0
            l = acc[:, other:other + 1]
            outs.append(acc / l)
            lse_ref[0, hh, 0] = _col_to_row(m + jnp.log(l))
        o2 = jnp.where(m0, outs[0], outs[1])
        z = z_ref[...].astype(F32)
        o_ref[...] = o2.astype(BF16)
        y_ref[...] = (o2 * z * _sigmoid(z)).astype(BF16)

    qblk = lambda c: pl.BlockSpec((bq, LANES), lambda b, p, i, c=c: (b * nq + i, c + p))
    sblk = lambda c: pl.BlockSpec((S, LANES), lambda b, p, i, c=c: (b, c + p))
    return _hosted_call(
        body, comm, f"fox_fwd_{li}", (B, NH // 2, nq),
        in_specs=[qblk(qc), sblk(kc), sblk(vc), qblk(zc),
                  pl.BlockSpec((1, 2, nq, 1, bq), lambda b, p, i: (b, p, 0, 0, 0))],
        out_specs=[qblk(0), qblk(0), pl.BlockSpec((1, 2, 1, 1, bq), lambda b, p, i: (b, p, i, 0, 0))],
        out_shape=[jax.ShapeDtypeStruct((T, D), BF16), jax.ShapeDtypeStruct((T, D), BF16),
                   jax.ShapeDtypeStruct((B, NH, nq, 1, bq), F32)],
        scratch=[pltpu.VMEM((2, S, LANES), BF16), pltpu.VMEM((2, S, LANES), BF16)],
        dims=("parallel", "parallel", "arbitrary"), operands=(proj, proj, proj, proj, cum_row))


def _fox_bwd(proj, do, o, cum_row, lse, S, li, comm=None):
    T = proj.shape[0]
    B = T // S
    bq, nq = _fox_blocks(S)
    qc, kc, vc = OFF_CQ // LANES, OFF_CK // LANES, OFF_CV // LANES

    def body(q_ref, k_ref, v_ref, do_ref, o_ref, cr_ref, lse_ref, dq_ref, dk_ref, dv_ref, dc_ref, dr_ref,
             dq_scr, dr_scr, qaug):
        j = pl.program_id(2)
        m0 = _lane_iota() < HD

        @pl.when(j == 0)
        def _():
            dq_scr[...] = jnp.zeros_like(dq_scr)
            dr_scr[...] = jnp.zeros_like(dr_scr)
            for t in range(nq):
                rows = slice(bq * t, bq * (t + 1))
                qf = q_ref[rows, :].astype(F32) * SCALE
                for hh in range(2):
                    sel = m0 if hh == 0 else jnp.logical_not(m0)
                    qaug[hh, rows, :] = _augment(jnp.where(sel, qf, 0.0), cr_ref[0, hh, t] - lse_ref[0, hh, t],
                                                 False, hh)

        k2 = k_ref[...]
        v2 = v_ref[...]
        zk = jnp.zeros_like(k2)
        kh = (jnp.where(m0, k2, zk), jnp.where(m0, zk, k2))
        kf = k2.astype(F32)
        ka = [_augment(kf, -cr_ref[0, hh, j], True, hh) for hh in range(2)]
        row = lax.broadcasted_iota(jnp.int32, (bq, bq), 0)
        col = lax.broadcasted_iota(jnp.int32, (bq, bq), 1)

        def step(i, carry, masked):
            dk, dv, dc0, dc1 = carry
            dcs = [dc0, dc1]
            start = pl.multiple_of(i * bq, bq)
            q2 = q_ref[pl.ds(start, bq), :]
            do2 = do_ref[pl.ds(start, bq), :]
            prod = do2.astype(F32) * o_ref[pl.ds(start, bq), :].astype(F32)
            zq = jnp.zeros_like(q2)
            dq = jnp.zeros((bq, LANES), F32)
            for hh in range(2):
                sel = m0 if hh == 0 else jnp.logical_not(m0)
                qh = jnp.where(sel, q2, zq)
                doh = jnp.where(sel, do2, zq)
                delta = _head_sum(prod, hh)
                s = _dot_nt(qaug[hh, pl.ds(start, bq), :], ka[hh])
                if masked:
                    s = jnp.where(row >= col, s, NEG)
                p = jnp.exp(s)
                dp = _dot_nt(doh, v2)
                ds = p * (dp - delta)
                dcs[hh] = dcs[hh] - jnp.sum(ds, axis=0, keepdims=True)
                dr_scr[hh, pl.ds(start, bq), :] += jnp.sum(ds, axis=1, keepdims=True)
                dsb = ds.astype(BF16)
                dv = dv + _dot_tn(p.astype(BF16), doh)
                dk = dk + _dot_tn(dsb, qh)
                dq = dq + _dot(dsb, kh[hh])
            dq_scr[pl.ds(start, bq), :] += dq
            return dk, dv, dcs[0], dcs[1]

        zero = jnp.zeros((bq, LANES), F32)
        zrow = jnp.zeros((1, bq), F32)
        carry = step(j, (zero, zero, zrow, zrow), True)
        dk, dv, dc0, dc1 = lax.fori_loop(j + 1, nq, functools.partial(step, masked=False), carry)
        dk_ref[...] = (dk * SCALE).astype(BF16)
        dv_ref[...] = dv.astype(BF16)
        dc_ref[0, 0, 0] = dc0
        dc_ref[0, 1, 0] = dc1

        @pl.when(j == nq - 1)
        def _():
            dq_ref[...] = (dq_scr[...] * SCALE).astype(BF16)
            step_r = min(4 * LANES, S)
            for hh in range(2):
                for t in range(S // step_r):
                    dr_ref[0, hh, :, step_r * t:step_r * (t + 1)] = _col_to_row(dr_scr[hh, step_r * t:step_r * (t + 1), :])

    sblk = lambda c: pl.BlockSpec((S, LANES), lambda b, p, j, c=c: (b, c + p))
    kblk = lambda c: pl.BlockSpec((bq, LANES), lambda b, p, j, c=c: (b * nq + j, c + p))
    rows_spec = pl.BlockSpec((1, 2, nq, 1, bq), lambda b, p, j: (b, p, 0, 0, 0))
    row_spec = pl.BlockSpec((1, 2, 1, S), lambda b, p, j: (b, p, 0, 0))
    return _hosted_call(
        body, comm, f"fox_bwd_{li}", (B, NH // 2, nq),
        in_specs=[sblk(qc), kblk(kc), kblk(vc), sblk(0), sblk(0), rows_spec, rows_spec],
        out_specs=[sblk(0), kblk(0), kblk(0), pl.BlockSpec((1, 2, 1, 1, bq), lambda b, p, j: (b, p, j, 0, 0)),
                   row_spec],
        out_shape=[jax.ShapeDtypeStruct((T, D), BF16), jax.ShapeDtypeStruct((T, D), BF16),
                   jax.ShapeDtypeStruct((T, D), BF16), jax.ShapeDtypeStruct((B, NH, nq, 1, bq), F32),
                   jax.ShapeDtypeStruct((B, NH, 1, S), F32)],
        scratch=[pltpu.VMEM((S, LANES), F32), pltpu.VMEM((2, S, 1), F32), pltpu.VMEM((2, S, LANES), BF16)],
        dims=("parallel", "parallel", "arbitrary"), operands=(proj, proj, proj, do, o, cum_row, lse))


def _swa_blocks(S):
    bq = min(512, S)
    return bq, S // bq, bq // LCH


def _dup_head(xw, kvl):
    m0 = _lane_iota() < HD
    a = jnp.where(m0 if kvl == 0 else jnp.logical_not(m0), xw, 0.0)
    return (a + pltpu.roll(a, HD, 1)).astype(BF16)


def _band(same_block):
    r = lax.broadcasted_iota(jnp.int32, (LCH, LCH), 0)
    c = lax.broadcasted_iota(jnp.int32, (LCH, LCH), 1)
    return (c <= r) if same_block else (c > r)


def _stack_heads(ref, rows, kvl):
    m0 = _lane_iota() < HD
    parts = []
    for ch in (2 * kvl, 2 * kvl + 1):
        x = ref[rows, LANES * ch:LANES * (ch + 1)]
        parts += [jnp.where(m0, x, jnp.zeros_like(x)), jnp.where(m0, jnp.zeros_like(x), x)]
    return jnp.concatenate(parts, axis=0)


def _stack_delta(do_ref, o_ref, rows, kvl, scale=None):
    parts = []
    for ch in (2 * kvl, 2 * kvl + 1):
        lanes = slice(LANES * ch, LANES * (ch + 1))
        prod = do_ref[rows, lanes].astype(F32) * o_ref[rows, lanes].astype(F32)
        parts += [_head_sum(prod, 0), _head_sum(prod, 1)]
    out = jnp.concatenate(parts, axis=0)
    return out if scale is None else out * scale


def _stack_cols(ref, rows, kvl):
    return jnp.concatenate([ref[0, 4 * kvl + t, rows, :] for t in range(4)], axis=0)


def _swa_fwd(proj, sinks, S, li):
    T = proj.shape[0]
    B = T // S
    bq, nq, nsub = _swa_blocks(S)
    nrow = S // LCH
    qc, zc, kc, vc = OFF_BQ // 512, OFF_BZ // 512, OFF_BK // LANES, OFF_BV // LANES

    def body(sk_ref, q_ref, z_ref, kp_ref, kc_ref, vp_ref, vc_ref, y_ref, o_ref, lse_ref):
        c, i = pl.program_id(0), pl.program_id(2)
        m0 = _lane_iota() < HD
        kw = jnp.concatenate([kp_ref[...].astype(F32), kc_ref[...].astype(F32)], axis=0)
        vw = jnp.concatenate([vp_ref[...].astype(F32), vc_ref[...].astype(F32)], axis=0)
        kd = (_dup_head(kw, 0), _dup_head(kw, 1))
        vd = (_dup_head(vw, 0), _dup_head(vw, 1))
        valid = jnp.concatenate([_band(False), _band(True)], axis=1)
        col = lax.broadcasted_iota(jnp.int32, (LCH, 2 * LCH), 1)
        valid_first = jnp.logical_and(valid, jnp.logical_or(col >= LCH, i > 0))
        valid4 = jnp.concatenate([valid] * 4, axis=0)
        valid4_first = jnp.concatenate([valid_first] * 4, axis=0)
        for r in range(nsub):
            rows = slice(LCH * r, LCH * (r + 1))
            msk = valid4_first if r == 0 else valid4
            for kvl in range(2):
                kwin = kd[kvl][LCH * r:LCH * (r + 2)]
                vwin = vd[kvl][LCH * r:LCH * (r + 2)]
                qs = _stack_heads(q_ref, rows, kvl)
                sink = jnp.concatenate([jnp.full((LCH, 1), sk_ref[8 * c + 4 * kvl + t], F32) for t in range(4)], axis=0)
                s = jnp.where(msk, _dot_nt(qs, kwin) * SCALE, NEG)
                m = jnp.maximum(jnp.max(s, axis=1, keepdims=True), sink)
                p = jnp.exp(s - m)
                l = jnp.sum(p, axis=1, keepdims=True) + jnp.exp(sink - m)
                os_ = _dot(p.astype(BF16), vwin) / l
                lse = m + jnp.log(l)
                for t in range(4):
                    lse_ref[0, 4 * kvl + t, rows, :] = lse[LCH * t:LCH * (t + 1)]
                for u in range(2):
                    lanes = slice(LANES * (2 * kvl + u), LANES * (2 * kvl + u + 1))
                    o2 = jnp.where(m0, os_[LCH * 2 * u:LCH * (2 * u + 1)], os_[LCH * (2 * u + 1):LCH * (2 * u + 2)])
                    z = z_ref[rows, lanes].astype(F32)
                    o_ref[rows, lanes] = o2.astype(BF16)
                    y_ref[rows, lanes] = (o2 * z * _sigmoid(z)).astype(BF16)

    wide = lambda cc: pl.BlockSpec((bq, 512), lambda c, b, i, cc=cc: (b * nq + i, cc + c))
    cur = lambda cc: pl.BlockSpec((bq, LANES), lambda c, b, i, cc=cc: (b * nq + i, cc + c))
    prev = lambda cc: pl.BlockSpec((LCH, LANES), lambda c, b, i, cc=cc: (b * nrow + jnp.maximum(i * nsub - 1, 0), cc + c))
    return pl.pallas_call(
        body, name=f"swa_fwd_{li}", grid=(2, B, nq),
        in_specs=[pl.BlockSpec(memory_space=pltpu.SMEM), wide(qc), wide(zc), prev(kc), cur(kc), prev(vc), cur(vc)],
        out_specs=[wide(0), wide(0), pl.BlockSpec((1, 8, bq, 1), lambda c, b, i: (b, c, i, 0))],
        out_shape=[jax.ShapeDtypeStruct((T, D), BF16), jax.ShapeDtypeStruct((T, D), BF16),
                   jax.ShapeDtypeStruct((B, NH, S, 1), F32)],
        compiler_params=_cparams(("parallel", "parallel", "parallel"), VMEM_LIMIT),
    )(sinks, proj, proj, proj, proj, proj, proj)


def _swa_bwd_dq(proj, do, o, lse, sinks, cos128, sin128, S, li):
    T = proj.shape[0]
    B = T // S
    bq, nq, nsub = _swa_blocks(S)
    nrow = S // LCH
    qc, kc, vc = OFF_BQ // 512, OFF_BK // LANES, OFF_BV // LANES

    def body(sk_ref, q_ref, do_ref, o_ref, lse_ref, kp_ref, kc_ref, vp_ref, vc_ref, cos_ref, sin_ref, dq_ref, dsk_ref):
        c, b, i = pl.program_id(0), pl.program_id(1), pl.program_id(2)

        @pl.when(jnp.logical_and(b == 0, i == 0))
        def _():
            dsk_ref[...] = jnp.zeros_like(dsk_ref)

        m0 = _lane_iota() < HD
        kw = jnp.concatenate([kp_ref[...].astype(F32), kc_ref[...].astype(F32)], axis=0)
        vw = jnp.concatenate([vp_ref[...].astype(F32), vc_ref[...].astype(F32)], axis=0)
        kd = (_dup_head(kw, 0), _dup_head(kw, 1))
        vd = (_dup_head(vw, 0), _dup_head(vw, 1))
        valid = jnp.concatenate([_band(False), _band(True)], axis=1)
        col = lax.broadcasted_iota(jnp.int32, (LCH, 2 * LCH), 1)
        valid_first = jnp.logical_and(valid, jnp.logical_or(col >= LCH, i > 0))
        dsk = [jnp.zeros((1, 1), F32) for _ in range(8)]
        valid4 = jnp.concatenate([valid] * 4, axis=0)
        valid4_first = jnp.concatenate([valid_first] * 4, axis=0)
        for r in range(nsub):
            rows = slice(LCH * r, LCH * (r + 1))
            msk = valid4_first if r == 0 else valid4
            for kvl in range(2):
                kwin = kd[kvl][LCH * r:LCH * (r + 2)]
                vwin = vd[kvl][LCH * r:LCH * (r + 2)]
                qs = _stack_heads(q_ref, rows, kvl)
                dos = _stack_heads(do_ref, rows, kvl)
                delta = _stack_delta(do_ref, o_ref, rows, kvl)
                lse = _stack_cols(lse_ref, rows, kvl)
                sink = jnp.concatenate([jnp.full((LCH, 1), sk_ref[8 * c + 4 * kvl + t], F32) for t in range(4)], axis=0)
                s = jnp.where(msk, _dot_nt(qs, kwin) * SCALE, NEG)
                p = jnp.exp(s - lse)
                ds = p * (_dot_nt(dos, vwin) - delta)
                dqs = _dot(ds.astype(BF16), kwin) * SCALE
                dsink = jnp.exp(sink - lse) * delta
                for t in range(4):
                    hl = 4 * kvl + t
                    dsk[hl] = dsk[hl] - jnp.sum(dsink[LCH * t:LCH * (t + 1)], axis=0, keepdims=True)
                for u in range(2):
                    lanes = slice(LANES * (2 * kvl + u), LANES * (2 * kvl + u + 1))
                    dq2 = jnp.where(m0, dqs[LCH * 2 * u:LCH * (2 * u + 1)], dqs[LCH * (2 * u + 1):LCH * (2 * u + 2)])
                    dq2 = dq2 * cos_ref[rows, :] - _rot_half(dq2) * sin_ref[rows, :]
                    dq_ref[rows, lanes] = dq2.astype(BF16)
        for hl in range(8):
            dsk_ref[0, hl:hl + 1, :] += jnp.broadcast_to(dsk[hl], (1, LANES))

    wide = lambda cc: pl.BlockSpec((bq, 512), lambda c, b, i, cc=cc: (b * nq + i, cc + c))
    cur = lambda cc: pl.BlockSpec((bq, LANES), lambda c, b, i, cc=cc: (b * nq + i, cc + c))
    prev = lambda cc: pl.BlockSpec((LCH, LANES), lambda c, b, i, cc=cc: (b * nrow + jnp.maximum(i * nsub - 1, 0), cc + c))
    pos = pl.BlockSpec((bq, LANES), lambda c, b, i: (i, 0))
    return pl.pallas_call(
        body, name=f"swa_bwd_dq_{li}", grid=(2, B, nq),
        in_specs=[pl.BlockSpec(memory_space=pltpu.SMEM), wide(qc), wide(0), wide(0),
                  pl.BlockSpec((1, 8, bq, 1), lambda c, b, i: (b, c, i, 0)),
                  prev(kc), cur(kc), prev(vc), cur(vc), pos, pos],
        out_specs=[wide(0), pl.BlockSpec((1, 8, LANES), lambda c, b, i: (c, 0, 0))],
        out_shape=[jax.ShapeDtypeStruct((T, D), BF16), jax.ShapeDtypeStruct((2, 8, LANES), F32)],
        compiler_params=_cparams(("arbitrary", "arbitrary", "arbitrary"), VMEM_LIMIT),
    )(sinks, proj, do, o, lse, proj, proj, proj, proj, cos128, sin128)


def _swa_bwd_dkv(proj, do, o, lse, cos128, sin128, S, li):
    T = proj.shape[0]
    B = T // S
    bk, nk, nsub = _swa_blocks(S)
    nrow = S // LCH
    qc, kc, vc = OFF_BQ // 512, OFF_BK // LANES, OFF_BV // LANES

    def body(q_ref, qn_ref, do_ref, don_ref, o_ref, on_ref, lse_ref, lsen_ref, k_ref, v_ref, cos_ref, sin_ref,
             dk_ref, dv_ref):
        j = pl.program_id(2)
        m0 = _lane_iota() < HD
        has_next = (j < nk - 1).astype(F32)
        kf = k_ref[...].astype(F32)
        vf = v_ref[...].astype(F32)
        kd = (_dup_head(kf, 0), _dup_head(kf, 1))
        vd = (_dup_head(vf, 0), _dup_head(vf, 1))
        lane = _lane_iota()

        def stat_rows(lse_r, do_r, o_r, rows, scale):
            a_lse = jnp.zeros((rows, LANES), F32)
            a_del = jnp.zeros((rows, LANES), F32)
            for ch in range(4):
                lanes = slice(LANES * ch, LANES * (ch + 1))
                prod = do_r[:, lanes].astype(F32) * o_r[:, lanes].astype(F32)
                for hh in range(2):
                    h = 2 * ch + hh
                    a_lse = jnp.where(lane == h, lse_r[0, h], a_lse)
                    a_del = jnp.where(lane == h, _head_sum(prod, hh), a_del)
            if scale is not None:
                a_del = a_del * scale
            return a_lse.T, a_del.T

        lse_t, del_t = stat_rows(lse_ref, do_ref, o_ref, bk, None)
        lsen_t, deln_t = stat_rows(lsen_ref, don_ref, on_ref, LCH, has_next)
        r_ = lax.broadcasted_iota(jnp.int32, (LCH, LCH), 0)
        c_ = lax.broadcasted_iota(jnp.int32, (LCH, LCH), 1)
        masks4 = (jnp.concatenate([r_ <= c_] * 4, axis=1), jnp.concatenate([r_ > c_] * 4, axis=1))
        for kr in range(nsub):
            krows = slice(LCH * kr, LCH * (kr + 1))
            dk = jnp.zeros((LCH, LANES), F32)
            dv = jnp.zeros((LCH, LANES), F32)
            for dq_blk in range(2):
                rq = kr + dq_blk
                nxt = rq == nsub
                qrows = slice(0, LCH) if nxt else slice(LCH * rq, LCH * (rq + 1))
                qr, dor = (qn_ref, don_ref) if nxt else (q_ref, do_ref)
                lt, dt_ = (lsen_t, deln_t) if nxt else (lse_t, del_t)
                for kvl in range(2):
                    qs = _stack_heads(qr, qrows, kvl)
                    dos = _stack_heads(dor, qrows, kvl)
                    if nxt:
                        dos = (dos.astype(F32) * has_next).astype(BF16)
                    lse_row = jnp.concatenate([lt[4 * kvl + t:4 * kvl + t + 1, qrows] for t in range(4)], axis=1)
                    del_row = jnp.concatenate([dt_[4 * kvl + t:4 * kvl + t + 1, qrows] for t in range(4)], axis=1)
                    st = jnp.where(masks4[dq_blk], _dot_nt(kd[kvl][krows], qs) * SCALE, NEG)
                    pt = jnp.exp(st - lse_row)
                    dst = pt * (_dot_nt(vd[kvl][krows], dos) - del_row)
                    dvc = _dot(pt.astype(BF16), dos)
                    dkc = _dot(dst.astype(BF16), qs) * SCALE
                    own = m0 if kvl == 0 else jnp.logical_not(m0)
                    dv = dv + jnp.where(own, dvc + pltpu.roll(dvc, HD, 1), 0.0)
                    dk = dk + jnp.where(own, dkc + pltpu.roll(dkc, HD, 1), 0.0)
            dk = dk * cos_ref[krows, :] - _rot_half(dk) * sin_ref[krows, :]
            dk_ref[krows, :] = dk.astype(BF16)
            dv_ref[krows, :] = dv.astype(BF16)

    wide = lambda cc: pl.BlockSpec((bk, 512), lambda c, b, j, cc=cc: (b * nk + j, cc + c))
    nxt = lambda cc: pl.BlockSpec((LCH, 512), lambda c, b, j, cc=cc: (b * nrow + jnp.minimum((j + 1) * nsub, nrow - 1), cc + c))
    cur = lambda cc: pl.BlockSpec((bk, LANES), lambda c, b, j, cc=cc: (b * nk + j, cc + c))
    pos = pl.BlockSpec((bk, LANES), lambda c, b, j: (j, 0))
    return pl.pallas_call(
        body, name=f"swa_bwd_dkv_{li}", grid=(2, B, nk),
        in_specs=[wide(qc), nxt(qc), wide(0), nxt(0), wide(0), nxt(0),
                  pl.BlockSpec((1, 8, bk, 1), lambda c, b, j: (b, c, j, 0)),
                  pl.BlockSpec((1, 8, LCH, 1), lambda c, b, j: (b, c, jnp.minimum((j + 1) * nsub, nrow - 1), 0)),
                  cur(kc), cur(vc), pos, pos],
        out_specs=[cur(0), cur(0)],
        out_shape=[jax.ShapeDtypeStruct((T, 2 * LANES), BF16), jax.ShapeDtypeStruct((T, 2 * LANES), BF16)],
        compiler_params=_cparams(("parallel", "parallel", "parallel"), VMEM_LIMIT),
    )(proj, proj, do, do, o, o, lse, lse, proj, proj, cos128, sin128)


HALO = 16


def _shift_matrices():
    r = lax.broadcasted_iota(jnp.int32, (3 * LCH, LCH + HALO), 0)
    c = lax.broadcasted_iota(jnp.int32, (3 * LCH, LCH + HALO), 1)
    t, d = r % LCH, r // LCH + 1
    return (c == HALO + t - d).astype(BF16), (c == t + d).astype(BF16)


def _ssm_chunk_pre(prev16, cur16, first, sdn_ref, cw_ref, cb_ref, ps, dtb, alog):
    ext16 = jnp.concatenate([jnp.where(first, jnp.zeros_like(prev16), prev16), cur16], axis=0)
    sh = _dot(sdn_ref[...], ext16)
    pre = cb_ref[...] + cw_ref[3:4, :] * cur16.astype(F32)
    for d in range(1, 4):
        pre = pre + cw_ref[3 - d:4 - d, :] * sh[LCH * (d - 1):LCH * d]
    sg = _sigmoid(pre)
    dt = _softplus(ps + dtb)
    a = -jnp.exp(alog)
    r = lax.broadcasted_iota(jnp.int32, (LCH, LCH), 0)
    c = lax.broadcasted_iota(jnp.int32, (LCH, LCH), 1)
    acum = _dot_hi((r >= c).astype(F32), dt * a)
    return pre, sg, dt, a, acum, sh


def _expand_matrix():
    r = lax.broadcasted_iota(jnp.int32, (3 * LANES, D), 0)
    c = lax.broadcasted_iota(jnp.int32, (3 * LANES, D), 1)
    return ((r % LANES) == c // HD).astype(BF16)


def _expand_heads(v, ex_ref):
    return _dot(jnp.concatenate(_split3(v), axis=1).astype(BF16), ex_ref[...])


def _decay(acum, acum_t, h):
    r = lax.broadcasted_iota(jnp.int32, (LCH, LCH), 0)
    c = lax.broadcasted_iota(jnp.int32, (LCH, LCH), 1)
    causal = r >= c
    seg = acum[:, h:h + 1] - acum_t[h:h + 1, :]
    return jnp.where(causal, jnp.exp(jnp.where(causal, seg, 0.0)), 0.0)


def _ssm_pair_fwd(p, x, dt_x, acum, acum_t, e_x, w_x, cd, cb_g, b_g, c_g, hprev, dsk_ref):
    m0 = _lane_iota() < HD
    lanes = slice(LANES * p, LANES * (p + 1))
    x2 = x[:, lanes]
    dt2 = dt_x[:, lanes]
    xdt2 = x2 * dt2
    xdtb = xdt2.astype(BF16)
    lms, ms, yds = [], [], []
    for hh in range(2):
        lm = _decay(acum, acum_t, 2 * p + hh)
        mm = cb_g * lm
        lms.append(lm)
        ms.append(mm)
        yds.append(_dot(mm.astype(BF16), xdtb))
    yd2 = jnp.where(m0, yds[0], yds[1])
    w2 = w_x[:, lanes]
    xw = (xdt2 * w2).astype(BF16)
    s2 = _dot_tn(xw, b_g)
    z2 = _dot_nt(c_g, hprev.astype(BF16))
    e2 = e_x[:, lanes]
    rowsel = lax.broadcasted_iota(jnp.int32, (LANES, 1), 0) < HD
    cdcol = jnp.where(rowsel, cd[:, 2 * p:2 * p + 1], cd[:, 2 * p + 1:2 * p + 2])
    y2 = yd2 + z2 * e2 + dsk_ref[:, lanes] * x2
    return dict(x2=x2, dt2=dt2, xdt2=xdt2, xdtb=xdtb, lms=lms, ms=ms, yd2=yd2, w2=w2, xw=xw, s2=s2, z2=z2, e2=e2,
                cdcol=cdcol, y2=y2)


def _ssm_specs(S, rev):
    nc = S // LCH
    ch = (lambda c: nc - 1 - c) if rev else (lambda c: c)
    prev = pl.BlockSpec((HALO, 2 * D), lambda b, c: (jnp.maximum(b * (S // HALO) + ch(c) * (LCH // HALO) - 1, 0), 0))
    cur = pl.BlockSpec((LCH, 2 * D), lambda b, c: (b * nc + ch(c), 0))
    zed = pl.BlockSpec((LCH, D), lambda b, c: (b * nc + ch(c), OFF_AZ // D))
    row = pl.BlockSpec((LCH, D), lambda b, c: (b * nc + ch(c), 0))
    psb = pl.BlockSpec((LCH, LANES), lambda b, c: (b * nc + ch(c), 0))
    hpb = pl.BlockSpec((1, 1, NH // 2, LANES, NST), lambda b, c: (b, ch(c), 0, 0, 0))
    const = lambda r, w: pl.BlockSpec((r, w), lambda b, c: (0, 0))
    return nc, prev, cur, zed, row, psb, hpb, const


def _ssm_fwd(proj, ps, cw, cb, dtb, alog, dsk, nw, S, li):
    T = proj.shape[0]
    B = T // S
    nc, prev, cur, zed, row, psb, hpb, const = _ssm_specs(S, False)

    def body(prev_ref, cur_ref, z_ref, ps_ref, sdn_ref, ex_ref, cw_ref, cb_ref, dtb_ref, alog_ref, dsk_ref, nw_ref,
             ya_ref, hp_ref, h_scr):
        c = pl.program_id(1)

        @pl.when(c == 0)
        def _():
            h_scr[...] = jnp.zeros_like(h_scr)

        pre, sg, dt, a, acum, _ = _ssm_chunk_pre(prev_ref[...], cur_ref[...], c == 0, sdn_ref, cw_ref, cb_ref,
                                                 ps_ref[...], dtb_ref[...], alog_ref[...])
        act = pre * sg
        acum_t = acum.T
        last = acum[LCH - 1:LCH, :]
        cd = jnp.exp(last)
        dt, e_all, w_all = (_expand_heads(v, ex_ref) for v in (dt, jnp.exp(acum), jnp.exp(last - acum)))
        x = act[:, :D]
        for g in range(NGRP):
            b_g = act[:, D + NST * g:D + NST * (g + 1)].astype(BF16)
            c_g = act[:, D + NGRP * NST + NST * g:D + NGRP * NST + NST * (g + 1)].astype(BF16)
            cb_g = _dot_nt(c_g, b_g)
            ygs = []
            for p in (2 * g, 2 * g + 1):
                hprev = h_scr[p]
                hp_ref[0, 0, p] = hprev
                f = _ssm_pair_fwd(p, x, dt, acum, acum_t, e_all, w_all, cd, cb_g, b_g, c_g, hprev, dsk_ref)
                h_scr[p] = hprev * f["cdcol"] + f["s2"]
                z2 = z_ref[:, LANES * p:LANES * (p + 1)].astype(F32)
                ygs.append(f["y2"] * z2 * _sigmoid(z2))
            yg = jnp.concatenate(ygs, axis=1)
            r = lax.rsqrt(jnp.mean(yg * yg, axis=1, keepdims=True) + EPS)
            ya_ref[:, 2 * LANES * g:2 * LANES * (g + 1)] = (yg * r * nw_ref[:, 2 * LANES * g:2 * LANES * (g + 1)]).astype(BF16)

    return pl.pallas_call(
        body, name=f"ssm_fwd_{li}", grid=(B, nc),
        in_specs=[prev, cur, zed, psb, const(3 * LCH, LCH + HALO), const(3 * LANES, D), const(4, 2 * D),
                  const(1, 2 * D), const(1, LANES), const(1, LANES), const(1, D), const(1, D)],
        out_specs=[row, hpb],
        out_shape=[jax.ShapeDtypeStruct((T, D), BF16), jax.ShapeDtypeStruct((B, nc, NH // 2, LANES, NST), F32)],
        scratch_shapes=[pltpu.VMEM((NH // 2, LANES, NST), F32)],
        compiler_params=_cparams(("arbitrary", "arbitrary"), VMEM_LIMIT),
    )(proj, proj, proj, ps, _shift_matrices()[0], _expand_matrix(), cw, cb, dtb, alog, dsk, nw)


def _ssm_bwd(proj, ps, hp, dya, cw, cb, dtb, alog, dsk, nw, S, li, comm=None):
    T = proj.shape[0]
    B = T // S
    nc, prev, cur, zed, row, psb, hpb, const = _ssm_specs(S, True)

    def body(prev_ref, cur_ref, z_ref, ps_ref, hp_ref, dy_ref, sdn_ref, sup_ref, ex_ref, cw_ref, cb_ref, dtb_ref,
             alog_ref, dsk_ref, nw_ref, dxbc_ref, dz_ref, dps_ref, pgw_ref, pg1_ref, pgh_ref, dh_scr, dhead, dact):
        b, cc = pl.program_id(0), pl.program_id(1)
        c = nc - 1 - cc

        @pl.when(jnp.logical_and(b == 0, cc == 0))
        def _():
            pgw_ref[...] = jnp.zeros_like(pgw_ref)
            pg1_ref[...] = jnp.zeros_like(pg1_ref)
            pgh_ref[...] = jnp.zeros_like(pgh_ref)

        @pl.when(cc == 0)
        def _():
            dh_scr[...] = jnp.zeros_like(dh_scr)
            dhead[...] = jnp.zeros_like(dhead)

        psv = ps_ref[...]
        cur16 = cur_ref[...]
        pre, sg, dt, a, acum, sh = _ssm_chunk_pre(prev_ref[...], cur16, c == 0, sdn_ref, cw_ref, cb_ref, psv,
                                                  dtb_ref[...], alog_ref[...])
        act = pre * sg
        acum_t = acum.T
        last = acum[LCH - 1:LCH, :]
        w_all = jnp.exp(last - acum)
        cd = jnp.exp(last)
        dt_x, e_x, w_x = (_expand_heads(v, ex_ref) for v in (dt, jnp.exp(acum), w_all))
        x = act[:, :D]
        lane = _lane_iota()
        m0 = lane < HD
        head_row = lax.broadcasted_iota(jnp.int32, (LANES, 1), 0)
        rowsel = head_row < HD
        is_last_row = lax.broadcasted_iota(jnp.int32, (LCH, 1), 0) == LCH - 1
        dacum_all = jnp.zeros((LCH, LANES), F32)
        dacum_t = jnp.zeros((LANES, LCH), F32)
        ddt_all = jnp.zeros((LCH, LANES), F32)
        dd_row = jnp.zeros((1, LANES), F32)
        for g in range(NGRP):
            b_g = act[:, D + NST * g:D + NST * (g + 1)].astype(BF16)
            c_g = act[:, D + NGRP * NST + NST * g:D + NGRP * NST + NST * (g + 1)].astype(BF16)
            cb_g = _dot_nt(c_g, b_g)
            pairs = (2 * g, 2 * g + 1)
            fs, hps, zs, ygs = [], [], [], []
            for p in pairs:
                hprev = hp_ref[0, 0, p]
                f = _ssm_pair_fwd(p, x, dt_x, acum, acum_t, e_x, w_x, cd, cb_g, b_g, c_g, hprev, dsk_ref)
                z2 = z_ref[:, LANES * p:LANES * (p + 1)].astype(F32)
                fs.append(f)
                hps.append(hprev)
                zs.append(z2)
                ygs.append(f["y2"] * z2 * _sigmoid(z2))
            gl = slice(2 * LANES * g, 2 * LANES * (g + 1))
            yg = jnp.concatenate(ygs, axis=1)
            r = lax.rsqrt(jnp.mean(yg * yg, axis=1, keepdims=True) + EPS)
            dyn = dy_ref[:, gl].astype(F32)
            gg = dyn * nw_ref[:, gl]
            dyg = r * gg - yg * (r * r * r) * jnp.mean(gg * yg, axis=1, keepdims=True)
            pg1_ref[0:1, gl] += jnp.sum(dyn * yg * r, axis=0, keepdims=True)
            dg_g = jnp.zeros((LCH, LCH), F32)
            db_g = jnp.zeros((LCH, NST), F32)
            dc_g = jnp.zeros((LCH, NST), F32)
            for idx, p in enumerate(pairs):
                f, hprev, z2 = fs[idx], hps[idx], zs[idx]
                lanes = slice(LANES * p, LANES * (p + 1))
                dyg2 = dyg[:, LANES * idx:LANES * (idx + 1)]
                sgz = _sigmoid(z2)
                dy2 = dyg2 * z2 * sgz
                dz_ref[:, lanes] = (dyg2 * f["y2"] * sgz * (1.0 + z2 * (1.0 - sgz))).astype(BF16)
                x2, dt2, xdt2, xdtb, w2, e2, z2m = f["x2"], f["dt2"], f["xdt2"], f["xdtb"], f["w2"], f["e2"], f["z2"]
                dx2 = dsk_ref[:, lanes] * dy2
                dyx = dy2 * x2
                dxdt2 = jnp.zeros((LCH, LANES), F32)
                diag_cols = []
                for hh in range(2):
                    sel = m0 if hh == 0 else jnp.logical_not(m0)
                    dyb = jnp.where(sel, dy2, 0.0).astype(BF16)
                    dm = _dot_nt(dyb, xdtb)
                    dg_g = dg_g + dm * f["lms"][hh]
                    dxdt2 = dxdt2 + _dot_tn(f["ms"][hh].astype(BF16), dyb)
                    em = dm * f["ms"][hh]
                    diag_cols.append(jnp.sum(em, axis=1, keepdims=True))
                    dacum_t = dacum_t - jnp.where(head_row == 2 * p + hh, jnp.sum(em, axis=0, keepdims=True), 0.0)
                dz2m = dy2 * e2
                t_off = dz2m * z2m
                dc_g = dc_g + _dot(dz2m.astype(BF16), hprev.astype(BF16))
                dhprev = _dot_tn(dz2m.astype(BF16), c_g)
                dhn = dh_scr[p]
                dhnb = dhn.astype(BF16)
                dhprev = dhprev + dhn * f["cdcol"]
                t_h = dhn * hprev
                dxw2 = _dot_nt(b_g, dhnb)
                db_g = db_g + _dot(f["xw"], dhnb)
                dxdt2 = dxdt2 + dxw2 * w2
                t_w = dxw2 * xdt2
                dx2 = dx2 + dxdt2 * dt2
                t_dt = dxdt2 * x2
                for hh in range(2):
                    h = 2 * p + hh
                    onehot = (lane == h).astype(F32)
                    w_col = w_all[:, h:h + 1]
                    dw_col = _head_sum(t_w, hh) * w_col
                    rs = rowsel if hh == 0 else jnp.logical_not(rowsel)
                    dlast = (jnp.sum(jnp.where(rs, t_h, 0.0), keepdims=True) * cd[:, h:h + 1]
                             + jnp.sum(dw_col, keepdims=True))
                    dacum_col = diag_cols[hh] + _head_sum(t_off, hh) - dw_col + jnp.where(is_last_row, dlast, 0.0)
                    dacum_all = dacum_all + dacum_col * onehot
                    ddt_all = ddt_all + _head_sum(t_dt, hh) * onehot
                    sel = m0 if hh == 0 else jnp.logical_not(m0)
                    dd_row = dd_row + jnp.sum(jnp.where(sel, dyx, 0.0), keepdims=True) * onehot
                dh_scr[p] = dhprev
                dact[:, lanes] = dx2
            dgb = dg_g.astype(BF16)
            dc_g = dc_g + _dot(dgb, b_g)
            db_g = db_g + _dot_tn(dgb, c_g)
            dact[:, D + NST * g:D + NST * (g + 1)] = db_g
            dact[:, D + NGRP * NST + NST * g:D + NGRP * NST + NST * (g + 1)] = dc_g
        rr = lax.broadcasted_iota(jnp.int32, (LCH, LCH), 0)
        cc2 = lax.broadcasted_iota(jnp.int32, (LCH, LCH), 1)
        dadt = _dot_hi((cc2 >= rr).astype(F32), dacum_all + dacum_t.T)
        ddt_all = ddt_all + dadt * a
        heads = lane < NH
        da = jnp.sum(dadt * dt, axis=0, keepdims=True)
        dr = jnp.where(heads, ddt_all * _sigmoid(psv + dtb_ref[...]), 0.0)
        dps_ref[...] = dr
        pgh_ref[0:1, :] += jnp.sum(dr, axis=0, keepdims=True)
        pgh_ref[1:2, :] += jnp.where(heads, da * a, 0.0)
        pgh_ref[2:3, :] += dd_row
        dpre = dact[...] * sg * (1.0 + pre * (1.0 - sg))
        extd = jnp.concatenate([dpre, dhead[...]], axis=0)
        hi = extd.astype(BF16)
        lo = (extd - hi.astype(F32)).astype(BF16)
        up = _dot(sup_ref[...], hi) + _dot(sup_ref[...], lo)
        du = cw_ref[3:4, :] * dpre
        pgw_ref[3:4, :] += jnp.sum(dpre * cur16.astype(F32), axis=0, keepdims=True)
        for d in range(1, 4):
            du = du + cw_ref[3 - d:4 - d, :] * up[LCH * (d - 1):LCH * d]
            pgw_ref[3 - d:4 - d, :] += jnp.sum(dpre * sh[LCH * (d - 1):LCH * d], axis=0, keepdims=True)
        pgw_ref[4:5, :] += jnp.sum(dpre, axis=0, keepdims=True)
        dxbc_ref[...] = du.astype(BF16)
        dhead[...] = dpre[0:HALO, :]

    xbc_out = pl.BlockSpec((LCH, 2 * D), lambda b, c: (b * nc + nc - 1 - c, 0))
    acc = lambda w: pl.BlockSpec((8, w), lambda b, c: (0, 0))
    sdn, sup = _shift_matrices()
    return _hosted_call(
        body, comm, f"ssm_bwd_{li}", (B, nc),
        in_specs=[prev, cur, zed, psb, hpb, row, const(3 * LCH, LCH + HALO), const(3 * LCH, LCH + HALO),
                  const(3 * LANES, D), const(4, 2 * D), const(1, 2 * D), const(1, LANES), const(1, LANES),
                  const(1, D), const(1, D)],
        out_specs=[xbc_out, row, psb, acc(2 * D), acc(D), acc(LANES)],
        out_shape=[jax.ShapeDtypeStruct((T, 2 * D), BF16), jax.ShapeDtypeStruct((T, D), BF16),
                   jax.ShapeDtypeStruct((T, LANES), F32), jax.ShapeDtypeStruct((8, 2 * D), F32),
                   jax.ShapeDtypeStruct((8, D), F32), jax.ShapeDtypeStruct((8, LANES), F32)],
        scratch=[pltpu.VMEM((NH // 2, LANES, NST), F32), pltpu.VMEM((HALO, 2 * D), F32),
                 pltpu.VMEM((LCH, 2 * D), F32)],
        dims=("arbitrary", "arbitrary"),
        operands=(proj, proj, proj, ps, hp, dya, sdn, sup, _expand_matrix(), cw, cb, dtb, alog, dsk, nw))


def _lane_row(v, offset):
    return jnp.pad(v.astype(F32), (offset, LANES - offset - v.shape[0]))[None]


def _pack_rows(arrays):
    parts = []
    for a in arrays:
        flat = a.reshape(-1).astype(F32)
        pad = (-flat.shape[0]) % LANES
        parts.append(jnp.pad(flat, (0, pad)))
    flat = jnp.concatenate(parts)
    pad = (-flat.shape[0]) % (8 * LANES)
    return jnp.pad(flat, (0, pad)).reshape(-1, LANES)


def _unpack_rows(pack, shapes):
    flat = pack.reshape(-1)
    out, pos = [], 0
    for shp in shapes:
        n = math.prod(shp)
        out.append(flat[pos:pos + n].reshape(shp))
        pos += n + (-n) % LANES
    return out


def _split_w_in(blocks):
    def cols(a, b):
        out = []
        for d in range(NDEV):
            lo, hi = max(a, d * NSH), min(b, (d + 1) * NSH)
            if lo < hi:
                out.append(blocks[d, :, lo - d * NSH:hi - d * NSH])
        return out

    main = jnp.concatenate(cols(0, 3072) + cols(3088, 4112) + cols(4624, 5648) + cols(5648, 8720)
                           + cols(8736, 12832) + cols(4112, 4624), axis=1)
    small = jnp.concatenate(cols(3072, 3088) + cols(8720, 8736) + [jnp.zeros((D, LANES - 2 * NH), blocks.dtype)],
                            axis=1)
    return main, small


def _w_in_blocks(dw, ds, r0, r1):
    xbc, az, bq, bz, cq, ck, cv, cz, gates, bk, bv = dw
    order = [xbc, az, ds[:, 0:NH], bq, bk, bv, bz, cq, ck, cv, ds[:, NH:2 * NH], cz, gates]
    blocks, pos = [[] for _ in range(NDEV)], 0
    for seg in order:
        w = seg.shape[1]
        for d in range(NDEV):
            lo, hi = max(pos, d * NSH), min(pos + w, (d + 1) * NSH)
            if lo < hi:
                blocks[d].append(seg[r0:r1, lo - pos:hi - pos])
        pos += w
    return jnp.stack([jnp.concatenate(b, axis=1) for b in blocks])


def kernel(x, norm_w, w_in, conv_w, conv_b, dt_bias, a_log, d_skip, ssm_norm_w, sinks, f_bias, gate_bias, w_proj, w_out, final_norm_w, loss_target, m_norm_w, m_w_in, m_conv_w, m_conv_b, m_dt_bias, m_a_log, m_d_skip, m_ssm_norm_w, m_sinks, m_f_bias, m_gate_bias, m_w_proj, m_w_out, m_final_norm_w, v_norm_w, v_w_in, v_conv_w, v_conv_b, v_dt_bias, v_a_log, v_d_skip, v_ssm_norm_w, v_sinks, v_f_bias, v_gate_bias, v_w_proj, v_w_out, v_final_norm_w):
    Bl, S, _ = x.shape
    T = Bl * S
    depth = norm_w.shape[0]
    me = 4 * lax.axis_index("x") + 2 * lax.axis_index("y") + lax.axis_index("c")
    csh, gsh = conv_w.shape[2], gate_bias.shape[2]

    def gather_plan(l):
        small = jnp.concatenate([conv_w[l].reshape(-1), gate_bias[l].reshape(-1)]).reshape(-1, LANES)
        return _Comm("gather", [w_in[l].astype(BF16), w_proj[l].astype(BF16), w_out[l].astype(BF16), small])

    def unpack_weights(res):
        g_win, g_wp, g_wo, g_small = res
        flat = g_small.reshape(NDEV, -1)
        return (_split_w_in(g_win),
                g_wp.transpose(1, 0, 2, 3).reshape(3, D, D),
                g_wo.reshape(D, D),
                flat[:, :4 * csh].reshape(NDEV, 4, csh).transpose(1, 0, 2).reshape(4, 2 * D),
                flat[:, 4 * csh:].reshape(NDEV, 3, gsh).transpose(1, 0, 2).reshape(3, D))

    def scatter_plan(gw_in_blocks=None, gw_p=None, gw_o=None):
        arrays = [] if gw_in_blocks is None else [gw_in_blocks]
        if gw_p is not None:
            arrays += [gw_p.astype(BF16).reshape(3, NDEV, D // NDEV, D).transpose(1, 0, 2, 3),
                       gw_o.astype(BF16).reshape(NDEV, D // NDEV, D)]
        return _Comm("scatter", arrays)

    pos = jnp.arange(S, dtype=F32)
    inv_freq = ROPE_THETA ** (-jnp.arange(0, HD, 2, dtype=F32) / HD)
    ang = pos[:, None] * inv_freq[None, :]
    cos128 = jnp.tile(jnp.cos(ang), (1, 4))
    sign = jnp.where((jnp.arange(LANES) % HD) < HD // 2, -1.0, 1.0).astype(F32)
    sin128 = jnp.tile(jnp.sin(ang), (1, 4)) * sign[None, :]

    x2 = x.reshape(T, D)
    tgt2 = loss_target.reshape(T, D)
    fox_bq = _fox_blocks(S)[0]

    saved = []
    xcur = x2
    weights = [None] * depth
    weights[0] = unpack_weights(_gather_two_level(gather_plan(0).arrays, "gather_weights_0"))
    for l in range(depth):
        (wmain, wsmall), wp_l, wo_l, cw_l, gb_l = weights[l]
        proj, ps, h_t = _inproj_fwd(xcur, norm_w[l][None], wmain, wsmall, cos128, sin128, S, l)
        dtb = _lane_row(dt_bias[l], 0)
        alog = _lane_row(a_log[l], 0)
        fb = _lane_row(f_bias[l], NH)
        dsk = jnp.repeat(d_skip[l], HD)[None]
        ya, hp = _ssm_fwd(proj, ps, cw_l, conv_b[l][None], dtb, alog, dsk, ssm_norm_w[l][None], S, l)
        yb, ob, lse_b = _swa_fwd(proj, sinks[l], S, l)
        cum = _fox_cum(ps, fb, S, l)
        cumh = cum[:, NH:2 * NH].reshape(Bl, S, NH).transpose(0, 2, 1)
        cum_row = cumh.reshape(Bl, NH, S // fox_bq, 1, fox_bq)
        comm = gather_plan(l + 1) if l + 1 < depth else None
        res = _fox_fwd(proj, cum_row, S, l, comm)
        yc, oc, lse_c = res[:3]
        if comm is not None:
            weights[l + 1] = unpack_weights(res[3:])
        xnext, br, y_t = _merge_fwd(ya, yb, yc, proj, gb_l, wp_l, wo_l, xcur, l)
        saved.append(dict(x=xcur, wmain=wmain, wsmall=wsmall, proj=proj, ps=ps, h_t=h_t, dtb=dtb, alog=alog, fb=fb,
                          dsk=dsk, hp=hp, ob=ob, lse_b=lse_b, cum_row=cum_row, oc=oc, lse_c=lse_c, br=br, y_t=y_t))
        xcur = xnext

    dx, dx16, st = _final_loss(xcur, tgt2, final_norm_w[None])
    loss_part = st[2, 0]
    g_final = st[0]

    gsm = {k: [None] * depth for k in ("norm_w", "conv_w", "conv_b", "dt_bias", "a_log", "d_skip", "ssm_norm_w",
                                      "sinks", "f_bias", "gate_bias")}
    parts = [None] * depth
    pending = None
    for l in reversed(range(depth)):
        sv = saved[l]
        proj, ps = sv["proj"], sv["ps"]
        _, wp_l, wo_l, cw_l, gb_l = weights[l]
        dbr, dgates, merged_t, dgb, dy_a, do_b, dbz, do_c, dcz = _merge_bwd(dx16, wo_l, wp_l, sv["br"], proj, gb_l,
                                                                            sv["ob"], sv["oc"], l)
        g_wo = _matmul(merged_t, dx16, BF16, f"dwout_{l}")
        g_wp = _matmul_batched(sv["y_t"], dbr, BF16, f"dwproj_{l}")
        gsm["gate_bias"][l] = dgb[0:3]
        hosted = ([] if pending is None else pending.arrays) + (scatter_plan(None, g_wp, g_wo).arrays if l == 0 else [])
        res = _ssm_bwd(proj, ps, sv["hp"], dy_a, cw_l, conv_b[l][None], sv["dtb"], sv["alog"], sv["dsk"],
                       ssm_norm_w[l][None], S, l, _Comm("scatter", hosted) if hosted else None)
        dxbc, daz, dps_a, pgw, pg1, pgh = res[:6]
        if pending is not None:
            parts[l + 1] = res[6:9]
        if l == 0:
            parts_po = res[len(res) - 2:]
        gsm["conv_w"][l], gsm["conv_b"][l] = pgw[0:4], pgw[4]
        gsm["ssm_norm_w"][l] = pg1[0]
        gsm["dt_bias"][l], gsm["a_log"][l], gsm["d_skip"][l] = pgh[0, :NH], pgh[1, :NH], pgh[2, :NH]
        dq_b, dsk_b = _swa_bwd_dq(proj, do_b, sv["ob"], sv["lse_b"], sinks[l], cos128, sin128, S, l)
        dk_b, dv_b = _swa_bwd_dkv(proj, do_b, sv["ob"], sv["lse_b"], cos128, sin128, S, l)
        gsm["sinks"][l] = dsk_b[:, :, 0].reshape(NH)
        dq_c, dk_c, dv_c, dcum_k, dcum_q = _fox_bwd(proj, do_c, sv["oc"], sv["cum_row"], sv["lse_c"], S, l)
        dcum_tm = (dcum_k.reshape(Bl, NH, S) + dcum_q.reshape(Bl, NH, S)).transpose(0, 2, 1).reshape(T, NH)
        dcum_pad = jnp.pad(dcum_tm, ((0, 0), (NH, LANES - 2 * NH)))
        df, dfb = _fox_cum_bwd(dcum_pad, ps, sv["fb"], S, l)
        gsm["f_bias"][l] = dfb[0, NH:2 * NH]
        dps16 = (dps_a + df).astype(BF16)
        pieces = (dxbc, daz, dq_b, dbz, dq_c, dk_c, dv_c, dcz, dgates, dk_b, dv_b)
        dw_pieces = [_matmul(sv["h_t"], pc, BF16, f"dwin_{l}_{i}") for i, pc in enumerate(pieces)]
        dws = _matmul(sv["h_t"], dps16, BF16, f"dwin_small_{l}")
        if l == 0:
            plans = [scatter_plan(_w_in_blocks(dw_pieces, dws, r0, r1)) for r0, r1 in ROW_CHUNKS]
        else:
            plans, pending = [None] * len(ROW_CHUNKS), scatter_plan(_w_in_blocks(dw_pieces, dws, 0, D), g_wp, g_wo)
        dkv_b = jnp.concatenate([dk_b, dv_b], axis=1)
        res1 = _inproj_bwd_dx([(dxbc, OFF_XBC), (daz, OFF_AZ), (dq_b, OFF_BQ), (dbz, OFF_BZ)], sv["wmain"],
                              ("narrow", dps16, sv["wsmall"]), None, f"inproj_bwd_dh1_{l}", plans[0])
        res2 = _inproj_bwd_dx([(dq_c, OFF_CQ), (dk_c, OFF_CK), (dv_c, OFF_CV), (dcz, OFF_CZ)], sv["wmain"],
                              ("acc", res1[0]), None, f"inproj_bwd_dh2_{l}", plans[1])
        dx, dx16, dnw = _inproj_bwd_dx([(dgates, OFF_G), (dkv_b, OFF_BK)], sv["wmain"], ("acc", res2[0]),
                                       (sv["x"], norm_w[l][None], dx), f"inproj_bwd_dx_{l}")
        if l == 0:
            parts[0] = [jnp.concatenate([res1[1], res2[1]], axis=1), *parts_po]
        gsm["norm_w"][l] = dnw[0]

    big = {}
    for idx, (name, w, m, v) in enumerate((("w_in", w_in, m_w_in, v_w_in), ("w_proj", w_proj, m_w_proj, v_w_proj),
                                          ("w_out", w_out, m_w_out, v_w_out))):
        cols = w.shape[-1]
        res = _sum_adamw([parts[l][idx].reshape(NDEV, -1, cols) for l in range(depth)], w.reshape(depth, -1, cols),
                         m.reshape(depth, -1, cols), v.reshape(depth, -1, cols), f"adamw_{name}")
        big[name] = [r.reshape(w.shape) for r in res]

    small_names = ("norm_w", "conv_b", "dt_bias", "a_log", "d_skip", "ssm_norm_w", "sinks", "f_bias")
    small_parts = [jnp.stack(gsm[k]) for k in small_names] + [g_final, jnp.stack(gsm["conv_w"]),
                                                              jnp.stack(gsm["gate_bias"]), loss_part.reshape(1)]
    shapes = [a.shape for a in small_parts]
    summed = _unpack_rows(_all_reduce_small(_pack_rows(small_parts)), shapes)
    g_small = dict(zip(small_names, summed[:len(small_names)]))
    g_small["final_norm_w"] = summed[len(small_names)]
    g_small["conv_w"] = lax.dynamic_slice_in_dim(summed[len(small_names) + 1], me * csh, csh, axis=2)
    g_small["gate_bias"] = lax.dynamic_slice_in_dim(summed[len(small_names) + 2], me * gsh, gsh, axis=2)
    loss = summed[len(small_names) + 3][0]

    ws = dict(norm_w=norm_w, conv_w=conv_w, conv_b=conv_b, dt_bias=dt_bias, a_log=a_log, d_skip=d_skip,
              ssm_norm_w=ssm_norm_w, sinks=sinks, f_bias=f_bias, gate_bias=gate_bias, final_norm_w=final_norm_w)
    ms = dict(norm_w=m_norm_w, conv_w=m_conv_w, conv_b=m_conv_b, dt_bias=m_dt_bias, a_log=m_a_log, d_skip=m_d_skip,
              ssm_norm_w=m_ssm_norm_w, sinks=m_sinks, f_bias=m_f_bias, gate_bias=m_gate_bias,
              final_norm_w=m_final_norm_w)
    vs = dict(norm_w=v_norm_w, conv_w=v_conv_w, conv_b=v_conv_b, dt_bias=v_dt_bias, a_log=v_a_log, d_skip=v_d_skip,
              ssm_norm_w=v_ssm_norm_w, sinks=v_sinks, f_bias=v_f_bias, gate_bias=v_gate_bias,
              final_norm_w=v_final_norm_w)
    order = list(ws)
    oshapes = [ws[k].shape for k in order]
    res = _adamw_small(_pack_rows([g_small[k] for k in order]), _pack_rows([ws[k] for k in order]),
                       _pack_rows([ms[k] for k in order]), _pack_rows([vs[k] for k in order]))
    d_s, m_s, v_s = (dict(zip(order, _unpack_rows(r, oshapes))) for r in res)

    names = ("norm_w", "w_in", "conv_w", "conv_b", "dt_bias", "a_log", "d_skip", "ssm_norm_w", "sinks", "f_bias",
             "gate_bias", "w_proj", "w_out", "final_norm_w")
    grads, deltas, new_m, new_v = [], [], [], []
    for k in names:
        if k in big:
            g, d_, m_, v_ = big[k]
        else:
            g, d_, m_, v_ = g_small[k], d_s[k], m_s[k], v_s[k]
        grads.append(g)
        deltas.append(d_)
        new_m.append(m_)
        new_v.append(v_)
    return (loss, dx.reshape(Bl, S, D), *grads, *deltas, *new_m, *new_v)
```

```python
import functools
import math

import jax
import jax.numpy as jnp
from jax import lax
from jax.experimental import pallas as pl
from jax.experimental.pallas import tpu as pltpu

F32 = jnp.float32
BF16 = jnp.bfloat16
MESH = pl.DeviceIdType.MESH
NDEV = 8

D = 1024
NH = 16
HD = 64
NST = 128
NGRP = 4
LCH = 128
EPS = 1e-6
ROPE_THETA = 10000.0
SCALE = HD ** -0.5
NEG = -1e30

LANES = 128
VMEM_LIMIT = 56 * 1024 * 1024

OFF_XBC, OFF_AZ, OFF_BQ, OFF_BZ, OFF_CQ, OFF_CK, OFF_CV, OFF_CZ, OFF_G, OFF_BK, OFF_BV = (
    0, 2048, 3072, 4096, 5120, 6144, 7168, 8192, 9216, 12288, 12544)
NMAIN = 12800
NIN = 12832
NSH = NIN // NDEV

ROW_CHUNKS = ((0, 512), (512, 1024))

ADAM_LR, ADAM_B1, ADAM_B2, ADAM_EPS, ADAM_WD, ADAM_STEP = 0.001, 0.9, 0.999, 1e-08, 0.01, 10


def _cparams(dims=None, vmem=None):
    return pltpu.CompilerParams(dimension_semantics=dims, vmem_limit_bytes=vmem)


def _dot(a, b):
    return jnp.dot(a, b, preferred_element_type=F32)


def _dot_nt(a, b):
    return lax.dot_general(a, b, (((1,), (1,)), ((), ())), preferred_element_type=F32)


def _dot_tn(a, b):
    return lax.dot_general(a, b, (((0,), (0,)), ((), ())), preferred_element_type=F32)


def _dot_hi(a, b):
    return jnp.dot(a, b, precision=lax.Precision.HIGHEST, preferred_element_type=F32)


def _sigmoid(x):
    return 0.5 * jnp.tanh(0.5 * x) + 0.5


def _softplus(x):
    return jnp.maximum(x, 0.0) + jnp.log(1.0 + jnp.exp(-jnp.abs(x)))


def _lane_iota(n=LANES):
    return lax.broadcasted_iota(jnp.int32, (1, n), 1)


def _rot_half(x):
    first = (_lane_iota() % HD) < (HD // 2)
    return jnp.where(first, pltpu.roll(x, LANES - HD // 2, 1), pltpu.roll(x, HD // 2, 1))


def _head_sum(x, head):
    m = (_lane_iota() < HD) if head == 0 else (_lane_iota() >= HD)
    return jnp.sum(jnp.where(m, x, 0.0), axis=1, keepdims=True)


def _me_and_peers():
    x, y, c = lax.axis_index("x"), lax.axis_index("y"), lax.axis_index("c")
    me = 4 * x + 2 * y + c
    peers = []
    for k in range(1, NDEV):
        kx, ky, kc = (k >> 2) & 1, (k >> 1) & 1, k & 1
        px, py, pc = x ^ kx, y ^ ky, c ^ kc
        peers.append(((px, py, pc), 4 * px + 2 * py + pc))
    return me, peers


class _Comm:
    def __init__(self, kind, arrays):
        self.kind, self.arrays, self.n = kind, list(arrays), len(arrays)
        any_spec = pl.BlockSpec(memory_space=pl.ANY)
        self.in_specs = [any_spec] * self.n
        self.out_specs = [any_spec] * self.n
        self.out_shape = [jax.ShapeDtypeStruct(((NDEV,) + a.shape) if kind == "gather" else a.shape, a.dtype)
                          for a in self.arrays]
        self.scratch = [pltpu.SemaphoreType.DMA((self.n, NDEV - 1)), pltpu.SemaphoreType.DMA((self.n, NDEV - 1)),
                        pltpu.SemaphoreType.DMA((self.n,))]

    def copies(self, ins, outs, sems):
        send_sems, recv_sems, local_sems = sems
        me, peers = _me_and_peers()
        out = []
        for a in range(self.n):
            mine = ins[a] if self.kind == "gather" else ins[a].at[me]
            out.append(pltpu.make_async_copy(mine, outs[a].at[me], local_sems.at[a]))
            for k, (peer, pidx) in enumerate(peers):
                src = ins[a] if self.kind == "gather" else ins[a].at[pidx]
                out.append(pltpu.make_async_remote_copy(
                    src_ref=src, dst_ref=outs[a].at[me], send_sem=send_sems.at[a, k], recv_sem=recv_sems.at[a, k],
                    device_id=peer, device_id_type=MESH))
        return out


def _gather_two_level(arrays, name):
    n = len(arrays)

    def body(*refs):
        ins, outs = refs[:n], refs[n:2 * n]
        send_sems, recv_sems, local_sems = refs[2 * n:]
        x, y, c = lax.axis_index("x"), lax.axis_index("y"), lax.axis_index("c")
        me, sibling = (x, y, c), (x, y, 1 - c)
        chips = [(1 - x, y), (x, 1 - y), (1 - x, 1 - y)]

        def slot(a, dev):
            return outs[a].at[4 * dev[0] + 2 * dev[1] + dev[2]]

        def copy(a, k, block, to, src=None):
            return pltpu.make_async_remote_copy(
                src_ref=slot(a, block) if src is None else src, dst_ref=slot(a, block),
                send_sem=send_sems.at[a, k], recv_sem=recv_sems.at[a, k], device_id=to, device_id_type=MESH)

        mine = [pltpu.make_async_copy(ins[a], slot(a, me), local_sems.at[a]) for a in range(n)]
        for cp in mine:
            cp.start()
        first = []
        for a in range(n):
            first.append(copy(a, 0, me, sibling, src=ins[a]))
            first += [copy(a, 1 + j, me, (*chip, c), src=ins[a]) for j, chip in enumerate(chips)]
        for cp in first:
            cp.start()
        passed = []
        for j, chip in enumerate(chips):
            for a in range(n):
                copy(a, 1 + j, (*chip, c), me).wait_recv()
                fwd = copy(a, 4 + j, (*chip, c), sibling)
                fwd.start()
                passed.append(fwd)
        for a in range(n):
            copy(a, 0, sibling, me).wait_recv()
            for j, chip in enumerate(chips):
                copy(a, 4 + j, (*chip, 1 - c), me).wait_recv()
        for cp in first + passed:
            cp.wait_send()
        for cp in mine:
            cp.wait()

    any_spec = pl.BlockSpec(memory_space=pl.ANY)
    return pl.pallas_call(
        body, name=name, out_shape=[jax.ShapeDtypeStruct((NDEV,) + a.shape, a.dtype) for a in arrays],
        in_specs=[any_spec] * n, out_specs=[any_spec] * n,
        scratch_shapes=[pltpu.SemaphoreType.DMA((n, NDEV - 1)), pltpu.SemaphoreType.DMA((n, NDEV - 1)),
                        pltpu.SemaphoreType.DMA((n,))])(*arrays)


def _hosted_call(body, comm, name, grid, in_specs, out_specs, out_shape, scratch, dims, operands):
    if comm is None:
        return pl.pallas_call(body, name=name, grid=grid, in_specs=in_specs, out_specs=out_specs, out_shape=out_shape,
                              scratch_shapes=scratch, compiler_params=_cparams(dims, VMEM_LIMIT))(*operands)
    n_in, n_out, n_scr, n = len(in_specs), len(out_specs), len(scratch), comm.n

    def hosted(*refs):
        hin, cin = refs[:n_in], refs[n_in:n_in + n]
        hout = refs[n_in + n:n_in + n + n_out]
        cout = refs[n_in + n + n_out:n_in + 2 * n + n_out]
        hscr = refs[n_in + 2 * n + n_out:n_in + 2 * n + n_out + n_scr]
        sems = refs[n_in + 2 * n + n_out + n_scr:]
        ids = [pl.program_id(a) for a in range(len(grid))]
        first = functools.reduce(jnp.logical_and, [i == 0 for i in ids])
        last = functools.reduce(jnp.logical_and, [i == g - 1 for i, g in zip(ids, grid)])

        @pl.when(first)
        def _():
            for cp in comm.copies(cin, cout, sems):
                cp.start()

        body(*hin, *hout, *hscr)

        @pl.when(last)
        def _():
            for cp in comm.copies(cin, cout, sems):
                cp.wait()

    return pl.pallas_call(
        hosted, name=name, grid=grid, in_specs=list(in_specs) + comm.in_specs,
        out_specs=list(out_specs) + comm.out_specs, out_shape=list(out_shape) + comm.out_shape,
        scratch_shapes=list(scratch) + comm.scratch,
        compiler_params=_cparams(("arbitrary",) * len(grid), VMEM_LIMIT))(*operands, *comm.arrays)


def _all_reduce_small(v):
    rows = v.shape[0]

    def body(v_ref, sum_ref, all_ref, send_sems, recv_sems):
        me, peers = _me_and_peers()
        all_ref[me] = v_ref[...]
        copies = []
        for k, (peer, _) in enumerate(peers):
            cp = pltpu.make_async_remote_copy(
                src_ref=v_ref, dst_ref=all_ref.at[me],
                send_sem=send_sems.at[k], recv_sem=recv_sems.at[k],
                device_id=peer, device_id_type=MESH)
            cp.start()
            copies.append(cp)
        for cp in copies:
            cp.wait()
        acc = all_ref[0]
        for d in range(1, NDEV):
            acc = acc + all_ref[d]
        sum_ref[...] = acc

    vm = pl.BlockSpec(memory_space=pltpu.VMEM)
    return pl.pallas_call(
        body, name="all_reduce_small",
        out_shape=jax.ShapeDtypeStruct((rows, LANES), F32),
        in_specs=[vm], out_specs=vm,
        scratch_shapes=[pltpu.VMEM((NDEV, rows, LANES), F32),
                        pltpu.SemaphoreType.DMA((NDEV - 1,)), pltpu.SemaphoreType.DMA((NDEV - 1,))],
    )(v)


def _adamw_math(w, g, m, v):
    m = ADAM_B1 * m + (1.0 - ADAM_B1) * g
    v = ADAM_B2 * v + (1.0 - ADAM_B2) * jnp.square(g)
    m_hat = m / (1.0 - ADAM_B1 ** ADAM_STEP)
    v_hat = v / (1.0 - ADAM_B2 ** ADAM_STEP)
    delta = -ADAM_LR * (m_hat / (jnp.sqrt(v_hat) + ADAM_EPS) + ADAM_WD * w)
    return delta, m, v


def _sum_adamw(parts, w, m, v, name):
    depth, rows, cols = w.shape
    tr = next(c for c in (256, 128, 64, 32, 16) if rows % c == 0)
    nb = rows // tr

    def body(*refs):
        p_refs, (w_ref, m_ref, v_ref, g_ref, d_ref, nm_ref, nv_ref) = refs[:depth], refs[depth:]
        l = pl.program_id(0)
        for ll in range(depth):
            @pl.when(l == ll)
            def _(ll=ll):
                g = p_refs[ll][0].astype(F32)
                for d in range(1, NDEV):
                    g = g + p_refs[ll][d].astype(F32)
                delta, nm, nv = _adamw_math(w_ref[0], g, m_ref[0], v_ref[0])
                g_ref[0] = g
                d_ref[0] = delta
                nm_ref[0] = nm
                nv_ref[0] = nv

    part = lambda ll: pl.BlockSpec((NDEV, tr, cols), lambda l, i, ll=ll: (0, jnp.where(l == ll, i, jnp.where(l < ll, 0, nb - 1)), 0))
    blk = pl.BlockSpec((1, tr, cols), lambda l, i: (l, i, 0))
    sds = jax.ShapeDtypeStruct((depth, rows, cols), F32)
    return pl.pallas_call(
        body, name=name, grid=(depth, nb),
        in_specs=[part(ll) for ll in range(depth)] + [blk, blk, blk],
        out_specs=[blk, blk, blk, blk], out_shape=[sds, sds, sds, sds],
        compiler_params=_cparams(("arbitrary", "arbitrary"), VMEM_LIMIT),
    )(*parts, w, m, v)


def _adamw_small(g, w, m, v):
    def body(g_ref, w_ref, m_ref, v_ref, d_ref, nm_ref, nv_ref):
        delta, nm, nv = _adamw_math(w_ref[...], g_ref[...], m_ref[...], v_ref[...])
        d_ref[...] = delta
        nm_ref[...] = nm
        nv_ref[...] = nv

    sds = jax.ShapeDtypeStruct(g.shape, F32)
    return pl.pallas_call(body, name="adamw_small", out_shape=[sds, sds, sds])(g, w, m, v)


def _matmul(a, b, out_dtype, name, tm=1024, tn=1024, tk=1024):
    M, K = a.shape
    N = b.shape[1]
    tm, tn, tk = min(tm, M), min(tn, N), min(tk, K)
    nk = K // tk

    def body(a_ref, b_ref, o_ref, acc):
        k = pl.program_id(2)

        @pl.when(k == 0)
        def _():
            acc[...] = jnp.zeros_like(acc)

        acc[...] += _dot(a_ref[...], b_ref[...])

        @pl.when(k == nk - 1)
        def _():
            o_ref[...] = acc[...].astype(out_dtype)

    return pl.pallas_call(
        body, name=name, grid=(M // tm, N // tn, nk),
        in_specs=[pl.BlockSpec((tm, tk), lambda i, j, k: (i, k)), pl.BlockSpec((tk, tn), lambda i, j, k: (k, j))],
        out_specs=pl.BlockSpec((tm, tn), lambda i, j, k: (i, j)),
        out_shape=jax.ShapeDtypeStruct((M, N), out_dtype),
        scratch_shapes=[pltpu.VMEM((tm, tn), F32)],
        compiler_params=_cparams(("parallel", "parallel", "arbitrary"), VMEM_LIMIT),
    )(a, b)


def _matmul_batched(a, b, out_dtype, name, tm=1024, tn=1024, tk=512):
    G, M, K = a.shape
    N = b.shape[2]
    tm, tn, tk = min(tm, M), min(tn, N), min(tk, K)
    nk = K // tk

    def body(a_ref, b_ref, o_ref, acc):
        k = pl.program_id(3)

        @pl.when(k == 0)
        def _():
            acc[...] = jnp.zeros_like(acc)

        acc[...] += _dot(a_ref[0], b_ref[0])

        @pl.when(k == nk - 1)
        def _():
            o_ref[0] = acc[...].astype(out_dtype)

    return pl.pallas_call(
        body, name=name, grid=(G, M // tm, N // tn, nk),
        in_specs=[pl.BlockSpec((1, tm, tk), lambda g, i, j, k: (g, i, k)),
                  pl.BlockSpec((1, tk, tn), lambda g, i, j, k: (g, k, j))],
        out_specs=pl.BlockSpec((1, tm, tn), lambda g, i, j, k: (g, i, j)),
        out_shape=jax.ShapeDtypeStruct((G, M, N), out_dtype),
        scratch_shapes=[pltpu.VMEM((tm, tn), F32)],
        compiler_params=_cparams(("parallel", "parallel", "parallel", "arbitrary"), VMEM_LIMIT),
    )(a, b)


def _inproj_fwd(x2, nw, wmain, wsmall, cos128, sin128, S, li, comm=None):
    T = x2.shape[0]
    tm, tn = min(2048, S), 512
    nj, npos = NMAIN // tn, S // tm
    jq0, jk = OFF_BQ // tn, OFF_BK // tn

    def body(x_ref, nw_ref, w_ref, ws_ref, cos_ref, sin_ref, proj_ref, ps_ref, ht_ref, h_scr):
        j = pl.program_id(1)

        @pl.when(j == 0)
        def _():
            x = x_ref[...]
            r = lax.rsqrt(jnp.mean(x * x, axis=-1, keepdims=True) + EPS)
            h = (x * r * nw_ref[...]).astype(BF16)
            h_scr[...] = h
            ht_ref[...] = h.T
            ps_ref[...] = _dot(h, ws_ref[...])

        acc = _dot(h_scr[...], w_ref[...])

        def roped(c):
            xc = acc[:, LANES * c:LANES * (c + 1)]
            return (xc * cos_ref[...] + _rot_half(xc) * sin_ref[...]).astype(BF16)

        def plain(c):
            return acc[:, LANES * c:LANES * (c + 1)].astype(BF16)

        is_q = jnp.logical_or(j == jq0, j == jq0 + 1)
        is_k = j == jk

        @pl.when(is_q)
        def _():
            for c in range(4):
                proj_ref[:, LANES * c:LANES * (c + 1)] = roped(c)

        @pl.when(is_k)
        def _():
            for c in range(4):
                proj_ref[:, LANES * c:LANES * (c + 1)] = roped(c) if c < 2 else plain(c)

        @pl.when(jnp.logical_not(jnp.logical_or(is_q, is_k)))
        def _():
            proj_ref[...] = acc.astype(BF16)

    return _hosted_call(
        body, comm, f"inproj_fwd_{li}", (T // tm, nj),
        in_specs=[pl.BlockSpec((tm, D), lambda i, j: (i, 0)),
                  pl.BlockSpec((1, D), lambda i, j: (0, 0)),
                  pl.BlockSpec((D, tn), lambda i, j: (0, j)),
                  pl.BlockSpec((D, LANES), lambda i, j: (0, 0)),
                  pl.BlockSpec((tm, LANES), lambda i, j: (i % npos, 0)),
                  pl.BlockSpec((tm, LANES), lambda i, j: (i % npos, 0))],
        out_specs=[pl.BlockSpec((tm, tn), lambda i, j: (i, j)),
                   pl.BlockSpec((tm, LANES), lambda i, j: (i, 0)),
                   pl.BlockSpec((D, tm), lambda i, j: (0, i))],
        out_shape=[jax.ShapeDtypeStruct((T, NMAIN), BF16), jax.ShapeDtypeStruct((T, LANES), F32),
                   jax.ShapeDtypeStruct((D, T), BF16)],
        scratch=[pltpu.VMEM((tm, D), BF16)], dims=("parallel", "arbitrary"),
        operands=(x2, nw, wmain, wsmall, cos128, sin128))


def _inproj_bwd_dx(segs, wmain, init, final, name, comm=None):
    T = segs[0][0].shape[0]
    tm = min(1024, T)
    tk = 1024 if all(a.shape[1] % 1024 == 0 and c % 1024 == 0 for a, c in segs) else 512
    ni = T // tm
    k0s, nks, c0s = [], [], []
    for arr, col0 in segs:
        k0s.append(sum(nks))
        nks.append(arr.shape[1] // tk)
        c0s.append(col0 // tk)
    nk = sum(nks)
    ns = len(segs)

    def in_range(k, s):
        return jnp.logical_and(k >= k0s[s], k < k0s[s] + nks[s])

    def wcol(i, k):
        g = 0
        for s in range(ns):
            g = g + jnp.where(in_range(k, s), c0s[s] + k - k0s[s], 0)
        return (0, g)

    n_init = 2 if init[0] == "narrow" else 1

    def body(*refs):
        seg_refs, w_ref = refs[:ns], refs[ns]
        init_refs = refs[ns + 1:ns + 1 + n_init]
        rest = refs[ns + 1 + n_init:]
        i, k = pl.program_id(0), pl.program_id(1)
        acc = rest[-1]

        @pl.when(k == 0)
        def _():
            if init[0] == "narrow":
                acc[...] = _dot_nt(init_refs[0][...], init_refs[1][...])
            else:
                acc[...] = init_refs[0][...]

        for s in range(ns):
            @pl.when(in_range(k, s))
            def _(s=s):
                acc[...] += _dot_nt(seg_refs[s][...], w_ref[...])

        if final is None:
            @pl.when(k == nk - 1)
            def _():
                rest[0][...] = acc[...]
        else:
            x_ref, nw_ref, dxo_ref, dx_ref, dx16_ref, dnw_ref = rest[:6]

            @pl.when(jnp.logical_and(i == 0, k == 0))
            def _():
                dnw_ref[...] = jnp.zeros_like(dnw_ref)

            @pl.when(k == nk - 1)
            def _():
                x = x_ref[...]
                r = lax.rsqrt(jnp.mean(x * x, axis=-1, keepdims=True) + EPS)
                dh = acc[...]
                g = dh * nw_ref[...]
                dx = dxo_ref[...] + r * g - x * (r * r * r) * jnp.mean(g * x, axis=-1, keepdims=True)
                dx_ref[...] = dx
                dx16_ref[...] = dx.astype(BF16)
                dnw_ref[0:1, :] += jnp.sum(dh * x * r, axis=0, keepdims=True)

    row = pl.BlockSpec((tm, D), lambda i, k: (i, 0))
    in_specs = [pl.BlockSpec((tm, tk), lambda i, k, s=s: (i, jnp.clip(k - k0s[s], 0, nks[s] - 1))) for s in range(ns)]
    in_specs.append(pl.BlockSpec((D, tk), wcol))
    operands = [a for a, _ in segs] + [wmain]
    if init[0] == "narrow":
        in_specs += [pl.BlockSpec((tm, LANES), lambda i, k: (i, 0)), pl.BlockSpec((D, LANES), lambda i, k: (0, 0))]
    else:
        in_specs.append(row)
    operands += list(init[1:])
    if final is None:
        out_specs, out_shape = [row], [jax.ShapeDtypeStruct((T, D), F32)]
    else:
        in_specs += [row, pl.BlockSpec((1, D), lambda i, k: (0, 0)), row]
        operands += list(final)
        out_specs = [row, row, pl.BlockSpec((8, D), lambda i, k: (0, 0))]
        out_shape = [jax.ShapeDtypeStruct((T, D), F32), jax.ShapeDtypeStruct((T, D), BF16),
                     jax.ShapeDtypeStruct((8, D), F32)]
    return _hosted_call(body, comm, name, (ni, nk), in_specs=in_specs, out_specs=out_specs, out_shape=out_shape,
                        scratch=[pltpu.VMEM((tm, D), F32)], dims=("arbitrary", "arbitrary"), operands=tuple(operands))


def _merge_fwd(ya, yb, yc, proj, gbias, wp, wout, x2, li):
    T = x2.shape[0]
    tm = min(512, T)
    gcol = OFF_G // D

    def body(ya_ref, yb_ref, yc_ref, g0_ref, g1_ref, g2_ref, gb_ref, wp_ref, wo_ref, x_ref, xn_ref, br_ref, yt_ref):
        merged = jnp.zeros((tm, D), F32)
        for i, (y_ref, g_ref) in enumerate(((ya_ref, g0_ref), (yb_ref, g1_ref), (yc_ref, g2_ref))):
            y = y_ref[...]
            yt_ref[i] = y.T
            br = _dot(y, wp_ref[i])
            br_ref[i] = br.astype(BF16)
            gate = _sigmoid(g_ref[...].astype(F32) + gb_ref[i:i + 1, :])
            merged = merged + gate * br
        xn_ref[...] = x_ref[...] + _dot(merged.astype(BF16), wo_ref[...])

    row = lambda c: pl.BlockSpec((tm, D), lambda i, c=c: (i, c))
    return pl.pallas_call(
        body, name=f"merge_fwd_{li}", grid=(T // tm,),
        in_specs=[row(0), row(0), row(0), row(gcol), row(gcol + 1), row(gcol + 2),
                  pl.BlockSpec((3, D), lambda i: (0, 0)),
                  pl.BlockSpec((3, D, D), lambda i: (0, 0, 0)),
                  pl.BlockSpec((D, D), lambda i: (0, 0)),
                  row(0)],
        out_specs=[row(0), pl.BlockSpec((3, tm, D), lambda i: (0, i, 0)), pl.BlockSpec((3, D, tm), lambda i: (0, 0, i))],
        out_shape=[jax.ShapeDtypeStruct((T, D), F32), jax.ShapeDtypeStruct((3, T, D), BF16),
                   jax.ShapeDtypeStruct((3, D, T), BF16)],
        compiler_params=_cparams(("parallel",), VMEM_LIMIT),
    )(ya, yb, yc, proj, proj, proj, gbias, wp, wout, x2)


def _merge_bwd(dxo16, wout, wp, br, proj, gbias, ob, oc, li):
    T = dxo16.shape[0]
    tm = min(256, T)
    gcol = OFF_G // D

    def body(dx_ref, wo_ref, wp_ref, br_ref, g0_ref, g1_ref, g2_ref, gb_ref, ob_ref, oc_ref, zb_ref, zc_ref,
             dbr_ref, dg_ref, mt_ref, dgb_ref, dya_ref, dob_ref, dzb_ref, doc_ref, dzc_ref):
        @pl.when(pl.program_id(0) == 0)
        def _():
            dgb_ref[...] = jnp.zeros_like(dgb_ref)

        dm = _dot_nt(dx_ref[...], wo_ref[...])
        merged = jnp.zeros((tm, D), F32)
        dys = []
        for i, g_ref in enumerate((g0_ref, g1_ref, g2_ref)):
            b = br_ref[i].astype(F32)
            gate = _sigmoid(g_ref[...].astype(F32) + gb_ref[i:i + 1, :])
            merged = merged + gate * b
            dbr = (dm * gate).astype(BF16)
            dbr_ref[i] = dbr
            dgate = dm * b * gate * (1.0 - gate)
            dg_ref[:, D * i:D * (i + 1)] = dgate.astype(BF16)
            dgb_ref[i:i + 1, :] += jnp.sum(dgate, axis=0, keepdims=True)
            dys.append(_dot_nt(dbr, wp_ref[i]))
        mt_ref[...] = merged.astype(BF16).T
        dya_ref[...] = dys[0].astype(BF16)
        for dy, o_ref, z_ref, do_ref, dz_ref in ((dys[1], ob_ref, zb_ref, dob_ref, dzb_ref),
                                                 (dys[2], oc_ref, zc_ref, doc_ref, dzc_ref)):
            z = z_ref[...].astype(F32)
            sg = _sigmoid(z)
            do_ref[...] = (dy * z * sg).astype(BF16)
            dz_ref[...] = (dy * o_ref[...].astype(F32) * sg * (1.0 + z * (1.0 - sg))).astype(BF16)

    row = lambda c: pl.BlockSpec((tm, D), lambda i, c=c: (i, c))
    sds = jax.ShapeDtypeStruct((T, D), BF16)
    return pl.pallas_call(
        body, name=f"merge_bwd_{li}", grid=(T // tm,),
        in_specs=[row(0), pl.BlockSpec((D, D), lambda i: (0, 0)), pl.BlockSpec((3, D, D), lambda i: (0, 0, 0)),
                  pl.BlockSpec((3, tm, D), lambda i: (0, i, 0)),
                  row(gcol), row(gcol + 1), row(gcol + 2),
                  pl.BlockSpec((3, D), lambda i: (0, 0)),
                  row(0), row(0), row(OFF_BZ // D), row(OFF_CZ // D)],
        out_specs=[pl.BlockSpec((3, tm, D), lambda i: (0, i, 0)),
                   pl.BlockSpec((tm, 3 * D), lambda i: (i, 0)),
                   pl.BlockSpec((D, tm), lambda i: (0, i)),
                   pl.BlockSpec((8, D), lambda i: (0, 0)),
                   row(0), row(0), row(0), row(0), row(0)],
        out_shape=[jax.ShapeDtypeStruct((3, T, D), BF16), jax.ShapeDtypeStruct((T, 3 * D), BF16),
                   jax.ShapeDtypeStruct((D, T), BF16), jax.ShapeDtypeStruct((8, D), F32), sds, sds, sds, sds, sds],
        compiler_params=_cparams(("arbitrary",), VMEM_LIMIT),
    )(dxo16, wout, wp, br, proj, proj, proj, gbias, ob, oc, proj, proj)


def _final_loss(x2, tgt, fw):
    T = x2.shape[0]
    tm = min(512, T)
    ni = T // tm

    def body(x_ref, t_ref, w_ref, dx_ref, dx16_ref, st_ref):
        i = pl.program_id(0)

        @pl.when(i == 0)
        def _():
            st_ref[...] = jnp.zeros_like(st_ref)

        x = x_ref[...]
        r = lax.rsqrt(jnp.mean(x * x, axis=-1, keepdims=True) + EPS)
        xh = x * r
        err = xh * w_ref[...] - t_ref[...]
        dy = err * (1.0 / D)
        g = dy * w_ref[...]
        dx = r * g - x * (r * r * r) * jnp.mean(g * x, axis=-1, keepdims=True)
        dx_ref[...] = dx
        dx16_ref[...] = dx.astype(BF16)
        st_ref[0:1, :] += jnp.sum(dy * xh, axis=0, keepdims=True)
        st_ref[1:2, :] += jnp.sum(err * err, axis=0, keepdims=True)

        @pl.when(i == ni - 1)
        def _():
            tot = jnp.sum(st_ref[1:2, :], axis=1, keepdims=True) * (0.5 / D)
            st_ref[2:3, :] = jnp.broadcast_to(tot, (1, D))

    row = pl.BlockSpec((tm, D), lambda i: (i, 0))
    return pl.pallas_call(
        body, name="final_loss", grid=(ni,),
        in_specs=[row, row, pl.BlockSpec((1, D), lambda i: (0, 0))],
        out_specs=[row, row, pl.BlockSpec((8, D), lambda i: (0, 0))],
        out_shape=[jax.ShapeDtypeStruct((T, D), F32), jax.ShapeDtypeStruct((T, D), BF16),
                   jax.ShapeDtypeStruct((8, D), F32)],
        compiler_params=_cparams(("arbitrary",), VMEM_LIMIT),
    )(x2, tgt, fw)


def _fox_cum(ps, fb_row, S, li):
    T = ps.shape[0]
    blk = min(4 * LCH, S)
    nb, nsub = S // blk, blk // LCH

    def body(ps_ref, fb_ref, cum_ref, carry):
        @pl.when(pl.program_id(1) == 0)
        def _():
            carry[...] = jnp.zeros_like(carry)

        r = lax.broadcasted_iota(jnp.int32, (LCH, LCH), 0)
        c = lax.broadcasted_iota(jnp.int32, (LCH, LCH), 1)
        tri = (r >= c).astype(F32)
        run = carry[0:1, :]
        for u in range(nsub):
            rows = slice(LCH * u, LCH * (u + 1))
            logf = -_softplus(-(ps_ref[rows, :] + fb_ref[...]))
            cum = _dot_hi(tri, logf) + run
            cum_ref[rows, :] = cum
            run = cum[LCH - 1:LCH, :]
        carry[0:1, :] = run

    return pl.pallas_call(
        body, name=f"fox_cum_{li}", grid=(T // S, nb),
        in_specs=[pl.BlockSpec((blk, LANES), lambda b, i: (b * nb + i, 0)),
                  pl.BlockSpec((1, LANES), lambda b, i: (0, 0))],
        out_specs=pl.BlockSpec((blk, LANES), lambda b, i: (b * nb + i, 0)),
        out_shape=jax.ShapeDtypeStruct((T, LANES), F32),
        scratch_shapes=[pltpu.VMEM((8, LANES), F32)],
        compiler_params=_cparams(("arbitrary", "arbitrary")),
    )(ps, fb_row)


def _fox_cum_bwd(dcum, ps, fb_row, S, li):
    T = ps.shape[0]
    rows_blk = min(4 * LCH, S)
    nb, nsub = S // rows_blk, rows_blk // LCH

    def body(dc_ref, ps_ref, fb_ref, df_ref, dfb_ref, carry):
        b, i = pl.program_id(0), pl.program_id(1)

        @pl.when(i == 0)
        def _():
            carry[...] = jnp.zeros_like(carry)

        @pl.when(jnp.logical_and(b == 0, i == 0))
        def _():
            dfb_ref[...] = jnp.zeros_like(dfb_ref)

        r = lax.broadcasted_iota(jnp.int32, (LCH, LCH), 0)
        c = lax.broadcasted_iota(jnp.int32, (LCH, LCH), 1)
        tri = (c >= r).astype(F32)
        lane = _lane_iota()
        live = jnp.logical_and(lane >= NH, lane < 2 * NH)
        run = carry[0:1, :]
        dfb = jnp.zeros((1, LANES), F32)
        for u in reversed(range(nsub)):
            rows = slice(LCH * u, LCH * (u + 1))
            dc = dc_ref[rows, :]
            dlogf = _dot_hi(tri, dc) + run
            run = run + jnp.sum(dc, axis=0, keepdims=True)
            df = jnp.where(live, dlogf * _sigmoid(-(ps_ref[rows, :] + fb_ref[...])), 0.0)
            df_ref[rows, :] = df
            dfb = dfb + jnp.sum(df, axis=0, keepdims=True)
        carry[0:1, :] = run
        dfb_ref[0:1, :] += dfb

    blk = pl.BlockSpec((rows_blk, LANES), lambda b, i: (b * nb + nb - 1 - i, 0))
    return pl.pallas_call(
        body, name=f"fox_cum_bwd_{li}", grid=(T // S, nb),
        in_specs=[blk, blk, pl.BlockSpec((1, LANES), lambda b, i: (0, 0))],
        out_specs=[blk, pl.BlockSpec((8, LANES), lambda b, i: (0, 0))],
        out_shape=[jax.ShapeDtypeStruct((T, LANES), F32), jax.ShapeDtypeStruct((8, LANES), F32)],
        scratch_shapes=[pltpu.VMEM((8, LANES), F32)],
        compiler_params=_cparams(("arbitrary", "arbitrary")),
    )(dcum, ps, fb_row)


def _fox_blocks(S):
    bq = min(512, S)
    return bq, S // bq


def _split3(c):
    hi = c.astype(BF16).astype(F32)
    r = c - hi
    mid = r.astype(BF16).astype(F32)
    return hi, mid, (r - mid).astype(BF16).astype(F32)


def _augment(x, bias_row, key_side, hh):
    n = x.shape[0]
    b0 = HD if hh == 0 else 0
    hi, mid, lo = _split3(bias_row)
    one = jnp.ones_like(bias_row)
    six = (one, one, one, hi, mid, lo) if key_side else (hi, mid, lo, one, one, one)
    sub = lax.broadcasted_iota(jnp.int32, (LANES, 1), 0)
    a = jnp.zeros((LANES, n), F32)
    for t, r in enumerate(six):
        a = jnp.where(sub == b0 + t, r, a)
    lane = _lane_iota()
    return jnp.where(jnp.logical_and(lane >= b0, lane < b0 + 6), a.T, x).astype(BF16)


def _col_to_row(col):
    return jnp.broadcast_to(col, (col.shape[0], LANES)).T[0:1, :]


def _fox_fwd(proj, cum_row, S, li, comm=None):
    T = proj.shape[0]
    B = T // S
    bq, nq = _fox_blocks(S)
    qc, kc, vc, zc = OFF_CQ // LANES, OFF_CK // LANES, OFF_CV // LANES, OFF_CZ // LANES

    def body(q_ref, k_ref, v_ref, z_ref, cr_ref, y_ref, o_ref, lse_ref, kaug, vaug):
        i = pl.program_id(2)
        m0 = _lane_iota() < HD

        @pl.when(i == 0)
        def _():
            vf = v_ref[...]
            for hh in range(2):
                for t in range(nq):
                    rows = slice(bq * t, bq * (t + 1))
                    kaug[hh, rows, :] = _augment(k_ref[rows, :].astype(F32), -cr_ref[0, hh, t], True, hh)
                vaug[hh] = jnp.where(m0 if hh == 0 else jnp.logical_not(m0), vf, jnp.ones_like(vf))

        q2 = q_ref[...].astype(F32) * SCALE
        row = lax.broadcasted_iota(jnp.int32, (bq, bq), 0)
        col = lax.broadcasted_iota(jnp.int32, (bq, bq), 1)
        qa = [_augment(jnp.where(m0 if hh == 0 else jnp.logical_not(m0), q2, 0.0), cr_ref[0, hh, i], False, hh)
              for hh in range(2)]

        def step(j, carry, masked):
            start = pl.multiple_of(j * bq, bq)
            out = []
            for hh in range(2):
                m, acc = carry[2 * hh:2 * hh + 2]
                s = _dot_nt(qa[hh], kaug[hh, pl.ds(start, bq), :])
                if masked:
                    s = jnp.where(row >= col, s, NEG)
                mn = jnp.maximum(m, jnp.max(s, axis=1, keepdims=True))
                p = jnp.exp(s - mn)
                out += [mn, jnp.exp(m - mn) * acc + _dot(p.astype(BF16), vaug[hh, pl.ds(start, bq), :])]
            return tuple(out)

        init = (jnp.full((bq, 1), NEG, F32), jnp.zeros((bq, LANES), F32)) * 2
        carry = step(i, lax.fori_loop(0, i, functools.partial(step, masked=False), init), True)
        outs = []
        for hh in range(2):
            m, acc = carry[2 * hh:2 * hh + 2]
            other = HD if hh == 0 else 0
            l = acc[:, other:other + 1]
            outs.append(acc / l)
            lse_ref[0, hh, 0] = _col_to_row(m + jnp.log(l))
        o2 = jnp.where(m0, outs[0], outs[1])
        z = z_ref[...].astype(F32)
        o_ref[...] = o2.astype(BF16)
        y_ref[...] = (o2 * z * _sigmoid(z)).astype(BF16)

    qblk = lambda c: pl.BlockSpec((bq, LANES), lambda b, p, i, c=c: (b * nq + i, c + p))
    sblk = lambda c: pl.BlockSpec((S, LANES), lambda b, p, i, c=c: (b, c + p))
    return _hosted_call(
        body, comm, f"fox_fwd_{li}", (B, NH // 2, nq),
        in_specs=[qblk(qc), sblk(kc), sblk(vc), qblk(zc),
                  pl.BlockSpec((1, 2, nq, 1, bq), lambda b, p, i: (b, p, 0, 0, 0))],
        out_specs=[qblk(0), qblk(0), pl.BlockSpec((1, 2, 1, 1, bq), lambda b, p, i: (b, p, i, 0, 0))],
        out_shape=[jax.ShapeDtypeStruct((T, D), BF16), jax.ShapeDtypeStruct((T, D), BF16),
                   jax.ShapeDtypeStruct((B, NH, nq, 1, bq), F32)],
        scratch=[pltpu.VMEM((2, S, LANES), BF16), pltpu.VMEM((2, S, LANES), BF16)],
        dims=("parallel", "parallel", "arbitrary"), operands=(proj, proj, proj, proj, cum_row))


def _fox_bwd(proj, do, o, cum_row, lse, S, li, comm=None):
    T = proj.shape[0]
    B = T // S
    bq, nq = _fox_blocks(S)
    qc, kc, vc = OFF_CQ // LANES, OFF_CK // LANES, OFF_CV // LANES

    def body(q_ref, k_ref, v_ref, do_ref, o_ref, cr_ref, lse_ref, dq_ref, dk_ref, dv_ref, dc_ref, dr_ref,
             dq_scr, dr_scr, qaug):
        j = pl.program_id(2)
        m0 = _lane_iota() < HD

        @pl.when(j == 0)
        def _():
            dq_scr[...] = jnp.zeros_like(dq_scr)
            dr_scr[...] = jnp.zeros_like(dr_scr)
            for t in range(nq):
                rows = slice(bq * t, bq * (t + 1))
                qf = q_ref[rows, :].astype(F32) * SCALE
                for hh in range(2):
                    sel = m0 if hh == 0 else jnp.logical_not(m0)
                    qaug[hh, rows, :] = _augment(jnp.where(sel, qf, 0.0), cr_ref[0, hh, t] - lse_ref[0, hh, t],
                                                 False, hh)

        k2 = k_ref[...]
        v2 = v_ref[...]
        zk = jnp.zeros_like(k2)
        kh = (jnp.where(m0, k2, zk), jnp.where(m0, zk, k2))
        kf = k2.astype(F32)
        ka = [_augment(kf, -cr_ref[0, hh, j], True, hh) for hh in range(2)]
        row = lax.broadcasted_iota(jnp.int32, (bq, bq), 0)
        col = lax.broadcasted_iota(jnp.int32, (bq, bq), 1)

        def step(i, carry, masked):
            dk, dv, dc0, dc1 = carry
            dcs = [dc0, dc1]
            start = pl.multiple_of(i * bq, bq)
            q2 = q_ref[pl.ds(start, bq), :]
            do2 = do_ref[pl.ds(start, bq), :]
            prod = do2.astype(F32) * o_ref[pl.ds(start, bq), :].astype(F32)
            zq = jnp.zeros_like(q2)
            dq = jnp.zeros((bq, LANES), F32)
            for hh in range(2):
                sel = m0 if hh == 0 else jnp.logical_not(m0)
                qh = jnp.where(sel, q2, zq)
                doh = jnp.where(sel, do2, zq)
                delta = _head_sum(prod, hh)
                s = _dot_nt(qaug[hh, pl.ds(start, bq), :], ka[hh])
                if masked:
                    s = jnp.where(row >= col, s, NEG)
                p = jnp.exp(s)
                dp = _dot_nt(doh, v2)
                ds = p * (dp - delta)
                dcs[hh] = dcs[hh] - jnp.sum(ds, axis=0, keepdims=True)
                dr_scr[hh, pl.ds(start, bq), :] += jnp.sum(ds, axis=1, keepdims=True)
                dsb = ds.astype(BF16)
                dv = dv + _dot_tn(p.astype(BF16), doh)
                dk = dk + _dot_tn(dsb, qh)
                dq = dq + _dot(dsb, kh[hh])
            dq_scr[pl.ds(start, bq), :] += dq
            return dk, dv, dcs[0], dcs[1]

        zero = jnp.zeros((bq, LANES), F32)
        zrow = jnp.zeros((1, bq), F32)
        carry = step(j, (zero, zero, zrow, zrow), True)
        dk, dv, dc0, dc1 = lax.fori_loop(j + 1, nq, functools.partial(step, masked=False), carry)
        dk_ref[...] = (dk * SCALE).astype(BF16)
        dv_ref[...] = dv.astype(BF16)
        dc_ref[0, 0, 0] = dc0
        dc_ref[0, 1, 0] = dc1

        @pl.when(j == nq - 1)
        def _():
            dq_ref[...] = (dq_scr[...] * SCALE).astype(BF16)
            step_r = min(4 * LANES, S)
            for hh in range(2):
                for t in range(S // step_r):
                    dr_ref[0, hh, :, step_r * t:step_r * (t + 1)] = _col_to_row(dr_scr[hh, step_r * t:step_r * (t + 1), :])

    sblk = lambda c: pl.BlockSpec((S, LANES), lambda b, p, j, c=c: (b, c + p))
    kblk = lambda c: pl.BlockSpec((bq, LANES), lambda b, p, j, c=c: (b * nq + j, c + p))
    rows_spec = pl.BlockSpec((1, 2, nq, 1, bq), lambda b, p, j: (b, p, 0, 0, 0))
    row_spec = pl.BlockSpec((1, 2, 1, S), lambda b, p, j: (b, p, 0, 0))
    return _hosted_call(
        body, comm, f"fox_bwd_{li}", (B, NH // 2, nq),
        in_specs=[sblk(qc), kblk(kc), kblk(vc), sblk(0), sblk(0), rows_spec, rows_spec],
        out_specs=[sblk(0), kblk(0), kblk(0), pl.BlockSpec((1, 2, 1, 1, bq), lambda b, p, j: (b, p, j, 0, 0)),
                   row_spec],
        out_shape=[jax.ShapeDtypeStruct((T, D), BF16), jax.ShapeDtypeStruct((T, D), BF16),
                   jax.ShapeDtypeStruct((T, D), BF16), jax.ShapeDtypeStruct((B, NH, nq, 1, bq), F32),
                   jax.ShapeDtypeStruct((B, NH, 1, S), F32)],
        scratch=[pltpu.VMEM((S, LANES), F32), pltpu.VMEM((2, S, 1), F32), pltpu.VMEM((2, S, LANES), BF16)],
        dims=("parallel", "parallel", "arbitrary"), operands=(proj, proj, proj, do, o, cum_row, lse))


def _swa_blocks(S):
    bq = min(512, S)
    return bq, S // bq, bq // LCH


def _dup_head(xw, kvl):
    m0 = _lane_iota() < HD
    a = jnp.where(m0 if kvl == 0 else jnp.logical_not(m0), xw, 0.0)
    return (a + pltpu.roll(a, HD, 1)).astype(BF16)


def _band(same_block):
    r = lax.broadcasted_iota(jnp.int32, (LCH, LCH), 0)
    c = lax.broadcasted_iota(jnp.int32, (LCH, LCH), 1)
    return (c <= r) if same_block else (c > r)


def _stack_heads(ref, rows, kvl):
    m0 = _lane_iota() < HD
    parts = []
    for ch in (2 * kvl, 2 * kvl + 1):
        x = ref[rows, LANES * ch:LANES * (ch + 1)]
        parts += [jnp.where(m0, x, jnp.zeros_like(x)), jnp.where(m0, jnp.zeros_like(x), x)]
    return jnp.concatenate(parts, axis=0)


def _stack_delta(do_ref, o_ref, rows, kvl, scale=None):
    parts = []
    for ch in (2 * kvl, 2 * kvl + 1):
        lanes = slice(LANES * ch, LANES * (ch + 1))
        prod = do_ref[rows, lanes].astype(F32) * o_ref[rows, lanes].astype(F32)
        parts += [_head_sum(prod, 0), _head_sum(prod, 1)]
    out = jnp.concatenate(parts, axis=0)
    return out if scale is None else out * scale


def _stack_cols(ref, rows, kvl):
    return jnp.concatenate([ref[0, 4 * kvl + t, rows, :] for t in range(4)], axis=0)


def _swa_fwd(proj, sinks, S, li):
    T = proj.shape[0]
    B = T // S
    bq, nq, nsub = _swa_blocks(S)
    nrow = S // LCH
    qc, zc, kc, vc = OFF_BQ // 512, OFF_BZ // 512, OFF_BK // LANES, OFF_BV // LANES

    def body(sk_ref, q_ref, z_ref, kp_ref, kc_ref, vp_ref, vc_ref, y_ref, o_ref, lse_ref):
        c, i = pl.program_id(0), pl.program_id(2)
        m0 = _lane_iota() < HD
        kw = jnp.concatenate([kp_ref[...].astype(F32), kc_ref[...].astype(F32)], axis=0)
        vw = jnp.concatenate([vp_ref[...].astype(F32), vc_ref[...].astype(F32)], axis=0)
        kd = (_dup_head(kw, 0), _dup_head(kw, 1))
        vd = (_dup_head(vw, 0), _dup_head(vw, 1))
        valid = jnp.concatenate([_band(False), _band(True)], axis=1)
        col = lax.broadcasted_iota(jnp.int32, (LCH, 2 * LCH), 1)
        valid_first = jnp.logical_and(valid, jnp.logical_or(col >= LCH, i > 0))
        for r in range(nsub):
            rows = slice(LCH * r, LCH * (r + 1))
            msk = valid_first if r == 0 else valid
            for ch in range(4):
                kvl = ch // 2
                kwin = kd[kvl][LCH * r:LCH * (r + 2)]
                vwin = vd[kvl][LCH * r:LCH * (r + 2)]
                lanes = slice(LANES * ch, LANES * (ch + 1))
                q2 = q_ref[rows, lanes]
                outs = []
                for hh in range(2):
                    hl = 2 * ch + hh
                    qh = jnp.where(m0 if hh == 0 else jnp.logical_not(m0), q2, jnp.zeros_like(q2))
                    s = jnp.where(msk, _dot_nt(qh, kwin) * SCALE, NEG)
                    sink = sk_ref[8 * c + hl]
                    m = jnp.maximum(jnp.max(s, axis=1, keepdims=True), sink)
                    p = jnp.exp(s - m)
                    l = jnp.sum(p, axis=1, keepdims=True) + jnp.exp(sink - m)
                    outs.append(_dot(p.astype(BF16), vwin) / l)
                    lse_ref[0, hl, rows, :] = m + jnp.log(l)
                o2 = jnp.where(m0, outs[0], outs[1])
                z = z_ref[rows, lanes].astype(F32)
                o_ref[rows, lanes] = o2.astype(BF16)
                y_ref[rows, lanes] = (o2 * z * _sigmoid(z)).astype(BF16)

    wide = lambda cc: pl.BlockSpec((bq, 512), lambda c, b, i, cc=cc: (b * nq + i, cc + c))
    cur = lambda cc: pl.BlockSpec((bq, LANES), lambda c, b, i, cc=cc: (b * nq + i, cc + c))
    prev = lambda cc: pl.BlockSpec((LCH, LANES), lambda c, b, i, cc=cc: (b * nrow + jnp.maximum(i * nsub - 1, 0), cc + c))
    return pl.pallas_call(
        body, name=f"swa_fwd_{li}", grid=(2, B, nq),
        in_specs=[pl.BlockSpec(memory_space=pltpu.SMEM), wide(qc), wide(zc), prev(kc), cur(kc), prev(vc), cur(vc)],
        out_specs=[wide(0), wide(0), pl.BlockSpec((1, 8, bq, 1), lambda c, b, i: (b, c, i, 0))],
        out_shape=[jax.ShapeDtypeStruct((T, D), BF16), jax.ShapeDtypeStruct((T, D), BF16),
                   jax.ShapeDtypeStruct((B, NH, S, 1), F32)],
        compiler_params=_cparams(("parallel", "parallel", "parallel"), VMEM_LIMIT),
    )(sinks, proj, proj, proj, proj, proj, proj)


def _swa_bwd_dq(proj, do, o, lse, sinks, cos128, sin128, S, li):
    T = proj.shape[0]
    B = T // S
    bq, nq, nsub = _swa_blocks(S)
    nrow = S // LCH
    qc, kc, vc = OFF_BQ // 512, OFF_BK // LANES, OFF_BV // LANES

    def body(sk_ref, q_ref, do_ref, o_ref, lse_ref, kp_ref, kc_ref, vp_ref, vc_ref, cos_ref, sin_ref, dq_ref, dsk_ref):
        c, b, i = pl.program_id(0), pl.program_id(1), pl.program_id(2)

        @pl.when(jnp.logical_and(b == 0, i == 0))
        def _():
            dsk_ref[...] = jnp.zeros_like(dsk_ref)

        m0 = _lane_iota() < HD
        kw = jnp.concatenate([kp_ref[...].astype(F32), kc_ref[...].astype(F32)], axis=0)
        vw = jnp.concatenate([vp_ref[...].astype(F32), vc_ref[...].astype(F32)], axis=0)
        kd = (_dup_head(kw, 0), _dup_head(kw, 1))
        vd = (_dup_head(vw, 0), _dup_head(vw, 1))
        valid = jnp.concatenate([_band(False), _band(True)], axis=1)
        col = lax.broadcasted_iota(jnp.int32, (LCH, 2 * LCH), 1)
        valid_first = jnp.logical_and(valid, jnp.logical_or(col >= LCH, i > 0))
        dsk = [jnp.zeros((1, 1), F32) for _ in range(8)]
        valid4 = jnp.concatenate([valid] * 4, axis=0)
        valid4_first = jnp.concatenate([valid_first] * 4, axis=0)
        for r in range(nsub):
            rows = slice(LCH * r, LCH * (r + 1))
            msk = valid4_first if r == 0 else valid4
            for kvl in range(2):
                kwin = kd[kvl][LCH * r:LCH * (r + 2)]
                vwin = vd[kvl][LCH * r:LCH * (r + 2)]
                qs = _stack_heads(q_ref, rows, kvl)
                dos = _stack_heads(do_ref, rows, kvl)
                delta = _stack_delta(do_ref, o_ref, rows, kvl)
                lse = _stack_cols(lse_ref, rows, kvl)
                sink = jnp.concatenate([jnp.full((LCH, 1), sk_ref[8 * c + 4 * kvl + t], F32) for t in range(4)], axis=0)
                s = jnp.where(msk, _dot_nt(qs, kwin) * SCALE, NEG)
                p = jnp.exp(s - lse)
                ds = p * (_dot_nt(dos, vwin) - delta)
                dqs = _dot(ds.astype(BF16), kwin) * SCALE
                dsink = jnp.exp(sink - lse) * delta
                for t in range(4):
                    hl = 4 * kvl + t
                    dsk[hl] = dsk[hl] - jnp.sum(dsink[LCH * t:LCH * (t + 1)], axis=0, keepdims=True)
                for u in range(2):
                    lanes = slice(LANES * (2 * kvl + u), LANES * (2 * kvl + u + 1))
                    dq2 = jnp.where(m0, dqs[LCH * 2 * u:LCH * (2 * u + 1)], dqs[LCH * (2 * u + 1):LCH * (2 * u + 2)])
                    dq2 = dq2 * cos_ref[rows, :] - _rot_half(dq2) * sin_ref[rows, :]
                    dq_ref[rows, lanes] = dq2.astype(BF16)
        for hl in range(8):
            dsk_ref[0, hl:hl + 1, :] += jnp.broadcast_to(dsk[hl], (1, LANES))

    wide = lambda cc: pl.BlockSpec((bq, 512), lambda c, b, i, cc=cc: (b * nq + i, cc + c))
    cur = lambda cc: pl.BlockSpec((bq, LANES), lambda c, b, i, cc=cc: (b * nq + i, cc + c))
    prev = lambda cc: pl.BlockSpec((LCH, LANES), lambda c, b, i, cc=cc: (b * nrow + jnp.maximum(i * nsub - 1, 0), cc + c))
    pos = pl.BlockSpec((bq, LANES), lambda c, b, i: (i, 0))
    return pl.pallas_call(
        body, name=f"swa_bwd_dq_{li}", grid=(2, B, nq),
        in_specs=[pl.BlockSpec(memory_space=pltpu.SMEM), wide(qc), wide(0), wide(0),
                  pl.BlockSpec((1, 8, bq, 1), lambda c, b, i: (b, c, i, 0)),
                  prev(kc), cur(kc), prev(vc), cur(vc), pos, pos],
        out_specs=[wide(0), pl.BlockSpec((1, 8, LANES), lambda c, b, i: (c, 0, 0))],
        out_shape=[jax.ShapeDtypeStruct((T, D), BF16), jax.ShapeDtypeStruct((2, 8, LANES), F32)],
        compiler_params=_cparams(("arbitrary", "arbitrary", "arbitrary"), VMEM_LIMIT),
    )(sinks, proj, do, o, lse, proj, proj, proj, proj, cos128, sin128)


def _swa_bwd_dkv(proj, do, o, lse, cos128, sin128, S, li):
    T = proj.shape[0]
    B = T // S
    bk, nk, nsub = _swa_blocks(S)
    nrow = S // LCH
    qc, kc, vc = OFF_BQ // 512, OFF_BK // LANES, OFF_BV // LANES

    def body(q_ref, qn_ref, do_ref, don_ref, o_ref, on_ref, lse_ref, lsen_ref, k_ref, v_ref, cos_ref, sin_ref,
             dk_ref, dv_ref):
        j = pl.program_id(2)
        m0 = _lane_iota() < HD
        has_next = (j < nk - 1).astype(F32)
        kf = k_ref[...].astype(F32)
        vf = v_ref[...].astype(F32)
        kd = (_dup_head(kf, 0), _dup_head(kf, 1))
        vd = (_dup_head(vf, 0), _dup_head(vf, 1))
        lane = _lane_iota()

        def stat_rows(lse_r, do_r, o_r, rows, scale):
            a_lse = jnp.zeros((rows, LANES), F32)
            a_del = jnp.zeros((rows, LANES), F32)
            for ch in range(4):
                lanes = slice(LANES * ch, LANES * (ch + 1))
                prod = do_r[:, lanes].astype(F32) * o_r[:, lanes].astype(F32)
                for hh in range(2):
                    h = 2 * ch + hh
                    a_lse = jnp.where(lane == h, lse_r[0, h], a_lse)
                    a_del = jnp.where(lane == h, _head_sum(prod, hh), a_del)
            if scale is not None:
                a_del = a_del * scale
            return a_lse.T, a_del.T

        lse_t, del_t = stat_rows(lse_ref, do_ref, o_ref, bk, None)
        lsen_t, deln_t = stat_rows(lsen_ref, don_ref, on_ref, LCH, has_next)
        r_ = lax.broadcasted_iota(jnp.int32, (LCH, LCH), 0)
        c_ = lax.broadcasted_iota(jnp.int32, (LCH, LCH), 1)
        masks4 = (jnp.concatenate([r_ <= c_] * 4, axis=1), jnp.concatenate([r_ > c_] * 4, axis=1))
        for kr in range(nsub):
            krows = slice(LCH * kr, LCH * (kr + 1))
            dk = jnp.zeros((LCH, LANES), F32)
            dv = jnp.zeros((LCH, LANES), F32)
            for dq_blk in range(2):
                rq = kr + dq_blk
                nxt = rq == nsub
                qrows = slice(0, LCH) if nxt else slice(LCH * rq, LCH * (rq + 1))
                qr, dor = (qn_ref, don_ref) if nxt else (q_ref, do_ref)
                lt, dt_ = (lsen_t, deln_t) if nxt else (lse_t, del_t)
                for kvl in range(2):
                    qs = _stack_heads(qr, qrows, kvl)
                    dos = _stack_heads(dor, qrows, kvl)
                    if nxt:
                        dos = (dos.astype(F32) * has_next).astype(BF16)
                    lse_row = jnp.concatenate([lt[4 * kvl + t:4 * kvl + t + 1, qrows] for t in range(4)], axis=1)
                    del_row = jnp.concatenate([dt_[4 * kvl + t:4 * kvl + t + 1, qrows] for t in range(4)], axis=1)
                    st = jnp.where(masks4[dq_blk], _dot_nt(kd[kvl][krows], qs) * SCALE, NEG)
                    pt = jnp.exp(st - lse_row)
                    dst = pt * (_dot_nt(vd[kvl][krows], dos) - del_row)
                    dvc = _dot(pt.astype(BF16), dos)
                    dkc = _dot(dst.astype(BF16), qs) * SCALE
                    own = m0 if kvl == 0 else jnp.logical_not(m0)
                    dv = dv + jnp.where(own, dvc + pltpu.roll(dvc, HD, 1), 0.0)
                    dk = dk + jnp.where(own, dkc + pltpu.roll(dkc, HD, 1), 0.0)
            dk = dk * cos_ref[krows, :] - _rot_half(dk) * sin_ref[krows, :]
            dk_ref[krows, :] = dk.astype(BF16)
            dv_ref[krows, :] = dv.astype(BF16)

    wide = lambda cc: pl.BlockSpec((bk, 512), lambda c, b, j, cc=cc: (b * nk + j, cc + c))
    nxt = lambda cc: pl.BlockSpec((LCH, 512), lambda c, b, j, cc=cc: (b * nrow + jnp.minimum((j + 1) * nsub, nrow - 1), cc + c))
    cur = lambda cc: pl.BlockSpec((bk, LANES), lambda c, b, j, cc=cc: (b * nk + j, cc + c))
    pos = pl.BlockSpec((bk, LANES), lambda c, b, j: (j, 0))
    return pl.pallas_call(
        body, name=f"swa_bwd_dkv_{li}", grid=(2, B, nk),
        in_specs=[wide(qc), nxt(qc), wide(0), nxt(0), wide(0), nxt(0),
                  pl.BlockSpec((1, 8, bk, 1), lambda c, b, j: (b, c, j, 0)),
                  pl.BlockSpec((1, 8, LCH, 1), lambda c, b, j: (b, c, jnp.minimum((j + 1) * nsub, nrow - 1), 0)),
                  cur(kc), cur(vc), pos, pos],
        out_specs=[cur(0), cur(0)],
        out_shape=[jax.ShapeDtypeStruct((T, 2 * LANES), BF16), jax.ShapeDtypeStruct((T, 2 * LANES), BF16)],
        compiler_params=_cparams(("parallel", "parallel", "parallel"), VMEM_LIMIT),
    )(proj, proj, do, do, o, o, lse, lse, proj, proj, cos128, sin128)


HALO = 16


def _shift_matrices():
    r = lax.broadcasted_iota(jnp.int32, (3 * LCH, LCH + HALO), 0)
    c = lax.broadcasted_iota(jnp.int32, (3 * LCH, LCH + HALO), 1)
    t, d = r % LCH, r // LCH + 1
    return (c == HALO + t - d).astype(BF16), (c == t + d).astype(BF16)


def _ssm_chunk_pre(prev16, cur16, first, sdn_ref, cw_ref, cb_ref, ps, dtb, alog):
    ext16 = jnp.concatenate([jnp.where(first, jnp.zeros_like(prev16), prev16), cur16], axis=0)
    sh = _dot(sdn_ref[...], ext16)
    pre = cb_ref[...] + cw_ref[3:4, :] * cur16.astype(F32)
    for d in range(1, 4):
        pre = pre + cw_ref[3 - d:4 - d, :] * sh[LCH * (d - 1):LCH * d]
    sg = _sigmoid(pre)
    dt = _softplus(ps + dtb)
    a = -jnp.exp(alog)
    r = lax.broadcasted_iota(jnp.int32, (LCH, LCH), 0)
    c = lax.broadcasted_iota(jnp.int32, (LCH, LCH), 1)
    acum = _dot_hi((r >= c).astype(F32), dt * a)
    return pre, sg, dt, a, acum, sh


def _expand_matrix():
    r = lax.broadcasted_iota(jnp.int32, (3 * LANES, D), 0)
    c = lax.broadcasted_iota(jnp.int32, (3 * LANES, D), 1)
    return ((r % LANES) == c // HD).astype(BF16)


def _expand_heads(v, ex_ref):
    return _dot(jnp.concatenate(_split3(v), axis=1).astype(BF16), ex_ref[...])


def _decay(acum, acum_t, h):
    r = lax.broadcasted_iota(jnp.int32, (LCH, LCH), 0)
    c = lax.broadcasted_iota(jnp.int32, (LCH, LCH), 1)
    causal = r >= c
    seg = acum[:, h:h + 1] - acum_t[h:h + 1, :]
    return jnp.where(causal, jnp.exp(jnp.where(causal, seg, 0.0)), 0.0)


def _ssm_pair_fwd(p, x, dt_x, acum, acum_t, e_x, w_x, cd, cb_g, b_g, c_g, hprev, dsk_ref):
    m0 = _lane_iota() < HD
    lanes = slice(LANES * p, LANES * (p + 1))
    x2 = x[:, lanes]
    dt2 = dt_x[:, lanes]
    xdt2 = x2 * dt2
    xdtb = xdt2.astype(BF16)
    lms, ms, yds = [], [], []
    for hh in range(2):
        lm = _decay(acum, acum_t, 2 * p + hh)
        mm = cb_g * lm
        lms.append(lm)
        ms.append(mm)
        yds.append(_dot(mm.astype(BF16), xdtb))
    yd2 = jnp.where(m0, yds[0], yds[1])
    w2 = w_x[:, lanes]
    xw = (xdt2 * w2).astype(BF16)
    s2 = _dot_tn(xw, b_g)
    z2 = _dot_nt(c_g, hprev.astype(BF16))
    e2 = e_x[:, lanes]
    rowsel = lax.broadcasted_iota(jnp.int32, (LANES, 1), 0) < HD
    cdcol = jnp.where(rowsel, cd[:, 2 * p:2 * p + 1], cd[:, 2 * p + 1:2 * p + 2])
    y2 = yd2 + z2 * e2 + dsk_ref[:, lanes] * x2
    return dict(x2=x2, dt2=dt2, xdt2=xdt2, xdtb=xdtb, lms=lms, ms=ms, yd2=yd2, w2=w2, xw=xw, s2=s2, z2=z2, e2=e2,
                cdcol=cdcol, y2=y2)


def _ssm_specs(S, rev):
    nc = S // LCH
    ch = (lambda c: nc - 1 - c) if rev else (lambda c: c)
    prev = pl.BlockSpec((HALO, 2 * D), lambda b, c: (jnp.maximum(b * (S // HALO) + ch(c) * (LCH // HALO) - 1, 0), 0))
    cur = pl.BlockSpec((LCH, 2 * D), lambda b, c: (b * nc + ch(c), 0))
    zed = pl.BlockSpec((LCH, D), lambda b, c: (b * nc + ch(c), OFF_AZ // D))
    row = pl.BlockSpec((LCH, D), lambda b, c: (b * nc + ch(c), 0))
    psb = pl.BlockSpec((LCH, LANES), lambda b, c: (b * nc + ch(c), 0))
    hpb = pl.BlockSpec((1, 1, NH // 2, LANES, NST), lambda b, c: (b, ch(c), 0, 0, 0))
    const = lambda r, w: pl.BlockSpec((r, w), lambda b, c: (0, 0))
    return nc, prev, cur, zed, row, psb, hpb, const


def _ssm_fwd(proj, ps, cw, cb, dtb, alog, dsk, nw, S, li):
    T = proj.shape[0]
    B = T // S
    nc, prev, cur, zed, row, psb, hpb, const = _ssm_specs(S, False)

    def body(prev_ref, cur_ref, z_ref, ps_ref, sdn_ref, ex_ref, cw_ref, cb_ref, dtb_ref, alog_ref, dsk_ref, nw_ref,
             ya_ref, hp_ref, h_scr):
        c = pl.program_id(1)

        @pl.when(c == 0)
        def _():
            h_scr[...] = jnp.zeros_like(h_scr)

        pre, sg, dt, a, acum, _ = _ssm_chunk_pre(prev_ref[...], cur_ref[...], c == 0, sdn_ref, cw_ref, cb_ref,
                                                 ps_ref[...], dtb_ref[...], alog_ref[...])
        act = pre * sg
        acum_t = acum.T
        last = acum[LCH - 1:LCH, :]
        cd = jnp.exp(last)
        dt, e_all, w_all = (_expand_heads(v, ex_ref) for v in (dt, jnp.exp(acum), jnp.exp(last - acum)))
        x = act[:, :D]
        for g in range(NGRP):
            b_g = act[:, D + NST * g:D + NST * (g + 1)].astype(BF16)
            c_g = act[:, D + NGRP * NST + NST * g:D + NGRP * NST + NST * (g + 1)].astype(BF16)
            cb_g = _dot_nt(c_g, b_g)
            ygs = []
            for p in (2 * g, 2 * g + 1):
                hprev = h_scr[p]
                hp_ref[0, 0, p] = hprev
                f = _ssm_pair_fwd(p, x, dt, acum, acum_t, e_all, w_all, cd, cb_g, b_g, c_g, hprev, dsk_ref)
                h_scr[p] = hprev * f["cdcol"] + f["s2"]
                z2 = z_ref[:, LANES * p:LANES * (p + 1)].astype(F32)
                ygs.append(f["y2"] * z2 * _sigmoid(z2))
            yg = jnp.concatenate(ygs, axis=1)
            r = lax.rsqrt(jnp.mean(yg * yg, axis=1, keepdims=True) + EPS)
            ya_ref[:, 2 * LANES * g:2 * LANES * (g + 1)] = (yg * r * nw_ref[:, 2 * LANES * g:2 * LANES * (g + 1)]).astype(BF16)

    return pl.pallas_call(
        body, name=f"ssm_fwd_{li}", grid=(B, nc),
        in_specs=[prev, cur, zed, psb, const(3 * LCH, LCH + HALO), const(3 * LANES, D), const(4, 2 * D),
                  const(1, 2 * D), const(1, LANES), const(1, LANES), const(1, D), const(1, D)],
        out_specs=[row, hpb],
        out_shape=[jax.ShapeDtypeStruct((T, D), BF16), jax.ShapeDtypeStruct((B, nc, NH // 2, LANES, NST), F32)],
        scratch_shapes=[pltpu.VMEM((NH // 2, LANES, NST), F32)],
        compiler_params=_cparams(("arbitrary", "arbitrary"), VMEM_LIMIT),
    )(proj, proj, proj, ps, _shift_matrices()[0], _expand_matrix(), cw, cb, dtb, alog, dsk, nw)


def _ssm_bwd(proj, ps, hp, dya, cw, cb, dtb, alog, dsk, nw, S, li, comm=None):
    T = proj.shape[0]
    B = T // S
    nc, prev, cur, zed, row, psb, hpb, const = _ssm_specs(S, True)

    def body(prev_ref, cur_ref, z_ref, ps_ref, hp_ref, dy_ref, sdn_ref, sup_ref, ex_ref, cw_ref, cb_ref, dtb_ref,
             alog_ref, dsk_ref, nw_ref, dxbc_ref, dz_ref, dps_ref, pgw_ref, pg1_ref, pgh_ref, dh_scr, dhead, dact):
        b, cc = pl.program_id(0), pl.program_id(1)
        c = nc - 1 - cc

        @pl.when(jnp.logical_and(b == 0, cc == 0))
        def _():
            pgw_ref[...] = jnp.zeros_like(pgw_ref)
            pg1_ref[...] = jnp.zeros_like(pg1_ref)
            pgh_ref[...] = jnp.zeros_like(pgh_ref)

        @pl.when(cc == 0)
        def _():
            dh_scr[...] = jnp.zeros_like(dh_scr)
            dhead[...] = jnp.zeros_like(dhead)

        psv = ps_ref[...]
        cur16 = cur_ref[...]
        pre, sg, dt, a, acum, sh = _ssm_chunk_pre(prev_ref[...], cur16, c == 0, sdn_ref, cw_ref, cb_ref, psv,
                                                  dtb_ref[...], alog_ref[...])
        act = pre * sg
        acum_t = acum.T
        last = acum[LCH - 1:LCH, :]
        w_all = jnp.exp(last - acum)
        cd = jnp.exp(last)
        dt_x, e_x, w_x = (_expand_heads(v, ex_ref) for v in (dt, jnp.exp(acum), w_all))
        x = act[:, :D]
        lane = _lane_iota()
        m0 = lane < HD
        head_row = lax.broadcasted_iota(jnp.int32, (LANES, 1), 0)
        rowsel = head_row < HD
        is_last_row = lax.broadcasted_iota(jnp.int32, (LCH, 1), 0) == LCH - 1
        dacum_all = jnp.zeros((LCH, LANES), F32)
        dacum_t = jnp.zeros((LANES, LCH), F32)
        ddt_all = jnp.zeros((LCH, LANES), F32)
        dd_row = jnp.zeros((1, LANES), F32)
        for g in range(NGRP):
            b_g = act[:, D + NST * g:D + NST * (g + 1)].astype(BF16)
            c_g = act[:, D + NGRP * NST + NST * g:D + NGRP * NST + NST * (g + 1)].astype(BF16)
            cb_g = _dot_nt(c_g, b_g)
            pairs = (2 * g, 2 * g + 1)
            fs, hps, zs, ygs = [], [], [], []
            for p in pairs:
                hprev = hp_ref[0, 0, p]
                f = _ssm_pair_fwd(p, x, dt_x, acum, acum_t, e_x, w_x, cd, cb_g, b_g, c_g, hprev, dsk_ref)
                z2 = z_ref[:, LANES * p:LANES * (p + 1)].astype(F32)
                fs.append(f)
                hps.append(hprev)
                zs.append(z2)
                ygs.append(f["y2"] * z2 * _sigmoid(z2))
            gl = slice(2 * LANES * g, 2 * LANES * (g + 1))
            yg = jnp.concatenate(ygs, axis=1)
            r = lax.rsqrt(jnp.mean(yg * yg, axis=1, keepdims=True) + EPS)
            dyn = dy_ref[:, gl].astype(F32)
            gg = dyn * nw_ref[:, gl]
            dyg = r * gg - yg * (r * r * r) * jnp.mean(gg * yg, axis=1, keepdims=True)
            pg1_ref[0:1, gl] += jnp.sum(dyn * yg * r, axis=0, keepdims=True)
            dg_g = jnp.zeros((LCH, LCH), F32)
            db_g = jnp.zeros((LCH, NST), F32)
            dc_g = jnp.zeros((LCH, NST), F32)
            for idx, p in enumerate(pairs):
                f, hprev, z2 = fs[idx], hps[idx], zs[idx]
                lanes = slice(LANES * p, LANES * (p + 1))
                dyg2 = dyg[:, LANES * idx:LANES * (idx + 1)]
                sgz = _sigmoid(z2)
                dy2 = dyg2 * z2 * sgz
                dz_ref[:, lanes] = (dyg2 * f["y2"] * sgz * (1.0 + z2 * (1.0 - sgz))).astype(BF16)
                x2, dt2, xdt2, xdtb, w2, e2, z2m = f["x2"], f["dt2"], f["xdt2"], f["xdtb"], f["w2"], f["e2"], f["z2"]
                dx2 = dsk_ref[:, lanes] * dy2
                dyx = dy2 * x2
                dxdt2 = jnp.zeros((LCH, LANES), F32)
                diag_cols = []
                for hh in range(2):
                    sel = m0 if hh == 0 else jnp.logical_not(m0)
                    dyb = jnp.where(sel, dy2, 0.0).astype(BF16)
                    dm = _dot_nt(dyb, xdtb)
                    dg_g = dg_g + dm * f["lms"][hh]
                    dxdt2 = dxdt2 + _dot_tn(f["ms"][hh].astype(BF16), dyb)
                    em = dm * f["ms"][hh]
                    diag_cols.append(jnp.sum(em, axis=1, keepdims=True))
                    dacum_t = dacum_t - jnp.where(head_row == 2 * p + hh, jnp.sum(em, axis=0, keepdims=True), 0.0)
                dz2m = dy2 * e2
                t_off = dz2m * z2m
                dc_g = dc_g + _dot(dz2m.astype(BF16), hprev.astype(BF16))
                dhprev = _dot_tn(dz2m.astype(BF16), c_g)
                dhn = dh_scr[p]
                dhnb = dhn.astype(BF16)
                dhprev = dhprev + dhn * f["cdcol"]
                t_h = dhn * hprev
                dxw2 = _dot_nt(b_g, dhnb)
                db_g = db_g + _dot(f["xw"], dhnb)
                dxdt2 = dxdt2 + dxw2 * w2
                t_w = dxw2 * xdt2
                dx2 = dx2 + dxdt2 * dt2
                t_dt = dxdt2 * x2
                for hh in range(2):
                    h = 2 * p + hh
                    onehot = (lane == h).astype(F32)
                    w_col = w_all[:, h:h + 1]
                    dw_col = _head_sum(t_w, hh) * w_col
                    rs = rowsel if hh == 0 else jnp.logical_not(rowsel)
                    dlast = (jnp.sum(jnp.where(rs, t_h, 0.0), keepdims=True) * cd[:, h:h + 1]
                             + jnp.sum(dw_col, keepdims=True))
                    dacum_col = diag_cols[hh] + _head_sum(t_off, hh) - dw_col + jnp.where(is_last_row, dlast, 0.0)
                    dacum_all = dacum_all + dacum_col * onehot
                    ddt_all = ddt_all + _head_sum(t_dt, hh) * onehot
                    sel = m0 if hh == 0 else jnp.logical_not(m0)
                    dd_row = dd_row + jnp.sum(jnp.where(sel, dyx, 0.0), keepdims=True) * onehot
                dh_scr[p] = dhprev
                dact[:, lanes] = dx2
            dgb = dg_g.astype(BF16)
            dc_g = dc_g + _dot(dgb, b_g)
            db_g = db_g + _dot_tn(dgb, c_g)
            dact[:, D + NST * g:D + NST * (g + 1)] = db_g
            dact[:, D + NGRP * NST + NST * g:D + NGRP * NST + NST * (g + 1)] = dc_g
        rr = lax.broadcasted_iota(jnp.int32, (LCH, LCH), 0)
        cc2 = lax.broadcasted_iota(jnp.int32, (LCH, LCH), 1)
        dadt = _dot_hi((cc2 >= rr).astype(F32), dacum_all + dacum_t.T)
        ddt_all = ddt_all + dadt * a
        heads = lane < NH
        da = jnp.sum(dadt * dt, axis=0, keepdims=True)
        dr = jnp.where(heads, ddt_all * _sigmoid(psv + dtb_ref[...]), 0.0)
        dps_ref[...] = dr
        pgh_ref[0:1, :] += jnp.sum(dr, axis=0, keepdims=True)
        pgh_ref[1:2, :] += jnp.where(heads, da * a, 0.0)
        pgh_ref[2:3, :] += dd_row
        dpre = dact[...] * sg * (1.0 + pre * (1.0 - sg))
        extd = jnp.concatenate([dpre, dhead[...]], axis=0)
        hi = extd.astype(BF16)
        lo = (extd - hi.astype(F32)).astype(BF16)
        up = _dot(sup_ref[...], hi) + _dot(sup_ref[...], lo)
        du = cw_ref[3:4, :] * dpre
        pgw_ref[3:4, :] += jnp.sum(dpre * cur16.astype(F32), axis=0, keepdims=True)
        for d in range(1, 4):
            du = du + cw_ref[3 - d:4 - d, :] * up[LCH * (d - 1):LCH * d]
            pgw_ref[3 - d:4 - d, :] += jnp.sum(dpre * sh[LCH * (d - 1):LCH * d], axis=0, keepdims=True)
        pgw_ref[4:5, :] += jnp.sum(dpre, axis=0, keepdims=True)
        dxbc_ref[...] = du.astype(BF16)
        dhead[...] = dpre[0:HALO, :]

    xbc_out = pl.BlockSpec((LCH, 2 * D), lambda b, c: (b * nc + nc - 1 - c, 0))
    acc = lambda w: pl.BlockSpec((8, w), lambda b, c: (0, 0))
    sdn, sup = _shift_matrices()
    return _hosted_call(
        body, comm, f"ssm_bwd_{li}", (B, nc),
        in_specs=[prev, cur, zed, psb, hpb, row, const(3 * LCH, LCH + HALO), const(3 * LCH, LCH + HALO),
                  const(3 * LANES, D), const(4, 2 * D), const(1, 2 * D), const(1, LANES), const(1, LANES),
                  const(1, D), const(1, D)],
        out_specs=[xbc_out, row, psb, acc(2 * D), acc(D), acc(LANES)],
        out_shape=[jax.ShapeDtypeStruct((T, 2 * D), BF16), jax.ShapeDtypeStruct((T, D), BF16),
                   jax.ShapeDtypeStruct((T, LANES), F32), jax.ShapeDtypeStruct((8, 2 * D), F32),
                   jax.ShapeDtypeStruct((8, D), F32), jax.ShapeDtypeStruct((8, LANES), F32)],
        scratch=[pltpu.VMEM((NH // 2, LANES, NST), F32), pltpu.VMEM((HALO, 2 * D), F32),
                 pltpu.VMEM((LCH, 2 * D), F32)],
        dims=("arbitrary", "arbitrary"),
        operands=(proj, proj, proj, ps, hp, dya, sdn, sup, _expand_matrix(), cw, cb, dtb, alog, dsk, nw))


def _lane_row(v, offset):
    return jnp.pad(v.astype(F32), (offset, LANES - offset - v.shape[0]))[None]


def _pack_rows(arrays):
    parts = []
    for a in arrays:
        flat = a.reshape(-1).astype(F32)
        pad = (-flat.shape[0]) % LANES
        parts.append(jnp.pad(flat, (0, pad)))
    flat = jnp.concatenate(parts)
    pad = (-flat.shape[0]) % (8 * LANES)
    return jnp.pad(flat, (0, pad)).reshape(-1, LANES)


def _unpack_rows(pack, shapes):
    flat = pack.reshape(-1)
    out, pos = [], 0
    for shp in shapes:
        n = math.prod(shp)
        out.append(flat[pos:pos + n].reshape(shp))
        pos += n + (-n) % LANES
    return out


def _split_w_in(blocks):
    def cols(a, b):
        out = []
        for d in range(NDEV):
            lo, hi = max(a, d * NSH), min(b, (d + 1) * NSH)
            if lo < hi:
                out.append(blocks[d, :, lo - d * NSH:hi - d * NSH])
        return out

    main = jnp.concatenate(cols(0, 3072) + cols(3088, 4112) + cols(4624, 5648) + cols(5648, 8720)
                           + cols(8736, 12832) + cols(4112, 4624), axis=1)
    small = jnp.concatenate(cols(3072, 3088) + cols(8720, 8736) + [jnp.zeros((D, LANES - 2 * NH), blocks.dtype)],
                            axis=1)
    return main, small


def _w_in_blocks(dw, ds, r0, r1):
    xbc, az, bq, bz, cq, ck, cv, cz, gates, bk, bv = dw
    order = [xbc, az, ds[:, 0:NH], bq, bk, bv, bz, cq, ck, cv, ds[:, NH:2 * NH], cz, gates]
    blocks, pos = [[] for _ in range(NDEV)], 0
    for seg in order:
        w = seg.shape[1]
        for d in range(NDEV):
            lo, hi = max(pos, d * NSH), min(pos + w, (d + 1) * NSH)
            if lo < hi:
                blocks[d].append(seg[r0:r1, lo - pos:hi - pos])
        pos += w
    return jnp.stack([jnp.concatenate(b, axis=1) for b in blocks])


def kernel(x, norm_w, w_in, conv_w, conv_b, dt_bias, a_log, d_skip, ssm_norm_w, sinks, f_bias, gate_bias, w_proj, w_out, final_norm_w, loss_target, m_norm_w, m_w_in, m_conv_w, m_conv_b, m_dt_bias, m_a_log, m_d_skip, m_ssm_norm_w, m_sinks, m_f_bias, m_gate_bias, m_w_proj, m_w_out, m_final_norm_w, v_norm_w, v_w_in, v_conv_w, v_conv_b, v_dt_bias, v_a_log, v_d_skip, v_ssm_norm_w, v_sinks, v_f_bias, v_gate_bias, v_w_proj, v_w_out, v_final_norm_w):
    Bl, S, _ = x.shape
    T = Bl * S
    depth = norm_w.shape[0]
    me = 4 * lax.axis_index("x") + 2 * lax.axis_index("y") + lax.axis_index("c")
    csh, gsh = conv_w.shape[2], gate_bias.shape[2]

    def gather_plan(l):
        small = jnp.concatenate([conv_w[l].reshape(-1), gate_bias[l].reshape(-1)]).reshape(-1, LANES)
        return _Comm("gather", [w_in[l].astype(BF16), w_proj[l].astype(BF16), w_out[l].astype(BF16), small])

    def unpack_weights(res):
        g_win, g_wp, g_wo, g_small = res
        flat = g_small.reshape(NDEV, -1)
        return (_split_w_in(g_win),
                g_wp.transpose(1, 0, 2, 3).reshape(3, D, D),
                g_wo.reshape(D, D),
                flat[:, :4 * csh].reshape(NDEV, 4, csh).transpose(1, 0, 2).reshape(4, 2 * D),
                flat[:, 4 * csh:].reshape(NDEV, 3, gsh).transpose(1, 0, 2).reshape(3, D))

    def scatter_plan(gw_in_blocks=None, gw_p=None, gw_o=None):
        arrays = [] if gw_in_blocks is None else [gw_in_blocks]
        if gw_p is not None:
            arrays += [gw_p.astype(BF16).reshape(3, NDEV, D // NDEV, D).transpose(1, 0, 2, 3),
                       gw_o.astype(BF16).reshape(NDEV, D // NDEV, D)]
        return _Comm("scatter", arrays)

    pos = jnp.arange(S, dtype=F32)
    inv_freq = ROPE_THETA ** (-jnp.arange(0, HD, 2, dtype=F32) / HD)
    ang = pos[:, None] * inv_freq[None, :]
    cos128 = jnp.tile(jnp.cos(ang), (1, 4))
    sign = jnp.where((jnp.arange(LANES) % HD) < HD // 2, -1.0, 1.0).astype(F32)
    sin128 = jnp.tile(jnp.sin(ang), (1, 4)) * sign[None, :]

    x2 = x.reshape(T, D)
    tgt2 = loss_target.reshape(T, D)
    fox_bq = _fox_blocks(S)[0]

    saved = []
    xcur = x2
    weights = [None] * depth
    weights[0] = unpack_weights(_gather_two_level(gather_plan(0).arrays, "gather_weights_0"))
    for l in range(depth):
        (wmain, wsmall), wp_l, wo_l, cw_l, gb_l = weights[l]
        proj, ps, h_t = _inproj_fwd(xcur, norm_w[l][None], wmain, wsmall, cos128, sin128, S, l)
        dtb = _lane_row(dt_bias[l], 0)
        alog = _lane_row(a_log[l], 0)
        fb = _lane_row(f_bias[l], NH)
        dsk = jnp.repeat(d_skip[l], HD)[None]
        ya, hp = _ssm_fwd(proj, ps, cw_l, conv_b[l][None], dtb, alog, dsk, ssm_norm_w[l][None], S, l)
        yb, ob, lse_b = _swa_fwd(proj, sinks[l], S, l)
        cum = _fox_cum(ps, fb, S, l)
        cumh = cum[:, NH:2 * NH].reshape(Bl, S, NH).transpose(0, 2, 1)
        cum_row = cumh.reshape(Bl, NH, S // fox_bq, 1, fox_bq)
        comm = gather_plan(l + 1) if l + 1 < depth else None
        res = _fox_fwd(proj, cum_row, S, l, comm)
        yc, oc, lse_c = res[:3]
        if comm is not None:
            weights[l + 1] = unpack_weights(res[3:])
        xnext, br, y_t = _merge_fwd(ya, yb, yc, proj, gb_l, wp_l, wo_l, xcur, l)
        saved.append(dict(x=xcur, wmain=wmain, wsmall=wsmall, proj=proj, ps=ps, h_t=h_t, dtb=dtb, alog=alog, fb=fb,
                          dsk=dsk, hp=hp, ob=ob, lse_b=lse_b, cum_row=cum_row, oc=oc, lse_c=lse_c, br=br, y_t=y_t))
        xcur = xnext

    dx, dx16, st = _final_loss(xcur, tgt2, final_norm_w[None])
    loss_part = st[2, 0]
    g_final = st[0]

    gsm = {k: [None] * depth for k in ("norm_w", "conv_w", "conv_b", "dt_bias", "a_log", "d_skip", "ssm_norm_w",
                                      "sinks", "f_bias", "gate_bias")}
    parts = [None] * depth
    pending = None
    for l in reversed(range(depth)):
        sv = saved[l]
        proj, ps = sv["proj"], sv["ps"]
        _, wp_l, wo_l, cw_l, gb_l = weights[l]
        dbr, dgates, merged_t, dgb, dy_a, do_b, dbz, do_c, dcz = _merge_bwd(dx16, wo_l, wp_l, sv["br"], proj, gb_l,
                                                                            sv["ob"], sv["oc"], l)
        g_wo = _matmul(merged_t, dx16, BF16, f"dwout_{l}")
        g_wp = _matmul_batched(sv["y_t"], dbr, BF16, f"dwproj_{l}")
        gsm["gate_bias"][l] = dgb[0:3]
        hosted = ([] if pending is None else pending.arrays) + (scatter_plan(None, g_wp, g_wo).arrays if l == 0 else [])
        res = _ssm_bwd(proj, ps, sv["hp"], dy_a, cw_l, conv_b[l][None], sv["dtb"], sv["alog"], sv["dsk"],
                       ssm_norm_w[l][None], S, l, _Comm("scatter", hosted) if hosted else None)
        dxbc, daz, dps_a, pgw, pg1, pgh = res[:6]
        if pending is not None:
            parts[l + 1] = res[6:9]
        if l == 0:
            parts_po = res[len(res) - 2:]
        gsm["conv_w"][l], gsm["conv_b"][l] = pgw[0:4], pgw[4]
        gsm["ssm_norm_w"][l] = pg1[0]
        gsm["dt_bias"][l], gsm["a_log"][l], gsm["d_skip"][l] = pgh[0, :NH], pgh[1, :NH], pgh[2, :NH]
        dq_b, dsk_b = _swa_bwd_dq(proj, do_b, sv["ob"], sv["lse_b"], sinks[l], cos128, sin128, S, l)
        dk_b, dv_b = _swa_bwd_dkv(proj, do_b, sv["ob"], sv["lse_b"], cos128, sin128, S, l)
        gsm["sinks"][l] = dsk_b[:, :, 0].reshape(NH)
        dq_c, dk_c, dv_c, dcum_k, dcum_q = _fox_bwd(proj, do_c, sv["oc"], sv["cum_row"], sv["lse_c"], S, l)
        dcum_tm = (dcum_k.reshape(Bl, NH, S) + dcum_q.reshape(Bl, NH, S)).transpose(0, 2, 1).reshape(T, NH)
        dcum_pad = jnp.pad(dcum_tm, ((0, 0), (NH, LANES - 2 * NH)))
        df, dfb = _fox_cum_bwd(dcum_pad, ps, sv["fb"], S, l)
        gsm["f_bias"][l] = dfb[0, NH:2 * NH]
        dps16 = (dps_a + df).astype(BF16)
        dkv_b = jnp.concatenate([dk_b, dv_b], axis=1)
        pieces = (dxbc, daz, dq_b, dbz, dq_c, dk_c, dv_c, dcz, dgates, dkv_b)
        dw_pieces = [_matmul(sv["h_t"], pc, BF16, f"dwin_{l}_{i}") for i, pc in enumerate(pieces)]
        dw_pieces = dw_pieces[:-1] + [dw_pieces[-1][:, :2 * LANES], dw_pieces[-1][:, 2 * LANES:]]
        dws = _matmul(sv["h_t"], dps16, BF16, f"dwin_small_{l}")
        if l == 0:
            plans = [scatter_plan(_w_in_blocks(dw_pieces, dws, r0, r1)) for r0, r1 in ROW_CHUNKS]
        else:
            plans, pending = [None] * len(ROW_CHUNKS), scatter_plan(_w_in_blocks(dw_pieces, dws, 0, D), g_wp, g_wo)
        res1 = _inproj_bwd_dx([(dxbc, OFF_XBC), (daz, OFF_AZ), (dq_b, OFF_BQ), (dbz, OFF_BZ)], sv["wmain"],
                              ("narrow", dps16, sv["wsmall"]), None, f"inproj_bwd_dh1_{l}", plans[0])
        res2 = _inproj_bwd_dx([(dq_c, OFF_CQ), (dk_c, OFF_CK), (dv_c, OFF_CV), (dcz, OFF_CZ)], sv["wmain"],
                              ("acc", res1[0]), None, f"inproj_bwd_dh2_{l}", plans[1])
        dx, dx16, dnw = _inproj_bwd_dx([(dgates, OFF_G), (dkv_b, OFF_BK)], sv["wmain"], ("acc", res2[0]),
                                       (sv["x"], norm_w[l][None], dx), f"inproj_bwd_dx_{l}")
        if l == 0:
            parts[0] = [jnp.concatenate([res1[1], res2[1]], axis=1), *parts_po]
        gsm["norm_w"][l] = dnw[0]

    big = {}
    for idx, (name, w, m, v) in enumerate((("w_in", w_in, m_w_in, v_w_in), ("w_proj", w_proj, m_w_proj, v_w_proj),
                                          ("w_out", w_out, m_w_out, v_w_out))):
        cols = w.shape[-1]
        res = _sum_adamw([parts[l][idx].reshape(NDEV, -1, cols) for l in range(depth)], w.reshape(depth, -1, cols),
                         m.reshape(depth, -1, cols), v.reshape(depth, -1, cols), f"adamw_{name}")
        big[name] = [r.reshape(w.shape) for r in res]

    small_names = ("norm_w", "conv_b", "dt_bias", "a_log", "d_skip", "ssm_norm_w", "sinks", "f_bias")
    small_parts = [jnp.stack(gsm[k]) for k in small_names] + [g_final, jnp.stack(gsm["conv_w"]),
                                                              jnp.stack(gsm["gate_bias"]), loss_part.reshape(1)]
    shapes = [a.shape for a in small_parts]
    summed = _unpack_rows(_all_reduce_small(_pack_rows(small_parts)), shapes)
    g_small = dict(zip(small_names, summed[:len(small_names)]))
    g_small["final_norm_w"] = summed[len(small_names)]
    g_small["conv_w"] = lax.dynamic_slice_in_dim(summed[len(small_names) + 1], me * csh, csh, axis=2)
    g_small["gate_bias"] = lax.dynamic_slice_in_dim(summed[len(small_names) + 2], me * gsh, gsh, axis=2)
    loss = summed[len(small_names) + 3][0]

    ws = dict(norm_w=norm_w, conv_w=conv_w, conv_b=conv_b, dt_bias=dt_bias, a_log=a_log, d_skip=d_skip,
              ssm_norm_w=ssm_norm_w, sinks=sinks, f_bias=f_bias, gate_bias=gate_bias, final_norm_w=final_norm_w)
    ms = dict(norm_w=m_norm_w, conv_w=m_conv_w, conv_b=m_conv_b, dt_bias=m_dt_bias, a_log=m_a_log, d_skip=m_d_skip,
              ssm_norm_w=m_ssm_norm_w, sinks=m_sinks, f_bias=m_f_bias, gate_bias=m_gate_bias,
              final_norm_w=m_final_norm_w)
    vs = dict(norm_w=v_norm_w, conv_w=v_conv_w, conv_b=v_conv_b, dt_bias=v_dt_bias, a_log=v_a_log, d_skip=v_d_skip,
              ssm_norm_w=v_ssm_norm_w, sinks=v_sinks, f_bias=v_f_bias, gate_bias=v_gate_bias,
              final_norm_w=v_final_norm_w)
    order = list(ws)
    oshapes = [ws[k].shape for k in order]
    res = _adamw_small(_pack_rows([g_small[k] for k in order]), _pack_rows([ws[k] for k in order]),
                       _pack_rows([ms[k] for k in order]), _pack_rows([vs[k] for k in order]))
    d_s, m_s, v_s = (dict(zip(order, _unpack_rows(r, oshapes))) for r in res)

    names = ("norm_w", "w_in", "conv_w", "conv_b", "dt_bias", "a_log", "d_skip", "ssm_norm_w", "sinks", "f_bias",
             "gate_bias", "w_proj", "w_out", "final_norm_w")
    grads, deltas, new_m, new_v = [], [], [], []
    for k in names:
        if k in big:
            g, d_, m_, v_ = big[k]
        else:
            g, d_, m_, v_ = g_small[k], d_s[k], m_s[k], v_s[k]
        grads.append(g)
        deltas.append(d_)
        new_m.append(m_)
        new_v.append(v_)
    return (loss, dx.reshape(Bl, S, D), *grads, *deltas, *new_m, *new_v)
```

```python
import functools
import math

import jax
import jax.numpy as jnp
from jax import lax
from jax.experimental import pallas as pl
from jax.experimental.pallas import tpu as pltpu

F32 = jnp.float32
BF16 = jnp.bfloat16
MESH = pl.DeviceIdType.MESH
NDEV = 8

D = 1024
NH = 16
HD = 64
NST = 128
NGRP = 4
LCH = 128
EPS = 1e-6
ROPE_THETA = 10000.0
SCALE = HD ** -0.5
NEG = -1e30

LANES = 128
VMEM_LIMIT = 56 * 1024 * 1024

OFF_XBC, OFF_AZ, OFF_BQ, OFF_BZ, OFF_CQ, OFF_CK, OFF_CV, OFF_CZ, OFF_G, OFF_BK, OFF_BV = (
    0, 2048, 3072, 4096, 5120, 6144, 7168, 8192, 9216, 12288, 12544)
NMAIN = 12800
NIN = 12832
NSH = NIN // NDEV

ROW_CHUNKS = ((0, 512), (512, 1024))

ADAM_LR, ADAM_B1, ADAM_B2, ADAM_EPS, ADAM_WD, ADAM_STEP = 0.001, 0.9, 0.999, 1e-08, 0.01, 10


def _cparams(dims=None, vmem=None):
    return pltpu.CompilerParams(dimension_semantics=dims, vmem_limit_bytes=vmem)


def _dot(a, b):
    return jnp.dot(a, b, preferred_element_type=F32)


def _dot_nt(a, b):
    return lax.dot_general(a, b, (((1,), (1,)), ((), ())), preferred_element_type=F32)


def _dot_tn(a, b):
    return lax.dot_general(a, b, (((0,), (0,)), ((), ())), preferred_element_type=F32)


def _dot_hi(a, b):
    return jnp.dot(a, b, precision=lax.Precision.HIGHEST, preferred_element_type=F32)


def _sigmoid(x):
    return 0.5 * jnp.tanh(0.5 * x) + 0.5


def _softplus(x):
    return jnp.maximum(x, 0.0) + jnp.log(1.0 + jnp.exp(-jnp.abs(x)))


def _lane_iota(n=LANES):
    return lax.broadcasted_iota(jnp.int32, (1, n), 1)


def _rot_half(x):
    first = (_lane_iota() % HD) < (HD // 2)
    return jnp.where(first, pltpu.roll(x, LANES - HD // 2, 1), pltpu.roll(x, HD // 2, 1))


def _head_sum(x, head):
    m = (_lane_iota() < HD) if head == 0 else (_lane_iota() >= HD)
    return jnp.sum(jnp.where(m, x, 0.0), axis=1, keepdims=True)


def _me_and_peers():
    x, y, c = lax.axis_index("x"), lax.axis_index("y"), lax.axis_index("c")
    me = 4 * x + 2 * y + c
    peers = []
    for k in range(1, NDEV):
        kx, ky, kc = (k >> 2) & 1, (k >> 1) & 1, k & 1
        px, py, pc = x ^ kx, y ^ ky, c ^ kc
        peers.append(((px, py, pc), 4 * px + 2 * py + pc))
    return me, peers


class _Comm:
    def __init__(self, kind, arrays):
        self.kind, self.arrays, self.n = kind, list(arrays), len(arrays)
        any_spec = pl.BlockSpec(memory_space=pl.ANY)
        self.in_specs = [any_spec] * self.n
        self.out_specs = [any_spec] * self.n
        self.out_shape = [jax.ShapeDtypeStruct(((NDEV,) + a.shape) if kind == "gather" else a.shape, a.dtype)
                          for a in self.arrays]
        self.scratch = [pltpu.SemaphoreType.DMA((self.n, NDEV - 1)), pltpu.SemaphoreType.DMA((self.n, NDEV - 1)),
                        pltpu.SemaphoreType.DMA((self.n,))]

    def copies(self, ins, outs, sems):
        send_sems, recv_sems, local_sems = sems
        me, peers = _me_and_peers()
        out = []
        for a in range(self.n):
            mine = ins[a] if self.kind == "gather" else ins[a].at[me]
            out.append(pltpu.make_async_copy(mine, outs[a].at[me], local_sems.at[a]))
            for k, (peer, pidx) in enumerate(peers):
                src = ins[a] if self.kind == "gather" else ins[a].at[pidx]
                out.append(pltpu.make_async_remote_copy(
                    src_ref=src, dst_ref=outs[a].at[me], send_sem=send_sems.at[a, k], recv_sem=recv_sems.at[a, k],
                    device_id=peer, device_id_type=MESH))
        return out


def _gather_two_level(arrays, name):
    n = len(arrays)

    def body(*refs):
        ins, outs = refs[:n], refs[n:2 * n]
        send_sems, recv_sems, local_sems = refs[2 * n:]
        x, y, c = lax.axis_index("x"), lax.axis_index("y"), lax.axis_index("c")
        me, sibling = (x, y, c), (x, y, 1 - c)
        chips = [(1 - x, y), (x, 1 - y), (1 - x, 1 - y)]

        def slot(a, dev):
            return outs[a].at[4 * dev[0] + 2 * dev[1] + dev[2]]

        def copy(a, k, block, to, src=None):
            return pltpu.make_async_remote_copy(
                src_ref=slot(a, block) if src is None else src, dst_ref=slot(a, block),
                send_sem=send_sems.at[a, k], recv_sem=recv_sems.at[a, k], device_id=to, device_id_type=MESH)

        mine = [pltpu.make_async_copy(ins[a], slot(a, me), local_sems.at[a]) for a in range(n)]
        for cp in mine:
            cp.start()
        first = []
        for a in range(n):
            first.append(copy(a, 0, me, sibling, src=ins[a]))
            first += [copy(a, 1 + j, me, (*chip, c), src=ins[a]) for j, chip in enumerate(chips)]
        for cp in first:
            cp.start()
        passed = []
        for j, chip in enumerate(chips):
            for a in range(n):
                copy(a, 1 + j, (*chip, c), me).wait_recv()
                fwd = copy(a, 4 + j, (*chip, c), sibling)
                fwd.start()
                passed.append(fwd)
        for a in range(n):
            copy(a, 0, sibling, me).wait_recv()
            for j, chip in enumerate(chips):
                copy(a, 4 + j, (*chip, 1 - c), me).wait_recv()
        for cp in first + passed:
            cp.wait_send()
        for cp in mine:
            cp.wait()

    any_spec = pl.BlockSpec(memory_space=pl.ANY)
    return pl.pallas_call(
        body, name=name, out_shape=[jax.ShapeDtypeStruct((NDEV,) + a.shape, a.dtype) for a in arrays],
        in_specs=[any_spec] * n, out_specs=[any_spec] * n,
        scratch_shapes=[pltpu.SemaphoreType.DMA((n, NDEV - 1)), pltpu.SemaphoreType.DMA((n, NDEV - 1)),
                        pltpu.SemaphoreType.DMA((n,))])(*arrays)


def _hosted_call(body, comm, name, grid, in_specs, out_specs, out_shape, scratch, dims, operands):
    if comm is None:
        return pl.pallas_call(body, name=name, grid=grid, in_specs=in_specs, out_specs=out_specs, out_shape=out_shape,
                              scratch_shapes=scratch, compiler_params=_cparams(dims, VMEM_LIMIT))(*operands)
    n_in, n_out, n_scr, n = len(in_specs), len(out_specs), len(scratch), comm.n

    def hosted(*refs):
        hin, cin = refs[:n_in], refs[n_in:n_in + n]
        hout = refs[n_in + n:n_in + n + n_out]
        cout = refs[n_in + n + n_out:n_in + 2 * n + n_out]
        hscr = refs[n_in + 2 * n + n_out:n_in + 2 * n + n_out + n_scr]
        sems = refs[n_in + 2 * n + n_out + n_scr:]
        ids = [pl.program_id(a) for a in range(len(grid))]
        first = functools.reduce(jnp.logical_and, [i == 0 for i in ids])
        last = functools.reduce(jnp.logical_and, [i == g - 1 for i, g in zip(ids, grid)])

        @pl.when(first)
        def _():
            for cp in comm.copies(cin, cout, sems):
                cp.start()

        body(*hin, *hout, *hscr)

        @pl.when(last)
        def _():
            for cp in comm.copies(cin, cout, sems):
                cp.wait()

    return pl.pallas_call(
        hosted, name=name, grid=grid, in_specs=list(in_specs) + comm.in_specs,
        out_specs=list(out_specs) + comm.out_specs, out_shape=list(out_shape) + comm.out_shape,
        scratch_shapes=list(scratch) + comm.scratch,
        compiler_params=_cparams(("arbitrary",) * len(grid), VMEM_LIMIT))(*operands, *comm.arrays)


def _all_reduce_small(v):
    rows = v.shape[0]

    def body(v_ref, sum_ref, all_ref, send_sems, recv_sems):
        me, peers = _me_and_peers()
        all_ref[me] = v_ref[...]
        copies = []
        for k, (peer, _) in enumerate(peers):
            cp = pltpu.make_async_remote_copy(
                src_ref=v_ref, dst_ref=all_ref.at[me],
                send_sem=send_sems.at[k], recv_sem=recv_sems.at[k],
                device_id=peer, device_id_type=MESH)
            cp.start()
            copies.append(cp)
        for cp in copies:
            cp.wait()
        acc = all_ref[0]
        for d in range(1, NDEV):
            acc = acc + all_ref[d]
        sum_ref[...] = acc

    vm = pl.BlockSpec(memory_space=pltpu.VMEM)
    return pl.pallas_call(
        body, name="all_reduce_small",
        out_shape=jax.ShapeDtypeStruct((rows, LANES), F32),
        in_specs=[vm], out_specs=vm,
        scratch_shapes=[pltpu.VMEM((NDEV, rows, LANES), F32),
                        pltpu.SemaphoreType.DMA((NDEV - 1,)), pltpu.SemaphoreType.DMA((NDEV - 1,))],
    )(v)


def _adamw_math(w, g, m, v):
    m = ADAM_B1 * m + (1.0 - ADAM_B1) * g
    v = ADAM_B2 * v + (1.0 - ADAM_B2) * jnp.square(g)
    m_hat = m / (1.0 - ADAM_B1 ** ADAM_STEP)
    v_hat = v / (1.0 - ADAM_B2 ** ADAM_STEP)
    delta = -ADAM_LR * (m_hat / (jnp.sqrt(v_hat) + ADAM_EPS) + ADAM_WD * w)
    return delta, m, v


def _sum_adamw(parts, w, m, v, name):
    depth, rows, cols = w.shape
    tr = next(c for c in (256, 128, 64, 32, 16) if rows % c == 0)
    nb = rows // tr

    def body(*refs):
        p_refs, (w_ref, m_ref, v_ref, g_ref, d_ref, nm_ref, nv_ref) = refs[:depth], refs[depth:]
        l = pl.program_id(0)
        for ll in range(depth):
            @pl.when(l == ll)
            def _(ll=ll):
                g = p_refs[ll][0].astype(F32)
                for d in range(1, NDEV):
                    g = g + p_refs[ll][d].astype(F32)
                delta, nm, nv = _adamw_math(w_ref[0], g, m_ref[0], v_ref[0])
                g_ref[0] = g
                d_ref[0] = delta
                nm_ref[0] = nm
                nv_ref[0] = nv

    part = lambda ll: pl.BlockSpec((NDEV, tr, cols), lambda l, i, ll=ll: (0, jnp.where(l == ll, i, jnp.where(l < ll, 0, nb - 1)), 0))
    blk = pl.BlockSpec((1, tr, cols), lambda l, i: (l, i, 0))
    sds = jax.ShapeDtypeStruct((depth, rows, cols), F32)
    return pl.pallas_call(
        body, name=name, grid=(depth, nb),
        in_specs=[part(ll) for ll in range(depth)] + [blk, blk, blk],
        out_specs=[blk, blk, blk, blk], out_shape=[sds, sds, sds, sds],
        compiler_params=_cparams(("arbitrary", "arbitrary"), VMEM_LIMIT),
    )(*parts, w, m, v)


def _adamw_small(g, w, m, v):
    def body(g_ref, w_ref, m_ref, v_ref, d_ref, nm_ref, nv_ref):
        delta, nm, nv = _adamw_math(w_ref[...], g_ref[...], m_ref[...], v_ref[...])
        d_ref[...] = delta
        nm_ref[...] = nm
        nv_ref[...] = nv

    sds = jax.ShapeDtypeStruct(g.shape, F32)
    return pl.pallas_call(body, name="adamw_small", out_shape=[sds, sds, sds])(g, w, m, v)


def _matmul(a, b, out_dtype, name, tm=1024, tn=1024, tk=1024):
    M, K = a.shape
    N = b.shape[1]
    tm, tn, tk = min(tm, M), min(tn, N), min(tk, K)
    nk = K // tk

    def body(a_ref, b_ref, o_ref, acc):
        k = pl.program_id(2)

        @pl.when(k == 0)
        def _():
            acc[...] = jnp.zeros_like(acc)

        acc[...] += _dot(a_ref[...], b_ref[...])

        @pl.when(k == nk - 1)
        def _():
            o_ref[...] = acc[...].astype(out_dtype)

    return pl.pallas_call(
        body, name=name, grid=(M // tm, N // tn, nk),
        in_specs=[pl.BlockSpec((tm, tk), lambda i, j, k: (i, k)), pl.BlockSpec((tk, tn), lambda i, j, k: (k, j))],
        out_specs=pl.BlockSpec((tm, tn), lambda i, j, k: (i, j)),
        out_shape=jax.ShapeDtypeStruct((M, N), out_dtype),
        scratch_shapes=[pltpu.VMEM((tm, tn), F32)],
        compiler_params=_cparams(("parallel", "parallel", "arbitrary"), VMEM_LIMIT),
    )(a, b)


def _matmul_batched(a, b, out_dtype, name, tm=1024, tn=1024, tk=1024):
    G, M, K = a.shape
    N = b.shape[2]
    tm, tn, tk = min(tm, M), min(tn, N), min(tk, K)
    nk = K // tk

    def body(a_ref, b_ref, o_ref, acc):
        k = pl.program_id(3)

        @pl.when(k == 0)
        def _():
            acc[...] = jnp.zeros_like(acc)

        acc[...] += _dot(a_ref[0], b_ref[0])

        @pl.when(k == nk - 1)
        def _():
            o_ref[0] = acc[...].astype(out_dtype)

    return pl.pallas_call(
        body, name=name, grid=(G, M // tm, N // tn, nk),
        in_specs=[pl.BlockSpec((1, tm, tk), lambda g, i, j, k: (g, i, k)),
                  pl.BlockSpec((1, tk, tn), lambda g, i, j, k: (g, k, j))],
        out_specs=pl.BlockSpec((1, tm, tn), lambda g, i, j, k: (g, i, j)),
        out_shape=jax.ShapeDtypeStruct((G, M, N), out_dtype),
        scratch_shapes=[pltpu.VMEM((tm, tn), F32)],
        compiler_params=_cparams(("parallel", "parallel", "parallel", "arbitrary"), VMEM_LIMIT),
    )(a, b)


def _inproj_fwd(x2, nw, wmain, wsmall, cos128, sin128, S, li, comm=None):
    T = x2.shape[0]
    tm, tn = min(2048, S), 512
    nj, npos = NMAIN // tn, S // tm
    jq0, jk = OFF_BQ // tn, OFF_BK // tn

    def body(x_ref, nw_ref, w_ref, ws_ref, cos_ref, sin_ref, proj_ref, ps_ref, ht_ref, h_scr):
        j = pl.program_id(1)

        @pl.when(j == 0)
        def _():
            x = x_ref[...]
            r = lax.rsqrt(jnp.mean(x * x, axis=-1, keepdims=True) + EPS)
            h = (x * r * nw_ref[...]).astype(BF16)
            h_scr[...] = h
            ht_ref[...] = h.T
            ps_ref[...] = _dot(h, ws_ref[...])

        acc = _dot(h_scr[...], w_ref[...])

        def roped(c):
            xc = acc[:, LANES * c:LANES * (c + 1)]
            return (xc * cos_ref[...] + _rot_half(xc) * sin_ref[...]).astype(BF16)

        def plain(c):
            return acc[:, LANES * c:LANES * (c + 1)].astype(BF16)

        is_q = jnp.logical_or(j == jq0, j == jq0 + 1)
        is_k = j == jk

        @pl.when(is_q)
        def _():
            for c in range(4):
                proj_ref[:, LANES * c:LANES * (c + 1)] = roped(c)

        @pl.when(is_k)
        def _():
            for c in range(4):
                proj_ref[:, LANES * c:LANES * (c + 1)] = roped(c) if c < 2 else plain(c)

        @pl.when(jnp.logical_not(jnp.logical_or(is_q, is_k)))
        def _():
            proj_ref[...] = acc.astype(BF16)

    return _hosted_call(
        body, comm, f"inproj_fwd_{li}", (T // tm, nj),
        in_specs=[pl.BlockSpec((tm, D), lambda i, j: (i, 0)),
                  pl.BlockSpec((1, D), lambda i, j: (0, 0)),
                  pl.BlockSpec((D, tn), lambda i, j: (0, j)),
                  pl.BlockSpec((D, LANES), lambda i, j: (0, 0)),
                  pl.BlockSpec((tm, LANES), lambda i, j: (i % npos, 0)),
                  pl.BlockSpec((tm, LANES), lambda i, j: (i % npos, 0))],
        out_specs=[pl.BlockSpec((tm, tn), lambda i, j: (i, j)),
                   pl.BlockSpec((tm, LANES), lambda i, j: (i, 0)),
                   pl.BlockSpec((D, tm), lambda i, j: (0, i))],
        out_shape=[jax.ShapeDtypeStruct((T, NMAIN), BF16), jax.ShapeDtypeStruct((T, LANES), F32),
                   jax.ShapeDtypeStruct((D, T), BF16)],
        scratch=[pltpu.VMEM((tm, D), BF16)], dims=("parallel", "arbitrary"),
        operands=(x2, nw, wmain, wsmall, cos128, sin128))


def _inproj_bwd_dx(segs, wmain, init, final, name, comm=None):
    T = segs[0][0].shape[0]
    tm = min(1024, T)
    tk = 1024 if all(a.shape[1] % 1024 == 0 and c % 1024 == 0 for a, c in segs) else 512
    ni = T // tm
    k0s, nks, c0s = [], [], []
    for arr, col0 in segs:
        k0s.append(sum(nks))
        nks.append(arr.shape[1] // tk)
        c0s.append(col0 // tk)
    nk = sum(nks)
    ns = len(segs)

    def in_range(k, s):
        return jnp.logical_and(k >= k0s[s], k < k0s[s] + nks[s])

    def wcol(i, k):
        g = 0
        for s in range(ns):
            g = g + jnp.where(in_range(k, s), c0s[s] + k - k0s[s], 0)
        return (0, g)

    n_init = 2 if init[0] == "narrow" else 1

    def body(*refs):
        seg_refs, w_ref = refs[:ns], refs[ns]
        init_refs = refs[ns + 1:ns + 1 + n_init]
        rest = refs[ns + 1 + n_init:]
        i, k = pl.program_id(0), pl.program_id(1)
        acc = rest[-1]

        @pl.when(k == 0)
        def _():
            if init[0] == "narrow":
                acc[...] = _dot_nt(init_refs[0][...], init_refs[1][...])
            else:
                acc[...] = init_refs[0][...]

        for s in range(ns):
            @pl.when(in_range(k, s))
            def _(s=s):
                acc[...] += _dot_nt(seg_refs[s][...], w_ref[...])

        if final is None:
            @pl.when(k == nk - 1)
            def _():
                rest[0][...] = acc[...]
        else:
            x_ref, nw_ref, dxo_ref, dx_ref, dx16_ref, dnw_ref = rest[:6]

            @pl.when(jnp.logical_and(i == 0, k == 0))
            def _():
                dnw_ref[...] = jnp.zeros_like(dnw_ref)

            @pl.when(k == nk - 1)
            def _():
                x = x_ref[...]
                r = lax.rsqrt(jnp.mean(x * x, axis=-1, keepdims=True) + EPS)
                dh = acc[...]
                g = dh * nw_ref[...]
                dx = dxo_ref[...] + r * g - x * (r * r * r) * jnp.mean(g * x, axis=-1, keepdims=True)
                dx_ref[...] = dx
                dx16_ref[...] = dx.astype(BF16)
                dnw_ref[0:1, :] += jnp.sum(dh * x * r, axis=0, keepdims=True)

    row = pl.BlockSpec((tm, D), lambda i, k: (i, 0))
    in_specs = [pl.BlockSpec((tm, tk), lambda i, k, s=s: (i, jnp.clip(k - k0s[s], 0, nks[s] - 1))) for s in range(ns)]
    in_specs.append(pl.BlockSpec((D, tk), wcol))
    operands = [a for a, _ in segs] + [wmain]
    if init[0] == "narrow":
        in_specs += [pl.BlockSpec((tm, LANES), lambda i, k: (i, 0)), pl.BlockSpec((D, LANES), lambda i, k: (0, 0))]
    else:
        in_specs.append(row)
    operands += list(init[1:])
    if final is None:
        out_specs, out_shape = [row], [jax.ShapeDtypeStruct((T, D), F32)]
    else:
        in_specs += [row, pl.BlockSpec((1, D), lambda i, k: (0, 0)), row]
        operands += list(final)
        out_specs = [row, row, pl.BlockSpec((8, D), lambda i, k: (0, 0))]
        out_shape = [jax.ShapeDtypeStruct((T, D), F32), jax.ShapeDtypeStruct((T, D), BF16),
                     jax.ShapeDtypeStruct((8, D), F32)]
    return _hosted_call(body, comm, name, (ni, nk), in_specs=in_specs, out_specs=out_specs, out_shape=out_shape,
                        scratch=[pltpu.VMEM((tm, D), F32)], dims=("arbitrary", "arbitrary"), operands=tuple(operands))


def _merge_fwd(ya, yb, yc, proj, gbias, wp, wout, x2, li):
    T = x2.shape[0]
    tm = min(512, T)
    gcol = OFF_G // D

    def body(ya_ref, yb_ref, yc_ref, g0_ref, g1_ref, g2_ref, gb_ref, wp_ref, wo_ref, x_ref, xn_ref, br_ref, yt_ref):
        merged = jnp.zeros((tm, D), F32)
        for i, (y_ref, g_ref) in enumerate(((ya_ref, g0_ref), (yb_ref, g1_ref), (yc_ref, g2_ref))):
            y = y_ref[...]
            yt_ref[i] = y.T
            br = _dot(y, wp_ref[i])
            br_ref[i] = br.astype(BF16)
            gate = _sigmoid(g_ref[...].astype(F32) + gb_ref[i:i + 1, :])
            merged = merged + gate * br
        xn_ref[...] = x_ref[...] + _dot(merged.astype(BF16), wo_ref[...])

    row = lambda c: pl.BlockSpec((tm, D), lambda i, c=c: (i, c))
    return pl.pallas_call(
        body, name=f"merge_fwd_{li}", grid=(T // tm,),
        in_specs=[row(0), row(0), row(0), row(gcol), row(gcol + 1), row(gcol + 2),
                  pl.BlockSpec((3, D), lambda i: (0, 0)),
                  pl.BlockSpec((3, D, D), lambda i: (0, 0, 0)),
                  pl.BlockSpec((D, D), lambda i: (0, 0)),
                  row(0)],
        out_specs=[row(0), pl.BlockSpec((3, tm, D), lambda i: (0, i, 0)), pl.BlockSpec((3, D, tm), lambda i: (0, 0, i))],
        out_shape=[jax.ShapeDtypeStruct((T, D), F32), jax.ShapeDtypeStruct((3, T, D), BF16),
                   jax.ShapeDtypeStruct((3, D, T), BF16)],
        compiler_params=_cparams(("parallel",), VMEM_LIMIT),
    )(ya, yb, yc, proj, proj, proj, gbias, wp, wout, x2)


def _merge_bwd(dxo16, wout, wp, br, proj, gbias, ob, oc, li):
    T = dxo16.shape[0]
    tm = min(256, T)
    gcol = OFF_G // D

    def body(dx_ref, wo_ref, wp_ref, br_ref, g0_ref, g1_ref, g2_ref, gb_ref, ob_ref, oc_ref, zb_ref, zc_ref,
             dbr_ref, dg_ref, mt_ref, dgb_ref, dya_ref, dob_ref, dzb_ref, doc_ref, dzc_ref):
        @pl.when(pl.program_id(0) == 0)
        def _():
            dgb_ref[...] = jnp.zeros_like(dgb_ref)

        dm = _dot_nt(dx_ref[...], wo_ref[...])
        merged = jnp.zeros((tm, D), F32)
        dys = []
        for i, g_ref in enumerate((g0_ref, g1_ref, g2_ref)):
            b = br_ref[i].astype(F32)
            gate = _sigmoid(g_ref[...].astype(F32) + gb_ref[i:i + 1, :])
            merged = merged + gate * b
            dbr = (dm * gate).astype(BF16)
            dbr_ref[i] = dbr
            dgate = dm * b * gate * (1.0 - gate)
            dg_ref[:, D * i:D * (i + 1)] = dgate.astype(BF16)
            dgb_ref[i:i + 1, :] += jnp.sum(dgate, axis=0, keepdims=True)
            dys.append(_dot_nt(dbr, wp_ref[i]))
        mt_ref[...] = merged.astype(BF16).T
        dya_ref[...] = dys[0].astype(BF16)
        for dy, o_ref, z_ref, do_ref, dz_ref in ((dys[1], ob_ref, zb_ref, dob_ref, dzb_ref),
                                                 (dys[2], oc_ref, zc_ref, doc_ref, dzc_ref)):
            z = z_ref[...].astype(F32)
            sg = _sigmoid(z)
            do_ref[...] = (dy * z * sg).astype(BF16)
            dz_ref[...] = (dy * o_ref[...].astype(F32) * sg * (1.0 + z * (1.0 - sg))).astype(BF16)

    row = lambda c: pl.BlockSpec((tm, D), lambda i, c=c: (i, c))
    sds = jax.ShapeDtypeStruct((T, D), BF16)
    return pl.pallas_call(
        body, name=f"merge_bwd_{li}", grid=(T // tm,),
        in_specs=[row(0), pl.BlockSpec((D, D), lambda i: (0, 0)), pl.BlockSpec((3, D, D), lambda i: (0, 0, 0)),
                  pl.BlockSpec((3, tm, D), lambda i: (0, i, 0)),
                  row(gcol), row(gcol + 1), row(gcol + 2),
                  pl.BlockSpec((3, D), lambda i: (0, 0)),
                  row(0), row(0), row(OFF_BZ // D), row(OFF_CZ // D)],
        out_specs=[pl.BlockSpec((3, tm, D), lambda i: (0, i, 0)),
                   pl.BlockSpec((tm, 3 * D), lambda i: (i, 0)),
                   pl.BlockSpec((D, tm), lambda i: (0, i)),
                   pl.BlockSpec((8, D), lambda i: (0, 0)),
                   row(0), row(0), row(0), row(0), row(0)],
        out_shape=[jax.ShapeDtypeStruct((3, T, D), BF16), jax.ShapeDtypeStruct((T, 3 * D), BF16),
                   jax.ShapeDtypeStruct((D, T), BF16), jax.ShapeDtypeStruct((8, D), F32), sds, sds, sds, sds, sds],
        compiler_params=_cparams(("arbitrary",), VMEM_LIMIT),
    )(dxo16, wout, wp, br, proj, proj, proj, gbias, ob, oc, proj, proj)


def _final_loss(x2, tgt, fw):
    T = x2.shape[0]
    tm = min(512, T)
    ni = T // tm

    def body(x_ref, t_ref, w_ref, dx_ref, dx16_ref, st_ref):
        i = pl.program_id(0)

        @pl.when(i == 0)
        def _():
            st_ref[...] = jnp.zeros_like(st_ref)

        x = x_ref[...]
        r = lax.rsqrt(jnp.mean(x * x, axis=-1, keepdims=True) + EPS)
        xh = x * r
        err = xh * w_ref[...] - t_ref[...]
        dy = err * (1.0 / D)
        g = dy * w_ref[...]
        dx = r * g - x * (r * r * r) * jnp.mean(g * x, axis=-1, keepdims=True)
        dx_ref[...] = dx
        dx16_ref[...] = dx.astype(BF16)
        st_ref[0:1, :] += jnp.sum(dy * xh, axis=0, keepdims=True)
        st_ref[1:2, :] += jnp.sum(err * err, axis=0, keepdims=True)

        @pl.when(i == ni - 1)
        def _():
            tot = jnp.sum(st_ref[1:2, :], axis=1, keepdims=True) * (0.5 / D)
            st_ref[2:3, :] = jnp.broadcast_to(tot, (1, D))

    row = pl.BlockSpec((tm, D), lambda i: (i, 0))
    return pl.pallas_call(
        body, name="final_loss", grid=(ni,),
        in_specs=[row, row, pl.BlockSpec((1, D), lambda i: (0, 0))],
        out_specs=[row, row, pl.BlockSpec((8, D), lambda i: (0, 0))],
        out_shape=[jax.ShapeDtypeStruct((T, D), F32), jax.ShapeDtypeStruct((T, D), BF16),
                   jax.ShapeDtypeStruct((8, D), F32)],
        compiler_params=_cparams(("arbitrary",), VMEM_LIMIT),
    )(x2, tgt, fw)


def _fox_cum(ps, fb_row, S, li):
    T = ps.shape[0]
    blk = min(4 * LCH, S)
    nb, nsub = S // blk, blk // LCH

    def body(ps_ref, fb_ref, cum_ref, carry):
        @pl.when(pl.program_id(1) == 0)
        def _():
            carry[...] = jnp.zeros_like(carry)

        r = lax.broadcasted_iota(jnp.int32, (LCH, LCH), 0)
        c = lax.broadcasted_iota(jnp.int32, (LCH, LCH), 1)
        tri = (r >= c).astype(F32)
        run = carry[0:1, :]
        for u in range(nsub):
            rows = slice(LCH * u, LCH * (u + 1))
            logf = -_softplus(-(ps_ref[rows, :] + fb_ref[...]))
            cum = _dot_hi(tri, logf) + run
            cum_ref[rows, :] = cum
            run = cum[LCH - 1:LCH, :]
        carry[0:1, :] = run

    return pl.pallas_call(
        body, name=f"fox_cum_{li}", grid=(T // S, nb),
        in_specs=[pl.BlockSpec((blk, LANES), lambda b, i: (b * nb + i, 0)),
                  pl.BlockSpec((1, LANES), lambda b, i: (0, 0))],
        out_specs=pl.BlockSpec((blk, LANES), lambda b, i: (b * nb + i, 0)),
        out_shape=jax.ShapeDtypeStruct((T, LANES), F32),
        scratch_shapes=[pltpu.VMEM((8, LANES), F32)],
        compiler_params=_cparams(("arbitrary", "arbitrary")),
    )(ps, fb_row)


def _fox_cum_bwd(dcum, ps, fb_row, S, li):
    T = ps.shape[0]
    rows_blk = min(4 * LCH, S)
    nb, nsub = S // rows_blk, rows_blk // LCH

    def body(dc_ref, ps_ref, fb_ref, df_ref, dfb_ref, carry):
        b, i = pl.program_id(0), pl.program_id(1)

        @pl.when(i == 0)
        def _():
            carry[...] = jnp.zeros_like(carry)

        @pl.when(jnp.logical_and(b == 0, i == 0))
        def _():
            dfb_ref[...] = jnp.zeros_like(dfb_ref)

        r = lax.broadcasted_iota(jnp.int32, (LCH, LCH), 0)
        c = lax.broadcasted_iota(jnp.int32, (LCH, LCH), 1)
        tri = (c >= r).astype(F32)
        lane = _lane_iota()
        live = jnp.logical_and(lane >= NH, lane < 2 * NH)
        run = carry[0:1, :]
        dfb = jnp.zeros((1, LANES), F32)
        for u in reversed(range(nsub)):
            rows = slice(LCH * u, LCH * (u + 1))
            dc = dc_ref[rows, :]
            dlogf = _dot_hi(tri, dc) + run
            run = run + jnp.sum(dc, axis=0, keepdims=True)
            df = jnp.where(live, dlogf * _sigmoid(-(ps_ref[rows, :] + fb_ref[...])), 0.0)
            df_ref[rows, :] = df
            dfb = dfb + jnp.sum(df, axis=0, keepdims=True)
        carry[0:1, :] = run
        dfb_ref[0:1, :] += dfb

    blk = pl.BlockSpec((rows_blk, LANES), lambda b, i: (b * nb + nb - 1 - i, 0))
    return pl.pallas_call(
        body, name=f"fox_cum_bwd_{li}", grid=(T // S, nb),
        in_specs=[blk, blk, pl.BlockSpec((1, LANES), lambda b, i: (0, 0))],
        out_specs=[blk, pl.BlockSpec((8, LANES), lambda b, i: (0, 0))],
        out_shape=[jax.ShapeDtypeStruct((T, LANES), F32), jax.ShapeDtypeStruct((8, LANES), F32)],
        scratch_shapes=[pltpu.VMEM((8, LANES), F32)],
        compiler_params=_cparams(("arbitrary", "arbitrary")),
    )(dcum, ps, fb_row)


def _fox_blocks(S):
    bq = min(512, S)
    return bq, S // bq


def _split3(c):
    hi = c.astype(BF16).astype(F32)
    r = c - hi
    mid = r.astype(BF16).astype(F32)
    return hi, mid, (r - mid).astype(BF16).astype(F32)


def _augment(x, bias_row, key_side, hh):
    n = x.shape[0]
    b0 = HD if hh == 0 else 0
    hi, mid, lo = _split3(bias_row)
    one = jnp.ones_like(bias_row)
    six = (one, one, one, hi, mid, lo) if key_side else (hi, mid, lo, one, one, one)
    sub = lax.broadcasted_iota(jnp.int32, (LANES, 1), 0)
    a = jnp.zeros((LANES, n), F32)
    for t, r in enumerate(six):
        a = jnp.where(sub == b0 + t, r, a)
    lane = _lane_iota()
    return jnp.where(jnp.logical_and(lane >= b0, lane < b0 + 6), a.T, x).astype(BF16)


def _col_to_row(col):
    return jnp.broadcast_to(col, (col.shape[0], LANES)).T[0:1, :]


def _fox_fwd(proj, cum_row, S, li, comm=None):
    T = proj.shape[0]
    B = T // S
    bq, nq = _fox_blocks(S)
    qc, kc, vc, zc = OFF_CQ // LANES, OFF_CK // LANES, OFF_CV // LANES, OFF_CZ // LANES

    def body(q_ref, k_ref, v_ref, z_ref, cr_ref, y_ref, o_ref, lse_ref, kaug, vaug):
        i = pl.program_id(2)
        m0 = _lane_iota() < HD

        @pl.when(i == 0)
        def _():
            vf = v_ref[...]
            for hh in range(2):
                for t in range(nq):
                    rows = slice(bq * t, bq * (t + 1))
                    kaug[hh, rows, :] = _augment(k_ref[rows, :].astype(F32), -cr_ref[0, hh, t], True, hh)
                vaug[hh] = jnp.where(m0 if hh == 0 else jnp.logical_not(m0), vf, jnp.ones_like(vf))

        q2 = q_ref[...].astype(F32) * SCALE
        row = lax.broadcasted_iota(jnp.int32, (bq, bq), 0)
        col = lax.broadcasted_iota(jnp.int32, (bq, bq), 1)
        qa = [_augment(jnp.where(m0 if hh == 0 else jnp.logical_not(m0), q2, 0.0), cr_ref[0, hh, i], False, hh)
              for hh in range(2)]

        def step(j, carry, masked):
            start = pl.multiple_of(j * bq, bq)
            out = []
            for hh in range(2):
                m, acc = carry[2 * hh:2 * hh + 2]
                s = _dot_nt(qa[hh], kaug[hh, pl.ds(start, bq), :])
                if masked:
                    s = jnp.where(row >= col, s, NEG)
                mn = jnp.maximum(m, jnp.max(s, axis=1, keepdims=True))
                p = jnp.exp(s - mn)
                out += [mn, jnp.exp(m - mn) * acc + _dot(p.astype(BF16), vaug[hh, pl.ds(start, bq), :])]
            return tuple(out)

        init = (jnp.full((bq, 1), NEG, F32), jnp.zeros((bq, LANES), F32)) * 2
        carry = step(i, lax.fori_loop(0, i, functools.partial(step, masked=False), init), True)
        outs = []
        for hh in range(2):
            m, acc = carry[2 * hh:2 * hh + 2]
            other = HD if hh == 0 else 0
            l = acc[:, other:other + 1]
            outs.append(acc / l)
            lse_ref[0, hh, 0] = _col_to_row(m + jnp.log(l))
        o2 = jnp.where(m0, outs[0], outs[1])
        z = z_ref[...].astype(F32)
        o_ref[...] = o2.astype(BF16)
        y_ref[...] = (o2 * z * _sigmoid(z)).astype(BF16)

    qblk = lambda c: pl.BlockSpec((bq, LANES), lambda b, p, i, c=c: (b * nq + i, c + p))
    sblk = lambda c: pl.BlockSpec((S, LANES), lambda b, p, i, c=c: (b, c + p))
    return _hosted_call(
        body, comm, f"fox_fwd_{li}", (B, NH // 2, nq),
        in_specs=[qblk(qc), sblk(kc), sblk(vc), qblk(zc),
                  pl.BlockSpec((1, 2, nq, 1, bq), lambda b, p, i: (b, p, 0, 0, 0))],
        out_specs=[qblk(0), qblk(0), pl.BlockSpec((1, 2, 1, 1, bq), lambda b, p, i: (b, p, i, 0, 0))],
        out_shape=[jax.ShapeDtypeStruct((T, D), BF16), jax.ShapeDtypeStruct((T, D), BF16),
                   jax.ShapeDtypeStruct((B, NH, nq, 1, bq), F32)],
        scratch=[pltpu.VMEM((2, S, LANES), BF16), pltpu.VMEM((2, S, LANES), BF16)],
        dims=("parallel", "parallel", "arbitrary"), operands=(proj, proj, proj, proj, cum_row))


def _fox_bwd(proj, do, o, cum_row, lse, S, li, comm=None):
    T = proj.shape[0]
    B = T // S
    bq, nq = _fox_blocks(S)
    qc, kc, vc = OFF_CQ // LANES, OFF_CK // LANES, OFF_CV // LANES

    def body(q_ref, k_ref, v_ref, do_ref, o_ref, cr_ref, lse_ref, dq_ref, dk_ref, dv_ref, dc_ref, dr_ref,
             dq_scr, dr_scr, qaug):
        j = pl.program_id(2)
        m0 = _lane_iota() < HD

        @pl.when(j == 0)
        def _():
            dq_scr[...] = jnp.zeros_like(dq_scr)
            dr_scr[...] = jnp.zeros_like(dr_scr)
            for t in range(nq):
                rows = slice(bq * t, bq * (t + 1))
                qf = q_ref[rows, :].astype(F32) * SCALE
                for hh in range(2):
                    sel = m0 if hh == 0 else jnp.logical_not(m0)
                    qaug[hh, rows, :] = _augment(jnp.where(sel, qf, 0.0), cr_ref[0, hh, t] - lse_ref[0, hh, t],
                                                 False, hh)

        k2 = k_ref[...]
        v2 = v_ref[...]
        zk = jnp.zeros_like(k2)
        kh = (jnp.where(m0, k2, zk), jnp.where(m0, zk, k2))
        kf = k2.astype(F32)
        ka = [_augment(kf, -cr_ref[0, hh, j], True, hh) for hh in range(2)]
        row = lax.broadcasted_iota(jnp.int32, (bq, bq), 0)
        col = lax.broadcasted_iota(jnp.int32, (bq, bq), 1)

        def step(i, carry, masked):
            dk, dv, dc0, dc1 = carry
            dcs = [dc0, dc1]
            start = pl.multiple_of(i * bq, bq)
            q2 = q_ref[pl.ds(start, bq), :]
            do2 = do_ref[pl.ds(start, bq), :]
            prod = do2.astype(F32) * o_ref[pl.ds(start, bq), :].astype(F32)
            zq = jnp.zeros_like(q2)
            dq = jnp.zeros((bq, LANES), F32)
            for hh in range(2):
                sel = m0 if hh == 0 else jnp.logical_not(m0)
                qh = jnp.where(sel, q2, zq)
                doh = jnp.where(sel, do2, zq)
                delta = _head_sum(prod, hh)
                s = _dot_nt(qaug[hh, pl.ds(start, bq), :], ka[hh])
                if masked:
                    s = jnp.where(row >= col, s, NEG)
                p = jnp.exp(s)
                dp = _dot_nt(doh, v2)
                ds = p * (dp - delta)
                dcs[hh] = dcs[hh] - jnp.sum(ds, axis=0, keepdims=True)
                dr_scr[hh, pl.ds(start, bq), :] += jnp.sum(ds, axis=1, keepdims=True)
                dsb = ds.astype(BF16)
                dv = dv + _dot_tn(p.astype(BF16), doh)
                dk = dk + _dot_tn(dsb, qh)
                dq = dq + _dot(dsb, kh[hh])
            dq_scr[pl.ds(start, bq), :] += dq
            return dk, dv, dcs[0], dcs[1]

        zero = jnp.zeros((bq, LANES), F32)
        zrow = jnp.zeros((1, bq), F32)
        carry = step(j, (zero, zero, zrow, zrow), True)
        dk, dv, dc0, dc1 = lax.fori_loop(j + 1, nq, functools.partial(step, masked=False), carry)
        dk_ref[...] = (dk * SCALE).astype(BF16)
        dv_ref[...] = dv.astype(BF16)
        dc_ref[0, 0, 0] = dc0
        dc_ref[0, 1, 0] = dc1

        @pl.when(j == nq - 1)
        def _():
            dq_ref[...] = (dq_scr[...] * SCALE).astype(BF16)
            step_r = min(4 * LANES, S)
            for hh in range(2):
                for t in range(S // step_r):
                    dr_ref[0, hh, :, step_r * t:step_r * (t + 1)] = _col_to_row(dr_scr[hh, step_r * t:step_r * (t + 1), :])

    sblk = lambda c: pl.BlockSpec((S, LANES), lambda b, p, j, c=c: (b, c + p))
    kblk = lambda c: pl.BlockSpec((bq, LANES), lambda b, p, j, c=c: (b * nq + j, c + p))
    rows_spec = pl.BlockSpec((1, 2, nq, 1, bq), lambda b, p, j: (b, p, 0, 0, 0))
    row_spec = pl.BlockSpec((1, 2, 1, S), lambda b, p, j: (b, p, 0, 0))
    return _hosted_call(
        body, comm, f"fox_bwd_{li}", (B, NH // 2, nq),
        in_specs=[sblk(qc), kblk(kc), kblk(vc), sblk(0), sblk(0), rows_spec, rows_spec],
        out_specs=[sblk(0), kblk(0), kblk(0), pl.BlockSpec((1, 2, 1, 1, bq), lambda b, p, j: (b, p, j, 0, 0)),
                   row_spec],
        out_shape=[jax.ShapeDtypeStruct((T, D), BF16), jax.ShapeDtypeStruct((T, D), BF16),
                   jax.ShapeDtypeStruct((T, D), BF16), jax.ShapeDtypeStruct((B, NH, nq, 1, bq), F32),
                   jax.ShapeDtypeStruct((B, NH, 1, S), F32)],
        scratch=[pltpu.VMEM((S, LANES), F32), pltpu.VMEM((2, S, 1), F32), pltpu.VMEM((2, S, LANES), BF16)],
        dims=("parallel", "parallel", "arbitrary"), operands=(proj, proj, proj, do, o, cum_row, lse))


def _swa_blocks(S):
    bq = min(512, S)
    return bq, S // bq, bq // LCH


def _dup_head(xw, kvl):
    m0 = _lane_iota() < HD
    a = jnp.where(m0 if kvl == 0 else jnp.logical_not(m0), xw, 0.0)
    return (a + pltpu.roll(a, HD, 1)).astype(BF16)


def _band(same_block):
    r = lax.broadcasted_iota(jnp.int32, (LCH, LCH), 0)
    c = lax.broadcasted_iota(jnp.int32, (LCH, LCH), 1)
    return (c <= r) if same_block else (c > r)


def _stack_heads(ref, rows, kvl):
    m0 = _lane_iota() < HD
    parts = []
    for ch in (2 * kvl, 2 * kvl + 1):
        x = ref[rows, LANES * ch:LANES * (ch + 1)]
        parts += [jnp.where(m0, x, jnp.zeros_like(x)), jnp.where(m0, jnp.zeros_like(x), x)]
    return jnp.concatenate(parts, axis=0)


def _stack_delta(do_ref, o_ref, rows, kvl, scale=None):
    parts = []
    for ch in (2 * kvl, 2 * kvl + 1):
        lanes = slice(LANES * ch, LANES * (ch + 1))
        prod = do_ref[rows, lanes].astype(F32) * o_ref[rows, lanes].astype(F32)
        parts += [_head_sum(prod, 0), _head_sum(prod, 1)]
    out = jnp.concatenate(parts, axis=0)
    return out if scale is None else out * scale


def _stack_cols(ref, rows, kvl):
    return jnp.concatenate([ref[0, 4 * kvl + t, rows, :] for t in range(4)], axis=0)


def _swa_fwd(proj, sinks, S, li):
    T = proj.shape[0]
    B = T // S
    bq, nq, nsub = _swa_blocks(S)
    nrow = S // LCH
    qc, zc, kc, vc = OFF_BQ // 512, OFF_BZ // 512, OFF_BK // LANES, OFF_BV // LANES

    def body(sk_ref, q_ref, z_ref, kp_ref, kc_ref, vp_ref, vc_ref, y_ref, o_ref, lse_ref):
        c, i = pl.program_id(0), pl.program_id(2)
        m0 = _lane_iota() < HD
        kw = jnp.concatenate([kp_ref[...].astype(F32), kc_ref[...].astype(F32)], axis=0)
        vw = jnp.concatenate([vp_ref[...].astype(F32), vc_ref[...].astype(F32)], axis=0)
        kd = (_dup_head(kw, 0), _dup_head(kw, 1))
        vd = (_dup_head(vw, 0), _dup_head(vw, 1))
        valid = jnp.concatenate([_band(False), _band(True)], axis=1)
        col = lax.broadcasted_iota(jnp.int32, (LCH, 2 * LCH), 1)
        valid_first = jnp.logical_and(valid, jnp.logical_or(col >= LCH, i > 0))
        for r in range(nsub):
            rows = slice(LCH * r, LCH * (r + 1))
            msk = valid_first if r == 0 else valid
            for ch in range(4):
                kvl = ch // 2
                kwin = kd[kvl][LCH * r:LCH * (r + 2)]
                vwin = vd[kvl][LCH * r:LCH * (r + 2)]
                lanes = slice(LANES * ch, LANES * (ch + 1))
                q2 = q_ref[rows, lanes]
                outs = []
                for hh in range(2):
                    hl = 2 * ch + hh
                    qh = jnp.where(m0 if hh == 0 else jnp.logical_not(m0), q2, jnp.zeros_like(q2))
                    s = jnp.where(msk, _dot_nt(qh, kwin) * SCALE, NEG)
                    sink = sk_ref[8 * c + hl]
                    m = jnp.maximum(jnp.max(s, axis=1, keepdims=True), sink)
                    p = jnp.exp(s - m)
                    l = jnp.sum(p, axis=1, keepdims=True) + jnp.exp(sink - m)
                    outs.append(_dot(p.astype(BF16), vwin) / l)
                    lse_ref[0, hl, rows, :] = m + jnp.log(l)
                o2 = jnp.where(m0, outs[0], outs[1])
                z = z_ref[rows, lanes].astype(F32)
                o_ref[rows, lanes] = o2.astype(BF16)
                y_ref[rows, lanes] = (o2 * z * _sigmoid(z)).astype(BF16)

    wide = lambda cc: pl.BlockSpec((bq, 512), lambda c, b, i, cc=cc: (b * nq + i, cc + c))
    cur = lambda cc: pl.BlockSpec((bq, LANES), lambda c, b, i, cc=cc: (b * nq + i, cc + c))
    prev = lambda cc: pl.BlockSpec((LCH, LANES), lambda c, b, i, cc=cc: (b * nrow + jnp.maximum(i * nsub - 1, 0), cc + c))
    return pl.pallas_call(
        body, name=f"swa_fwd_{li}", grid=(2, B, nq),
        in_specs=[pl.BlockSpec(memory_space=pltpu.SMEM), wide(qc), wide(zc), prev(kc), cur(kc), prev(vc), cur(vc)],
        out_specs=[wide(0), wide(0), pl.BlockSpec((1, 8, bq, 1), lambda c, b, i: (b, c, i, 0))],
        out_shape=[jax.ShapeDtypeStruct((T, D), BF16), jax.ShapeDtypeStruct((T, D), BF16),
                   jax.ShapeDtypeStruct((B, NH, S, 1), F32)],
        compiler_params=_cparams(("parallel", "parallel", "parallel"), VMEM_LIMIT),
    )(sinks, proj, proj, proj, proj, proj, proj)


def _swa_bwd_dq(proj, do, o, lse, sinks, cos128, sin128, S, li):
    T = proj.shape[0]
    B = T // S
    bq, nq, nsub = _swa_blocks(S)
    nrow = S // LCH
    qc, kc, vc = OFF_BQ // 512, OFF_BK // LANES, OFF_BV // LANES

    def body(sk_ref, q_ref, do_ref, o_ref, lse_ref, kp_ref, kc_ref, vp_ref, vc_ref, cos_ref, sin_ref, dq_ref, dsk_ref):
        c, b, i = pl.program_id(0), pl.program_id(1), pl.program_id(2)

        @pl.when(jnp.logical_and(b == 0, i == 0))
        def _():
            dsk_ref[...] = jnp.zeros_like(dsk_ref)

        m0 = _lane_iota() < HD
        kw = jnp.concatenate([kp_ref[...].astype(F32), kc_ref[...].astype(F32)], axis=0)
        vw = jnp.concatenate([vp_ref[...].astype(F32), vc_ref[...].astype(F32)], axis=0)
        kd = (_dup_head(kw, 0), _dup_head(kw, 1))
        vd = (_dup_head(vw, 0), _dup_head(vw, 1))
        valid = jnp.concatenate([_band(False), _band(True)], axis=1)
        col = lax.broadcasted_iota(jnp.int32, (LCH, 2 * LCH), 1)
        valid_first = jnp.logical_and(valid, jnp.logical_or(col >= LCH, i > 0))
        dsk = [jnp.zeros((1, 1), F32) for _ in range(8)]
        valid4 = jnp.concatenate([valid] * 4, axis=0)
        valid4_first = jnp.concatenate([valid_first] * 4, axis=0)
        for r in range(nsub):
            rows = slice(LCH * r, LCH * (r + 1))
            msk = valid4_first if r == 0 else valid4
            for kvl in range(2):
                kwin = kd[kvl][LCH * r:LCH * (r + 2)]
                vwin = vd[kvl][LCH * r:LCH * (r + 2)]
                qs = _stack_heads(q_ref, rows, kvl)
                dos = _stack_heads(do_ref, rows, kvl)
                delta = _stack_delta(do_ref, o_ref, rows, kvl)
                lse = _stack_cols(lse_ref, rows, kvl)
                sink = jnp.concatenate([jnp.full((LCH, 1), sk_ref[8 * c + 4 * kvl + t], F32) for t in range(4)], axis=0)
                s = jnp.where(msk, _dot_nt(qs, kwin) * SCALE, NEG)
                p = jnp.exp(s - lse)
                ds = p * (_dot_nt(dos, vwin) - delta)
                dqs = _dot(ds.astype(BF16), kwin) * SCALE
                dsink = jnp.exp(sink - lse) * delta
                for t in range(4):
                    hl = 4 * kvl + t
                    dsk[hl] = dsk[hl] - jnp.sum(dsink[LCH * t:LCH * (t + 1)], axis=0, keepdims=True)
                for u in range(2):
                    lanes = slice(LANES * (2 * kvl + u), LANES * (2 * kvl + u + 1))
                    dq2 = jnp.where(m0, dqs[LCH * 2 * u:LCH * (2 * u + 1)], dqs[LCH * (2 * u + 1):LCH * (2 * u + 2)])
                    dq2 = dq2 * cos_ref[rows, :] - _rot_half(dq2) * sin_ref[rows, :]
                    dq_ref[rows, lanes] = dq2.astype(BF16)
        for hl in range(8):
            dsk_ref[0, hl:hl + 1, :] += jnp.broadcast_to(dsk[hl], (1, LANES))

    wide = lambda cc: pl.BlockSpec((bq, 512), lambda c, b, i, cc=cc: (b * nq + i, cc + c))
    cur = lambda cc: pl.BlockSpec((bq, LANES), lambda c, b, i, cc=cc: (b * nq + i, cc + c))
    prev = lambda cc: pl.BlockSpec((LCH, LANES), lambda c, b, i, cc=cc: (b * nrow + jnp.maximum(i * nsub - 1, 0), cc + c))
    pos = pl.BlockSpec((bq, LANES), lambda c, b, i: (i, 0))
    return pl.pallas_call(
        body, name=f"swa_bwd_dq_{li}", grid=(2, B, nq),
        in_specs=[pl.BlockSpec(memory_space=pltpu.SMEM), wide(qc), wide(0), wide(0),
                  pl.BlockSpec((1, 8, bq, 1), lambda c, b, i: (b, c, i, 0)),
                  prev(kc), cur(kc), prev(vc), cur(vc), pos, pos],
        out_specs=[wide(0), pl.BlockSpec((1, 8, LANES), lambda c, b, i: (c, 0, 0))],
        out_shape=[jax.ShapeDtypeStruct((T, D), BF16), jax.ShapeDtypeStruct((2, 8, LANES), F32)],
        compiler_params=_cparams(("arbitrary", "arbitrary", "arbitrary"), VMEM_LIMIT),
    )(sinks, proj, do, o, lse, proj, proj, proj, proj, cos128, sin128)


def _swa_bwd_dkv(proj, do, o, lse, cos128, sin128, S, li):
    T = proj.shape[0]
    B = T // S
    bk, nk, nsub = _swa_blocks(S)
    nrow = S // LCH
    qc, kc, vc = OFF_BQ // 512, OFF_BK // LANES, OFF_BV // LANES

    def body(q_ref, qn_ref, do_ref, don_ref, o_ref, on_ref, lse_ref, lsen_ref, k_ref, v_ref, cos_ref, sin_ref,
             dk_ref, dv_ref):
        j = pl.program_id(2)
        m0 = _lane_iota() < HD
        has_next = (j < nk - 1).astype(F32)
        kf = k_ref[...].astype(F32)
        vf = v_ref[...].astype(F32)
        kd = (_dup_head(kf, 0), _dup_head(kf, 1))
        vd = (_dup_head(vf, 0), _dup_head(vf, 1))
        lane = _lane_iota()

        def stat_rows(lse_r, do_r, o_r, rows, scale):
            a_lse = jnp.zeros((rows, LANES), F32)
            a_del = jnp.zeros((rows, LANES), F32)
            for ch in range(4):
                lanes = slice(LANES * ch, LANES * (ch + 1))
                prod = do_r[:, lanes].astype(F32) * o_r[:, lanes].astype(F32)
                for hh in range(2):
                    h = 2 * ch + hh
                    a_lse = jnp.where(lane == h, lse_r[0, h], a_lse)
                    a_del = jnp.where(lane == h, _head_sum(prod, hh), a_del)
            if scale is not None:
                a_del = a_del * scale
            return a_lse.T, a_del.T

        lse_t, del_t = stat_rows(lse_ref, do_ref, o_ref, bk, None)
        lsen_t, deln_t = stat_rows(lsen_ref, don_ref, on_ref, LCH, has_next)
        r_ = lax.broadcasted_iota(jnp.int32, (LCH, LCH), 0)
        c_ = lax.broadcasted_iota(jnp.int32, (LCH, LCH), 1)
        masks4 = (jnp.concatenate([r_ <= c_] * 4, axis=1), jnp.concatenate([r_ > c_] * 4, axis=1))
        for kr in range(nsub):
            krows = slice(LCH * kr, LCH * (kr + 1))
            dk = jnp.zeros((LCH, LANES), F32)
            dv = jnp.zeros((LCH, LANES), F32)
            for dq_blk in range(2):
                rq = kr + dq_blk
                nxt = rq == nsub
                qrows = slice(0, LCH) if nxt else slice(LCH * rq, LCH * (rq + 1))
                qr, dor = (qn_ref, don_ref) if nxt else (q_ref, do_ref)
                lt, dt_ = (lsen_t, deln_t) if nxt else (lse_t, del_t)
                for kvl in range(2):
                    qs = _stack_heads(qr, qrows, kvl)
                    dos = _stack_heads(dor, qrows, kvl)
                    if nxt:
                        dos = (dos.astype(F32) * has_next).astype(BF16)
                    lse_row = jnp.concatenate([lt[4 * kvl + t:4 * kvl + t + 1, qrows] for t in range(4)], axis=1)
                    del_row = jnp.concatenate([dt_[4 * kvl + t:4 * kvl + t + 1, qrows] for t in range(4)], axis=1)
                    st = jnp.where(masks4[dq_blk], _dot_nt(kd[kvl][krows], qs) * SCALE, NEG)
                    pt = jnp.exp(st - lse_row)
                    dst = pt * (_dot_nt(vd[kvl][krows], dos) - del_row)
                    dvc = _dot(pt.astype(BF16), dos)
                    dkc = _dot(dst.astype(BF16), qs) * SCALE
                    own = m0 if kvl == 0 else jnp.logical_not(m0)
                    dv = dv + jnp.where(own, dvc + pltpu.roll(dvc, HD, 1), 0.0)
                    dk = dk + jnp.where(own, dkc + pltpu.roll(dkc, HD, 1), 0.0)
            dk = dk * cos_ref[krows, :] - _rot_half(dk) * sin_ref[krows, :]
            dk_ref[krows, :] = dk.astype(BF16)
            dv_ref[krows, :] = dv.astype(BF16)

    wide = lambda cc: pl.BlockSpec((bk, 512), lambda c, b, j, cc=cc: (b * nk + j, cc + c))
    nxt = lambda cc: pl.BlockSpec((LCH, 512), lambda c, b, j, cc=cc: (b * nrow + jnp.minimum((j + 1) * nsub, nrow - 1), cc + c))
    cur = lambda cc: pl.BlockSpec((bk, LANES), lambda c, b, j, cc=cc: (b * nk + j, cc + c))
    pos = pl.BlockSpec((bk, LANES), lambda c, b, j: (j, 0))
    return pl.pallas_call(
        body, name=f"swa_bwd_dkv_{li}", grid=(2, B, nk),
        in_specs=[wide(qc), nxt(qc), wide(0), nxt(0), wide(0), nxt(0),
                  pl.BlockSpec((1, 8, bk, 1), lambda c, b, j: (b, c, j, 0)),
                  pl.BlockSpec((1, 8, LCH, 1), lambda c, b, j: (b, c, jnp.minimum((j + 1) * nsub, nrow - 1), 0)),
                  cur(kc), cur(vc), pos, pos],
        out_specs=[cur(0), cur(0)],
        out_shape=[jax.ShapeDtypeStruct((T, 2 * LANES), BF16), jax.ShapeDtypeStruct((T, 2 * LANES), BF16)],
        compiler_params=_cparams(("parallel", "parallel", "parallel"), VMEM_LIMIT),
    )(proj, proj, do, do, o, o, lse, lse, proj, proj, cos128, sin128)


HALO = 16


def _shift_matrices():
    r = lax.broadcasted_iota(jnp.int32, (3 * LCH, LCH + HALO), 0)
    c = lax.broadcasted_iota(jnp.int32, (3 * LCH, LCH + HALO), 1)
    t, d = r % LCH, r // LCH + 1
    return (c == HALO + t - d).astype(BF16), (c == t + d).astype(BF16)


def _ssm_chunk_pre(prev16, cur16, first, sdn_ref, cw_ref, cb_ref, ps, dtb, alog):
    ext16 = jnp.concatenate([jnp.where(first, jnp.zeros_like(prev16), prev16), cur16], axis=0)
    sh = _dot(sdn_ref[...], ext16)
    pre = cb_ref[...] + cw_ref[3:4, :] * cur16.astype(F32)
    for d in range(1, 4):
        pre = pre + cw_ref[3 - d:4 - d, :] * sh[LCH * (d - 1):LCH * d]
    sg = _sigmoid(pre)
    dt = _softplus(ps + dtb)
    a = -jnp.exp(alog)
    r = lax.broadcasted_iota(jnp.int32, (LCH, LCH), 0)
    c = lax.broadcasted_iota(jnp.int32, (LCH, LCH), 1)
    acum = _dot_hi((r >= c).astype(F32), dt * a)
    return pre, sg, dt, a, acum, sh


def _expand_matrix():
    r = lax.broadcasted_iota(jnp.int32, (3 * LANES, D), 0)
    c = lax.broadcasted_iota(jnp.int32, (3 * LANES, D), 1)
    return ((r % LANES) == c // HD).astype(BF16)


def _expand_heads(v, ex_ref):
    return _dot(jnp.concatenate(_split3(v), axis=1).astype(BF16), ex_ref[...])


def _decay(acum, acum_t, h):
    r = lax.broadcasted_iota(jnp.int32, (LCH, LCH), 0)
    c = lax.broadcasted_iota(jnp.int32, (LCH, LCH), 1)
    causal = r >= c
    seg = acum[:, h:h + 1] - acum_t[h:h + 1, :]
    return jnp.where(causal, jnp.exp(jnp.where(causal, seg, 0.0)), 0.0)


def _ssm_pair_fwd(p, x, dt_x, acum, acum_t, e_x, w_x, cd, cb_g, b_g, c_g, hprev, dsk_ref):
    m0 = _lane_iota() < HD
    lanes = slice(LANES * p, LANES * (p + 1))
    x2 = x[:, lanes]
    dt2 = dt_x[:, lanes]
    xdt2 = x2 * dt2
    xdtb = xdt2.astype(BF16)
    lms, ms, yds = [], [], []
    for hh in range(2):
        lm = _decay(acum, acum_t, 2 * p + hh)
        mm = cb_g * lm
        lms.append(lm)
        ms.append(mm)
        yds.append(_dot(mm.astype(BF16), xdtb))
    yd2 = jnp.where(m0, yds[0], yds[1])
    w2 = w_x[:, lanes]
    xw = (xdt2 * w2).astype(BF16)
    s2 = _dot_tn(xw, b_g)
    z2 = _dot_nt(c_g, hprev.astype(BF16))
    e2 = e_x[:, lanes]
    rowsel = lax.broadcasted_iota(jnp.int32, (LANES, 1), 0) < HD
    cdcol = jnp.where(rowsel, cd[:, 2 * p:2 * p + 1], cd[:, 2 * p + 1:2 * p + 2])
    y2 = yd2 + z2 * e2 + dsk_ref[:, lanes] * x2
    return dict(x2=x2, dt2=dt2, xdt2=xdt2, xdtb=xdtb, lms=lms, ms=ms, yd2=yd2, w2=w2, xw=xw, s2=s2, z2=z2, e2=e2,
                cdcol=cdcol, y2=y2)


def _ssm_specs(S, rev):
    nc = S // LCH
    ch = (lambda c: nc - 1 - c) if rev else (lambda c: c)
    prev = pl.BlockSpec((HALO, 2 * D), lambda b, c: (jnp.maximum(b * (S // HALO) + ch(c) * (LCH // HALO) - 1, 0), 0))
    cur = pl.BlockSpec((LCH, 2 * D), lambda b, c: (b * nc + ch(c), 0))
    zed = pl.BlockSpec((LCH, D), lambda b, c: (b * nc + ch(c), OFF_AZ // D))
    row = pl.BlockSpec((LCH, D), lambda b, c: (b * nc + ch(c), 0))
    psb = pl.BlockSpec((LCH, LANES), lambda b, c: (b * nc + ch(c), 0))
    hpb = pl.BlockSpec((1, 1, NH // 2, LANES, NST), lambda b, c: (b, ch(c), 0, 0, 0))
    const = lambda r, w: pl.BlockSpec((r, w), lambda b, c: (0, 0))
    return nc, prev, cur, zed, row, psb, hpb, const


def _ssm_fwd(proj, ps, cw, cb, dtb, alog, dsk, nw, S, li):
    T = proj.shape[0]
    B = T // S
    nc, prev, cur, zed, row, psb, hpb, const = _ssm_specs(S, False)

    def body(prev_ref, cur_ref, z_ref, ps_ref, sdn_ref, ex_ref, cw_ref, cb_ref, dtb_ref, alog_ref, dsk_ref, nw_ref,
             ya_ref, hp_ref, h_scr):
        c = pl.program_id(1)

        @pl.when(c == 0)
        def _():
            h_scr[...] = jnp.zeros_like(h_scr)

        pre, sg, dt, a, acum, _ = _ssm_chunk_pre(prev_ref[...], cur_ref[...], c == 0, sdn_ref, cw_ref, cb_ref,
                                                 ps_ref[...], dtb_ref[...], alog_ref[...])
        act = pre * sg
        acum_t = acum.T
        last = acum[LCH - 1:LCH, :]
        cd = jnp.exp(last)
        dt, e_all, w_all = (_expand_heads(v, ex_ref) for v in (dt, jnp.exp(acum), jnp.exp(last - acum)))
        x = act[:, :D]
        for g in range(NGRP):
            b_g = act[:, D + NST * g:D + NST * (g + 1)].astype(BF16)
            c_g = act[:, D + NGRP * NST + NST * g:D + NGRP * NST + NST * (g + 1)].astype(BF16)
            cb_g = _dot_nt(c_g, b_g)
            ygs = []
            for p in (2 * g, 2 * g + 1):
                hprev = h_scr[p]
                hp_ref[0, 0, p] = hprev
                f = _ssm_pair_fwd(p, x, dt, acum, acum_t, e_all, w_all, cd, cb_g, b_g, c_g, hprev, dsk_ref)
                h_scr[p] = hprev * f["cdcol"] + f["s2"]
                z2 = z_ref[:, LANES * p:LANES * (p + 1)].astype(F32)
                ygs.append(f["y2"] * z2 * _sigmoid(z2))
            yg = jnp.concatenate(ygs, axis=1)
            r = lax.rsqrt(jnp.mean(yg * yg, axis=1, keepdims=True) + EPS)
            ya_ref[:, 2 * LANES * g:2 * LANES * (g + 1)] = (yg * r * nw_ref[:, 2 * LANES * g:2 * LANES * (g + 1)]).astype(BF16)

    return pl.pallas_call(
        body, name=f"ssm_fwd_{li}", grid=(B, nc),
        in_specs=[prev, cur, zed, psb, const(3 * LCH, LCH + HALO), const(3 * LANES, D), const(4, 2 * D),
                  const(1, 2 * D), const(1, LANES), const(1, LANES), const(1, D), const(1, D)],
        out_specs=[row, hpb],
        out_shape=[jax.ShapeDtypeStruct((T, D), BF16), jax.ShapeDtypeStruct((B, nc, NH // 2, LANES, NST), F32)],
        scratch_shapes=[pltpu.VMEM((NH // 2, LANES, NST), F32)],
        compiler_params=_cparams(("arbitrary", "arbitrary"), VMEM_LIMIT),
    )(proj, proj, proj, ps, _shift_matrices()[0], _expand_matrix(), cw, cb, dtb, alog, dsk, nw)


def _ssm_bwd(proj, ps, hp, dya, cw, cb, dtb, alog, dsk, nw, S, li, comm=None):
    T = proj.shape[0]
    B = T // S
    nc, prev, cur, zed, row, psb, hpb, const = _ssm_specs(S, True)

    def body(prev_ref, cur_ref, z_ref, ps_ref, hp_ref, dy_ref, sdn_ref, sup_ref, ex_ref, cw_ref, cb_ref, dtb_ref,
             alog_ref, dsk_ref, nw_ref, dxbc_ref, dz_ref, dps_ref, pgw_ref, pg1_ref, pgh_ref, dh_scr, dhead, dact):
        b, cc = pl.program_id(0), pl.program_id(1)
        c = nc - 1 - cc

        @pl.when(jnp.logical_and(b == 0, cc == 0))
        def _():
            pgw_ref[...] = jnp.zeros_like(pgw_ref)
            pg1_ref[...] = jnp.zeros_like(pg1_ref)
            pgh_ref[...] = jnp.zeros_like(pgh_ref)

        @pl.when(cc == 0)
        def _():
            dh_scr[...] = jnp.zeros_like(dh_scr)
            dhead[...] = jnp.zeros_like(dhead)

        psv = ps_ref[...]
        cur16 = cur_ref[...]
        pre, sg, dt, a, acum, sh = _ssm_chunk_pre(prev_ref[...], cur16, c == 0, sdn_ref, cw_ref, cb_ref, psv,
                                                  dtb_ref[...], alog_ref[...])
        act = pre * sg
        acum_t = acum.T
        last = acum[LCH - 1:LCH, :]
        w_all = jnp.exp(last - acum)
        cd = jnp.exp(last)
        dt_x, e_x, w_x = (_expand_heads(v, ex_ref) for v in (dt, jnp.exp(acum), w_all))
        x = act[:, :D]
        lane = _lane_iota()
        m0 = lane < HD
        head_row = lax.broadcasted_iota(jnp.int32, (LANES, 1), 0)
        rowsel = head_row < HD
        is_last_row = lax.broadcasted_iota(jnp.int32, (LCH, 1), 0) == LCH - 1
        dacum_all = jnp.zeros((LCH, LANES), F32)
        dacum_t = jnp.zeros((LANES, LCH), F32)
        ddt_all = jnp.zeros((LCH, LANES), F32)
        dd_row = jnp.zeros((1, LANES), F32)
        for g in range(NGRP):
            b_g = act[:, D + NST * g:D + NST * (g + 1)].astype(BF16)
            c_g = act[:, D + NGRP * NST + NST * g:D + NGRP * NST + NST * (g + 1)].astype(BF16)
            cb_g = _dot_nt(c_g, b_g)
            pairs = (2 * g, 2 * g + 1)
            fs, hps, zs, ygs = [], [], [], []
            for p in pairs:
                hprev = hp_ref[0, 0, p]
                f = _ssm_pair_fwd(p, x, dt_x, acum, acum_t, e_x, w_x, cd, cb_g, b_g, c_g, hprev, dsk_ref)
                z2 = z_ref[:, LANES * p:LANES * (p + 1)].astype(F32)
                fs.append(f)
                hps.append(hprev)
                zs.append(z2)
                ygs.append(f["y2"] * z2 * _sigmoid(z2))
            gl = slice(2 * LANES * g, 2 * LANES * (g + 1))
            yg = jnp.concatenate(ygs, axis=1)
            r = lax.rsqrt(jnp.mean(yg * yg, axis=1, keepdims=True) + EPS)
            dyn = dy_ref[:, gl].astype(F32)
            gg = dyn * nw_ref[:, gl]
            dyg = r * gg - yg * (r * r * r) * jnp.mean(gg * yg, axis=1, keepdims=True)
            pg1_ref[0:1, gl] += jnp.sum(dyn * yg * r, axis=0, keepdims=True)
            dg_g = jnp.zeros((LCH, LCH), F32)
            db_g = jnp.zeros((LCH, NST), F32)
            dc_g = jnp.zeros((LCH, NST), F32)
            for idx, p in enumerate(pairs):
                f, hprev, z2 = fs[idx], hps[idx], zs[idx]
                lanes = slice(LANES * p, LANES * (p + 1))
                dyg2 = dyg[:, LANES * idx:LANES * (idx + 1)]
                sgz = _sigmoid(z2)
                dy2 = dyg2 * z2 * sgz
                dz_ref[:, lanes] = (dyg2 * f["y2"] * sgz * (1.0 + z2 * (1.0 - sgz))).astype(BF16)
                x2, dt2, xdt2, xdtb, w2, e2, z2m = f["x2"], f["dt2"], f["xdt2"], f["xdtb"], f["w2"], f["e2"], f["z2"]
                dx2 = dsk_ref[:, lanes] * dy2
                dyx = dy2 * x2
                dxdt2 = jnp.zeros((LCH, LANES), F32)
                diag_cols = []
                for hh in range(2):
                    sel = m0 if hh == 0 else jnp.logical_not(m0)
                    dyb = jnp.where(sel, dy2, 0.0).astype(BF16)
                    dm = _dot_nt(dyb, xdtb)
                    dg_g = dg_g + dm * f["lms"][hh]
                    dxdt2 = dxdt2 + _dot_tn(f["ms"][hh].astype(BF16), dyb)
                    em = dm * f["ms"][hh]
                    diag_cols.append(jnp.sum(em, axis=1, keepdims=True))
                    dacum_t = dacum_t - jnp.where(head_row == 2 * p + hh, jnp.sum(em, axis=0, keepdims=True), 0.0)
                dz2m = dy2 * e2
                t_off = dz2m * z2m
                dc_g = dc_g + _dot(dz2m.astype(BF16), hprev.astype(BF16))
                dhprev = _dot_tn(dz2m.astype(BF16), c_g)
                dhn = dh_scr[p]
                dhnb = dhn.astype(BF16)
                dhprev = dhprev + dhn * f["cdcol"]
                t_h = dhn * hprev
                dxw2 = _dot_nt(b_g, dhnb)
                db_g = db_g + _dot(f["xw"], dhnb)
                dxdt2 = dxdt2 + dxw2 * w2
                t_w = dxw2 * xdt2
                dx2 = dx2 + dxdt2 * dt2
                t_dt = dxdt2 * x2
                for hh in range(2):
                    h = 2 * p + hh
                    onehot = (lane == h).astype(F32)
                    w_col = w_all[:, h:h + 1]
                    dw_col = _head_sum(t_w, hh) * w_col
                    rs = rowsel if hh == 0 else jnp.logical_not(rowsel)
                    dlast = (jnp.sum(jnp.where(rs, t_h, 0.0), keepdims=True) * cd[:, h:h + 1]
                             + jnp.sum(dw_col, keepdims=True))
                    dacum_col = diag_cols[hh] + _head_sum(t_off, hh) - dw_col + jnp.where(is_last_row, dlast, 0.0)
                    dacum_all = dacum_all + dacum_col * onehot
                    ddt_all = ddt_all + _head_sum(t_dt, hh) * onehot
                    sel = m0 if hh == 0 else jnp.logical_not(m0)
                    dd_row = dd_row + jnp.sum(jnp.where(sel, dyx, 0.0), keepdims=True) * onehot
                dh_scr[p] = dhprev
                dact[:, lanes] = dx2
            dgb = dg_g.astype(BF16)
            dc_g = dc_g + _dot(dgb, b_g)
            db_g = db_g + _dot_tn(dgb, c_g)
            dact[:, D + NST * g:D + NST * (g + 1)] = db_g
            dact[:, D + NGRP * NST + NST * g:D + NGRP * NST + NST * (g + 1)] = dc_g
        rr = lax.broadcasted_iota(jnp.int32, (LCH, LCH), 0)
        cc2 = lax.broadcasted_iota(jnp.int32, (LCH, LCH), 1)
        dadt = _dot_hi((cc2 >= rr).astype(F32), dacum_all + dacum_t.T)
        ddt_all = ddt_all + dadt * a
        heads = lane < NH
        da = jnp.sum(dadt * dt, axis=0, keepdims=True)
        dr = jnp.where(heads, ddt_all * _sigmoid(psv + dtb_ref[...]), 0.0)
        dps_ref[...] = dr
        pgh_ref[0:1, :] += jnp.sum(dr, axis=0, keepdims=True)
        pgh_ref[1:2, :] += jnp.where(heads, da * a, 0.0)
        pgh_ref[2:3, :] += dd_row
        dpre = dact[...] * sg * (1.0 + pre * (1.0 - sg))
        extd = jnp.concatenate([dpre, dhead[...]], axis=0)
        hi = extd.astype(BF16)
        lo = (extd - hi.astype(F32)).astype(BF16)
        up = _dot(sup_ref[...], hi) + _dot(sup_ref[...], lo)
        du = cw_ref[3:4, :] * dpre
        pgw_ref[3:4, :] += jnp.sum(dpre * cur16.astype(F32), axis=0, keepdims=True)
        for d in range(1, 4):
            du = du + cw_ref[3 - d:4 - d, :] * up[LCH * (d - 1):LCH * d]
            pgw_ref[3 - d:4 - d, :] += jnp.sum(dpre * sh[LCH * (d - 1):LCH * d], axis=0, keepdims=True)
        pgw_ref[4:5, :] += jnp.sum(dpre, axis=0, keepdims=True)
        dxbc_ref[...] = du.astype(BF16)
        dhead[...] = dpre[0:HALO, :]

    xbc_out = pl.BlockSpec((LCH, 2 * D), lambda b, c: (b * nc + nc - 1 - c, 0))
    acc = lambda w: pl.BlockSpec((8, w), lambda b, c: (0, 0))
    sdn, sup = _shift_matrices()
    return _hosted_call(
        body, comm, f"ssm_bwd_{li}", (B, nc),
        in_specs=[prev, cur, zed, psb, hpb, row, const(3 * LCH, LCH + HALO), const(3 * LCH, LCH + HALO),
                  const(3 * LANES, D), const(4, 2 * D), const(1, 2 * D), const(1, LANES), const(1, LANES),
                  const(1, D), const(1, D)],
        out_specs=[xbc_out, row, psb, acc(2 * D), acc(D), acc(LANES)],
        out_shape=[jax.ShapeDtypeStruct((T, 2 * D), BF16), jax.ShapeDtypeStruct((T, D), BF16),
                   jax.ShapeDtypeStruct((T, LANES), F32), jax.ShapeDtypeStruct((8, 2 * D), F32),
                   jax.ShapeDtypeStruct((8, D), F32), jax.ShapeDtypeStruct((8, LANES), F32)],
        scratch=[pltpu.VMEM((NH // 2, LANES, NST), F32), pltpu.VMEM((HALO, 2 * D), F32),
                 pltpu.VMEM((LCH, 2 * D), F32)],
        dims=("arbitrary", "arbitrary"),
        operands=(proj, proj, proj, ps, hp, dya, sdn, sup, _expand_matrix(), cw, cb, dtb, alog, dsk, nw))


def _lane_row(v, offset):
    return jnp.pad(v.astype(F32), (offset, LANES - offset - v.shape[0]))[None]


def _pack_rows(arrays):
    parts = []
    for a in arrays:
        flat = a.reshape(-1).astype(F32)
        pad = (-flat.shape[0]) % LANES
        parts.append(jnp.pad(flat, (0, pad)))
    flat = jnp.concatenate(parts)
    pad = (-flat.shape[0]) % (8 * LANES)
    return jnp.pad(flat, (0, pad)).reshape(-1, LANES)


def _unpack_rows(pack, shapes):
    flat = pack.reshape(-1)
    out, pos = [], 0
    for shp in shapes:
        n = math.prod(shp)
        out.append(flat[pos:pos + n].reshape(shp))
        pos += n + (-n) % LANES
    return out


def _split_w_in(blocks):
    def cols(a, b):
        out = []
        for d in range(NDEV):
            lo, hi = max(a, d * NSH), min(b, (d + 1) * NSH)
            if lo < hi:
                out.append(blocks[d, :, lo - d * NSH:hi - d * NSH])
        return out

    main = jnp.concatenate(cols(0, 3072) + cols(3088, 4112) + cols(4624, 5648) + cols(5648, 8720)
                           + cols(8736, 12832) + cols(4112, 4624), axis=1)
    small = jnp.concatenate(cols(3072, 3088) + cols(8720, 8736) + [jnp.zeros((D, LANES - 2 * NH), blocks.dtype)],
                            axis=1)
    return main, small


def _w_in_blocks(dw, ds, r0, r1):
    xbc, az, bq, bz, cq, ck, cv, cz, gates, bk, bv = dw
    order = [xbc, az, ds[:, 0:NH], bq, bk, bv, bz, cq, ck, cv, ds[:, NH:2 * NH], cz, gates]
    blocks, pos = [[] for _ in range(NDEV)], 0
    for seg in order:
        w = seg.shape[1]
        for d in range(NDEV):
            lo, hi = max(pos, d * NSH), min(pos + w, (d + 1) * NSH)
            if lo < hi:
                blocks[d].append(seg[r0:r1, lo - pos:hi - pos])
        pos += w
    return jnp.stack([jnp.concatenate(b, axis=1) for b in blocks])


def kernel(x, norm_w, w_in, conv_w, conv_b, dt_bias, a_log, d_skip, ssm_norm_w, sinks, f_bias, gate_bias, w_proj, w_out, final_norm_w, loss_target, m_norm_w, m_w_in, m_conv_w, m_conv_b, m_dt_bias, m_a_log, m_d_skip, m_ssm_norm_w, m_sinks, m_f_bias, m_gate_bias, m_w_proj, m_w_out, m_final_norm_w, v_norm_w, v_w_in, v_conv_w, v_conv_b, v_dt_bias, v_a_log, v_d_skip, v_ssm_norm_w, v_sinks, v_f_bias, v_gate_bias, v_w_proj, v_w_out, v_final_norm_w):
    Bl, S, _ = x.shape
    T = Bl * S
    depth = norm_w.shape[0]
    me = 4 * lax.axis_index("x") + 2 * lax.axis_index("y") + lax.axis_index("c")
    csh, gsh = conv_w.shape[2], gate_bias.shape[2]

    def gather_plan(l):
        small = jnp.concatenate([conv_w[l].reshape(-1), gate_bias[l].reshape(-1)]).reshape(-1, LANES)
        return _Comm("gather", [w_in[l].astype(BF16), w_proj[l].astype(BF16), w_out[l].astype(BF16), small])

    def unpack_weights(res):
        g_win, g_wp, g_wo, g_small = res
        flat = g_small.reshape(NDEV, -1)
        return (_split_w_in(g_win),
                g_wp.transpose(1, 0, 2, 3).reshape(3, D, D),
                g_wo.reshape(D, D),
                flat[:, :4 * csh].reshape(NDEV, 4, csh).transpose(1, 0, 2).reshape(4, 2 * D),
                flat[:, 4 * csh:].reshape(NDEV, 3, gsh).transpose(1, 0, 2).reshape(3, D))

    def scatter_plan(gw_in_blocks=None, gw_p=None, gw_o=None):
        arrays = [] if gw_in_blocks is None else [gw_in_blocks]
        if gw_p is not None:
            arrays += [gw_p.astype(BF16).reshape(3, NDEV, D // NDEV, D).transpose(1, 0, 2, 3),
                       gw_o.astype(BF16).reshape(NDEV, D // NDEV, D)]
        return _Comm("scatter", arrays)

    pos = jnp.arange(S, dtype=F32)
    inv_freq = ROPE_THETA ** (-jnp.arange(0, HD, 2, dtype=F32) / HD)
    ang = pos[:, None] * inv_freq[None, :]
    cos128 = jnp.tile(jnp.cos(ang), (1, 4))
    sign = jnp.where((jnp.arange(LANES) % HD) < HD // 2, -1.0, 1.0).astype(F32)
    sin128 = jnp.tile(jnp.sin(ang), (1, 4)) * sign[None, :]

    x2 = x.reshape(T, D)
    tgt2 = loss_target.reshape(T, D)
    fox_bq = _fox_blocks(S)[0]

    saved = []
    xcur = x2
    weights = [None] * depth
    weights[0] = unpack_weights(_gather_two_level(gather_plan(0).arrays, "gather_weights_0"))
    for l in range(depth):
        (wmain, wsmall), wp_l, wo_l, cw_l, gb_l = weights[l]
        proj, ps, h_t = _inproj_fwd(xcur, norm_w[l][None], wmain, wsmall, cos128, sin128, S, l)
        dtb = _lane_row(dt_bias[l], 0)
        alog = _lane_row(a_log[l], 0)
        fb = _lane_row(f_bias[l], NH)
        dsk = jnp.repeat(d_skip[l], HD)[None]
        ya, hp = _ssm_fwd(proj, ps, cw_l, conv_b[l][None], dtb, alog, dsk, ssm_norm_w[l][None], S, l)
        yb, ob, lse_b = _swa_fwd(proj, sinks[l], S, l)
        cum = _fox_cum(ps, fb, S, l)
        cumh = cum[:, NH:2 * NH].reshape(Bl, S, NH).transpose(0, 2, 1)
        cum_row = cumh.reshape(Bl, NH, S // fox_bq, 1, fox_bq)
        comm = gather_plan(l + 1) if l + 1 < depth else None
        res = _fox_fwd(proj, cum_row, S, l, comm)
        yc, oc, lse_c = res[:3]
        if comm is not None:
            weights[l + 1] = unpack_weights(res[3:])
        xnext, br, y_t = _merge_fwd(ya, yb, yc, proj, gb_l, wp_l, wo_l, xcur, l)
        saved.append(dict(x=xcur, wmain=wmain, wsmall=wsmall, proj=proj, ps=ps, h_t=h_t, dtb=dtb, alog=alog, fb=fb,
                          dsk=dsk, hp=hp, ob=ob, lse_b=lse_b, cum_row=cum_row, oc=oc, lse_c=lse_c, br=br, y_t=y_t))
        xcur = xnext

    dx, dx16, st = _final_loss(xcur, tgt2, final_norm_w[None])
    loss_part = st[2, 0]
    g_final = st[0]

    gsm = {k: [None] * depth for k in ("norm_w", "conv_w", "conv_b", "dt_bias", "a_log", "d_skip", "ssm_norm_w",
                                      "sinks", "f_bias", "gate_bias")}
    parts = [None] * depth
    pending = None
    for l in reversed(range(depth)):
        sv = saved[l]
        proj, ps = sv["proj"], sv["ps"]
        _, wp_l, wo_l, cw_l, gb_l = weights[l]
        dbr, dgates, merged_t, dgb, dy_a, do_b, dbz, do_c, dcz = _merge_bwd(dx16, wo_l, wp_l, sv["br"], proj, gb_l,
                                                                            sv["ob"], sv["oc"], l)
        g_wo = _matmul(merged_t, dx16, BF16, f"dwout_{l}")
        g_wp = _matmul_batched(sv["y_t"], dbr, BF16, f"dwproj_{l}")
        gsm["gate_bias"][l] = dgb[0:3]
        hosted = ([] if pending is None else pending.arrays) + (scatter_plan(None, g_wp, g_wo).arrays if l == 0 else [])
        res = _ssm_bwd(proj, ps, sv["hp"], dy_a, cw_l, conv_b[l][None], sv["dtb"], sv["alog"], sv["dsk"],
                       ssm_norm_w[l][None], S, l, _Comm("scatter", hosted) if hosted else None)
        dxbc, daz, dps_a, pgw, pg1, pgh = res[:6]
        if pending is not None:
            parts[l + 1] = res[6:9]
        if l == 0:
            parts_po = res[len(res) - 2:]
        gsm["conv_w"][l], gsm["conv_b"][l] = pgw[0:4], pgw[4]
        gsm["ssm_norm_w"][l] = pg1[0]
        gsm["dt_bias"][l], gsm["a_log"][l], gsm["d_skip"][l] = pgh[0, :NH], pgh[1, :NH], pgh[2, :NH]
        dq_b, dsk_b = _swa_bwd_dq(proj, do_b, sv["ob"], sv["lse_b"], sinks[l], cos128, sin128, S, l)
        dk_b, dv_b = _swa_bwd_dkv(proj, do_b, sv["ob"], sv["lse_b"], cos128, sin128, S, l)
        gsm["sinks"][l] = dsk_b[:, :, 0].reshape(NH)
        dq_c, dk_c, dv_c, dcum_k, dcum_q = _fox_bwd(proj, do_c, sv["oc"], sv["cum_row"], sv["lse_c"], S, l)
        dcum_tm = (dcum_k.reshape(Bl, NH, S) + dcum_q.reshape(Bl, NH, S)).transpose(0, 2, 1).reshape(T, NH)
        dcum_pad = jnp.pad(dcum_tm, ((0, 0), (NH, LANES - 2 * NH)))
        df, dfb = _fox_cum_bwd(dcum_pad, ps, sv["fb"], S, l)
        gsm["f_bias"][l] = dfb[0, NH:2 * NH]
        dps16 = (dps_a + df).astype(BF16)
        dkv_b = jnp.concatenate([dk_b, dv_b], axis=1)
        pieces = (dxbc, daz, dq_b, dbz, dq_c, dk_c, dv_c, dcz, dgates, dkv_b)
        dw_pieces = [_matmul(sv["h_t"], pc, BF16, f"dwin_{l}_{i}", tk=2048) for i, pc in enumerate(pieces)]
        dw_pieces = dw_pieces[:-1] + [dw_pieces[-1][:, :2 * LANES], dw_pieces[-1][:, 2 * LANES:]]
        dws = _matmul(sv["h_t"], dps16, BF16, f"dwin_small_{l}")
        if l == 0:
            plans = [scatter_plan(_w_in_blocks(dw_pieces, dws, r0, r1)) for r0, r1 in ROW_CHUNKS]
        else:
            plans, pending = [None] * len(ROW_CHUNKS), scatter_plan(_w_in_blocks(dw_pieces, dws, 0, D), g_wp, g_wo)
        res1 = _inproj_bwd_dx([(dxbc, OFF_XBC), (daz, OFF_AZ), (dq_b, OFF_BQ), (dbz, OFF_BZ)], sv["wmain"],
                              ("narrow", dps16, sv["wsmall"]), None, f"inproj_bwd_dh1_{l}", plans[0])
        res2 = _inproj_bwd_dx([(dq_c, OFF_CQ), (dk_c, OFF_CK), (dv_c, OFF_CV), (dcz, OFF_CZ)], sv["wmain"],
                              ("acc", res1[0]), None, f"inproj_bwd_dh2_{l}", plans[1])
        dx, dx16, dnw = _inproj_bwd_dx([(dgates, OFF_G), (dkv_b, OFF_BK)], sv["wmain"], ("acc", res2[0]),
                                       (sv["x"], norm_w[l][None], dx), f"inproj_bwd_dx_{l}")
        if l == 0:
            parts[0] = [jnp.concatenate([res1[1], res2[1]], axis=1), *parts_po]
        gsm["norm_w"][l] = dnw[0]

    big = {}
    for idx, (name, w, m, v) in enumerate((("w_in", w_in, m_w_in, v_w_in), ("w_proj", w_proj, m_w_proj, v_w_proj),
                                          ("w_out", w_out, m_w_out, v_w_out))):
        cols = w.shape[-1]
        res = _sum_adamw([parts[l][idx].reshape(NDEV, -1, cols) for l in range(depth)], w.reshape(depth, -1, cols),
                         m.reshape(depth, -1, cols), v.reshape(depth, -1, cols), f"adamw_{name}")
        big[name] = [r.reshape(w.shape) for r in res]

    small_names = ("norm_w", "conv_b", "dt_bias", "a_log", "d_skip", "ssm_norm_w", "sinks", "f_bias")
    small_parts = [jnp.stack(gsm[k]) for k in small_names] + [g_final, jnp.stack(gsm["conv_w"]),
                                                              jnp.stack(gsm["gate_bias"]), loss_part.reshape(1)]
    shapes = [a.shape for a in small_parts]
    summed = _unpack_rows(_all_reduce_small(_pack_rows(small_parts)), shapes)
    g_small = dict(zip(small_names, summed[:len(small_names)]))
    g_small["final_norm_w"] = summed[len(small_names)]
    g_small["conv_w"] = lax.dynamic_slice_in_dim(summed[len(small_names) + 1], me * csh, csh, axis=2)
    g_small["gate_bias"] = lax.dynamic_slice_in_dim(summed[len(small_names) + 2], me * gsh, gsh, axis=2)
    loss = summed[len(small_names) + 3][0]

    ws = dict(norm_w=norm_w, conv_w=conv_w, conv_b=conv_b, dt_bias=dt_bias, a_log=a_log, d_skip=d_skip,
              ssm_norm_w=ssm_norm_w, sinks=sinks, f_bias=f_bias, gate_bias=gate_bias, final_norm_w=final_norm_w)
    ms = dict(norm_w=m_norm_w, conv_w=m_conv_w, conv_b=m_conv_b, dt_bias=m_dt_bias, a_log=m_a_log, d_skip=m_d_skip,
              ssm_norm_w=m_ssm_norm_w, sinks=m_sinks, f_bias=m_f_bias, gate_bias=m_gate_bias,
              final_norm_w=m_final_norm_w)
    vs = dict(norm_w=v_norm_w, conv_w=v_conv_w, conv_b=v_conv_b, dt_bias=v_dt_bias, a_log=v_a_log, d_skip=v_d_skip,
              ssm_norm_w=v_ssm_norm_w, sinks=v_sinks, f_bias=v_f_bias, gate_bias=v_gate_bias,
              final_norm_w=v_final_norm_w)
    order = list(ws)
    oshapes = [ws[k].shape for k in order]
    res = _adamw_small(_pack_rows([g_small[k] for k in order]), _pack_rows([ws[k] for k in order]),
                       _pack_rows([ms[k] for k in order]), _pack_rows([vs[k] for k in order]))
    d_s, m_s, v_s = (dict(zip(order, _unpack_rows(r, oshapes))) for r in res)

    names = ("norm_w", "w_in", "conv_w", "conv_b", "dt_bias", "a_log", "d_skip", "ssm_norm_w", "sinks", "f_bias",
             "gate_bias", "w_proj", "w_out", "final_norm_w")
    grads, deltas, new_m, new_v = [], [], [], []
    for k in names:
        if k in big:
            g, d_, m_, v_ = big[k]
        else:
            g, d_, m_, v_ = g_small[k], d_s[k], m_s[k], v_s[k]
        grads.append(g)
        deltas.append(d_)
        new_m.append(m_)
        new_v.append(v_)
    return (loss, dx.reshape(Bl, S, D), *grads, *deltas, *new_m, *new_v)
```

```python
import functools
import math

import jax
import jax.numpy as jnp
from jax import lax
from jax.experimental import pallas as pl
from jax.experimental.pallas import tpu as pltpu

F32 = jnp.float32
BF16 = jnp.bfloat16
MESH = pl.DeviceIdType.MESH
NDEV = 8

D = 1024
NH = 16
HD = 64
NST = 128
NGRP = 4
LCH = 128
EPS = 1e-6
ROPE_THETA = 10000.0
SCALE = HD ** -0.5
NEG = -1e30

LANES = 128
VMEM_LIMIT = 56 * 1024 * 1024

OFF_XBC, OFF_AZ, OFF_BQ, OFF_BZ, OFF_CQ, OFF_CK, OFF_CV, OFF_CZ, OFF_G, OFF_BK, OFF_BV = (
    0, 2048, 3072, 4096, 5120, 6144, 7168, 8192, 9216, 12288, 12544)
NMAIN = 12800
NIN = 12832
NSH = NIN // NDEV

ROW_CHUNKS = ((0, 352), (352, 688), (688, 1024))

ADAM_LR, ADAM_B1, ADAM_B2, ADAM_EPS, ADAM_WD, ADAM_STEP = 0.001, 0.9, 0.999, 1e-08, 0.01, 10


def _cparams(dims=None, vmem=None):
    return pltpu.CompilerParams(dimension_semantics=dims, vmem_limit_bytes=vmem)


def _dot(a, b):
    return jnp.dot(a, b, preferred_element_type=F32)


def _dot_nt(a, b):
    return lax.dot_general(a, b, (((1,), (1,)), ((), ())), preferred_element_type=F32)


def _dot_tn(a, b):
    return lax.dot_general(a, b, (((0,), (0,)), ((), ())), preferred_element_type=F32)


def _dot_hi(a, b):
    return jnp.dot(a, b, precision=lax.Precision.HIGHEST, preferred_element_type=F32)


def _sigmoid(x):
    return 0.5 * jnp.tanh(0.5 * x) + 0.5


def _softplus(x):
    return jnp.maximum(x, 0.0) + jnp.log(1.0 + jnp.exp(-jnp.abs(x)))


def _lane_iota(n=LANES):
    return lax.broadcasted_iota(jnp.int32, (1, n), 1)


def _rot_half(x):
    first = (_lane_iota() % HD) < (HD // 2)
    return jnp.where(first, pltpu.roll(x, LANES - HD // 2, 1), pltpu.roll(x, HD // 2, 1))


def _head_sum(x, head):
    m = (_lane_iota() < HD) if head == 0 else (_lane_iota() >= HD)
    return jnp.sum(jnp.where(m, x, 0.0), axis=1, keepdims=True)


def _me_and_peers():
    x, y, c = lax.axis_index("x"), lax.axis_index("y"), lax.axis_index("c")
    me = 4 * x + 2 * y + c
    peers = []
    for k in range(1, NDEV):
        kx, ky, kc = (k >> 2) & 1, (k >> 1) & 1, k & 1
        px, py, pc = x ^ kx, y ^ ky, c ^ kc
        peers.append(((px, py, pc), 4 * px + 2 * py + pc))
    return me, peers


class _Comm:
    def __init__(self, kind, arrays):
        self.kind, self.arrays, self.n = kind, list(arrays), len(arrays)
        any_spec = pl.BlockSpec(memory_space=pl.ANY)
        self.in_specs = [any_spec] * self.n
        self.out_specs = [any_spec] * self.n
        self.out_shape = [jax.ShapeDtypeStruct(((NDEV,) + a.shape) if kind == "gather" else a.shape, a.dtype)
                          for a in self.arrays]
        self.scratch = [pltpu.SemaphoreType.DMA((self.n, NDEV - 1)), pltpu.SemaphoreType.DMA((self.n, NDEV - 1)),
                        pltpu.SemaphoreType.DMA((self.n,))]

    def copies(self, ins, outs, sems):
        send_sems, recv_sems, local_sems = sems
        me, peers = _me_and_peers()
        out = []
        for a in range(self.n):
            mine = ins[a] if self.kind == "gather" else ins[a].at[me]
            out.append(pltpu.make_async_copy(mine, outs[a].at[me], local_sems.at[a]))
            for k, (peer, pidx) in enumerate(peers):
                src = ins[a] if self.kind == "gather" else ins[a].at[pidx]
                out.append(pltpu.make_async_remote_copy(
                    src_ref=src, dst_ref=outs[a].at[me], send_sem=send_sems.at[a, k], recv_sem=recv_sems.at[a, k],
                    device_id=peer, device_id_type=MESH))
        return out


def _gather_two_level(arrays, name):
    n = len(arrays)

    def body(*refs):
        ins, outs = refs[:n], refs[n:2 * n]
        send_sems, recv_sems, local_sems = refs[2 * n:]
        x, y, c = lax.axis_index("x"), lax.axis_index("y"), lax.axis_index("c")
        me, sibling = (x, y, c), (x, y, 1 - c)
        chips = [(1 - x, y), (x, 1 - y), (1 - x, 1 - y)]

        def slot(a, dev):
            return outs[a].at[4 * dev[0] + 2 * dev[1] + dev[2]]

        def copy(a, k, block, to, src=None):
            return pltpu.make_async_remote_copy(
                src_ref=slot(a, block) if src is None else src, dst_ref=slot(a, block),
                send_sem=send_sems.at[a, k], recv_sem=recv_sems.at[a, k], device_id=to, device_id_type=MESH)

        mine = [pltpu.make_async_copy(ins[a], slot(a, me), local_sems.at[a]) for a in range(n)]
        for cp in mine:
            cp.start()
        first = []
        for a in range(n):
            first.append(copy(a, 0, me, sibling, src=ins[a]))
            first += [copy(a, 1 + j, me, (*chip, c), src=ins[a]) for j, chip in enumerate(chips)]
        for cp in first:
            cp.start()
        passed = []
        for j, chip in enumerate(chips):
            for a in range(n):
                copy(a, 1 + j, (*chip, c), me).wait_recv()
                fwd = copy(a, 4 + j, (*chip, c), sibling)
                fwd.start()
                passed.append(fwd)
        for a in range(n):
            copy(a, 0, sibling, me).wait_recv()
            for j, chip in enumerate(chips):
                copy(a, 4 + j, (*chip, 1 - c), me).wait_recv()
        for cp in first + passed:
            cp.wait_send()
        for cp in mine:
            cp.wait()

    any_spec = pl.BlockSpec(memory_space=pl.ANY)
    return pl.pallas_call(
        body, name=name, out_shape=[jax.ShapeDtypeStruct((NDEV,) + a.shape, a.dtype) for a in arrays],
        in_specs=[any_spec] * n, out_specs=[any_spec] * n,
        scratch_shapes=[pltpu.SemaphoreType.DMA((n, NDEV - 1)), pltpu.SemaphoreType.DMA((n, NDEV - 1)),
                        pltpu.SemaphoreType.DMA((n,))])(*arrays)


def _hosted_call(body, comm, name, grid, in_specs, out_specs, out_shape, scratch, dims, operands):
    if comm is None:
        return pl.pallas_call(body, name=name, grid=grid, in_specs=in_specs, out_specs=out_specs, out_shape=out_shape,
                              scratch_shapes=scratch, compiler_params=_cparams(dims, VMEM_LIMIT))(*operands)
    n_in, n_out, n_scr, n = len(in_specs), len(out_specs), len(scratch), comm.n

    def hosted(*refs):
        hin, cin = refs[:n_in], refs[n_in:n_in + n]
        hout = refs[n_in + n:n_in + n + n_out]
        cout = refs[n_in + n + n_out:n_in + 2 * n + n_out]
        hscr = refs[n_in + 2 * n + n_out:n_in + 2 * n + n_out + n_scr]
        sems = refs[n_in + 2 * n + n_out + n_scr:]
        ids = [pl.program_id(a) for a in range(len(grid))]
        first = functools.reduce(jnp.logical_and, [i == 0 for i in ids])
        last = functools.reduce(jnp.logical_and, [i == g - 1 for i, g in zip(ids, grid)])

        @pl.when(first)
        def _():
            for cp in comm.copies(cin, cout, sems):
                cp.start()

        body(*hin, *hout, *hscr)

        @pl.when(last)
        def _():
            for cp in comm.copies(cin, cout, sems):
                cp.wait()

    return pl.pallas_call(
        hosted, name=name, grid=grid, in_specs=list(in_specs) + comm.in_specs,
        out_specs=list(out_specs) + comm.out_specs, out_shape=list(out_shape) + comm.out_shape,
        scratch_shapes=list(scratch) + comm.scratch,
        compiler_params=_cparams(("arbitrary",) * len(grid), VMEM_LIMIT))(*operands, *comm.arrays)


def _all_reduce_small(v):
    rows = v.shape[0]

    def body(v_ref, sum_ref, all_ref, send_sems, recv_sems):
        me, peers = _me_and_peers()
        all_ref[me] = v_ref[...]
        copies = []
        for k, (peer, _) in enumerate(peers):
            cp = pltpu.make_async_remote_copy(
                src_ref=v_ref, dst_ref=all_ref.at[me],
                send_sem=send_sems.at[k], recv_sem=recv_sems.at[k],
                device_id=peer, device_id_type=MESH)
            cp.start()
            copies.append(cp)
        for cp in copies:
            cp.wait()
        acc = all_ref[0]
        for d in range(1, NDEV):
            acc = acc + all_ref[d]
        sum_ref[...] = acc

    vm = pl.BlockSpec(memory_space=pltpu.VMEM)
    return pl.pallas_call(
        body, name="all_reduce_small",
        out_shape=jax.ShapeDtypeStruct((rows, LANES), F32),
        in_specs=[vm], out_specs=vm,
        scratch_shapes=[pltpu.VMEM((NDEV, rows, LANES), F32),
                        pltpu.SemaphoreType.DMA((NDEV - 1,)), pltpu.SemaphoreType.DMA((NDEV - 1,))],
    )(v)


def _adamw_math(w, g, m, v):
    m = ADAM_B1 * m + (1.0 - ADAM_B1) * g
    v = ADAM_B2 * v + (1.0 - ADAM_B2) * jnp.square(g)
    m_hat = m / (1.0 - ADAM_B1 ** ADAM_STEP)
    v_hat = v / (1.0 - ADAM_B2 ** ADAM_STEP)
    delta = -ADAM_LR * (m_hat / (jnp.sqrt(v_hat) + ADAM_EPS) + ADAM_WD * w)
    return delta, m, v


def _sum_adamw(parts, w, m, v, name):
    depth, rows, cols = w.shape
    tr = next(c for c in (256, 128, 64, 32, 16) if rows % c == 0)
    nb = rows // tr

    def body(*refs):
        p_refs, (w_ref, m_ref, v_ref, g_ref, d_ref, nm_ref, nv_ref) = refs[:depth], refs[depth:]
        l = pl.program_id(0)
        for ll in range(depth):
            @pl.when(l == ll)
            def _(ll=ll):
                g = p_refs[ll][0].astype(F32)
                for d in range(1, NDEV):
                    g = g + p_refs[ll][d].astype(F32)
                delta, nm, nv = _adamw_math(w_ref[0], g, m_ref[0], v_ref[0])
                g_ref[0] = g
                d_ref[0] = delta
                nm_ref[0] = nm
                nv_ref[0] = nv

    part = lambda ll: pl.BlockSpec((NDEV, tr, cols), lambda l, i, ll=ll: (0, jnp.where(l == ll, i, jnp.where(l < ll, 0, nb - 1)), 0))
    blk = pl.BlockSpec((1, tr, cols), lambda l, i: (l, i, 0))
    sds = jax.ShapeDtypeStruct((depth, rows, cols), F32)
    return pl.pallas_call(
        body, name=name, grid=(depth, nb),
        in_specs=[part(ll) for ll in range(depth)] + [blk, blk, blk],
        out_specs=[blk, blk, blk, blk], out_shape=[sds, sds, sds, sds],
        compiler_params=_cparams(("arbitrary", "arbitrary"), VMEM_LIMIT),
    )(*parts, w, m, v)


def _adamw_small(g, w, m, v):
    def body(g_ref, w_ref, m_ref, v_ref, d_ref, nm_ref, nv_ref):
        delta, nm, nv = _adamw_math(w_ref[...], g_ref[...], m_ref[...], v_ref[...])
        d_ref[...] = delta
        nm_ref[...] = nm
        nv_ref[...] = nv

    sds = jax.ShapeDtypeStruct(g.shape, F32)
    return pl.pallas_call(body, name="adamw_small", out_shape=[sds, sds, sds])(g, w, m, v)


def _matmul(a, b, out_dtype, name, tm=1024, tn=1024, tk=1024):
    M, K = a.shape
    N = b.shape[1]
    tm, tn, tk = min(tm, M), min(tn, N), min(tk, K)
    nk = K // tk

    def body(a_ref, b_ref, o_ref, acc):
        k = pl.program_id(2)

        @pl.when(k == 0)
        def _():
            acc[...] = jnp.zeros_like(acc)

        acc[...] += _dot(a_ref[...], b_ref[...])

        @pl.when(k == nk - 1)
        def _():
            o_ref[...] = acc[...].astype(out_dtype)

    return pl.pallas_call(
        body, name=name, grid=(M // tm, N // tn, nk),
        in_specs=[pl.BlockSpec((tm, tk), lambda i, j, k: (i, k)), pl.BlockSpec((tk, tn), lambda i, j, k: (k, j))],
        out_specs=pl.BlockSpec((tm, tn), lambda i, j, k: (i, j)),
        out_shape=jax.ShapeDtypeStruct((M, N), out_dtype),
        scratch_shapes=[pltpu.VMEM((tm, tn), F32)],
        compiler_params=_cparams(("parallel", "parallel", "arbitrary"), VMEM_LIMIT),
    )(a, b)


def _matmul_batched(a, b, out_dtype, name, tm=1024, tn=1024, tk=1024):
    G, M, K = a.shape
    N = b.shape[2]
    tm, tn, tk = min(tm, M), min(tn, N), min(tk, K)
    nk = K // tk

    def body(a_ref, b_ref, o_ref, acc):
        k = pl.program_id(3)

        @pl.when(k == 0)
        def _():
            acc[...] = jnp.zeros_like(acc)

        acc[...] += _dot(a_ref[0], b_ref[0])

        @pl.when(k == nk - 1)
        def _():
            o_ref[0] = acc[...].astype(out_dtype)

    return pl.pallas_call(
        body, name=name, grid=(G, M // tm, N // tn, nk),
        in_specs=[pl.BlockSpec((1, tm, tk), lambda g, i, j, k: (g, i, k)),
                  pl.BlockSpec((1, tk, tn), lambda g, i, j, k: (g, k, j))],
        out_specs=pl.BlockSpec((1, tm, tn), lambda g, i, j, k: (g, i, j)),
        out_shape=jax.ShapeDtypeStruct((G, M, N), out_dtype),
        scratch_shapes=[pltpu.VMEM((tm, tn), F32)],
        compiler_params=_cparams(("parallel", "parallel", "parallel", "arbitrary"), VMEM_LIMIT),
    )(a, b)


def _inproj_fwd(x2, nw, wmain, wsmall, cos128, sin128, S, li, comm=None):
    T = x2.shape[0]
    tm, tn = min(2048, S), 512
    nj, npos = NMAIN // tn, S // tm
    jq0, jk = OFF_BQ // tn, OFF_BK // tn

    def body(x_ref, nw_ref, w_ref, ws_ref, cos_ref, sin_ref, proj_ref, ps_ref, ht_ref, h_scr):
        j = pl.program_id(1)

        @pl.when(j == 0)
        def _():
            x = x_ref[...]
            r = lax.rsqrt(jnp.mean(x * x, axis=-1, keepdims=True) + EPS)
            h = (x * r * nw_ref[...]).astype(BF16)
            h_scr[...] = h
            ht_ref[...] = h.T
            ps_ref[...] = _dot(h, ws_ref[...])

        acc = _dot(h_scr[...], w_ref[...])

        def roped(c):
            xc = acc[:, LANES * c:LANES * (c + 1)]
            return (xc * cos_ref[...] + _rot_half(xc) * sin_ref[...]).astype(BF16)

        def plain(c):
            return acc[:, LANES * c:LANES * (c + 1)].astype(BF16)

        is_q = jnp.logical_or(j == jq0, j == jq0 + 1)
        is_k = j == jk

        @pl.when(is_q)
        def _():
            for c in range(4):
                proj_ref[:, LANES * c:LANES * (c + 1)] = roped(c)

        @pl.when(is_k)
        def _():
            for c in range(4):
                proj_ref[:, LANES * c:LANES * (c + 1)] = roped(c) if c < 2 else plain(c)

        @pl.when(jnp.logical_not(jnp.logical_or(is_q, is_k)))
        def _():
            proj_ref[...] = acc.astype(BF16)

    return _hosted_call(
        body, comm, f"inproj_fwd_{li}", (T // tm, nj),
        in_specs=[pl.BlockSpec((tm, D), lambda i, j: (i, 0)),
                  pl.BlockSpec((1, D), lambda i, j: (0, 0)),
                  pl.BlockSpec((D, tn), lambda i, j: (0, j)),
                  pl.BlockSpec((D, LANES), lambda i, j: (0, 0)),
                  pl.BlockSpec((tm, LANES), lambda i, j: (i % npos, 0)),
                  pl.BlockSpec((tm, LANES), lambda i, j: (i % npos, 0))],
        out_specs=[pl.BlockSpec((tm, tn), lambda i, j: (i, j)),
                   pl.BlockSpec((tm, LANES), lambda i, j: (i, 0)),
                   pl.BlockSpec((D, tm), lambda i, j: (0, i))],
        out_shape=[jax.ShapeDtypeStruct((T, NMAIN), BF16), jax.ShapeDtypeStruct((T, LANES), F32),
                   jax.ShapeDtypeStruct((D, T), BF16)],
        scratch=[pltpu.VMEM((tm, D), BF16)], dims=("parallel", "arbitrary"),
        operands=(x2, nw, wmain, wsmall, cos128, sin128))


def _inproj_bwd_dx(segs, wmain, init, final, name, comm=None):
    T = segs[0][0].shape[0]
    tm = min(1024, T)
    tk = 1024 if all(a.shape[1] % 1024 == 0 and c % 1024 == 0 for a, c in segs) else 512
    ni = T // tm
    k0s, nks, c0s = [], [], []
    for arr, col0 in segs:
        k0s.append(sum(nks))
        nks.append(arr.shape[1] // tk)
        c0s.append(col0 // tk)
    nk = sum(nks)
    ns = len(segs)

    def in_range(k, s):
        return jnp.logical_and(k >= k0s[s], k < k0s[s] + nks[s])

    def wcol(i, k):
        g = 0
        for s in range(ns):
            g = g + jnp.where(in_range(k, s), c0s[s] + k - k0s[s], 0)
        return (0, g)

    n_init = 2 if init[0] == "narrow" else 1

    def body(*refs):
        seg_refs, w_ref = refs[:ns], refs[ns]
        init_refs = refs[ns + 1:ns + 1 + n_init]
        rest = refs[ns + 1 + n_init:]
        i, k = pl.program_id(0), pl.program_id(1)
        acc = rest[-1]

        @pl.when(k == 0)
        def _():
            if init[0] == "narrow":
                acc[...] = _dot_nt(init_refs[0][...], init_refs[1][...])
            else:
                acc[...] = init_refs[0][...]

        for s in range(ns):
            @pl.when(in_range(k, s))
            def _(s=s):
                acc[...] += _dot_nt(seg_refs[s][...], w_ref[...])

        if final is None:
            @pl.when(k == nk - 1)
            def _():
                rest[0][...] = acc[...]
        else:
            x_ref, nw_ref, dxo_ref, dx_ref, dx16_ref, dnw_ref = rest[:6]

            @pl.when(jnp.logical_and(i == 0, k == 0))
            def _():
                dnw_ref[...] = jnp.zeros_like(dnw_ref)

            @pl.when(k == nk - 1)
            def _():
                x = x_ref[...]
                r = lax.rsqrt(jnp.mean(x * x, axis=-1, keepdims=True) + EPS)
                dh = acc[...]
                g = dh * nw_ref[...]
                dx = dxo_ref[...] + r * g - x * (r * r * r) * jnp.mean(g * x, axis=-1, keepdims=True)
                dx_ref[...] = dx
                dx16_ref[...] = dx.astype(BF16)
                dnw_ref[0:1, :] += jnp.sum(dh * x * r, axis=0, keepdims=True)

    row = pl.BlockSpec((tm, D), lambda i, k: (i, 0))
    in_specs = [pl.BlockSpec((tm, tk), lambda i, k, s=s: (i, jnp.clip(k - k0s[s], 0, nks[s] - 1))) for s in range(ns)]
    in_specs.append(pl.BlockSpec((D, tk), wcol))
    operands = [a for a, _ in segs] + [wmain]
    if init[0] == "narrow":
        in_specs += [pl.BlockSpec((tm, LANES), lambda i, k: (i, 0)), pl.BlockSpec((D, LANES), lambda i, k: (0, 0))]
    else:
        in_specs.append(row)
    operands += list(init[1:])
    if final is None:
        out_specs, out_shape = [row], [jax.ShapeDtypeStruct((T, D), F32)]
    else:
        in_specs += [row, pl.BlockSpec((1, D), lambda i, k: (0, 0)), row]
        operands += list(final)
        out_specs = [row, row, pl.BlockSpec((8, D), lambda i, k: (0, 0))]
        out_shape = [jax.ShapeDtypeStruct((T, D), F32), jax.ShapeDtypeStruct((T, D), BF16),
                     jax.ShapeDtypeStruct((8, D), F32)]
    return _hosted_call(body, comm, name, (ni, nk), in_specs=in_specs, out_specs=out_specs, out_shape=out_shape,
                        scratch=[pltpu.VMEM((tm, D), F32)], dims=("arbitrary", "arbitrary"), operands=tuple(operands))


def _merge_fwd(ya, yb, yc, proj, gbias, wp, wout, x2, li):
    T = x2.shape[0]
    tm = min(512, T)
    gcol = OFF_G // D

    def body(ya_ref, yb_ref, yc_ref, g0_ref, g1_ref, g2_ref, gb_ref, wp_ref, wo_ref, x_ref, xn_ref, br_ref, yt_ref):
        merged = jnp.zeros((tm, D), F32)
        for i, (y_ref, g_ref) in enumerate(((ya_ref, g0_ref), (yb_ref, g1_ref), (yc_ref, g2_ref))):
            y = y_ref[...]
            yt_ref[i] = y.T
            br = _dot(y, wp_ref[i])
            br_ref[i] = br.astype(BF16)
            gate = _sigmoid(g_ref[...].astype(F32) + gb_ref[i:i + 1, :])
            merged = merged + gate * br
        xn_ref[...] = x_ref[...] + _dot(merged.astype(BF16), wo_ref[...])

    row = lambda c: pl.BlockSpec((tm, D), lambda i, c=c: (i, c))
    return pl.pallas_call(
        body, name=f"merge_fwd_{li}", grid=(T // tm,),
        in_specs=[row(0), row(0), row(0), row(gcol), row(gcol + 1), row(gcol + 2),
                  pl.BlockSpec((3, D), lambda i: (0, 0)),
                  pl.BlockSpec((3, D, D), lambda i: (0, 0, 0)),
                  pl.BlockSpec((D, D), lambda i: (0, 0)),
                  row(0)],
        out_specs=[row(0), pl.BlockSpec((3, tm, D), lambda i: (0, i, 0)), pl.BlockSpec((3, D, tm), lambda i: (0, 0, i))],
        out_shape=[jax.ShapeDtypeStruct((T, D), F32), jax.ShapeDtypeStruct((3, T, D), BF16),
                   jax.ShapeDtypeStruct((3, D, T), BF16)],
        compiler_params=_cparams(("parallel",), VMEM_LIMIT),
    )(ya, yb, yc, proj, proj, proj, gbias, wp, wout, x2)


def _merge_bwd(dxo16, wout, wp, br, proj, gbias, ob, oc, li):
    T = dxo16.shape[0]
    tm = min(256, T)
    gcol = OFF_G // D

    def body(dx_ref, wo_ref, wp_ref, br_ref, g0_ref, g1_ref, g2_ref, gb_ref, ob_ref, oc_ref, zb_ref, zc_ref,
             dbr_ref, dg_ref, mt_ref, dgb_ref, dya_ref, dob_ref, dzb_ref, doc_ref, dzc_ref):
        @pl.when(pl.program_id(0) == 0)
        def _():
            dgb_ref[...] = jnp.zeros_like(dgb_ref)

        dm = _dot_nt(dx_ref[...], wo_ref[...])
        merged = jnp.zeros((tm, D), F32)
        dys = []
        for i, g_ref in enumerate((g0_ref, g1_ref, g2_ref)):
            b = br_ref[i].astype(F32)
            gate = _sigmoid(g_ref[...].astype(F32) + gb_ref[i:i + 1, :])
            merged = merged + gate * b
            dbr = (dm * gate).astype(BF16)
            dbr_ref[i] = dbr
            dgate = dm * b * gate * (1.0 - gate)
            dg_ref[:, D * i:D * (i + 1)] = dgate.astype(BF16)
            dgb_ref[i:i + 1, :] += jnp.sum(dgate, axis=0, keepdims=True)
            dys.append(_dot_nt(dbr, wp_ref[i]))
        mt_ref[...] = merged.astype(BF16).T
        dya_ref[...] = dys[0].astype(BF16)
        for dy, o_ref, z_ref, do_ref, dz_ref in ((dys[1], ob_ref, zb_ref, dob_ref, dzb_ref),
                                                 (dys[2], oc_ref, zc_ref, doc_ref, dzc_ref)):
            z = z_ref[...].astype(F32)
            sg = _sigmoid(z)
            do_ref[...] = (dy * z * sg).astype(BF16)
            dz_ref[...] = (dy * o_ref[...].astype(F32) * sg * (1.0 + z * (1.0 - sg))).astype(BF16)

    row = lambda c: pl.BlockSpec((tm, D), lambda i, c=c: (i, c))
    sds = jax.ShapeDtypeStruct((T, D), BF16)
    return pl.pallas_call(
        body, name=f"merge_bwd_{li}", grid=(T // tm,),
        in_specs=[row(0), pl.BlockSpec((D, D), lambda i: (0, 0)), pl.BlockSpec((3, D, D), lambda i: (0, 0, 0)),
                  pl.BlockSpec((3, tm, D), lambda i: (0, i, 0)),
                  row(gcol), row(gcol + 1), row(gcol + 2),
                  pl.BlockSpec((3, D), lambda i: (0, 0)),
                  row(0), row(0), row(OFF_BZ // D), row(OFF_CZ // D)],
        out_specs=[pl.BlockSpec((3, tm, D), lambda i: (0, i, 0)),
                   pl.BlockSpec((tm, 3 * D), lambda i: (i, 0)),
                   pl.BlockSpec((D, tm), lambda i: (0, i)),
                   pl.BlockSpec((8, D), lambda i: (0, 0)),
                   row(0), row(0), row(0), row(0), row(0)],
        out_shape=[jax.ShapeDtypeStruct((3, T, D), BF16), jax.ShapeDtypeStruct((T, 3 * D), BF16),
                   jax.ShapeDtypeStruct((D, T), BF16), jax.ShapeDtypeStruct((8, D), F32), sds, sds, sds, sds, sds],
        compiler_params=_cparams(("arbitrary",), VMEM_LIMIT),
    )(dxo16, wout, wp, br, proj, proj, proj, gbias, ob, oc, proj, proj)


def _final_loss(x2, tgt, fw):
    T = x2.shape[0]
    tm = min(512, T)
    ni = T // tm

    def body(x_ref, t_ref, w_ref, dx_ref, dx16_ref, st_ref):
        i = pl.program_id(0)

        @pl.when(i == 0)
        def _():
            st_ref[...] = jnp.zeros_like(st_ref)

        x = x_ref[...]
        r = lax.rsqrt(jnp.mean(x * x, axis=-1, keepdims=True) + EPS)
        xh = x * r
        err = xh * w_ref[...] - t_ref[...]
        dy = err * (1.0 / D)
        g = dy * w_ref[...]
        dx = r * g - x * (r * r * r) * jnp.mean(g * x, axis=-1, keepdims=True)
        dx_ref[...] = dx
        dx16_ref[...] = dx.astype(BF16)
        st_ref[0:1, :] += jnp.sum(dy * xh, axis=0, keepdims=True)
        st_ref[1:2, :] += jnp.sum(err * err, axis=0, keepdims=True)

        @pl.when(i == ni - 1)
        def _():
            tot = jnp.sum(st_ref[1:2, :], axis=1, keepdims=True) * (0.5 / D)
            st_ref[2:3, :] = jnp.broadcast_to(tot, (1, D))

    row = pl.BlockSpec((tm, D), lambda i: (i, 0))
    return pl.pallas_call(
        body, name="final_loss", grid=(ni,),
        in_specs=[row, row, pl.BlockSpec((1, D), lambda i: (0, 0))],
        out_specs=[row, row, pl.BlockSpec((8, D), lambda i: (0, 0))],
        out_shape=[jax.ShapeDtypeStruct((T, D), F32), jax.ShapeDtypeStruct((T, D), BF16),
                   jax.ShapeDtypeStruct((8, D), F32)],
        compiler_params=_cparams(("arbitrary",), VMEM_LIMIT),
    )(x2, tgt, fw)


def _fox_cum(ps, fb_row, S, li):
    T = ps.shape[0]
    blk = min(4 * LCH, S)
    nb, nsub = S // blk, blk // LCH

    def body(ps_ref, fb_ref, cum_ref, carry):
        @pl.when(pl.program_id(1) == 0)
        def _():
            carry[...] = jnp.zeros_like(carry)

        r = lax.broadcasted_iota(jnp.int32, (LCH, LCH), 0)
        c = lax.broadcasted_iota(jnp.int32, (LCH, LCH), 1)
        tri = (r >= c).astype(F32)
        run = carry[0:1, :]
        for u in range(nsub):
            rows = slice(LCH * u, LCH * (u + 1))
            logf = -_softplus(-(ps_ref[rows, :] + fb_ref[...]))
            cum = _dot_hi(tri, logf) + run
            cum_ref[rows, :] = cum
            run = cum[LCH - 1:LCH, :]
        carry[0:1, :] = run

    return pl.pallas_call(
        body, name=f"fox_cum_{li}", grid=(T // S, nb),
        in_specs=[pl.BlockSpec((blk, LANES), lambda b, i: (b * nb + i, 0)),
                  pl.BlockSpec((1, LANES), lambda b, i: (0, 0))],
        out_specs=pl.BlockSpec((blk, LANES), lambda b, i: (b * nb + i, 0)),
        out_shape=jax.ShapeDtypeStruct((T, LANES), F32),
        scratch_shapes=[pltpu.VMEM((8, LANES), F32)],
        compiler_params=_cparams(("arbitrary", "arbitrary")),
    )(ps, fb_row)


def _fox_cum_bwd(dcum, ps, fb_row, S, li):
    T = ps.shape[0]
    rows_blk = min(4 * LCH, S)
    nb, nsub = S // rows_blk, rows_blk // LCH

    def body(dc_ref, ps_ref, fb_ref, df_ref, dfb_ref, carry):
        b, i = pl.program_id(0), pl.program_id(1)

        @pl.when(i == 0)
        def _():
            carry[...] = jnp.zeros_like(carry)

        @pl.when(jnp.logical_and(b == 0, i == 0))
        def _():
            dfb_ref[...] = jnp.zeros_like(dfb_ref)

        r = lax.broadcasted_iota(jnp.int32, (LCH, LCH), 0)
        c = lax.broadcasted_iota(jnp.int32, (LCH, LCH), 1)
        tri = (c >= r).astype(F32)
        lane = _lane_iota()
        live = jnp.logical_and(lane >= NH, lane < 2 * NH)
        run = carry[0:1, :]
        dfb = jnp.zeros((1, LANES), F32)
        for u in reversed(range(nsub)):
            rows = slice(LCH * u, LCH * (u + 1))
            dc = dc_ref[rows, :]
            dlogf = _dot_hi(tri, dc) + run
            run = run + jnp.sum(dc, axis=0, keepdims=True)
            df = jnp.where(live, dlogf * _sigmoid(-(ps_ref[rows, :] + fb_ref[...])), 0.0)
            df_ref[rows, :] = df
            dfb = dfb + jnp.sum(df, axis=0, keepdims=True)
        carry[0:1, :] = run
        dfb_ref[0:1, :] += dfb

    blk = pl.BlockSpec((rows_blk, LANES), lambda b, i: (b * nb + nb - 1 - i, 0))
    return pl.pallas_call(
        body, name=f"fox_cum_bwd_{li}", grid=(T // S, nb),
        in_specs=[blk, blk, pl.BlockSpec((1, LANES), lambda b, i: (0, 0))],
        out_specs=[blk, pl.BlockSpec((8, LANES), lambda b, i: (0, 0))],
        out_shape=[jax.ShapeDtypeStruct((T, LANES), F32), jax.ShapeDtypeStruct((8, LANES), F32)],
        scratch_shapes=[pltpu.VMEM((8, LANES), F32)],
        compiler_params=_cparams(("arbitrary", "arbitrary")),
    )(dcum, ps, fb_row)


def _fox_blocks(S):
    bq = min(512, S)
    return bq, S // bq


def _split3(c):
    hi = c.astype(BF16).astype(F32)
    r = c - hi
    mid = r.astype(BF16).astype(F32)
    return hi, mid, (r - mid).astype(BF16).astype(F32)


def _augment(x, bias_row, key_side, hh):
    n = x.shape[0]
    b0 = HD if hh == 0 else 0
    hi, mid, lo = _split3(bias_row)
    one = jnp.ones_like(bias_row)
    six = (one, one, one, hi, mid, lo) if key_side else (hi, mid, lo, one, one, one)
    sub = lax.broadcasted_iota(jnp.int32, (LANES, 1), 0)
    a = jnp.zeros((LANES, n), F32)
    for t, r in enumerate(six):
        a = jnp.where(sub == b0 + t, r, a)
    lane = _lane_iota()
    return jnp.where(jnp.logical_and(lane >= b0, lane < b0 + 6), a.T, x).astype(BF16)


def _col_to_row(col):
    return jnp.broadcast_to(col, (col.shape[0], LANES)).T[0:1, :]


def _fox_fwd(proj, cum_row, S, li, comm=None):
    T = proj.shape[0]
    B = T // S
    bq, nq = _fox_blocks(S)
    qc, kc, vc, zc = OFF_CQ // LANES, OFF_CK // LANES, OFF_CV // LANES, OFF_CZ // LANES

    def body(q_ref, k_ref, v_ref, z_ref, cr_ref, y_ref, o_ref, lse_ref, kaug, vaug):
        i = pl.program_id(2)
        m0 = _lane_iota() < HD

        @pl.when(i == 0)
        def _():
            vf = v_ref[...]
            for hh in range(2):
                for t in range(nq):
                    rows = slice(bq * t, bq * (t + 1))
                    kaug[hh, rows, :] = _augment(k_ref[rows, :].astype(F32), -cr_ref[0, hh, t], True, hh)
                vaug[hh] = jnp.where(m0 if hh == 0 else jnp.logical_not(m0), vf, jnp.ones_like(vf))

        q2 = q_ref[...].astype(F32) * SCALE
        row = lax.broadcasted_iota(jnp.int32, (bq, bq), 0)
        col = lax.broadcasted_iota(jnp.int32, (bq, bq), 1)
        qa = [_augment(jnp.where(m0 if hh == 0 else jnp.logical_not(m0), q2, 0.0), cr_ref[0, hh, i], False, hh)
              for hh in range(2)]

        def step(j, carry, masked):
            start = pl.multiple_of(j * bq, bq)
            out = []
            for hh in range(2):
                m, acc = carry[2 * hh:2 * hh + 2]
                s = _dot_nt(qa[hh], kaug[hh, pl.ds(start, bq), :])
                if masked:
                    s = jnp.where(row >= col, s, NEG)
                mn = jnp.maximum(m, jnp.max(s, axis=1, keepdims=True))
                p = jnp.exp(s - mn)
                out += [mn, jnp.exp(m - mn) * acc + _dot(p.astype(BF16), vaug[hh, pl.ds(start, bq), :])]
            return tuple(out)

        init = (jnp.full((bq, 1), NEG, F32), jnp.zeros((bq, LANES), F32)) * 2
        carry = step(i, lax.fori_loop(0, i, functools.partial(step, masked=False), init), True)
        outs = []
        for hh in range(2):
            m, acc = carry[2 * hh:2 * hh + 2]
            other = HD if hh == 0 else 0
            l = acc[:, other:other + 1]
            outs.append(acc / l)
            lse_ref[0, hh, 0] = _col_to_row(m + jnp.log(l))
        o2 = jnp.where(m0, outs[0], outs[1])
        z = z_ref[...].astype(F32)
        o_ref[...] = o2.astype(BF16)
        y_ref[...] = (o2 * z * _sigmoid(z)).astype(BF16)

    qblk = lambda c: pl.BlockSpec((bq, LANES), lambda b, p, i, c=c: (b * nq + i, c + p))
    sblk = lambda c: pl.BlockSpec((S, LANES), lambda b, p, i, c=c: (b, c + p))
    return _hosted_call(
        body, comm, f"fox_fwd_{li}", (B, NH // 2, nq),
        in_specs=[qblk(qc), sblk(kc), sblk(vc), qblk(zc),
                  pl.BlockSpec((1, 2, nq, 1, bq), lambda b, p, i: (b, p, 0, 0, 0))],
        out_specs=[qblk(0), qblk(0), pl.BlockSpec((1, 2, 1, 1, bq), lambda b, p, i: (b, p, i, 0, 0))],
        out_shape=[jax.ShapeDtypeStruct((T, D), BF16), jax.ShapeDtypeStruct((T, D), BF16),
                   jax.ShapeDtypeStruct((B, NH, nq, 1, bq), F32)],
        scratch=[pltpu.VMEM((2, S, LANES), BF16), pltpu.VMEM((2, S, LANES), BF16)],
        dims=("parallel", "parallel", "arbitrary"), operands=(proj, proj, proj, proj, cum_row))


def _fox_bwd(proj, do, o, cum_row, lse, S, li, comm=None):
    T = proj.shape[0]
    B = T // S
    bq, nq = _fox_blocks(S)
    qc, kc, vc = OFF_CQ // LANES, OFF_CK // LANES, OFF_CV // LANES

    def body(q_ref, k_ref, v_ref, do_ref, o_ref, cr_ref, lse_ref, dq_ref, dk_ref, dv_ref, dc_ref, dr_ref,
             dq_scr, dr_scr, qaug):
        j = pl.program_id(2)
        m0 = _lane_iota() < HD

        @pl.when(j == 0)
        def _():
            dq_scr[...] = jnp.zeros_like(dq_scr)
            dr_scr[...] = jnp.zeros_like(dr_scr)
            for t in range(nq):
                rows = slice(bq * t, bq * (t + 1))
                qf = q_ref[rows, :].astype(F32) * SCALE
                for hh in range(2):
                    sel = m0 if hh == 0 else jnp.logical_not(m0)
                    qaug[hh, rows, :] = _augment(jnp.where(sel, qf, 0.0), cr_ref[0, hh, t] - lse_ref[0, hh, t],
                                                 False, hh)

        k2 = k_ref[...]
        v2 = v_ref[...]
        zk = jnp.zeros_like(k2)
        kh = (jnp.where(m0, k2, zk), jnp.where(m0, zk, k2))
        kf = k2.astype(F32)
        ka = [_augment(kf, -cr_ref[0, hh, j], True, hh) for hh in range(2)]
        row = lax.broadcasted_iota(jnp.int32, (bq, bq), 0)
        col = lax.broadcasted_iota(jnp.int32, (bq, bq), 1)

        def step(i, carry, masked):
            dk, dv, dc0, dc1 = carry
            dcs = [dc0, dc1]
            start = pl.multiple_of(i * bq, bq)
            q2 = q_ref[pl.ds(start, bq), :]
            do2 = do_ref[pl.ds(start, bq), :]
            prod = do2.astype(F32) * o_ref[pl.ds(start, bq), :].astype(F32)
            zq = jnp.zeros_like(q2)
            dq = jnp.zeros((bq, LANES), F32)
            for hh in range(2):
                sel = m0 if hh == 0 else jnp.logical_not(m0)
                qh = jnp.where(sel, q2, zq)
                doh = jnp.where(sel, do2, zq)
                delta = _head_sum(prod, hh)
                s = _dot_nt(qaug[hh, pl.ds(start, bq), :], ka[hh])
                if masked:
                    s = jnp.where(row >= col, s, NEG)
                p = jnp.exp(s)
                dp = _dot_nt(doh, v2)
                ds = p * (dp - delta)
                dcs[hh] = dcs[hh] - jnp.sum(ds, axis=0, keepdims=True)
                dr_scr[hh, pl.ds(start, bq), :] += jnp.sum(ds, axis=1, keepdims=True)
                dsb = ds.astype(BF16)
                dv = dv + _dot_tn(p.astype(BF16), doh)
                dk = dk + _dot_tn(dsb, qh)
                dq = dq + _dot(dsb, kh[hh])
            dq_scr[pl.ds(start, bq), :] += dq
            return dk, dv, dcs[0], dcs[1]

        zero = jnp.zeros((bq, LANES), F32)
        zrow = jnp.zeros((1, bq), F32)
        carry = step(j, (zero, zero, zrow, zrow), True)
        dk, dv, dc0, dc1 = lax.fori_loop(j + 1, nq, functools.partial(step, masked=False), carry)
        dk_ref[...] = (dk * SCALE).astype(BF16)
        dv_ref[...] = dv.astype(BF16)
        dc_ref[0, 0, 0] = dc0
        dc_ref[0, 1, 0] = dc1

        @pl.when(j == nq - 1)
        def _():
            dq_ref[...] = (dq_scr[...] * SCALE).astype(BF16)
            step_r = min(4 * LANES, S)
            for hh in range(2):
                for t in range(S // step_r):
                    dr_ref[0, hh, :, step_r * t:step_r * (t + 1)] = _col_to_row(dr_scr[hh, step_r * t:step_r * (t + 1), :])

    sblk = lambda c: pl.BlockSpec((S, LANES), lambda b, p, j, c=c: (b, c + p))
    kblk = lambda c: pl.BlockSpec((bq, LANES), lambda b, p, j, c=c: (b * nq + j, c + p))
    rows_spec = pl.BlockSpec((1, 2, nq, 1, bq), lambda b, p, j: (b, p, 0, 0, 0))
    row_spec = pl.BlockSpec((1, 2, 1, S), lambda b, p, j: (b, p, 0, 0))
    return _hosted_call(
        body, comm, f"fox_bwd_{li}", (B, NH // 2, nq),
        in_specs=[sblk(qc), kblk(kc), kblk(vc), sblk(0), sblk(0), rows_spec, rows_spec],
        out_specs=[sblk(0), kblk(0), kblk(0), pl.BlockSpec((1, 2, 1, 1, bq), lambda b, p, j: (b, p, j, 0, 0)),
                   row_spec],
        out_shape=[jax.ShapeDtypeStruct((T, D), BF16), jax.ShapeDtypeStruct((T, D), BF16),
                   jax.ShapeDtypeStruct((T, D), BF16), jax.ShapeDtypeStruct((B, NH, nq, 1, bq), F32),
                   jax.ShapeDtypeStruct((B, NH, 1, S), F32)],
        scratch=[pltpu.VMEM((S, LANES), F32), pltpu.VMEM((2, S, 1), F32), pltpu.VMEM((2, S, LANES), BF16)],
        dims=("parallel", "parallel", "arbitrary"), operands=(proj, proj, proj, do, o, cum_row, lse))


def _swa_blocks(S):
    bq = min(512, S)
    return bq, S // bq, bq // LCH


def _dup_head(xw, kvl):
    m0 = _lane_iota() < HD
    a = jnp.where(m0 if kvl == 0 else jnp.logical_not(m0), xw, 0.0)
    return (a + pltpu.roll(a, HD, 1)).astype(BF16)


def _band(same_block):
    r = lax.broadcasted_iota(jnp.int32, (LCH, LCH), 0)
    c = lax.broadcasted_iota(jnp.int32, (LCH, LCH), 1)
    return (c <= r) if same_block else (c > r)


def _stack_heads(ref, rows, kvl):
    m0 = _lane_iota() < HD
    parts = []
    for ch in (2 * kvl, 2 * kvl + 1):
        x = ref[rows, LANES * ch:LANES * (ch + 1)]
        parts += [jnp.where(m0, x, jnp.zeros_like(x)), jnp.where(m0, jnp.zeros_like(x), x)]
    return jnp.concatenate(parts, axis=0)


def _stack_delta(do_ref, o_ref, rows, kvl, scale=None):
    parts = []
    for ch in (2 * kvl, 2 * kvl + 1):
        lanes = slice(LANES * ch, LANES * (ch + 1))
        prod = do_ref[rows, lanes].astype(F32) * o_ref[rows, lanes].astype(F32)
        parts += [_head_sum(prod, 0), _head_sum(prod, 1)]
    out = jnp.concatenate(parts, axis=0)
    return out if scale is None else out * scale


def _stack_cols(ref, rows, kvl):
    return jnp.concatenate([ref[0, 4 * kvl + t, rows, :] for t in range(4)], axis=0)


def _swa_fwd(proj, sinks, S, li):
    T = proj.shape[0]
    B = T // S
    bq, nq, nsub = _swa_blocks(S)
    nrow = S // LCH
    qc, zc, kc, vc = OFF_BQ // 512, OFF_BZ // 512, OFF_BK // LANES, OFF_BV // LANES

    def body(sk_ref, q_ref, z_ref, kp_ref, kc_ref, vp_ref, vc_ref, y_ref, o_ref, lse_ref):
        c, i = pl.program_id(0), pl.program_id(2)
        m0 = _lane_iota() < HD
        kw = jnp.concatenate([kp_ref[...].astype(F32), kc_ref[...].astype(F32)], axis=0)
        vw = jnp.concatenate([vp_ref[...].astype(F32), vc_ref[...].astype(F32)], axis=0)
        kd = (_dup_head(kw, 0), _dup_head(kw, 1))
        vd = (_dup_head(vw, 0), _dup_head(vw, 1))
        valid = jnp.concatenate([_band(False), _band(True)], axis=1)
        col = lax.broadcasted_iota(jnp.int32, (LCH, 2 * LCH), 1)
        valid_first = jnp.logical_and(valid, jnp.logical_or(col >= LCH, i > 0))
        for r in range(nsub):
            rows = slice(LCH * r, LCH * (r + 1))
            msk = valid_first if r == 0 else valid
            for ch in range(4):
                kvl = ch // 2
                kwin = kd[kvl][LCH * r:LCH * (r + 2)]
                vwin = vd[kvl][LCH * r:LCH * (r + 2)]
                lanes = slice(LANES * ch, LANES * (ch + 1))
                q2 = q_ref[rows, lanes]
                outs = []
                for hh in range(2):
                    hl = 2 * ch + hh
                    qh = jnp.where(m0 if hh == 0 else jnp.logical_not(m0), q2, jnp.zeros_like(q2))
                    s = jnp.where(msk, _dot_nt(qh, kwin) * SCALE, NEG)
                    sink = sk_ref[8 * c + hl]
                    m = jnp.maximum(jnp.max(s, axis=1, keepdims=True), sink)
                    p = jnp.exp(s - m)
                    l = jnp.sum(p, axis=1, keepdims=True) + jnp.exp(sink - m)
                    outs.append(_dot(p.astype(BF16), vwin) / l)
                    lse_ref[0, hl, rows, :] = m + jnp.log(l)
                o2 = jnp.where(m0, outs[0], outs[1])
                z = z_ref[rows, lanes].astype(F32)
                o_ref[rows, lanes] = o2.astype(BF16)
                y_ref[rows, lanes] = (o2 * z * _sigmoid(z)).astype(BF16)

    wide = lambda cc: pl.BlockSpec((bq, 512), lambda c, b, i, cc=cc: (b * nq + i, cc + c))
    cur = lambda cc: pl.BlockSpec((bq, LANES), lambda c, b, i, cc=cc: (b * nq + i, cc + c))
    prev = lambda cc: pl.BlockSpec((LCH, LANES), lambda c, b, i, cc=cc: (b * nrow + jnp.maximum(i * nsub - 1, 0), cc + c))
    return pl.pallas_call(
        body, name=f"swa_fwd_{li}", grid=(2, B, nq),
        in_specs=[pl.BlockSpec(memory_space=pltpu.SMEM), wide(qc), wide(zc), prev(kc), cur(kc), prev(vc), cur(vc)],
        out_specs=[wide(0), wide(0), pl.BlockSpec((1, 8, bq, 1), lambda c, b, i: (b, c, i, 0))],
        out_shape=[jax.ShapeDtypeStruct((T, D), BF16), jax.ShapeDtypeStruct((T, D), BF16),
                   jax.ShapeDtypeStruct((B, NH, S, 1), F32)],
        compiler_params=_cparams(("parallel", "parallel", "parallel"), VMEM_LIMIT),
    )(sinks, proj, proj, proj, proj, proj, proj)


def _swa_bwd_dq(proj, do, o, lse, sinks, cos128, sin128, S, li):
    T = proj.shape[0]
    B = T // S
    bq, nq, nsub = _swa_blocks(S)
    nrow = S // LCH
    qc, kc, vc = OFF_BQ // 512, OFF_BK // LANES, OFF_BV // LANES

    def body(sk_ref, q_ref, do_ref, o_ref, lse_ref, kp_ref, kc_ref, vp_ref, vc_ref, cos_ref, sin_ref, dq_ref, dsk_ref):
        c, b, i = pl.program_id(0), pl.program_id(1), pl.program_id(2)

        @pl.when(jnp.logical_and(b == 0, i == 0))
        def _():
            dsk_ref[...] = jnp.zeros_like(dsk_ref)

        m0 = _lane_iota() < HD
        kw = jnp.concatenate([kp_ref[...].astype(F32), kc_ref[...].astype(F32)], axis=0)
        vw = jnp.concatenate([vp_ref[...].astype(F32), vc_ref[...].astype(F32)], axis=0)
        kd = (_dup_head(kw, 0), _dup_head(kw, 1))
        vd = (_dup_head(vw, 0), _dup_head(vw, 1))
        valid = jnp.concatenate([_band(False), _band(True)], axis=1)
        col = lax.broadcasted_iota(jnp.int32, (LCH, 2 * LCH), 1)
        valid_first = jnp.logical_and(valid, jnp.logical_or(col >= LCH, i > 0))
        dsk = [jnp.zeros((1, 1), F32) for _ in range(8)]
        valid4 = jnp.concatenate([valid] * 4, axis=0)
        valid4_first = jnp.concatenate([valid_first] * 4, axis=0)
        for r in range(nsub):
            rows = slice(LCH * r, LCH * (r + 1))
            msk = valid4_first if r == 0 else valid4
            for kvl in range(2):
                kwin = kd[kvl][LCH * r:LCH * (r + 2)]
                vwin = vd[kvl][LCH * r:LCH * (r + 2)]
                qs = _stack_heads(q_ref, rows, kvl)
                dos = _stack_heads(do_ref, rows, kvl)
                delta = _stack_delta(do_ref, o_ref, rows, kvl)
                lse = _stack_cols(lse_ref, rows, kvl)
                sink = jnp.concatenate([jnp.full((LCH, 1), sk_ref[8 * c + 4 * kvl + t], F32) for t in range(4)], axis=0)
                s = jnp.where(msk, _dot_nt(qs, kwin) * SCALE, NEG)
                p = jnp.exp(s - lse)
                ds = p * (_dot_nt(dos, vwin) - delta)
                dqs = _dot(ds.astype(BF16), kwin) * SCALE
                dsink = jnp.exp(sink - lse) * delta
                for t in range(4):
                    hl = 4 * kvl + t
                    dsk[hl] = dsk[hl] - jnp.sum(dsink[LCH * t:LCH * (t + 1)], axis=0, keepdims=True)
                for u in range(2):
                    lanes = slice(LANES * (2 * kvl + u), LANES * (2 * kvl + u + 1))
                    dq2 = jnp.where(m0, dqs[LCH * 2 * u:LCH * (2 * u + 1)], dqs[LCH * (2 * u + 1):LCH * (2 * u + 2)])
                    dq2 = dq2 * cos_ref[rows, :] - _rot_half(dq2) * sin_ref[rows, :]
                    dq_ref[rows, lanes] = dq2.astype(BF16)
        for hl in range(8):
            dsk_ref[0, hl:hl + 1, :] += jnp.broadcast_to(dsk[hl], (1, LANES))

    wide = lambda cc: pl.BlockSpec((bq, 512), lambda c, b, i, cc=cc: (b * nq + i, cc + c))
    cur = lambda cc: pl.BlockSpec((bq, LANES), lambda c, b, i, cc=cc: (b * nq + i, cc + c))
    prev = lambda cc: pl.BlockSpec((LCH, LANES), lambda c, b, i, cc=cc: (b * nrow + jnp.maximum(i * nsub - 1, 0), cc + c))
    pos = pl.BlockSpec((bq, LANES), lambda c, b, i: (i, 0))
    return pl.pallas_call(
        body, name=f"swa_bwd_dq_{li}", grid=(2, B, nq),
        in_specs=[pl.BlockSpec(memory_space=pltpu.SMEM), wide(qc), wide(0), wide(0),
                  pl.BlockSpec((1, 8, bq, 1), lambda c, b, i: (b, c, i, 0)),
                  prev(kc), cur(kc), prev(vc), cur(vc), pos, pos],
        out_specs=[wide(0), pl.BlockSpec((1, 8, LANES), lambda c, b, i: (c, 0, 0))],
        out_shape=[jax.ShapeDtypeStruct((T, D), BF16), jax.ShapeDtypeStruct((2, 8, LANES), F32)],
        compiler_params=_cparams(("arbitrary", "arbitrary", "arbitrary"), VMEM_LIMIT),
    )(sinks, proj, do, o, lse, proj, proj, proj, proj, cos128, sin128)


def _swa_bwd_dkv(proj, do, o, lse, cos128, sin128, S, li):
    T = proj.shape[0]
    B = T // S
    bk, nk, nsub = _swa_blocks(S)
    nrow = S // LCH
    qc, kc, vc = OFF_BQ // 512, OFF_BK // LANES, OFF_BV // LANES

    def body(q_ref, qn_ref, do_ref, don_ref, o_ref, on_ref, lse_ref, lsen_ref, k_ref, v_ref, cos_ref, sin_ref,
             dk_ref, dv_ref):
        j = pl.program_id(2)
        m0 = _lane_iota() < HD
        has_next = (j < nk - 1).astype(F32)
        kf = k_ref[...].astype(F32)
        vf = v_ref[...].astype(F32)
        kd = (_dup_head(kf, 0), _dup_head(kf, 1))
        vd = (_dup_head(vf, 0), _dup_head(vf, 1))
        lane = _lane_iota()

        def stat_rows(lse_r, do_r, o_r, rows, scale):
            a_lse = jnp.zeros((rows, LANES), F32)
            a_del = jnp.zeros((rows, LANES), F32)
            for ch in range(4):
                lanes = slice(LANES * ch, LANES * (ch + 1))
                prod = do_r[:, lanes].astype(F32) * o_r[:, lanes].astype(F32)
                for hh in range(2):
                    h = 2 * ch + hh
                    a_lse = jnp.where(lane == h, lse_r[0, h], a_lse)
                    a_del = jnp.where(lane == h, _head_sum(prod, hh), a_del)
            if scale is not None:
                a_del = a_del * scale
            return a_lse.T, a_del.T

        lse_t, del_t = stat_rows(lse_ref, do_ref, o_ref, bk, None)
        lsen_t, deln_t = stat_rows(lsen_ref, don_ref, on_ref, LCH, has_next)
        r_ = lax.broadcasted_iota(jnp.int32, (LCH, LCH), 0)
        c_ = lax.broadcasted_iota(jnp.int32, (LCH, LCH), 1)
        masks4 = (jnp.concatenate([r_ <= c_] * 4, axis=1), jnp.concatenate([r_ > c_] * 4, axis=1))
        for kr in range(nsub):
            krows = slice(LCH * kr, LCH * (kr + 1))
            dk = jnp.zeros((LCH, LANES), F32)
            dv = jnp.zeros((LCH, LANES), F32)
            for dq_blk in range(2):
                rq = kr + dq_blk
                nxt = rq == nsub
                qrows = slice(0, LCH) if nxt else slice(LCH * rq, LCH * (rq + 1))
                qr, dor = (qn_ref, don_ref) if nxt else (q_ref, do_ref)
                lt, dt_ = (lsen_t, deln_t) if nxt else (lse_t, del_t)
                for kvl in range(2):
                    qs = _stack_heads(qr, qrows, kvl)
                    dos = _stack_heads(dor, qrows, kvl)
                    if nxt:
                        dos = (dos.astype(F32) * has_next).astype(BF16)
                    lse_row = jnp.concatenate([lt[4 * kvl + t:4 * kvl + t + 1, qrows] for t in range(4)], axis=1)
                    del_row = jnp.concatenate([dt_[4 * kvl + t:4 * kvl + t + 1, qrows] for t in range(4)], axis=1)
                    st = jnp.where(masks4[dq_blk], _dot_nt(kd[kvl][krows], qs) * SCALE, NEG)
                    pt = jnp.exp(st - lse_row)
                    dst = pt * (_dot_nt(vd[kvl][krows], dos) - del_row)
                    dvc = _dot(pt.astype(BF16), dos)
                    dkc = _dot(dst.astype(BF16), qs) * SCALE
                    own = m0 if kvl == 0 else jnp.logical_not(m0)
                    dv = dv + jnp.where(own, dvc + pltpu.roll(dvc, HD, 1), 0.0)
                    dk = dk + jnp.where(own, dkc + pltpu.roll(dkc, HD, 1), 0.0)
            dk = dk * cos_ref[krows, :] - _rot_half(dk) * sin_ref[krows, :]
            dk_ref[krows, :] = dk.astype(BF16)
            dv_ref[krows, :] = dv.astype(BF16)

    wide = lambda cc: pl.BlockSpec((bk, 512), lambda c, b, j, cc=cc: (b * nk + j, cc + c))
    nxt = lambda cc: pl.BlockSpec((LCH, 512), lambda c, b, j, cc=cc: (b * nrow + jnp.minimum((j + 1) * nsub, nrow - 1), cc + c))
    cur = lambda cc: pl.BlockSpec((bk, LANES), lambda c, b, j, cc=cc: (b * nk + j, cc + c))
    pos = pl.BlockSpec((bk, LANES), lambda c, b, j: (j, 0))
    return pl.pallas_call(
        body, name=f"swa_bwd_dkv_{li}", grid=(2, B, nk),
        in_specs=[wide(qc), nxt(qc), wide(0), nxt(0), wide(0), nxt(0),
                  pl.BlockSpec((1, 8, bk, 1), lambda c, b, j: (b, c, j, 0)),
                  pl.BlockSpec((1, 8, LCH, 1), lambda c, b, j: (b, c, jnp.minimum((j + 1) * nsub, nrow - 1), 0)),
                  cur(kc), cur(vc), pos, pos],
        out_specs=[cur(0), cur(0)],
        out_shape=[jax.ShapeDtypeStruct((T, 2 * LANES), BF16), jax.ShapeDtypeStruct((T, 2 * LANES), BF16)],
        compiler_params=_cparams(("parallel", "parallel", "parallel"), VMEM_LIMIT),
    )(proj, proj, do, do, o, o, lse, lse, proj, proj, cos128, sin128)


HALO = 16


def _shift_matrices():
    r = lax.broadcasted_iota(jnp.int32, (3 * LCH, LCH + HALO), 0)
    c = lax.broadcasted_iota(jnp.int32, (3 * LCH, LCH + HALO), 1)
    t, d = r % LCH, r // LCH + 1
    return (c == HALO + t - d).astype(BF16), (c == t + d).astype(BF16)


def _ssm_chunk_pre(prev16, cur16, first, sdn_ref, cw_ref, cb_ref, ps, dtb, alog):
    ext16 = jnp.concatenate([jnp.where(first, jnp.zeros_like(prev16), prev16), cur16], axis=0)
    sh = _dot(sdn_ref[...], ext16)
    pre = cb_ref[...] + cw_ref[3:4, :] * cur16.astype(F32)
    for d in range(1, 4):
        pre = pre + cw_ref[3 - d:4 - d, :] * sh[LCH * (d - 1):LCH * d]
    sg = _sigmoid(pre)
    dt = _softplus(ps + dtb)
    a = -jnp.exp(alog)
    r = lax.broadcasted_iota(jnp.int32, (LCH, LCH), 0)
    c = lax.broadcasted_iota(jnp.int32, (LCH, LCH), 1)
    acum = _dot_hi((r >= c).astype(F32), dt * a)
    return pre, sg, dt, a, acum, sh


def _expand_matrix():
    r = lax.broadcasted_iota(jnp.int32, (3 * LANES, D), 0)
    c = lax.broadcasted_iota(jnp.int32, (3 * LANES, D), 1)
    return ((r % LANES) == c // HD).astype(BF16)


def _expand_heads(v, ex_ref):
    return _dot(jnp.concatenate(_split3(v), axis=1).astype(BF16), ex_ref[...])


def _decay(acum, acum_t, h):
    r = lax.broadcasted_iota(jnp.int32, (LCH, LCH), 0)
    c = lax.broadcasted_iota(jnp.int32, (LCH, LCH), 1)
    causal = r >= c
    seg = acum[:, h:h + 1] - acum_t[h:h + 1, :]
    return jnp.where(causal, jnp.exp(jnp.where(causal, seg, 0.0)), 0.0)


def _ssm_pair_fwd(p, x, dt_x, acum, acum_t, e_x, w_x, cd, cb_g, b_g, c_g, hprev, dsk_ref):
    m0 = _lane_iota() < HD
    lanes = slice(LANES * p, LANES * (p + 1))
    x2 = x[:, lanes]
    dt2 = dt_x[:, lanes]
    xdt2 = x2 * dt2
    xdtb = xdt2.astype(BF16)
    lms, ms, yds = [], [], []
    for hh in range(2):
        lm = _decay(acum, acum_t, 2 * p + hh)
        mm = cb_g * lm
        lms.append(lm)
        ms.append(mm)
        yds.append(_dot(mm.astype(BF16), xdtb))
    yd2 = jnp.where(m0, yds[0], yds[1])
    w2 = w_x[:, lanes]
    xw = (xdt2 * w2).astype(BF16)
    s2 = _dot_tn(xw, b_g)
    z2 = _dot_nt(c_g, hprev.astype(BF16))
    e2 = e_x[:, lanes]
    rowsel = lax.broadcasted_iota(jnp.int32, (LANES, 1), 0) < HD
    cdcol = jnp.where(rowsel, cd[:, 2 * p:2 * p + 1], cd[:, 2 * p + 1:2 * p + 2])
    y2 = yd2 + z2 * e2 + dsk_ref[:, lanes] * x2
    return dict(x2=x2, dt2=dt2, xdt2=xdt2, xdtb=xdtb, lms=lms, ms=ms, yd2=yd2, w2=w2, xw=xw, s2=s2, z2=z2, e2=e2,
                cdcol=cdcol, y2=y2)


def _ssm_specs(S, rev):
    nc = S // LCH
    ch = (lambda c: nc - 1 - c) if rev else (lambda c: c)
    prev = pl.BlockSpec((HALO, 2 * D), lambda b, c: (jnp.maximum(b * (S // HALO) + ch(c) * (LCH // HALO) - 1, 0), 0))
    cur = pl.BlockSpec((LCH, 2 * D), lambda b, c: (b * nc + ch(c), 0))
    zed = pl.BlockSpec((LCH, D), lambda b, c: (b * nc + ch(c), OFF_AZ // D))
    row = pl.BlockSpec((LCH, D), lambda b, c: (b * nc + ch(c), 0))
    psb = pl.BlockSpec((LCH, LANES), lambda b, c: (b * nc + ch(c), 0))
    hpb = pl.BlockSpec((1, 1, NH // 2, LANES, NST), lambda b, c: (b, ch(c), 0, 0, 0))
    const = lambda r, w: pl.BlockSpec((r, w), lambda b, c: (0, 0))
    return nc, prev, cur, zed, row, psb, hpb, const


def _ssm_fwd(proj, ps, cw, cb, dtb, alog, dsk, nw, S, li):
    T = proj.shape[0]
    B = T // S
    nc, prev, cur, zed, row, psb, hpb, const = _ssm_specs(S, False)

    def body(prev_ref, cur_ref, z_ref, ps_ref, sdn_ref, ex_ref, cw_ref, cb_ref, dtb_ref, alog_ref, dsk_ref, nw_ref,
             ya_ref, hp_ref, h_scr):
        c = pl.program_id(1)

        @pl.when(c == 0)
        def _():
            h_scr[...] = jnp.zeros_like(h_scr)

        pre, sg, dt, a, acum, _ = _ssm_chunk_pre(prev_ref[...], cur_ref[...], c == 0, sdn_ref, cw_ref, cb_ref,
                                                 ps_ref[...], dtb_ref[...], alog_ref[...])
        act = pre * sg
        acum_t = acum.T
        last = acum[LCH - 1:LCH, :]
        cd = jnp.exp(last)
        dt, e_all, w_all = (_expand_heads(v, ex_ref) for v in (dt, jnp.exp(acum), jnp.exp(last - acum)))
        x = act[:, :D]
        for g in range(NGRP):
            b_g = act[:, D + NST * g:D + NST * (g + 1)].astype(BF16)
            c_g = act[:, D + NGRP * NST + NST * g:D + NGRP * NST + NST * (g + 1)].astype(BF16)
            cb_g = _dot_nt(c_g, b_g)
            ygs = []
            for p in (2 * g, 2 * g + 1):
                hprev = h_scr[p]
                hp_ref[0, 0, p] = hprev
                f = _ssm_pair_fwd(p, x, dt, acum, acum_t, e_all, w_all, cd, cb_g, b_g, c_g, hprev, dsk_ref)
                h_scr[p] = hprev * f["cdcol"] + f["s2"]
                z2 = z_ref[:, LANES * p:LANES * (p + 1)].astype(F32)
                ygs.append(f["y2"] * z2 * _sigmoid(z2))
            yg = jnp.concatenate(ygs, axis=1)
            r = lax.rsqrt(jnp.mean(yg * yg, axis=1, keepdims=True) + EPS)
            ya_ref[:, 2 * LANES * g:2 * LANES * (g + 1)] = (yg * r * nw_ref[:, 2 * LANES * g:2 * LANES * (g + 1)]).astype(BF16)

    return pl.pallas_call(
        body, name=f"ssm_fwd_{li}", grid=(B, nc),
        in_specs=[prev, cur, zed, psb, const(3 * LCH, LCH + HALO), const(3 * LANES, D), const(4, 2 * D),
                  const(1, 2 * D), const(1, LANES), const(1, LANES), const(1, D), const(1, D)],
        out_specs=[row, hpb],
        out_shape=[jax.ShapeDtypeStruct((T, D), BF16), jax.ShapeDtypeStruct((B, nc, NH // 2, LANES, NST), F32)],
        scratch_shapes=[pltpu.VMEM((NH // 2, LANES, NST), F32)],
        compiler_params=_cparams(("arbitrary", "arbitrary"), VMEM_LIMIT),
    )(proj, proj, proj, ps, _shift_matrices()[0], _expand_matrix(), cw, cb, dtb, alog, dsk, nw)


def _ssm_bwd(proj, ps, hp, dya, cw, cb, dtb, alog, dsk, nw, S, li, comm=None):
    T = proj.shape[0]
    B = T // S
    nc, prev, cur, zed, row, psb, hpb, const = _ssm_specs(S, True)

    def body(prev_ref, cur_ref, z_ref, ps_ref, hp_ref, dy_ref, sdn_ref, sup_ref, ex_ref, cw_ref, cb_ref, dtb_ref,
             alog_ref, dsk_ref, nw_ref, dxbc_ref, dz_ref, dps_ref, pgw_ref, pg1_ref, pgh_ref, dh_scr, dhead, dact):
        b, cc = pl.program_id(0), pl.program_id(1)
        c = nc - 1 - cc

        @pl.when(jnp.logical_and(b == 0, cc == 0))
        def _():
            pgw_ref[...] = jnp.zeros_like(pgw_ref)
            pg1_ref[...] = jnp.zeros_like(pg1_ref)
            pgh_ref[...] = jnp.zeros_like(pgh_ref)

        @pl.when(cc == 0)
        def _():
            dh_scr[...] = jnp.zeros_like(dh_scr)
            dhead[...] = jnp.zeros_like(dhead)

        psv = ps_ref[...]
        cur16 = cur_ref[...]
        pre, sg, dt, a, acum, sh = _ssm_chunk_pre(prev_ref[...], cur16, c == 0, sdn_ref, cw_ref, cb_ref, psv,
                                                  dtb_ref[...], alog_ref[...])
        act = pre * sg
        acum_t = acum.T
        last = acum[LCH - 1:LCH, :]
        w_all = jnp.exp(last - acum)
        cd = jnp.exp(last)
        dt_x, e_x, w_x = (_expand_heads(v, ex_ref) for v in (dt, jnp.exp(acum), w_all))
        x = act[:, :D]
        lane = _lane_iota()
        m0 = lane < HD
        head_row = lax.broadcasted_iota(jnp.int32, (LANES, 1), 0)
        rowsel = head_row < HD
        is_last_row = lax.broadcasted_iota(jnp.int32, (LCH, 1), 0) == LCH - 1
        dacum_all = jnp.zeros((LCH, LANES), F32)
        dacum_t = jnp.zeros((LANES, LCH), F32)
        ddt_all = jnp.zeros((LCH, LANES), F32)
        dd_row = jnp.zeros((1, LANES), F32)
        for g in range(NGRP):
            b_g = act[:, D + NST * g:D + NST * (g + 1)].astype(BF16)
            c_g = act[:, D + NGRP * NST + NST * g:D + NGRP * NST + NST * (g + 1)].astype(BF16)
            cb_g = _dot_nt(c_g, b_g)
            pairs = (2 * g, 2 * g + 1)
            fs, hps, zs, ygs = [], [], [], []
            for p in pairs:
                hprev = hp_ref[0, 0, p]
                f = _ssm_pair_fwd(p, x, dt_x, acum, acum_t, e_x, w_x, cd, cb_g, b_g, c_g, hprev, dsk_ref)
                z2 = z_ref[:, LANES * p:LANES * (p + 1)].astype(F32)
                fs.append(f)
                hps.append(hprev)
                zs.append(z2)
                ygs.append(f["y2"] * z2 * _sigmoid(z2))
            gl = slice(2 * LANES * g, 2 * LANES * (g + 1))
            yg = jnp.concatenate(ygs, axis=1)
            r = lax.rsqrt(jnp.mean(yg * yg, axis=1, keepdims=True) + EPS)
            dyn = dy_ref[:, gl].astype(F32)
            gg = dyn * nw_ref[:, gl]
            dyg = r * gg - yg * (r * r * r) * jnp.mean(gg * yg, axis=1, keepdims=True)
            pg1_ref[0:1, gl] += jnp.sum(dyn * yg * r, axis=0, keepdims=True)
            dg_g = jnp.zeros((LCH, LCH), F32)
            db_g = jnp.zeros((LCH, NST), F32)
            dc_g = jnp.zeros((LCH, NST), F32)
            for idx, p in enumerate(pairs):
                f, hprev, z2 = fs[idx], hps[idx], zs[idx]
                lanes = slice(LANES * p, LANES * (p + 1))
                dyg2 = dyg[:, LANES * idx:LANES * (idx + 1)]
                sgz = _sigmoid(z2)
                dy2 = dyg2 * z2 * sgz
                dz_ref[:, lanes] = (dyg2 * f["y2"] * sgz * (1.0 + z2 * (1.0 - sgz))).astype(BF16)
                x2, dt2, xdt2, xdtb, w2, e2, z2m = f["x2"], f["dt2"], f["xdt2"], f["xdtb"], f["w2"], f["e2"], f["z2"]
                dx2 = dsk_ref[:, lanes] * dy2
                dyx = dy2 * x2
                dxdt2 = jnp.zeros((LCH, LANES), F32)
                diag_cols = []
                for hh in range(2):
                    sel = m0 if hh == 0 else jnp.logical_not(m0)
                    dyb = jnp.where(sel, dy2, 0.0).astype(BF16)
                    dm = _dot_nt(dyb, xdtb)
                    dg_g = dg_g + dm * f["lms"][hh]
                    dxdt2 = dxdt2 + _dot_tn(f["ms"][hh].astype(BF16), dyb)
                    em = dm * f["ms"][hh]
                    diag_cols.append(jnp.sum(em, axis=1, keepdims=True))
                    dacum_t = dacum_t - jnp.where(head_row == 2 * p + hh, jnp.sum(em, axis=0, keepdims=True), 0.0)
                dz2m = dy2 * e2
                t_off = dz2m * z2m
                dc_g = dc_g + _dot(dz2m.astype(BF16), hprev.astype(BF16))
                dhprev = _dot_tn(dz2m.astype(BF16), c_g)
                dhn = dh_scr[p]
                dhnb = dhn.astype(BF16)
                dhprev = dhprev + dhn * f["cdcol"]
                t_h = dhn * hprev
                dxw2 = _dot_nt(b_g, dhnb)
                db_g = db_g + _dot(f["xw"], dhnb)
                dxdt2 = dxdt2 + dxw2 * w2
                t_w = dxw2 * xdt2
                dx2 = dx2 + dxdt2 * dt2
                t_dt = dxdt2 * x2
                for hh in range(2):
                    h = 2 * p + hh
                    onehot = (lane == h).astype(F32)
                    w_col = w_all[:, h:h + 1]
                    dw_col = _head_sum(t_w, hh) * w_col
                    rs = rowsel if hh == 0 else jnp.logical_not(rowsel)
                    dlast = (jnp.sum(jnp.where(rs, t_h, 0.0), keepdims=True) * cd[:, h:h + 1]
                             + jnp.sum(dw_col, keepdims=True))
                    dacum_col = diag_cols[hh] + _head_sum(t_off, hh) - dw_col + jnp.where(is_last_row, dlast, 0.0)
                    dacum_all = dacum_all + dacum_col * onehot
                    ddt_all = ddt_all + _head_sum(t_dt, hh) * onehot
                    sel = m0 if hh == 0 else jnp.logical_not(m0)
                    dd_row = dd_row + jnp.sum(jnp.where(sel, dyx, 0.0), keepdims=True) * onehot
                dh_scr[p] = dhprev
                dact[:, lanes] = dx2
            dgb = dg_g.astype(BF16)
            dc_g = dc_g + _dot(dgb, b_g)
            db_g = db_g + _dot_tn(dgb, c_g)
            dact[:, D + NST * g:D + NST * (g + 1)] = db_g
            dact[:, D + NGRP * NST + NST * g:D + NGRP * NST + NST * (g + 1)] = dc_g
        rr = lax.broadcasted_iota(jnp.int32, (LCH, LCH), 0)
        cc2 = lax.broadcasted_iota(jnp.int32, (LCH, LCH), 1)
        dadt = _dot_hi((cc2 >= rr).astype(F32), dacum_all + dacum_t.T)
        ddt_all = ddt_all + dadt * a
        heads = lane < NH
        da = jnp.sum(dadt * dt, axis=0, keepdims=True)
        dr = jnp.where(heads, ddt_all * _sigmoid(psv + dtb_ref[...]), 0.0)
        dps_ref[...] = dr
        pgh_ref[0:1, :] += jnp.sum(dr, axis=0, keepdims=True)
        pgh_ref[1:2, :] += jnp.where(heads, da * a, 0.0)
        pgh_ref[2:3, :] += dd_row
        dpre = dact[...] * sg * (1.0 + pre * (1.0 - sg))
        extd = jnp.concatenate([dpre, dhead[...]], axis=0)
        hi = extd.astype(BF16)
        lo = (extd - hi.astype(F32)).astype(BF16)
        up = _dot(sup_ref[...], hi) + _dot(sup_ref[...], lo)
        du = cw_ref[3:4, :] * dpre
        pgw_ref[3:4, :] += jnp.sum(dpre * cur16.astype(F32), axis=0, keepdims=True)
        for d in range(1, 4):
            du = du + cw_ref[3 - d:4 - d, :] * up[LCH * (d - 1):LCH * d]
            pgw_ref[3 - d:4 - d, :] += jnp.sum(dpre * sh[LCH * (d - 1):LCH * d], axis=0, keepdims=True)
        pgw_ref[4:5, :] += jnp.sum(dpre, axis=0, keepdims=True)
        dxbc_ref[...] = du.astype(BF16)
        dhead[...] = dpre[0:HALO, :]

    xbc_out = pl.BlockSpec((LCH, 2 * D), lambda b, c: (b * nc + nc - 1 - c, 0))
    acc = lambda w: pl.BlockSpec((8, w), lambda b, c: (0, 0))
    sdn, sup = _shift_matrices()
    return _hosted_call(
        body, comm, f"ssm_bwd_{li}", (B, nc),
        in_specs=[prev, cur, zed, psb, hpb, row, const(3 * LCH, LCH + HALO), const(3 * LCH, LCH + HALO),
                  const(3 * LANES, D), const(4, 2 * D), const(1, 2 * D), const(1, LANES), const(1, LANES),
                  const(1, D), const(1, D)],
        out_specs=[xbc_out, row, psb, acc(2 * D), acc(D), acc(LANES)],
        out_shape=[jax.ShapeDtypeStruct((T, 2 * D), BF16), jax.ShapeDtypeStruct((T, D), BF16),
                   jax.ShapeDtypeStruct((T, LANES), F32), jax.ShapeDtypeStruct((8, 2 * D), F32),
                   jax.ShapeDtypeStruct((8, D), F32), jax.ShapeDtypeStruct((8, LANES), F32)],
        scratch=[pltpu.VMEM((NH // 2, LANES, NST), F32), pltpu.VMEM((HALO, 2 * D), F32),
                 pltpu.VMEM((LCH, 2 * D), F32)],
        dims=("arbitrary", "arbitrary"),
        operands=(proj, proj, proj, ps, hp, dya, sdn, sup, _expand_matrix(), cw, cb, dtb, alog, dsk, nw))


def _lane_row(v, offset):
    return jnp.pad(v.astype(F32), (offset, LANES - offset - v.shape[0]))[None]


def _pack_rows(arrays):
    parts = []
    for a in arrays:
        flat = a.reshape(-1).astype(F32)
        pad = (-flat.shape[0]) % LANES
        parts.append(jnp.pad(flat, (0, pad)))
    flat = jnp.concatenate(parts)
    pad = (-flat.shape[0]) % (8 * LANES)
    return jnp.pad(flat, (0, pad)).reshape(-1, LANES)


def _unpack_rows(pack, shapes):
    flat = pack.reshape(-1)
    out, pos = [], 0
    for shp in shapes:
        n = math.prod(shp)
        out.append(flat[pos:pos + n].reshape(shp))
        pos += n + (-n) % LANES
    return out


def _split_w_in(blocks):
    def cols(a, b):
        out = []
        for d in range(NDEV):
            lo, hi = max(a, d * NSH), min(b, (d + 1) * NSH)
            if lo < hi:
                out.append(blocks[d, :, lo - d * NSH:hi - d * NSH])
        return out

    main = jnp.concatenate(cols(0, 3072) + cols(3088, 4112) + cols(4624, 5648) + cols(5648, 8720)
                           + cols(8736, 12832) + cols(4112, 4624), axis=1)
    small = jnp.concatenate(cols(3072, 3088) + cols(8720, 8736) + [jnp.zeros((D, LANES - 2 * NH), blocks.dtype)],
                            axis=1)
    return main, small


def _w_in_blocks(dw, ds, r0, r1):
    xbc, az, bq, bz, cq, ck, cv, cz, gates, bk, bv = dw
    order = [xbc, az, ds[:, 0:NH], bq, bk, bv, bz, cq, ck, cv, ds[:, NH:2 * NH], cz, gates]
    blocks, pos = [[] for _ in range(NDEV)], 0
    for seg in order:
        w = seg.shape[1]
        for d in range(NDEV):
            lo, hi = max(pos, d * NSH), min(pos + w, (d + 1) * NSH)
            if lo < hi:
                blocks[d].append(seg[r0:r1, lo - pos:hi - pos])
        pos += w
    return jnp.stack([jnp.concatenate(b, axis=1) for b in blocks])


def kernel(x, norm_w, w_in, conv_w, conv_b, dt_bias, a_log, d_skip, ssm_norm_w, sinks, f_bias, gate_bias, w_proj, w_out, final_norm_w, loss_target, m_norm_w, m_w_in, m_conv_w, m_conv_b, m_dt_bias, m_a_log, m_d_skip, m_ssm_norm_w, m_sinks, m_f_bias, m_gate_bias, m_w_proj, m_w_out, m_final_norm_w, v_norm_w, v_w_in, v_conv_w, v_conv_b, v_dt_bias, v_a_log, v_d_skip, v_ssm_norm_w, v_sinks, v_f_bias, v_gate_bias, v_w_proj, v_w_out, v_final_norm_w):
    Bl, S, _ = x.shape
    T = Bl * S
    depth = norm_w.shape[0]
    me = 4 * lax.axis_index("x") + 2 * lax.axis_index("y") + lax.axis_index("c")
    csh, gsh = conv_w.shape[2], gate_bias.shape[2]

    def gather_plan(l):
        small = jnp.concatenate([conv_w[l].reshape(-1), gate_bias[l].reshape(-1)]).reshape(-1, LANES)
        return _Comm("gather", [w_in[l].astype(BF16), w_proj[l].astype(BF16), w_out[l].astype(BF16), small])

    def unpack_weights(res):
        g_win, g_wp, g_wo, g_small = res
        flat = g_small.reshape(NDEV, -1)
        return (_split_w_in(g_win),
                g_wp.transpose(1, 0, 2, 3).reshape(3, D, D),
                g_wo.reshape(D, D),
                flat[:, :4 * csh].reshape(NDEV, 4, csh).transpose(1, 0, 2).reshape(4, 2 * D),
                flat[:, 4 * csh:].reshape(NDEV, 3, gsh).transpose(1, 0, 2).reshape(3, D))

    def scatter_plan(gw_in_blocks=None, gw_p=None, gw_o=None):
        arrays = [] if gw_in_blocks is None else [gw_in_blocks]
        if gw_p is not None:
            arrays += [gw_p.astype(BF16).reshape(3, NDEV, D // NDEV, D).transpose(1, 0, 2, 3),
                       gw_o.astype(BF16).reshape(NDEV, D // NDEV, D)]
        return _Comm("scatter", arrays)

    pos = jnp.arange(S, dtype=F32)
    inv_freq = ROPE_THETA ** (-jnp.arange(0, HD, 2, dtype=F32) / HD)
    ang = pos[:, None] * inv_freq[None, :]
    cos128 = jnp.tile(jnp.cos(ang), (1, 4))
    sign = jnp.where((jnp.arange(LANES) % HD) < HD // 2, -1.0, 1.0).astype(F32)
    sin128 = jnp.tile(jnp.sin(ang), (1, 4)) * sign[None, :]

    x2 = x.reshape(T, D)
    tgt2 = loss_target.reshape(T, D)
    fox_bq = _fox_blocks(S)[0]

    saved = []
    xcur = x2
    weights = [None] * depth
    weights[0] = unpack_weights(_gather_two_level(gather_plan(0).arrays, "gather_weights_0"))
    for l in range(depth):
        (wmain, wsmall), wp_l, wo_l, cw_l, gb_l = weights[l]
        proj, ps, h_t = _inproj_fwd(xcur, norm_w[l][None], wmain, wsmall, cos128, sin128, S, l)
        dtb = _lane_row(dt_bias[l], 0)
        alog = _lane_row(a_log[l], 0)
        fb = _lane_row(f_bias[l], NH)
        dsk = jnp.repeat(d_skip[l], HD)[None]
        ya, hp = _ssm_fwd(proj, ps, cw_l, conv_b[l][None], dtb, alog, dsk, ssm_norm_w[l][None], S, l)
        yb, ob, lse_b = _swa_fwd(proj, sinks[l], S, l)
        cum = _fox_cum(ps, fb, S, l)
        cumh = cum[:, NH:2 * NH].reshape(Bl, S, NH).transpose(0, 2, 1)
        cum_row = cumh.reshape(Bl, NH, S // fox_bq, 1, fox_bq)
        comm = gather_plan(l + 1) if l + 1 < depth else None
        res = _fox_fwd(proj, cum_row, S, l, comm)
        yc, oc, lse_c = res[:3]
        if comm is not None:
            weights[l + 1] = unpack_weights(res[3:])
        xnext, br, y_t = _merge_fwd(ya, yb, yc, proj, gb_l, wp_l, wo_l, xcur, l)
        saved.append(dict(x=xcur, wmain=wmain, wsmall=wsmall, proj=proj, ps=ps, h_t=h_t, dtb=dtb, alog=alog, fb=fb,
                          dsk=dsk, hp=hp, ob=ob, lse_b=lse_b, cum_row=cum_row, oc=oc, lse_c=lse_c, br=br, y_t=y_t))
        xcur = xnext

    dx, dx16, st = _final_loss(xcur, tgt2, final_norm_w[None])
    loss_part = st[2, 0]
    g_final = st[0]

    gsm = {k: [None] * depth for k in ("norm_w", "conv_w", "conv_b", "dt_bias", "a_log", "d_skip", "ssm_norm_w",
                                      "sinks", "f_bias", "gate_bias")}
    parts = [None] * depth
    pending = None
    for l in reversed(range(depth)):
        sv = saved[l]
        proj, ps = sv["proj"], sv["ps"]
        _, wp_l, wo_l, cw_l, gb_l = weights[l]
        dbr, dgates, merged_t, dgb, dy_a, do_b, dbz, do_c, dcz = _merge_bwd(dx16, wo_l, wp_l, sv["br"], proj, gb_l,
                                                                            sv["ob"], sv["oc"], l)
        g_wo = _matmul(merged_t, dx16, BF16, f"dwout_{l}")
        g_wp = _matmul_batched(sv["y_t"], dbr, BF16, f"dwproj_{l}")
        gsm["gate_bias"][l] = dgb[0:3]
        hosted = ([] if pending is None else pending.arrays) + (scatter_plan(None, g_wp, g_wo).arrays if l == 0 else [])
        res = _ssm_bwd(proj, ps, sv["hp"], dy_a, cw_l, conv_b[l][None], sv["dtb"], sv["alog"], sv["dsk"],
                       ssm_norm_w[l][None], S, l, _Comm("scatter", hosted) if hosted else None)
        dxbc, daz, dps_a, pgw, pg1, pgh = res[:6]
        if pending is not None:
            parts[l + 1] = res[6:9]
        if l == 0:
            parts_po = res[len(res) - 2:]
        gsm["conv_w"][l], gsm["conv_b"][l] = pgw[0:4], pgw[4]
        gsm["ssm_norm_w"][l] = pg1[0]
        gsm["dt_bias"][l], gsm["a_log"][l], gsm["d_skip"][l] = pgh[0, :NH], pgh[1, :NH], pgh[2, :NH]
        dq_b, dsk_b = _swa_bwd_dq(proj, do_b, sv["ob"], sv["lse_b"], sinks[l], cos128, sin128, S, l)
        dk_b, dv_b = _swa_bwd_dkv(proj, do_b, sv["ob"], sv["lse_b"], cos128, sin128, S, l)
        gsm["sinks"][l] = dsk_b[:, :, 0].reshape(NH)
        dq_c, dk_c, dv_c, dcum_k, dcum_q = _fox_bwd(proj, do_c, sv["oc"], sv["cum_row"], sv["lse_c"], S, l)
        dcum_tm = (dcum_k.reshape(Bl, NH, S) + dcum_q.reshape(Bl, NH, S)).transpose(0, 2, 1).reshape(T, NH)
        dcum_pad = jnp.pad(dcum_tm, ((0, 0), (NH, LANES - 2 * NH)))
        df, dfb = _fox_cum_bwd(dcum_pad, ps, sv["fb"], S, l)
        gsm["f_bias"][l] = dfb[0, NH:2 * NH]
        dps16 = (dps_a + df).astype(BF16)
        dkv_b = jnp.concatenate([dk_b, dv_b], axis=1)
        pieces = (dxbc, daz, dq_b, dbz, dq_c, dk_c, dv_c, dcz, dgates, dkv_b)
        dw_pieces = [_matmul(sv["h_t"], pc, BF16, f"dwin_{l}_{i}", tk=2048) for i, pc in enumerate(pieces)]
        dw_pieces = dw_pieces[:-1] + [dw_pieces[-1][:, :2 * LANES], dw_pieces[-1][:, 2 * LANES:]]
        dws = _matmul(sv["h_t"], dps16, BF16, f"dwin_small_{l}")
        if l == 0:
            plans = [scatter_plan(_w_in_blocks(dw_pieces, dws, r0, r1)) for r0, r1 in ROW_CHUNKS]
        else:
            plans, pending = [None] * len(ROW_CHUNKS), scatter_plan(_w_in_blocks(dw_pieces, dws, 0, D), g_wp, g_wo)
        res1 = _inproj_bwd_dx([(dxbc, OFF_XBC), (daz, OFF_AZ), (dq_b, OFF_BQ), (dbz, OFF_BZ)], sv["wmain"],
                              ("narrow", dps16, sv["wsmall"]), None, f"inproj_bwd_dh1_{l}", plans[0])
        res2 = _inproj_bwd_dx([(dq_c, OFF_CQ), (dk_c, OFF_CK), (dv_c, OFF_CV), (dcz, OFF_CZ)], sv["wmain"],
                              ("acc", res1[0]), None, f"inproj_bwd_dh2_{l}", plans[1])
        res3 = _inproj_bwd_dx([(dgates, OFF_G), (dkv_b, OFF_BK)], sv["wmain"], ("acc", res2[0]),
                              (sv["x"], norm_w[l][None], dx), f"inproj_bwd_dx_{l}", plans[2])
        dx, dx16, dnw = res3[:3]
        if l == 0:
            parts[0] = [jnp.concatenate([res1[1], res2[1], res3[3]], axis=1), *parts_po]
        gsm["norm_w"][l] = dnw[0]

    big = {}
    for idx, (name, w, m, v) in enumerate((("w_in", w_in, m_w_in, v_w_in), ("w_proj", w_proj, m_w_proj, v_w_proj),
                                          ("w_out", w_out, m_w_out, v_w_out))):
        cols = w.shape[-1]
        res = _sum_adamw([parts[l][idx].reshape(NDEV, -1, cols) for l in range(depth)], w.reshape(depth, -1, cols),
                         m.reshape(depth, -1, cols), v.reshape(depth, -1, cols), f"adamw_{name}")
        big[name] = [r.reshape(w.shape) for r in res]

    small_names = ("norm_w", "conv_b", "dt_bias", "a_log", "d_skip", "ssm_norm_w", "sinks", "f_bias")
    small_parts = [jnp.stack(gsm[k]) for k in small_names] + [g_final, jnp.stack(gsm["conv_w"]),
                                                              jnp.stack(gsm["gate_bias"]), loss_part.reshape(1)]
    shapes = [a.shape for a in small_parts]
    summed = _unpack_rows(_all_reduce_small(_pack_rows(small_parts)), shapes)
    g_small = dict(zip(small_names, summed[:len(small_names)]))
    g_small["final_norm_w"] = summed[len(small_names)]
    g_small["conv_w"] = lax.dynamic_slice_in_dim(summed[len(small_names) + 1], me * csh, csh, axis=2)
    g_small["gate_bias"] = lax.dynamic_slice_in_dim(summed[len(small_names) + 2], me * gsh, gsh, axis=2)
    loss = summed[len(small_names) + 3][0]

    ws = dict(norm_w=norm_w, conv_w=conv_w, conv_b=conv_b, dt_bias=dt_bias, a_log=a_log, d_skip=d_skip,
              ssm_norm_w=ssm_norm_w, sinks=sinks, f_bias=f_bias, gate_bias=gate_bias, final_norm_w=final_norm_w)
    ms = dict(norm_w=m_norm_w, conv_w=m_conv_w, conv_b=m_conv_b, dt_bias=m_dt_bias, a_log=m_a_log, d_skip=m_d_skip,
              ssm_norm_w=m_ssm_norm_w, sinks=m_sinks, f_bias=m_f_bias, gate_bias=m_gate_bias,
              final_norm_w=m_final_norm_w)
    vs = dict(norm_w=v_norm_w, conv_w=v_conv_w, conv_b=v_conv_b, dt_bias=v_dt_bias, a_log=v_a_log, d_skip=v_d_skip,
              ssm_norm_w=v_ssm_norm_w, sinks=v_sinks, f_bias=v_f_bias, gate_bias=v_gate_bias,
              final_norm_w=v_final_norm_w)
    order = list(ws)
    oshapes = [ws[k].shape for k in order]
    res = _adamw_small(_pack_rows([g_small[k] for k in order]), _pack_rows([ws[k] for k in order]),
                       _pack_rows([ms[k] for k in order]), _pack_rows([vs[k] for k in order]))
    d_s, m_s, v_s = (dict(zip(order, _unpack_rows(r, oshapes))) for r in res)

    names = ("norm_w", "w_in", "conv_w", "conv_b", "dt_bias", "a_log", "d_skip", "ssm_norm_w", "sinks", "f_bias",
             "gate_bias", "w_proj", "w_out", "final_norm_w")
    grads, deltas, new_m, new_v = [], [], [], []
    for k in names:
        if k in big:
            g, d_, m_, v_ = big[k]
        else:
            g, d_, m_, v_ = g_small[k], d_s[k], m_s[k], v_s[k]
        grads.append(g)
        deltas.append(d_)
        new_m.append(m_)
        new_v.append(v_)
    return (loss, dx.reshape(Bl, S, D), *grads, *deltas, *new_m, *new_v)
```
